```python
import jax, jax.numpy as jnp
from jax import lax
import numpy as np

D_MODEL = 1024
BATCH = 16
SEQ = 2048
DEPTH = 1

CONV_CH = 512
CONV_WIDTH = 31
NSA_HEADS = 8
NSA_KV_GROUPS = 2
HEAD_DIM = 64
NSA_Q_PER_G = NSA_HEADS // NSA_KV_GROUPS
NSA_WIDTH = NSA_HEADS * HEAD_DIM
MIX_WIDTH = CONV_CH + NSA_WIDTH
KV_W = NSA_KV_GROUPS * HEAD_DIM
IN_COLS = 2 * CONV_CH + NSA_WIDTH + 6 * KV_W + 3 * NSA_HEADS
ROT_DIM = HEAD_DIM // 4
ROT_HALF = ROT_DIM // 2
ROPE_THETA = 500000.0
CMP_LEN = 32
CMP_STRIDE = 16
CMP_HIDDEN = 128
SEL_BLK = 64
SEL_TOPN = 16
SEL_QCHUNK = 64
WINDOW = 512
WIN_QBLK = 128
N_EXPERTS = 32
TOP_K = 4
D_FF = 1024
SWIGLU_ALPHA = 1.702
SWIGLU_LIMIT = 7.0
MOE_BLOCK = 128
NORM_EPS = 1e-5
NEG_INF = -1e30
FORCE_SCORE = 1e9

kernel_name = "hybrid_conv_nsa_moe_adaln"


def rmsnorm(x, g):
    xf = x.astype(jnp.float32)
    y = xf * lax.rsqrt(jnp.mean(xf * xf, axis=-1, keepdims=True) + NORM_EPS)
    return (y * g.astype(jnp.float32)).astype(x.dtype)


def layernorm(x, g, b):
    xf = x.astype(jnp.float32)
    mu = jnp.mean(xf, axis=-1, keepdims=True)
    var = jnp.mean(jnp.square(xf - mu), axis=-1, keepdims=True)
    y = (xf - mu) * lax.rsqrt(var + NORM_EPS)
    return (y * g.astype(jnp.float32) + b.astype(jnp.float32)).astype(x.dtype)


def rope_tables(S):
    inv = ROPE_THETA ** (-jnp.arange(0, ROT_DIM, 2, dtype=jnp.float32) / ROT_DIM)
    ang = jnp.arange(S, dtype=jnp.float32)[:, None] * inv[None, :]
    return jnp.cos(ang)[:, None, :], jnp.sin(ang)[:, None, :]


def partial_rope(x, cos, sin):
    xr = x[..., :ROT_DIM].astype(jnp.float32)
    x1, x2 = xr[..., :ROT_HALF], xr[..., ROT_HALF:]
    rot = jnp.concatenate([x1 * cos - x2 * sin, x2 * cos + x1 * sin], axis=-1)
    return jnp.concatenate([rot.astype(x.dtype), x[..., ROT_DIM:]], axis=-1)


def conv_module(a, gate, conv_w, conv_b, ln_g, ln_b):
    u = a * jax.nn.sigmoid(gate)
    y = lax.conv_general_dilated(
        u, conv_w.reshape(CONV_WIDTH, 1, CONV_CH).astype(u.dtype),
        window_strides=(1,), padding=[(CONV_WIDTH - 1, 0)],
        dimension_numbers=('NWC', 'WIO', 'NWC'), feature_group_count=CONV_CH)
    y = y + conv_b
    return jax.nn.silu(layernorm(y, ln_g, ln_b))


def compress_blocks(x, cidx, pe, w1, w2):
    B = x.shape[0]
    blk = x[:, cidx] + pe[:, None, :]
    blk = blk.transpose(0, 1, 3, 2, 4).reshape(B, cidx.shape[0], NSA_KV_GROUPS, CMP_LEN * HEAD_DIM)
    return jax.nn.silu(blk @ w1) @ w2


def nsa_mixer(q, k_cmp, v_cmp, k_slc, v_slc, k_win, v_win, gates, pe_k, pe_v, ck_w1, ck_w2, cv_w1, cv_w2):
    B, S = q.shape[0], q.shape[1]
    scale = HEAD_DIM ** -0.5
    t = jnp.arange(S)

    n_cmp = (S - CMP_LEN) // CMP_STRIDE + 1
    cstart = jnp.arange(n_cmp) * CMP_STRIDE
    cidx = cstart[:, None] + jnp.arange(CMP_LEN)[None, :]
    kc = compress_blocks(k_cmp, cidx, pe_k, ck_w1, ck_w2)
    vc = compress_blocks(v_cmp, cidx, pe_v, cv_w1, cv_w2)
    s_c = jnp.einsum('bsgnd,bcgd->bgnsc', q, kc).astype(jnp.float32) * scale
    m_c = (cstart + CMP_LEN - 1)[None, :] <= t[:, None]
    p_c = jax.nn.softmax(jnp.where(m_c, s_c, NEG_INF), axis=-1) * m_c
    o_cmp = jnp.einsum('bgnsc,bcgd->bsgnd', p_c.astype(vc.dtype), vc)

    n_sel = S // SEL_BLK
    jstart = jnp.arange(n_sel) * SEL_BLK
    overlap = ((cstart[:, None] <= jstart[None, :] + SEL_BLK - 1)
               & (cstart[:, None] + CMP_LEN - 1 >= jstart[None, :])).astype(jnp.float32)
    imp = jnp.einsum('bgnsc,cj->bgsj', p_c, overlap)
    jb = jnp.arange(n_sel)[None, :]
    cur = (t // SEL_BLK)[:, None]
    valid = jstart[None, :] <= t[:, None]
    forced = (jb == 0) | (jb == cur) | (jb == cur - 1)
    score = jnp.where(valid, jnp.where(forced, FORCE_SCORE, imp), NEG_INF)
    n_top = min(SEL_TOPN, n_sel)
    _, sel_idx = lax.top_k(score, n_top)

    kblk = k_slc.reshape(B, n_sel, SEL_BLK, NSA_KV_GROUPS, HEAD_DIM).transpose(0, 3, 1, 2, 4)
    vblk = v_slc.reshape(B, n_sel, SEL_BLK, NSA_KV_GROUPS, HEAD_DIM).transpose(0, 3, 1, 2, 4)
    n_ch = S // SEL_QCHUNK
    q_ch = q.reshape(B, n_ch, SEL_QCHUNK, NSA_KV_GROUPS, NSA_Q_PER_G, HEAD_DIM).transpose(1, 0, 2, 3, 4, 5)
    idx_ch = sel_idx.reshape(B, NSA_KV_GROUPS, n_ch, SEL_QCHUNK, n_top).transpose(2, 0, 1, 3, 4)
    t_ch = t.reshape(n_ch, SEL_QCHUNK)
    bi = jnp.arange(B)[:, None, None, None]
    gi = jnp.arange(NSA_KV_GROUPS)[None, :, None, None]
    in_blk = jnp.arange(SEL_BLK)

    def sel_chunk(args):
        qc, ic, tc = args
        kg = kblk[bi, gi, ic]
        vg = vblk[bi, gi, ic]
        s = jnp.einsum('bqgnd,bgqkld->bgnqkl', qc, kg).astype(jnp.float32) * scale
        kpos = ic[..., None] * SEL_BLK + in_blk
        mask = kpos <= tc[None, None, :, None, None]
        s = jnp.where(mask[:, :, None], s, NEG_INF)
        sh = s.shape
        p = jax.nn.softmax(s.reshape(sh[0], sh[1], sh[2], sh[3], -1), axis=-1).reshape(sh)
        return jnp.einsum('bgnqkl,bgqkld->bqgnd', p.astype(vg.dtype), vg)

    o_slc = lax.map(sel_chunk, (q_ch, idx_ch, t_ch))
    o_slc = o_slc.transpose(1, 0, 2, 3, 4, 5).reshape(B, S, NSA_KV_GROUPS, NSA_Q_PER_G, HEAD_DIM)

    n_qb = S // WIN_QBLK
    kp = jnp.pad(k_win, ((0, 0), (WINDOW, 0), (0, 0), (0, 0)))
    vp = jnp.pad(v_win, ((0, 0), (WINDOW, 0), (0, 0), (0, 0)))
    widx = jnp.arange(n_qb)[:, None] * WIN_QBLK + jnp.arange(WINDOW + WIN_QBLK)[None, :]
    kw = kp[:, widx]
    vw = vp[:, widx]
    qw = q.reshape(B, n_qb, WIN_QBLK, NSA_KV_GROUPS, NSA_Q_PER_G, HEAD_DIM)
    s_w = jnp.einsum('biqgnd,bikgd->bignqk', qw, kw).astype(jnp.float32) * scale
    tq = jnp.arange(n_qb)[:, None] * WIN_QBLK + jnp.arange(WIN_QBLK)[None, :]
    pk = widx - WINDOW
    rel = tq[:, :, None] - pk[:, None, :]
    m_w = (rel >= 0) & (rel < WINDOW) & (pk[:, None, :] >= 0)
    p_w = jax.nn.softmax(jnp.where(m_w[None, :, None, None], s_w, NEG_INF), axis=-1)
    o_win = jnp.einsum('bignqk,bikgd->biqgnd', p_w.astype(vw.dtype), vw)
    o_win = o_win.reshape(B, S, NSA_KV_GROUPS, NSA_Q_PER_G, HEAD_DIM)

    o = gates[..., 0:1] * o_cmp + gates[..., 1:2] * o_slc + gates[..., 2:3] * o_win
    return o.reshape(B, S, NSA_WIDTH)


def moe(h, w_router, b_router, w_gate, b_gate, w_up, b_up, w_down, b_down):
    B, S, D = h.shape
    T = B * S
    xf = h.reshape(T, D)
    logits = xf.astype(jnp.float32) @ w_router.astype(jnp.float32) + b_router.astype(jnp.float32)
    top_val, top_idx = lax.top_k(logits, TOP_K)
    gates = jax.nn.softmax(top_val, axis=-1).astype(h.dtype)

    A = T * TOP_K
    flat_e = top_idx.reshape(A)
    order = jnp.argsort(flat_e)
    sorted_e = flat_e[order]
    sorted_tok = order // TOP_K
    counts = jnp.bincount(flat_e, length=N_EXPERTS)
    starts = jnp.cumsum(counts) - counts
    padded = (counts + MOE_BLOCK - 1) // MOE_BLOCK * MOE_BLOCK
    pends = jnp.cumsum(padded)
    pstarts = pends - padded
    dest = pstarts[sorted_e] + (jnp.arange(A) - starts[sorted_e])
    P = ((A + N_EXPERTS * MOE_BLOCK + MOE_BLOCK - 1) // MOE_BLOCK) * MOE_BLOCK
    NB = P // MOE_BLOCK
    buf_tok = jnp.full((P,), T, dtype=jnp.int32).at[dest].set(sorted_tok.astype(jnp.int32))
    xpad = jnp.concatenate([xf, jnp.zeros((1, D), xf.dtype)], axis=0)
    xbuf = xpad[buf_tok].reshape(NB, MOE_BLOCK, D)
    block_e = jnp.minimum(jnp.searchsorted(pends, jnp.arange(NB) * MOE_BLOCK, side='right'), N_EXPERTS - 1)

    def expert_block(args):
        xb, e = args
        g = xb @ w_gate[e] + b_gate[e]
        u = xb @ w_up[e] + b_up[e]
        g = jnp.minimum(g, SWIGLU_LIMIT)
        u = jnp.clip(u, -SWIGLU_LIMIT, SWIGLU_LIMIT)
        act = g * jax.nn.sigmoid(SWIGLU_ALPHA * g) * (u + 1.0)
        return act @ w_down[e] + b_down[e]

    ybuf = lax.map(expert_block, (xbuf, block_e)).reshape(P, D)
    y_assign = jnp.zeros((A, D), ybuf.dtype).at[order].set(ybuf[dest])
    y = jnp.einsum('tkd,tk->td', y_assign.reshape(T, TOP_K, D), gates)
    return y.reshape(B, S, D)


def setup_inputs(seed: int = 0) -> dict:
    key = jax.random.key(seed)
    ks = jax.random.split(key, 32)
    L, D, E, F = DEPTH, D_MODEL, N_EXPERTS, D_FF
    nrm = lambda k, shape, fan_in: jax.random.normal(k, shape, jnp.float32) * (fan_in ** -0.5)
    gain = lambda k, shape: 1.0 + 0.02 * jax.random.normal(k, shape, jnp.float32)
    small = lambda k, shape: 0.01 * jax.random.normal(k, shape, jnp.float32)
    return {
        "x": jax.random.normal(ks[0], (BATCH, SEQ, D), jnp.float32),
        "c": jax.random.normal(ks[1], (BATCH, D), jnp.float32),
        "norm1_g": gain(ks[2], (L, D)),
        "norm2_g": gain(ks[3], (L, D)),
        "w_ada": 0.5 * nrm(ks[4], (L, D, 6 * D), D),
        "b_ada": small(ks[5], (L, 6 * D)),
        "w_in": nrm(ks[6], (L, D, IN_COLS), D),
        "conv_w": nrm(ks[7], (L, CONV_WIDTH, CONV_CH), CONV_WIDTH),
        "conv_b": small(ks[8], (L, CONV_CH)),
        "conv_ln_g": gain(ks[9], (L, CONV_CH)),
        "conv_ln_b": small(ks[10], (L, CONV_CH)),
        "cmp_pe_k": 0.1 * jax.random.normal(ks[11], (L, CMP_LEN, HEAD_DIM), jnp.float32),
        "cmp_pe_v": 0.1 * jax.random.normal(ks[12], (L, CMP_LEN, HEAD_DIM), jnp.float32),
        "cmp_k_w1": nrm(ks[13], (L, CMP_LEN * HEAD_DIM, CMP_HIDDEN), CMP_LEN * HEAD_DIM),
        "cmp_k_w2": nrm(ks[14], (L, CMP_HIDDEN, HEAD_DIM), CMP_HIDDEN),
        "cmp_v_w1": nrm(ks[15], (L, CMP_LEN * HEAD_DIM, CMP_HIDDEN), CMP_LEN * HEAD_DIM),
        "cmp_v_w2": nrm(ks[16], (L, CMP_HIDDEN, HEAD_DIM), CMP_HIDDEN),
        "out_norm_conv": gain(ks[17], (L, CONV_CH)),
        "out_norm_nsa": gain(ks[18], (L, NSA_WIDTH)),
        "w_out": nrm(ks[19], (L, MIX_WIDTH, D), MIX_WIDTH),
        "w_router": nrm(ks[20], (L, D, E), D),
        "b_router": small(ks[21], (L, E)),
        "w_gate": nrm(ks[22], (L, E, D, F), D),
        "b_gate": small(ks[23], (L, E, F)),
        "w_up": nrm(ks[24], (L, E, D, F), D),
        "b_up": small(ks[25], (L, E, F)),
        "w_down": nrm(ks[26], (L, E, F, D), F),
        "b_down": small(ks[27], (L, E, D)),
        "final_norm_g": gain(ks[28], (D,)),
    }


def reference(x, c, norm1_g, norm2_g, w_ada, b_ada, w_in, conv_w, conv_b, conv_ln_g, conv_ln_b,
              cmp_pe_k, cmp_pe_v, cmp_k_w1, cmp_k_w2, cmp_v_w1, cmp_v_w2, out_norm_conv, out_norm_nsa,
              w_out, w_router, b_router, w_gate, b_gate, w_up, b_up, w_down, b_down, final_norm_g):
    B, S, _ = x.shape
    cos, sin = rope_tables(S)
    c_act = jax.nn.silu(c)
    split_pts = [int(v) for v in np.cumsum([CONV_CH, CONV_CH, NSA_WIDTH, KV_W, KV_W, KV_W, KV_W, KV_W, KV_W])]
    G, NQ, HD = NSA_KV_GROUPS, NSA_Q_PER_G, HEAD_DIM
    for l in range(DEPTH):
        mod = c_act @ w_ada[l] + b_ada[l]
        sh1, sc1, gt1, sh2, sc2, gt2 = jnp.split(mod, 6, axis=-1)

        h = rmsnorm(x, norm1_g[l]) * (1.0 + sc1[:, None, :]) + sh1[:, None, :]
        proj = h @ w_in[l]
        a_c, g_c, q, kc, vc, ks_, vs_, kw, vw, gl = jnp.split(proj, split_pts, axis=-1)

        conv_out = conv_module(a_c, g_c, conv_w[l], conv_b[l], conv_ln_g[l], conv_ln_b[l])

        q = partial_rope(q.reshape(B, S, NSA_HEADS, HD), cos, sin).reshape(B, S, G, NQ, HD)
        kc = partial_rope(kc.reshape(B, S, G, HD), cos, sin)
        ks_ = partial_rope(ks_.reshape(B, S, G, HD), cos, sin)
        kw = partial_rope(kw.reshape(B, S, G, HD), cos, sin)
        vc = vc.reshape(B, S, G, HD)
        vs_ = vs_.reshape(B, S, G, HD)
        vw = vw.reshape(B, S, G, HD)
        br_gates = jax.nn.sigmoid(gl).reshape(B, S, G, NQ, 3)
        nsa_out = nsa_mixer(q, kc, vc, ks_, vs_, kw, vw, br_gates,
                            cmp_pe_k[l], cmp_pe_v[l], cmp_k_w1[l], cmp_k_w2[l], cmp_v_w1[l], cmp_v_w2[l])

        mixed = jnp.concatenate([rmsnorm(conv_out, out_norm_conv[l]),
                                 rmsnorm(nsa_out, out_norm_nsa[l])], axis=-1)
        x = x + gt1[:, None, :] * (mixed @ w_out[l])

        h2 = rmsnorm(x, norm2_g[l]) * (1.0 + sc2[:, None, :]) + sh2[:, None, :]
        x = x + gt2[:, None, :] * moe(h2, w_router[l], b_router[l], w_gate[l], b_gate[l],
                                      w_up[l], b_up[l], w_down[l], b_down[l])
    return rmsnorm(x, final_norm_g)
```

```python
import functools

import jax
import jax.numpy as jnp
from jax import lax
from jax.experimental import pallas as pl
from jax.experimental.pallas import tpu as pltpu

F32 = jnp.float32
BF16 = jnp.bfloat16
I32 = jnp.int32
HI = lax.Precision.HIGHEST

D_MODEL = 1024
CONV_CH = 512
CONV_WIDTH = 31
NSA_HEADS = 8
KV_GROUPS = 2
Q_PER_G = NSA_HEADS // KV_GROUPS
HEAD_DIM = 64
NSA_WIDTH = NSA_HEADS * HEAD_DIM
KV_W = KV_GROUPS * HEAD_DIM
ROT_DIM = HEAD_DIM // 4
ROT_HALF = ROT_DIM // 2
ROPE_THETA = 500000.0
CMP_LEN = 32
CMP_STRIDE = 16
CMP_HIDDEN = 128
SEL_BLK = 64
SEL_TOPN = 16
WINDOW = 512
N_EXPERTS = 32
TOP_K = 4
D_FF = 1024
SWIGLU_ALPHA = 1.702
SWIGLU_LIMIT = 7.0
NORM_EPS = 1e-5
NEG_INF = -1e30
FORCE_SCORE = 1e9

LANES = 128
SUBLANES = 8
ROW_TILES = D_MODEL // LANES

GATE_PAD = LANES
IN_COLS_PAD = 2 * CONV_CH + NSA_WIDTH + 6 * KV_W + GATE_PAD

INPROJ_TM = 512
CONV_TR = 256
CONV_HALO = 32
ATTN_TQ = 128
ATTN_TK = 256
OUT_TM = 512
FFN_BM = 512
COMB_TM = 256
ROUTE_W = 8


def _rms(x, g):
    return x * lax.rsqrt(jnp.mean(x * x, axis=-1, keepdims=True) + NORM_EPS) * g


def _ada_kernel(c_ref, w_ref, b_ref, o_ref):
    c = c_ref[...]
    ca = c * jax.nn.sigmoid(c)
    o_ref[...] = jnp.dot(ca, w_ref[...], preferred_element_type=F32, precision=HI) + b_ref[...]


def _adaln(c, w, b):
    B = c.shape[0]
    D = D_MODEL
    return pl.pallas_call(
        _ada_kernel,
        grid=(6,),
        in_specs=[pl.BlockSpec((B, D), lambda j: (0, 0)),
                  pl.BlockSpec((D, D), lambda j: (0, j)),
                  pl.BlockSpec((1, D), lambda j: (0, j))],
        out_specs=pl.BlockSpec((B, D), lambda j: (0, j)),
        out_shape=jax.ShapeDtypeStruct((B, 6 * D), F32),
        name="adaln",
    )(c, w, b)


def _inproj_kernel(x_ref, sc_ref, sh_ref, g_ref, w_ref, rc_ref, rs1_ref, rs2_ref,
                   u_ref, q_ref, kc_ref, vc_ref, ks_ref, vs_ref, kw_ref, vw_ref, gt_ref):
    h = _rms(x_ref[0], g_ref[...]) * (1.0 + sc_ref[0]) + sh_ref[0]
    p = jnp.dot(h.astype(BF16), w_ref[...], preferred_element_type=F32)
    c0 = 0
    a = p[:, c0:c0 + CONV_CH]
    g = p[:, c0 + CONV_CH:c0 + 2 * CONV_CH]
    u_ref[0] = a * jax.nn.sigmoid(g)
    c0 = 2 * CONV_CH

    rc, rs1, rs2 = rc_ref[...], rs1_ref[...], rs2_ref[...]

    def rope(v):
        w = v.shape[1]
        n = w // LANES
        tile = lambda t: t if n == 1 else jnp.concatenate([t] * n, axis=1)
        return (v * tile(rc) + pltpu.roll(v, ROT_HALF, 1) * tile(rs1)
                + pltpu.roll(v, w - ROT_HALF, 1) * tile(rs2))

    q = rope(p[:, c0:c0 + NSA_WIDTH]) * (HEAD_DIM ** -0.5)
    for hh in range(NSA_HEADS):
        q_ref[0, hh] = q[:, HEAD_DIM * hh:HEAD_DIM * (hh + 1)].astype(BF16)
    c0 += NSA_WIDTH
    outs = ((kc_ref, True), (ks_ref, True), (kw_ref, True), (vc_ref, False), (vs_ref, False), (vw_ref, False))
    for ref, roped in outs:
        v = p[:, c0:c0 + KV_W]
        if roped:
            v = rope(v)
        for gg in range(KV_GROUPS):
            ref[0, gg] = v[:, HEAD_DIM * gg:HEAD_DIM * (gg + 1)].astype(ref.dtype)
        c0 += KV_W
    sg = jax.nn.sigmoid(p[:, c0:c0 + GATE_PAD])
    per_g = 3 * Q_PER_G
    for gg in range(KV_GROUPS):
        gt_ref[0, gg] = sg[:, per_g * gg:per_g * (gg + 1)]


def _inproj(x, sc, sh, g, w, rc, rs1, rs2):
    B, S, D = x.shape
    tm = min(INPROJ_TM, S)
    kv = lambda dt: jax.ShapeDtypeStruct((B, KV_GROUPS, S, HEAD_DIM), dt)
    kv_spec = pl.BlockSpec((1, KV_GROUPS, tm, HEAD_DIM), lambda b, i: (b, 0, i, 0))
    row = pl.BlockSpec((1, 1, D), lambda b, i: (b, 0, 0))
    tab = pl.BlockSpec((tm, LANES), lambda b, i: (i, 0))
    return pl.pallas_call(
        _inproj_kernel,
        grid=(B, S // tm),
        in_specs=[pl.BlockSpec((1, tm, D), lambda b, i: (b, i, 0)), row, row,
                  pl.BlockSpec((1, D), lambda b, i: (0, 0)),
                  pl.BlockSpec((D, IN_COLS_PAD), lambda b, i: (0, 0)),
                  tab, tab, tab],
        out_specs=[pl.BlockSpec((1, tm, CONV_CH), lambda b, i: (b, i, 0)),
                   pl.BlockSpec((1, NSA_HEADS, tm, HEAD_DIM), lambda b, i: (b, 0, i, 0)),
                   kv_spec, kv_spec, kv_spec, kv_spec, kv_spec, kv_spec,
                   pl.BlockSpec((1, KV_GROUPS, tm, 3 * Q_PER_G), lambda b, i: (b, 0, i, 0))],
        out_shape=[jax.ShapeDtypeStruct((B, S, CONV_CH), F32),
                   jax.ShapeDtypeStruct((B, NSA_HEADS, S, HEAD_DIM), BF16),
                   kv(F32), kv(F32), kv(BF16), kv(BF16), kv(BF16), kv(BF16),
                   jax.ShapeDtypeStruct((B, KV_GROUPS, S, 3 * Q_PER_G), F32)],
        compiler_params=pltpu.CompilerParams(dimension_semantics=("parallel", "parallel")),
        name="inproj",
    )(x, sc, sh, g, w, rc, rs1, rs2)


def _conv_kernel(prev_ref, cur_ref, w_ref, cb_ref, lg_ref, lb_ref, on_ref, o_ref, pad_ref):
    tr = cur_ref.shape[1]
    first = pl.program_id(1) == 0
    halo = prev_ref[0, tr - CONV_HALO:tr, :]
    pad_ref[0:CONV_HALO, :] = jnp.where(first, 0.0, halo)
    pad_ref[CONV_HALO:CONV_HALO + tr, :] = cur_ref[0]
    off = CONV_HALO - (CONV_WIDTH - 1)
    acc = jnp.zeros((tr, CONV_CH), F32)
    for k in range(CONV_WIDTH):
        acc = acc + pad_ref[off + k:off + k + tr, :] * w_ref[k:k + 1, :]
    y = acc + cb_ref[...]
    mu = jnp.mean(y, axis=-1, keepdims=True)
    yc = y - mu
    var = jnp.mean(yc * yc, axis=-1, keepdims=True)
    yn = yc * lax.rsqrt(var + NORM_EPS) * lg_ref[...] + lb_ref[...]
    s = yn * jax.nn.sigmoid(yn)
    o_ref[0] = _rms(s, on_ref[...]).astype(o_ref.dtype)


def _conv(u, w, cb, lg, lb, on):
    B, S, C = u.shape
    tr = min(CONV_TR, S)
    vec = pl.BlockSpec((1, C), lambda b, i: (0, 0))
    return pl.pallas_call(
        _conv_kernel,
        grid=(B, S // tr),
        in_specs=[pl.BlockSpec((1, tr, C), lambda b, i: (b, jnp.maximum(i - 1, 0), 0)),
                  pl.BlockSpec((1, tr, C), lambda b, i: (b, i, 0)),
                  pl.BlockSpec((CONV_WIDTH, C), lambda b, i: (0, 0)),
                  vec, vec, vec, vec],
        out_specs=pl.BlockSpec((1, tr, C), lambda b, i: (b, i, 0)),
        out_shape=jax.ShapeDtypeStruct((B, S, C), BF16),
        scratch_shapes=[pltpu.VMEM((CONV_HALO + tr, C), F32)],
        compiler_params=pltpu.CompilerParams(dimension_semantics=("parallel", "parallel")),
        name="conv",
    )(u, u, w, cb, lg, lb, on)


def _cmp_kernel(kx_ref, vx_ref, pek_ref, pev_ref, kw1_ref, kw2_ref, vw1_ref, vw2_ref, ko_ref, vo_ref):
    nc = kx_ref.shape[2]
    half = kx_ref.shape[3]
    for x_ref, pe_ref, w1_ref, w2_ref, o_ref in ((kx_ref, pek_ref, kw1_ref, kw2_ref, ko_ref),
                                                 (vx_ref, pev_ref, vw1_ref, vw2_ref, vo_ref)):
        w1 = w1_ref[...]
        pe = jnp.broadcast_to(pe_ref[...], (SUBLANES, 2 * half))
        pe_proj = jnp.dot(pe, w1, preferred_element_type=F32, precision=HI)[0:1]
        for gg in range(KV_GROUPS):
            xg = x_ref[0, gg]
            first = jnp.dot(xg, w1[0:half], preferred_element_type=F32, precision=HI)
            second = jnp.dot(xg, w1[half:2 * half], preferred_element_type=F32, precision=HI)
            hid = first + pltpu.roll(second, nc - 1, 0) + pe_proj
            hid = hid * jax.nn.sigmoid(hid)
            o_ref[0, gg] = jnp.dot(hid, w2_ref[...], preferred_element_type=F32, precision=HI)


def _compress(kx, vx, pek, pev, kw1, kw2, vw1, vw2):
    B, G, NC, W = kx.shape
    xs = pl.BlockSpec((1, G, NC, W), lambda b: (b, 0, 0, 0))
    full = lambda a: pl.BlockSpec(a.shape, lambda b: (0,) * a.ndim)
    os_ = pl.BlockSpec((1, G, NC, HEAD_DIM), lambda b: (b, 0, 0, 0))
    oshape = jax.ShapeDtypeStruct((B, G, NC, HEAD_DIM), F32)
    return pl.pallas_call(
        _cmp_kernel,
        grid=(B,),
        in_specs=[xs, xs, full(pek), full(pev), full(kw1), full(kw2), full(vw1), full(vw2)],
        out_specs=[os_, os_],
        out_shape=[oshape, oshape],
        compiler_params=pltpu.CompilerParams(dimension_semantics=("parallel",)),
        name="compress",
    )(kx, vx, pek, pev, kw1, kw2, vw1, vw2)


def _attn_kernel(q_ref, kc_ref, vc_ref, ks_ref, vs_ref, kw_ref, vw_ref, gt_ref, ov_ref, ex_ref,
                 o_ref, bias_ref):
    tq, tk = ATTN_TQ, ATTN_TK
    rows = Q_PER_G * tq
    S = ks_ref.shape[2]
    ncp = kc_ref.shape[2]
    nsel = ov_ref.shape[1]
    qi = pl.program_id(2)
    q0 = qi * tq
    q = q_ref[0].reshape(rows, HEAD_DIM)
    t_col = q0 + lax.broadcasted_iota(I32, (tq, 1), 0)
    nt = (((1,), (1,)), ((), ()))

    sc = lax.dot_general(q, kc_ref[0, 0].astype(BF16), nt, preferred_element_type=F32)
    sc = sc.reshape(Q_PER_G, tq, ncp)
    c_io = lax.broadcasted_iota(I32, (tq, ncp), 1)
    m_c = (c_io * CMP_STRIDE + (CMP_LEN - 1) <= t_col) & (c_io < ncp - 1)
    scm = jnp.where(m_c[None], sc, NEG_INF)
    mx = jnp.max(scm, axis=-1, keepdims=True)
    e = jnp.where(m_c[None], jnp.exp(scm - mx), 0.0)
    den = jnp.sum(e, axis=-1, keepdims=True)
    pc = e / jnp.where(den > 0.0, den, 1.0)
    o_cmp = jnp.dot(pc.reshape(rows, ncp).astype(BF16), vc_ref[0, 0].astype(BF16),
                    preferred_element_type=F32)

    imp = jnp.dot(jnp.sum(pc, axis=0), ov_ref[...], preferred_element_type=F32, precision=HI)
    j_io = lax.broadcasted_iota(I32, (tq, nsel), 1)
    cur = t_col // SEL_BLK
    valid = j_io * SEL_BLK <= t_col
    forced = (j_io == 0) | (j_io == cur) | (j_io == cur - 1)
    score = jnp.where(valid, jnp.where(forced, FORCE_SCORE, imp), NEG_INF)
    rank = jnp.zeros((tq, nsel), F32)
    for i in range(nsel):
        col = score[:, i:i + 1]
        beats = (col > score) | ((col == score) & (j_io > i))
        rank = rank + jnp.where(beats, 1.0, 0.0)
    sel = jnp.where(rank < float(min(SEL_TOPN, nsel)), 1.0, 0.0).astype(BF16)
    selexp = jnp.dot(sel, ex_ref[...], preferred_element_type=F32)
    for jj in range(S // tk):
        bias_ref[jj] = (selexp[:, jj * tk:(jj + 1) * tk] - 1.0) * (-NEG_INF)

    k_io = lax.broadcasted_iota(I32, (tq, tk), 1)

    def flash_step(k_ref, v_ref, kj, bias, carry):
        m, l, acc = carry
        k0 = pl.multiple_of(kj * tk, tk)
        s = lax.dot_general(q, k_ref[0, 0, pl.ds(k0, tk), :], nt, preferred_element_type=F32)
        s = s.reshape(Q_PER_G, tq, tk) + bias[None]
        m_new = jnp.maximum(m, jnp.max(s, axis=-1, keepdims=True))
        alpha = jnp.exp(m - m_new)
        p = jnp.exp(s - m_new)
        l = alpha * l + jnp.sum(p, axis=-1, keepdims=True)
        pv = jnp.dot(p.reshape(rows, tk).astype(BF16), v_ref[0, 0, pl.ds(k0, tk), :],
                     preferred_element_type=F32)
        acc = alpha * acc + pv.reshape(Q_PER_G, tq, HEAD_DIM)
        return m_new, l, acc

    init = (jnp.full((Q_PER_G, tq, 1), NEG_INF, F32), jnp.zeros((Q_PER_G, tq, 1), F32),
            jnp.zeros((Q_PER_G, tq, HEAD_DIM), F32))

    def slc_body(kj, carry):
        kpos = kj * tk + k_io
        bias = jnp.where(kpos <= t_col, bias_ref[kj], NEG_INF)
        return flash_step(ks_ref, vs_ref, kj, bias, carry)

    n_slc = (q0 + tq + tk - 1) // tk
    _, l_s, acc_s = lax.fori_loop(0, n_slc, slc_body, init)
    o_slc = acc_s / l_s

    def win_body(kj, carry):
        kpos = kj * tk + k_io
        rel = t_col - kpos
        bias = jnp.where((rel >= 0) & (rel < WINDOW), 0.0, NEG_INF)
        return flash_step(kw_ref, vw_ref, kj, bias, carry)

    lo_tile = jnp.maximum(q0 - (WINDOW - 1), 0) // tk
    _, l_w, acc_w = lax.fori_loop(lo_tile, n_slc, win_body, init)
    o_win = acc_w / l_w

    o_cmp = o_cmp.reshape(Q_PER_G, tq, HEAD_DIM)
    gt = gt_ref[0, 0]
    outs = []
    for n in range(Q_PER_G):
        outs.append(gt[:, 3 * n:3 * n + 1] * o_cmp[n] + gt[:, 3 * n + 1:3 * n + 2] * o_slc[n]
                    + gt[:, 3 * n + 2:3 * n + 3] * o_win[n])
    o_ref[0] = jnp.concatenate(outs, axis=1)


def _attention(q, kc, vc, ks, vs, kw, vw, gt, overlap, expand):
    B, H, S, _ = q.shape
    G = KV_GROUPS
    tq, tk = ATTN_TQ, ATTN_TK
    ncp = kc.shape[2]
    kvfull = pl.BlockSpec((1, 1, S, HEAD_DIM), lambda b, g, i: (b, g, 0, 0))
    cfull = pl.BlockSpec((1, 1, ncp, HEAD_DIM), lambda b, g, i: (b, g, 0, 0))
    return pl.pallas_call(
        _attn_kernel,
        grid=(B, G, S // tq),
        in_specs=[pl.BlockSpec((1, Q_PER_G, tq, HEAD_DIM), lambda b, g, i: (b, g, i, 0)),
                  cfull, cfull, kvfull, kvfull, kvfull, kvfull,
                  pl.BlockSpec((1, 1, tq, 3 * Q_PER_G), lambda b, g, i: (b, g, i, 0)),
                  pl.BlockSpec(overlap.shape, lambda b, g, i: (0, 0)),
                  pl.BlockSpec(expand.shape, lambda b, g, i: (0, 0))],
        out_specs=pl.BlockSpec((1, tq, Q_PER_G * HEAD_DIM), lambda b, g, i: (b, i, g)),
        out_shape=jax.ShapeDtypeStruct((B, S, NSA_WIDTH), F32),
        scratch_shapes=[pltpu.VMEM((S // tk, tq, tk), F32)],
        compiler_params=pltpu.CompilerParams(dimension_semantics=("parallel", "parallel", "arbitrary")),
        name="attn",
    )(q, kc, vc, ks, vs, kw, vw, gt, overlap, expand)


def _outproj_kernel(x_ref, cv_ref, nsa_ref, on_ref, w_ref, gt1_ref, g2_ref, sc2_ref, sh2_ref, wr_ref, br_ref,
                    x1_ref, h2_ref, rt_ref):
    tm = x_ref.shape[1]
    nn = _rms(nsa_ref[0], on_ref[...]).astype(BF16)
    y = (jnp.dot(cv_ref[0], w_ref[0:CONV_CH, :], preferred_element_type=F32)
         + jnp.dot(nn, w_ref[CONV_CH:CONV_CH + NSA_WIDTH, :], preferred_element_type=F32))
    x1 = x_ref[0] + gt1_ref[0] * y
    x1_ref[0] = x1
    h2 = _rms(x1, g2_ref[...]) * (1.0 + sc2_ref[0]) + sh2_ref[0]
    for s in range(ROW_TILES):
        h2_ref[pl.ds(s, tm, stride=ROW_TILES), :] = h2[:, s * LANES:(s + 1) * LANES]
    logits = jnp.dot(h2, wr_ref[...], preferred_element_type=F32, precision=HI) + br_ref[...]
    lane = lax.broadcasted_iota(I32, (tm, N_EXPERTS), 1).astype(F32)
    vals, idxs = [], []
    for _ in range(TOP_K):
        m = jnp.max(logits, axis=-1, keepdims=True)
        ix = jnp.min(jnp.where(logits == m, lane, float(N_EXPERTS)), axis=-1, keepdims=True)
        vals.append(m)
        idxs.append(ix)
        logits = jnp.where(lane == ix, -jnp.inf, logits)
    es = [jnp.exp(v - vals[0]) for v in vals]
    den = es[0] + es[1] + es[2] + es[3]
    ol = lax.broadcasted_iota(I32, (tm, ROUTE_W), 1)
    out = jnp.zeros((tm, ROUTE_W), F32)
    for r in range(TOP_K):
        out = jnp.where(ol == r, idxs[r], out)
        out = jnp.where(ol == TOP_K + r, es[r] / den, out)
    rt_ref[0] = out


def _outproj(x, cv, nsa, on, w, gt1, g2, sc2, sh2, wr, br):
    B, S, D = x.shape
    tm = min(OUT_TM, S)
    nt = S // tm
    row = pl.BlockSpec((1, 1, D), lambda b, i: (b, 0, 0))
    vec = lambda n: pl.BlockSpec((1, n), lambda b, i: (0, 0))
    return pl.pallas_call(
        _outproj_kernel,
        grid=(B, nt),
        in_specs=[pl.BlockSpec((1, tm, D), lambda b, i: (b, i, 0)),
                  pl.BlockSpec((1, tm, CONV_CH), lambda b, i: (b, i, 0)),
                  pl.BlockSpec((1, tm, NSA_WIDTH), lambda b, i: (b, i, 0)),
                  vec(NSA_WIDTH),
                  pl.BlockSpec((D, D), lambda b, i: (0, 0)),
                  row, vec(D), row, row,
                  pl.BlockSpec((D, N_EXPERTS), lambda b, i: (0, 0)), vec(N_EXPERTS)],
        out_specs=[pl.BlockSpec((1, tm, D), lambda b, i: (b, i, 0)),
                   pl.BlockSpec((tm * ROW_TILES, LANES), lambda b, i: (b * nt + i, 0)),
                   pl.BlockSpec((1, tm, ROUTE_W), lambda b, i: (b, i, 0))],
        out_shape=[jax.ShapeDtypeStruct((B, S, D), F32),
                   jax.ShapeDtypeStruct((B * S * ROW_TILES, LANES), F32),
                   jax.ShapeDtypeStruct((B, S, ROUTE_W), F32)],
        compiler_params=pltpu.CompilerParams(dimension_semantics=("parallel", "parallel")),
        name="outproj",
    )(x, cv, nsa, on, w, gt1, g2, sc2, sh2, wr, br)


def _issue_rows(idx_ref, n, src_hbm, dst, slot, sem):
    def body(r, c):
        pltpu.make_async_copy(src_hbm.at[idx_ref[0, 0, r]],
                              dst.at[slot, pl.ds(pl.multiple_of(r * ROW_TILES, ROW_TILES), ROW_TILES), :],
                              sem.at[slot]).start()
        return c
    lax.fori_loop(0, n, body, 0)


def _wait_rows(dst, slot, sem):
    pltpu.make_async_copy(dst.at[slot], dst.at[slot], sem.at[slot]).wait()


def _rows_2d(buf, slot, base, n):
    return jnp.concatenate(
        [buf[slot, pl.ds(base * ROW_TILES + s, n, stride=ROW_TILES), :] for s in range(ROW_TILES)], axis=1)


def _ffn_kernel(be_ref, nb_ref, ta_ref, tb_ref, h2_hbm, gate_ref, wg_ref, bg_ref, wu_ref, bu_ref, wd_ref, bd_ref,
                o_ref, xbuf, sem):
    bm = gate_ref.shape[0]
    i = pl.program_id(0)
    nb = nb_ref[0]
    slot = i % 2

    @pl.when(i == 0)
    def _():
        _issue_rows(ta_ref, bm, h2_hbm, xbuf, 0, sem)

    @pl.when(i + 1 < nb)
    def _():
        _issue_rows(tb_ref, bm, h2_hbm, xbuf, 1 - slot, sem)

    @pl.when(i < nb)
    def _():
        _wait_rows(xbuf, slot, sem)
        x = _rows_2d(xbuf, slot, 0, bm).astype(BF16)
        g = jnp.dot(x, wg_ref[0], preferred_element_type=F32) + bg_ref[0]
        u = jnp.dot(x, wu_ref[0], preferred_element_type=F32) + bu_ref[0]
        g = jnp.minimum(g, SWIGLU_LIMIT)
        u = jnp.clip(u, -SWIGLU_LIMIT, SWIGLU_LIMIT)
        act = g * jax.nn.sigmoid(SWIGLU_ALPHA * g) * (u + 1.0)
        y = jnp.dot(act.astype(BF16), wd_ref[0], preferred_element_type=F32) + bd_ref[0]
        y = y * gate_ref[...]
        for s in range(ROW_TILES):
            o_ref[pl.ds(s, bm, stride=ROW_TILES), :] = y[:, s * LANES:(s + 1) * LANES]

    @pl.when(i >= nb)
    def _():
        o_ref[...] = jnp.zeros(o_ref.shape, o_ref.dtype)


def _ffn(block_e, nb_used, buf_tok, h2_rows, buf_gate, wg, bg, wu, bu, wd, bd):
    NB = block_e.shape[0]
    bm = FFN_BM
    D, F = D_MODEL, D_FF
    tok3 = buf_tok.reshape(NB, 1, bm)
    wspec = lambda r, c: pl.BlockSpec((1, r, c), lambda i, be, nb: (be[i], 0, 0))
    return pl.pallas_call(
        _ffn_kernel,
        grid_spec=pltpu.PrefetchScalarGridSpec(
            num_scalar_prefetch=2,
            grid=(NB,),
            in_specs=[pl.BlockSpec((1, 1, bm), lambda i, be, nb: (i, 0, 0), memory_space=pltpu.SMEM),
                      pl.BlockSpec((1, 1, bm), lambda i, be, nb: (jnp.minimum(i + 1, NB - 1), 0, 0),
                                   memory_space=pltpu.SMEM),
                      pl.BlockSpec(memory_space=pl.ANY),
                      pl.BlockSpec((bm, 1), lambda i, be, nb: (i, 0)),
                      wspec(D, F), wspec(1, F), wspec(D, F), wspec(1, F), wspec(F, D), wspec(1, D)],
            out_specs=pl.BlockSpec((bm * ROW_TILES, LANES), lambda i, be, nb: (i, 0)),
            scratch_shapes=[pltpu.VMEM((2, bm * ROW_TILES, LANES), F32), pltpu.SemaphoreType.DMA((2,))]),
        out_shape=jax.ShapeDtypeStruct((NB * bm * ROW_TILES, LANES), F32),
        compiler_params=pltpu.CompilerParams(dimension_semantics=("arbitrary",)),
        name="ffn",
    )(block_e, nb_used, tok3, tok3, h2_rows, buf_gate, wg, bg, wu, bu, wd, bd)


def _combine_kernel(da_ref, db_ref, y_hbm, x1_ref, gt2_ref, fg_ref, o_ref, buf, sem):
    tm = x1_ref.shape[0]
    n = TOP_K * tm
    i = pl.program_id(0)
    slot = i % 2

    @pl.when(i == 0)
    def _():
        _issue_rows(da_ref, n, y_hbm, buf, 0, sem)

    @pl.when(i + 1 < pl.num_programs(0))
    def _():
        _issue_rows(db_ref, n, y_hbm, buf, 1 - slot, sem)

    _wait_rows(buf, slot, sem)
    y = _rows_2d(buf, slot, 0, tm)
    for k in range(1, TOP_K):
        y = y + _rows_2d(buf, slot, k * tm, tm)
    x2 = x1_ref[...] + gt2_ref[0] * y
    o_ref[...] = _rms(x2, fg_ref[...])


def _combine(dest3, y_rows, x1, gt2, fg, S):
    T, D = x1.shape
    tm = min(COMB_TM, S)
    NT = T // tm
    per_b = S // tm
    n = TOP_K * tm
    return pl.pallas_call(
        _combine_kernel,
        grid=(NT,),
        in_specs=[pl.BlockSpec((1, 1, n), lambda i: (i, 0, 0), memory_space=pltpu.SMEM),
                  pl.BlockSpec((1, 1, n), lambda i: (jnp.minimum(i + 1, NT - 1), 0, 0), memory_space=pltpu.SMEM),
                  pl.BlockSpec(memory_space=pl.ANY),
                  pl.BlockSpec((tm, D), lambda i: (i, 0)),
                  pl.BlockSpec((1, 1, D), lambda i: (i // per_b, 0, 0)),
                  pl.BlockSpec((1, D), lambda i: (0, 0))],
        out_specs=pl.BlockSpec((tm, D), lambda i: (i, 0)),
        out_shape=jax.ShapeDtypeStruct((T, D), F32),
        scratch_shapes=[pltpu.VMEM((2, n * ROW_TILES, LANES), F32), pltpu.SemaphoreType.DMA((2,))],
        compiler_params=pltpu.CompilerParams(dimension_semantics=("arbitrary",)),
        name="combine",
    )(dest3, dest3, y_rows, x1, gt2, fg)


def _rope_lane_tables(S):
    inv = ROPE_THETA ** (-jnp.arange(0, ROT_DIM, 2, dtype=F32) / ROT_DIM)
    ang = jnp.arange(S, dtype=F32)[:, None] * inv[None, :]
    cos, sin = jnp.cos(ang), jnp.sin(ang)
    d = jnp.arange(LANES) % HEAD_DIM
    first, second = d < ROT_HALF, (d >= ROT_HALF) & (d < ROT_DIM)
    cos_l = cos[:, d % ROT_HALF]
    sin_l = sin[:, d % ROT_HALF]
    rc = jnp.where((d < ROT_DIM)[None], cos_l, 1.0)
    rs1 = jnp.where(second[None], sin_l, 0.0)
    rs2 = jnp.where(first[None], -sin_l, 0.0)
    return rc, rs1, rs2


def _route_plan(route, T):
    bm = FFN_BM
    A = T * TOP_K
    flat_e = route[:, :TOP_K].astype(I32).reshape(A)
    flat_g = route[:, TOP_K:2 * TOP_K].reshape(A)
    order = jnp.argsort(flat_e, stable=True).astype(I32)
    sorted_e = flat_e[order]
    counts = jnp.bincount(flat_e, length=N_EXPERTS).astype(I32)
    starts = jnp.cumsum(counts) - counts
    padded = (counts + bm - 1) // bm * bm
    pends = jnp.cumsum(padded)
    pstarts = pends - padded
    dest_sorted = pstarts[sorted_e] + (jnp.arange(A, dtype=I32) - starts[sorted_e])
    P = (A + N_EXPERTS * bm + bm - 1) // bm * bm
    NB = P // bm
    buf_tok = jnp.zeros((P,), I32).at[dest_sorted].set(order // TOP_K)
    buf_gate = jnp.zeros((P,), F32).at[dest_sorted].set(flat_g[order])
    dest_assign = jnp.zeros((A,), I32).at[order].set(dest_sorted)
    block_e = jnp.minimum(jnp.searchsorted(pends, jnp.arange(NB, dtype=I32) * bm, side='right'),
                          N_EXPERTS - 1).astype(I32)
    nb_used = (pends[-1] // bm).astype(I32).reshape(1)
    return block_e, nb_used, buf_tok, buf_gate.reshape(P, 1), dest_assign.reshape(T, TOP_K)


def kernel(x, c, norm1_g, norm2_g, w_ada, b_ada, w_in, conv_w, conv_b, conv_ln_g, conv_ln_b, cmp_pe_k, cmp_pe_v,
           cmp_k_w1, cmp_k_w2, cmp_v_w1, cmp_v_w2, out_norm_conv, out_norm_nsa, w_out, w_router, b_router,
           w_gate, b_gate, w_up, b_up, w_down, b_down, final_norm_g):
    B, S, D = x.shape
    T = B * S
    G = KV_GROUPS
    assert D == D_MODEL and S % ATTN_TK == 0 and S % CMP_STRIDE == 0
    rc, rs1, rs2 = _rope_lane_tables(S)
    n_sel = S // SEL_BLK
    nc = S // CMP_STRIDE
    cstart = jnp.arange(nc) * CMP_STRIDE
    jstart = jnp.arange(n_sel) * SEL_BLK
    overlap = ((cstart[:, None] <= jstart[None, :] + SEL_BLK - 1)
               & (cstart[:, None] + CMP_LEN - 1 >= jstart[None, :])
               & (jnp.arange(nc)[:, None] < nc - 1)).astype(F32)
    expand = (jnp.arange(S)[None, :] // SEL_BLK == jnp.arange(n_sel)[:, None]).astype(BF16)

    assert w_ada.shape[0] == 1
    for l in range(1):
        mod = _adaln(c, w_ada[l], b_ada[l][None])
        sh1, sc1, gt1, sh2, sc2, gt2 = [m[:, None, :] for m in jnp.split(mod, 6, axis=-1)]

        o = 2 * CONV_CH + NSA_WIDTH
        wl = w_in[l]
        kvc = [wl[:, o + i * KV_W:o + (i + 1) * KV_W] for i in range(6)]
        gl = wl[:, o + 6 * KV_W:]
        w_perm = jnp.concatenate([wl[:, :o], kvc[0], kvc[2], kvc[4], kvc[1], kvc[3], kvc[5],
                                  jnp.pad(gl, ((0, 0), (0, GATE_PAD - gl.shape[1])))], axis=1).astype(BF16)
        u, q, kc, vc, ks, vs, kw, vw, gates = _inproj(x, sc1, sh1, norm1_g[l][None], w_perm, rc, rs1, rs2)

        conv_n = _conv(u, conv_w[l], conv_b[l][None], conv_ln_g[l][None], conv_ln_b[l][None],
                       out_norm_conv[l][None])

        chunk = CMP_STRIDE * HEAD_DIM
        kcmp, vcmp = _compress(kc.reshape(B, G, nc, chunk), vc.reshape(B, G, nc, chunk),
                               cmp_pe_k[l].reshape(1, -1), cmp_pe_v[l].reshape(1, -1),
                               cmp_k_w1[l], cmp_k_w2[l], cmp_v_w1[l], cmp_v_w2[l])
        nsa = _attention(q, kcmp, vcmp, ks, vs, kw, vw, gates, overlap, expand)

        x1, h2_rows, route = _outproj(x, conv_n, nsa, out_norm_nsa[l][None], w_out[l].astype(BF16), gt1,
                                      norm2_g[l][None], sc2, sh2, w_router[l], b_router[l][None])

        block_e, nb_used, buf_tok, buf_gate, dest_assign = _route_plan(route.reshape(T, ROUTE_W), T)
        y_rows = _ffn(block_e, nb_used, buf_tok, h2_rows.reshape(T, ROW_TILES, LANES), buf_gate,
                      w_gate[l].astype(BF16), b_gate[l][:, None, :], w_up[l].astype(BF16), b_up[l][:, None, :],
                      w_down[l].astype(BF16), b_down[l][:, None, :])
        tm = min(COMB_TM, S)
        dest3 = dest_assign.reshape(T // tm, tm, TOP_K).transpose(0, 2, 1).reshape(T // tm, 1, TOP_K * tm)
        P = y_rows.shape[0] // ROW_TILES
        x = _combine(dest3, y_rows.reshape(P, ROW_TILES, LANES), x1.reshape(T, D), gt2, final_norm_g[None],
                     S).reshape(B, S, D)
    return x
```

```python
import functools

import jax
import jax.numpy as jnp
from jax import lax
from jax.experimental import pallas as pl
from jax.experimental.pallas import tpu as pltpu

F32 = jnp.float32
BF16 = jnp.bfloat16
I32 = jnp.int32
HI = lax.Precision.HIGHEST

D_MODEL = 1024
CONV_CH = 512
CONV_WIDTH = 31
NSA_HEADS = 8
KV_GROUPS = 2
Q_PER_G = NSA_HEADS // KV_GROUPS
HEAD_DIM = 64
NSA_WIDTH = NSA_HEADS * HEAD_DIM
KV_W = KV_GROUPS * HEAD_DIM
ROT_DIM = HEAD_DIM // 4
ROT_HALF = ROT_DIM // 2
ROPE_THETA = 500000.0
CMP_LEN = 32
CMP_STRIDE = 16
CMP_HIDDEN = 128
SEL_BLK = 64
SEL_TOPN = 16
WINDOW = 512
N_EXPERTS = 32
TOP_K = 4
D_FF = 1024
SWIGLU_ALPHA = 1.702
SWIGLU_LIMIT = 7.0
NORM_EPS = 1e-5
NEG_INF = -1e30
FORCE_SCORE = 1e9

LANES = 128
SUBLANES = 8
ROW_TILES = D_MODEL // LANES

GATE_PAD = LANES
IN_COLS_PAD = 2 * CONV_CH + NSA_WIDTH + 6 * KV_W + GATE_PAD

INPROJ_TM = 512
CONV_TR = 256
CONV_HALO = 32
ATTN_TQ = 128
ATTN_TK = 256
OUT_TM = 512
FFN_BM = 512
COMB_TM = 256
ROUTE_W = 8


def _rms(x, g):
    return x * lax.rsqrt(jnp.mean(x * x, axis=-1, keepdims=True) + NORM_EPS) * g


def _ada_kernel(c_ref, w_ref, b_ref, o_ref):
    c = c_ref[...]
    ca = c * jax.nn.sigmoid(c)
    o_ref[...] = jnp.dot(ca, w_ref[...], preferred_element_type=F32, precision=HI) + b_ref[...]


def _adaln(c, w, b):
    B = c.shape[0]
    D = D_MODEL
    return pl.pallas_call(
        _ada_kernel,
        grid=(6,),
        in_specs=[pl.BlockSpec((B, D), lambda j: (0, 0)),
                  pl.BlockSpec((D, D), lambda j: (0, j)),
                  pl.BlockSpec((1, D), lambda j: (0, j))],
        out_specs=pl.BlockSpec((B, D), lambda j: (0, j)),
        out_shape=jax.ShapeDtypeStruct((B, 6 * D), F32),
        name="adaln",
    )(c, w, b)


def _inproj_kernel(x_ref, sc_ref, sh_ref, g_ref, w_ref, rc_ref, rs1_ref, rs2_ref,
                   u_ref, q_ref, kc_ref, vc_ref, ks_ref, vs_ref, kw_ref, vw_ref, gt_ref):
    h = _rms(x_ref[0], g_ref[...]) * (1.0 + sc_ref[0]) + sh_ref[0]
    p = jnp.dot(h.astype(BF16), w_ref[...], preferred_element_type=F32)
    c0 = 0
    a = p[:, c0:c0 + CONV_CH]
    g = p[:, c0 + CONV_CH:c0 + 2 * CONV_CH]
    u_ref[0] = a * jax.nn.sigmoid(g)
    c0 = 2 * CONV_CH

    rc, rs1, rs2 = rc_ref[...], rs1_ref[...], rs2_ref[...]

    def rope(v):
        w = v.shape[1]
        n = w // LANES
        tile = lambda t: t if n == 1 else jnp.concatenate([t] * n, axis=1)
        return (v * tile(rc) + pltpu.roll(v, ROT_HALF, 1) * tile(rs1)
                + pltpu.roll(v, w - ROT_HALF, 1) * tile(rs2))

    q = rope(p[:, c0:c0 + NSA_WIDTH]) * (HEAD_DIM ** -0.5)
    for hh in range(NSA_HEADS):
        q_ref[0, hh] = q[:, HEAD_DIM * hh:HEAD_DIM * (hh + 1)].astype(BF16)
    c0 += NSA_WIDTH
    outs = ((kc_ref, True), (ks_ref, True), (kw_ref, True), (vc_ref, False), (vs_ref, False), (vw_ref, False))
    for ref, roped in outs:
        v = p[:, c0:c0 + KV_W]
        if roped:
            v = rope(v)
        for gg in range(KV_GROUPS):
            ref[0, gg] = v[:, HEAD_DIM * gg:HEAD_DIM * (gg + 1)].astype(ref.dtype)
        c0 += KV_W
    sg = jax.nn.sigmoid(p[:, c0:c0 + GATE_PAD])
    per_g = 3 * Q_PER_G
    for gg in range(KV_GROUPS):
        gt_ref[0, gg] = sg[:, per_g * gg:per_g * (gg + 1)]


def _inproj(x, sc, sh, g, w, rc, rs1, rs2):
    B, S, D = x.shape
    tm = min(INPROJ_TM, S)
    kv = lambda dt: jax.ShapeDtypeStruct((B, KV_GROUPS, S, HEAD_DIM), dt)
    kv_spec = pl.BlockSpec((1, KV_GROUPS, tm, HEAD_DIM), lambda b, i: (b, 0, i, 0))
    row = pl.BlockSpec((1, 1, D), lambda b, i: (b, 0, 0))
    tab = pl.BlockSpec((tm, LANES), lambda b, i: (i, 0))
    return pl.pallas_call(
        _inproj_kernel,
        grid=(B, S // tm),
        in_specs=[pl.BlockSpec((1, tm, D), lambda b, i: (b, i, 0)), row, row,
                  pl.BlockSpec((1, D), lambda b, i: (0, 0)),
                  pl.BlockSpec((D, IN_COLS_PAD), lambda b, i: (0, 0)),
                  tab, tab, tab],
        out_specs=[pl.BlockSpec((1, tm, CONV_CH), lambda b, i: (b, i, 0)),
                   pl.BlockSpec((1, NSA_HEADS, tm, HEAD_DIM), lambda b, i: (b, 0, i, 0)),
                   kv_spec, kv_spec, kv_spec, kv_spec, kv_spec, kv_spec,
                   pl.BlockSpec((1, KV_GROUPS, tm, 3 * Q_PER_G), lambda b, i: (b, 0, i, 0))],
        out_shape=[jax.ShapeDtypeStruct((B, S, CONV_CH), F32),
                   jax.ShapeDtypeStruct((B, NSA_HEADS, S, HEAD_DIM), BF16),
                   kv(F32), kv(F32), kv(BF16), kv(BF16), kv(BF16), kv(BF16),
                   jax.ShapeDtypeStruct((B, KV_GROUPS, S, 3 * Q_PER_G), F32)],
        compiler_params=pltpu.CompilerParams(dimension_semantics=("parallel", "parallel")),
        name="inproj",
    )(x, sc, sh, g, w, rc, rs1, rs2)


def _conv_kernel(prev_ref, cur_ref, w_ref, cb_ref, lg_ref, lb_ref, on_ref, o_ref, pad_ref):
    tr = cur_ref.shape[1]
    first = pl.program_id(1) == 0
    halo = prev_ref[0, tr - CONV_HALO:tr, :]
    pad_ref[0:CONV_HALO, :] = jnp.where(first, 0.0, halo)
    pad_ref[CONV_HALO:CONV_HALO + tr, :] = cur_ref[0]
    off = CONV_HALO - (CONV_WIDTH - 1)
    acc = jnp.zeros((tr, CONV_CH), F32)
    for k in range(CONV_WIDTH):
        acc = acc + pad_ref[off + k:off + k + tr, :] * w_ref[k:k + 1, :]
    y = acc + cb_ref[...]
    mu = jnp.mean(y, axis=-1, keepdims=True)
    yc = y - mu
    var = jnp.mean(yc * yc, axis=-1, keepdims=True)
    yn = yc * lax.rsqrt(var + NORM_EPS) * lg_ref[...] + lb_ref[...]
    s = yn * jax.nn.sigmoid(yn)
    o_ref[0] = _rms(s, on_ref[...]).astype(o_ref.dtype)


def _conv(u, w, cb, lg, lb, on):
    B, S, C = u.shape
    tr = min(CONV_TR, S)
    vec = pl.BlockSpec((1, C), lambda b, i: (0, 0))
    return pl.pallas_call(
        _conv_kernel,
        grid=(B, S // tr),
        in_specs=[pl.BlockSpec((1, tr, C), lambda b, i: (b, jnp.maximum(i - 1, 0), 0)),
                  pl.BlockSpec((1, tr, C), lambda b, i: (b, i, 0)),
                  pl.BlockSpec((CONV_WIDTH, C), lambda b, i: (0, 0)),
                  vec, vec, vec, vec],
        out_specs=pl.BlockSpec((1, tr, C), lambda b, i: (b, i, 0)),
        out_shape=jax.ShapeDtypeStruct((B, S, C), BF16),
        scratch_shapes=[pltpu.VMEM((CONV_HALO + tr, C), F32)],
        compiler_params=pltpu.CompilerParams(dimension_semantics=("parallel", "parallel")),
        name="conv",
    )(u, u, w, cb, lg, lb, on)


def _cmp_kernel(kx_ref, vx_ref, pek_ref, pev_ref, kw1_ref, kw2_ref, vw1_ref, vw2_ref, ko_ref, vo_ref):
    nc = kx_ref.shape[2]
    half = kx_ref.shape[3]
    for x_ref, pe_ref, w1_ref, w2_ref, o_ref in ((kx_ref, pek_ref, kw1_ref, kw2_ref, ko_ref),
                                                 (vx_ref, pev_ref, vw1_ref, vw2_ref, vo_ref)):
        w1 = w1_ref[...]
        pe = jnp.broadcast_to(pe_ref[...], (SUBLANES, 2 * half))
        pe_proj = jnp.dot(pe, w1, preferred_element_type=F32, precision=HI)[0:1]
        for gg in range(KV_GROUPS):
            xg = x_ref[0, gg]
            first = jnp.dot(xg, w1[0:half], preferred_element_type=F32, precision=HI)
            second = jnp.dot(xg, w1[half:2 * half], preferred_element_type=F32, precision=HI)
            hid = first + pltpu.roll(second, nc - 1, 0) + pe_proj
            hid = hid * jax.nn.sigmoid(hid)
            o_ref[0, gg] = jnp.dot(hid, w2_ref[...], preferred_element_type=F32, precision=HI)


def _compress(kx, vx, pek, pev, kw1, kw2, vw1, vw2):
    B, G, NC, W = kx.shape
    xs = pl.BlockSpec((1, G, NC, W), lambda b: (b, 0, 0, 0))
    full = lambda a: pl.BlockSpec(a.shape, lambda b: (0,) * a.ndim)
    os_ = pl.BlockSpec((1, G, NC, HEAD_DIM), lambda b: (b, 0, 0, 0))
    oshape = jax.ShapeDtypeStruct((B, G, NC, HEAD_DIM), F32)
    return pl.pallas_call(
        _cmp_kernel,
        grid=(B,),
        in_specs=[xs, xs, full(pek), full(pev), full(kw1), full(kw2), full(vw1), full(vw2)],
        out_specs=[os_, os_],
        out_shape=[oshape, oshape],
        compiler_params=pltpu.CompilerParams(dimension_semantics=("parallel",)),
        name="compress",
    )(kx, vx, pek, pev, kw1, kw2, vw1, vw2)


def _attn_kernel(q_ref, kc_ref, vc_ref, ks_ref, vs_ref, kw_ref, vw_ref, gt_ref, ov_ref, ex_ref,
                 o_ref, bias_ref):
    tq, tk = ATTN_TQ, ATTN_TK
    rows = Q_PER_G * tq
    S = ks_ref.shape[2]
    ncp = kc_ref.shape[2]
    nsel = ov_ref.shape[1]
    qi = pl.program_id(2)
    q0 = qi * tq
    q = q_ref[0].reshape(rows, HEAD_DIM)
    t_col = q0 + lax.broadcasted_iota(I32, (tq, 1), 0)
    nt = (((1,), (1,)), ((), ()))

    sc = lax.dot_general(q, kc_ref[0, 0].astype(BF16), nt, preferred_element_type=F32)
    sc = sc.reshape(Q_PER_G, tq, ncp)
    c_io = lax.broadcasted_iota(I32, (tq, ncp), 1)
    m_c = (c_io * CMP_STRIDE + (CMP_LEN - 1) <= t_col) & (c_io < ncp - 1)
    scm = jnp.where(m_c[None], sc, NEG_INF)
    mx = jnp.max(scm, axis=-1, keepdims=True)
    e = jnp.where(m_c[None], jnp.exp(scm - mx), 0.0)
    den = jnp.sum(e, axis=-1, keepdims=True)
    pc = e / jnp.where(den > 0.0, den, 1.0)
    o_cmp = jnp.dot(pc.reshape(rows, ncp).astype(BF16), vc_ref[0, 0].astype(BF16),
                    preferred_element_type=F32)

    imp = jnp.dot(jnp.sum(pc, axis=0), ov_ref[...], preferred_element_type=F32, precision=HI)
    j_io = lax.broadcasted_iota(I32, (tq, nsel), 1)
    cur = t_col // SEL_BLK
    valid = j_io * SEL_BLK <= t_col
    forced = (j_io == 0) | (j_io == cur) | (j_io == cur - 1)
    score = jnp.where(valid, jnp.where(forced, FORCE_SCORE, imp), NEG_INF)
    rank = jnp.zeros((tq, nsel), F32)
    for i in range(nsel):
        col = score[:, i:i + 1]
        beats = (col > score) | ((col == score) & (j_io > i))
        rank = rank + jnp.where(beats, 1.0, 0.0)
    sel = jnp.where(rank < float(min(SEL_TOPN, nsel)), 1.0, 0.0).astype(BF16)
    selexp = jnp.dot(sel, ex_ref[...], preferred_element_type=F32)
    for jj in range(S // tk):
        bias_ref[jj] = (selexp[:, jj * tk:(jj + 1) * tk] - 1.0) * (-NEG_INF)

    k_io = lax.broadcasted_iota(I32, (tq, tk), 1)

    def flash_step(k_ref, v_ref, kj, bias, carry):
        m, l, acc = carry
        k0 = pl.multiple_of(kj * tk, tk)
        s = lax.dot_general(q, k_ref[0, 0, pl.ds(k0, tk), :], nt, preferred_element_type=F32)
        s = s.reshape(Q_PER_G, tq, tk) + bias[None]
        m_new = jnp.maximum(m, jnp.max(s, axis=-1, keepdims=True))
        alpha = jnp.exp(m - m_new)
        p = jnp.exp(s - m_new)
        l = alpha * l + jnp.sum(p, axis=-1, keepdims=True)
        pv = jnp.dot(p.reshape(rows, tk).astype(BF16), v_ref[0, 0, pl.ds(k0, tk), :],
                     preferred_element_type=F32)
        acc = alpha * acc + pv.reshape(Q_PER_G, tq, HEAD_DIM)
        return m_new, l, acc

    init = (jnp.full((Q_PER_G, tq, 1), NEG_INF, F32), jnp.zeros((Q_PER_G, tq, 1), F32),
            jnp.zeros((Q_PER_G, tq, HEAD_DIM), F32))

    def slc_body(kj, carry):
        kpos = kj * tk + k_io
        bias = jnp.where(kpos <= t_col, bias_ref[kj], NEG_INF)
        return flash_step(ks_ref, vs_ref, kj, bias, carry)

    n_slc = (q0 + tq + tk - 1) // tk
    _, l_s, acc_s = lax.fori_loop(0, n_slc, slc_body, init)
    o_slc = acc_s / l_s

    def win_body(kj, carry):
        kpos = kj * tk + k_io
        rel = t_col - kpos
        bias = jnp.where((rel >= 0) & (rel < WINDOW), 0.0, NEG_INF)
        return flash_step(kw_ref, vw_ref, kj, bias, carry)

    lo_tile = jnp.maximum(q0 - (WINDOW - 1), 0) // tk
    _, l_w, acc_w = lax.fori_loop(lo_tile, n_slc, win_body, init)
    o_win = acc_w / l_w

    o_cmp = o_cmp.reshape(Q_PER_G, tq, HEAD_DIM)
    gt = gt_ref[0, 0]
    outs = []
    for n in range(Q_PER_G):
        outs.append(gt[:, 3 * n:3 * n + 1] * o_cmp[n] + gt[:, 3 * n + 1:3 * n + 2] * o_slc[n]
                    + gt[:, 3 * n + 2:3 * n + 3] * o_win[n])
    o_ref[0] = jnp.concatenate(outs, axis=1)


def _attention(q, kc, vc, ks, vs, kw, vw, gt, overlap, expand):
    B, H, S, _ = q.shape
    G = KV_GROUPS
    tq, tk = ATTN_TQ, ATTN_TK
    ncp = kc.shape[2]
    kvfull = pl.BlockSpec((1, 1, S, HEAD_DIM), lambda b, g, i: (b, g, 0, 0))
    cfull = pl.BlockSpec((1, 1, ncp, HEAD_DIM), lambda b, g, i: (b, g, 0, 0))
    return pl.pallas_call(
        _attn_kernel,
        grid=(B, G, S // tq),
        in_specs=[pl.BlockSpec((1, Q_PER_G, tq, HEAD_DIM), lambda b, g, i: (b, g, i, 0)),
                  cfull, cfull, kvfull, kvfull, kvfull, kvfull,
                  pl.BlockSpec((1, 1, tq, 3 * Q_PER_G), lambda b, g, i: (b, g, i, 0)),
                  pl.BlockSpec(overlap.shape, lambda b, g, i: (0, 0)),
                  pl.BlockSpec(expand.shape, lambda b, g, i: (0, 0))],
        out_specs=pl.BlockSpec((1, tq, Q_PER_G * HEAD_DIM), lambda b, g, i: (b, i, g)),
        out_shape=jax.ShapeDtypeStruct((B, S, NSA_WIDTH), F32),
        scratch_shapes=[pltpu.VMEM((S // tk, tq, tk), F32)],
        compiler_params=pltpu.CompilerParams(dimension_semantics=("parallel", "parallel", "arbitrary")),
        name="attn",
    )(q, kc, vc, ks, vs, kw, vw, gt, overlap, expand)


def _outproj_kernel(x_ref, cv_ref, nsa_ref, on_ref, w_ref, gt1_ref, g2_ref, sc2_ref, sh2_ref, wr_ref, br_ref,
                    x1_ref, h2_ref, rt_ref):
    tm = x_ref.shape[1]
    nn = _rms(nsa_ref[0], on_ref[...]).astype(BF16)
    y = (jnp.dot(cv_ref[0], w_ref[0:CONV_CH, :], preferred_element_type=F32)
         + jnp.dot(nn, w_ref[CONV_CH:CONV_CH + NSA_WIDTH, :], preferred_element_type=F32))
    x1 = x_ref[0] + gt1_ref[0] * y
    x1_ref[0] = x1
    h2 = _rms(x1, g2_ref[...]) * (1.0 + sc2_ref[0]) + sh2_ref[0]
    for s in range(ROW_TILES):
        h2_ref[pl.ds(s, tm, stride=ROW_TILES), :] = h2[:, s * LANES:(s + 1) * LANES]
    logits = jnp.dot(h2, wr_ref[...], preferred_element_type=F32, precision=HI) + br_ref[...]
    lane = lax.broadcasted_iota(I32, (tm, N_EXPERTS), 1).astype(F32)
    vals, idxs = [], []
    for _ in range(TOP_K):
        m = jnp.max(logits, axis=-1, keepdims=True)
        ix = jnp.min(jnp.where(logits == m, lane, float(N_EXPERTS)), axis=-1, keepdims=True)
        vals.append(m)
        idxs.append(ix)
        logits = jnp.where(lane == ix, -jnp.inf, logits)
    es = [jnp.exp(v - vals[0]) for v in vals]
    den = es[0] + es[1] + es[2] + es[3]
    ol = lax.broadcasted_iota(I32, (tm, ROUTE_W), 1)
    out = jnp.zeros((tm, ROUTE_W), F32)
    for r in range(TOP_K):
        out = jnp.where(ol == r, idxs[r], out)
        out = jnp.where(ol == TOP_K + r, es[r] / den, out)
    rt_ref[0] = out


def _outproj(x, cv, nsa, on, w, gt1, g2, sc2, sh2, wr, br):
    B, S, D = x.shape
    tm = min(OUT_TM, S)
    nt = S // tm
    row = pl.BlockSpec((1, 1, D), lambda b, i: (b, 0, 0))
    vec = lambda n: pl.BlockSpec((1, n), lambda b, i: (0, 0))
    return pl.pallas_call(
        _outproj_kernel,
        grid=(B, nt),
        in_specs=[pl.BlockSpec((1, tm, D), lambda b, i: (b, i, 0)),
                  pl.BlockSpec((1, tm, CONV_CH), lambda b, i: (b, i, 0)),
                  pl.BlockSpec((1, tm, NSA_WIDTH), lambda b, i: (b, i, 0)),
                  vec(NSA_WIDTH),
                  pl.BlockSpec((D, D), lambda b, i: (0, 0)),
                  row, vec(D), row, row,
                  pl.BlockSpec((D, N_EXPERTS), lambda b, i: (0, 0)), vec(N_EXPERTS)],
        out_specs=[pl.BlockSpec((1, tm, D), lambda b, i: (b, i, 0)),
                   pl.BlockSpec((tm * ROW_TILES, LANES), lambda b, i: (b * nt + i, 0)),
                   pl.BlockSpec((1, tm, ROUTE_W), lambda b, i: (b, i, 0))],
        out_shape=[jax.ShapeDtypeStruct((B, S, D), F32),
                   jax.ShapeDtypeStruct((B * S * ROW_TILES, LANES), F32),
                   jax.ShapeDtypeStruct((B, S, ROUTE_W), F32)],
        compiler_params=pltpu.CompilerParams(dimension_semantics=("parallel", "parallel")),
        name="outproj",
    )(x, cv, nsa, on, w, gt1, g2, sc2, sh2, wr, br)


def _issue_rows(idx_ref, n, src_hbm, dst, slot, sem):
    def body(r, c):
        pltpu.make_async_copy(src_hbm.at[idx_ref[0, 0, r]],
                              dst.at[slot, pl.ds(pl.multiple_of(r * ROW_TILES, ROW_TILES), ROW_TILES), :],
                              sem.at[slot]).start()
        return c
    lax.fori_loop(0, n, body, 0)


def _wait_rows(dst, slot, sem):
    pltpu.make_async_copy(dst.at[slot], dst.at[slot], sem.at[slot]).wait()


def _rows_2d(buf, slot, base, n):
    return jnp.concatenate(
        [buf[slot, pl.ds(base * ROW_TILES + s, n, stride=ROW_TILES), :] for s in range(ROW_TILES)], axis=1)


def _ffn_kernel(be_ref, nb_ref, ta_ref, tb_ref, h2_hbm, gate_ref, wg_ref, bg_ref, wu_ref, bu_ref, wd_ref, bd_ref,
                o_ref, xbuf, sem):
    bm = gate_ref.shape[0]
    i = pl.program_id(0)
    nb = nb_ref[0]
    slot = i % 2

    @pl.when(i == 0)
    def _():
        _issue_rows(ta_ref, bm, h2_hbm, xbuf, 0, sem)

    @pl.when(i + 1 < nb)
    def _():
        _issue_rows(tb_ref, bm, h2_hbm, xbuf, 1 - slot, sem)

    @pl.when(i < nb)
    def _():
        _wait_rows(xbuf, slot, sem)
        x = _rows_2d(xbuf, slot, 0, bm).astype(BF16)
        g = jnp.dot(x, wg_ref[0], preferred_element_type=F32) + bg_ref[0]
        u = jnp.dot(x, wu_ref[0], preferred_element_type=F32) + bu_ref[0]
        g = jnp.minimum(g, SWIGLU_LIMIT)
        u = jnp.clip(u, -SWIGLU_LIMIT, SWIGLU_LIMIT)
        act = g * jax.nn.sigmoid(SWIGLU_ALPHA * g) * (u + 1.0)
        y = jnp.dot(act.astype(BF16), wd_ref[0], preferred_element_type=F32) + bd_ref[0]
        y = y * gate_ref[...]
        for s in range(ROW_TILES):
            o_ref[pl.ds(s, bm, stride=ROW_TILES), :] = y[:, s * LANES:(s + 1) * LANES]

    @pl.when(i >= nb)
    def _():
        o_ref[...] = jnp.zeros(o_ref.shape, o_ref.dtype)


def _ffn(block_e, nb_used, buf_tok, h2_rows, buf_gate, wg, bg, wu, bu, wd, bd):
    NB = block_e.shape[0]
    bm = FFN_BM
    D, F = D_MODEL, D_FF
    tok3 = buf_tok.reshape(NB, 1, bm)
    wspec = lambda r, c: pl.BlockSpec((1, r, c), lambda i, be, nb: (be[i], 0, 0))
    return pl.pallas_call(
        _ffn_kernel,
        grid_spec=pltpu.PrefetchScalarGridSpec(
            num_scalar_prefetch=2,
            grid=(NB,),
            in_specs=[pl.BlockSpec((1, 1, bm), lambda i, be, nb: (i, 0, 0), memory_space=pltpu.SMEM),
                      pl.BlockSpec((1, 1, bm), lambda i, be, nb: (jnp.minimum(i + 1, NB - 1), 0, 0),
                                   memory_space=pltpu.SMEM),
                      pl.BlockSpec(memory_space=pl.ANY),
                      pl.BlockSpec((bm, 1), lambda i, be, nb: (i, 0)),
                      wspec(D, F), wspec(1, F), wspec(D, F), wspec(1, F), wspec(F, D), wspec(1, D)],
            out_specs=pl.BlockSpec((bm * ROW_TILES, LANES), lambda i, be, nb: (i, 0)),
            scratch_shapes=[pltpu.VMEM((2, bm * ROW_TILES, LANES), F32), pltpu.SemaphoreType.DMA((2,))]),
        out_shape=jax.ShapeDtypeStruct((NB * bm * ROW_TILES, LANES), F32),
        compiler_params=pltpu.CompilerParams(dimension_semantics=("arbitrary",)),
        name="ffn",
    )(block_e, nb_used, tok3, tok3, h2_rows, buf_gate, wg, bg, wu, bu, wd, bd)


def _combine_kernel(da_ref, db_ref, y_hbm, x1_ref, gt2_ref, fg_ref, o_ref, buf, sem):
    tm = x1_ref.shape[0]
    n = TOP_K * tm
    i = pl.program_id(0)
    slot = i % 2

    @pl.when(i == 0)
    def _():
        _issue_rows(da_ref, n, y_hbm, buf, 0, sem)

    @pl.when(i + 1 < pl.num_programs(0))
    def _():
        _issue_rows(db_ref, n, y_hbm, buf, 1 - slot, sem)

    _wait_rows(buf, slot, sem)
    y = _rows_2d(buf, slot, 0, tm)
    for k in range(1, TOP_K):
        y = y + _rows_2d(buf, slot, k * tm, tm)
    x2 = x1_ref[...] + gt2_ref[0] * y
    o_ref[...] = _rms(x2, fg_ref[...])


def _combine(dest3, y_rows, x1, gt2, fg, S):
    T, D = x1.shape
    tm = min(COMB_TM, S)
    NT = T // tm
    per_b = S // tm
    n = TOP_K * tm
    return pl.pallas_call(
        _combine_kernel,
        grid=(NT,),
        in_specs=[pl.BlockSpec((1, 1, n), lambda i: (i, 0, 0), memory_space=pltpu.SMEM),
                  pl.BlockSpec((1, 1, n), lambda i: (jnp.minimum(i + 1, NT - 1), 0, 0), memory_space=pltpu.SMEM),
                  pl.BlockSpec(memory_space=pl.ANY),
                  pl.BlockSpec((tm, D), lambda i: (i, 0)),
                  pl.BlockSpec((1, 1, D), lambda i: (i // per_b, 0, 0)),
                  pl.BlockSpec((1, D), lambda i: (0, 0))],
        out_specs=pl.BlockSpec((tm, D), lambda i: (i, 0)),
        out_shape=jax.ShapeDtypeStruct((T, D), F32),
        scratch_shapes=[pltpu.VMEM((2, n * ROW_TILES, LANES), F32), pltpu.SemaphoreType.DMA((2,))],
        compiler_params=pltpu.CompilerParams(dimension_semantics=("arbitrary",)),
        name="combine",
    )(dest3, dest3, y_rows, x1, gt2, fg)


def _rope_lane_tables(S):
    inv = ROPE_THETA ** (-jnp.arange(0, ROT_DIM, 2, dtype=F32) / ROT_DIM)
    ang = jnp.arange(S, dtype=F32)[:, None] * inv[None, :]
    cos, sin = jnp.cos(ang), jnp.sin(ang)
    d = jnp.arange(LANES) % HEAD_DIM
    first, second = d < ROT_HALF, (d >= ROT_HALF) & (d < ROT_DIM)
    cos_l = cos[:, d % ROT_HALF]
    sin_l = sin[:, d % ROT_HALF]
    rc = jnp.where((d < ROT_DIM)[None], cos_l, 1.0)
    rs1 = jnp.where(second[None], sin_l, 0.0)
    rs2 = jnp.where(first[None], -sin_l, 0.0)
    return rc, rs1, rs2


def _route_plan(route, T):
    bm = FFN_BM
    A = T * TOP_K
    flat_e = route[:, :TOP_K].astype(I32).reshape(A)
    flat_g = route[:, TOP_K:2 * TOP_K].reshape(A)
    order = jnp.argsort(flat_e, stable=True).astype(I32)
    inv = jnp.argsort(order).astype(I32)
    eids = jnp.arange(N_EXPERTS, dtype=I32)
    counts = jnp.sum((flat_e[:, None] == eids[None, :]).astype(I32), axis=0)
    starts = jnp.cumsum(counts) - counts
    padded = (counts + bm - 1) // bm * bm
    pends = jnp.cumsum(padded)
    pstarts = pends - padded
    P = (A + N_EXPERTS * bm + bm - 1) // bm * bm
    NB = P // bm
    blk0 = jnp.arange(NB, dtype=I32) * bm
    block_e = jnp.minimum(jnp.sum((pends[None, :] <= blk0[:, None]).astype(I32), axis=1), N_EXPERTS - 1)
    e_p = jnp.repeat(block_e, bm)
    r = jnp.arange(P, dtype=I32) - pstarts[e_p]
    valid = r < counts[e_p]
    a_p = order[jnp.clip(starts[e_p] + r, 0, A - 1)]
    buf_tok = jnp.where(valid, a_p // TOP_K, 0)
    buf_gate = jnp.where(valid, flat_g[a_p], 0.0)
    dest_assign = pstarts[flat_e] + inv - starts[flat_e]
    nb_used = (pends[-1] // bm).astype(I32).reshape(1)
    return block_e, nb_used, buf_tok, buf_gate.reshape(P, 1), dest_assign.reshape(T, TOP_K)


def kernel(x, c, norm1_g, norm2_g, w_ada, b_ada, w_in, conv_w, conv_b, conv_ln_g, conv_ln_b, cmp_pe_k, cmp_pe_v,
           cmp_k_w1, cmp_k_w2, cmp_v_w1, cmp_v_w2, out_norm_conv, out_norm_nsa, w_out, w_router, b_router,
           w_gate, b_gate, w_up, b_up, w_down, b_down, final_norm_g):
    B, S, D = x.shape
    T = B * S
    G = KV_GROUPS
    assert D == D_MODEL and S % ATTN_TK == 0 and S % CMP_STRIDE == 0
    rc, rs1, rs2 = _rope_lane_tables(S)
    n_sel = S // SEL_BLK
    nc = S // CMP_STRIDE
    cstart = jnp.arange(nc) * CMP_STRIDE
    jstart = jnp.arange(n_sel) * SEL_BLK
    overlap = ((cstart[:, None] <= jstart[None, :] + SEL_BLK - 1)
               & (cstart[:, None] + CMP_LEN - 1 >= jstart[None, :])
               & (jnp.arange(nc)[:, None] < nc - 1)).astype(F32)
    expand = (jnp.arange(S)[None, :] // SEL_BLK == jnp.arange(n_sel)[:, None]).astype(BF16)

    assert w_ada.shape[0] == 1
    for l in range(1):
        mod = _adaln(c, w_ada[l], b_ada[l][None])
        sh1, sc1, gt1, sh2, sc2, gt2 = [m[:, None, :] for m in jnp.split(mod, 6, axis=-1)]

        o = 2 * CONV_CH + NSA_WIDTH
        wl = w_in[l]
        kvc = [wl[:, o + i * KV_W:o + (i + 1) * KV_W] for i in range(6)]
        gl = wl[:, o + 6 * KV_W:]
        w_perm = jnp.concatenate([wl[:, :o], kvc[0], kvc[2], kvc[4], kvc[1], kvc[3], kvc[5],
                                  jnp.pad(gl, ((0, 0), (0, GATE_PAD - gl.shape[1])))], axis=1).astype(BF16)
        u, q, kc, vc, ks, vs, kw, vw, gates = _inproj(x, sc1, sh1, norm1_g[l][None], w_perm, rc, rs1, rs2)

        conv_n = _conv(u, conv_w[l], conv_b[l][None], conv_ln_g[l][None], conv_ln_b[l][None],
                       out_norm_conv[l][None])

        chunk = CMP_STRIDE * HEAD_DIM
        kcmp, vcmp = _compress(kc.reshape(B, G, nc, chunk), vc.reshape(B, G, nc, chunk),
                               cmp_pe_k[l].reshape(1, -1), cmp_pe_v[l].reshape(1, -1),
                               cmp_k_w1[l], cmp_k_w2[l], cmp_v_w1[l], cmp_v_w2[l])
        nsa = _attention(q, kcmp, vcmp, ks, vs, kw, vw, gates, overlap, expand)

        x1, h2_rows, route = _outproj(x, conv_n, nsa, out_norm_nsa[l][None], w_out[l].astype(BF16), gt1,
                                      norm2_g[l][None], sc2, sh2, w_router[l], b_router[l][None])

        block_e, nb_used, buf_tok, buf_gate, dest_assign = _route_plan(route.reshape(T, ROUTE_W), T)
        y_rows = _ffn(block_e, nb_used, buf_tok, h2_rows.reshape(T, ROW_TILES, LANES), buf_gate,
                      w_gate[l].astype(BF16), b_gate[l][:, None, :], w_up[l].astype(BF16), b_up[l][:, None, :],
                      w_down[l].astype(BF16), b_down[l][:, None, :])
        tm = min(COMB_TM, S)
        dest3 = dest_assign.reshape(T // tm, tm, TOP_K).transpose(0, 2, 1).reshape(T // tm, 1, TOP_K * tm)
        P = y_rows.shape[0] // ROW_TILES
        x = _combine(dest3, y_rows.reshape(P, ROW_TILES, LANES), x1.reshape(T, D), gt2, final_norm_g[None],
                     S).reshape(B, S, D)
    return x
```

```python
import functools

import jax
import jax.numpy as jnp
from jax import lax
from jax.experimental import pallas as pl
from jax.experimental.pallas import tpu as pltpu

F32 = jnp.float32
BF16 = jnp.bfloat16
I32 = jnp.int32
HI = lax.Precision.HIGHEST

D_MODEL = 1024
CONV_CH = 512
CONV_WIDTH = 31
NSA_HEADS = 8
KV_GROUPS = 2
Q_PER_G = NSA_HEADS // KV_GROUPS
HEAD_DIM = 64
NSA_WIDTH = NSA_HEADS * HEAD_DIM
KV_W = KV_GROUPS * HEAD_DIM
ROT_DIM = HEAD_DIM // 4
ROT_HALF = ROT_DIM // 2
ROPE_THETA = 500000.0
CMP_LEN = 32
CMP_STRIDE = 16
CMP_HIDDEN = 128
SEL_BLK = 64
SEL_TOPN = 16
WINDOW = 512
N_EXPERTS = 32
TOP_K = 4
D_FF = 1024
SWIGLU_ALPHA = 1.702
SWIGLU_LIMIT = 7.0
NORM_EPS = 1e-5
NEG_INF = -1e30
FORCE_SCORE = 1e9

LANES = 128
SUBLANES = 8
ROW_TILES = D_MODEL // LANES

GATE_PAD = LANES
IN_COLS_PAD = 2 * CONV_CH + NSA_WIDTH + 6 * KV_W + GATE_PAD

INPROJ_TM = 512
CONV_TR = 256
CONV_HALO = 32
ATTN_TQ = 128
ATTN_TK = 256
OUT_TM = 512
FFN_BM = 512
COMB_TM = 256
ROUTE_W = 16
ISSUE_UNROLL = 8


def _rms(x, g):
    return x * lax.rsqrt(jnp.mean(x * x, axis=-1, keepdims=True) + NORM_EPS) * g


def _ada_kernel(c_ref, w_ref, b_ref, o_ref):
    c = c_ref[...]
    ca = c * jax.nn.sigmoid(c)
    o_ref[...] = jnp.dot(ca, w_ref[...], preferred_element_type=F32, precision=HI) + b_ref[...]


def _adaln(c, w, b):
    B = c.shape[0]
    D = D_MODEL
    return pl.pallas_call(
        _ada_kernel,
        grid=(6,),
        in_specs=[pl.BlockSpec((B, D), lambda j: (0, 0)),
                  pl.BlockSpec((D, D), lambda j: (0, j)),
                  pl.BlockSpec((1, D), lambda j: (0, j))],
        out_specs=pl.BlockSpec((B, D), lambda j: (0, j)),
        out_shape=jax.ShapeDtypeStruct((B, 6 * D), F32),
        name="adaln",
    )(c, w, b)


def _inproj_kernel(x_ref, sc_ref, sh_ref, g_ref, w_ref, rc_ref, rs1_ref, rs2_ref,
                   u_ref, q_ref, kc_ref, vc_ref, ks_ref, vs_ref, kw_ref, vw_ref, gt_ref):
    h = _rms(x_ref[0], g_ref[...]) * (1.0 + sc_ref[0]) + sh_ref[0]
    p = jnp.dot(h.astype(BF16), w_ref[...], preferred_element_type=F32)
    c0 = 0
    a = p[:, c0:c0 + CONV_CH]
    g = p[:, c0 + CONV_CH:c0 + 2 * CONV_CH]
    u_ref[0] = a * jax.nn.sigmoid(g)
    c0 = 2 * CONV_CH

    rc, rs1, rs2 = rc_ref[...], rs1_ref[...], rs2_ref[...]

    def rope(v):
        w = v.shape[1]
        n = w // LANES
        tile = lambda t: t if n == 1 else jnp.concatenate([t] * n, axis=1)
        return (v * tile(rc) + pltpu.roll(v, ROT_HALF, 1) * tile(rs1)
                + pltpu.roll(v, w - ROT_HALF, 1) * tile(rs2))

    q = rope(p[:, c0:c0 + NSA_WIDTH]) * (HEAD_DIM ** -0.5)
    for hh in range(NSA_HEADS):
        q_ref[0, hh] = q[:, HEAD_DIM * hh:HEAD_DIM * (hh + 1)].astype(BF16)
    c0 += NSA_WIDTH
    outs = ((kc_ref, True), (ks_ref, True), (kw_ref, True), (vc_ref, False), (vs_ref, False), (vw_ref, False))
    for ref, roped in outs:
        v = p[:, c0:c0 + KV_W]
        if roped:
            v = rope(v)
        for gg in range(KV_GROUPS):
            ref[0, gg] = v[:, HEAD_DIM * gg:HEAD_DIM * (gg + 1)].astype(ref.dtype)
        c0 += KV_W
    sg = jax.nn.sigmoid(p[:, c0:c0 + GATE_PAD])
    per_g = 3 * Q_PER_G
    for gg in range(KV_GROUPS):
        gt_ref[0, gg] = sg[:, per_g * gg:per_g * (gg + 1)]


def _inproj(x, sc, sh, g, w, rc, rs1, rs2):
    B, S, D = x.shape
    tm = min(INPROJ_TM, S)
    kv = lambda dt: jax.ShapeDtypeStruct((B, KV_GROUPS, S, HEAD_DIM), dt)
    kv_spec = pl.BlockSpec((1, KV_GROUPS, tm, HEAD_DIM), lambda b, i: (b, 0, i, 0))
    row = pl.BlockSpec((1, 1, D), lambda b, i: (b, 0, 0))
    tab = pl.BlockSpec((tm, LANES), lambda b, i: (i, 0))
    return pl.pallas_call(
        _inproj_kernel,
        grid=(B, S // tm),
        in_specs=[pl.BlockSpec((1, tm, D), lambda b, i: (b, i, 0)), row, row,
                  pl.BlockSpec((1, D), lambda b, i: (0, 0)),
                  pl.BlockSpec((D, IN_COLS_PAD), lambda b, i: (0, 0)),
                  tab, tab, tab],
        out_specs=[pl.BlockSpec((1, tm, CONV_CH), lambda b, i: (b, i, 0)),
                   pl.BlockSpec((1, NSA_HEADS, tm, HEAD_DIM), lambda b, i: (b, 0, i, 0)),
                   kv_spec, kv_spec, kv_spec, kv_spec, kv_spec, kv_spec,
                   pl.BlockSpec((1, KV_GROUPS, tm, 3 * Q_PER_G), lambda b, i: (b, 0, i, 0))],
        out_shape=[jax.ShapeDtypeStruct((B, S, CONV_CH), F32),
                   jax.ShapeDtypeStruct((B, NSA_HEADS, S, HEAD_DIM), BF16),
                   kv(F32), kv(F32), kv(BF16), kv(BF16), kv(BF16), kv(BF16),
                   jax.ShapeDtypeStruct((B, KV_GROUPS, S, 3 * Q_PER_G), F32)],
        compiler_params=pltpu.CompilerParams(dimension_semantics=("parallel", "parallel")),
        name="inproj",
    )(x, sc, sh, g, w, rc, rs1, rs2)


def _conv_kernel(prev_ref, cur_ref, w_ref, cb_ref, lg_ref, lb_ref, on_ref, o_ref, pad_ref):
    tr = cur_ref.shape[1]
    first = pl.program_id(1) == 0
    halo = prev_ref[0, tr - CONV_HALO:tr, :]
    pad_ref[0:CONV_HALO, :] = jnp.where(first, 0.0, halo)
    pad_ref[CONV_HALO:CONV_HALO + tr, :] = cur_ref[0]
    off = CONV_HALO - (CONV_WIDTH - 1)
    acc = jnp.zeros((tr, CONV_CH), F32)
    for k in range(CONV_WIDTH):
        acc = acc + pad_ref[off + k:off + k + tr, :] * w_ref[k:k + 1, :]
    y = acc + cb_ref[...]
    mu = jnp.mean(y, axis=-1, keepdims=True)
    yc = y - mu
    var = jnp.mean(yc * yc, axis=-1, keepdims=True)
    yn = yc * lax.rsqrt(var + NORM_EPS) * lg_ref[...] + lb_ref[...]
    s = yn * jax.nn.sigmoid(yn)
    o_ref[0] = _rms(s, on_ref[...]).astype(o_ref.dtype)


def _conv(u, w, cb, lg, lb, on):
    B, S, C = u.shape
    tr = min(CONV_TR, S)
    vec = pl.BlockSpec((1, C), lambda b, i: (0, 0))
    return pl.pallas_call(
        _conv_kernel,
        grid=(B, S // tr),
        in_specs=[pl.BlockSpec((1, tr, C), lambda b, i: (b, jnp.maximum(i - 1, 0), 0)),
                  pl.BlockSpec((1, tr, C), lambda b, i: (b, i, 0)),
                  pl.BlockSpec((CONV_WIDTH, C), lambda b, i: (0, 0)),
                  vec, vec, vec, vec],
        out_specs=pl.BlockSpec((1, tr, C), lambda b, i: (b, i, 0)),
        out_shape=jax.ShapeDtypeStruct((B, S, C), BF16),
        scratch_shapes=[pltpu.VMEM((CONV_HALO + tr, C), F32)],
        compiler_params=pltpu.CompilerParams(dimension_semantics=("parallel", "parallel")),
        name="conv",
    )(u, u, w, cb, lg, lb, on)


def _cmp_kernel(kx_ref, vx_ref, pek_ref, pev_ref, kw1_ref, kw2_ref, vw1_ref, vw2_ref, ko_ref, vo_ref):
    nc = kx_ref.shape[2]
    half = kx_ref.shape[3]
    for x_ref, pe_ref, w1_ref, w2_ref, o_ref in ((kx_ref, pek_ref, kw1_ref, kw2_ref, ko_ref),
                                                 (vx_ref, pev_ref, vw1_ref, vw2_ref, vo_ref)):
        w1 = w1_ref[...]
        pe = jnp.broadcast_to(pe_ref[...], (SUBLANES, 2 * half))
        pe_proj = jnp.dot(pe, w1, preferred_element_type=F32, precision=HI)[0:1]
        for gg in range(KV_GROUPS):
            xg = x_ref[0, gg]
            first = jnp.dot(xg, w1[0:half], preferred_element_type=F32, precision=HI)
            second = jnp.dot(xg, w1[half:2 * half], preferred_element_type=F32, precision=HI)
            hid = first + pltpu.roll(second, nc - 1, 0) + pe_proj
            hid = hid * jax.nn.sigmoid(hid)
            o_ref[0, gg] = jnp.dot(hid, w2_ref[...], preferred_element_type=F32, precision=HI)


def _compress(kx, vx, pek, pev, kw1, kw2, vw1, vw2):
    B, G, NC, W = kx.shape
    xs = pl.BlockSpec((1, G, NC, W), lambda b: (b, 0, 0, 0))
    full = lambda a: pl.BlockSpec(a.shape, lambda b: (0,) * a.ndim)
    os_ = pl.BlockSpec((1, G, NC, HEAD_DIM), lambda b: (b, 0, 0, 0))
    oshape = jax.ShapeDtypeStruct((B, G, NC, HEAD_DIM), F32)
    return pl.pallas_call(
        _cmp_kernel,
        grid=(B,),
        in_specs=[xs, xs, full(pek), full(pev), full(kw1), full(kw2), full(vw1), full(vw2)],
        out_specs=[os_, os_],
        out_shape=[oshape, oshape],
        compiler_params=pltpu.CompilerParams(dimension_semantics=("parallel",)),
        name="compress",
    )(kx, vx, pek, pev, kw1, kw2, vw1, vw2)


def _attn_kernel(q_ref, kc_ref, vc_ref, ks_ref, vs_ref, kw_ref, vw_ref, gt_ref, ov_ref, ex_ref,
                 o_ref, bias_ref):
    tq, tk = ATTN_TQ, ATTN_TK
    rows = Q_PER_G * tq
    S = ks_ref.shape[2]
    ncp = kc_ref.shape[2]
    nsel = ov_ref.shape[1]
    qi = pl.program_id(2)
    q0 = qi * tq
    q = q_ref[0].reshape(rows, HEAD_DIM)
    t_col = q0 + lax.broadcasted_iota(I32, (tq, 1), 0)
    nt = (((1,), (1,)), ((), ()))

    sc = lax.dot_general(q, kc_ref[0, 0].astype(BF16), nt, preferred_element_type=F32)
    sc = sc.reshape(Q_PER_G, tq, ncp)
    c_io = lax.broadcasted_iota(I32, (tq, ncp), 1)
    m_c = (c_io * CMP_STRIDE + (CMP_LEN - 1) <= t_col) & (c_io < ncp - 1)
    scm = jnp.where(m_c[None], sc, NEG_INF)
    mx = jnp.max(scm, axis=-1, keepdims=True)
    e = jnp.where(m_c[None], jnp.exp(scm - mx), 0.0)
    den = jnp.sum(e, axis=-1, keepdims=True)
    pc = e / jnp.where(den > 0.0, den, 1.0)
    o_cmp = jnp.dot(pc.reshape(rows, ncp).astype(BF16), vc_ref[0, 0].astype(BF16),
                    preferred_element_type=F32)

    imp = jnp.dot(jnp.sum(pc, axis=0), ov_ref[...], preferred_element_type=F32, precision=HI)
    j_io = lax.broadcasted_iota(I32, (tq, nsel), 1)
    cur = t_col // SEL_BLK
    valid = j_io * SEL_BLK <= t_col
    forced = (j_io == 0) | (j_io == cur) | (j_io == cur - 1)
    score = jnp.where(valid, jnp.where(forced, FORCE_SCORE, imp), NEG_INF)
    rank = jnp.zeros((tq, nsel), F32)
    for i in range(nsel):
        col = score[:, i:i + 1]
        beats = (col > score) | ((col == score) & (j_io > i))
        rank = rank + jnp.where(beats, 1.0, 0.0)
    sel = jnp.where(rank < float(min(SEL_TOPN, nsel)), 1.0, 0.0).astype(BF16)
    selexp = jnp.dot(sel, ex_ref[...], preferred_element_type=F32)
    for jj in range(S // tk):
        bias_ref[jj] = (selexp[:, jj * tk:(jj + 1) * tk] - 1.0) * (-NEG_INF)

    k_io = lax.broadcasted_iota(I32, (tq, tk), 1)

    def flash_step(k_ref, v_ref, kj, bias, carry):
        m, l, acc = carry
        k0 = pl.multiple_of(kj * tk, tk)
        s = lax.dot_general(q, k_ref[0, 0, pl.ds(k0, tk), :], nt, preferred_element_type=F32)
        s = s.reshape(Q_PER_G, tq, tk) + bias[None]
        m_new = jnp.maximum(m, jnp.max(s, axis=-1, keepdims=True))
        alpha = jnp.exp(m - m_new)
        p = jnp.exp(s - m_new)
        l = alpha * l + jnp.sum(p, axis=-1, keepdims=True)
        pv = jnp.dot(p.reshape(rows, tk).astype(BF16), v_ref[0, 0, pl.ds(k0, tk), :],
                     preferred_element_type=F32)
        acc = alpha * acc + pv.reshape(Q_PER_G, tq, HEAD_DIM)
        return m_new, l, acc

    init = (jnp.full((Q_PER_G, tq, 1), NEG_INF, F32), jnp.zeros((Q_PER_G, tq, 1), F32),
            jnp.zeros((Q_PER_G, tq, HEAD_DIM), F32))

    def slc_body(kj, carry):
        kpos = kj * tk + k_io
        bias = jnp.where(kpos <= t_col, bias_ref[kj], NEG_INF)
        return flash_step(ks_ref, vs_ref, kj, bias, carry)

    n_slc = (q0 + tq + tk - 1) // tk
    _, l_s, acc_s = lax.fori_loop(0, n_slc, slc_body, init)
    o_slc = acc_s / l_s

    def win_body(kj, carry):
        kpos = kj * tk + k_io
        rel = t_col - kpos
        bias = jnp.where((rel >= 0) & (rel < WINDOW), 0.0, NEG_INF)
        return flash_step(kw_ref, vw_ref, kj, bias, carry)

    lo_tile = jnp.maximum(q0 - (WINDOW - 1), 0) // tk
    _, l_w, acc_w = lax.fori_loop(lo_tile, n_slc, win_body, init)
    o_win = acc_w / l_w

    o_cmp = o_cmp.reshape(Q_PER_G, tq, HEAD_DIM)
    gt = gt_ref[0, 0]
    outs = []
    for n in range(Q_PER_G):
        outs.append(gt[:, 3 * n:3 * n + 1] * o_cmp[n] + gt[:, 3 * n + 1:3 * n + 2] * o_slc[n]
                    + gt[:, 3 * n + 2:3 * n + 3] * o_win[n])
    o_ref[0] = jnp.concatenate(outs, axis=1)


def _attention(q, kc, vc, ks, vs, kw, vw, gt, overlap, expand):
    B, H, S, _ = q.shape
    G = KV_GROUPS
    tq, tk = ATTN_TQ, ATTN_TK
    ncp = kc.shape[2]
    kvfull = pl.BlockSpec((1, 1, S, HEAD_DIM), lambda b, g, i: (b, g, 0, 0))
    cfull = pl.BlockSpec((1, 1, ncp, HEAD_DIM), lambda b, g, i: (b, g, 0, 0))
    return pl.pallas_call(
        _attn_kernel,
        grid=(B, G, S // tq),
        in_specs=[pl.BlockSpec((1, Q_PER_G, tq, HEAD_DIM), lambda b, g, i: (b, g, i, 0)),
                  cfull, cfull, kvfull, kvfull, kvfull, kvfull,
                  pl.BlockSpec((1, 1, tq, 3 * Q_PER_G), lambda b, g, i: (b, g, i, 0)),
                  pl.BlockSpec(overlap.shape, lambda b, g, i: (0, 0)),
                  pl.BlockSpec(expand.shape, lambda b, g, i: (0, 0))],
        out_specs=pl.BlockSpec((1, tq, Q_PER_G * HEAD_DIM), lambda b, g, i: (b, i, g)),
        out_shape=jax.ShapeDtypeStruct((B, S, NSA_WIDTH), F32),
        scratch_shapes=[pltpu.VMEM((S // tk, tq, tk), F32)],
        compiler_params=pltpu.CompilerParams(dimension_semantics=("parallel", "parallel", "arbitrary")),
        name="attn",
    )(q, kc, vc, ks, vs, kw, vw, gt, overlap, expand)


def _outproj_kernel(x_ref, cv_ref, nsa_ref, on_ref, w_ref, gt1_ref, g2_ref, sc2_ref, sh2_ref, wr_ref, br_ref, tri_ref,
                    x1_ref, h2_ref, rt_ref, cnt_ref, run_ref):
    tm = x_ref.shape[1]

    @pl.when((pl.program_id(0) == 0) & (pl.program_id(1) == 0))
    def _():
        run_ref[...] = jnp.zeros(run_ref.shape, F32)

    nn = _rms(nsa_ref[0], on_ref[...]).astype(BF16)
    y = (jnp.dot(cv_ref[0], w_ref[0:CONV_CH, :], preferred_element_type=F32)
         + jnp.dot(nn, w_ref[CONV_CH:CONV_CH + NSA_WIDTH, :], preferred_element_type=F32))
    x1 = x_ref[0] + gt1_ref[0] * y
    x1_ref[0] = x1
    h2 = _rms(x1, g2_ref[...]) * (1.0 + sc2_ref[0]) + sh2_ref[0]
    for s in range(ROW_TILES):
        h2_ref[pl.ds(s, tm, stride=ROW_TILES), :] = h2[:, s * LANES:(s + 1) * LANES]
    logits = jnp.dot(h2, wr_ref[...], preferred_element_type=F32, precision=HI) + br_ref[...]
    lane = lax.broadcasted_iota(I32, (tm, N_EXPERTS), 1).astype(F32)
    vals, idxs = [], []
    for _ in range(TOP_K):
        m = jnp.max(logits, axis=-1, keepdims=True)
        ix = jnp.min(jnp.where(logits == m, lane, float(N_EXPERTS)), axis=-1, keepdims=True)
        vals.append(m)
        idxs.append(ix)
        logits = jnp.where(lane == ix, -jnp.inf, logits)
    es = [jnp.exp(v - vals[0]) for v in vals]
    den = es[0] + es[1] + es[2] + es[3]
    hot = jnp.zeros((tm, N_EXPERTS), F32)
    for r in range(TOP_K):
        hot = hot + jnp.where(lane == idxs[r], 1.0, 0.0)
    before = run_ref[...] + jnp.dot(tri_ref[...], hot.astype(BF16), preferred_element_type=F32)
    ranks = [jnp.sum(jnp.where(lane == idxs[r], before, 0.0), axis=-1, keepdims=True) for r in range(TOP_K)]
    run_ref[...] = run_ref[...] + jnp.sum(hot, axis=0, keepdims=True)
    cnt_ref[...] = run_ref[...]
    ol = lax.broadcasted_iota(I32, (tm, ROUTE_W), 1)
    out = jnp.zeros((tm, ROUTE_W), F32)
    for r in range(TOP_K):
        out = jnp.where(ol == r, idxs[r], out)
        out = jnp.where(ol == TOP_K + r, es[r] / den, out)
        out = jnp.where(ol == 2 * TOP_K + r, ranks[r], out)
    rt_ref[0] = out


def _outproj(x, cv, nsa, on, w, gt1, g2, sc2, sh2, wr, br):
    B, S, D = x.shape
    tm = min(OUT_TM, S)
    nt = S // tm
    tri = (jnp.arange(tm)[None, :] < jnp.arange(tm)[:, None]).astype(BF16)
    row = pl.BlockSpec((1, 1, D), lambda b, i: (b, 0, 0))
    vec = lambda n: pl.BlockSpec((1, n), lambda b, i: (0, 0))
    return pl.pallas_call(
        _outproj_kernel,
        grid=(B, nt),
        in_specs=[pl.BlockSpec((1, tm, D), lambda b, i: (b, i, 0)),
                  pl.BlockSpec((1, tm, CONV_CH), lambda b, i: (b, i, 0)),
                  pl.BlockSpec((1, tm, NSA_WIDTH), lambda b, i: (b, i, 0)),
                  vec(NSA_WIDTH),
                  pl.BlockSpec((D, D), lambda b, i: (0, 0)),
                  row, vec(D), row, row,
                  pl.BlockSpec((D, N_EXPERTS), lambda b, i: (0, 0)), vec(N_EXPERTS),
                  pl.BlockSpec((tm, tm), lambda b, i: (0, 0))],
        out_specs=[pl.BlockSpec((1, tm, D), lambda b, i: (b, i, 0)),
                   pl.BlockSpec((tm * ROW_TILES, LANES), lambda b, i: (b * nt + i, 0)),
                   pl.BlockSpec((1, tm, ROUTE_W), lambda b, i: (b, i, 0)),
                   vec(N_EXPERTS)],
        out_shape=[jax.ShapeDtypeStruct((B, S, D), F32),
                   jax.ShapeDtypeStruct((B * S * ROW_TILES, LANES), F32),
                   jax.ShapeDtypeStruct((B, S, ROUTE_W), F32),
                   jax.ShapeDtypeStruct((1, N_EXPERTS), F32)],
        scratch_shapes=[pltpu.VMEM((1, N_EXPERTS), F32)],
        compiler_params=pltpu.CompilerParams(dimension_semantics=("arbitrary", "arbitrary")),
        name="outproj",
    )(x, cv, nsa, on, w, gt1, g2, sc2, sh2, wr, br, tri)


def _issue_rows(idx_ref, n, src_hbm, dst, slot, sem):
    assert n % ISSUE_UNROLL == 0

    def body(c, carry):
        for u in range(ISSUE_UNROLL):
            r = c * ISSUE_UNROLL + u
            pltpu.make_async_copy(src_hbm.at[idx_ref[0, 0, r]],
                                  dst.at[slot, pl.ds(pl.multiple_of(r * ROW_TILES, ROW_TILES), ROW_TILES), :],
                                  sem.at[slot]).start()
        return carry
    lax.fori_loop(0, n // ISSUE_UNROLL, body, 0)


def _wait_rows(dst, slot, sem):
    pltpu.make_async_copy(dst.at[slot], dst.at[slot], sem.at[slot]).wait()


def _rows_2d(buf, slot, base, n):
    return jnp.concatenate(
        [buf[slot, pl.ds(base * ROW_TILES + s, n, stride=ROW_TILES), :] for s in range(ROW_TILES)], axis=1)


def _ffn_kernel(be_ref, nb_ref, ta_ref, tb_ref, h2_hbm, wg_ref, bg_ref, wu_ref, bu_ref, wd_ref, bd_ref,
                o_ref, xbuf, sem):
    bm = ta_ref.shape[2]
    i = pl.program_id(0)
    nb = nb_ref[0]
    slot = i % 2

    @pl.when(i == 0)
    def _():
        _issue_rows(ta_ref, bm, h2_hbm, xbuf, 0, sem)

    @pl.when(i + 1 < nb)
    def _():
        _issue_rows(tb_ref, bm, h2_hbm, xbuf, 1 - slot, sem)

    @pl.when(i < nb)
    def _():
        _wait_rows(xbuf, slot, sem)
        x = _rows_2d(xbuf, slot, 0, bm).astype(BF16)
        g = jnp.dot(x, wg_ref[0], preferred_element_type=F32) + bg_ref[0]
        u = jnp.dot(x, wu_ref[0], preferred_element_type=F32) + bu_ref[0]
        g = jnp.minimum(g, SWIGLU_LIMIT)
        u = jnp.clip(u, -SWIGLU_LIMIT, SWIGLU_LIMIT)
        act = g * jax.nn.sigmoid(SWIGLU_ALPHA * g) * (u + 1.0)
        y = jnp.dot(act.astype(BF16), wd_ref[0], preferred_element_type=F32) + bd_ref[0]
        for s in range(ROW_TILES):
            o_ref[pl.ds(s, bm, stride=ROW_TILES), :] = y[:, s * LANES:(s + 1) * LANES]

    @pl.when(i >= nb)
    def _():
        o_ref[...] = jnp.zeros(o_ref.shape, o_ref.dtype)


def _ffn(block_e, nb_used, buf_tok, h2_rows, wg, bg, wu, bu, wd, bd):
    NB = block_e.shape[0]
    bm = FFN_BM
    D, F = D_MODEL, D_FF
    tok3 = buf_tok.reshape(NB, 1, bm)
    wspec = lambda r, c: pl.BlockSpec((1, r, c), lambda i, be, nb: (be[i], 0, 0))
    return pl.pallas_call(
        _ffn_kernel,
        grid_spec=pltpu.PrefetchScalarGridSpec(
            num_scalar_prefetch=2,
            grid=(NB,),
            in_specs=[pl.BlockSpec((1, 1, bm), lambda i, be, nb: (i, 0, 0), memory_space=pltpu.SMEM),
                      pl.BlockSpec((1, 1, bm), lambda i, be, nb: (jnp.minimum(i + 1, NB - 1), 0, 0),
                                   memory_space=pltpu.SMEM),
                      pl.BlockSpec(memory_space=pl.ANY),
                      wspec(D, F), wspec(1, F), wspec(D, F), wspec(1, F), wspec(F, D), wspec(1, D)],
            out_specs=pl.BlockSpec((bm * ROW_TILES, LANES), lambda i, be, nb: (i, 0)),
            scratch_shapes=[pltpu.VMEM((2, bm * ROW_TILES, LANES), F32), pltpu.SemaphoreType.DMA((2,))]),
        out_shape=jax.ShapeDtypeStruct((NB * bm * ROW_TILES, LANES), F32),
        compiler_params=pltpu.CompilerParams(dimension_semantics=("arbitrary",)),
        name="ffn",
    )(block_e, nb_used, tok3, tok3, h2_rows, wg, bg, wu, bu, wd, bd)


def _combine_kernel(da_ref, db_ref, y_hbm, rt_ref, x1_ref, gt2_ref, fg_ref, o_ref, buf, sem):
    tm = x1_ref.shape[0]
    n = TOP_K * tm
    i = pl.program_id(0)
    slot = i % 2

    @pl.when(i == 0)
    def _():
        _issue_rows(da_ref, n, y_hbm, buf, 0, sem)

    @pl.when(i + 1 < pl.num_programs(0))
    def _():
        _issue_rows(db_ref, n, y_hbm, buf, 1 - slot, sem)

    _wait_rows(buf, slot, sem)
    rt = rt_ref[...]
    y = rt[:, TOP_K:TOP_K + 1] * _rows_2d(buf, slot, 0, tm)
    for k in range(1, TOP_K):
        y = y + rt[:, TOP_K + k:TOP_K + k + 1] * _rows_2d(buf, slot, k * tm, tm)
    x2 = x1_ref[...] + gt2_ref[0] * y
    o_ref[...] = _rms(x2, fg_ref[...])


def _combine(dest3, y_rows, route, x1, gt2, fg, S):
    T, D = x1.shape
    tm = min(COMB_TM, S)
    NT = T // tm
    per_b = S // tm
    n = TOP_K * tm
    return pl.pallas_call(
        _combine_kernel,
        grid=(NT,),
        in_specs=[pl.BlockSpec((1, 1, n), lambda i: (i, 0, 0), memory_space=pltpu.SMEM),
                  pl.BlockSpec((1, 1, n), lambda i: (jnp.minimum(i + 1, NT - 1), 0, 0), memory_space=pltpu.SMEM),
                  pl.BlockSpec(memory_space=pl.ANY),
                  pl.BlockSpec((tm, ROUTE_W), lambda i: (i, 0)),
                  pl.BlockSpec((tm, D), lambda i: (i, 0)),
                  pl.BlockSpec((1, 1, D), lambda i: (i // per_b, 0, 0)),
                  pl.BlockSpec((1, D), lambda i: (0, 0))],
        out_specs=pl.BlockSpec((tm, D), lambda i: (i, 0)),
        out_shape=jax.ShapeDtypeStruct((T, D), F32),
        scratch_shapes=[pltpu.VMEM((2, n * ROW_TILES, LANES), F32), pltpu.SemaphoreType.DMA((2,))],
        compiler_params=pltpu.CompilerParams(dimension_semantics=("arbitrary",)),
        name="combine",
    )(dest3, dest3, y_rows, route, x1, gt2, fg)


def _rope_lane_tables(S):
    inv = ROPE_THETA ** (-jnp.arange(0, ROT_DIM, 2, dtype=F32) / ROT_DIM)
    ang = jnp.arange(S, dtype=F32)[:, None] * inv[None, :]
    cos, sin = jnp.cos(ang), jnp.sin(ang)
    d = jnp.arange(LANES) % HEAD_DIM
    first, second = d < ROT_HALF, (d >= ROT_HALF) & (d < ROT_DIM)
    cos_l = cos[:, d % ROT_HALF]
    sin_l = sin[:, d % ROT_HALF]
    rc = jnp.where((d < ROT_DIM)[None], cos_l, 1.0)
    rs1 = jnp.where(second[None], sin_l, 0.0)
    rs2 = jnp.where(first[None], -sin_l, 0.0)
    return rc, rs1, rs2


def _route_plan(route, counts, T):
    bm = FFN_BM
    A = T * TOP_K
    flat_e = route[:, :TOP_K].astype(I32).reshape(A)
    rank = route[:, 2 * TOP_K:3 * TOP_K].astype(I32)
    order = jnp.argsort(flat_e, stable=True).astype(I32)
    counts = counts.astype(I32)
    starts = jnp.cumsum(counts) - counts
    padded = (counts + bm - 1) // bm * bm
    pends = jnp.cumsum(padded)
    pstarts = pends - padded
    P = (A + N_EXPERTS * bm + bm - 1) // bm * bm
    NB = P // bm
    blk0 = jnp.arange(NB, dtype=I32) * bm
    block_e = jnp.minimum(jnp.sum((pends[None, :] <= blk0[:, None]).astype(I32), axis=1), N_EXPERTS - 1)
    r = (blk0 - pstarts[block_e])[:, None] + jnp.arange(bm, dtype=I32)[None, :]
    valid = r < counts[block_e][:, None]
    a_p = order[jnp.clip(starts[block_e][:, None] + r, 0, A - 1)]
    buf_tok = jnp.where(valid, a_p // TOP_K, 0)
    dest_assign = pstarts[flat_e].reshape(T, TOP_K) + rank
    nb_used = (pends[-1] // bm).astype(I32).reshape(1)
    return block_e, nb_used, buf_tok, dest_assign


def kernel(x, c, norm1_g, norm2_g, w_ada, b_ada, w_in, conv_w, conv_b, conv_ln_g, conv_ln_b, cmp_pe_k, cmp_pe_v,
           cmp_k_w1, cmp_k_w2, cmp_v_w1, cmp_v_w2, out_norm_conv, out_norm_nsa, w_out, w_router, b_router,
           w_gate, b_gate, w_up, b_up, w_down, b_down, final_norm_g):
    B, S, D = x.shape
    T = B * S
    G = KV_GROUPS
    assert D == D_MODEL and S % ATTN_TK == 0 and S % CMP_STRIDE == 0
    rc, rs1, rs2 = _rope_lane_tables(S)
    n_sel = S // SEL_BLK
    nc = S // CMP_STRIDE
    cstart = jnp.arange(nc) * CMP_STRIDE
    jstart = jnp.arange(n_sel) * SEL_BLK
    overlap = ((cstart[:, None] <= jstart[None, :] + SEL_BLK - 1)
               & (cstart[:, None] + CMP_LEN - 1 >= jstart[None, :])
               & (jnp.arange(nc)[:, None] < nc - 1)).astype(F32)
    expand = (jnp.arange(S)[None, :] // SEL_BLK == jnp.arange(n_sel)[:, None]).astype(BF16)

    assert w_ada.shape[0] == 1
    for l in range(1):
        mod = _adaln(c, w_ada[l], b_ada[l][None])
        sh1, sc1, gt1, sh2, sc2, gt2 = [m[:, None, :] for m in jnp.split(mod, 6, axis=-1)]

        o = 2 * CONV_CH + NSA_WIDTH
        wl = w_in[l]
        kvc = [wl[:, o + i * KV_W:o + (i + 1) * KV_W] for i in range(6)]
        gl = wl[:, o + 6 * KV_W:]
        w_perm = jnp.concatenate([wl[:, :o], kvc[0], kvc[2], kvc[4], kvc[1], kvc[3], kvc[5],
                                  jnp.pad(gl, ((0, 0), (0, GATE_PAD - gl.shape[1])))], axis=1).astype(BF16)
        u, q, kc, vc, ks, vs, kw, vw, gates = _inproj(x, sc1, sh1, norm1_g[l][None], w_perm, rc, rs1, rs2)

        conv_n = _conv(u, conv_w[l], conv_b[l][None], conv_ln_g[l][None], conv_ln_b[l][None],
                       out_norm_conv[l][None])

        chunk = CMP_STRIDE * HEAD_DIM
        kcmp, vcmp = _compress(kc.reshape(B, G, nc, chunk), vc.reshape(B, G, nc, chunk),
                               cmp_pe_k[l].reshape(1, -1), cmp_pe_v[l].reshape(1, -1),
                               cmp_k_w1[l], cmp_k_w2[l], cmp_v_w1[l], cmp_v_w2[l])
        nsa = _attention(q, kcmp, vcmp, ks, vs, kw, vw, gates, overlap, expand)

        x1, h2_rows, route, counts = _outproj(x, conv_n, nsa, out_norm_nsa[l][None], w_out[l].astype(BF16), gt1,
                                      norm2_g[l][None], sc2, sh2, w_router[l], b_router[l][None])

        route = route.reshape(T, ROUTE_W)
        block_e, nb_used, buf_tok, dest_assign = _route_plan(route, counts[0], T)
        y_rows = _ffn(block_e, nb_used, buf_tok, h2_rows.reshape(T, ROW_TILES, LANES),
                      w_gate[l].astype(BF16), b_gate[l][:, None, :], w_up[l].astype(BF16), b_up[l][:, None, :],
                      w_down[l].astype(BF16), b_down[l][:, None, :])
        tm = min(COMB_TM, S)
        dest3 = dest_assign.reshape(T // tm, tm, TOP_K).transpose(0, 2, 1).reshape(T // tm, 1, TOP_K * tm)
        P = y_rows.shape[0] // ROW_TILES
        x = _combine(dest3, y_rows.reshape(P, ROW_TILES, LANES), route, x1.reshape(T, D), gt2, final_norm_g[None],
                     S).reshape(B, S, D)
    return x
```

```python
import functools

import jax
import jax.numpy as jnp
from jax import lax
from jax.experimental import pallas as pl
from jax.experimental.pallas import tpu as pltpu

F32 = jnp.float32
BF16 = jnp.bfloat16
I32 = jnp.int32
HI = lax.Precision.HIGHEST

D_MODEL = 1024
CONV_CH = 512
CONV_WIDTH = 31
NSA_HEADS = 8
KV_GROUPS = 2
Q_PER_G = NSA_HEADS // KV_GROUPS
HEAD_DIM = 64
NSA_WIDTH = NSA_HEADS * HEAD_DIM
KV_W = KV_GROUPS * HEAD_DIM
ROT_DIM = HEAD_DIM // 4
ROT_HALF = ROT_DIM // 2
ROPE_THETA = 500000.0
CMP_LEN = 32
CMP_STRIDE = 16
CMP_HIDDEN = 128
SEL_BLK = 64
SEL_TOPN = 16
WINDOW = 512
N_EXPERTS = 32
TOP_K = 4
D_FF = 1024
SWIGLU_ALPHA = 1.702
SWIGLU_LIMIT = 7.0
NORM_EPS = 1e-5
NEG_INF = -1e30
FORCE_SCORE = 1e9

LANES = 128
SUBLANES = 8
ROW_TILES = D_MODEL // LANES

GATE_ROWS = 16

INPROJ_TM = 512
CONV_TR = 256
CONV_HALO = 32
ATTN_TQ = 128
ATTN_TK = 256
OUT_TM = 512
FFN_BM = 512
COMB_TM = 256
ROUTE_W = 16
ISSUE_UNROLL = 8


def _rms(x, g):
    return x * lax.rsqrt(jnp.mean(x * x, axis=-1, keepdims=True) + NORM_EPS) * g


def _ada_kernel(c_ref, w_ref, b_ref, o_ref):
    c = c_ref[...]
    ca = c * jax.nn.sigmoid(c)
    o_ref[...] = jnp.dot(ca, w_ref[...], preferred_element_type=F32, precision=HI) + b_ref[...]


def _adaln(c, w, b):
    B = c.shape[0]
    D = D_MODEL
    return pl.pallas_call(
        _ada_kernel,
        grid=(6,),
        in_specs=[pl.BlockSpec((B, D), lambda j: (0, 0)),
                  pl.BlockSpec((D, D), lambda j: (0, j)),
                  pl.BlockSpec((1, D), lambda j: (0, j))],
        out_specs=pl.BlockSpec((B, D), lambda j: (0, j)),
        out_shape=jax.ShapeDtypeStruct((B, 6 * D), F32),
        name="adaln",
    )(c, w, b)


def _inproj_kernel(x_ref, sc_ref, sh_ref, g_ref, wn_ref, wt_ref, rc_ref, rs1_ref, rs2_ref, ct_ref, st_ref,
                   u_ref, kc_ref, vc_ref, ks_ref, kw_ref, qt_ref, vst_ref, vwt_ref, gt_ref):
    tm = x_ref.shape[1]
    tq, tk = ATTN_TQ, ATTN_TK
    h = (_rms(x_ref[0], g_ref[...]) * (1.0 + sc_ref[0]) + sh_ref[0]).astype(BF16)

    p = jnp.dot(h, wn_ref[...], preferred_element_type=F32)
    u_ref[0] = p[:, 0:CONV_CH] * jax.nn.sigmoid(p[:, CONV_CH:2 * CONV_CH])
    c0 = 2 * CONV_CH
    rc, rs1, rs2 = rc_ref[...], rs1_ref[...], rs2_ref[...]
    for ref, roped in ((kc_ref, True), (ks_ref, True), (kw_ref, True), (vc_ref, False)):
        v = p[:, c0:c0 + KV_W]
        if roped:
            v = v * rc + pltpu.roll(v, ROT_HALF, 1) * rs1 + pltpu.roll(v, KV_W - ROT_HALF, 1) * rs2
        for gg in range(KV_GROUPS):
            ref[0, gg] = v[:, HEAD_DIM * gg:HEAD_DIM * (gg + 1)].astype(ref.dtype)
        c0 += KV_W

    pt = lax.dot_general(wt_ref[...], h, (((1,), (1,)), ((), ())), preferred_element_type=F32)
    cos_t, sin_t = ct_ref[...], st_ref[...]
    scale = HEAD_DIM ** -0.5
    for hh in range(NSA_HEADS):
        blk = pt[HEAD_DIM * hh:HEAD_DIM * (hh + 1), :]
        x1, x2 = blk[0:ROT_HALF], blk[ROT_HALF:ROT_DIM]
        qh = (jnp.concatenate([x1 * cos_t - x2 * sin_t, x2 * cos_t + x1 * sin_t, blk[ROT_DIM:]], axis=0)
              * scale).astype(BF16)
        gg, n = divmod(hh, Q_PER_G)
        for j in range(tm // tq):
            qt_ref[0, gg, j, :, n * tq:(n + 1) * tq] = qh[:, j * tq:(j + 1) * tq]
    r0 = NSA_WIDTH
    for ref in (vst_ref, vwt_ref):
        for gg in range(KV_GROUPS):
            blk = pt[r0 + HEAD_DIM * gg:r0 + HEAD_DIM * (gg + 1), :].astype(BF16)
            for j in range(tm // tk):
                ref[0, gg, j] = blk[:, j * tk:(j + 1) * tk]
        r0 += KV_W
    for gg in range(KV_GROUPS):
        gt_ref[0, gg] = jax.nn.sigmoid(pt[r0 + GATE_ROWS * gg:r0 + GATE_ROWS * (gg + 1), :])


def _inproj(x, sc, sh, g, wn, wt, rc, rs1, rs2, cos_t, sin_t):
    B, S, D = x.shape
    tm = min(INPROJ_TM, S)
    tq, tk = ATTN_TQ, ATTN_TK
    G = KV_GROUPS
    kv = lambda dt: jax.ShapeDtypeStruct((B, G, S, HEAD_DIM), dt)
    kv_spec = pl.BlockSpec((1, G, tm, HEAD_DIM), lambda b, i: (b, 0, i, 0))
    vt_shape = jax.ShapeDtypeStruct((B, G, S // tk, HEAD_DIM, tk), BF16)
    vt_spec = pl.BlockSpec((1, G, tm // tk, HEAD_DIM, tk), lambda b, i: (b, 0, i, 0, 0))
    row = pl.BlockSpec((1, 1, D), lambda b, i: (b, 0, 0))
    tab = pl.BlockSpec((tm, LANES), lambda b, i: (i, 0))
    tab_t = pl.BlockSpec((ROT_HALF, tm), lambda b, i: (0, i))
    return pl.pallas_call(
        _inproj_kernel,
        grid=(B, S // tm),
        in_specs=[pl.BlockSpec((1, tm, D), lambda b, i: (b, i, 0)), row, row,
                  pl.BlockSpec((1, D), lambda b, i: (0, 0)),
                  pl.BlockSpec(wn.shape, lambda b, i: (0, 0)),
                  pl.BlockSpec(wt.shape, lambda b, i: (0, 0)),
                  tab, tab, tab, tab_t, tab_t],
        out_specs=[pl.BlockSpec((1, tm, CONV_CH), lambda b, i: (b, i, 0)),
                   kv_spec, kv_spec, kv_spec, kv_spec,
                   pl.BlockSpec((1, G, tm // tq, HEAD_DIM, Q_PER_G * tq), lambda b, i: (b, 0, i, 0, 0)),
                   vt_spec, vt_spec,
                   pl.BlockSpec((1, G, GATE_ROWS, tm), lambda b, i: (b, 0, 0, i))],
        out_shape=[jax.ShapeDtypeStruct((B, S, CONV_CH), F32),
                   kv(F32), kv(F32), kv(BF16), kv(BF16),
                   jax.ShapeDtypeStruct((B, G, S // tq, HEAD_DIM, Q_PER_G * tq), BF16),
                   vt_shape, vt_shape,
                   jax.ShapeDtypeStruct((B, G, GATE_ROWS, S), F32)],
        compiler_params=pltpu.CompilerParams(dimension_semantics=("parallel", "parallel")),
        name="inproj",
    )(x, sc, sh, g, wn, wt, rc, rs1, rs2, cos_t, sin_t)


def _conv_kernel(prev_ref, cur_ref, w_ref, cb_ref, lg_ref, lb_ref, on_ref, o_ref, pad_ref):
    tr = cur_ref.shape[1]
    first = pl.program_id(1) == 0
    halo = prev_ref[0, tr - CONV_HALO:tr, :]
    pad_ref[0:CONV_HALO, :] = jnp.where(first, 0.0, halo)
    pad_ref[CONV_HALO:CONV_HALO + tr, :] = cur_ref[0]
    off = CONV_HALO - (CONV_WIDTH - 1)
    acc = jnp.zeros((tr, CONV_CH), F32)
    for k in range(CONV_WIDTH):
        acc = acc + pad_ref[off + k:off + k + tr, :] * w_ref[k:k + 1, :]
    y = acc + cb_ref[...]
    mu = jnp.mean(y, axis=-1, keepdims=True)
    yc = y - mu
    var = jnp.mean(yc * yc, axis=-1, keepdims=True)
    yn = yc * lax.rsqrt(var + NORM_EPS) * lg_ref[...] + lb_ref[...]
    s = yn * jax.nn.sigmoid(yn)
    o_ref[0] = _rms(s, on_ref[...]).astype(o_ref.dtype)


def _conv(u, w, cb, lg, lb, on):
    B, S, C = u.shape
    tr = min(CONV_TR, S)
    vec = pl.BlockSpec((1, C), lambda b, i: (0, 0))
    return pl.pallas_call(
        _conv_kernel,
        grid=(B, S // tr),
        in_specs=[pl.BlockSpec((1, tr, C), lambda b, i: (b, jnp.maximum(i - 1, 0), 0)),
                  pl.BlockSpec((1, tr, C), lambda b, i: (b, i, 0)),
                  pl.BlockSpec((CONV_WIDTH, C), lambda b, i: (0, 0)),
                  vec, vec, vec, vec],
        out_specs=pl.BlockSpec((1, tr, C), lambda b, i: (b, i, 0)),
        out_shape=jax.ShapeDtypeStruct((B, S, C), BF16),
        scratch_shapes=[pltpu.VMEM((CONV_HALO + tr, C), F32)],
        compiler_params=pltpu.CompilerParams(dimension_semantics=("parallel", "parallel")),
        name="conv",
    )(u, u, w, cb, lg, lb, on)


def _cmp_kernel(kx_ref, vx_ref, pek_ref, pev_ref, kw1_ref, kw2_ref, vw1_ref, vw2t_ref, ko_ref, vo_ref):
    nc = kx_ref.shape[2]
    half = kx_ref.shape[3]
    nt = (((1,), (1,)), ((), ()))
    for x_ref, pe_ref, w1_ref, w2_ref, o_ref, transposed in ((kx_ref, pek_ref, kw1_ref, kw2_ref, ko_ref, False),
                                                             (vx_ref, pev_ref, vw1_ref, vw2t_ref, vo_ref, True)):
        w1 = w1_ref[...]
        pe = jnp.broadcast_to(pe_ref[...], (SUBLANES, 2 * half))
        pe_proj = jnp.dot(pe, w1, preferred_element_type=F32, precision=HI)[0:1]
        for gg in range(KV_GROUPS):
            xg = x_ref[0, gg]
            first = jnp.dot(xg, w1[0:half], preferred_element_type=F32, precision=HI)
            second = jnp.dot(xg, w1[half:2 * half], preferred_element_type=F32, precision=HI)
            hid = first + pltpu.roll(second, nc - 1, 0) + pe_proj
            hid = hid * jax.nn.sigmoid(hid)
            if transposed:
                o_ref[0, gg] = lax.dot_general(w2_ref[...], hid, nt, preferred_element_type=F32, precision=HI)
            else:
                o_ref[0, gg] = jnp.dot(hid, w2_ref[...], preferred_element_type=F32, precision=HI)


def _compress(kx, vx, pek, pev, kw1, kw2, vw1, vw2t):
    B, G, NC, W = kx.shape
    xs = pl.BlockSpec((1, G, NC, W), lambda b: (b, 0, 0, 0))
    full = lambda a: pl.BlockSpec(a.shape, lambda b: (0,) * a.ndim)
    return pl.pallas_call(
        _cmp_kernel,
        grid=(B,),
        in_specs=[xs, xs, full(pek), full(pev), full(kw1), full(kw2), full(vw1), full(vw2t)],
        out_specs=[pl.BlockSpec((1, G, NC, HEAD_DIM), lambda b: (b, 0, 0, 0)),
                   pl.BlockSpec((1, G, HEAD_DIM, NC), lambda b: (b, 0, 0, 0))],
        out_shape=[jax.ShapeDtypeStruct((B, G, NC, HEAD_DIM), F32),
                   jax.ShapeDtypeStruct((B, G, HEAD_DIM, NC), F32)],
        compiler_params=pltpu.CompilerParams(dimension_semantics=("parallel",)),
        name="compress",
    )(kx, vx, pek, pev, kw1, kw2, vw1, vw2t)


def _attn_kernel(qt_ref, kc_ref, vct_ref, ks_ref, vst_ref, kw_ref, vwt_ref, gt_ref, ovt_ref, o_ref, sel_ref):
    tq, tk = ATTN_TQ, ATTN_TK
    R = Q_PER_G * tq
    per_tile = tk // SEL_BLK
    S = ks_ref.shape[2]
    ncp = kc_ref.shape[2]
    nsel = ovt_ref.shape[0]
    qi = pl.program_id(2)
    q0 = qi * tq
    qt = qt_ref[0, 0, 0]
    t_row = q0 + lax.broadcasted_iota(I32, (1, tq), 1)
    heads = lambda a: jnp.concatenate([a] * Q_PER_G, axis=1)

    sc = jnp.dot(kc_ref[0, 0].astype(BF16), qt, preferred_element_type=F32)
    c_io = lax.broadcasted_iota(I32, (ncp, tq), 0)
    m_c = (c_io * CMP_STRIDE + (CMP_LEN - 1) <= t_row) & (c_io < ncp - 1)
    scb = sc + heads(jnp.where(m_c, 0.0, NEG_INF))
    e = jnp.exp(scb - jnp.max(scb, axis=0, keepdims=True)) * heads(jnp.where(m_c, 1.0, 0.0))
    den = jnp.sum(e, axis=0, keepdims=True)
    pc = e / jnp.where(den > 0.0, den, 1.0)
    o_cmp = jnp.dot(vct_ref[0, 0].astype(BF16), pc.astype(BF16), preferred_element_type=F32)

    psum = pc[:, 0:tq]
    for n in range(1, Q_PER_G):
        psum = psum + pc[:, n * tq:(n + 1) * tq]
    imp = jnp.dot(ovt_ref[...], psum, preferred_element_type=F32, precision=HI)
    j_io = lax.broadcasted_iota(I32, (nsel, tq), 0)
    cur = t_row // SEL_BLK
    valid = j_io * SEL_BLK <= t_row
    forced = (j_io == 0) | (j_io == cur) | (j_io == cur - 1)
    score = jnp.where(valid, jnp.where(forced, FORCE_SCORE, imp), NEG_INF)
    rank = jnp.zeros((nsel, tq), F32)
    for i in range(nsel):
        row = score[i:i + 1, :]
        tie = jnp.where(j_io > i, 1.0, 0.0)
        rank = rank + jnp.where(row > score, 1.0, jnp.where(row == score, tie, 0.0))
    sel_bias = jnp.where(rank < float(min(SEL_TOPN, nsel)), 0.0, NEG_INF)
    sel_ref[...] = jnp.zeros(sel_ref.shape, F32)
    for jj in range(S // tk):
        sel_ref[jj, 0:per_tile, :] = sel_bias[jj * per_tile:(jj + 1) * per_tile, :]

    k_io = lax.broadcasted_iota(I32, (tk, tq), 0)

    def flash_step(k_ref, vt_ref, kj, bias, carry):
        m, l, acc = carry
        k0 = pl.multiple_of(kj * tk, tk)
        s = jnp.dot(k_ref[0, 0, pl.ds(k0, tk), :], qt, preferred_element_type=F32) + heads(bias)
        m_new = jnp.maximum(m, jnp.max(s, axis=0, keepdims=True))
        alpha = jnp.exp(m - m_new)
        p = jnp.exp(s - m_new)
        l = alpha * l + jnp.sum(p, axis=0, keepdims=True)
        acc = alpha * acc + jnp.dot(vt_ref[0, 0, kj], p.astype(BF16), preferred_element_type=F32)
        return m_new, l, acc

    init = (jnp.full((1, R), NEG_INF, F32), jnp.zeros((1, R), F32), jnp.zeros((HEAD_DIM, R), F32))

    def slc_body(kj, carry):
        blocks = sel_ref[kj]
        bias = jnp.concatenate([jnp.broadcast_to(blocks[b:b + 1, :], (SEL_BLK, tq)) for b in range(per_tile)], axis=0)
        bias = jnp.where(kj * tk + k_io <= t_row, bias, NEG_INF)
        return flash_step(ks_ref, vst_ref, kj, bias, carry)

    n_slc = (q0 + tq + tk - 1) // tk
    _, l_s, acc_s = lax.fori_loop(0, n_slc, slc_body, init)

    def win_body(kj, carry):
        rel = t_row - (kj * tk + k_io)
        bias = jnp.where((rel >= 0) & (rel < WINDOW), 0.0, NEG_INF)
        return flash_step(kw_ref, vwt_ref, kj, bias, carry)

    lo_tile = jnp.maximum(q0 - (WINDOW - 1), 0) // tk
    _, l_w, acc_w = lax.fori_loop(lo_tile, n_slc, win_body, init)

    gt = gt_ref[0, 0]
    o_slc = acc_s / l_s
    o_win = acc_w / l_w
    outs = []
    for n in range(Q_PER_G):
        cols = slice(n * tq, (n + 1) * tq)
        outs.append(gt[3 * n:3 * n + 1, :] * o_cmp[:, cols] + gt[3 * n + 1:3 * n + 2, :] * o_slc[:, cols]
                    + gt[3 * n + 2:3 * n + 3, :] * o_win[:, cols])
    o_ref[0] = jnp.concatenate(outs, axis=0).T


def _attention(qt, kc, vct, ks, vst, kw, vwt, gt, overlap_t):
    B, G, NQT, _, R = qt.shape
    S = ks.shape[2]
    tq, tk = ATTN_TQ, ATTN_TK
    ncp = kc.shape[2]
    kfull = pl.BlockSpec((1, 1, S, HEAD_DIM), lambda b, g, i: (b, g, 0, 0))
    vfull = pl.BlockSpec((1, 1, S // tk, HEAD_DIM, tk), lambda b, g, i: (b, g, 0, 0, 0))
    return pl.pallas_call(
        _attn_kernel,
        grid=(B, G, NQT),
        in_specs=[pl.BlockSpec((1, 1, 1, HEAD_DIM, R), lambda b, g, i: (b, g, i, 0, 0)),
                  pl.BlockSpec((1, 1, ncp, HEAD_DIM), lambda b, g, i: (b, g, 0, 0)),
                  pl.BlockSpec((1, 1, HEAD_DIM, ncp), lambda b, g, i: (b, g, 0, 0)),
                  kfull, vfull, kfull, vfull,
                  pl.BlockSpec((1, 1, GATE_ROWS, tq), lambda b, g, i: (b, g, 0, i)),
                  pl.BlockSpec(overlap_t.shape, lambda b, g, i: (0, 0))],
        out_specs=pl.BlockSpec((1, tq, Q_PER_G * HEAD_DIM), lambda b, g, i: (b, i, g)),
        out_shape=jax.ShapeDtypeStruct((B, S, NSA_WIDTH), F32),
        scratch_shapes=[pltpu.VMEM((S // tk, SUBLANES, tq), F32)],
        compiler_params=pltpu.CompilerParams(dimension_semantics=("parallel", "parallel", "arbitrary")),
        name="attn",
    )(qt, kc, vct, ks, vst, kw, vwt, gt, overlap_t)


def _outproj_kernel(x_ref, cv_ref, nsa_ref, on_ref, w_ref, gt1_ref, g2_ref, sc2_ref, sh2_ref, wr_ref, br_ref, tri_ref,
                    x1_ref, h2_ref, rt_ref, cnt_ref, run_ref):
    tm = x_ref.shape[1]

    @pl.when((pl.program_id(0) == 0) & (pl.program_id(1) == 0))
    def _():
        run_ref[...] = jnp.zeros(run_ref.shape, F32)

    nn = _rms(nsa_ref[0], on_ref[...]).astype(BF16)
    y = (jnp.dot(cv_ref[0], w_ref[0:CONV_CH, :], preferred_element_type=F32)
         + jnp.dot(nn, w_ref[CONV_CH:CONV_CH + NSA_WIDTH, :], preferred_element_type=F32))
    x1 = x_ref[0] + gt1_ref[0] * y
    x1_ref[0] = x1
    h2 = _rms(x1, g2_ref[...]) * (1.0 + sc2_ref[0]) + sh2_ref[0]
    for s in range(ROW_TILES):
        h2_ref[pl.ds(s, tm, stride=ROW_TILES), :] = h2[:, s * LANES:(s + 1) * LANES]
    logits = jnp.dot(h2, wr_ref[...], preferred_element_type=F32, precision=HI) + br_ref[...]
    lane = lax.broadcasted_iota(I32, (tm, N_EXPERTS), 1).astype(F32)
    vals, idxs = [], []
    for _ in range(TOP_K):
        m = jnp.max(logits, axis=-1, keepdims=True)
        ix = jnp.min(jnp.where(logits == m, lane, float(N_EXPERTS)), axis=-1, keepdims=True)
        vals.append(m)
        idxs.append(ix)
        logits = jnp.where(lane == ix, -jnp.inf, logits)
    es = [jnp.exp(v - vals[0]) for v in vals]
    den = es[0] + es[1] + es[2] + es[3]
    hot = jnp.zeros((tm, N_EXPERTS), F32)
    for r in range(TOP_K):
        hot = hot + jnp.where(lane == idxs[r], 1.0, 0.0)
    before = run_ref[...] + jnp.dot(tri_ref[...], hot.astype(BF16), preferred_element_type=F32)
    ranks = [jnp.sum(jnp.where(lane == idxs[r], before, 0.0), axis=-1, keepdims=True) for r in range(TOP_K)]
    run_ref[...] = run_ref[...] + jnp.sum(hot, axis=0, keepdims=True)
    cnt_ref[...] = run_ref[...]
    ol = lax.broadcasted_iota(I32, (tm, ROUTE_W), 1)
    out = jnp.zeros((tm, ROUTE_W), F32)
    for r in range(TOP_K):
        out = jnp.where(ol == r, idxs[r], out)
        out = jnp.where(ol == TOP_K + r, es[r] / den, out)
        out = jnp.where(ol == 2 * TOP_K + r, ranks[r], out)
    rt_ref[0] = out


def _outproj(x, cv, nsa, on, w, gt1, g2, sc2, sh2, wr, br):
    B, S, D = x.shape
    tm = min(OUT_TM, S)
    nt = S // tm
    tri = (jnp.arange(tm)[None, :] < jnp.arange(tm)[:, None]).astype(BF16)
    row = pl.BlockSpec((1, 1, D), lambda b, i: (b, 0, 0))
    vec = lambda n: pl.BlockSpec((1, n), lambda b, i: (0, 0))
    return pl.pallas_call(
        _outproj_kernel,
        grid=(B, nt),
        in_specs=[pl.BlockSpec((1, tm, D), lambda b, i: (b, i, 0)),
                  pl.BlockSpec((1, tm, CONV_CH), lambda b, i: (b, i, 0)),
                  pl.BlockSpec((1, tm, NSA_WIDTH), lambda b, i: (b, i, 0)),
                  vec(NSA_WIDTH),
                  pl.BlockSpec((D, D), lambda b, i: (0, 0)),
                  row, vec(D), row, row,
                  pl.BlockSpec((D, N_EXPERTS), lambda b, i: (0, 0)), vec(N_EXPERTS),
                  pl.BlockSpec((tm, tm), lambda b, i: (0, 0))],
        out_specs=[pl.BlockSpec((1, tm, D), lambda b, i: (b, i, 0)),
                   pl.BlockSpec((tm * ROW_TILES, LANES), lambda b, i: (b * nt + i, 0)),
                   pl.BlockSpec((1, tm, ROUTE_W), lambda b, i: (b, i, 0)),
                   vec(N_EXPERTS)],
        out_shape=[jax.ShapeDtypeStruct((B, S, D), F32),
                   jax.ShapeDtypeStruct((B * S * ROW_TILES, LANES), F32),
                   jax.ShapeDtypeStruct((B, S, ROUTE_W), F32),
                   jax.ShapeDtypeStruct((1, N_EXPERTS), F32)],
        scratch_shapes=[pltpu.VMEM((1, N_EXPERTS), F32)],
        compiler_params=pltpu.CompilerParams(dimension_semantics=("arbitrary", "arbitrary")),
        name="outproj",
    )(x, cv, nsa, on, w, gt1, g2, sc2, sh2, wr, br, tri)


def _issue_rows(idx_ref, n, src_hbm, dst, slot, sem):
    assert n % ISSUE_UNROLL == 0

    def body(c, carry):
        for u in range(ISSUE_UNROLL):
            r = c * ISSUE_UNROLL + u
            pltpu.make_async_copy(src_hbm.at[idx_ref[0, 0, r]],
                                  dst.at[slot, pl.ds(pl.multiple_of(r * ROW_TILES, ROW_TILES), ROW_TILES), :],
                                  sem.at[slot]).start()
        return carry
    lax.fori_loop(0, n // ISSUE_UNROLL, body, 0)


def _wait_rows(dst, slot, sem):
    pltpu.make_async_copy(dst.at[slot], dst.at[slot], sem.at[slot]).wait()


def _rows_2d(buf, slot, base, n):
    return jnp.concatenate(
        [buf[slot, pl.ds(base * ROW_TILES + s, n, stride=ROW_TILES), :] for s in range(ROW_TILES)], axis=1)


def _ffn_kernel(be_ref, nb_ref, ta_ref, tb_ref, h2_hbm, wg_ref, bg_ref, wu_ref, bu_ref, wd_ref, bd_ref,
                o_ref, xbuf, sem):
    bm = ta_ref.shape[2]
    i = pl.program_id(0)
    nb = nb_ref[0]
    slot = i % 2

    @pl.when(i == 0)
    def _():
        _issue_rows(ta_ref, bm, h2_hbm, xbuf, 0, sem)

    @pl.when(i + 1 < nb)
    def _():
        _issue_rows(tb_ref, bm, h2_hbm, xbuf, 1 - slot, sem)

    @pl.when(i < nb)
    def _():
        _wait_rows(xbuf, slot, sem)
        x = _rows_2d(xbuf, slot, 0, bm).astype(BF16)
        g = jnp.dot(x, wg_ref[0], preferred_element_type=F32) + bg_ref[0]
        u = jnp.dot(x, wu_ref[0], preferred_element_type=F32) + bu_ref[0]
        g = jnp.minimum(g, SWIGLU_LIMIT)
        u = jnp.clip(u, -SWIGLU_LIMIT, SWIGLU_LIMIT)
        act = g * jax.nn.sigmoid(SWIGLU_ALPHA * g) * (u + 1.0)
        y = jnp.dot(act.astype(BF16), wd_ref[0], preferred_element_type=F32) + bd_ref[0]
        for s in range(ROW_TILES):
            o_ref[pl.ds(s, bm, stride=ROW_TILES), :] = y[:, s * LANES:(s + 1) * LANES]

    @pl.when(i >= nb)
    def _():
        o_ref[...] = jnp.zeros(o_ref.shape, o_ref.dtype)


def _ffn(block_e, nb_used, buf_tok, h2_rows, wg, bg, wu, bu, wd, bd):
    NB = block_e.shape[0]
    bm = FFN_BM
    D, F = D_MODEL, D_FF
    tok3 = buf_tok.reshape(NB, 1, bm)
    wspec = lambda r, c: pl.BlockSpec((1, r, c), lambda i, be, nb: (be[i], 0, 0))
    return pl.pallas_call(
        _ffn_kernel,
        grid_spec=pltpu.PrefetchScalarGridSpec(
            num_scalar_prefetch=2,
            grid=(NB,),
            in_specs=[pl.BlockSpec((1, 1, bm), lambda i, be, nb: (i, 0, 0), memory_space=pltpu.SMEM),
                      pl.BlockSpec((1, 1, bm), lambda i, be, nb: (jnp.minimum(i + 1, NB - 1), 0, 0),
                                   memory_space=pltpu.SMEM),
                      pl.BlockSpec(memory_space=pl.ANY),
                      wspec(D, F), wspec(1, F), wspec(D, F), wspec(1, F), wspec(F, D), wspec(1, D)],
            out_specs=pl.BlockSpec((bm * ROW_TILES, LANES), lambda i, be, nb: (i, 0)),
            scratch_shapes=[pltpu.VMEM((2, bm * ROW_TILES, LANES), F32), pltpu.SemaphoreType.DMA((2,))]),
        out_shape=jax.ShapeDtypeStruct((NB * bm * ROW_TILES, LANES), F32),
        compiler_params=pltpu.CompilerParams(dimension_semantics=("arbitrary",)),
        name="ffn",
    )(block_e, nb_used, tok3, tok3, h2_rows, wg, bg, wu, bu, wd, bd)


def _combine_kernel(da_ref, db_ref, y_hbm, rt_ref, x1_ref, gt2_ref, fg_ref, o_ref, buf, sem):
    tm = x1_ref.shape[0]
    n = TOP_K * tm
    i = pl.program_id(0)
    slot = i % 2

    @pl.when(i == 0)
    def _():
        _issue_rows(da_ref, n, y_hbm, buf, 0, sem)

    @pl.when(i + 1 < pl.num_programs(0))
    def _():
        _issue_rows(db_ref, n, y_hbm, buf, 1 - slot, sem)

    _wait_rows(buf, slot, sem)
    rt = rt_ref[...]
    y = rt[:, TOP_K:TOP_K + 1] * _rows_2d(buf, slot, 0, tm)
    for k in range(1, TOP_K):
        y = y + rt[:, TOP_K + k:TOP_K + k + 1] * _rows_2d(buf, slot, k * tm, tm)
    x2 = x1_ref[...] + gt2_ref[0] * y
    o_ref[...] = _rms(x2, fg_ref[...])


def _combine(dest3, y_rows, route, x1, gt2, fg, S):
    T, D = x1.shape
    tm = min(COMB_TM, S)
    NT = T // tm
    per_b = S // tm
    n = TOP_K * tm
    return pl.pallas_call(
        _combine_kernel,
        grid=(NT,),
        in_specs=[pl.BlockSpec((1, 1, n), lambda i: (i, 0, 0), memory_space=pltpu.SMEM),
                  pl.BlockSpec((1, 1, n), lambda i: (jnp.minimum(i + 1, NT - 1), 0, 0), memory_space=pltpu.SMEM),
                  pl.BlockSpec(memory_space=pl.ANY),
                  pl.BlockSpec((tm, ROUTE_W), lambda i: (i, 0)),
                  pl.BlockSpec((tm, D), lambda i: (i, 0)),
                  pl.BlockSpec((1, 1, D), lambda i: (i // per_b, 0, 0)),
                  pl.BlockSpec((1, D), lambda i: (0, 0))],
        out_specs=pl.BlockSpec((tm, D), lambda i: (i, 0)),
        out_shape=jax.ShapeDtypeStruct((T, D), F32),
        scratch_shapes=[pltpu.VMEM((2, n * ROW_TILES, LANES), F32), pltpu.SemaphoreType.DMA((2,))],
        compiler_params=pltpu.CompilerParams(dimension_semantics=("arbitrary",)),
        name="combine",
    )(dest3, dest3, y_rows, route, x1, gt2, fg)


def _rope_tables(S):
    inv = ROPE_THETA ** (-jnp.arange(0, ROT_DIM, 2, dtype=F32) / ROT_DIM)
    ang = jnp.arange(S, dtype=F32)[:, None] * inv[None, :]
    cos, sin = jnp.cos(ang), jnp.sin(ang)
    d = jnp.arange(KV_W) % HEAD_DIM
    first, second = d < ROT_HALF, (d >= ROT_HALF) & (d < ROT_DIM)
    cos_l = cos[:, d % ROT_HALF]
    sin_l = sin[:, d % ROT_HALF]
    rc = jnp.where((d < ROT_DIM)[None], cos_l, 1.0)
    rs1 = jnp.where(second[None], sin_l, 0.0)
    rs2 = jnp.where(first[None], -sin_l, 0.0)
    return rc, rs1, rs2, cos.T, sin.T


def _route_plan(route, counts, T):
    bm = FFN_BM
    A = T * TOP_K
    flat_e = route[:, :TOP_K].astype(I32).reshape(A)
    rank = route[:, 2 * TOP_K:3 * TOP_K].astype(I32)
    order = jnp.argsort(flat_e, stable=True).astype(I32)
    counts = counts.astype(I32)
    starts = jnp.cumsum(counts) - counts
    padded = (counts + bm - 1) // bm * bm
    pends = jnp.cumsum(padded)
    pstarts = pends - padded
    P = (A + N_EXPERTS * bm + bm - 1) // bm * bm
    NB = P // bm
    blk0 = jnp.arange(NB, dtype=I32) * bm
    block_e = jnp.minimum(jnp.sum((pends[None, :] <= blk0[:, None]).astype(I32), axis=1), N_EXPERTS - 1)
    r = (blk0 - pstarts[block_e])[:, None] + jnp.arange(bm, dtype=I32)[None, :]
    valid = r < counts[block_e][:, None]
    a_p = order[jnp.clip(starts[block_e][:, None] + r, 0, A - 1)]
    buf_tok = jnp.where(valid, a_p // TOP_K, 0)
    dest_assign = pstarts[flat_e].reshape(T, TOP_K) + rank
    nb_used = (pends[-1] // bm).astype(I32).reshape(1)
    return block_e, nb_used, buf_tok, dest_assign


def kernel(x, c, norm1_g, norm2_g, w_ada, b_ada, w_in, conv_w, conv_b, conv_ln_g, conv_ln_b, cmp_pe_k, cmp_pe_v,
           cmp_k_w1, cmp_k_w2, cmp_v_w1, cmp_v_w2, out_norm_conv, out_norm_nsa, w_out, w_router, b_router,
           w_gate, b_gate, w_up, b_up, w_down, b_down, final_norm_g):
    B, S, D = x.shape
    T = B * S
    G = KV_GROUPS
    assert D == D_MODEL and S % ATTN_TK == 0 and S % CMP_STRIDE == 0 and KV_W == LANES
    rc, rs1, rs2, cos_t, sin_t = _rope_tables(S)
    n_sel = S // SEL_BLK
    nc = S // CMP_STRIDE
    cstart = jnp.arange(nc) * CMP_STRIDE
    jstart = jnp.arange(n_sel) * SEL_BLK
    overlap_t = ((cstart[None, :] <= jstart[:, None] + SEL_BLK - 1)
                 & (cstart[None, :] + CMP_LEN - 1 >= jstart[:, None])
                 & (jnp.arange(nc)[None, :] < nc - 1)).astype(F32)

    assert w_ada.shape[0] == 1
    for l in range(1):
        mod = _adaln(c, w_ada[l], b_ada[l][None])
        sh1, sc1, gt1, sh2, sc2, gt2 = [m[:, None, :] for m in jnp.split(mod, 6, axis=-1)]

        wl = w_in[l]
        o = 2 * CONV_CH + NSA_WIDTH
        kvc = [wl[:, o + i * KV_W:o + (i + 1) * KV_W] for i in range(6)]
        gl = wl[:, o + 6 * KV_W:]
        per_g = 3 * Q_PER_G
        gpad = [jnp.pad(gl[:, per_g * g:per_g * (g + 1)], ((0, 0), (0, GATE_ROWS - per_g))) for g in range(G)]
        wn = jnp.concatenate([wl[:, :2 * CONV_CH], kvc[0], kvc[2], kvc[4], kvc[1]], axis=1).astype(BF16)
        wt = jnp.concatenate([wl[:, 2 * CONV_CH:o], kvc[3], kvc[5]] + gpad, axis=1).T.astype(BF16)
        u, kc, vc, ks, kw, qt, vst, vwt, gates = _inproj(x, sc1, sh1, norm1_g[l][None], wn, wt,
                                                         rc, rs1, rs2, cos_t, sin_t)

        conv_n = _conv(u, conv_w[l], conv_b[l][None], conv_ln_g[l][None], conv_ln_b[l][None],
                       out_norm_conv[l][None])

        chunk = CMP_STRIDE * HEAD_DIM
        kcmp, vcmp_t = _compress(kc.reshape(B, G, nc, chunk), vc.reshape(B, G, nc, chunk),
                                 cmp_pe_k[l].reshape(1, -1), cmp_pe_v[l].reshape(1, -1),
                                 cmp_k_w1[l], cmp_k_w2[l], cmp_v_w1[l], cmp_v_w2[l].T)
        nsa = _attention(qt, kcmp, vcmp_t, ks, vst, kw, vwt, gates, overlap_t)

        x1, h2_rows, route, counts = _outproj(x, conv_n, nsa, out_norm_nsa[l][None], w_out[l].astype(BF16), gt1,
                                              norm2_g[l][None], sc2, sh2, w_router[l], b_router[l][None])

        route = route.reshape(T, ROUTE_W)
        block_e, nb_used, buf_tok, dest_assign = _route_plan(route, counts[0], T)
        y_rows = _ffn(block_e, nb_used, buf_tok, h2_rows.reshape(T, ROW_TILES, LANES),
                      w_gate[l].astype(BF16), b_gate[l][:, None, :], w_up[l].astype(BF16), b_up[l][:, None, :],
                      w_down[l].astype(BF16), b_down[l][:, None, :])
        tm = min(COMB_TM, S)
        dest3 = dest_assign.reshape(T // tm, tm, TOP_K).transpose(0, 2, 1).reshape(T // tm, 1, TOP_K * tm)
        P = y_rows.shape[0] // ROW_TILES
        x = _combine(dest3, y_rows.reshape(P, ROW_TILES, LANES), route, x1.reshape(T, D), gt2, final_norm_g[None],
                     S).reshape(B, S, D)
    return x
```

```python
import functools

import jax
import jax.numpy as jnp
from jax import lax
from jax.experimental import pallas as pl
from jax.experimental.pallas import tpu as pltpu

F32 = jnp.float32
BF16 = jnp.bfloat16
I32 = jnp.int32
HI = lax.Precision.HIGHEST

D_MODEL = 1024
CONV_CH = 512
CONV_WIDTH = 31
NSA_HEADS = 8
KV_GROUPS = 2
Q_PER_G = NSA_HEADS // KV_GROUPS
HEAD_DIM = 64
NSA_WIDTH = NSA_HEADS * HEAD_DIM
KV_W = KV_GROUPS * HEAD_DIM
ROT_DIM = HEAD_DIM // 4
ROT_HALF = ROT_DIM // 2
ROPE_THETA = 500000.0
CMP_LEN = 32
CMP_STRIDE = 16
CMP_HIDDEN = 128
SEL_BLK = 64
SEL_TOPN = 16
WINDOW = 512
N_EXPERTS = 32
TOP_K = 4
D_FF = 1024
SWIGLU_ALPHA = 1.702
SWIGLU_LIMIT = 7.0
NORM_EPS = 1e-5
NEG_INF = -1e30
FORCE_SCORE = 1e9

LANES = 128
SUBLANES = 8
ROW_TILES = D_MODEL // LANES

GATE_ROWS = 16

INPROJ_TM = 512
CONV_TR = 256
CONV_HALO = 32
ATTN_TQ = 128
ATTN_TK = 256
OUT_TM = 512
FFN_BM = 512
COMB_TM = 256
ROUTE_W = 16
ISSUE_UNROLL = 8


def _rms(x, g):
    return x * lax.rsqrt(jnp.mean(x * x, axis=-1, keepdims=True) + NORM_EPS) * g


def _ada_kernel(c_ref, w_ref, b_ref, o_ref):
    c = c_ref[...]
    ca = c * jax.nn.sigmoid(c)
    o_ref[...] = jnp.dot(ca, w_ref[...], preferred_element_type=F32, precision=HI) + b_ref[...]


def _adaln(c, w, b):
    B = c.shape[0]
    D = D_MODEL
    return pl.pallas_call(
        _ada_kernel,
        grid=(6,),
        in_specs=[pl.BlockSpec((B, D), lambda j: (0, 0)),
                  pl.BlockSpec((D, D), lambda j: (0, j)),
                  pl.BlockSpec((1, D), lambda j: (0, j))],
        out_specs=pl.BlockSpec((B, D), lambda j: (0, j)),
        out_shape=jax.ShapeDtypeStruct((B, 6 * D), F32),
        name="adaln",
    )(c, w, b)


def _inproj_kernel(x_ref, sc_ref, sh_ref, g_ref, wn_ref, wt_ref, rc_ref, rs1_ref, rs2_ref, ct_ref, st_ref,
                   u_ref, kc_ref, vc_ref, ks_ref, kw_ref, qt_ref, vst_ref, vwt_ref, gt_ref):
    tm = x_ref.shape[1]
    tq, tk = ATTN_TQ, ATTN_TK
    h = (_rms(x_ref[0], g_ref[...]) * (1.0 + sc_ref[0]) + sh_ref[0]).astype(BF16)

    p = jnp.dot(h, wn_ref[...], preferred_element_type=F32)
    u_ref[0] = p[:, 0:CONV_CH] * jax.nn.sigmoid(p[:, CONV_CH:2 * CONV_CH])
    c0 = 2 * CONV_CH
    rc, rs1, rs2 = rc_ref[...], rs1_ref[...], rs2_ref[...]
    for ref, roped in ((kc_ref, True), (ks_ref, True), (kw_ref, True), (vc_ref, False)):
        v = p[:, c0:c0 + KV_W]
        if roped:
            v = v * rc + pltpu.roll(v, ROT_HALF, 1) * rs1 + pltpu.roll(v, KV_W - ROT_HALF, 1) * rs2
        for gg in range(KV_GROUPS):
            ref[0, gg] = v[:, HEAD_DIM * gg:HEAD_DIM * (gg + 1)].astype(ref.dtype)
        c0 += KV_W

    pt = lax.dot_general(wt_ref[...], h, (((1,), (1,)), ((), ())), preferred_element_type=F32)
    cos_t, sin_t = ct_ref[...], st_ref[...]
    scale = HEAD_DIM ** -0.5
    for hh in range(NSA_HEADS):
        blk = pt[HEAD_DIM * hh:HEAD_DIM * (hh + 1), :]
        x1, x2 = blk[0:ROT_HALF], blk[ROT_HALF:ROT_DIM]
        qh = (jnp.concatenate([x1 * cos_t - x2 * sin_t, x2 * cos_t + x1 * sin_t, blk[ROT_DIM:]], axis=0)
              * scale).astype(BF16)
        gg, n = divmod(hh, Q_PER_G)
        for j in range(tm // tq):
            qt_ref[0, gg, j, :, n * tq:(n + 1) * tq] = qh[:, j * tq:(j + 1) * tq]
    r0 = NSA_WIDTH
    for ref in (vst_ref, vwt_ref):
        for gg in range(KV_GROUPS):
            blk = pt[r0 + HEAD_DIM * gg:r0 + HEAD_DIM * (gg + 1), :].astype(BF16)
            for j in range(tm // tk):
                ref[0, gg, j] = blk[:, j * tk:(j + 1) * tk]
        r0 += KV_W
    for gg in range(KV_GROUPS):
        gt_ref[0, gg] = jax.nn.sigmoid(pt[r0 + GATE_ROWS * gg:r0 + GATE_ROWS * (gg + 1), :])


def _inproj(x, sc, sh, g, wn, wt, rc, rs1, rs2, cos_t, sin_t):
    B, S, D = x.shape
    tm = min(INPROJ_TM, S)
    tq, tk = ATTN_TQ, ATTN_TK
    G = KV_GROUPS
    kv = lambda dt: jax.ShapeDtypeStruct((B, G, S, HEAD_DIM), dt)
    kv_spec = pl.BlockSpec((1, G, tm, HEAD_DIM), lambda b, i: (b, 0, i, 0))
    vt_shape = jax.ShapeDtypeStruct((B, G, S // tk, HEAD_DIM, tk), BF16)
    vt_spec = pl.BlockSpec((1, G, tm // tk, HEAD_DIM, tk), lambda b, i: (b, 0, i, 0, 0))
    row = pl.BlockSpec((1, 1, D), lambda b, i: (b, 0, 0))
    tab = pl.BlockSpec((tm, LANES), lambda b, i: (i, 0))
    tab_t = pl.BlockSpec((ROT_HALF, tm), lambda b, i: (0, i))
    return pl.pallas_call(
        _inproj_kernel,
        grid=(B, S // tm),
        in_specs=[pl.BlockSpec((1, tm, D), lambda b, i: (b, i, 0)), row, row,
                  pl.BlockSpec((1, D), lambda b, i: (0, 0)),
                  pl.BlockSpec(wn.shape, lambda b, i: (0, 0)),
                  pl.BlockSpec(wt.shape, lambda b, i: (0, 0)),
                  tab, tab, tab, tab_t, tab_t],
        out_specs=[pl.BlockSpec((1, tm, CONV_CH), lambda b, i: (b, i, 0)),
                   kv_spec, kv_spec, kv_spec, kv_spec,
                   pl.BlockSpec((1, G, tm // tq, HEAD_DIM, Q_PER_G * tq), lambda b, i: (b, 0, i, 0, 0)),
                   vt_spec, vt_spec,
                   pl.BlockSpec((1, G, GATE_ROWS, tm), lambda b, i: (b, 0, 0, i))],
        out_shape=[jax.ShapeDtypeStruct((B, S, CONV_CH), F32),
                   kv(F32), kv(F32), kv(BF16), kv(BF16),
                   jax.ShapeDtypeStruct((B, G, S // tq, HEAD_DIM, Q_PER_G * tq), BF16),
                   vt_shape, vt_shape,
                   jax.ShapeDtypeStruct((B, G, GATE_ROWS, S), F32)],
        compiler_params=pltpu.CompilerParams(dimension_semantics=("parallel", "parallel")),
        name="inproj",
    )(x, sc, sh, g, wn, wt, rc, rs1, rs2, cos_t, sin_t)


def _conv_kernel(prev_ref, cur_ref, w_ref, cb_ref, lg_ref, lb_ref, on_ref, o_ref, pad_ref):
    tr = cur_ref.shape[1]
    first = pl.program_id(1) == 0
    halo = prev_ref[0, tr - CONV_HALO:tr, :]
    pad_ref[0:CONV_HALO, :] = jnp.where(first, 0.0, halo)
    pad_ref[CONV_HALO:CONV_HALO + tr, :] = cur_ref[0]
    off = CONV_HALO - (CONV_WIDTH - 1)
    acc = jnp.zeros((tr, CONV_CH), F32)
    for k in range(CONV_WIDTH):
        acc = acc + pad_ref[off + k:off + k + tr, :] * w_ref[k:k + 1, :]
    y = acc + cb_ref[...]
    mu = jnp.mean(y, axis=-1, keepdims=True)
    yc = y - mu
    var = jnp.mean(yc * yc, axis=-1, keepdims=True)
    yn = yc * lax.rsqrt(var + NORM_EPS) * lg_ref[...] + lb_ref[...]
    s = yn * jax.nn.sigmoid(yn)
    o_ref[0] = _rms(s, on_ref[...]).astype(o_ref.dtype)


def _conv(u, w, cb, lg, lb, on):
    B, S, C = u.shape
    tr = min(CONV_TR, S)
    vec = pl.BlockSpec((1, C), lambda b, i: (0, 0))
    return pl.pallas_call(
        _conv_kernel,
        grid=(B, S // tr),
        in_specs=[pl.BlockSpec((1, tr, C), lambda b, i: (b, jnp.maximum(i - 1, 0), 0)),
                  pl.BlockSpec((1, tr, C), lambda b, i: (b, i, 0)),
                  pl.BlockSpec((CONV_WIDTH, C), lambda b, i: (0, 0)),
                  vec, vec, vec, vec],
        out_specs=pl.BlockSpec((1, tr, C), lambda b, i: (b, i, 0)),
        out_shape=jax.ShapeDtypeStruct((B, S, C), BF16),
        scratch_shapes=[pltpu.VMEM((CONV_HALO + tr, C), F32)],
        compiler_params=pltpu.CompilerParams(dimension_semantics=("parallel", "parallel")),
        name="conv",
    )(u, u, w, cb, lg, lb, on)


def _cmp_kernel(kx_ref, vx_ref, pek_ref, pev_ref, kw1_ref, kw2_ref, vw1_ref, vw2t_ref, ko_ref, vo_ref):
    nc = kx_ref.shape[2]
    half = kx_ref.shape[3]
    nt = (((1,), (1,)), ((), ()))
    for x_ref, pe_ref, w1_ref, w2_ref, o_ref, transposed in ((kx_ref, pek_ref, kw1_ref, kw2_ref, ko_ref, False),
                                                             (vx_ref, pev_ref, vw1_ref, vw2t_ref, vo_ref, True)):
        w1 = w1_ref[...]
        pe = jnp.broadcast_to(pe_ref[...], (SUBLANES, 2 * half))
        pe_proj = jnp.dot(pe, w1, preferred_element_type=F32, precision=HI)[0:1]
        for gg in range(KV_GROUPS):
            xg = x_ref[0, gg]
            first = jnp.dot(xg, w1[0:half], preferred_element_type=F32, precision=HI)
            second = jnp.dot(xg, w1[half:2 * half], preferred_element_type=F32, precision=HI)
            hid = first + pltpu.roll(second, nc - 1, 0) + pe_proj
            hid = hid * jax.nn.sigmoid(hid)
            if transposed:
                o_ref[0, gg] = lax.dot_general(w2_ref[...], hid, nt, preferred_element_type=F32, precision=HI)
            else:
                o_ref[0, gg] = jnp.dot(hid, w2_ref[...], preferred_element_type=F32, precision=HI)


def _compress(kx, vx, pek, pev, kw1, kw2, vw1, vw2t):
    B, G, NC, W = kx.shape
    xs = pl.BlockSpec((1, G, NC, W), lambda b: (b, 0, 0, 0))
    full = lambda a: pl.BlockSpec(a.shape, lambda b: (0,) * a.ndim)
    return pl.pallas_call(
        _cmp_kernel,
        grid=(B,),
        in_specs=[xs, xs, full(pek), full(pev), full(kw1), full(kw2), full(vw1), full(vw2t)],
        out_specs=[pl.BlockSpec((1, G, NC, HEAD_DIM), lambda b: (b, 0, 0, 0)),
                   pl.BlockSpec((1, G, HEAD_DIM, NC), lambda b: (b, 0, 0, 0))],
        out_shape=[jax.ShapeDtypeStruct((B, G, NC, HEAD_DIM), F32),
                   jax.ShapeDtypeStruct((B, G, HEAD_DIM, NC), F32)],
        compiler_params=pltpu.CompilerParams(dimension_semantics=("parallel",)),
        name="compress",
    )(kx, vx, pek, pev, kw1, kw2, vw1, vw2t)


def _attn_kernel(qt_ref, kc_ref, vct_ref, ks_ref, vst_ref, kw_ref, vwt_ref, gt_ref, ovt_ref, o_ref, sel_ref):
    tq, tk = ATTN_TQ, ATTN_TK
    R = Q_PER_G * tq
    per_tile = tk // SEL_BLK
    S = ks_ref.shape[2]
    ncp = kc_ref.shape[2]
    nsel = ovt_ref.shape[0]
    qi = pl.program_id(2)
    q0 = qi * tq
    qt = qt_ref[0, 0, 0]
    t_row = q0 + lax.broadcasted_iota(I32, (1, tq), 1)
    heads = lambda a: jnp.concatenate([a] * Q_PER_G, axis=1)

    sc = jnp.dot(kc_ref[0, 0].astype(BF16), qt, preferred_element_type=F32)
    c_io = lax.broadcasted_iota(I32, (ncp, tq), 0)
    m_c = (c_io * CMP_STRIDE + (CMP_LEN - 1) <= t_row) & (c_io < ncp - 1)
    scb = sc + heads(jnp.where(m_c, 0.0, NEG_INF))
    e = jnp.exp(scb - jnp.max(scb, axis=0, keepdims=True)) * heads(jnp.where(m_c, 1.0, 0.0))
    den = jnp.sum(e, axis=0, keepdims=True)
    pc = e / jnp.where(den > 0.0, den, 1.0)
    o_cmp = jnp.dot(vct_ref[0, 0].astype(BF16), pc.astype(BF16), preferred_element_type=F32)

    psum = pc[:, 0:tq]
    for n in range(1, Q_PER_G):
        psum = psum + pc[:, n * tq:(n + 1) * tq]
    imp = jnp.dot(ovt_ref[...], psum, preferred_element_type=F32, precision=HI)
    j_io = lax.broadcasted_iota(I32, (nsel, tq), 0)
    cur = t_row // SEL_BLK
    valid = j_io * SEL_BLK <= t_row
    forced = (j_io == 0) | (j_io == cur) | (j_io == cur - 1)
    score = jnp.where(valid, jnp.where(forced, FORCE_SCORE, imp), NEG_INF)
    rank = jnp.zeros((nsel, tq), F32)
    for i in range(nsel):
        row = score[i:i + 1, :]
        tie = jnp.where(j_io > i, 1.0, 0.0)
        rank = rank + jnp.where(row > score, 1.0, jnp.where(row == score, tie, 0.0))
    sel_bias = jnp.where(rank < float(min(SEL_TOPN, nsel)), 0.0, NEG_INF)
    sel_ref[...] = jnp.zeros(sel_ref.shape, F32)
    for jj in range(S // tk):
        sel_ref[jj, 0:per_tile, :] = sel_bias[jj * per_tile:(jj + 1) * per_tile, :]

    k_io = lax.broadcasted_iota(I32, (tk, tq), 0)

    def flash_step(k_ref, vt_ref, kj, bias, carry):
        m, l, acc = carry
        k0 = pl.multiple_of(kj * tk, tk)
        s = jnp.dot(k_ref[0, 0, pl.ds(k0, tk), :], qt, preferred_element_type=F32) + heads(bias)
        m_new = jnp.maximum(m, jnp.max(s, axis=0, keepdims=True))
        alpha = jnp.exp(m - m_new)
        p = jnp.exp(s - m_new)
        l = alpha * l + jnp.sum(p, axis=0, keepdims=True)
        acc = alpha * acc + jnp.dot(vt_ref[0, 0, kj], p.astype(BF16), preferred_element_type=F32)
        return m_new, l, acc

    init = (jnp.full((1, R), NEG_INF, F32), jnp.zeros((1, R), F32), jnp.zeros((HEAD_DIM, R), F32))

    def slc_body(kj, carry):
        blocks = sel_ref[kj]
        bias = jnp.concatenate([jnp.broadcast_to(blocks[b:b + 1, :], (SEL_BLK, tq)) for b in range(per_tile)], axis=0)
        bias = jnp.where(kj * tk + k_io <= t_row, bias, NEG_INF)
        return flash_step(ks_ref, vst_ref, kj, bias, carry)

    def win_body(kj, carry):
        rel = t_row - (kj * tk + k_io)
        bias = jnp.where((rel >= 0) & (rel < WINDOW), 0.0, NEG_INF)
        return flash_step(kw_ref, vwt_ref, kj, bias, carry)

    n_slc = (q0 + tq + tk - 1) // tk
    lo_tile = jnp.maximum(q0 - (WINDOW - 1), 0) // tk
    n_pair = lo_tile // 2

    def pair_body(i, carry):
        return slc_body(2 * i, carry[0]), slc_body(2 * i + 1, carry[1])

    st_a, st_b = lax.fori_loop(0, n_pair, pair_body, (init, init))
    st_a = lax.fori_loop(2 * n_pair, lo_tile, slc_body, st_a)

    def both_body(kj, carry):
        return slc_body(kj, carry[0]), win_body(kj, carry[1])

    st_a, (_, l_w, acc_w) = lax.fori_loop(lo_tile, n_slc, both_body, (st_a, init))
    m_s = jnp.maximum(st_a[0], st_b[0])
    w_a, w_b = jnp.exp(st_a[0] - m_s), jnp.exp(st_b[0] - m_s)
    l_s = w_a * st_a[1] + w_b * st_b[1]
    acc_s = w_a * st_a[2] + w_b * st_b[2]

    gt = gt_ref[0, 0]
    o_slc = acc_s / l_s
    o_win = acc_w / l_w
    outs = []
    for n in range(Q_PER_G):
        cols = slice(n * tq, (n + 1) * tq)
        outs.append(gt[3 * n:3 * n + 1, :] * o_cmp[:, cols] + gt[3 * n + 1:3 * n + 2, :] * o_slc[:, cols]
                    + gt[3 * n + 2:3 * n + 3, :] * o_win[:, cols])
    o_ref[0] = jnp.concatenate(outs, axis=0).T


def _attention(qt, kc, vct, ks, vst, kw, vwt, gt, overlap_t):
    B, G, NQT, _, R = qt.shape
    S = ks.shape[2]
    tq, tk = ATTN_TQ, ATTN_TK
    ncp = kc.shape[2]
    kfull = pl.BlockSpec((1, 1, S, HEAD_DIM), lambda b, g, i: (b, g, 0, 0))
    vfull = pl.BlockSpec((1, 1, S // tk, HEAD_DIM, tk), lambda b, g, i: (b, g, 0, 0, 0))
    return pl.pallas_call(
        _attn_kernel,
        grid=(B, G, NQT),
        in_specs=[pl.BlockSpec((1, 1, 1, HEAD_DIM, R), lambda b, g, i: (b, g, i, 0, 0)),
                  pl.BlockSpec((1, 1, ncp, HEAD_DIM), lambda b, g, i: (b, g, 0, 0)),
                  pl.BlockSpec((1, 1, HEAD_DIM, ncp), lambda b, g, i: (b, g, 0, 0)),
                  kfull, vfull, kfull, vfull,
                  pl.BlockSpec((1, 1, GATE_ROWS, tq), lambda b, g, i: (b, g, 0, i)),
                  pl.BlockSpec(overlap_t.shape, lambda b, g, i: (0, 0))],
        out_specs=pl.BlockSpec((1, tq, Q_PER_G * HEAD_DIM), lambda b, g, i: (b, i, g)),
        out_shape=jax.ShapeDtypeStruct((B, S, NSA_WIDTH), F32),
        scratch_shapes=[pltpu.VMEM((S // tk, SUBLANES, tq), F32)],
        compiler_params=pltpu.CompilerParams(dimension_semantics=("parallel", "parallel", "arbitrary")),
        name="attn",
    )(qt, kc, vct, ks, vst, kw, vwt, gt, overlap_t)


def _outproj_kernel(x_ref, cv_ref, nsa_ref, on_ref, w_ref, gt1_ref, g2_ref, sc2_ref, sh2_ref, wr_ref, br_ref, tri_ref,
                    x1_ref, h2_ref, rt_ref, cnt_ref, run_ref):
    tm = x_ref.shape[1]

    @pl.when((pl.program_id(0) == 0) & (pl.program_id(1) == 0))
    def _():
        run_ref[...] = jnp.zeros(run_ref.shape, F32)

    nn = _rms(nsa_ref[0], on_ref[...]).astype(BF16)
    y = (jnp.dot(cv_ref[0], w_ref[0:CONV_CH, :], preferred_element_type=F32)
         + jnp.dot(nn, w_ref[CONV_CH:CONV_CH + NSA_WIDTH, :], preferred_element_type=F32))
    x1 = x_ref[0] + gt1_ref[0] * y
    x1_ref[0] = x1
    h2 = _rms(x1, g2_ref[...]) * (1.0 + sc2_ref[0]) + sh2_ref[0]
    for s in range(ROW_TILES):
        h2_ref[pl.ds(s, tm, stride=ROW_TILES), :] = h2[:, s * LANES:(s + 1) * LANES]
    logits = jnp.dot(h2, wr_ref[...], preferred_element_type=F32, precision=HI) + br_ref[...]
    lane = lax.broadcasted_iota(I32, (tm, N_EXPERTS), 1).astype(F32)
    vals, idxs = [], []
    for _ in range(TOP_K):
        m = jnp.max(logits, axis=-1, keepdims=True)
        ix = jnp.min(jnp.where(logits == m, lane, float(N_EXPERTS)), axis=-1, keepdims=True)
        vals.append(m)
        idxs.append(ix)
        logits = jnp.where(lane == ix, -jnp.inf, logits)
    es = [jnp.exp(v - vals[0]) for v in vals]
    den = es[0] + es[1] + es[2] + es[3]
    hot = jnp.zeros((tm, N_EXPERTS), F32)
    for r in range(TOP_K):
        hot = hot + jnp.where(lane == idxs[r], 1.0, 0.0)
    before = run_ref[...] + jnp.dot(tri_ref[...], hot.astype(BF16), preferred_element_type=F32)
    ranks = [jnp.sum(jnp.where(lane == idxs[r], before, 0.0), axis=-1, keepdims=True) for r in range(TOP_K)]
    run_ref[...] = run_ref[...] + jnp.sum(hot, axis=0, keepdims=True)
    cnt_ref[...] = run_ref[...]
    ol = lax.broadcasted_iota(I32, (tm, ROUTE_W), 1)
    out = jnp.zeros((tm, ROUTE_W), F32)
    for r in range(TOP_K):
        out = jnp.where(ol == r, idxs[r], out)
        out = jnp.where(ol == TOP_K + r, es[r] / den, out)
        out = jnp.where(ol == 2 * TOP_K + r, ranks[r], out)
    rt_ref[0] = out


def _outproj(x, cv, nsa, on, w, gt1, g2, sc2, sh2, wr, br):
    B, S, D = x.shape
    tm = min(OUT_TM, S)
    nt = S // tm
    tri = (jnp.arange(tm)[None, :] < jnp.arange(tm)[:, None]).astype(BF16)
    row = pl.BlockSpec((1, 1, D), lambda b, i: (b, 0, 0))
    vec = lambda n: pl.BlockSpec((1, n), lambda b, i: (0, 0))
    return pl.pallas_call(
        _outproj_kernel,
        grid=(B, nt),
        in_specs=[pl.BlockSpec((1, tm, D), lambda b, i: (b, i, 0)),
                  pl.BlockSpec((1, tm, CONV_CH), lambda b, i: (b, i, 0)),
                  pl.BlockSpec((1, tm, NSA_WIDTH), lambda b, i: (b, i, 0)),
                  vec(NSA_WIDTH),
                  pl.BlockSpec((D, D), lambda b, i: (0, 0)),
                  row, vec(D), row, row,
                  pl.BlockSpec((D, N_EXPERTS), lambda b, i: (0, 0)), vec(N_EXPERTS),
                  pl.BlockSpec((tm, tm), lambda b, i: (0, 0))],
        out_specs=[pl.BlockSpec((1, tm, D), lambda b, i: (b, i, 0)),
                   pl.BlockSpec((tm * ROW_TILES, LANES), lambda b, i: (b * nt + i, 0)),
                   pl.BlockSpec((1, tm, ROUTE_W), lambda b, i: (b, i, 0)),
                   vec(N_EXPERTS)],
        out_shape=[jax.ShapeDtypeStruct((B, S, D), F32),
                   jax.ShapeDtypeStruct((B * S * ROW_TILES, LANES), F32),
                   jax.ShapeDtypeStruct((B, S, ROUTE_W), F32),
                   jax.ShapeDtypeStruct((1, N_EXPERTS), F32)],
        scratch_shapes=[pltpu.VMEM((1, N_EXPERTS), F32)],
        compiler_params=pltpu.CompilerParams(dimension_semantics=("arbitrary", "arbitrary")),
        name="outproj",
    )(x, cv, nsa, on, w, gt1, g2, sc2, sh2, wr, br, tri)


def _issue_rows(idx_ref, n, src_hbm, dst, slot, sem):
    assert n % ISSUE_UNROLL == 0

    def body(c, carry):
        for u in range(ISSUE_UNROLL):
            r = c * ISSUE_UNROLL + u
            pltpu.make_async_copy(src_hbm.at[idx_ref[0, 0, r]],
                                  dst.at[slot, pl.ds(pl.multiple_of(r * ROW_TILES, ROW_TILES), ROW_TILES), :],
                                  sem.at[slot]).start(priority=u % 2)
        return carry
    lax.fori_loop(0, n // ISSUE_UNROLL, body, 0)


def _wait_rows(dst, slot, sem):
    pltpu.make_async_copy(dst.at[slot], dst.at[slot], sem.at[slot]).wait()


def _rows_2d(buf, slot, base, n):
    return jnp.concatenate(
        [buf[slot, pl.ds(base * ROW_TILES + s, n, stride=ROW_TILES), :] for s in range(ROW_TILES)], axis=1)


def _ffn_kernel(be_ref, nb_ref, ta_ref, tb_ref, h2_hbm, wg_ref, bg_ref, wu_ref, bu_ref, wd_ref, bd_ref,
                o_ref, xbuf, sem):
    bm = ta_ref.shape[2]
    i = pl.program_id(0)
    nb = nb_ref[0]
    slot = i % 2

    @pl.when(i == 0)
    def _():
        _issue_rows(ta_ref, bm, h2_hbm, xbuf, 0, sem)

    @pl.when(i + 1 < nb)
    def _():
        _issue_rows(tb_ref, bm, h2_hbm, xbuf, 1 - slot, sem)

    @pl.when(i < nb)
    def _():
        _wait_rows(xbuf, slot, sem)
        x = _rows_2d(xbuf, slot, 0, bm).astype(BF16)
        g = jnp.dot(x, wg_ref[0], preferred_element_type=F32) + bg_ref[0]
        u = jnp.dot(x, wu_ref[0], preferred_element_type=F32) + bu_ref[0]
        g = jnp.minimum(g, SWIGLU_LIMIT)
        u = jnp.clip(u, -SWIGLU_LIMIT, SWIGLU_LIMIT)
        act = g * jax.nn.sigmoid(SWIGLU_ALPHA * g) * (u + 1.0)
        y = jnp.dot(act.astype(BF16), wd_ref[0], preferred_element_type=F32) + bd_ref[0]
        for s in range(ROW_TILES):
            o_ref[pl.ds(s, bm, stride=ROW_TILES), :] = y[:, s * LANES:(s + 1) * LANES]

    @pl.when(i >= nb)
    def _():
        o_ref[...] = jnp.zeros(o_ref.shape, o_ref.dtype)


def _ffn(block_e, nb_used, buf_tok, h2_rows, wg, bg, wu, bu, wd, bd):
    NB = block_e.shape[0]
    bm = FFN_BM
    D, F = D_MODEL, D_FF
    tok3 = buf_tok.reshape(NB, 1, bm)
    wspec = lambda r, c: pl.BlockSpec((1, r, c), lambda i, be, nb: (be[i], 0, 0))
    return pl.pallas_call(
        _ffn_kernel,
        grid_spec=pltpu.PrefetchScalarGridSpec(
            num_scalar_prefetch=2,
            grid=(NB,),
            in_specs=[pl.BlockSpec((1, 1, bm), lambda i, be, nb: (i, 0, 0), memory_space=pltpu.SMEM),
                      pl.BlockSpec((1, 1, bm), lambda i, be, nb: (jnp.minimum(i + 1, NB - 1), 0, 0),
                                   memory_space=pltpu.SMEM),
                      pl.BlockSpec(memory_space=pl.ANY),
                      wspec(D, F), wspec(1, F), wspec(D, F), wspec(1, F), wspec(F, D), wspec(1, D)],
            out_specs=pl.BlockSpec((bm * ROW_TILES, LANES), lambda i, be, nb: (i, 0)),
            scratch_shapes=[pltpu.VMEM((2, bm * ROW_TILES, LANES), F32), pltpu.SemaphoreType.DMA((2,))]),
        out_shape=jax.ShapeDtypeStruct((NB * bm * ROW_TILES, LANES), F32),
        compiler_params=pltpu.CompilerParams(dimension_semantics=("arbitrary",)),
        name="ffn",
    )(block_e, nb_used, tok3, tok3, h2_rows, wg, bg, wu, bu, wd, bd)


def _combine_kernel(da_ref, db_ref, y_hbm, rt_ref, x1_ref, gt2_ref, fg_ref, o_ref, buf, sem):
    tm = x1_ref.shape[0]
    n = TOP_K * tm
    i = pl.program_id(0)
    slot = i % 2

    @pl.when(i == 0)
    def _():
        _issue_rows(da_ref, n, y_hbm, buf, 0, sem)

    @pl.when(i + 1 < pl.num_programs(0))
    def _():
        _issue_rows(db_ref, n, y_hbm, buf, 1 - slot, sem)

    _wait_rows(buf, slot, sem)
    rt = rt_ref[...]
    y = rt[:, TOP_K:TOP_K + 1] * _rows_2d(buf, slot, 0, tm)
    for k in range(1, TOP_K):
        y = y + rt[:, TOP_K + k:TOP_K + k + 1] * _rows_2d(buf, slot, k * tm, tm)
    x2 = x1_ref[...] + gt2_ref[0] * y
    o_ref[...] = _rms(x2, fg_ref[...])


def _combine(dest3, y_rows, route, x1, gt2, fg, S):
    T, D = x1.shape
    tm = min(COMB_TM, S)
    NT = T // tm
    per_b = S // tm
    n = TOP_K * tm
    return pl.pallas_call(
        _combine_kernel,
        grid=(NT,),
        in_specs=[pl.BlockSpec((1, 1, n), lambda i: (i, 0, 0), memory_space=pltpu.SMEM),
                  pl.BlockSpec((1, 1, n), lambda i: (jnp.minimum(i + 1, NT - 1), 0, 0), memory_space=pltpu.SMEM),
                  pl.BlockSpec(memory_space=pl.ANY),
                  pl.BlockSpec((tm, ROUTE_W), lambda i: (i, 0)),
                  pl.BlockSpec((tm, D), lambda i: (i, 0)),
                  pl.BlockSpec((1, 1, D), lambda i: (i // per_b, 0, 0)),
                  pl.BlockSpec((1, D), lambda i: (0, 0))],
        out_specs=pl.BlockSpec((tm, D), lambda i: (i, 0)),
        out_shape=jax.ShapeDtypeStruct((T, D), F32),
        scratch_shapes=[pltpu.VMEM((2, n * ROW_TILES, LANES), F32), pltpu.SemaphoreType.DMA((2,))],
        compiler_params=pltpu.CompilerParams(dimension_semantics=("arbitrary",)),
        name="combine",
    )(dest3, dest3, y_rows, route, x1, gt2, fg)


def _rope_tables(S):
    inv = ROPE_THETA ** (-jnp.arange(0, ROT_DIM, 2, dtype=F32) / ROT_DIM)
    ang = jnp.arange(S, dtype=F32)[:, None] * inv[None, :]
    cos, sin = jnp.cos(ang), jnp.sin(ang)
    d = jnp.arange(KV_W) % HEAD_DIM
    first, second = d < ROT_HALF, (d >= ROT_HALF) & (d < ROT_DIM)
    cos_l = cos[:, d % ROT_HALF]
    sin_l = sin[:, d % ROT_HALF]
    rc = jnp.where((d < ROT_DIM)[None], cos_l, 1.0)
    rs1 = jnp.where(second[None], sin_l, 0.0)
    rs2 = jnp.where(first[None], -sin_l, 0.0)
    return rc, rs1, rs2, cos.T, sin.T


def _route_plan(route, counts, T):
    bm = FFN_BM
    A = T * TOP_K
    flat_e = route[:, :TOP_K].astype(I32).reshape(A)
    rank = route[:, 2 * TOP_K:3 * TOP_K].astype(I32)
    order = jnp.argsort(flat_e, stable=True).astype(I32)
    counts = counts.astype(I32)
    starts = jnp.cumsum(counts) - counts
    padded = (counts + bm - 1) // bm * bm
    pends = jnp.cumsum(padded)
    pstarts = pends - padded
    P = (A + N_EXPERTS * bm + bm - 1) // bm * bm
    NB = P // bm
    blk0 = jnp.arange(NB, dtype=I32) * bm
    block_e = jnp.minimum(jnp.sum((pends[None, :] <= blk0[:, None]).astype(I32), axis=1), N_EXPERTS - 1)
    r = (blk0 - pstarts[block_e])[:, None] + jnp.arange(bm, dtype=I32)[None, :]
    valid = r < counts[block_e][:, None]
    a_p = order[jnp.clip(starts[block_e][:, None] + r, 0, A - 1)]
    buf_tok = jnp.where(valid, a_p // TOP_K, 0)
    dest_assign = pstarts[flat_e].reshape(T, TOP_K) + rank
    nb_used = (pends[-1] // bm).astype(I32).reshape(1)
    return block_e, nb_used, buf_tok, dest_assign


def kernel(x, c, norm1_g, norm2_g, w_ada, b_ada, w_in, conv_w, conv_b, conv_ln_g, conv_ln_b, cmp_pe_k, cmp_pe_v,
           cmp_k_w1, cmp_k_w2, cmp_v_w1, cmp_v_w2, out_norm_conv, out_norm_nsa, w_out, w_router, b_router,
           w_gate, b_gate, w_up, b_up, w_down, b_down, final_norm_g):
    B, S, D = x.shape
    T = B * S
    G = KV_GROUPS
    assert D == D_MODEL and S % ATTN_TK == 0 and S % CMP_STRIDE == 0 and KV_W == LANES
    rc, rs1, rs2, cos_t, sin_t = _rope_tables(S)
    n_sel = S // SEL_BLK
    nc = S // CMP_STRIDE
    cstart = jnp.arange(nc) * CMP_STRIDE
    jstart = jnp.arange(n_sel) * SEL_BLK
    overlap_t = ((cstart[None, :] <= jstart[:, None] + SEL_BLK - 1)
                 & (cstart[None, :] + CMP_LEN - 1 >= jstart[:, None])
                 & (jnp.arange(nc)[None, :] < nc - 1)).astype(F32)

    assert w_ada.shape[0] == 1
    for l in range(1):
        mod = _adaln(c, w_ada[l], b_ada[l][None])
        sh1, sc1, gt1, sh2, sc2, gt2 = [m[:, None, :] for m in jnp.split(mod, 6, axis=-1)]

        wl = w_in[l]
        o = 2 * CONV_CH + NSA_WIDTH
        kvc = [wl[:, o + i * KV_W:o + (i + 1) * KV_W] for i in range(6)]
        gl = wl[:, o + 6 * KV_W:]
        per_g = 3 * Q_PER_G
        gpad = [jnp.pad(gl[:, per_g * g:per_g * (g + 1)], ((0, 0), (0, GATE_ROWS - per_g))) for g in range(G)]
        wn = jnp.concatenate([wl[:, :2 * CONV_CH], kvc[0], kvc[2], kvc[4], kvc[1]], axis=1).astype(BF16)
        wt = jnp.concatenate([wl[:, 2 * CONV_CH:o], kvc[3], kvc[5]] + gpad, axis=1).T.astype(BF16)
        u, kc, vc, ks, kw, qt, vst, vwt, gates = _inproj(x, sc1, sh1, norm1_g[l][None], wn, wt,
                                                         rc, rs1, rs2, cos_t, sin_t)

        conv_n = _conv(u, conv_w[l], conv_b[l][None], conv_ln_g[l][None], conv_ln_b[l][None],
                       out_norm_conv[l][None])

        chunk = CMP_STRIDE * HEAD_DIM
        kcmp, vcmp_t = _compress(kc.reshape(B, G, nc, chunk), vc.reshape(B, G, nc, chunk),
                                 cmp_pe_k[l].reshape(1, -1), cmp_pe_v[l].reshape(1, -1),
                                 cmp_k_w1[l], cmp_k_w2[l], cmp_v_w1[l], cmp_v_w2[l].T)
        nsa = _attention(qt, kcmp, vcmp_t, ks, vst, kw, vwt, gates, overlap_t)

        x1, h2_rows, route, counts = _outproj(x, conv_n, nsa, out_norm_nsa[l][None], w_out[l].astype(BF16), gt1,
                                              norm2_g[l][None], sc2, sh2, w_router[l], b_router[l][None])

        route = route.reshape(T, ROUTE_W)
        block_e, nb_used, buf_tok, dest_assign = _route_plan(route, counts[0], T)
        y_rows = _ffn(block_e, nb_used, buf_tok, h2_rows.reshape(T, ROW_TILES, LANES),
                      w_gate[l].astype(BF16), b_gate[l][:, None, :], w_up[l].astype(BF16), b_up[l][:, None, :],
                      w_down[l].astype(BF16), b_down[l][:, None, :])
        tm = min(COMB_TM, S)
        dest3 = dest_assign.reshape(T // tm, tm, TOP_K).transpose(0, 2, 1).reshape(T // tm, 1, TOP_K * tm)
        P = y_rows.shape[0] // ROW_TILES
        x = _combine(dest3, y_rows.reshape(P, ROW_TILES, LANES), route, x1.reshape(T, D), gt2, final_norm_g[None],
                     S).reshape(B, S, D)
    return x
```

```python
import functools

import jax
import jax.numpy as jnp
from jax import lax
from jax.experimental import pallas as pl
from jax.experimental.pallas import tpu as pltpu

F32 = jnp.float32
BF16 = jnp.bfloat16
I32 = jnp.int32
HI = lax.Precision.HIGHEST

D_MODEL = 1024
CONV_CH = 512
CONV_WIDTH = 31
NSA_HEADS = 8
KV_GROUPS = 2
Q_PER_G = NSA_HEADS // KV_GROUPS
HEAD_DIM = 64
NSA_WIDTH = NSA_HEADS * HEAD_DIM
KV_W = KV_GROUPS * HEAD_DIM
ROT_DIM = HEAD_DIM // 4
ROT_HALF = ROT_DIM // 2
ROPE_THETA = 500000.0
CMP_LEN = 32
CMP_STRIDE = 16
CMP_HIDDEN = 128
SEL_BLK = 64
SEL_TOPN = 16
WINDOW = 512
N_EXPERTS = 32
TOP_K = 4
D_FF = 1024
SWIGLU_ALPHA = 1.702
SWIGLU_LIMIT = 7.0
NORM_EPS = 1e-5
NEG_INF = -1e30
FORCE_SCORE = 1e9

LANES = 128
SUBLANES = 8
ROW_TILES = D_MODEL // LANES

GATE_ROWS = 16

INPROJ_TM = 512
CONV_TR = 256
CONV_HALO = 32
ATTN_TQ = 128
ATTN_TK = 256
OUT_TM = 512
FFN_BM = 512
COMB_TM = 256
ROUTE_W = 16
ISSUE_UNROLL = 8


def _rms(x, g):
    return x * lax.rsqrt(jnp.mean(x * x, axis=-1, keepdims=True) + NORM_EPS) * g


def _ada_kernel(c_ref, w_ref, b_ref, o_ref):
    c = c_ref[...]
    ca = c * jax.nn.sigmoid(c)
    o_ref[...] = jnp.dot(ca, w_ref[...], preferred_element_type=F32, precision=HI) + b_ref[...]


def _adaln(c, w, b):
    B = c.shape[0]
    D = D_MODEL
    return pl.pallas_call(
        _ada_kernel,
        grid=(6,),
        in_specs=[pl.BlockSpec((B, D), lambda j: (0, 0)),
                  pl.BlockSpec((D, D), lambda j: (0, j)),
                  pl.BlockSpec((1, D), lambda j: (0, j))],
        out_specs=pl.BlockSpec((B, D), lambda j: (0, j)),
        out_shape=jax.ShapeDtypeStruct((B, 6 * D), F32),
        name="adaln",
    )(c, w, b)


def _inproj_kernel(x_ref, sc_ref, sh_ref, g_ref, wn_ref, wt_ref, rc_ref, rs1_ref, rs2_ref, ct_ref, st_ref,
                   u_ref, kc_ref, vc_ref, ks_ref, kw_ref, qt_ref, vst_ref, vwt_ref, gt_ref):
    tm = x_ref.shape[1]
    tq, tk = ATTN_TQ, ATTN_TK
    h = (_rms(x_ref[0], g_ref[...]) * (1.0 + sc_ref[0]) + sh_ref[0]).astype(BF16)

    p = jnp.dot(h, wn_ref[...], preferred_element_type=F32)
    u_ref[0] = p[:, 0:CONV_CH] * jax.nn.sigmoid(p[:, CONV_CH:2 * CONV_CH])
    c0 = 2 * CONV_CH
    rc, rs1, rs2 = rc_ref[...], rs1_ref[...], rs2_ref[...]
    for ref, roped in ((kc_ref, True), (ks_ref, True), (kw_ref, True), (vc_ref, False)):
        v = p[:, c0:c0 + KV_W]
        if roped:
            v = v * rc + pltpu.roll(v, ROT_HALF, 1) * rs1 + pltpu.roll(v, KV_W - ROT_HALF, 1) * rs2
        for gg in range(KV_GROUPS):
            ref[0, gg] = v[:, HEAD_DIM * gg:HEAD_DIM * (gg + 1)].astype(ref.dtype)
        c0 += KV_W

    pt = lax.dot_general(wt_ref[...], h, (((1,), (1,)), ((), ())), preferred_element_type=F32)
    cos_t, sin_t = ct_ref[...], st_ref[...]
    scale = HEAD_DIM ** -0.5
    for hh in range(NSA_HEADS):
        blk = pt[HEAD_DIM * hh:HEAD_DIM * (hh + 1), :]
        x1, x2 = blk[0:ROT_HALF], blk[ROT_HALF:ROT_DIM]
        qh = (jnp.concatenate([x1 * cos_t - x2 * sin_t, x2 * cos_t + x1 * sin_t, blk[ROT_DIM:]], axis=0)
              * scale).astype(BF16)
        gg, n = divmod(hh, Q_PER_G)
        for j in range(tm // tq):
            qt_ref[0, gg, j, :, n * tq:(n + 1) * tq] = qh[:, j * tq:(j + 1) * tq]
    r0 = NSA_WIDTH
    for ref in (vst_ref, vwt_ref):
        for gg in range(KV_GROUPS):
            blk = pt[r0 + HEAD_DIM * gg:r0 + HEAD_DIM * (gg + 1), :].astype(BF16)
            for j in range(tm // tk):
                ref[0, gg, j] = blk[:, j * tk:(j + 1) * tk]
        r0 += KV_W
    for gg in range(KV_GROUPS):
        gt_ref[0, gg] = jax.nn.sigmoid(pt[r0 + GATE_ROWS * gg:r0 + GATE_ROWS * (gg + 1), :])


def _inproj(x, sc, sh, g, wn, wt, rc, rs1, rs2, cos_t, sin_t):
    B, S, D = x.shape
    tm = min(INPROJ_TM, S)
    tq, tk = ATTN_TQ, ATTN_TK
    G = KV_GROUPS
    kv = lambda dt: jax.ShapeDtypeStruct((B, G, S, HEAD_DIM), dt)
    kv_spec = pl.BlockSpec((1, G, tm, HEAD_DIM), lambda b, i: (b, 0, i, 0))
    vt_shape = jax.ShapeDtypeStruct((B, G, S // tk, HEAD_DIM, tk), BF16)
    vt_spec = pl.BlockSpec((1, G, tm // tk, HEAD_DIM, tk), lambda b, i: (b, 0, i, 0, 0))
    row = pl.BlockSpec((1, 1, D), lambda b, i: (b, 0, 0))
    tab = pl.BlockSpec((tm, LANES), lambda b, i: (i, 0))
    tab_t = pl.BlockSpec((ROT_HALF, tm), lambda b, i: (0, i))
    return pl.pallas_call(
        _inproj_kernel,
        grid=(B, S // tm),
        in_specs=[pl.BlockSpec((1, tm, D), lambda b, i: (b, i, 0)), row, row,
                  pl.BlockSpec((1, D), lambda b, i: (0, 0)),
                  pl.BlockSpec(wn.shape, lambda b, i: (0, 0)),
                  pl.BlockSpec(wt.shape, lambda b, i: (0, 0)),
                  tab, tab, tab, tab_t, tab_t],
        out_specs=[pl.BlockSpec((1, tm, CONV_CH), lambda b, i: (b, i, 0)),
                   kv_spec, kv_spec, kv_spec, kv_spec,
                   pl.BlockSpec((1, G, tm // tq, HEAD_DIM, Q_PER_G * tq), lambda b, i: (b, 0, i, 0, 0)),
                   vt_spec, vt_spec,
                   pl.BlockSpec((1, G, GATE_ROWS, tm), lambda b, i: (b, 0, 0, i))],
        out_shape=[jax.ShapeDtypeStruct((B, S, CONV_CH), F32),
                   kv(F32), kv(F32), kv(BF16), kv(BF16),
                   jax.ShapeDtypeStruct((B, G, S // tq, HEAD_DIM, Q_PER_G * tq), BF16),
                   vt_shape, vt_shape,
                   jax.ShapeDtypeStruct((B, G, GATE_ROWS, S), F32)],
        compiler_params=pltpu.CompilerParams(dimension_semantics=("parallel", "parallel")),
        name="inproj",
    )(x, sc, sh, g, wn, wt, rc, rs1, rs2, cos_t, sin_t)


def _conv_kernel(prev_ref, cur_ref, w_ref, cb_ref, lg_ref, lb_ref, on_ref, o_ref, pad_ref):
    tr = cur_ref.shape[1]
    first = pl.program_id(1) == 0
    halo = prev_ref[0, tr - CONV_HALO:tr, :]
    pad_ref[0:CONV_HALO, :] = jnp.where(first, 0.0, halo)
    pad_ref[CONV_HALO:CONV_HALO + tr, :] = cur_ref[0]
    off = CONV_HALO - (CONV_WIDTH - 1)
    acc = jnp.zeros((tr, CONV_CH), F32)
    for k in range(CONV_WIDTH):
        acc = acc + pad_ref[off + k:off + k + tr, :] * w_ref[k:k + 1, :]
    y = acc + cb_ref[...]
    mu = jnp.mean(y, axis=-1, keepdims=True)
    yc = y - mu
    var = jnp.mean(yc * yc, axis=-1, keepdims=True)
    yn = yc * lax.rsqrt(var + NORM_EPS) * lg_ref[...] + lb_ref[...]
    s = yn * jax.nn.sigmoid(yn)
    o_ref[0] = _rms(s, on_ref[...]).astype(o_ref.dtype)


def _conv(u, w, cb, lg, lb, on):
    B, S, C = u.shape
    tr = min(CONV_TR, S)
    vec = pl.BlockSpec((1, C), lambda b, i: (0, 0))
    return pl.pallas_call(
        _conv_kernel,
        grid=(B, S // tr),
        in_specs=[pl.BlockSpec((1, tr, C), lambda b, i: (b, jnp.maximum(i - 1, 0), 0)),
                  pl.BlockSpec((1, tr, C), lambda b, i: (b, i, 0)),
                  pl.BlockSpec((CONV_WIDTH, C), lambda b, i: (0, 0)),
                  vec, vec, vec, vec],
        out_specs=pl.BlockSpec((1, tr, C), lambda b, i: (b, i, 0)),
        out_shape=jax.ShapeDtypeStruct((B, S, C), BF16),
        scratch_shapes=[pltpu.VMEM((CONV_HALO + tr, C), F32)],
        compiler_params=pltpu.CompilerParams(dimension_semantics=("parallel", "parallel")),
        name="conv",
    )(u, u, w, cb, lg, lb, on)


def _cmp_kernel(kx_ref, vx_ref, pek_ref, pev_ref, kw1_ref, kw2_ref, vw1_ref, vw2t_ref, ko_ref, vo_ref):
    nc = kx_ref.shape[2]
    half = kx_ref.shape[3]
    nt = (((1,), (1,)), ((), ()))
    for x_ref, pe_ref, w1_ref, w2_ref, o_ref, transposed in ((kx_ref, pek_ref, kw1_ref, kw2_ref, ko_ref, False),
                                                             (vx_ref, pev_ref, vw1_ref, vw2t_ref, vo_ref, True)):
        w1 = w1_ref[...]
        pe = jnp.broadcast_to(pe_ref[...], (SUBLANES, 2 * half))
        pe_proj = jnp.dot(pe, w1, preferred_element_type=F32, precision=HI)[0:1]
        for gg in range(KV_GROUPS):
            xg = x_ref[0, gg]
            first = jnp.dot(xg, w1[0:half], preferred_element_type=F32, precision=HI)
            second = jnp.dot(xg, w1[half:2 * half], preferred_element_type=F32, precision=HI)
            hid = first + pltpu.roll(second, nc - 1, 0) + pe_proj
            hid = hid * jax.nn.sigmoid(hid)
            if transposed:
                o_ref[0, gg] = lax.dot_general(w2_ref[...], hid, nt, preferred_element_type=F32, precision=HI)
            else:
                o_ref[0, gg] = jnp.dot(hid, w2_ref[...], preferred_element_type=F32, precision=HI)


def _compress(kx, vx, pek, pev, kw1, kw2, vw1, vw2t):
    B, G, NC, W = kx.shape
    xs = pl.BlockSpec((1, G, NC, W), lambda b: (b, 0, 0, 0))
    full = lambda a: pl.BlockSpec(a.shape, lambda b: (0,) * a.ndim)
    return pl.pallas_call(
        _cmp_kernel,
        grid=(B,),
        in_specs=[xs, xs, full(pek), full(pev), full(kw1), full(kw2), full(vw1), full(vw2t)],
        out_specs=[pl.BlockSpec((1, G, NC, HEAD_DIM), lambda b: (b, 0, 0, 0)),
                   pl.BlockSpec((1, G, HEAD_DIM, NC), lambda b: (b, 0, 0, 0))],
        out_shape=[jax.ShapeDtypeStruct((B, G, NC, HEAD_DIM), F32),
                   jax.ShapeDtypeStruct((B, G, HEAD_DIM, NC), F32)],
        compiler_params=pltpu.CompilerParams(dimension_semantics=("parallel",)),
        name="compress",
    )(kx, vx, pek, pev, kw1, kw2, vw1, vw2t)


def _attn_kernel(qt_ref, kc_ref, vct_ref, ks_ref, vst_ref, kw_ref, vwt_ref, gt_ref, ovt_ref, o_ref, sel_ref):
    tq, tk = ATTN_TQ, ATTN_TK
    R = Q_PER_G * tq
    per_tile = tk // SEL_BLK
    S = ks_ref.shape[2]
    ncp = kc_ref.shape[2]
    nsel = ovt_ref.shape[0]
    qi = pl.program_id(2)
    q0 = qi * tq
    qt = qt_ref[0, 0, 0]
    t_row = q0 + lax.broadcasted_iota(I32, (1, tq), 1)
    heads = lambda a: jnp.concatenate([a] * Q_PER_G, axis=1)

    sc = jnp.dot(kc_ref[0, 0].astype(BF16), qt, preferred_element_type=F32)
    c_io = lax.broadcasted_iota(I32, (ncp, tq), 0)
    m_c = (c_io * CMP_STRIDE + (CMP_LEN - 1) <= t_row) & (c_io < ncp - 1)
    scb = sc + heads(jnp.where(m_c, 0.0, NEG_INF))
    e = jnp.exp(scb - jnp.max(scb, axis=0, keepdims=True)) * heads(jnp.where(m_c, 1.0, 0.0))
    den = jnp.sum(e, axis=0, keepdims=True)
    pc = e / jnp.where(den > 0.0, den, 1.0)
    o_cmp = jnp.dot(vct_ref[0, 0].astype(BF16), pc.astype(BF16), preferred_element_type=F32)

    psum = pc[:, 0:tq]
    for n in range(1, Q_PER_G):
        psum = psum + pc[:, n * tq:(n + 1) * tq]
    imp = jnp.dot(ovt_ref[...], psum, preferred_element_type=F32, precision=HI)
    j_io = lax.broadcasted_iota(I32, (nsel, tq), 0)
    cur = t_row // SEL_BLK
    valid = j_io * SEL_BLK <= t_row
    forced = (j_io == 0) | (j_io == cur) | (j_io == cur - 1)
    score = jnp.where(valid, jnp.where(forced, FORCE_SCORE, imp), NEG_INF)
    rank = jnp.zeros((nsel, tq), F32)
    for i in range(nsel):
        row = score[i:i + 1, :]
        tie = jnp.where(j_io > i, 1.0, 0.0)
        rank = rank + jnp.where(row > score, 1.0, jnp.where(row == score, tie, 0.0))
    sel_bias = jnp.where(rank < float(min(SEL_TOPN, nsel)), 0.0, NEG_INF)
    sel_ref[...] = jnp.zeros(sel_ref.shape, F32)
    for jj in range(S // tk):
        sel_ref[jj, 0:per_tile, :] = sel_bias[jj * per_tile:(jj + 1) * per_tile, :]

    k_io = lax.broadcasted_iota(I32, (tk, tq), 0)

    def flash_step(k_ref, vt_ref, kj, bias, carry):
        m, l, acc = carry
        k0 = pl.multiple_of(kj * tk, tk)
        s = jnp.dot(k_ref[0, 0, pl.ds(k0, tk), :], qt, preferred_element_type=F32) + heads(bias)
        m_new = jnp.maximum(m, jnp.max(s, axis=0, keepdims=True))
        alpha = jnp.exp(m - m_new)
        p = jnp.exp(s - m_new)
        l = alpha * l + jnp.sum(p, axis=0, keepdims=True)
        acc = alpha * acc + jnp.dot(vt_ref[0, 0, kj], p.astype(BF16), preferred_element_type=F32)
        return m_new, l, acc

    init = (jnp.full((1, R), NEG_INF, F32), jnp.zeros((1, R), F32), jnp.zeros((HEAD_DIM, R), F32))

    def slc_body(kj, carry):
        blocks = sel_ref[kj]
        bias = jnp.concatenate([jnp.broadcast_to(blocks[b:b + 1, :], (SEL_BLK, tq)) for b in range(per_tile)], axis=0)
        bias = jnp.where(kj * tk + k_io <= t_row, bias, NEG_INF)
        return flash_step(ks_ref, vst_ref, kj, bias, carry)

    def win_body(kj, carry):
        rel = t_row - (kj * tk + k_io)
        bias = jnp.where((rel >= 0) & (rel < WINDOW), 0.0, NEG_INF)
        return flash_step(kw_ref, vwt_ref, kj, bias, carry)

    n_slc = (q0 + tq + tk - 1) // tk
    lo_tile = jnp.maximum(q0 - (WINDOW - 1), 0) // tk
    n_pair = lo_tile // 2

    def pair_body(i, carry):
        return slc_body(2 * i, carry[0]), slc_body(2 * i + 1, carry[1])

    st_a, st_b = lax.fori_loop(0, n_pair, pair_body, (init, init))
    st_a = lax.fori_loop(2 * n_pair, lo_tile, slc_body, st_a)

    def both_body(kj, carry):
        return slc_body(kj, carry[0]), win_body(kj, carry[1])

    st_a, (_, l_w, acc_w) = lax.fori_loop(lo_tile, n_slc, both_body, (st_a, init))
    m_s = jnp.maximum(st_a[0], st_b[0])
    w_a, w_b = jnp.exp(st_a[0] - m_s), jnp.exp(st_b[0] - m_s)
    l_s = w_a * st_a[1] + w_b * st_b[1]
    acc_s = w_a * st_a[2] + w_b * st_b[2]

    gt = gt_ref[0, 0]
    o_slc = acc_s / l_s
    o_win = acc_w / l_w
    outs = []
    for n in range(Q_PER_G):
        cols = slice(n * tq, (n + 1) * tq)
        outs.append(gt[3 * n:3 * n + 1, :] * o_cmp[:, cols] + gt[3 * n + 1:3 * n + 2, :] * o_slc[:, cols]
                    + gt[3 * n + 2:3 * n + 3, :] * o_win[:, cols])
    o_ref[0] = jnp.concatenate(outs, axis=0).T


def _attention(qt, kc, vct, ks, vst, kw, vwt, gt, overlap_t):
    B, G, NQT, _, R = qt.shape
    S = ks.shape[2]
    tq, tk = ATTN_TQ, ATTN_TK
    ncp = kc.shape[2]
    kfull = pl.BlockSpec((1, 1, S, HEAD_DIM), lambda b, g, i: (b, g, 0, 0))
    vfull = pl.BlockSpec((1, 1, S // tk, HEAD_DIM, tk), lambda b, g, i: (b, g, 0, 0, 0))
    return pl.pallas_call(
        _attn_kernel,
        grid=(B, G, NQT),
        in_specs=[pl.BlockSpec((1, 1, 1, HEAD_DIM, R), lambda b, g, i: (b, g, i, 0, 0)),
                  pl.BlockSpec((1, 1, ncp, HEAD_DIM), lambda b, g, i: (b, g, 0, 0)),
                  pl.BlockSpec((1, 1, HEAD_DIM, ncp), lambda b, g, i: (b, g, 0, 0)),
                  kfull, vfull, kfull, vfull,
                  pl.BlockSpec((1, 1, GATE_ROWS, tq), lambda b, g, i: (b, g, 0, i)),
                  pl.BlockSpec(overlap_t.shape, lambda b, g, i: (0, 0))],
        out_specs=pl.BlockSpec((1, tq, Q_PER_G * HEAD_DIM), lambda b, g, i: (b, i, g)),
        out_shape=jax.ShapeDtypeStruct((B, S, NSA_WIDTH), F32),
        scratch_shapes=[pltpu.VMEM((S // tk, SUBLANES, tq), F32)],
        compiler_params=pltpu.CompilerParams(dimension_semantics=("parallel", "parallel", "arbitrary")),
        name="attn",
    )(qt, kc, vct, ks, vst, kw, vwt, gt, overlap_t)


def _outproj_kernel(x_ref, cv_ref, nsa_ref, on_ref, w_ref, gt1_ref, g2_ref, sc2_ref, sh2_ref, wrh_ref, wrl_ref, br_ref,
                    tri_ref, x1_ref, h2_ref, rt_ref, cnt_ref, run_ref):
    tm = x_ref.shape[1]

    @pl.when((pl.program_id(0) == 0) & (pl.program_id(1) == 0))
    def _():
        run_ref[...] = jnp.zeros(run_ref.shape, F32)

    nn = _rms(nsa_ref[0], on_ref[...]).astype(BF16)
    y = (jnp.dot(cv_ref[0], w_ref[0:CONV_CH, :], preferred_element_type=F32)
         + jnp.dot(nn, w_ref[CONV_CH:CONV_CH + NSA_WIDTH, :], preferred_element_type=F32))
    x1 = x_ref[0] + gt1_ref[0] * y
    x1_ref[0] = x1
    h2 = _rms(x1, g2_ref[...]) * (1.0 + sc2_ref[0]) + sh2_ref[0]
    for s in range(ROW_TILES):
        h2_ref[pl.ds(s, tm, stride=ROW_TILES), :] = h2[:, s * LANES:(s + 1) * LANES]
    nt = (((1,), (1,)), ((), ()))
    h_hi = h2.astype(BF16)
    h_lo = (h2 - h_hi.astype(F32)).astype(BF16)
    logits = (lax.dot_general(wrh_ref[...], h_hi, nt, preferred_element_type=F32)
              + lax.dot_general(wrh_ref[...], h_lo, nt, preferred_element_type=F32)
              + lax.dot_general(wrl_ref[...], h_hi, nt, preferred_element_type=F32)) + br_ref[...]
    eio = lax.broadcasted_iota(I32, (N_EXPERTS, tm), 0).astype(F32)
    vals, idxs = [], []
    for _ in range(TOP_K):
        m = jnp.max(logits, axis=0, keepdims=True)
        ix = jnp.min(jnp.where(logits == m, eio, float(N_EXPERTS)), axis=0, keepdims=True)
        vals.append(m)
        idxs.append(ix)
        logits = jnp.where(eio == ix, -jnp.inf, logits)
    es = [jnp.exp(v - vals[0]) for v in vals]
    den = es[0] + es[1] + es[2] + es[3]
    hot = jnp.zeros((N_EXPERTS, tm), F32)
    for r in range(TOP_K):
        hot = hot + jnp.where(eio == idxs[r], 1.0, 0.0)
    before = run_ref[...] + jnp.dot(hot.astype(BF16), tri_ref[...], preferred_element_type=F32)
    ranks = [jnp.sum(jnp.where(eio == idxs[r], before, 0.0), axis=0, keepdims=True) for r in range(TOP_K)]
    run_ref[...] = run_ref[...] + jnp.sum(hot, axis=1, keepdims=True)
    cnt_ref[...] = run_ref[...]
    rio = lax.broadcasted_iota(I32, (ROUTE_W, tm), 0)
    out = jnp.zeros((ROUTE_W, tm), F32)
    for r in range(TOP_K):
        out = jnp.where(rio == r, idxs[r], out)
        out = jnp.where(rio == TOP_K + r, es[r] / den, out)
        out = jnp.where(rio == 2 * TOP_K + r, ranks[r], out)
    rt_ref[0] = out


def _outproj(x, cv, nsa, on, w, gt1, g2, sc2, sh2, wr, br):
    B, S, D = x.shape
    tm = min(OUT_TM, S)
    nt = S // tm
    tri = (jnp.arange(tm)[:, None] < jnp.arange(tm)[None, :]).astype(BF16)
    wr_t = wr.T
    wr_hi = wr_t.astype(BF16)
    wr_lo = (wr_t - wr_hi.astype(F32)).astype(BF16)
    row = pl.BlockSpec((1, 1, D), lambda b, i: (b, 0, 0))
    vec = lambda n: pl.BlockSpec((1, n), lambda b, i: (0, 0))
    col = pl.BlockSpec((N_EXPERTS, 1), lambda b, i: (0, 0))
    wr_spec = pl.BlockSpec((N_EXPERTS, D), lambda b, i: (0, 0))
    return pl.pallas_call(
        _outproj_kernel,
        grid=(B, nt),
        in_specs=[pl.BlockSpec((1, tm, D), lambda b, i: (b, i, 0)),
                  pl.BlockSpec((1, tm, CONV_CH), lambda b, i: (b, i, 0)),
                  pl.BlockSpec((1, tm, NSA_WIDTH), lambda b, i: (b, i, 0)),
                  vec(NSA_WIDTH),
                  pl.BlockSpec((D, D), lambda b, i: (0, 0)),
                  row, vec(D), row, row,
                  wr_spec, wr_spec, col,
                  pl.BlockSpec((tm, tm), lambda b, i: (0, 0))],
        out_specs=[pl.BlockSpec((1, tm, D), lambda b, i: (b, i, 0)),
                   pl.BlockSpec((tm * ROW_TILES, LANES), lambda b, i: (b * nt + i, 0)),
                   pl.BlockSpec((1, ROUTE_W, tm), lambda b, i: (b, 0, i)),
                   col],
        out_shape=[jax.ShapeDtypeStruct((B, S, D), F32),
                   jax.ShapeDtypeStruct((B * S * ROW_TILES, LANES), F32),
                   jax.ShapeDtypeStruct((B, ROUTE_W, S), F32),
                   jax.ShapeDtypeStruct((N_EXPERTS, 1), F32)],
        scratch_shapes=[pltpu.VMEM((N_EXPERTS, 1), F32)],
        compiler_params=pltpu.CompilerParams(dimension_semantics=("arbitrary", "arbitrary")),
        name="outproj",
    )(x, cv, nsa, on, w, gt1, g2, sc2, sh2, wr_hi, wr_lo, br.reshape(N_EXPERTS, 1), tri)


def _issue_rows(idx_ref, n, src_hbm, dst, slot, sem):
    assert n % ISSUE_UNROLL == 0

    def body(c, carry):
        for u in range(ISSUE_UNROLL):
            r = c * ISSUE_UNROLL + u
            pltpu.make_async_copy(src_hbm.at[idx_ref[0, 0, r]],
                                  dst.at[slot, pl.ds(pl.multiple_of(r * ROW_TILES, ROW_TILES), ROW_TILES), :],
                                  sem.at[slot]).start(priority=u % 2)
        return carry
    lax.fori_loop(0, n // ISSUE_UNROLL, body, 0)


def _wait_rows(dst, slot, sem):
    pltpu.make_async_copy(dst.at[slot], dst.at[slot], sem.at[slot]).wait()


def _rows_2d(buf, slot, base, n):
    return jnp.concatenate(
        [buf[slot, pl.ds(base * ROW_TILES + s, n, stride=ROW_TILES), :] for s in range(ROW_TILES)], axis=1)


def _ffn_kernel(be_ref, nb_ref, ta_ref, tb_ref, h2_hbm, gate_ref, wg_ref, bg_ref, wu_ref, bu_ref, wd_ref, bd_ref,
                o_ref, xbuf, wgb, wub, wdb, sem):
    bm = ta_ref.shape[2]
    i = pl.program_id(0)
    nb = nb_ref[0]
    slot = i % 2

    @pl.when((i == 0) | (be_ref[i] != be_ref[jnp.maximum(i - 1, 0)]))
    def _():
        wgb[...] = wg_ref[0].astype(BF16)
        wub[...] = wu_ref[0].astype(BF16)
        wdb[...] = wd_ref[0].astype(BF16)

    @pl.when(i == 0)
    def _():
        _issue_rows(ta_ref, bm, h2_hbm, xbuf, 0, sem)

    @pl.when(i + 1 < nb)
    def _():
        _issue_rows(tb_ref, bm, h2_hbm, xbuf, 1 - slot, sem)

    @pl.when(i < nb)
    def _():
        _wait_rows(xbuf, slot, sem)
        x = _rows_2d(xbuf, slot, 0, bm).astype(BF16)
        g = jnp.dot(x, wgb[...], preferred_element_type=F32) + bg_ref[0]
        u = jnp.dot(x, wub[...], preferred_element_type=F32) + bu_ref[0]
        g = jnp.minimum(g, SWIGLU_LIMIT)
        u = jnp.clip(u, -SWIGLU_LIMIT, SWIGLU_LIMIT)
        act = g * jax.nn.sigmoid(SWIGLU_ALPHA * g) * (u + 1.0)
        y = (jnp.dot(act.astype(BF16), wdb[...], preferred_element_type=F32) + bd_ref[0]) * gate_ref[...]
        for s in range(ROW_TILES):
            o_ref[pl.ds(s, bm, stride=ROW_TILES), :] = y[:, s * LANES:(s + 1) * LANES]

    @pl.when(i >= nb)
    def _():
        o_ref[...] = jnp.zeros(o_ref.shape, o_ref.dtype)


def _ffn(block_e, nb_used, buf_tok, h2_rows, buf_gate, wg, bg, wu, bu, wd, bd):
    NB = block_e.shape[0]
    bm = FFN_BM
    D, F = D_MODEL, D_FF
    tok3 = buf_tok.reshape(NB, 1, bm)
    wspec = lambda r, c: pl.BlockSpec((1, r, c), lambda i, be, nb: (be[i], 0, 0))
    vmem_limit = 2 * 3 * D * F * 4 + 3 * D * F * 2 + 4 * bm * D * 4 + 6 * bm * F * 4
    return pl.pallas_call(
        _ffn_kernel,
        grid_spec=pltpu.PrefetchScalarGridSpec(
            num_scalar_prefetch=2,
            grid=(NB,),
            in_specs=[pl.BlockSpec((1, 1, bm), lambda i, be, nb: (i, 0, 0), memory_space=pltpu.SMEM),
                      pl.BlockSpec((1, 1, bm), lambda i, be, nb: (jnp.minimum(i + 1, NB - 1), 0, 0),
                                   memory_space=pltpu.SMEM),
                      pl.BlockSpec(memory_space=pl.ANY),
                      pl.BlockSpec((bm, 1), lambda i, be, nb: (i, 0)),
                      wspec(D, F), wspec(1, F), wspec(D, F), wspec(1, F), wspec(F, D), wspec(1, D)],
            out_specs=pl.BlockSpec((bm * ROW_TILES, LANES), lambda i, be, nb: (i, 0)),
            scratch_shapes=[pltpu.VMEM((2, bm * ROW_TILES, LANES), F32),
                            pltpu.VMEM((D, F), BF16), pltpu.VMEM((D, F), BF16), pltpu.VMEM((F, D), BF16),
                            pltpu.SemaphoreType.DMA((2,))]),
        out_shape=jax.ShapeDtypeStruct((NB * bm * ROW_TILES, LANES), F32),
        compiler_params=pltpu.CompilerParams(dimension_semantics=("arbitrary",),
                                             vmem_limit_bytes=vmem_limit),
        name="ffn",
    )(block_e, nb_used, tok3, tok3, h2_rows, buf_gate, wg, bg, wu, bu, wd, bd)


def _combine_kernel(da_ref, db_ref, y_hbm, x1_ref, gt2_ref, fg_ref, o_ref, buf, sem):
    tm = x1_ref.shape[0]
    n = TOP_K * tm
    i = pl.program_id(0)
    slot = i % 2

    @pl.when(i == 0)
    def _():
        _issue_rows(da_ref, n, y_hbm, buf, 0, sem)

    @pl.when(i + 1 < pl.num_programs(0))
    def _():
        _issue_rows(db_ref, n, y_hbm, buf, 1 - slot, sem)

    _wait_rows(buf, slot, sem)
    y = _rows_2d(buf, slot, 0, tm)
    for k in range(1, TOP_K):
        y = y + _rows_2d(buf, slot, k * tm, tm)
    x2 = x1_ref[...] + gt2_ref[0] * y
    o_ref[...] = _rms(x2, fg_ref[...])


def _combine(dest3, y_rows, x1, gt2, fg, S):
    T, D = x1.shape
    tm = min(COMB_TM, S)
    NT = T // tm
    per_b = S // tm
    n = TOP_K * tm
    return pl.pallas_call(
        _combine_kernel,
        grid=(NT,),
        in_specs=[pl.BlockSpec((1, 1, n), lambda i: (i, 0, 0), memory_space=pltpu.SMEM),
                  pl.BlockSpec((1, 1, n), lambda i: (jnp.minimum(i + 1, NT - 1), 0, 0), memory_space=pltpu.SMEM),
                  pl.BlockSpec(memory_space=pl.ANY),
                  pl.BlockSpec((tm, D), lambda i: (i, 0)),
                  pl.BlockSpec((1, 1, D), lambda i: (i // per_b, 0, 0)),
                  pl.BlockSpec((1, D), lambda i: (0, 0))],
        out_specs=pl.BlockSpec((tm, D), lambda i: (i, 0)),
        out_shape=jax.ShapeDtypeStruct((T, D), F32),
        scratch_shapes=[pltpu.VMEM((2, n * ROW_TILES, LANES), F32), pltpu.SemaphoreType.DMA((2,))],
        compiler_params=pltpu.CompilerParams(dimension_semantics=("arbitrary",)),
        name="combine",
    )(dest3, dest3, y_rows, x1, gt2, fg)


def _rope_tables(S):
    inv = ROPE_THETA ** (-jnp.arange(0, ROT_DIM, 2, dtype=F32) / ROT_DIM)
    ang = jnp.arange(S, dtype=F32)[:, None] * inv[None, :]
    cos, sin = jnp.cos(ang), jnp.sin(ang)
    d = jnp.arange(KV_W) % HEAD_DIM
    first, second = d < ROT_HALF, (d >= ROT_HALF) & (d < ROT_DIM)
    cos_l = cos[:, d % ROT_HALF]
    sin_l = sin[:, d % ROT_HALF]
    rc = jnp.where((d < ROT_DIM)[None], cos_l, 1.0)
    rs1 = jnp.where(second[None], sin_l, 0.0)
    rs2 = jnp.where(first[None], -sin_l, 0.0)
    return rc, rs1, rs2, cos.T, sin.T


def _route_plan(route, counts, T):
    bm = FFN_BM
    A = T * TOP_K
    tok_major = lambda r: jnp.swapaxes(r, 1, 2).reshape(T, TOP_K)
    flat_e = tok_major(route[:, 0:TOP_K]).astype(I32).reshape(A)
    flat_g = tok_major(route[:, TOP_K:2 * TOP_K]).reshape(A)
    rank = tok_major(route[:, 2 * TOP_K:3 * TOP_K]).astype(I32)
    order = jnp.argsort(flat_e, stable=True).astype(I32)
    counts = counts.astype(I32)
    starts = jnp.cumsum(counts) - counts
    padded = (counts + bm - 1) // bm * bm
    pends = jnp.cumsum(padded)
    pstarts = pends - padded
    P = (A + N_EXPERTS * bm + bm - 1) // bm * bm
    NB = P // bm
    blk0 = jnp.arange(NB, dtype=I32) * bm
    block_e = jnp.minimum(jnp.sum((pends[None, :] <= blk0[:, None]).astype(I32), axis=1), N_EXPERTS - 1)
    r = (blk0 - pstarts[block_e])[:, None] + jnp.arange(bm, dtype=I32)[None, :]
    valid = r < counts[block_e][:, None]
    a_p = order[jnp.clip(starts[block_e][:, None] + r, 0, A - 1)]
    buf_tok = jnp.where(valid, a_p // TOP_K, 0)
    buf_gate = jnp.where(valid, flat_g[a_p], 0.0).reshape(P, 1)
    dest_assign = pstarts[flat_e].reshape(T, TOP_K) + rank
    nb_used = (pends[-1] // bm).astype(I32).reshape(1)
    return block_e, nb_used, buf_tok, buf_gate, dest_assign


def kernel(x, c, norm1_g, norm2_g, w_ada, b_ada, w_in, conv_w, conv_b, conv_ln_g, conv_ln_b, cmp_pe_k, cmp_pe_v,
           cmp_k_w1, cmp_k_w2, cmp_v_w1, cmp_v_w2, out_norm_conv, out_norm_nsa, w_out, w_router, b_router,
           w_gate, b_gate, w_up, b_up, w_down, b_down, final_norm_g):
    B, S, D = x.shape
    T = B * S
    G = KV_GROUPS
    assert D == D_MODEL and S % ATTN_TK == 0 and S % CMP_STRIDE == 0 and KV_W == LANES
    rc, rs1, rs2, cos_t, sin_t = _rope_tables(S)
    n_sel = S // SEL_BLK
    nc = S // CMP_STRIDE
    cstart = jnp.arange(nc) * CMP_STRIDE
    jstart = jnp.arange(n_sel) * SEL_BLK
    overlap_t = ((cstart[None, :] <= jstart[:, None] + SEL_BLK - 1)
                 & (cstart[None, :] + CMP_LEN - 1 >= jstart[:, None])
                 & (jnp.arange(nc)[None, :] < nc - 1)).astype(F32)

    assert w_ada.shape[0] == 1
    for l in range(1):
        mod = _adaln(c, w_ada[l], b_ada[l][None])
        sh1, sc1, gt1, sh2, sc2, gt2 = [m[:, None, :] for m in jnp.split(mod, 6, axis=-1)]

        wl = w_in[l]
        o = 2 * CONV_CH + NSA_WIDTH
        kvc = [wl[:, o + i * KV_W:o + (i + 1) * KV_W] for i in range(6)]
        gl = wl[:, o + 6 * KV_W:]
        per_g = 3 * Q_PER_G
        gpad = [jnp.pad(gl[:, per_g * g:per_g * (g + 1)], ((0, 0), (0, GATE_ROWS - per_g))) for g in range(G)]
        wn = jnp.concatenate([wl[:, :2 * CONV_CH], kvc[0], kvc[2], kvc[4], kvc[1]], axis=1).astype(BF16)
        wt = jnp.concatenate([wl[:, 2 * CONV_CH:o], kvc[3], kvc[5]] + gpad, axis=1).T.astype(BF16)
        u, kc, vc, ks, kw, qt, vst, vwt, gates = _inproj(x, sc1, sh1, norm1_g[l][None], wn, wt,
                                                         rc, rs1, rs2, cos_t, sin_t)

        conv_n = _conv(u, conv_w[l], conv_b[l][None], conv_ln_g[l][None], conv_ln_b[l][None],
                       out_norm_conv[l][None])

        chunk = CMP_STRIDE * HEAD_DIM
        kcmp, vcmp_t = _compress(kc.reshape(B, G, nc, chunk), vc.reshape(B, G, nc, chunk),
                                 cmp_pe_k[l].reshape(1, -1), cmp_pe_v[l].reshape(1, -1),
                                 cmp_k_w1[l], cmp_k_w2[l], cmp_v_w1[l], cmp_v_w2[l].T)
        nsa = _attention(qt, kcmp, vcmp_t, ks, vst, kw, vwt, gates, overlap_t)

        x1, h2_rows, route, counts = _outproj(x, conv_n, nsa, out_norm_nsa[l][None], w_out[l].astype(BF16), gt1,
                                              norm2_g[l][None], sc2, sh2, w_router[l], b_router[l][None])

        block_e, nb_used, buf_tok, buf_gate, dest_assign = _route_plan(route, counts[:, 0], T)
        y_rows = _ffn(block_e, nb_used, buf_tok, h2_rows.reshape(T, ROW_TILES, LANES), buf_gate,
                      w_gate[l], b_gate[l][:, None, :], w_up[l], b_up[l][:, None, :],
                      w_down[l], b_down[l][:, None, :])
        tm = min(COMB_TM, S)
        dest3 = dest_assign.reshape(T // tm, tm, TOP_K).transpose(0, 2, 1).reshape(T // tm, 1, TOP_K * tm)
        P = y_rows.shape[0] // ROW_TILES
        x = _combine(dest3, y_rows.reshape(P, ROW_TILES, LANES), x1.reshape(T, D), gt2, final_norm_g[None],
                     S).reshape(B, S, D)
    return x
```

```python
import functools

import jax
import jax.numpy as jnp
from jax import lax
from jax.experimental import pallas as pl
from jax.experimental.pallas import tpu as pltpu

F32 = jnp.float32
BF16 = jnp.bfloat16
I32 = jnp.int32
HI = lax.Precision.HIGHEST

D_MODEL = 1024
CONV_CH = 512
CONV_WIDTH = 31
NSA_HEADS = 8
KV_GROUPS = 2
Q_PER_G = NSA_HEADS // KV_GROUPS
HEAD_DIM = 64
NSA_WIDTH = NSA_HEADS * HEAD_DIM
KV_W = KV_GROUPS * HEAD_DIM
ROT_DIM = HEAD_DIM // 4
ROT_HALF = ROT_DIM // 2
ROPE_THETA = 500000.0
CMP_LEN = 32
CMP_STRIDE = 16
CMP_HIDDEN = 128
SEL_BLK = 64
SEL_TOPN = 16
WINDOW = 512
N_EXPERTS = 32
TOP_K = 4
D_FF = 1024
SWIGLU_ALPHA = 1.702
SWIGLU_LIMIT = 7.0
NORM_EPS = 1e-5
NEG_INF = -1e30
FORCE_SCORE = 1e9

LANES = 128
SUBLANES = 8
ROW_TILES = D_MODEL // LANES

GATE_ROWS = 16

INPROJ_TM = 512
CONV_TR = 256
CONV_HALO = 32
ATTN_TQ = 256
ATTN_TK = 256
OUT_TM = 512
FFN_BM = 512
COMB_TM = 256
ROUTE_W = 16
ISSUE_UNROLL = 8


def _rms(x, g):
    return x * lax.rsqrt(jnp.mean(x * x, axis=-1, keepdims=True) + NORM_EPS) * g


def _ada_kernel(c_ref, w_ref, b_ref, o_ref):
    c = c_ref[...]
    ca = c * jax.nn.sigmoid(c)
    o_ref[...] = jnp.dot(ca, w_ref[...], preferred_element_type=F32, precision=HI) + b_ref[...]


def _adaln(c, w, b):
    B = c.shape[0]
    D = D_MODEL
    return pl.pallas_call(
        _ada_kernel,
        grid=(6,),
        in_specs=[pl.BlockSpec((B, D), lambda j: (0, 0)),
                  pl.BlockSpec((D, D), lambda j: (0, j)),
                  pl.BlockSpec((1, D), lambda j: (0, j))],
        out_specs=pl.BlockSpec((B, D), lambda j: (0, j)),
        out_shape=jax.ShapeDtypeStruct((B, 6 * D), F32),
        name="adaln",
    )(c, w, b)


def _inproj_kernel(x_ref, sc_ref, sh_ref, g_ref, wn_ref, wt_ref, rc_ref, rs1_ref, rs2_ref, ct_ref, st_ref,
                   u_ref, kc_ref, vc_ref, ks_ref, kw_ref, qt_ref, vst_ref, vwt_ref, gt_ref):
    tm = x_ref.shape[1]
    tq, tk = ATTN_TQ, ATTN_TK
    h = (_rms(x_ref[0], g_ref[...]) * (1.0 + sc_ref[0]) + sh_ref[0]).astype(BF16)

    p = jnp.dot(h, wn_ref[...], preferred_element_type=F32)
    u_ref[0] = p[:, 0:CONV_CH] * jax.nn.sigmoid(p[:, CONV_CH:2 * CONV_CH])
    c0 = 2 * CONV_CH
    rc, rs1, rs2 = rc_ref[...], rs1_ref[...], rs2_ref[...]
    for ref, roped in ((kc_ref, True), (ks_ref, True), (kw_ref, True), (vc_ref, False)):
        v = p[:, c0:c0 + KV_W]
        if roped:
            v = v * rc + pltpu.roll(v, ROT_HALF, 1) * rs1 + pltpu.roll(v, KV_W - ROT_HALF, 1) * rs2
        for gg in range(KV_GROUPS):
            ref[0, gg] = v[:, HEAD_DIM * gg:HEAD_DIM * (gg + 1)].astype(ref.dtype)
        c0 += KV_W

    pt = lax.dot_general(wt_ref[...], h, (((1,), (1,)), ((), ())), preferred_element_type=F32)
    cos_t, sin_t = ct_ref[...], st_ref[...]
    scale = HEAD_DIM ** -0.5
    for hh in range(NSA_HEADS):
        blk = pt[HEAD_DIM * hh:HEAD_DIM * (hh + 1), :]
        x1, x2 = blk[0:ROT_HALF], blk[ROT_HALF:ROT_DIM]
        qh = (jnp.concatenate([x1 * cos_t - x2 * sin_t, x2 * cos_t + x1 * sin_t, blk[ROT_DIM:]], axis=0)
              * scale).astype(BF16)
        gg, n = divmod(hh, Q_PER_G)
        for j in range(tm // tq):
            qt_ref[0, gg, j, :, n * tq:(n + 1) * tq] = qh[:, j * tq:(j + 1) * tq]
    r0 = NSA_WIDTH
    for ref in (vst_ref, vwt_ref):
        for gg in range(KV_GROUPS):
            blk = pt[r0 + HEAD_DIM * gg:r0 + HEAD_DIM * (gg + 1), :].astype(BF16)
            for j in range(tm // tk):
                ref[0, gg, j] = blk[:, j * tk:(j + 1) * tk]
        r0 += KV_W
    for gg in range(KV_GROUPS):
        gt_ref[0, gg] = jax.nn.sigmoid(pt[r0 + GATE_ROWS * gg:r0 + GATE_ROWS * (gg + 1), :])


def _inproj(x, sc, sh, g, wn, wt, rc, rs1, rs2, cos_t, sin_t):
    B, S, D = x.shape
    tm = min(INPROJ_TM, S)
    tq, tk = ATTN_TQ, ATTN_TK
    G = KV_GROUPS
    kv = lambda dt: jax.ShapeDtypeStruct((B, G, S, HEAD_DIM), dt)
    kv_spec = pl.BlockSpec((1, G, tm, HEAD_DIM), lambda b, i: (b, 0, i, 0))
    vt_shape = jax.ShapeDtypeStruct((B, G, S // tk, HEAD_DIM, tk), BF16)
    vt_spec = pl.BlockSpec((1, G, tm // tk, HEAD_DIM, tk), lambda b, i: (b, 0, i, 0, 0))
    row = pl.BlockSpec((1, 1, D), lambda b, i: (b, 0, 0))
    tab = pl.BlockSpec((tm, LANES), lambda b, i: (i, 0))
    tab_t = pl.BlockSpec((ROT_HALF, tm), lambda b, i: (0, i))
    return pl.pallas_call(
        _inproj_kernel,
        grid=(B, S // tm),
        in_specs=[pl.BlockSpec((1, tm, D), lambda b, i: (b, i, 0)), row, row,
                  pl.BlockSpec((1, D), lambda b, i: (0, 0)),
                  pl.BlockSpec(wn.shape, lambda b, i: (0, 0)),
                  pl.BlockSpec(wt.shape, lambda b, i: (0, 0)),
                  tab, tab, tab, tab_t, tab_t],
        out_specs=[pl.BlockSpec((1, tm, CONV_CH), lambda b, i: (b, i, 0)),
                   kv_spec, kv_spec, kv_spec, kv_spec,
                   pl.BlockSpec((1, G, tm // tq, HEAD_DIM, Q_PER_G * tq), lambda b, i: (b, 0, i, 0, 0)),
                   vt_spec, vt_spec,
                   pl.BlockSpec((1, G, GATE_ROWS, tm), lambda b, i: (b, 0, 0, i))],
        out_shape=[jax.ShapeDtypeStruct((B, S, CONV_CH), F32),
                   kv(F32), kv(F32), kv(BF16), kv(BF16),
                   jax.ShapeDtypeStruct((B, G, S // tq, HEAD_DIM, Q_PER_G * tq), BF16),
                   vt_shape, vt_shape,
                   jax.ShapeDtypeStruct((B, G, GATE_ROWS, S), F32)],
        compiler_params=pltpu.CompilerParams(dimension_semantics=("parallel", "parallel")),
        name="inproj",
    )(x, sc, sh, g, wn, wt, rc, rs1, rs2, cos_t, sin_t)


def _conv_kernel(prev_ref, cur_ref, w_ref, cb_ref, lg_ref, lb_ref, on_ref, o_ref, pad_ref):
    tr = cur_ref.shape[1]
    first = pl.program_id(1) == 0
    halo = prev_ref[0, tr - CONV_HALO:tr, :]
    pad_ref[0:CONV_HALO, :] = jnp.where(first, 0.0, halo)
    pad_ref[CONV_HALO:CONV_HALO + tr, :] = cur_ref[0]
    off = CONV_HALO - (CONV_WIDTH - 1)
    acc = jnp.zeros((tr, CONV_CH), F32)
    for k in range(CONV_WIDTH):
        acc = acc + pad_ref[off + k:off + k + tr, :] * w_ref[k:k + 1, :]
    y = acc + cb_ref[...]
    mu = jnp.mean(y, axis=-1, keepdims=True)
    yc = y - mu
    var = jnp.mean(yc * yc, axis=-1, keepdims=True)
    yn = yc * lax.rsqrt(var + NORM_EPS) * lg_ref[...] + lb_ref[...]
    s = yn * jax.nn.sigmoid(yn)
    o_ref[0] = _rms(s, on_ref[...]).astype(o_ref.dtype)


def _conv(u, w, cb, lg, lb, on):
    B, S, C = u.shape
    tr = min(CONV_TR, S)
    vec = pl.BlockSpec((1, C), lambda b, i: (0, 0))
    return pl.pallas_call(
        _conv_kernel,
        grid=(B, S // tr),
        in_specs=[pl.BlockSpec((1, tr, C), lambda b, i: (b, jnp.maximum(i - 1, 0), 0)),
                  pl.BlockSpec((1, tr, C), lambda b, i: (b, i, 0)),
                  pl.BlockSpec((CONV_WIDTH, C), lambda b, i: (0, 0)),
                  vec, vec, vec, vec],
        out_specs=pl.BlockSpec((1, tr, C), lambda b, i: (b, i, 0)),
        out_shape=jax.ShapeDtypeStruct((B, S, C), BF16),
        scratch_shapes=[pltpu.VMEM((CONV_HALO + tr, C), F32)],
        compiler_params=pltpu.CompilerParams(dimension_semantics=("parallel", "parallel")),
        name="conv",
    )(u, u, w, cb, lg, lb, on)


def _cmp_kernel(kx_ref, vx_ref, pek_ref, pev_ref, kw1_ref, kw2_ref, vw1_ref, vw2t_ref, ko_ref, vo_ref):
    nc = kx_ref.shape[2]
    half = kx_ref.shape[3]
    nt = (((1,), (1,)), ((), ()))
    for x_ref, pe_ref, w1_ref, w2_ref, o_ref, transposed in ((kx_ref, pek_ref, kw1_ref, kw2_ref, ko_ref, False),
                                                             (vx_ref, pev_ref, vw1_ref, vw2t_ref, vo_ref, True)):
        w1 = w1_ref[...]
        pe = jnp.broadcast_to(pe_ref[...], (SUBLANES, 2 * half))
        pe_proj = jnp.dot(pe, w1, preferred_element_type=F32, precision=HI)[0:1]
        for gg in range(KV_GROUPS):
            xg = x_ref[0, gg]
            first = jnp.dot(xg, w1[0:half], preferred_element_type=F32, precision=HI)
            second = jnp.dot(xg, w1[half:2 * half], preferred_element_type=F32, precision=HI)
            hid = first + pltpu.roll(second, nc - 1, 0) + pe_proj
            hid = hid * jax.nn.sigmoid(hid)
            if transposed:
                o_ref[0, gg] = lax.dot_general(w2_ref[...], hid, nt, preferred_element_type=F32, precision=HI)
            else:
                o_ref[0, gg] = jnp.dot(hid, w2_ref[...], preferred_element_type=F32, precision=HI)


def _compress(kx, vx, pek, pev, kw1, kw2, vw1, vw2t):
    B, G, NC, W = kx.shape
    xs = pl.BlockSpec((1, G, NC, W), lambda b: (b, 0, 0, 0))
    full = lambda a: pl.BlockSpec(a.shape, lambda b: (0,) * a.ndim)
    return pl.pallas_call(
        _cmp_kernel,
        grid=(B,),
        in_specs=[xs, xs, full(pek), full(pev), full(kw1), full(kw2), full(vw1), full(vw2t)],
        out_specs=[pl.BlockSpec((1, G, NC, HEAD_DIM), lambda b: (b, 0, 0, 0)),
                   pl.BlockSpec((1, G, HEAD_DIM, NC), lambda b: (b, 0, 0, 0))],
        out_shape=[jax.ShapeDtypeStruct((B, G, NC, HEAD_DIM), F32),
                   jax.ShapeDtypeStruct((B, G, HEAD_DIM, NC), F32)],
        compiler_params=pltpu.CompilerParams(dimension_semantics=("parallel",)),
        name="compress",
    )(kx, vx, pek, pev, kw1, kw2, vw1, vw2t)


def _attn_kernel(qt_ref, kc_ref, vct_ref, ks_ref, vst_ref, kw_ref, vwt_ref, gt_ref, ovt_ref, o_ref, sel_ref):
    tq, tk = ATTN_TQ, ATTN_TK
    R = Q_PER_G * tq
    per_tile = tk // SEL_BLK
    S = ks_ref.shape[2]
    ncp = kc_ref.shape[2]
    nsel = ovt_ref.shape[0]
    qi = pl.program_id(2)
    q0 = qi * tq
    qt = qt_ref[0, 0, 0]
    t_row = q0 + lax.broadcasted_iota(I32, (1, tq), 1)
    heads = lambda a: jnp.concatenate([a] * Q_PER_G, axis=1)

    sc = jnp.dot(kc_ref[0, 0].astype(BF16), qt, preferred_element_type=F32)
    c_io = lax.broadcasted_iota(I32, (ncp, tq), 0)
    m_c = (c_io * CMP_STRIDE + (CMP_LEN - 1) <= t_row) & (c_io < ncp - 1)
    scb = sc + heads(jnp.where(m_c, 0.0, NEG_INF))
    e = jnp.exp(scb - jnp.max(scb, axis=0, keepdims=True)) * heads(jnp.where(m_c, 1.0, 0.0))
    den = jnp.sum(e, axis=0, keepdims=True)
    pc = e / jnp.where(den > 0.0, den, 1.0)
    o_cmp = jnp.dot(vct_ref[0, 0].astype(BF16), pc.astype(BF16), preferred_element_type=F32)

    psum = pc[:, 0:tq]
    for n in range(1, Q_PER_G):
        psum = psum + pc[:, n * tq:(n + 1) * tq]
    imp = jnp.dot(ovt_ref[...], psum, preferred_element_type=F32, precision=HI)
    j_io = lax.broadcasted_iota(I32, (nsel, tq), 0)
    cur = t_row // SEL_BLK
    valid = j_io * SEL_BLK <= t_row
    forced = (j_io == 0) | (j_io == cur) | (j_io == cur - 1)
    score = jnp.where(valid, jnp.where(forced, FORCE_SCORE, imp), NEG_INF)
    rank = jnp.zeros((nsel, tq), F32)
    for i in range(nsel):
        row = score[i:i + 1, :]
        tie = jnp.where(j_io > i, 1.0, 0.0)
        rank = rank + jnp.where(row > score, 1.0, jnp.where(row == score, tie, 0.0))
    sel_bias = jnp.where(rank < float(min(SEL_TOPN, nsel)), 0.0, NEG_INF)
    sel_ref[...] = jnp.zeros(sel_ref.shape, F32)
    for jj in range(S // tk):
        sel_ref[jj, 0:per_tile, :] = sel_bias[jj * per_tile:(jj + 1) * per_tile, :]

    k_io = lax.broadcasted_iota(I32, (tk, tq), 0)

    def flash_step(k_ref, vt_ref, kj, bias, carry):
        m, l, acc = carry
        k0 = pl.multiple_of(kj * tk, tk)
        s = jnp.dot(k_ref[0, 0, pl.ds(k0, tk), :], qt, preferred_element_type=F32) + heads(bias)
        m_new = jnp.maximum(m, jnp.max(s, axis=0, keepdims=True))
        alpha = jnp.exp(m - m_new)
        p = jnp.exp(s - m_new)
        l = alpha * l + jnp.sum(p, axis=0, keepdims=True)
        acc = alpha * acc + jnp.dot(vt_ref[0, 0, kj], p.astype(BF16), preferred_element_type=F32)
        return m_new, l, acc

    init = (jnp.full((1, R), NEG_INF, F32), jnp.zeros((1, R), F32), jnp.zeros((HEAD_DIM, R), F32))

    def slc_body(kj, carry):
        blocks = sel_ref[kj]
        bias = jnp.concatenate([jnp.broadcast_to(blocks[b:b + 1, :], (SEL_BLK, tq)) for b in range(per_tile)], axis=0)
        bias = jnp.where(kj * tk + k_io <= t_row, bias, NEG_INF)
        return flash_step(ks_ref, vst_ref, kj, bias, carry)

    def win_body(kj, carry):
        rel = t_row - (kj * tk + k_io)
        bias = jnp.where((rel >= 0) & (rel < WINDOW), 0.0, NEG_INF)
        return flash_step(kw_ref, vwt_ref, kj, bias, carry)

    n_slc = (q0 + tq + tk - 1) // tk
    lo_tile = jnp.maximum(q0 - (WINDOW - 1), 0) // tk
    n_pair = lo_tile // 2

    def pair_body(i, carry):
        return slc_body(2 * i, carry[0]), slc_body(2 * i + 1, carry[1])

    st_a, st_b = lax.fori_loop(0, n_pair, pair_body, (init, init))
    st_a = lax.fori_loop(2 * n_pair, lo_tile, slc_body, st_a)

    def both_body(kj, carry):
        return slc_body(kj, carry[0]), win_body(kj, carry[1])

    st_a, (_, l_w, acc_w) = lax.fori_loop(lo_tile, n_slc, both_body, (st_a, init))
    m_s = jnp.maximum(st_a[0], st_b[0])
    w_a, w_b = jnp.exp(st_a[0] - m_s), jnp.exp(st_b[0] - m_s)
    l_s = w_a * st_a[1] + w_b * st_b[1]
    acc_s = w_a * st_a[2] + w_b * st_b[2]

    gt = gt_ref[0, 0]
    o_slc = acc_s / l_s
    o_win = acc_w / l_w
    outs = []
    for n in range(Q_PER_G):
        cols = slice(n * tq, (n + 1) * tq)
        outs.append(gt[3 * n:3 * n + 1, :] * o_cmp[:, cols] + gt[3 * n + 1:3 * n + 2, :] * o_slc[:, cols]
                    + gt[3 * n + 2:3 * n + 3, :] * o_win[:, cols])
    o_ref[0] = jnp.concatenate(outs, axis=0).T


def _attention(qt, kc, vct, ks, vst, kw, vwt, gt, overlap_t):
    B, G, NQT, _, R = qt.shape
    S = ks.shape[2]
    tq, tk = ATTN_TQ, ATTN_TK
    ncp = kc.shape[2]
    kfull = pl.BlockSpec((1, 1, S, HEAD_DIM), lambda b, g, i: (b, g, 0, 0))
    vfull = pl.BlockSpec((1, 1, S // tk, HEAD_DIM, tk), lambda b, g, i: (b, g, 0, 0, 0))
    return pl.pallas_call(
        _attn_kernel,
        grid=(B, G, NQT),
        in_specs=[pl.BlockSpec((1, 1, 1, HEAD_DIM, R), lambda b, g, i: (b, g, i, 0, 0)),
                  pl.BlockSpec((1, 1, ncp, HEAD_DIM), lambda b, g, i: (b, g, 0, 0)),
                  pl.BlockSpec((1, 1, HEAD_DIM, ncp), lambda b, g, i: (b, g, 0, 0)),
                  kfull, vfull, kfull, vfull,
                  pl.BlockSpec((1, 1, GATE_ROWS, tq), lambda b, g, i: (b, g, 0, i)),
                  pl.BlockSpec(overlap_t.shape, lambda b, g, i: (0, 0))],
        out_specs=pl.BlockSpec((1, tq, Q_PER_G * HEAD_DIM), lambda b, g, i: (b, i, g)),
        out_shape=jax.ShapeDtypeStruct((B, S, NSA_WIDTH), F32),
        scratch_shapes=[pltpu.VMEM((S // tk, SUBLANES, tq), F32)],
        compiler_params=pltpu.CompilerParams(dimension_semantics=("parallel", "parallel", "arbitrary")),
        name="attn",
    )(qt, kc, vct, ks, vst, kw, vwt, gt, overlap_t)


def _outproj_kernel(x_ref, cv_ref, nsa_ref, on_ref, w_ref, gt1_ref, g2_ref, sc2_ref, sh2_ref, wrh_ref, wrl_ref, br_ref,
                    tri_ref, x1_ref, h2_ref, rt_ref, cnt_ref, run_ref):
    tm = x_ref.shape[1]

    @pl.when((pl.program_id(0) == 0) & (pl.program_id(1) == 0))
    def _():
        run_ref[...] = jnp.zeros(run_ref.shape, F32)

    nn = _rms(nsa_ref[0], on_ref[...]).astype(BF16)
    y = (jnp.dot(cv_ref[0], w_ref[0:CONV_CH, :], preferred_element_type=F32)
         + jnp.dot(nn, w_ref[CONV_CH:CONV_CH + NSA_WIDTH, :], preferred_element_type=F32))
    x1 = x_ref[0] + gt1_ref[0] * y
    x1_ref[0] = x1
    h2 = _rms(x1, g2_ref[...]) * (1.0 + sc2_ref[0]) + sh2_ref[0]
    for s in range(ROW_TILES):
        h2_ref[pl.ds(s, tm, stride=ROW_TILES), :] = h2[:, s * LANES:(s + 1) * LANES]
    nt = (((1,), (1,)), ((), ()))
    h_hi = h2.astype(BF16)
    h_lo = (h2 - h_hi.astype(F32)).astype(BF16)
    logits = (lax.dot_general(wrh_ref[...], h_hi, nt, preferred_element_type=F32)
              + lax.dot_general(wrh_ref[...], h_lo, nt, preferred_element_type=F32)
              + lax.dot_general(wrl_ref[...], h_hi, nt, preferred_element_type=F32)) + br_ref[...]
    eio = lax.broadcasted_iota(I32, (N_EXPERTS, tm), 0).astype(F32)
    vals, idxs = [], []
    for _ in range(TOP_K):
        m = jnp.max(logits, axis=0, keepdims=True)
        ix = jnp.min(jnp.where(logits == m, eio, float(N_EXPERTS)), axis=0, keepdims=True)
        vals.append(m)
        idxs.append(ix)
        logits = jnp.where(eio == ix, -jnp.inf, logits)
    es = [jnp.exp(v - vals[0]) for v in vals]
    den = es[0] + es[1] + es[2] + es[3]
    hot = jnp.zeros((N_EXPERTS, tm), F32)
    for r in range(TOP_K):
        hot = hot + jnp.where(eio == idxs[r], 1.0, 0.0)
    before = run_ref[...] + jnp.dot(hot.astype(BF16), tri_ref[...], preferred_element_type=F32)
    ranks = [jnp.sum(jnp.where(eio == idxs[r], before, 0.0), axis=0, keepdims=True) for r in range(TOP_K)]
    run_ref[...] = run_ref[...] + jnp.sum(hot, axis=1, keepdims=True)
    cnt_ref[...] = run_ref[...]
    rio = lax.broadcasted_iota(I32, (ROUTE_W, tm), 0)
    out = jnp.zeros((ROUTE_W, tm), F32)
    for r in range(TOP_K):
        out = jnp.where(rio == r, idxs[r], out)
        out = jnp.where(rio == TOP_K + r, es[r] / den, out)
        out = jnp.where(rio == 2 * TOP_K + r, ranks[r], out)
    rt_ref[0] = out


def _outproj(x, cv, nsa, on, w, gt1, g2, sc2, sh2, wr, br):
    B, S, D = x.shape
    tm = min(OUT_TM, S)
    nt = S // tm
    tri = (jnp.arange(tm)[:, None] < jnp.arange(tm)[None, :]).astype(BF16)
    wr_t = wr.T
    wr_hi = wr_t.astype(BF16)
    wr_lo = (wr_t - wr_hi.astype(F32)).astype(BF16)
    row = pl.BlockSpec((1, 1, D), lambda b, i: (b, 0, 0))
    vec = lambda n: pl.BlockSpec((1, n), lambda b, i: (0, 0))
    col = pl.BlockSpec((N_EXPERTS, 1), lambda b, i: (0, 0))
    wr_spec = pl.BlockSpec((N_EXPERTS, D), lambda b, i: (0, 0))
    return pl.pallas_call(
        _outproj_kernel,
        grid=(B, nt),
        in_specs=[pl.BlockSpec((1, tm, D), lambda b, i: (b, i, 0)),
                  pl.BlockSpec((1, tm, CONV_CH), lambda b, i: (b, i, 0)),
                  pl.BlockSpec((1, tm, NSA_WIDTH), lambda b, i: (b, i, 0)),
                  vec(NSA_WIDTH),
                  pl.BlockSpec((D, D), lambda b, i: (0, 0)),
                  row, vec(D), row, row,
                  wr_spec, wr_spec, col,
                  pl.BlockSpec((tm, tm), lambda b, i: (0, 0))],
        out_specs=[pl.BlockSpec((1, tm, D), lambda b, i: (b, i, 0)),
                   pl.BlockSpec((tm * ROW_TILES, LANES), lambda b, i: (b * nt + i, 0)),
                   pl.BlockSpec((1, ROUTE_W, tm), lambda b, i: (b, 0, i)),
                   col],
        out_shape=[jax.ShapeDtypeStruct((B, S, D), F32),
                   jax.ShapeDtypeStruct((B * S * ROW_TILES, LANES), F32),
                   jax.ShapeDtypeStruct((B, ROUTE_W, S), F32),
                   jax.ShapeDtypeStruct((N_EXPERTS, 1), F32)],
        scratch_shapes=[pltpu.VMEM((N_EXPERTS, 1), F32)],
        compiler_params=pltpu.CompilerParams(dimension_semantics=("arbitrary", "arbitrary")),
        name="outproj",
    )(x, cv, nsa, on, w, gt1, g2, sc2, sh2, wr_hi, wr_lo, br.reshape(N_EXPERTS, 1), tri)


def _issue_rows(idx_ref, n, src_hbm, dst, slot, sem):
    assert n % ISSUE_UNROLL == 0

    def body(c, carry):
        for u in range(ISSUE_UNROLL):
            r = c * ISSUE_UNROLL + u
            pltpu.make_async_copy(src_hbm.at[idx_ref[0, 0, r]],
                                  dst.at[slot, pl.ds(pl.multiple_of(r * ROW_TILES, ROW_TILES), ROW_TILES), :],
                                  sem.at[slot]).start(priority=u % 2)
        return carry
    lax.fori_loop(0, n // ISSUE_UNROLL, body, 0)


def _wait_rows(dst, slot, sem):
    pltpu.make_async_copy(dst.at[slot], dst.at[slot], sem.at[slot]).wait()


def _rows_2d(buf, slot, base, n):
    return jnp.concatenate(
        [buf[slot, pl.ds(base * ROW_TILES + s, n, stride=ROW_TILES), :] for s in range(ROW_TILES)], axis=1)


def _ffn_kernel(be_ref, nb_ref, ta_ref, tb_ref, h2_hbm, gate_ref, wg_ref, bg_ref, wu_ref, bu_ref, wd_ref, bd_ref,
                o_ref, xbuf, wgb, wub, wdb, sem):
    bm = ta_ref.shape[2]
    i = pl.program_id(0)
    nb = nb_ref[0]
    slot = i % 2

    @pl.when((i == 0) | (be_ref[i] != be_ref[jnp.maximum(i - 1, 0)]))
    def _():
        wgb[...] = wg_ref[0].astype(BF16)
        wub[...] = wu_ref[0].astype(BF16)
        wdb[...] = wd_ref[0].astype(BF16)

    @pl.when(i == 0)
    def _():
        _issue_rows(ta_ref, bm, h2_hbm, xbuf, 0, sem)

    @pl.when(i + 1 < nb)
    def _():
        _issue_rows(tb_ref, bm, h2_hbm, xbuf, 1 - slot, sem)

    @pl.when(i < nb)
    def _():
        _wait_rows(xbuf, slot, sem)
        x = _rows_2d(xbuf, slot, 0, bm).astype(BF16)
        g = jnp.dot(x, wgb[...], preferred_element_type=F32) + bg_ref[0]
        u = jnp.dot(x, wub[...], preferred_element_type=F32) + bu_ref[0]
        g = jnp.minimum(g, SWIGLU_LIMIT)
        u = jnp.clip(u, -SWIGLU_LIMIT, SWIGLU_LIMIT)
        act = g * jax.nn.sigmoid(SWIGLU_ALPHA * g) * (u + 1.0)
        y = (jnp.dot(act.astype(BF16), wdb[...], preferred_element_type=F32) + bd_ref[0]) * gate_ref[...]
        for s in range(ROW_TILES):
            o_ref[pl.ds(s, bm, stride=ROW_TILES), :] = y[:, s * LANES:(s + 1) * LANES]

    @pl.when(i >= nb)
    def _():
        o_ref[...] = jnp.zeros(o_ref.shape, o_ref.dtype)


def _ffn(block_e, nb_used, buf_tok, h2_rows, buf_gate, wg, bg, wu, bu, wd, bd):
    NB = block_e.shape[0]
    bm = FFN_BM
    D, F = D_MODEL, D_FF
    tok3 = buf_tok.reshape(NB, 1, bm)
    wspec = lambda r, c: pl.BlockSpec((1, r, c), lambda i, be, nb: (be[i], 0, 0))
    vmem_limit = 2 * 3 * D * F * 4 + 3 * D * F * 2 + 4 * bm * D * 4 + 6 * bm * F * 4
    return pl.pallas_call(
        _ffn_kernel,
        grid_spec=pltpu.PrefetchScalarGridSpec(
            num_scalar_prefetch=2,
            grid=(NB,),
            in_specs=[pl.BlockSpec((1, 1, bm), lambda i, be, nb: (i, 0, 0), memory_space=pltpu.SMEM),
                      pl.BlockSpec((1, 1, bm), lambda i, be, nb: (jnp.minimum(i + 1, NB - 1), 0, 0),
                                   memory_space=pltpu.SMEM),
                      pl.BlockSpec(memory_space=pl.ANY),
                      pl.BlockSpec((bm, 1), lambda i, be, nb: (i, 0)),
                      wspec(D, F), wspec(1, F), wspec(D, F), wspec(1, F), wspec(F, D), wspec(1, D)],
            out_specs=pl.BlockSpec((bm * ROW_TILES, LANES), lambda i, be, nb: (i, 0)),
            scratch_shapes=[pltpu.VMEM((2, bm * ROW_TILES, LANES), F32),
                            pltpu.VMEM((D, F), BF16), pltpu.VMEM((D, F), BF16), pltpu.VMEM((F, D), BF16),
                            pltpu.SemaphoreType.DMA((2,))]),
        out_shape=jax.ShapeDtypeStruct((NB * bm * ROW_TILES, LANES), F32),
        compiler_params=pltpu.CompilerParams(dimension_semantics=("arbitrary",),
                                             vmem_limit_bytes=vmem_limit),
        name="ffn",
    )(block_e, nb_used, tok3, tok3, h2_rows, buf_gate, wg, bg, wu, bu, wd, bd)


def _combine_kernel(da_ref, db_ref, y_hbm, x1_ref, gt2_ref, fg_ref, o_ref, buf, sem):
    tm = x1_ref.shape[0]
    n = TOP_K * tm
    i = pl.program_id(0)
    slot = i % 2

    @pl.when(i == 0)
    def _():
        _issue_rows(da_ref, n, y_hbm, buf, 0, sem)

    @pl.when(i + 1 < pl.num_programs(0))
    def _():
        _issue_rows(db_ref, n, y_hbm, buf, 1 - slot, sem)

    _wait_rows(buf, slot, sem)
    y = _rows_2d(buf, slot, 0, tm)
    for k in range(1, TOP_K):
        y = y + _rows_2d(buf, slot, k * tm, tm)
    x2 = x1_ref[...] + gt2_ref[0] * y
    o_ref[...] = _rms(x2, fg_ref[...])


def _combine(dest3, y_rows, x1, gt2, fg, S):
    T, D = x1.shape
    tm = min(COMB_TM, S)
    NT = T // tm
    per_b = S // tm
    n = TOP_K * tm
    return pl.pallas_call(
        _combine_kernel,
        grid=(NT,),
        in_specs=[pl.BlockSpec((1, 1, n), lambda i: (i, 0, 0), memory_space=pltpu.SMEM),
                  pl.BlockSpec((1, 1, n), lambda i: (jnp.minimum(i + 1, NT - 1), 0, 0), memory_space=pltpu.SMEM),
                  pl.BlockSpec(memory_space=pl.ANY),
                  pl.BlockSpec((tm, D), lambda i: (i, 0)),
                  pl.BlockSpec((1, 1, D), lambda i: (i // per_b, 0, 0)),
                  pl.BlockSpec((1, D), lambda i: (0, 0))],
        out_specs=pl.BlockSpec((tm, D), lambda i: (i, 0)),
        out_shape=jax.ShapeDtypeStruct((T, D), F32),
        scratch_shapes=[pltpu.VMEM((2, n * ROW_TILES, LANES), F32), pltpu.SemaphoreType.DMA((2,))],
        compiler_params=pltpu.CompilerParams(dimension_semantics=("arbitrary",)),
        name="combine",
    )(dest3, dest3, y_rows, x1, gt2, fg)


def _rope_tables(S):
    inv = ROPE_THETA ** (-jnp.arange(0, ROT_DIM, 2, dtype=F32) / ROT_DIM)
    ang = jnp.arange(S, dtype=F32)[:, None] * inv[None, :]
    cos, sin = jnp.cos(ang), jnp.sin(ang)
    d = jnp.arange(KV_W) % HEAD_DIM
    first, second = d < ROT_HALF, (d >= ROT_HALF) & (d < ROT_DIM)
    cos_l = cos[:, d % ROT_HALF]
    sin_l = sin[:, d % ROT_HALF]
    rc = jnp.where((d < ROT_DIM)[None], cos_l, 1.0)
    rs1 = jnp.where(second[None], sin_l, 0.0)
    rs2 = jnp.where(first[None], -sin_l, 0.0)
    return rc, rs1, rs2, cos.T, sin.T


def _route_plan(route, counts, T):
    bm = FFN_BM
    A = T * TOP_K
    tok_major = lambda r: jnp.swapaxes(r, 1, 2).reshape(T, TOP_K)
    flat_e = tok_major(route[:, 0:TOP_K]).astype(I32).reshape(A)
    flat_g = tok_major(route[:, TOP_K:2 * TOP_K]).reshape(A)
    rank = tok_major(route[:, 2 * TOP_K:3 * TOP_K]).astype(I32)
    order = jnp.argsort(flat_e, stable=True).astype(I32)
    counts = counts.astype(I32)
    starts = jnp.cumsum(counts) - counts
    padded = (counts + bm - 1) // bm * bm
    pends = jnp.cumsum(padded)
    pstarts = pends - padded
    P = (A + N_EXPERTS * bm + bm - 1) // bm * bm
    NB = P // bm
    blk0 = jnp.arange(NB, dtype=I32) * bm
    block_e = jnp.minimum(jnp.sum((pends[None, :] <= blk0[:, None]).astype(I32), axis=1), N_EXPERTS - 1)
    r = (blk0 - pstarts[block_e])[:, None] + jnp.arange(bm, dtype=I32)[None, :]
    valid = r < counts[block_e][:, None]
    a_p = order[jnp.clip(starts[block_e][:, None] + r, 0, A - 1)]
    buf_tok = jnp.where(valid, a_p // TOP_K, 0)
    buf_gate = jnp.where(valid, flat_g[a_p], 0.0).reshape(P, 1)
    dest_assign = pstarts[flat_e].reshape(T, TOP_K) + rank
    nb_used = (pends[-1] // bm).astype(I32).reshape(1)
    return block_e, nb_used, buf_tok, buf_gate, dest_assign


def kernel(x, c, norm1_g, norm2_g, w_ada, b_ada, w_in, conv_w, conv_b, conv_ln_g, conv_ln_b, cmp_pe_k, cmp_pe_v,
           cmp_k_w1, cmp_k_w2, cmp_v_w1, cmp_v_w2, out_norm_conv, out_norm_nsa, w_out, w_router, b_router,
           w_gate, b_gate, w_up, b_up, w_down, b_down, final_norm_g):
    B, S, D = x.shape
    T = B * S
    G = KV_GROUPS
    assert D == D_MODEL and S % ATTN_TK == 0 and S % CMP_STRIDE == 0 and KV_W == LANES
    rc, rs1, rs2, cos_t, sin_t = _rope_tables(S)
    n_sel = S // SEL_BLK
    nc = S // CMP_STRIDE
    cstart = jnp.arange(nc) * CMP_STRIDE
    jstart = jnp.arange(n_sel) * SEL_BLK
    overlap_t = ((cstart[None, :] <= jstart[:, None] + SEL_BLK - 1)
                 & (cstart[None, :] + CMP_LEN - 1 >= jstart[:, None])
                 & (jnp.arange(nc)[None, :] < nc - 1)).astype(F32)

    assert w_ada.shape[0] == 1
    for l in range(1):
        mod = _adaln(c, w_ada[l], b_ada[l][None])
        sh1, sc1, gt1, sh2, sc2, gt2 = [m[:, None, :] for m in jnp.split(mod, 6, axis=-1)]

        wl = w_in[l]
        o = 2 * CONV_CH + NSA_WIDTH
        kvc = [wl[:, o + i * KV_W:o + (i + 1) * KV_W] for i in range(6)]
        gl = wl[:, o + 6 * KV_W:]
        per_g = 3 * Q_PER_G
        gpad = [jnp.pad(gl[:, per_g * g:per_g * (g + 1)], ((0, 0), (0, GATE_ROWS - per_g))) for g in range(G)]
        wn = jnp.concatenate([wl[:, :2 * CONV_CH], kvc[0], kvc[2], kvc[4], kvc[1]], axis=1).astype(BF16)
        wt = jnp.concatenate([wl[:, 2 * CONV_CH:o], kvc[3], kvc[5]] + gpad, axis=1).T.astype(BF16)
        u, kc, vc, ks, kw, qt, vst, vwt, gates = _inproj(x, sc1, sh1, norm1_g[l][None], wn, wt,
                                                         rc, rs1, rs2, cos_t, sin_t)

        conv_n = _conv(u, conv_w[l], conv_b[l][None], conv_ln_g[l][None], conv_ln_b[l][None],
                       out_norm_conv[l][None])

        chunk = CMP_STRIDE * HEAD_DIM
        kcmp, vcmp_t = _compress(kc.reshape(B, G, nc, chunk), vc.reshape(B, G, nc, chunk),
                                 cmp_pe_k[l].reshape(1, -1), cmp_pe_v[l].reshape(1, -1),
                                 cmp_k_w1[l], cmp_k_w2[l], cmp_v_w1[l], cmp_v_w2[l].T)
        nsa = _attention(qt, kcmp, vcmp_t, ks, vst, kw, vwt, gates, overlap_t)

        x1, h2_rows, route, counts = _outproj(x, conv_n, nsa, out_norm_nsa[l][None], w_out[l].astype(BF16), gt1,
                                              norm2_g[l][None], sc2, sh2, w_router[l], b_router[l][None])

        block_e, nb_used, buf_tok, buf_gate, dest_assign = _route_plan(route, counts[:, 0], T)
        y_rows = _ffn(block_e, nb_used, buf_tok, h2_rows.reshape(T, ROW_TILES, LANES), buf_gate,
                      w_gate[l], b_gate[l][:, None, :], w_up[l], b_up[l][:, None, :],
                      w_down[l], b_down[l][:, None, :])
        tm = min(COMB_TM, S)
        dest3 = dest_assign.reshape(T // tm, tm, TOP_K).transpose(0, 2, 1).reshape(T // tm, 1, TOP_K * tm)
        P = y_rows.shape[0] // ROW_TILES
        x = _combine(dest3, y_rows.reshape(P, ROW_TILES, LANES), x1.reshape(T, D), gt2, final_norm_g[None],
                     S).reshape(B, S, D)
    return x
```

```python
import functools

import jax
import jax.numpy as jnp
from jax import lax
from jax.experimental import pallas as pl
from jax.experimental.pallas import tpu as pltpu

F32 = jnp.float32
BF16 = jnp.bfloat16
I32 = jnp.int32
HI = lax.Precision.HIGHEST

D_MODEL = 1024
CONV_CH = 512
CONV_WIDTH = 31
NSA_HEADS = 8
KV_GROUPS = 2
Q_PER_G = NSA_HEADS // KV_GROUPS
HEAD_DIM = 64
NSA_WIDTH = NSA_HEADS * HEAD_DIM
KV_W = KV_GROUPS * HEAD_DIM
ROT_DIM = HEAD_DIM // 4
ROT_HALF = ROT_DIM // 2
ROPE_THETA = 500000.0
CMP_LEN = 32
CMP_STRIDE = 16
CMP_HIDDEN = 128
SEL_BLK = 64
SEL_TOPN = 16
WINDOW = 512
N_EXPERTS = 32
TOP_K = 4
D_FF = 1024
SWIGLU_ALPHA = 1.702
SWIGLU_LIMIT = 7.0
NORM_EPS = 1e-5
NEG_INF = -1e30
FORCE_SCORE = 1e9

LANES = 128
SUBLANES = 8
ROW_TILES = D_MODEL // LANES

GATE_ROWS = 16

INPROJ_TM = 512
CONV_TR = 256
CONV_HALO = 32
ATTN_TQ = 256
ATTN_TK = 256
OUT_TM = 512
FFN_BM = 512
COMB_TM = 256
ROUTE_W = 16
ISSUE_UNROLL = 8


def _rms(x, g):
    return x * lax.rsqrt(jnp.mean(x * x, axis=-1, keepdims=True) + NORM_EPS) * g


def _ada_kernel(c_ref, w_ref, b_ref, o_ref):
    c = c_ref[...]
    ca = c * jax.nn.sigmoid(c)
    o_ref[...] = jnp.dot(ca, w_ref[...], preferred_element_type=F32, precision=HI) + b_ref[...]


def _adaln(c, w, b):
    B = c.shape[0]
    D = D_MODEL
    return pl.pallas_call(
        _ada_kernel,
        grid=(6,),
        in_specs=[pl.BlockSpec((B, D), lambda j: (0, 0)),
                  pl.BlockSpec((D, D), lambda j: (0, j)),
                  pl.BlockSpec((1, D), lambda j: (0, j))],
        out_specs=pl.BlockSpec((B, D), lambda j: (0, j)),
        out_shape=jax.ShapeDtypeStruct((B, 6 * D), F32),
        name="adaln",
    )(c, w, b)


def _inproj_kernel(x_ref, sc_ref, sh_ref, g_ref, wn_ref, wt_ref, rc_ref, rs1_ref, rs2_ref, ct_ref, st_ref,
                   u_ref, kc_ref, vc_ref, ks_ref, kw_ref, qt_ref, vst_ref, vwt_ref, gt_ref):
    tm = x_ref.shape[1]
    tq, tk = ATTN_TQ, ATTN_TK
    h = (_rms(x_ref[0], g_ref[...]) * (1.0 + sc_ref[0]) + sh_ref[0]).astype(BF16)

    p = jnp.dot(h, wn_ref[...], preferred_element_type=F32)
    u_ref[0] = p[:, 0:CONV_CH] * jax.nn.sigmoid(p[:, CONV_CH:2 * CONV_CH])
    c0 = 2 * CONV_CH
    rc, rs1, rs2 = rc_ref[...], rs1_ref[...], rs2_ref[...]
    for ref, roped in ((kc_ref, True), (ks_ref, True), (kw_ref, True), (vc_ref, False)):
        v = p[:, c0:c0 + KV_W]
        if roped:
            v = v * rc + pltpu.roll(v, ROT_HALF, 1) * rs1 + pltpu.roll(v, KV_W - ROT_HALF, 1) * rs2
        for gg in range(KV_GROUPS):
            ref[0, gg] = v[:, HEAD_DIM * gg:HEAD_DIM * (gg + 1)].astype(ref.dtype)
        c0 += KV_W

    pt = lax.dot_general(wt_ref[...], h, (((1,), (1,)), ((), ())), preferred_element_type=F32)
    cos_t, sin_t = ct_ref[...], st_ref[...]
    scale = HEAD_DIM ** -0.5
    for hh in range(NSA_HEADS):
        blk = pt[HEAD_DIM * hh:HEAD_DIM * (hh + 1), :]
        x1, x2 = blk[0:ROT_HALF], blk[ROT_HALF:ROT_DIM]
        qh = (jnp.concatenate([x1 * cos_t - x2 * sin_t, x2 * cos_t + x1 * sin_t, blk[ROT_DIM:]], axis=0)
              * scale).astype(BF16)
        gg, n = divmod(hh, Q_PER_G)
        for j in range(tm // tq):
            qt_ref[0, gg, j, :, n * tq:(n + 1) * tq] = qh[:, j * tq:(j + 1) * tq]
    r0 = NSA_WIDTH
    for ref in (vst_ref, vwt_ref):
        for gg in range(KV_GROUPS):
            blk = pt[r0 + HEAD_DIM * gg:r0 + HEAD_DIM * (gg + 1), :].astype(BF16)
            for j in range(tm // tk):
                ref[0, gg, j] = blk[:, j * tk:(j + 1) * tk]
        r0 += KV_W
    for gg in range(KV_GROUPS):
        gt_ref[0, gg] = jax.nn.sigmoid(pt[r0 + GATE_ROWS * gg:r0 + GATE_ROWS * (gg + 1), :])


def _inproj(x, sc, sh, g, wn, wt, rc, rs1, rs2, cos_t, sin_t):
    B, S, D = x.shape
    tm = min(INPROJ_TM, S)
    tq, tk = ATTN_TQ, ATTN_TK
    G = KV_GROUPS
    kv = lambda dt: jax.ShapeDtypeStruct((B, G, S, HEAD_DIM), dt)
    kv_spec = pl.BlockSpec((1, G, tm, HEAD_DIM), lambda b, i: (b, 0, i, 0))
    vt_shape = jax.ShapeDtypeStruct((B, G, S // tk, HEAD_DIM, tk), BF16)
    vt_spec = pl.BlockSpec((1, G, tm // tk, HEAD_DIM, tk), lambda b, i: (b, 0, i, 0, 0))
    row = pl.BlockSpec((1, 1, D), lambda b, i: (b, 0, 0))
    tab = pl.BlockSpec((tm, LANES), lambda b, i: (i, 0))
    tab_t = pl.BlockSpec((ROT_HALF, tm), lambda b, i: (0, i))
    return pl.pallas_call(
        _inproj_kernel,
        grid=(B, S // tm),
        in_specs=[pl.BlockSpec((1, tm, D), lambda b, i: (b, i, 0)), row, row,
                  pl.BlockSpec((1, D), lambda b, i: (0, 0)),
                  pl.BlockSpec(wn.shape, lambda b, i: (0, 0)),
                  pl.BlockSpec(wt.shape, lambda b, i: (0, 0)),
                  tab, tab, tab, tab_t, tab_t],
        out_specs=[pl.BlockSpec((1, tm, CONV_CH), lambda b, i: (b, i, 0)),
                   kv_spec, kv_spec, kv_spec, kv_spec,
                   pl.BlockSpec((1, G, tm // tq, HEAD_DIM, Q_PER_G * tq), lambda b, i: (b, 0, i, 0, 0)),
                   vt_spec, vt_spec,
                   pl.BlockSpec((1, G, GATE_ROWS, tm), lambda b, i: (b, 0, 0, i))],
        out_shape=[jax.ShapeDtypeStruct((B, S, CONV_CH), F32),
                   kv(F32), kv(F32), kv(BF16), kv(BF16),
                   jax.ShapeDtypeStruct((B, G, S // tq, HEAD_DIM, Q_PER_G * tq), BF16),
                   vt_shape, vt_shape,
                   jax.ShapeDtypeStruct((B, G, GATE_ROWS, S), F32)],
        compiler_params=pltpu.CompilerParams(dimension_semantics=("parallel", "parallel")),
        name="inproj",
    )(x, sc, sh, g, wn, wt, rc, rs1, rs2, cos_t, sin_t)


def _conv_kernel(prev_ref, cur_ref, w_ref, cb_ref, lg_ref, lb_ref, on_ref, o_ref, pad_ref):
    tr = cur_ref.shape[1]
    first = pl.program_id(1) == 0
    halo = prev_ref[0, tr - CONV_HALO:tr, :]
    pad_ref[0:CONV_HALO, :] = jnp.where(first, 0.0, halo)
    pad_ref[CONV_HALO:CONV_HALO + tr, :] = cur_ref[0]
    off = CONV_HALO - (CONV_WIDTH - 1)
    acc = jnp.zeros((tr, CONV_CH), F32)
    for k in range(CONV_WIDTH):
        acc = acc + pad_ref[off + k:off + k + tr, :] * w_ref[k:k + 1, :]
    y = acc + cb_ref[...]
    mu = jnp.mean(y, axis=-1, keepdims=True)
    yc = y - mu
    var = jnp.mean(yc * yc, axis=-1, keepdims=True)
    yn = yc * lax.rsqrt(var + NORM_EPS) * lg_ref[...] + lb_ref[...]
    s = yn * jax.nn.sigmoid(yn)
    o_ref[0] = _rms(s, on_ref[...]).astype(o_ref.dtype)


def _conv(u, w, cb, lg, lb, on):
    B, S, C = u.shape
    tr = min(CONV_TR, S)
    vec = pl.BlockSpec((1, C), lambda b, i: (0, 0))
    return pl.pallas_call(
        _conv_kernel,
        grid=(B, S // tr),
        in_specs=[pl.BlockSpec((1, tr, C), lambda b, i: (b, jnp.maximum(i - 1, 0), 0)),
                  pl.BlockSpec((1, tr, C), lambda b, i: (b, i, 0)),
                  pl.BlockSpec((CONV_WIDTH, C), lambda b, i: (0, 0)),
                  vec, vec, vec, vec],
        out_specs=pl.BlockSpec((1, tr, C), lambda b, i: (b, i, 0)),
        out_shape=jax.ShapeDtypeStruct((B, S, C), BF16),
        scratch_shapes=[pltpu.VMEM((CONV_HALO + tr, C), F32)],
        compiler_params=pltpu.CompilerParams(dimension_semantics=("parallel", "parallel")),
        name="conv",
    )(u, u, w, cb, lg, lb, on)


def _cmp_kernel(kx_ref, vx_ref, pek_ref, pev_ref, kw1_ref, kw2_ref, vw1_ref, vw2t_ref, ko_ref, vo_ref):
    nc = kx_ref.shape[2]
    half = kx_ref.shape[3]
    nt = (((1,), (1,)), ((), ()))
    for x_ref, pe_ref, w1_ref, w2_ref, o_ref, transposed in ((kx_ref, pek_ref, kw1_ref, kw2_ref, ko_ref, False),
                                                             (vx_ref, pev_ref, vw1_ref, vw2t_ref, vo_ref, True)):
        w1 = w1_ref[...]
        pe = jnp.broadcast_to(pe_ref[...], (SUBLANES, 2 * half))
        pe_proj = jnp.dot(pe, w1, preferred_element_type=F32, precision=HI)[0:1]
        for gg in range(KV_GROUPS):
            xg = x_ref[0, gg]
            first = jnp.dot(xg, w1[0:half], preferred_element_type=F32, precision=HI)
            second = jnp.dot(xg, w1[half:2 * half], preferred_element_type=F32, precision=HI)
            hid = first + pltpu.roll(second, nc - 1, 0) + pe_proj
            hid = hid * jax.nn.sigmoid(hid)
            if transposed:
                o_ref[0, gg] = lax.dot_general(w2_ref[...], hid, nt, preferred_element_type=F32, precision=HI)
            else:
                o_ref[0, gg] = jnp.dot(hid, w2_ref[...], preferred_element_type=F32, precision=HI)


def _compress(kx, vx, pek, pev, kw1, kw2, vw1, vw2t):
    B, G, NC, W = kx.shape
    xs = pl.BlockSpec((1, G, NC, W), lambda b: (b, 0, 0, 0))
    full = lambda a: pl.BlockSpec(a.shape, lambda b: (0,) * a.ndim)
    return pl.pallas_call(
        _cmp_kernel,
        grid=(B,),
        in_specs=[xs, xs, full(pek), full(pev), full(kw1), full(kw2), full(vw1), full(vw2t)],
        out_specs=[pl.BlockSpec((1, G, NC, HEAD_DIM), lambda b: (b, 0, 0, 0)),
                   pl.BlockSpec((1, G, HEAD_DIM, NC), lambda b: (b, 0, 0, 0))],
        out_shape=[jax.ShapeDtypeStruct((B, G, NC, HEAD_DIM), F32),
                   jax.ShapeDtypeStruct((B, G, HEAD_DIM, NC), F32)],
        compiler_params=pltpu.CompilerParams(dimension_semantics=("parallel",)),
        name="compress",
    )(kx, vx, pek, pev, kw1, kw2, vw1, vw2t)


def _attn_kernel(qt_ref, kc_ref, vct_ref, ks_ref, vst_ref, kw_ref, vwt_ref, gt_ref, ovt_ref, o_ref, sel_ref):
    tq, tk = ATTN_TQ, ATTN_TK
    R = Q_PER_G * tq
    per_tile = tk // SEL_BLK
    S = ks_ref.shape[2]
    ncp = kc_ref.shape[2]
    nsel = ovt_ref.shape[0]
    qi = pl.program_id(2)
    q0 = qi * tq
    qt = qt_ref[0, 0, 0]
    t_row = q0 + lax.broadcasted_iota(I32, (1, tq), 1)
    heads = lambda a: jnp.concatenate([a] * Q_PER_G, axis=1)

    sc = jnp.dot(kc_ref[0, 0].astype(BF16), qt, preferred_element_type=F32)
    c_io = lax.broadcasted_iota(I32, (ncp, tq), 0)
    m_c = (c_io * CMP_STRIDE + (CMP_LEN - 1) <= t_row) & (c_io < ncp - 1)
    scb = sc + heads(jnp.where(m_c, 0.0, NEG_INF))
    e = jnp.exp(scb - jnp.max(scb, axis=0, keepdims=True)) * heads(jnp.where(m_c, 1.0, 0.0))
    den = jnp.sum(e, axis=0, keepdims=True)
    pc = e / jnp.where(den > 0.0, den, 1.0)
    o_cmp = jnp.dot(vct_ref[0, 0].astype(BF16), pc.astype(BF16), preferred_element_type=F32)

    psum = pc[:, 0:tq]
    for n in range(1, Q_PER_G):
        psum = psum + pc[:, n * tq:(n + 1) * tq]
    imp = jnp.dot(ovt_ref[...], psum, preferred_element_type=F32, precision=HI)
    j_io = lax.broadcasted_iota(I32, (nsel, tq), 0)
    cur = t_row // SEL_BLK
    valid = j_io * SEL_BLK <= t_row
    forced = (j_io == 0) | (j_io == cur) | (j_io == cur - 1)
    score = jnp.where(valid, jnp.where(forced, FORCE_SCORE, imp), NEG_INF)
    rank = jnp.zeros((nsel, tq), F32)
    for i in range(nsel):
        row = score[i:i + 1, :]
        tie = jnp.where(j_io > i, 1.0, 0.0)
        rank = rank + jnp.where(row > score, 1.0, jnp.where(row == score, tie, 0.0))
    sel_bias = jnp.where(rank < float(min(SEL_TOPN, nsel)), 0.0, NEG_INF)
    sel_ref[...] = jnp.zeros(sel_ref.shape, F32)
    for jj in range(S // tk):
        sel_ref[jj, 0:per_tile, :] = sel_bias[jj * per_tile:(jj + 1) * per_tile, :]

    k_io = lax.broadcasted_iota(I32, (tk, tq), 0)

    def flash_step(k_ref, vt_ref, kj, bias, carry):
        m, l, acc = carry
        k0 = pl.multiple_of(kj * tk, tk)
        s = jnp.dot(k_ref[0, 0, pl.ds(k0, tk), :], qt, preferred_element_type=F32) + heads(bias)
        m_new = jnp.maximum(m, jnp.max(s, axis=0, keepdims=True))
        alpha = jnp.exp(m - m_new)
        p = jnp.exp(s - m_new)
        l = alpha * l + jnp.sum(p, axis=0, keepdims=True)
        acc = alpha * acc + jnp.dot(vt_ref[0, 0, kj], p.astype(BF16), preferred_element_type=F32)
        return m_new, l, acc

    init = (jnp.full((1, R), NEG_INF, F32), jnp.zeros((1, R), F32), jnp.zeros((HEAD_DIM, R), F32))

    def slc_body(kj, carry):
        blocks = sel_ref[kj]
        bias = jnp.concatenate([jnp.broadcast_to(blocks[b:b + 1, :], (SEL_BLK, tq)) for b in range(per_tile)], axis=0)
        bias = jnp.where(kj * tk + k_io <= t_row, bias, NEG_INF)
        return flash_step(ks_ref, vst_ref, kj, bias, carry)

    def win_body(kj, carry):
        rel = t_row - (kj * tk + k_io)
        bias = jnp.where((rel >= 0) & (rel < WINDOW), 0.0, NEG_INF)
        return flash_step(kw_ref, vwt_ref, kj, bias, carry)

    n_slc = (q0 + tq + tk - 1) // tk
    lo_tile = jnp.maximum(q0 - (WINDOW - 1), 0) // tk
    n_pair = lo_tile // 2

    def pair_body(i, carry):
        return slc_body(2 * i, carry[0]), slc_body(2 * i + 1, carry[1])

    st_a, st_b = lax.fori_loop(0, n_pair, pair_body, (init, init))
    st_a = lax.fori_loop(2 * n_pair, lo_tile, slc_body, st_a)

    def both_body(kj, carry):
        return slc_body(kj, carry[0]), win_body(kj, carry[1])

    st_a, (_, l_w, acc_w) = lax.fori_loop(lo_tile, n_slc, both_body, (st_a, init))
    m_s = jnp.maximum(st_a[0], st_b[0])
    w_a, w_b = jnp.exp(st_a[0] - m_s), jnp.exp(st_b[0] - m_s)
    l_s = w_a * st_a[1] + w_b * st_b[1]
    acc_s = w_a * st_a[2] + w_b * st_b[2]

    gt = gt_ref[0, 0]
    o_slc = acc_s / l_s
    o_win = acc_w / l_w
    outs = []
    for n in range(Q_PER_G):
        cols = slice(n * tq, (n + 1) * tq)
        outs.append(gt[3 * n:3 * n + 1, :] * o_cmp[:, cols] + gt[3 * n + 1:3 * n + 2, :] * o_slc[:, cols]
                    + gt[3 * n + 2:3 * n + 3, :] * o_win[:, cols])
    o_ref[0] = jnp.concatenate(outs, axis=0).T


def _attention(qt, kc, vct, ks, vst, kw, vwt, gt, overlap_t):
    B, G, NQT, _, R = qt.shape
    S = ks.shape[2]
    tq, tk = ATTN_TQ, ATTN_TK
    ncp = kc.shape[2]
    kfull = pl.BlockSpec((1, 1, S, HEAD_DIM), lambda b, g, i: (b, g, 0, 0))
    vfull = pl.BlockSpec((1, 1, S // tk, HEAD_DIM, tk), lambda b, g, i: (b, g, 0, 0, 0))
    return pl.pallas_call(
        _attn_kernel,
        grid=(B, G, NQT),
        in_specs=[pl.BlockSpec((1, 1, 1, HEAD_DIM, R), lambda b, g, i: (b, g, i, 0, 0)),
                  pl.BlockSpec((1, 1, ncp, HEAD_DIM), lambda b, g, i: (b, g, 0, 0)),
                  pl.BlockSpec((1, 1, HEAD_DIM, ncp), lambda b, g, i: (b, g, 0, 0)),
                  kfull, vfull, kfull, vfull,
                  pl.BlockSpec((1, 1, GATE_ROWS, tq), lambda b, g, i: (b, g, 0, i)),
                  pl.BlockSpec(overlap_t.shape, lambda b, g, i: (0, 0))],
        out_specs=pl.BlockSpec((1, tq, Q_PER_G * HEAD_DIM), lambda b, g, i: (b, i, g)),
        out_shape=jax.ShapeDtypeStruct((B, S, NSA_WIDTH), F32),
        scratch_shapes=[pltpu.VMEM((S // tk, SUBLANES, tq), F32)],
        compiler_params=pltpu.CompilerParams(dimension_semantics=("parallel", "parallel", "arbitrary")),
        name="attn",
    )(qt, kc, vct, ks, vst, kw, vwt, gt, overlap_t)


def _outproj_kernel(x_ref, cv_ref, nsa_ref, on_ref, w_ref, gt1_ref, g2_ref, sc2_ref, sh2_ref, wrh_ref, wrl_ref, br_ref,
                    tri_ref, x1_ref, h2_ref, rt_ref, cnt_ref, run_ref):
    tm = x_ref.shape[1]

    @pl.when((pl.program_id(0) == 0) & (pl.program_id(1) == 0))
    def _():
        run_ref[...] = jnp.zeros(run_ref.shape, F32)

    nn = _rms(nsa_ref[0], on_ref[...]).astype(BF16)
    y = (jnp.dot(cv_ref[0], w_ref[0:CONV_CH, :], preferred_element_type=F32)
         + jnp.dot(nn, w_ref[CONV_CH:CONV_CH + NSA_WIDTH, :], preferred_element_type=F32))
    x1 = x_ref[0] + gt1_ref[0] * y
    x1_ref[0] = x1
    h2 = _rms(x1, g2_ref[...]) * (1.0 + sc2_ref[0]) + sh2_ref[0]
    for s in range(ROW_TILES):
        h2_ref[pl.ds(s, tm, stride=ROW_TILES), :] = h2[:, s * LANES:(s + 1) * LANES]
    nt = (((1,), (1,)), ((), ()))
    h_hi = h2.astype(BF16)
    h_lo = (h2 - h_hi.astype(F32)).astype(BF16)
    logits = (lax.dot_general(wrh_ref[...], h_hi, nt, preferred_element_type=F32)
              + lax.dot_general(wrh_ref[...], h_lo, nt, preferred_element_type=F32)
              + lax.dot_general(wrl_ref[...], h_hi, nt, preferred_element_type=F32)) + br_ref[...]
    eio = lax.broadcasted_iota(I32, (N_EXPERTS, tm), 0).astype(F32)
    vals, idxs = [], []
    for _ in range(TOP_K):
        m = jnp.max(logits, axis=0, keepdims=True)
        ix = jnp.min(jnp.where(logits == m, eio, float(N_EXPERTS)), axis=0, keepdims=True)
        vals.append(m)
        idxs.append(ix)
        logits = jnp.where(eio == ix, -jnp.inf, logits)
    es = [jnp.exp(v - vals[0]) for v in vals]
    den = es[0] + es[1] + es[2] + es[3]
    hot = jnp.zeros((N_EXPERTS, tm), F32)
    for r in range(TOP_K):
        hot = hot + jnp.where(eio == idxs[r], 1.0, 0.0)
    before = run_ref[...] + jnp.dot(hot.astype(BF16), tri_ref[...], preferred_element_type=F32)
    ranks = [jnp.sum(jnp.where(eio == idxs[r], before, 0.0), axis=0, keepdims=True) for r in range(TOP_K)]
    run_ref[...] = run_ref[...] + jnp.sum(hot, axis=1, keepdims=True)
    cnt_ref[...] = run_ref[...]
    rio = lax.broadcasted_iota(I32, (ROUTE_W, tm), 0)
    out = jnp.zeros((ROUTE_W, tm), F32)
    for r in range(TOP_K):
        out = jnp.where(rio == r, idxs[r], out)
        out = jnp.where(rio == TOP_K + r, es[r] / den, out)
        out = jnp.where(rio == 2 * TOP_K + r, ranks[r], out)
    rt_ref[0] = out


def _outproj(x, cv, nsa, on, w, gt1, g2, sc2, sh2, wr, br):
    B, S, D = x.shape
    tm = min(OUT_TM, S)
    nt = S // tm
    tri = (jnp.arange(tm)[:, None] < jnp.arange(tm)[None, :]).astype(BF16)
    wr_t = wr.T
    wr_hi = wr_t.astype(BF16)
    wr_lo = (wr_t - wr_hi.astype(F32)).astype(BF16)
    row = pl.BlockSpec((1, 1, D), lambda b, i: (b, 0, 0))
    vec = lambda n: pl.BlockSpec((1, n), lambda b, i: (0, 0))
    col = pl.BlockSpec((N_EXPERTS, 1), lambda b, i: (0, 0))
    wr_spec = pl.BlockSpec((N_EXPERTS, D), lambda b, i: (0, 0))
    return pl.pallas_call(
        _outproj_kernel,
        grid=(B, nt),
        in_specs=[pl.BlockSpec((1, tm, D), lambda b, i: (b, i, 0)),
                  pl.BlockSpec((1, tm, CONV_CH), lambda b, i: (b, i, 0)),
                  pl.BlockSpec((1, tm, NSA_WIDTH), lambda b, i: (b, i, 0)),
                  vec(NSA_WIDTH),
                  pl.BlockSpec((D, D), lambda b, i: (0, 0)),
                  row, vec(D), row, row,
                  wr_spec, wr_spec, col,
                  pl.BlockSpec((tm, tm), lambda b, i: (0, 0))],
        out_specs=[pl.BlockSpec((1, tm, D), lambda b, i: (b, i, 0)),
                   pl.BlockSpec((tm * ROW_TILES, LANES), lambda b, i: (b * nt + i, 0)),
                   pl.BlockSpec((1, ROUTE_W, tm), lambda b, i: (b, 0, i)),
                   col],
        out_shape=[jax.ShapeDtypeStruct((B, S, D), F32),
                   jax.ShapeDtypeStruct((B * S * ROW_TILES, LANES), F32),
                   jax.ShapeDtypeStruct((B, ROUTE_W, S), F32),
                   jax.ShapeDtypeStruct((N_EXPERTS, 1), F32)],
        scratch_shapes=[pltpu.VMEM((N_EXPERTS, 1), F32)],
        compiler_params=pltpu.CompilerParams(dimension_semantics=("arbitrary", "arbitrary")),
        name="outproj",
    )(x, cv, nsa, on, w, gt1, g2, sc2, sh2, wr_hi, wr_lo, br.reshape(N_EXPERTS, 1), tri)


def _issue_rows(idx_ref, n, src_hbm, dst, slot, sem, unrolled=False):
    assert n % ISSUE_UNROLL == 0

    def start(r, queue):
        pltpu.make_async_copy(src_hbm.at[idx_ref[0, 0, r]],
                              dst.at[slot, pl.ds(pl.multiple_of(r * ROW_TILES, ROW_TILES), ROW_TILES), :],
                              sem.at[slot]).start(priority=queue)

    if unrolled:
        for r in range(n):
            start(r, r % 2)
        return

    def body(c, carry):
        for u in range(ISSUE_UNROLL):
            start(c * ISSUE_UNROLL + u, u % 2)
        return carry
    lax.fori_loop(0, n // ISSUE_UNROLL, body, 0)


def _wait_rows(dst, slot, sem):
    pltpu.make_async_copy(dst.at[slot], dst.at[slot], sem.at[slot]).wait()


def _rows_2d(buf, slot, base, n):
    return jnp.concatenate(
        [buf[slot, pl.ds(base * ROW_TILES + s, n, stride=ROW_TILES), :] for s in range(ROW_TILES)], axis=1)


def _ffn_kernel(be_ref, nb_ref, ta_ref, tb_ref, h2_hbm, gate_ref, wg_ref, bg_ref, wu_ref, bu_ref, wd_ref, bd_ref,
                o_ref, xbuf, wgb, wub, wdb, sem):
    bm = ta_ref.shape[2]
    i = pl.program_id(0)
    nb = nb_ref[0]
    slot = i % 2

    @pl.when((i == 0) | (be_ref[i] != be_ref[jnp.maximum(i - 1, 0)]))
    def _():
        wgb[...] = wg_ref[0].astype(BF16)
        wub[...] = wu_ref[0].astype(BF16)
        wdb[...] = wd_ref[0].astype(BF16)

    @pl.when(i == 0)
    def _():
        _issue_rows(ta_ref, bm, h2_hbm, xbuf, 0, sem)

    @pl.when(i < nb)
    def _():
        _wait_rows(xbuf, slot, sem)
        x = _rows_2d(xbuf, slot, 0, bm).astype(BF16)
        _issue_rows(tb_ref, bm, h2_hbm, xbuf, 1 - slot, sem, unrolled=True)
        g = jnp.dot(x, wgb[...], preferred_element_type=F32) + bg_ref[0]
        u = jnp.dot(x, wub[...], preferred_element_type=F32) + bu_ref[0]
        g = jnp.minimum(g, SWIGLU_LIMIT)
        u = jnp.clip(u, -SWIGLU_LIMIT, SWIGLU_LIMIT)
        act = g * jax.nn.sigmoid(SWIGLU_ALPHA * g) * (u + 1.0)
        y = (jnp.dot(act.astype(BF16), wdb[...], preferred_element_type=F32) + bd_ref[0]) * gate_ref[...]
        for s in range(ROW_TILES):
            o_ref[pl.ds(s, bm, stride=ROW_TILES), :] = y[:, s * LANES:(s + 1) * LANES]

        @pl.when(i + 1 >= nb)
        def _():
            _wait_rows(xbuf, 1 - slot, sem)

    @pl.when(i >= nb)
    def _():
        o_ref[...] = jnp.zeros(o_ref.shape, o_ref.dtype)


def _ffn(block_e, nb_used, buf_tok, h2_rows, buf_gate, wg, bg, wu, bu, wd, bd):
    NB = block_e.shape[0]
    bm = FFN_BM
    D, F = D_MODEL, D_FF
    tok3 = buf_tok.reshape(NB, 1, bm)
    wspec = lambda r, c: pl.BlockSpec((1, r, c), lambda i, be, nb: (be[i], 0, 0))
    vmem_limit = 2 * 3 * D * F * 4 + 3 * D * F * 2 + 4 * bm * D * 4 + 6 * bm * F * 4
    return pl.pallas_call(
        _ffn_kernel,
        grid_spec=pltpu.PrefetchScalarGridSpec(
            num_scalar_prefetch=2,
            grid=(NB,),
            in_specs=[pl.BlockSpec((1, 1, bm), lambda i, be, nb: (i, 0, 0), memory_space=pltpu.SMEM),
                      pl.BlockSpec((1, 1, bm), lambda i, be, nb: (jnp.minimum(i + 1, NB - 1), 0, 0),
                                   memory_space=pltpu.SMEM),
                      pl.BlockSpec(memory_space=pl.ANY),
                      pl.BlockSpec((bm, 1), lambda i, be, nb: (i, 0)),
                      wspec(D, F), wspec(1, F), wspec(D, F), wspec(1, F), wspec(F, D), wspec(1, D)],
            out_specs=pl.BlockSpec((bm * ROW_TILES, LANES), lambda i, be, nb: (i, 0)),
            scratch_shapes=[pltpu.VMEM((2, bm * ROW_TILES, LANES), F32),
                            pltpu.VMEM((D, F), BF16), pltpu.VMEM((D, F), BF16), pltpu.VMEM((F, D), BF16),
                            pltpu.SemaphoreType.DMA((2,))]),
        out_shape=jax.ShapeDtypeStruct((NB * bm * ROW_TILES, LANES), F32),
        compiler_params=pltpu.CompilerParams(dimension_semantics=("arbitrary",),
                                             vmem_limit_bytes=vmem_limit),
        name="ffn",
    )(block_e, nb_used, tok3, tok3, h2_rows, buf_gate, wg, bg, wu, bu, wd, bd)


def _combine_kernel(da_ref, db_ref, y_hbm, x1_ref, gt2_ref, fg_ref, o_ref, buf, sem):
    tm = x1_ref.shape[0]
    n = TOP_K * tm
    i = pl.program_id(0)
    slot = i % 2

    @pl.when(i == 0)
    def _():
        _issue_rows(da_ref, n, y_hbm, buf, 0, sem)

    @pl.when(i + 1 < pl.num_programs(0))
    def _():
        _issue_rows(db_ref, n, y_hbm, buf, 1 - slot, sem)

    _wait_rows(buf, slot, sem)
    y = _rows_2d(buf, slot, 0, tm)
    for k in range(1, TOP_K):
        y = y + _rows_2d(buf, slot, k * tm, tm)
    x2 = x1_ref[...] + gt2_ref[0] * y
    o_ref[...] = _rms(x2, fg_ref[...])


def _combine(dest3, y_rows, x1, gt2, fg, S):
    T, D = x1.shape
    tm = min(COMB_TM, S)
    NT = T // tm
    per_b = S // tm
    n = TOP_K * tm
    return pl.pallas_call(
        _combine_kernel,
        grid=(NT,),
        in_specs=[pl.BlockSpec((1, 1, n), lambda i: (i, 0, 0), memory_space=pltpu.SMEM),
                  pl.BlockSpec((1, 1, n), lambda i: (jnp.minimum(i + 1, NT - 1), 0, 0), memory_space=pltpu.SMEM),
                  pl.BlockSpec(memory_space=pl.ANY),
                  pl.BlockSpec((tm, D), lambda i: (i, 0)),
                  pl.BlockSpec((1, 1, D), lambda i: (i // per_b, 0, 0)),
                  pl.BlockSpec((1, D), lambda i: (0, 0))],
        out_specs=pl.BlockSpec((tm, D), lambda i: (i, 0)),
        out_shape=jax.ShapeDtypeStruct((T, D), F32),
        scratch_shapes=[pltpu.VMEM((2, n * ROW_TILES, LANES), F32), pltpu.SemaphoreType.DMA((2,))],
        compiler_params=pltpu.CompilerParams(dimension_semantics=("arbitrary",)),
        name="combine",
    )(dest3, dest3, y_rows, x1, gt2, fg)


def _rope_tables(S):
    inv = ROPE_THETA ** (-jnp.arange(0, ROT_DIM, 2, dtype=F32) / ROT_DIM)
    ang = jnp.arange(S, dtype=F32)[:, None] * inv[None, :]
    cos, sin = jnp.cos(ang), jnp.sin(ang)
    d = jnp.arange(KV_W) % HEAD_DIM
    first, second = d < ROT_HALF, (d >= ROT_HALF) & (d < ROT_DIM)
    cos_l = cos[:, d % ROT_HALF]
    sin_l = sin[:, d % ROT_HALF]
    rc = jnp.where((d < ROT_DIM)[None], cos_l, 1.0)
    rs1 = jnp.where(second[None], sin_l, 0.0)
    rs2 = jnp.where(first[None], -sin_l, 0.0)
    return rc, rs1, rs2, cos.T, sin.T


def _route_plan(route, counts, T):
    bm = FFN_BM
    A = T * TOP_K
    tok_major = lambda r: jnp.swapaxes(r, 1, 2).reshape(T, TOP_K)
    flat_e = tok_major(route[:, 0:TOP_K]).astype(I32).reshape(A)
    flat_g = tok_major(route[:, TOP_K:2 * TOP_K]).reshape(A)
    rank = tok_major(route[:, 2 * TOP_K:3 * TOP_K]).astype(I32)
    order = jnp.argsort(flat_e, stable=True).astype(I32)
    counts = counts.astype(I32)
    starts = jnp.cumsum(counts) - counts
    padded = (counts + bm - 1) // bm * bm
    pends = jnp.cumsum(padded)
    pstarts = pends - padded
    P = (A + N_EXPERTS * bm + bm - 1) // bm * bm
    NB = P // bm
    blk0 = jnp.arange(NB, dtype=I32) * bm
    block_e = jnp.minimum(jnp.sum((pends[None, :] <= blk0[:, None]).astype(I32), axis=1), N_EXPERTS - 1)
    r = (blk0 - pstarts[block_e])[:, None] + jnp.arange(bm, dtype=I32)[None, :]
    valid = r < counts[block_e][:, None]
    a_p = order[jnp.clip(starts[block_e][:, None] + r, 0, A - 1)]
    buf_tok = jnp.where(valid, a_p // TOP_K, 0)
    buf_gate = jnp.where(valid, flat_g[a_p], 0.0).reshape(P, 1)
    dest_assign = pstarts[flat_e].reshape(T, TOP_K) + rank
    nb_used = (pends[-1] // bm).astype(I32).reshape(1)
    return block_e, nb_used, buf_tok, buf_gate, dest_assign


def kernel(x, c, norm1_g, norm2_g, w_ada, b_ada, w_in, conv_w, conv_b, conv_ln_g, conv_ln_b, cmp_pe_k, cmp_pe_v,
           cmp_k_w1, cmp_k_w2, cmp_v_w1, cmp_v_w2, out_norm_conv, out_norm_nsa, w_out, w_router, b_router,
           w_gate, b_gate, w_up, b_up, w_down, b_down, final_norm_g):
    B, S, D = x.shape
    T = B * S
    G = KV_GROUPS
    assert D == D_MODEL and S % ATTN_TK == 0 and S % CMP_STRIDE == 0 and KV_W == LANES
    rc, rs1, rs2, cos_t, sin_t = _rope_tables(S)
    n_sel = S // SEL_BLK
    nc = S // CMP_STRIDE
    cstart = jnp.arange(nc) * CMP_STRIDE
    jstart = jnp.arange(n_sel) * SEL_BLK
    overlap_t = ((cstart[None, :] <= jstart[:, None] + SEL_BLK - 1)
                 & (cstart[None, :] + CMP_LEN - 1 >= jstart[:, None])
                 & (jnp.arange(nc)[None, :] < nc - 1)).astype(F32)

    assert w_ada.shape[0] == 1
    for l in range(1):
        mod = _adaln(c, w_ada[l], b_ada[l][None])
        sh1, sc1, gt1, sh2, sc2, gt2 = [m[:, None, :] for m in jnp.split(mod, 6, axis=-1)]

        wl = w_in[l]
        o = 2 * CONV_CH + NSA_WIDTH
        kvc = [wl[:, o + i * KV_W:o + (i + 1) * KV_W] for i in range(6)]
        gl = wl[:, o + 6 * KV_W:]
        per_g = 3 * Q_PER_G
        gpad = [jnp.pad(gl[:, per_g * g:per_g * (g + 1)], ((0, 0), (0, GATE_ROWS - per_g))) for g in range(G)]
        wn = jnp.concatenate([wl[:, :2 * CONV_CH], kvc[0], kvc[2], kvc[4], kvc[1]], axis=1).astype(BF16)
        wt = jnp.concatenate([wl[:, 2 * CONV_CH:o], kvc[3], kvc[5]] + gpad, axis=1).T.astype(BF16)
        u, kc, vc, ks, kw, qt, vst, vwt, gates = _inproj(x, sc1, sh1, norm1_g[l][None], wn, wt,
                                                         rc, rs1, rs2, cos_t, sin_t)

        conv_n = _conv(u, conv_w[l], conv_b[l][None], conv_ln_g[l][None], conv_ln_b[l][None],
                       out_norm_conv[l][None])

        chunk = CMP_STRIDE * HEAD_DIM
        kcmp, vcmp_t = _compress(kc.reshape(B, G, nc, chunk), vc.reshape(B, G, nc, chunk),
                                 cmp_pe_k[l].reshape(1, -1), cmp_pe_v[l].reshape(1, -1),
                                 cmp_k_w1[l], cmp_k_w2[l], cmp_v_w1[l], cmp_v_w2[l].T)
        nsa = _attention(qt, kcmp, vcmp_t, ks, vst, kw, vwt, gates, overlap_t)

        x1, h2_rows, route, counts = _outproj(x, conv_n, nsa, out_norm_nsa[l][None], w_out[l].astype(BF16), gt1,
                                              norm2_g[l][None], sc2, sh2, w_router[l], b_router[l][None])

        block_e, nb_used, buf_tok, buf_gate, dest_assign = _route_plan(route, counts[:, 0], T)
        y_rows = _ffn(block_e, nb_used, buf_tok, h2_rows.reshape(T, ROW_TILES, LANES), buf_gate,
                      w_gate[l], b_gate[l][:, None, :], w_up[l], b_up[l][:, None, :],
                      w_down[l], b_down[l][:, None, :])
        tm = min(COMB_TM, S)
        dest3 = dest_assign.reshape(T // tm, tm, TOP_K).transpose(0, 2, 1).reshape(T // tm, 1, TOP_K * tm)
        P = y_rows.shape[0] // ROW_TILES
        x = _combine(dest3, y_rows.reshape(P, ROW_TILES, LANES), x1.reshape(T, D), gt2, final_norm_g[None],
                     S).reshape(B, S, D)
    return x
```

```python
import functools

import jax
import jax.numpy as jnp
from jax import lax
from jax.experimental import pallas as pl
from jax.experimental.pallas import tpu as pltpu

F32 = jnp.float32
BF16 = jnp.bfloat16
I32 = jnp.int32
HI = lax.Precision.HIGHEST

D_MODEL = 1024
CONV_CH = 512
CONV_WIDTH = 31
NSA_HEADS = 8
KV_GROUPS = 2
Q_PER_G = NSA_HEADS // KV_GROUPS
HEAD_DIM = 64
NSA_WIDTH = NSA_HEADS * HEAD_DIM
KV_W = KV_GROUPS * HEAD_DIM
ROT_DIM = HEAD_DIM // 4
ROT_HALF = ROT_DIM // 2
ROPE_THETA = 500000.0
CMP_LEN = 32
CMP_STRIDE = 16
CMP_HIDDEN = 128
SEL_BLK = 64
SEL_TOPN = 16
WINDOW = 512
N_EXPERTS = 32
TOP_K = 4
D_FF = 1024
SWIGLU_ALPHA = 1.702
SWIGLU_LIMIT = 7.0
NORM_EPS = 1e-5
NEG_INF = -1e30
FORCE_SCORE = 1e9
LOG2_E = 1.4426950408889634

LANES = 128
SUBLANES = 8
ROW_TILES = D_MODEL // LANES

GATE_ROWS = 16

INPROJ_TM = 512
CONV_TR = 256
CONV_HALO = 32
ATTN_TQ = 256
ATTN_TK = 256
OUT_TM = 512
FFN_BM = 512
COMB_TM = 256
ROUTE_W = 16
ISSUE_UNROLL = 8


def _rms(x, g):
    return x * lax.rsqrt(jnp.mean(x * x, axis=-1, keepdims=True) + NORM_EPS) * g


def _ada_kernel(c_ref, w_ref, b_ref, o_ref):
    c = c_ref[...]
    ca = c * jax.nn.sigmoid(c)
    o_ref[...] = jnp.dot(ca, w_ref[...], preferred_element_type=F32, precision=HI) + b_ref[...]


def _adaln(c, w, b):
    B = c.shape[0]
    D = D_MODEL
    return pl.pallas_call(
        _ada_kernel,
        grid=(6,),
        in_specs=[pl.BlockSpec((B, D), lambda j: (0, 0)),
                  pl.BlockSpec((D, D), lambda j: (0, j)),
                  pl.BlockSpec((1, D), lambda j: (0, j))],
        out_specs=pl.BlockSpec((B, D), lambda j: (0, j)),
        out_shape=jax.ShapeDtypeStruct((B, 6 * D), F32),
        name="adaln",
    )(c, w, b)


def _inproj_kernel(x_ref, sc_ref, sh_ref, g_ref, wn_ref, wt_ref, rc_ref, rs1_ref, rs2_ref, ct_ref, st_ref,
                   u_ref, kc_ref, vc_ref, ks_ref, kw_ref, qt_ref, vst_ref, vwt_ref, gt_ref):
    tm = x_ref.shape[1]
    tq, tk = ATTN_TQ, ATTN_TK
    h = (_rms(x_ref[0], g_ref[...]) * (1.0 + sc_ref[0]) + sh_ref[0]).astype(BF16)

    p = jnp.dot(h, wn_ref[...], preferred_element_type=F32)
    u_ref[0] = p[:, 0:CONV_CH] * jax.nn.sigmoid(p[:, CONV_CH:2 * CONV_CH])
    c0 = 2 * CONV_CH
    rc, rs1, rs2 = rc_ref[...], rs1_ref[...], rs2_ref[...]
    for ref, roped in ((kc_ref, True), (ks_ref, True), (kw_ref, True), (vc_ref, False)):
        v = p[:, c0:c0 + KV_W]
        if roped:
            v = v * rc + pltpu.roll(v, ROT_HALF, 1) * rs1 + pltpu.roll(v, KV_W - ROT_HALF, 1) * rs2
        for gg in range(KV_GROUPS):
            ref[0, gg] = v[:, HEAD_DIM * gg:HEAD_DIM * (gg + 1)].astype(ref.dtype)
        c0 += KV_W

    pt = lax.dot_general(wt_ref[...], h, (((1,), (1,)), ((), ())), preferred_element_type=F32)
    cos_t, sin_t = ct_ref[...], st_ref[...]
    scale = HEAD_DIM ** -0.5 * LOG2_E
    for hh in range(NSA_HEADS):
        blk = pt[HEAD_DIM * hh:HEAD_DIM * (hh + 1), :]
        x1, x2 = blk[0:ROT_HALF], blk[ROT_HALF:ROT_DIM]
        qh = (jnp.concatenate([x1 * cos_t - x2 * sin_t, x2 * cos_t + x1 * sin_t, blk[ROT_DIM:]], axis=0)
              * scale).astype(BF16)
        gg, n = divmod(hh, Q_PER_G)
        for j in range(tm // tq):
            qt_ref[0, gg, j, :, n * tq:(n + 1) * tq] = qh[:, j * tq:(j + 1) * tq]
    r0 = NSA_WIDTH
    for ref in (vst_ref, vwt_ref):
        for gg in range(KV_GROUPS):
            blk = pt[r0 + HEAD_DIM * gg:r0 + HEAD_DIM * (gg + 1), :].astype(BF16)
            for j in range(tm // tk):
                ref[0, gg, j] = blk[:, j * tk:(j + 1) * tk]
        r0 += KV_W
    for gg in range(KV_GROUPS):
        gt_ref[0, gg] = jax.nn.sigmoid(pt[r0 + GATE_ROWS * gg:r0 + GATE_ROWS * (gg + 1), :])


def _inproj(x, sc, sh, g, wn, wt, rc, rs1, rs2, cos_t, sin_t):
    B, S, D = x.shape
    tm = min(INPROJ_TM, S)
    tq, tk = ATTN_TQ, ATTN_TK
    G = KV_GROUPS
    kv = lambda dt: jax.ShapeDtypeStruct((B, G, S, HEAD_DIM), dt)
    kv_spec = pl.BlockSpec((1, G, tm, HEAD_DIM), lambda b, i: (b, 0, i, 0))
    vt_shape = jax.ShapeDtypeStruct((B, G, S // tk, HEAD_DIM, tk), BF16)
    vt_spec = pl.BlockSpec((1, G, tm // tk, HEAD_DIM, tk), lambda b, i: (b, 0, i, 0, 0))
    row = pl.BlockSpec((1, 1, D), lambda b, i: (b, 0, 0))
    tab = pl.BlockSpec((tm, LANES), lambda b, i: (i, 0))
    tab_t = pl.BlockSpec((ROT_HALF, tm), lambda b, i: (0, i))
    return pl.pallas_call(
        _inproj_kernel,
        grid=(B, S // tm),
        in_specs=[pl.BlockSpec((1, tm, D), lambda b, i: (b, i, 0)), row, row,
                  pl.BlockSpec((1, D), lambda b, i: (0, 0)),
                  pl.BlockSpec(wn.shape, lambda b, i: (0, 0)),
                  pl.BlockSpec(wt.shape, lambda b, i: (0, 0)),
                  tab, tab, tab, tab_t, tab_t],
        out_specs=[pl.BlockSpec((1, tm, CONV_CH), lambda b, i: (b, i, 0)),
                   kv_spec, kv_spec, kv_spec, kv_spec,
                   pl.BlockSpec((1, G, tm // tq, HEAD_DIM, Q_PER_G * tq), lambda b, i: (b, 0, i, 0, 0)),
                   vt_spec, vt_spec,
                   pl.BlockSpec((1, G, GATE_ROWS, tm), lambda b, i: (b, 0, 0, i))],
        out_shape=[jax.ShapeDtypeStruct((B, S, CONV_CH), F32),
                   kv(F32), kv(F32), kv(BF16), kv(BF16),
                   jax.ShapeDtypeStruct((B, G, S // tq, HEAD_DIM, Q_PER_G * tq), BF16),
                   vt_shape, vt_shape,
                   jax.ShapeDtypeStruct((B, G, GATE_ROWS, S), F32)],
        compiler_params=pltpu.CompilerParams(dimension_semantics=("parallel", "parallel")),
        name="inproj",
    )(x, sc, sh, g, wn, wt, rc, rs1, rs2, cos_t, sin_t)


def _conv_kernel(prev_ref, cur_ref, w_ref, cb_ref, lg_ref, lb_ref, on_ref, o_ref, pad_ref):
    tr = cur_ref.shape[1]
    first = pl.program_id(1) == 0
    halo = prev_ref[0, tr - CONV_HALO:tr, :]
    pad_ref[0:CONV_HALO, :] = jnp.where(first, 0.0, halo)
    pad_ref[CONV_HALO:CONV_HALO + tr, :] = cur_ref[0]
    off = CONV_HALO - (CONV_WIDTH - 1)
    acc = jnp.zeros((tr, CONV_CH), F32)
    for b in range(SUBLANES):
        taps = range(b, CONV_WIDTH, SUBLANES)
        win = pad_ref[off + b:off + b + tr + SUBLANES * (len(taps) - 1), :]
        for a, k in enumerate(taps):
            acc = acc + win[SUBLANES * a:SUBLANES * a + tr, :] * w_ref[k:k + 1, :]
    y = acc + cb_ref[...]
    mu = jnp.mean(y, axis=-1, keepdims=True)
    yc = y - mu
    var = jnp.mean(yc * yc, axis=-1, keepdims=True)
    yn = yc * lax.rsqrt(var + NORM_EPS) * lg_ref[...] + lb_ref[...]
    s = yn * jax.nn.sigmoid(yn)
    o_ref[0] = _rms(s, on_ref[...]).astype(o_ref.dtype)


def _conv(u, w, cb, lg, lb, on):
    B, S, C = u.shape
    tr = min(CONV_TR, S)
    vec = pl.BlockSpec((1, C), lambda b, i: (0, 0))
    return pl.pallas_call(
        _conv_kernel,
        grid=(B, S // tr),
        in_specs=[pl.BlockSpec((1, tr, C), lambda b, i: (b, jnp.maximum(i - 1, 0), 0)),
                  pl.BlockSpec((1, tr, C), lambda b, i: (b, i, 0)),
                  pl.BlockSpec((CONV_WIDTH, C), lambda b, i: (0, 0)),
                  vec, vec, vec, vec],
        out_specs=pl.BlockSpec((1, tr, C), lambda b, i: (b, i, 0)),
        out_shape=jax.ShapeDtypeStruct((B, S, C), BF16),
        scratch_shapes=[pltpu.VMEM((CONV_HALO + tr, C), F32)],
        compiler_params=pltpu.CompilerParams(dimension_semantics=("parallel", "parallel")),
        name="conv",
    )(u, u, w, cb, lg, lb, on)


def _cmp_kernel(kx_ref, vx_ref, pek_ref, pev_ref, kw1_ref, kw2_ref, vw1_ref, vw2t_ref, ko_ref, vo_ref):
    nc = kx_ref.shape[2]
    half = kx_ref.shape[3]
    nt = (((1,), (1,)), ((), ()))
    for x_ref, pe_ref, w1_ref, w2_ref, o_ref, transposed in ((kx_ref, pek_ref, kw1_ref, kw2_ref, ko_ref, False),
                                                             (vx_ref, pev_ref, vw1_ref, vw2t_ref, vo_ref, True)):
        w1 = w1_ref[...]
        pe = jnp.broadcast_to(pe_ref[...], (SUBLANES, 2 * half))
        pe_proj = jnp.dot(pe, w1, preferred_element_type=F32, precision=HI)[0:1]
        for gg in range(KV_GROUPS):
            xg = x_ref[0, gg]
            first = jnp.dot(xg, w1[0:half], preferred_element_type=F32, precision=HI)
            second = jnp.dot(xg, w1[half:2 * half], preferred_element_type=F32, precision=HI)
            hid = first + pltpu.roll(second, nc - 1, 0) + pe_proj
            hid = hid * jax.nn.sigmoid(hid)
            if transposed:
                o_ref[0, gg] = lax.dot_general(w2_ref[...], hid, nt, preferred_element_type=F32, precision=HI)
            else:
                o_ref[0, gg] = jnp.dot(hid, w2_ref[...], preferred_element_type=F32, precision=HI)


def _compress(kx, vx, pek, pev, kw1, kw2, vw1, vw2t):
    B, G, NC, W = kx.shape
    xs = pl.BlockSpec((1, G, NC, W), lambda b: (b, 0, 0, 0))
    full = lambda a: pl.BlockSpec(a.shape, lambda b: (0,) * a.ndim)
    return pl.pallas_call(
        _cmp_kernel,
        grid=(B,),
        in_specs=[xs, xs, full(pek), full(pev), full(kw1), full(kw2), full(vw1), full(vw2t)],
        out_specs=[pl.BlockSpec((1, G, NC, HEAD_DIM), lambda b: (b, 0, 0, 0)),
                   pl.BlockSpec((1, G, HEAD_DIM, NC), lambda b: (b, 0, 0, 0))],
        out_shape=[jax.ShapeDtypeStruct((B, G, NC, HEAD_DIM), F32),
                   jax.ShapeDtypeStruct((B, G, HEAD_DIM, NC), F32)],
        compiler_params=pltpu.CompilerParams(dimension_semantics=("parallel",)),
        name="compress",
    )(kx, vx, pek, pev, kw1, kw2, vw1, vw2t)


def _attn_kernel(qt_ref, kc_ref, vct_ref, ks_ref, vst_ref, kw_ref, vwt_ref, gt_ref, ovt_ref, o_ref, sel_ref):
    tq, tk = ATTN_TQ, ATTN_TK
    R = Q_PER_G * tq
    per_tile = tk // SEL_BLK
    S = ks_ref.shape[2]
    ncp = kc_ref.shape[2]
    nsel = ovt_ref.shape[0]
    qi = pl.program_id(2)
    q0 = qi * tq
    qt = qt_ref[0, 0, 0]
    t_row = q0 + lax.broadcasted_iota(I32, (1, tq), 1)
    heads = lambda a: jnp.concatenate([a] * Q_PER_G, axis=1)

    sc = jnp.dot(kc_ref[0, 0].astype(BF16), qt, preferred_element_type=F32)
    c_io = lax.broadcasted_iota(I32, (ncp, tq), 0)
    m_c = (c_io * CMP_STRIDE + (CMP_LEN - 1) <= t_row) & (c_io < ncp - 1)
    scb = sc + heads(jnp.where(m_c, 0.0, NEG_INF))
    e = jnp.exp2(scb - jnp.max(scb, axis=0, keepdims=True)) * heads(jnp.where(m_c, 1.0, 0.0))
    den = jnp.sum(e, axis=0, keepdims=True)
    pc = e / jnp.where(den > 0.0, den, 1.0)
    o_cmp = jnp.dot(vct_ref[0, 0].astype(BF16), pc.astype(BF16), preferred_element_type=F32)

    psum = pc[:, 0:tq]
    for n in range(1, Q_PER_G):
        psum = psum + pc[:, n * tq:(n + 1) * tq]
    imp = jnp.dot(ovt_ref[...], psum, preferred_element_type=F32, precision=HI)
    j_io = lax.broadcasted_iota(I32, (nsel, tq), 0)
    cur = t_row // SEL_BLK
    valid = j_io * SEL_BLK <= t_row
    forced = (j_io == 0) | (j_io == cur) | (j_io == cur - 1)
    score = jnp.where(valid, jnp.where(forced, FORCE_SCORE, imp), NEG_INF)
    rank = jnp.zeros((nsel, tq), F32)
    for i in range(nsel):
        row = score[i:i + 1, :]
        tie = jnp.where(j_io > i, 1.0, 0.0)
        rank = rank + jnp.where(row > score, 1.0, jnp.where(row == score, tie, 0.0))
    sel_bias = jnp.where(rank < float(min(SEL_TOPN, nsel)), 0.0, NEG_INF)
    sel_ref[...] = jnp.zeros(sel_ref.shape, F32)
    for jj in range(S // tk):
        sel_ref[jj, 0:per_tile, :] = sel_bias[jj * per_tile:(jj + 1) * per_tile, :]

    k_io = lax.broadcasted_iota(I32, (tk, tq), 0)

    def flash_step(k_ref, vt_ref, kj, bias, carry):
        m, l, acc = carry
        k0 = pl.multiple_of(kj * tk, tk)
        s = jnp.dot(k_ref[0, 0, pl.ds(k0, tk), :], qt, preferred_element_type=F32) + heads(bias)
        m_new = jnp.maximum(m, jnp.max(s, axis=0, keepdims=True))
        alpha = jnp.exp2(m - m_new)
        p = jnp.exp2(s - m_new)
        l = alpha * l + jnp.sum(p, axis=0, keepdims=True)
        acc = alpha * acc + jnp.dot(vt_ref[0, 0, kj], p.astype(BF16), preferred_element_type=F32)
        return m_new, l, acc

    init = (jnp.full((1, R), NEG_INF, F32), jnp.zeros((1, R), F32), jnp.zeros((HEAD_DIM, R), F32))

    def slc_body(kj, carry):
        blocks = sel_ref[kj]
        bias = jnp.concatenate([jnp.broadcast_to(blocks[b:b + 1, :], (SEL_BLK, tq)) for b in range(per_tile)], axis=0)
        bias = jnp.where(kj * tk + k_io <= t_row, bias, NEG_INF)
        return flash_step(ks_ref, vst_ref, kj, bias, carry)

    def win_body(kj, carry):
        rel = t_row - (kj * tk + k_io)
        bias = jnp.where((rel >= 0) & (rel < WINDOW), 0.0, NEG_INF)
        return flash_step(kw_ref, vwt_ref, kj, bias, carry)

    n_slc = (q0 + tq + tk - 1) // tk
    lo_tile = jnp.maximum(q0 - (WINDOW - 1), 0) // tk
    n_pair = lo_tile // 2

    def pair_body(i, carry):
        return slc_body(2 * i, carry[0]), slc_body(2 * i + 1, carry[1])

    st_a, st_b = lax.fori_loop(0, n_pair, pair_body, (init, init))
    st_a = lax.fori_loop(2 * n_pair, lo_tile, slc_body, st_a)

    def both_body(kj, carry):
        return slc_body(kj, carry[0]), win_body(kj, carry[1])

    st_a, (_, l_w, acc_w) = lax.fori_loop(lo_tile, n_slc, both_body, (st_a, init))
    m_s = jnp.maximum(st_a[0], st_b[0])
    w_a, w_b = jnp.exp2(st_a[0] - m_s), jnp.exp2(st_b[0] - m_s)
    l_s = w_a * st_a[1] + w_b * st_b[1]
    acc_s = w_a * st_a[2] + w_b * st_b[2]

    gt = gt_ref[0, 0]
    o_slc = acc_s / l_s
    o_win = acc_w / l_w
    outs = []
    for n in range(Q_PER_G):
        cols = slice(n * tq, (n + 1) * tq)
        outs.append(gt[3 * n:3 * n + 1, :] * o_cmp[:, cols] + gt[3 * n + 1:3 * n + 2, :] * o_slc[:, cols]
                    + gt[3 * n + 2:3 * n + 3, :] * o_win[:, cols])
    o_ref[0] = jnp.concatenate(outs, axis=0).T


def _attention(qt, kc, vct, ks, vst, kw, vwt, gt, overlap_t):
    B, G, NQT, _, R = qt.shape
    S = ks.shape[2]
    tq, tk = ATTN_TQ, ATTN_TK
    ncp = kc.shape[2]
    kfull = pl.BlockSpec((1, 1, S, HEAD_DIM), lambda b, g, i: (b, g, 0, 0))
    vfull = pl.BlockSpec((1, 1, S // tk, HEAD_DIM, tk), lambda b, g, i: (b, g, 0, 0, 0))
    return pl.pallas_call(
        _attn_kernel,
        grid=(B, G, NQT),
        in_specs=[pl.BlockSpec((1, 1, 1, HEAD_DIM, R), lambda b, g, i: (b, g, i, 0, 0)),
                  pl.BlockSpec((1, 1, ncp, HEAD_DIM), lambda b, g, i: (b, g, 0, 0)),
                  pl.BlockSpec((1, 1, HEAD_DIM, ncp), lambda b, g, i: (b, g, 0, 0)),
                  kfull, vfull, kfull, vfull,
                  pl.BlockSpec((1, 1, GATE_ROWS, tq), lambda b, g, i: (b, g, 0, i)),
                  pl.BlockSpec(overlap_t.shape, lambda b, g, i: (0, 0))],
        out_specs=pl.BlockSpec((1, tq, Q_PER_G * HEAD_DIM), lambda b, g, i: (b, i, g)),
        out_shape=jax.ShapeDtypeStruct((B, S, NSA_WIDTH), F32),
        scratch_shapes=[pltpu.VMEM((S // tk, SUBLANES, tq), F32)],
        compiler_params=pltpu.CompilerParams(dimension_semantics=("parallel", "parallel", "arbitrary")),
        name="attn",
    )(qt, kc, vct, ks, vst, kw, vwt, gt, overlap_t)


def _outproj_kernel(x_ref, cv_ref, nsa_ref, on_ref, w_ref, gt1_ref, g2_ref, sc2_ref, sh2_ref, wrh_ref, wrl_ref, br_ref,
                    tri_ref, x1_ref, h2_ref, rt_ref, cnt_ref, run_ref):
    tm = x_ref.shape[1]

    @pl.when((pl.program_id(0) == 0) & (pl.program_id(1) == 0))
    def _():
        run_ref[...] = jnp.zeros(run_ref.shape, F32)

    nn = _rms(nsa_ref[0], on_ref[...]).astype(BF16)
    y = (jnp.dot(cv_ref[0], w_ref[0:CONV_CH, :], preferred_element_type=F32)
         + jnp.dot(nn, w_ref[CONV_CH:CONV_CH + NSA_WIDTH, :], preferred_element_type=F32))
    x1 = x_ref[0] + gt1_ref[0] * y
    x1_ref[0] = x1
    h2 = _rms(x1, g2_ref[...]) * (1.0 + sc2_ref[0]) + sh2_ref[0]
    for s in range(ROW_TILES):
        h2_ref[pl.ds(s, tm, stride=ROW_TILES), :] = h2[:, s * LANES:(s + 1) * LANES]
    nt = (((1,), (1,)), ((), ()))
    h_hi = h2.astype(BF16)
    h_lo = (h2 - h_hi.astype(F32)).astype(BF16)
    logits = (lax.dot_general(wrh_ref[...], h_hi, nt, preferred_element_type=F32)
              + lax.dot_general(wrh_ref[...], h_lo, nt, preferred_element_type=F32)
              + lax.dot_general(wrl_ref[...], h_hi, nt, preferred_element_type=F32)) + br_ref[...]
    eio = lax.broadcasted_iota(I32, (N_EXPERTS, tm), 0).astype(F32)
    vals, idxs = [], []
    for _ in range(TOP_K):
        m = jnp.max(logits, axis=0, keepdims=True)
        ix = jnp.min(jnp.where(logits == m, eio, float(N_EXPERTS)), axis=0, keepdims=True)
        vals.append(m)
        idxs.append(ix)
        logits = jnp.where(eio == ix, -jnp.inf, logits)
    es = [jnp.exp(v - vals[0]) for v in vals]
    den = es[0] + es[1] + es[2] + es[3]
    hot = jnp.zeros((N_EXPERTS, tm), F32)
    for r in range(TOP_K):
        hot = hot + jnp.where(eio == idxs[r], 1.0, 0.0)
    before = run_ref[...] + jnp.dot(hot.astype(BF16), tri_ref[...], preferred_element_type=F32)
    ranks = [jnp.sum(jnp.where(eio == idxs[r], before, 0.0), axis=0, keepdims=True) for r in range(TOP_K)]
    run_ref[...] = run_ref[...] + jnp.sum(hot, axis=1, keepdims=True)
    cnt_ref[...] = run_ref[...]
    rio = lax.broadcasted_iota(I32, (ROUTE_W, tm), 0)
    out = jnp.zeros((ROUTE_W, tm), F32)
    for r in range(TOP_K):
        out = jnp.where(rio == r, idxs[r], out)
        out = jnp.where(rio == TOP_K + r, es[r] / den, out)
        out = jnp.where(rio == 2 * TOP_K + r, ranks[r], out)
    rt_ref[0] = out


def _outproj(x, cv, nsa, on, w, gt1, g2, sc2, sh2, wr, br):
    B, S, D = x.shape
    tm = min(OUT_TM, S)
    nt = S // tm
    tri = (jnp.arange(tm)[:, None] < jnp.arange(tm)[None, :]).astype(BF16)
    wr_t = wr.T
    wr_hi = wr_t.astype(BF16)
    wr_lo = (wr_t - wr_hi.astype(F32)).astype(BF16)
    row = pl.BlockSpec((1, 1, D), lambda b, i: (b, 0, 0))
    vec = lambda n: pl.BlockSpec((1, n), lambda b, i: (0, 0))
    col = pl.BlockSpec((N_EXPERTS, 1), lambda b, i: (0, 0))
    wr_spec = pl.BlockSpec((N_EXPERTS, D), lambda b, i: (0, 0))
    return pl.pallas_call(
        _outproj_kernel,
        grid=(B, nt),
        in_specs=[pl.BlockSpec((1, tm, D), lambda b, i: (b, i, 0)),
                  pl.BlockSpec((1, tm, CONV_CH), lambda b, i: (b, i, 0)),
                  pl.BlockSpec((1, tm, NSA_WIDTH), lambda b, i: (b, i, 0)),
                  vec(NSA_WIDTH),
                  pl.BlockSpec((D, D), lambda b, i: (0, 0)),
                  row, vec(D), row, row,
                  wr_spec, wr_spec, col,
                  pl.BlockSpec((tm, tm), lambda b, i: (0, 0))],
        out_specs=[pl.BlockSpec((1, tm, D), lambda b, i: (b, i, 0)),
                   pl.BlockSpec((tm * ROW_TILES, LANES), lambda b, i: (b * nt + i, 0)),
                   pl.BlockSpec((1, ROUTE_W, tm), lambda b, i: (b, 0, i)),
                   col],
        out_shape=[jax.ShapeDtypeStruct((B, S, D), F32),
                   jax.ShapeDtypeStruct((B * S * ROW_TILES, LANES), F32),
                   jax.ShapeDtypeStruct((B, ROUTE_W, S), F32),
                   jax.ShapeDtypeStruct((N_EXPERTS, 1), F32)],
        scratch_shapes=[pltpu.VMEM((N_EXPERTS, 1), F32)],
        compiler_params=pltpu.CompilerParams(dimension_semantics=("arbitrary", "arbitrary")),
        name="outproj",
    )(x, cv, nsa, on, w, gt1, g2, sc2, sh2, wr_hi, wr_lo, br.reshape(N_EXPERTS, 1), tri)


def _issue_rows(idx_ref, n, src_hbm, dst, slot, sem):
    assert n % ISSUE_UNROLL == 0

    def body(c, carry):
        for u in range(ISSUE_UNROLL):
            r = c * ISSUE_UNROLL + u
            pltpu.make_async_copy(src_hbm.at[idx_ref[0, 0, r]],
                                  dst.at[slot, pl.ds(pl.multiple_of(r * ROW_TILES, ROW_TILES), ROW_TILES), :],
                                  sem.at[slot]).start(priority=u % 2)
        return carry
    lax.fori_loop(0, n // ISSUE_UNROLL, body, 0)


def _wait_rows(dst, slot, sem):
    pltpu.make_async_copy(dst.at[slot], dst.at[slot], sem.at[slot]).wait()


def _rows_2d(buf, slot, base, n):
    return jnp.concatenate(
        [buf[slot, pl.ds(base * ROW_TILES + s, n, stride=ROW_TILES), :] for s in range(ROW_TILES)], axis=1)


def _ffn_kernel(be_ref, nb_ref, ta_ref, tb_ref, h2_hbm, gate_ref, wg_ref, bg_ref, wu_ref, bu_ref, wd_ref, bd_ref,
                o_ref, xbuf, wgb, wub, wdb, sem):
    bm = ta_ref.shape[2]
    i = pl.program_id(0)
    nb = nb_ref[0]
    slot = i % 2

    @pl.when((i == 0) | (be_ref[i] != be_ref[jnp.maximum(i - 1, 0)]))
    def _():
        wgb[...] = wg_ref[0].astype(BF16)
        wub[...] = wu_ref[0].astype(BF16)
        wdb[...] = wd_ref[0].astype(BF16)

    @pl.when(i == 0)
    def _():
        _issue_rows(ta_ref, bm, h2_hbm, xbuf, 0, sem)

    @pl.when(i + 1 < nb)
    def _():
        _issue_rows(tb_ref, bm, h2_hbm, xbuf, 1 - slot, sem)

    @pl.when(i < nb)
    def _():
        _wait_rows(xbuf, slot, sem)
        x = _rows_2d(xbuf, slot, 0, bm).astype(BF16)
        g = jnp.dot(x, wgb[...], preferred_element_type=F32) + bg_ref[0]
        u = jnp.dot(x, wub[...], preferred_element_type=F32) + bu_ref[0]
        g = jnp.minimum(g, SWIGLU_LIMIT)
        u = jnp.clip(u, -SWIGLU_LIMIT, SWIGLU_LIMIT)
        act = g * jax.nn.sigmoid(SWIGLU_ALPHA * g) * (u + 1.0)
        y = (jnp.dot(act.astype(BF16), wdb[...], preferred_element_type=F32) + bd_ref[0]) * gate_ref[...]
        for s in range(ROW_TILES):
            o_ref[pl.ds(s, bm, stride=ROW_TILES), :] = y[:, s * LANES:(s + 1) * LANES]

    @pl.when(i >= nb)
    def _():
        o_ref[...] = jnp.zeros(o_ref.shape, o_ref.dtype)


def _ffn(block_e, nb_used, buf_tok, h2_rows, buf_gate, wg, bg, wu, bu, wd, bd):
    NB = block_e.shape[0]
    bm = FFN_BM
    D, F = D_MODEL, D_FF
    tok3 = buf_tok.reshape(NB, 1, bm)
    wspec = lambda r, c: pl.BlockSpec((1, r, c), lambda i, be, nb: (be[i], 0, 0))
    vmem_limit = 2 * 3 * D * F * 4 + 3 * D * F * 2 + 4 * bm * D * 4 + 6 * bm * F * 4
    return pl.pallas_call(
        _ffn_kernel,
        grid_spec=pltpu.PrefetchScalarGridSpec(
            num_scalar_prefetch=2,
            grid=(NB,),
            in_specs=[pl.BlockSpec((1, 1, bm), lambda i, be, nb: (i, 0, 0), memory_space=pltpu.SMEM),
                      pl.BlockSpec((1, 1, bm), lambda i, be, nb: (jnp.minimum(i + 1, NB - 1), 0, 0),
                                   memory_space=pltpu.SMEM),
                      pl.BlockSpec(memory_space=pl.ANY),
                      pl.BlockSpec((bm, 1), lambda i, be, nb: (i, 0)),
                      wspec(D, F), wspec(1, F), wspec(D, F), wspec(1, F), wspec(F, D), wspec(1, D)],
            out_specs=pl.BlockSpec((bm * ROW_TILES, LANES), lambda i, be, nb: (i, 0)),
            scratch_shapes=[pltpu.VMEM((2, bm * ROW_TILES, LANES), F32),
                            pltpu.VMEM((D, F), BF16), pltpu.VMEM((D, F), BF16), pltpu.VMEM((F, D), BF16),
                            pltpu.SemaphoreType.DMA((2,))]),
        out_shape=jax.ShapeDtypeStruct((NB * bm * ROW_TILES, LANES), F32),
        compiler_params=pltpu.CompilerParams(dimension_semantics=("arbitrary",),
                                             vmem_limit_bytes=vmem_limit),
        name="ffn",
    )(block_e, nb_used, tok3, tok3, h2_rows, buf_gate, wg, bg, wu, bu, wd, bd)


def _combine_kernel(da_ref, db_ref, y_hbm, x1_ref, gt2_ref, fg_ref, o_ref, buf, sem):
    tm = x1_ref.shape[0]
    n = TOP_K * tm
    i = pl.program_id(0)
    slot = i % 2

    @pl.when(i == 0)
    def _():
        _issue_rows(da_ref, n, y_hbm, buf, 0, sem)

    @pl.when(i + 1 < pl.num_programs(0))
    def _():
        _issue_rows(db_ref, n, y_hbm, buf, 1 - slot, sem)

    _wait_rows(buf, slot, sem)
    y = _rows_2d(buf, slot, 0, tm)
    for k in range(1, TOP_K):
        y = y + _rows_2d(buf, slot, k * tm, tm)
    x2 = x1_ref[...] + gt2_ref[0] * y
    o_ref[...] = _rms(x2, fg_ref[...])


def _combine(dest3, y_rows, x1, gt2, fg, S):
    T, D = x1.shape
    tm = min(COMB_TM, S)
    NT = T // tm
    per_b = S // tm
    n = TOP_K * tm
    return pl.pallas_call(
        _combine_kernel,
        grid=(NT,),
        in_specs=[pl.BlockSpec((1, 1, n), lambda i: (i, 0, 0), memory_space=pltpu.SMEM),
                  pl.BlockSpec((1, 1, n), lambda i: (jnp.minimum(i + 1, NT - 1), 0, 0), memory_space=pltpu.SMEM),
                  pl.BlockSpec(memory_space=pl.ANY),
                  pl.BlockSpec((tm, D), lambda i: (i, 0)),
                  pl.BlockSpec((1, 1, D), lambda i: (i // per_b, 0, 0)),
                  pl.BlockSpec((1, D), lambda i: (0, 0))],
        out_specs=pl.BlockSpec((tm, D), lambda i: (i, 0)),
        out_shape=jax.ShapeDtypeStruct((T, D), F32),
        scratch_shapes=[pltpu.VMEM((2, n * ROW_TILES, LANES), F32), pltpu.SemaphoreType.DMA((2,))],
        compiler_params=pltpu.CompilerParams(dimension_semantics=("arbitrary",)),
        name="combine",
    )(dest3, dest3, y_rows, x1, gt2, fg)


def _rope_tables(S):
    inv = ROPE_THETA ** (-jnp.arange(0, ROT_DIM, 2, dtype=F32) / ROT_DIM)
    ang = jnp.arange(S, dtype=F32)[:, None] * inv[None, :]
    cos, sin = jnp.cos(ang), jnp.sin(ang)
    d = jnp.arange(KV_W) % HEAD_DIM
    first, second = d < ROT_HALF, (d >= ROT_HALF) & (d < ROT_DIM)
    cos_l = cos[:, d % ROT_HALF]
    sin_l = sin[:, d % ROT_HALF]
    rc = jnp.where((d < ROT_DIM)[None], cos_l, 1.0)
    rs1 = jnp.where(second[None], sin_l, 0.0)
    rs2 = jnp.where(first[None], -sin_l, 0.0)
    return rc, rs1, rs2, cos.T, sin.T


def _route_plan(route, counts, T):
    bm = FFN_BM
    A = T * TOP_K
    tok_major = lambda r: jnp.swapaxes(r, 1, 2).reshape(T, TOP_K)
    flat_e = tok_major(route[:, 0:TOP_K]).astype(I32).reshape(A)
    flat_g = tok_major(route[:, TOP_K:2 * TOP_K]).reshape(A)
    rank = tok_major(route[:, 2 * TOP_K:3 * TOP_K]).astype(I32)
    order = jnp.argsort(flat_e, stable=True).astype(I32)
    counts = counts.astype(I32)
    starts = jnp.cumsum(counts) - counts
    padded = (counts + bm - 1) // bm * bm
    pends = jnp.cumsum(padded)
    pstarts = pends - padded
    P = (A + N_EXPERTS * bm + bm - 1) // bm * bm
    NB = P // bm
    blk0 = jnp.arange(NB, dtype=I32) * bm
    block_e = jnp.minimum(jnp.sum((pends[None, :] <= blk0[:, None]).astype(I32), axis=1), N_EXPERTS - 1)
    r = (blk0 - pstarts[block_e])[:, None] + jnp.arange(bm, dtype=I32)[None, :]
    valid = r < counts[block_e][:, None]
    a_p = order[jnp.clip(starts[block_e][:, None] + r, 0, A - 1)]
    buf_tok = jnp.where(valid, a_p // TOP_K, 0)
    buf_gate = jnp.where(valid, flat_g[a_p], 0.0).reshape(P, 1)
    dest_assign = pstarts[flat_e].reshape(T, TOP_K) + rank
    nb_used = (pends[-1] // bm).astype(I32).reshape(1)
    return block_e, nb_used, buf_tok, buf_gate, dest_assign


def kernel(x, c, norm1_g, norm2_g, w_ada, b_ada, w_in, conv_w, conv_b, conv_ln_g, conv_ln_b, cmp_pe_k, cmp_pe_v,
           cmp_k_w1, cmp_k_w2, cmp_v_w1, cmp_v_w2, out_norm_conv, out_norm_nsa, w_out, w_router, b_router,
           w_gate, b_gate, w_up, b_up, w_down, b_down, final_norm_g):
    B, S, D = x.shape
    T = B * S
    G = KV_GROUPS
    assert D == D_MODEL and S % ATTN_TK == 0 and S % CMP_STRIDE == 0 and KV_W == LANES
    rc, rs1, rs2, cos_t, sin_t = _rope_tables(S)
    n_sel = S // SEL_BLK
    nc = S // CMP_STRIDE
    cstart = jnp.arange(nc) * CMP_STRIDE
    jstart = jnp.arange(n_sel) * SEL_BLK
    overlap_t = ((cstart[None, :] <= jstart[:, None] + SEL_BLK - 1)
                 & (cstart[None, :] + CMP_LEN - 1 >= jstart[:, None])
                 & (jnp.arange(nc)[None, :] < nc - 1)).astype(F32)

    assert w_ada.shape[0] == 1
    for l in range(1):
        mod = _adaln(c, w_ada[l], b_ada[l][None])
        sh1, sc1, gt1, sh2, sc2, gt2 = [m[:, None, :] for m in jnp.split(mod, 6, axis=-1)]

        wl = w_in[l]
        o = 2 * CONV_CH + NSA_WIDTH
        kvc = [wl[:, o + i * KV_W:o + (i + 1) * KV_W] for i in range(6)]
        gl = wl[:, o + 6 * KV_W:]
        per_g = 3 * Q_PER_G
        gpad = [jnp.pad(gl[:, per_g * g:per_g * (g + 1)], ((0, 0), (0, GATE_ROWS - per_g))) for g in range(G)]
        wn = jnp.concatenate([wl[:, :2 * CONV_CH], kvc[0], kvc[2], kvc[4], kvc[1]], axis=1).astype(BF16)
        wt = jnp.concatenate([wl[:, 2 * CONV_CH:o], kvc[3], kvc[5]] + gpad, axis=1).T.astype(BF16)
        u, kc, vc, ks, kw, qt, vst, vwt, gates = _inproj(x, sc1, sh1, norm1_g[l][None], wn, wt,
                                                         rc, rs1, rs2, cos_t, sin_t)

        conv_n = _conv(u, conv_w[l], conv_b[l][None], conv_ln_g[l][None], conv_ln_b[l][None],
                       out_norm_conv[l][None])

        chunk = CMP_STRIDE * HEAD_DIM
        kcmp, vcmp_t = _compress(kc.reshape(B, G, nc, chunk), vc.reshape(B, G, nc, chunk),
                                 cmp_pe_k[l].reshape(1, -1), cmp_pe_v[l].reshape(1, -1),
                                 cmp_k_w1[l], cmp_k_w2[l], cmp_v_w1[l], cmp_v_w2[l].T)
        nsa = _attention(qt, kcmp, vcmp_t, ks, vst, kw, vwt, gates, overlap_t)

        x1, h2_rows, route, counts = _outproj(x, conv_n, nsa, out_norm_nsa[l][None], w_out[l].astype(BF16), gt1,
                                              norm2_g[l][None], sc2, sh2, w_router[l], b_router[l][None])

        block_e, nb_used, buf_tok, buf_gate, dest_assign = _route_plan(route, counts[:, 0], T)
        y_rows = _ffn(block_e, nb_used, buf_tok, h2_rows.reshape(T, ROW_TILES, LANES), buf_gate,
                      w_gate[l], b_gate[l][:, None, :], w_up[l], b_up[l][:, None, :],
                      w_down[l], b_down[l][:, None, :])
        tm = min(COMB_TM, S)
        dest3 = dest_assign.reshape(T // tm, tm, TOP_K).transpose(0, 2, 1).reshape(T // tm, 1, TOP_K * tm)
        P = y_rows.shape[0] // ROW_TILES
        x = _combine(dest3, y_rows.reshape(P, ROW_TILES, LANES), x1.reshape(T, D), gt2, final_norm_g[None],
                     S).reshape(B, S, D)
    return x
```

```python
import functools

import jax
import jax.numpy as jnp
from jax import lax
from jax.experimental import pallas as pl
from jax.experimental.pallas import tpu as pltpu

F32 = jnp.float32
BF16 = jnp.bfloat16
I32 = jnp.int32
HI = lax.Precision.HIGHEST

D_MODEL = 1024
CONV_CH = 512
CONV_WIDTH = 31
NSA_HEADS = 8
KV_GROUPS = 2
Q_PER_G = NSA_HEADS // KV_GROUPS
HEAD_DIM = 64
NSA_WIDTH = NSA_HEADS * HEAD_DIM
KV_W = KV_GROUPS * HEAD_DIM
ROT_DIM = HEAD_DIM // 4
ROT_HALF = ROT_DIM // 2
ROPE_THETA = 500000.0
CMP_LEN = 32
CMP_STRIDE = 16
CMP_HIDDEN = 128
SEL_BLK = 64
SEL_TOPN = 16
WINDOW = 512
N_EXPERTS = 32
TOP_K = 4
D_FF = 1024
SWIGLU_ALPHA = 1.702
SWIGLU_LIMIT = 7.0
NORM_EPS = 1e-5
NEG_INF = -1e30
FORCE_SCORE = 1e9
LOG2_E = 1.4426950408889634

LANES = 128
SUBLANES = 8
ROW_TILES = D_MODEL // LANES

GATE_ROWS = 16

INPROJ_TM = 512
CONV_TR = 256
CONV_HALO = 32
ATTN_TQ = 256
ATTN_TK = 256
OUT_TM = 512
FFN_BM = 512
COMB_TM = 256
ROUTE_W = 16
ISSUE_UNROLL = 8


def _rms(x, g):
    return x * lax.rsqrt(jnp.mean(x * x, axis=-1, keepdims=True) + NORM_EPS) * g


def _ada_kernel(c_ref, w_ref, b_ref, o_ref):
    c = c_ref[...]
    ca = c * jax.nn.sigmoid(c)
    o_ref[...] = jnp.dot(ca, w_ref[...], preferred_element_type=F32, precision=HI) + b_ref[...]


def _adaln(c, w, b):
    B = c.shape[0]
    D = D_MODEL
    return pl.pallas_call(
        _ada_kernel,
        grid=(6,),
        in_specs=[pl.BlockSpec((B, D), lambda j: (0, 0)),
                  pl.BlockSpec((D, D), lambda j: (0, j)),
                  pl.BlockSpec((1, D), lambda j: (0, j))],
        out_specs=pl.BlockSpec((B, D), lambda j: (0, j)),
        out_shape=jax.ShapeDtypeStruct((B, 6 * D), F32),
        name="adaln",
    )(c, w, b)


def _inproj_kernel(x_ref, sc_ref, sh_ref, g_ref, wn_ref, wt_ref, rc_ref, rs1_ref, rs2_ref, ct_ref, st_ref,
                   u_ref, kc_ref, vc_ref, ks_ref, kw_ref, qt_ref, vst_ref, vwt_ref, gt_ref):
    tm = x_ref.shape[1]
    tq, tk = ATTN_TQ, ATTN_TK
    h = (_rms(x_ref[0], g_ref[...]) * (1.0 + sc_ref[0]) + sh_ref[0]).astype(BF16)

    p = jnp.dot(h, wn_ref[...], preferred_element_type=F32)
    u_ref[0] = p[:, 0:CONV_CH] * jax.nn.sigmoid(p[:, CONV_CH:2 * CONV_CH])
    c0 = 2 * CONV_CH
    rc, rs1, rs2 = rc_ref[...], rs1_ref[...], rs2_ref[...]
    for ref, roped in ((kc_ref, True), (ks_ref, True), (kw_ref, True), (vc_ref, False)):
        v = p[:, c0:c0 + KV_W]
        if roped:
            v = v * rc + pltpu.roll(v, ROT_HALF, 1) * rs1 + pltpu.roll(v, KV_W - ROT_HALF, 1) * rs2
        for gg in range(KV_GROUPS):
            ref[0, gg] = v[:, HEAD_DIM * gg:HEAD_DIM * (gg + 1)].astype(ref.dtype)
        c0 += KV_W

    pt = lax.dot_general(wt_ref[...], h, (((1,), (1,)), ((), ())), preferred_element_type=F32)
    cos_t, sin_t = ct_ref[...], st_ref[...]
    scale = HEAD_DIM ** -0.5 * LOG2_E
    for hh in range(NSA_HEADS):
        blk = pt[HEAD_DIM * hh:HEAD_DIM * (hh + 1), :]
        x1, x2 = blk[0:ROT_HALF], blk[ROT_HALF:ROT_DIM]
        qh = (jnp.concatenate([x1 * cos_t - x2 * sin_t, x2 * cos_t + x1 * sin_t, blk[ROT_DIM:]], axis=0)
              * scale).astype(BF16)
        gg, n = divmod(hh, Q_PER_G)
        for j in range(tm // tq):
            qt_ref[0, gg, j, :, n * tq:(n + 1) * tq] = qh[:, j * tq:(j + 1) * tq]
    r0 = NSA_WIDTH
    for ref in (vst_ref, vwt_ref):
        for gg in range(KV_GROUPS):
            blk = pt[r0 + HEAD_DIM * gg:r0 + HEAD_DIM * (gg + 1), :].astype(BF16)
            for j in range(tm // tk):
                ref[0, gg, j] = blk[:, j * tk:(j + 1) * tk]
        r0 += KV_W
    for gg in range(KV_GROUPS):
        gt_ref[0, gg] = jax.nn.sigmoid(pt[r0 + GATE_ROWS * gg:r0 + GATE_ROWS * (gg + 1), :])


def _inproj(x, sc, sh, g, wn, wt, rc, rs1, rs2, cos_t, sin_t):
    B, S, D = x.shape
    tm = min(INPROJ_TM, S)
    tq, tk = ATTN_TQ, ATTN_TK
    G = KV_GROUPS
    kv = lambda dt: jax.ShapeDtypeStruct((B, G, S, HEAD_DIM), dt)
    kv_spec = pl.BlockSpec((1, G, tm, HEAD_DIM), lambda b, i: (b, 0, i, 0))
    vt_shape = jax.ShapeDtypeStruct((B, G, S // tk, HEAD_DIM, tk), BF16)
    vt_spec = pl.BlockSpec((1, G, tm // tk, HEAD_DIM, tk), lambda b, i: (b, 0, i, 0, 0))
    row = pl.BlockSpec((1, 1, D), lambda b, i: (b, 0, 0))
    tab = pl.BlockSpec((tm, LANES), lambda b, i: (i, 0))
    tab_t = pl.BlockSpec((ROT_HALF, tm), lambda b, i: (0, i))
    return pl.pallas_call(
        _inproj_kernel,
        grid=(B, S // tm),
        in_specs=[pl.BlockSpec((1, tm, D), lambda b, i: (b, i, 0)), row, row,
                  pl.BlockSpec((1, D), lambda b, i: (0, 0)),
                  pl.BlockSpec(wn.shape, lambda b, i: (0, 0)),
                  pl.BlockSpec(wt.shape, lambda b, i: (0, 0)),
                  tab, tab, tab, tab_t, tab_t],
        out_specs=[pl.BlockSpec((1, tm, CONV_CH), lambda b, i: (b, i, 0)),
                   kv_spec, kv_spec, kv_spec, kv_spec,
                   pl.BlockSpec((1, G, tm // tq, HEAD_DIM, Q_PER_G * tq), lambda b, i: (b, 0, i, 0, 0)),
                   vt_spec, vt_spec,
                   pl.BlockSpec((1, G, GATE_ROWS, tm), lambda b, i: (b, 0, 0, i))],
        out_shape=[jax.ShapeDtypeStruct((B, S, CONV_CH), F32),
                   kv(F32), kv(F32), kv(BF16), kv(BF16),
                   jax.ShapeDtypeStruct((B, G, S // tq, HEAD_DIM, Q_PER_G * tq), BF16),
                   vt_shape, vt_shape,
                   jax.ShapeDtypeStruct((B, G, GATE_ROWS, S), F32)],
        compiler_params=pltpu.CompilerParams(dimension_semantics=("parallel", "parallel")),
        name="inproj",
    )(x, sc, sh, g, wn, wt, rc, rs1, rs2, cos_t, sin_t)


def _conv_kernel(prev_ref, cur_ref, w_ref, cb_ref, lg_ref, lb_ref, on_ref, o_ref, pad_ref):
    tr = cur_ref.shape[1]
    first = pl.program_id(1) == 0
    halo = prev_ref[0, tr - CONV_HALO:tr, :]
    pad_ref[0:CONV_HALO, :] = jnp.where(first, 0.0, halo)
    pad_ref[CONV_HALO:CONV_HALO + tr, :] = cur_ref[0]
    off = CONV_HALO - (CONV_WIDTH - 1)
    acc = jnp.zeros((tr, CONV_CH), F32)
    for b in range(SUBLANES):
        taps = range(b, CONV_WIDTH, SUBLANES)
        win = pad_ref[off + b:off + b + tr + SUBLANES * (len(taps) - 1), :]
        for a, k in enumerate(taps):
            acc = acc + win[SUBLANES * a:SUBLANES * a + tr, :] * w_ref[k:k + 1, :]
    y = acc + cb_ref[...]
    mu = jnp.mean(y, axis=-1, keepdims=True)
    yc = y - mu
    var = jnp.mean(yc * yc, axis=-1, keepdims=True)
    yn = yc * lax.rsqrt(var + NORM_EPS) * lg_ref[...] + lb_ref[...]
    s = yn * jax.nn.sigmoid(yn)
    o_ref[0] = _rms(s, on_ref[...]).astype(o_ref.dtype)


def _conv(u, w, cb, lg, lb, on):
    B, S, C = u.shape
    tr = min(CONV_TR, S)
    vec = pl.BlockSpec((1, C), lambda b, i: (0, 0))
    return pl.pallas_call(
        _conv_kernel,
        grid=(B, S // tr),
        in_specs=[pl.BlockSpec((1, tr, C), lambda b, i: (b, jnp.maximum(i - 1, 0), 0)),
                  pl.BlockSpec((1, tr, C), lambda b, i: (b, i, 0)),
                  pl.BlockSpec((CONV_WIDTH, C), lambda b, i: (0, 0)),
                  vec, vec, vec, vec],
        out_specs=pl.BlockSpec((1, tr, C), lambda b, i: (b, i, 0)),
        out_shape=jax.ShapeDtypeStruct((B, S, C), BF16),
        scratch_shapes=[pltpu.VMEM((CONV_HALO + tr, C), F32)],
        compiler_params=pltpu.CompilerParams(dimension_semantics=("parallel", "parallel")),
        name="conv",
    )(u, u, w, cb, lg, lb, on)


def _cmp_kernel(kx_ref, vx_ref, pek_ref, pev_ref, kw1_ref, kw2_ref, vw1_ref, vw2t_ref, ko_ref, vo_ref):
    nc = kx_ref.shape[2]
    half = kx_ref.shape[3]
    nt = (((1,), (1,)), ((), ()))
    for x_ref, pe_ref, w1_ref, w2_ref, o_ref, transposed in ((kx_ref, pek_ref, kw1_ref, kw2_ref, ko_ref, False),
                                                             (vx_ref, pev_ref, vw1_ref, vw2t_ref, vo_ref, True)):
        w1 = w1_ref[...]
        pe = jnp.broadcast_to(pe_ref[...], (SUBLANES, 2 * half))
        pe_proj = jnp.dot(pe, w1, preferred_element_type=F32, precision=HI)[0:1]
        for gg in range(KV_GROUPS):
            xg = x_ref[0, gg]
            first = jnp.dot(xg, w1[0:half], preferred_element_type=F32, precision=HI)
            second = jnp.dot(xg, w1[half:2 * half], preferred_element_type=F32, precision=HI)
            hid = first + pltpu.roll(second, nc - 1, 0) + pe_proj
            hid = hid * jax.nn.sigmoid(hid)
            if transposed:
                o_ref[0, gg] = lax.dot_general(w2_ref[...], hid, nt, preferred_element_type=F32, precision=HI)
            else:
                o_ref[0, gg] = jnp.dot(hid, w2_ref[...], preferred_element_type=F32, precision=HI)


def _compress(kx, vx, pek, pev, kw1, kw2, vw1, vw2t):
    B, G, NC, W = kx.shape
    xs = pl.BlockSpec((1, G, NC, W), lambda b: (b, 0, 0, 0))
    full = lambda a: pl.BlockSpec(a.shape, lambda b: (0,) * a.ndim)
    return pl.pallas_call(
        _cmp_kernel,
        grid=(B,),
        in_specs=[xs, xs, full(pek), full(pev), full(kw1), full(kw2), full(vw1), full(vw2t)],
        out_specs=[pl.BlockSpec((1, G, NC, HEAD_DIM), lambda b: (b, 0, 0, 0)),
                   pl.BlockSpec((1, G, HEAD_DIM, NC), lambda b: (b, 0, 0, 0))],
        out_shape=[jax.ShapeDtypeStruct((B, G, NC, HEAD_DIM), F32),
                   jax.ShapeDtypeStruct((B, G, HEAD_DIM, NC), F32)],
        compiler_params=pltpu.CompilerParams(dimension_semantics=("parallel",)),
        name="compress",
    )(kx, vx, pek, pev, kw1, kw2, vw1, vw2t)


def _attn_kernel(qt_ref, kc_ref, vct_ref, ks_ref, vst_ref, kw_ref, vwt_ref, gt_ref, ovt_ref, o_ref, sel_ref):
    tq, tk = ATTN_TQ, ATTN_TK
    R = Q_PER_G * tq
    per_tile = tk // SEL_BLK
    S = ks_ref.shape[2]
    ncp = kc_ref.shape[2]
    nsel = ovt_ref.shape[0]
    qi = pl.program_id(2)
    q0 = qi * tq
    qt = qt_ref[0, 0, 0]
    t_row = q0 + lax.broadcasted_iota(I32, (1, tq), 1)
    heads = lambda a: jnp.concatenate([a] * Q_PER_G, axis=1)

    sc = jnp.dot(kc_ref[0, 0].astype(BF16), qt, preferred_element_type=F32)
    c_io = lax.broadcasted_iota(I32, (ncp, tq), 0)
    m_c = (c_io * CMP_STRIDE + (CMP_LEN - 1) <= t_row) & (c_io < ncp - 1)
    scb = sc + heads(jnp.where(m_c, 0.0, NEG_INF))
    e = jnp.exp2(scb - jnp.max(scb, axis=0, keepdims=True)) * heads(jnp.where(m_c, 1.0, 0.0))
    den = jnp.sum(e, axis=0, keepdims=True)
    pc = e / jnp.where(den > 0.0, den, 1.0)
    o_cmp = jnp.dot(vct_ref[0, 0].astype(BF16), pc.astype(BF16), preferred_element_type=F32)

    psum = pc[:, 0:tq]
    for n in range(1, Q_PER_G):
        psum = psum + pc[:, n * tq:(n + 1) * tq]
    imp = jnp.dot(ovt_ref[...], psum, preferred_element_type=F32, precision=HI)
    j_io = lax.broadcasted_iota(I32, (nsel, tq), 0)
    cur = t_row // SEL_BLK
    valid = j_io * SEL_BLK <= t_row
    forced = (j_io == 0) | (j_io == cur) | (j_io == cur - 1)
    score = jnp.where(valid, jnp.where(forced, FORCE_SCORE, imp), NEG_INF)
    rank = jnp.zeros((nsel, tq), F32)
    for i in range(nsel):
        row = score[i:i + 1, :]
        tie = jnp.where(j_io > i, 1.0, 0.0)
        rank = rank + jnp.where(row > score, 1.0, jnp.where(row == score, tie, 0.0))
    sel_bias = jnp.where(rank < float(min(SEL_TOPN, nsel)), 0.0, NEG_INF)
    sel_ref[...] = jnp.zeros(sel_ref.shape, F32)
    for jj in range(S // tk):
        sel_ref[jj, 0:per_tile, :] = sel_bias[jj * per_tile:(jj + 1) * per_tile, :]

    k_io = lax.broadcasted_iota(I32, (tk, tq), 0)

    def scores(k_ref, kj):
        k0 = pl.multiple_of(kj * tk, tk)
        return jnp.dot(k_ref[0, 0, pl.ds(k0, tk), :], qt, preferred_element_type=F32)

    def run_branch(k_ref, vt_ref, lo, hi, bias_fn):
        def body(kj, carry):
            m, l, acc, s_cur, p_prev, alpha_prev = carry
            s_next = scores(k_ref, jnp.minimum(kj + 1, hi - 1))
            acc = alpha_prev * acc + jnp.dot(vt_ref[0, 0, jnp.maximum(kj - 1, lo)], p_prev,
                                             preferred_element_type=F32)
            s = s_cur + heads(bias_fn(kj))
            m_new = jnp.maximum(m, jnp.max(s, axis=0, keepdims=True))
            alpha = jnp.exp2(m - m_new)
            p = jnp.exp2(s - m_new)
            l = alpha * l + jnp.sum(p, axis=0, keepdims=True)
            return m_new, l, acc, s_next, p.astype(BF16), alpha

        init = (jnp.full((1, R), NEG_INF, F32), jnp.zeros((1, R), F32), jnp.zeros((HEAD_DIM, R), F32),
                scores(k_ref, lo), jnp.zeros((tk, R), BF16), jnp.ones((1, R), F32))
        _, l, acc, _, p_last, alpha_last = lax.fori_loop(lo, hi, body, init)
        acc = alpha_last * acc + jnp.dot(vt_ref[0, 0, hi - 1], p_last, preferred_element_type=F32)
        return l, acc

    def slc_bias(kj):
        blocks = sel_ref[kj]
        bias = jnp.concatenate([jnp.broadcast_to(blocks[b:b + 1, :], (SEL_BLK, tq)) for b in range(per_tile)], axis=0)
        return jnp.where(kj * tk + k_io <= t_row, bias, NEG_INF)

    def win_bias(kj):
        rel = t_row - (kj * tk + k_io)
        return jnp.where((rel >= 0) & (rel < WINDOW), 0.0, NEG_INF)

    n_slc = (q0 + tq + tk - 1) // tk
    lo_tile = jnp.maximum(q0 - (WINDOW - 1), 0) // tk
    l_s, acc_s = run_branch(ks_ref, vst_ref, 0, n_slc, slc_bias)
    l_w, acc_w = run_branch(kw_ref, vwt_ref, lo_tile, n_slc, win_bias)


    gt = gt_ref[0, 0]
    o_slc = acc_s / l_s
    o_win = acc_w / l_w
    outs = []
    for n in range(Q_PER_G):
        cols = slice(n * tq, (n + 1) * tq)
        outs.append(gt[3 * n:3 * n + 1, :] * o_cmp[:, cols] + gt[3 * n + 1:3 * n + 2, :] * o_slc[:, cols]
                    + gt[3 * n + 2:3 * n + 3, :] * o_win[:, cols])
    o_ref[0] = jnp.concatenate(outs, axis=0).T


def _attention(qt, kc, vct, ks, vst, kw, vwt, gt, overlap_t):
    B, G, NQT, _, R = qt.shape
    S = ks.shape[2]
    tq, tk = ATTN_TQ, ATTN_TK
    ncp = kc.shape[2]
    kfull = pl.BlockSpec((1, 1, S, HEAD_DIM), lambda b, g, i: (b, g, 0, 0))
    vfull = pl.BlockSpec((1, 1, S // tk, HEAD_DIM, tk), lambda b, g, i: (b, g, 0, 0, 0))
    return pl.pallas_call(
        _attn_kernel,
        grid=(B, G, NQT),
        in_specs=[pl.BlockSpec((1, 1, 1, HEAD_DIM, R), lambda b, g, i: (b, g, i, 0, 0)),
                  pl.BlockSpec((1, 1, ncp, HEAD_DIM), lambda b, g, i: (b, g, 0, 0)),
                  pl.BlockSpec((1, 1, HEAD_DIM, ncp), lambda b, g, i: (b, g, 0, 0)),
                  kfull, vfull, kfull, vfull,
                  pl.BlockSpec((1, 1, GATE_ROWS, tq), lambda b, g, i: (b, g, 0, i)),
                  pl.BlockSpec(overlap_t.shape, lambda b, g, i: (0, 0))],
        out_specs=pl.BlockSpec((1, tq, Q_PER_G * HEAD_DIM), lambda b, g, i: (b, i, g)),
        out_shape=jax.ShapeDtypeStruct((B, S, NSA_WIDTH), F32),
        scratch_shapes=[pltpu.VMEM((S // tk, SUBLANES, tq), F32)],
        compiler_params=pltpu.CompilerParams(dimension_semantics=("parallel", "parallel", "arbitrary")),
        name="attn",
    )(qt, kc, vct, ks, vst, kw, vwt, gt, overlap_t)


def _outproj_kernel(x_ref, cv_ref, nsa_ref, on_ref, w_ref, gt1_ref, g2_ref, sc2_ref, sh2_ref, wrh_ref, wrl_ref, br_ref,
                    tri_ref, x1_ref, h2_ref, rt_ref, cnt_ref, run_ref):
    tm = x_ref.shape[1]

    @pl.when((pl.program_id(0) == 0) & (pl.program_id(1) == 0))
    def _():
        run_ref[...] = jnp.zeros(run_ref.shape, F32)

    nn = _rms(nsa_ref[0], on_ref[...]).astype(BF16)
    y = (jnp.dot(cv_ref[0], w_ref[0:CONV_CH, :], preferred_element_type=F32)
         + jnp.dot(nn, w_ref[CONV_CH:CONV_CH + NSA_WIDTH, :], preferred_element_type=F32))
    x1 = x_ref[0] + gt1_ref[0] * y
    x1_ref[0] = x1
    h2 = _rms(x1, g2_ref[...]) * (1.0 + sc2_ref[0]) + sh2_ref[0]
    for s in range(ROW_TILES):
        h2_ref[pl.ds(s, tm, stride=ROW_TILES), :] = h2[:, s * LANES:(s + 1) * LANES]
    nt = (((1,), (1,)), ((), ()))
    h_hi = h2.astype(BF16)
    h_lo = (h2 - h_hi.astype(F32)).astype(BF16)
    logits = (lax.dot_general(wrh_ref[...], h_hi, nt, preferred_element_type=F32)
              + lax.dot_general(wrh_ref[...], h_lo, nt, preferred_element_type=F32)
              + lax.dot_general(wrl_ref[...], h_hi, nt, preferred_element_type=F32)) + br_ref[...]
    eio = lax.broadcasted_iota(I32, (N_EXPERTS, tm), 0).astype(F32)
    vals, idxs = [], []
    for _ in range(TOP_K):
        m = jnp.max(logits, axis=0, keepdims=True)
        ix = jnp.min(jnp.where(logits == m, eio, float(N_EXPERTS)), axis=0, keepdims=True)
        vals.append(m)
        idxs.append(ix)
        logits = jnp.where(eio == ix, -jnp.inf, logits)
    es = [jnp.exp(v - vals[0]) for v in vals]
    den = es[0] + es[1] + es[2] + es[3]
    hot = jnp.zeros((N_EXPERTS, tm), F32)
    for r in range(TOP_K):
        hot = hot + jnp.where(eio == idxs[r], 1.0, 0.0)
    before = run_ref[...] + jnp.dot(hot.astype(BF16), tri_ref[...], preferred_element_type=F32)
    ranks = [jnp.sum(jnp.where(eio == idxs[r], before, 0.0), axis=0, keepdims=True) for r in range(TOP_K)]
    run_ref[...] = run_ref[...] + jnp.sum(hot, axis=1, keepdims=True)
    cnt_ref[...] = run_ref[...]
    rio = lax.broadcasted_iota(I32, (ROUTE_W, tm), 0)
    out = jnp.zeros((ROUTE_W, tm), F32)
    for r in range(TOP_K):
        out = jnp.where(rio == r, idxs[r], out)
        out = jnp.where(rio == TOP_K + r, es[r] / den, out)
        out = jnp.where(rio == 2 * TOP_K + r, ranks[r], out)
    rt_ref[0] = out


def _outproj(x, cv, nsa, on, w, gt1, g2, sc2, sh2, wr, br):
    B, S, D = x.shape
    tm = min(OUT_TM, S)
    nt = S // tm
    tri = (jnp.arange(tm)[:, None] < jnp.arange(tm)[None, :]).astype(BF16)
    wr_t = wr.T
    wr_hi = wr_t.astype(BF16)
    wr_lo = (wr_t - wr_hi.astype(F32)).astype(BF16)
    row = pl.BlockSpec((1, 1, D), lambda b, i: (b, 0, 0))
    vec = lambda n: pl.BlockSpec((1, n), lambda b, i: (0, 0))
    col = pl.BlockSpec((N_EXPERTS, 1), lambda b, i: (0, 0))
    wr_spec = pl.BlockSpec((N_EXPERTS, D), lambda b, i: (0, 0))
    return pl.pallas_call(
        _outproj_kernel,
        grid=(B, nt),
        in_specs=[pl.BlockSpec((1, tm, D), lambda b, i: (b, i, 0)),
                  pl.BlockSpec((1, tm, CONV_CH), lambda b, i: (b, i, 0)),
                  pl.BlockSpec((1, tm, NSA_WIDTH), lambda b, i: (b, i, 0)),
                  vec(NSA_WIDTH),
                  pl.BlockSpec((D, D), lambda b, i: (0, 0)),
                  row, vec(D), row, row,
                  wr_spec, wr_spec, col,
                  pl.BlockSpec((tm, tm), lambda b, i: (0, 0))],
        out_specs=[pl.BlockSpec((1, tm, D), lambda b, i: (b, i, 0)),
                   pl.BlockSpec((tm * ROW_TILES, LANES), lambda b, i: (b * nt + i, 0)),
                   pl.BlockSpec((1, ROUTE_W, tm), lambda b, i: (b, 0, i)),
                   col],
        out_shape=[jax.ShapeDtypeStruct((B, S, D), F32),
                   jax.ShapeDtypeStruct((B * S * ROW_TILES, LANES), F32),
                   jax.ShapeDtypeStruct((B, ROUTE_W, S), F32),
                   jax.ShapeDtypeStruct((N_EXPERTS, 1), F32)],
        scratch_shapes=[pltpu.VMEM((N_EXPERTS, 1), F32)],
        compiler_params=pltpu.CompilerParams(dimension_semantics=("arbitrary", "arbitrary")),
        name="outproj",
    )(x, cv, nsa, on, w, gt1, g2, sc2, sh2, wr_hi, wr_lo, br.reshape(N_EXPERTS, 1), tri)


def _issue_rows(idx_ref, n, src_hbm, dst, slot, sem):
    assert n % ISSUE_UNROLL == 0

    def body(c, carry):
        for u in range(ISSUE_UNROLL):
            r = c * ISSUE_UNROLL + u
            pltpu.make_async_copy(src_hbm.at[idx_ref[0, 0, r]],
                                  dst.at[slot, pl.ds(pl.multiple_of(r * ROW_TILES, ROW_TILES), ROW_TILES), :],
                                  sem.at[slot]).start(priority=u % 2)
        return carry
    lax.fori_loop(0, n // ISSUE_UNROLL, body, 0)


def _wait_rows(dst, slot, sem):
    pltpu.make_async_copy(dst.at[slot], dst.at[slot], sem.at[slot]).wait()


def _rows_2d(buf, slot, base, n):
    return jnp.concatenate(
        [buf[slot, pl.ds(base * ROW_TILES + s, n, stride=ROW_TILES), :] for s in range(ROW_TILES)], axis=1)


def _ffn_kernel(be_ref, nb_ref, ta_ref, tb_ref, h2_hbm, gate_ref, wg_ref, bg_ref, wu_ref, bu_ref, wd_ref, bd_ref,
                o_ref, xbuf, wgb, wub, wdb, sem):
    bm = ta_ref.shape[2]
    i = pl.program_id(0)
    nb = nb_ref[0]
    slot = i % 2

    @pl.when((i == 0) | (be_ref[i] != be_ref[jnp.maximum(i - 1, 0)]))
    def _():
        wgb[...] = wg_ref[0].astype(BF16)
        wub[...] = wu_ref[0].astype(BF16)
        wdb[...] = wd_ref[0].astype(BF16)

    @pl.when(i == 0)
    def _():
        _issue_rows(ta_ref, bm, h2_hbm, xbuf, 0, sem)

    @pl.when(i + 1 < nb)
    def _():
        _issue_rows(tb_ref, bm, h2_hbm, xbuf, 1 - slot, sem)

    @pl.when(i < nb)
    def _():
        _wait_rows(xbuf, slot, sem)
        x = _rows_2d(xbuf, slot, 0, bm).astype(BF16)
        g = jnp.dot(x, wgb[...], preferred_element_type=F32) + bg_ref[0]
        u = jnp.dot(x, wub[...], preferred_element_type=F32) + bu_ref[0]
        g = jnp.minimum(g, SWIGLU_LIMIT)
        u = jnp.clip(u, -SWIGLU_LIMIT, SWIGLU_LIMIT)
        act = g * jax.nn.sigmoid(SWIGLU_ALPHA * g) * (u + 1.0)
        y = (jnp.dot(act.astype(BF16), wdb[...], preferred_element_type=F32) + bd_ref[0]) * gate_ref[...]
        for s in range(ROW_TILES):
            o_ref[pl.ds(s, bm, stride=ROW_TILES), :] = y[:, s * LANES:(s + 1) * LANES]

    @pl.when(i >= nb)
    def _():
        o_ref[...] = jnp.zeros(o_ref.shape, o_ref.dtype)


def _ffn(block_e, nb_used, buf_tok, h2_rows, buf_gate, wg, bg, wu, bu, wd, bd):
    NB = block_e.shape[0]
    bm = FFN_BM
    D, F = D_MODEL, D_FF
    tok3 = buf_tok.reshape(NB, 1, bm)
    wspec = lambda r, c: pl.BlockSpec((1, r, c), lambda i, be, nb: (be[i], 0, 0))
    vmem_limit = 2 * 3 * D * F * 4 + 3 * D * F * 2 + 4 * bm * D * 4 + 6 * bm * F * 4
    return pl.pallas_call(
        _ffn_kernel,
        grid_spec=pltpu.PrefetchScalarGridSpec(
            num_scalar_prefetch=2,
            grid=(NB,),
            in_specs=[pl.BlockSpec((1, 1, bm), lambda i, be, nb: (i, 0, 0), memory_space=pltpu.SMEM),
                      pl.BlockSpec((1, 1, bm), lambda i, be, nb: (jnp.minimum(i + 1, NB - 1), 0, 0),
                                   memory_space=pltpu.SMEM),
                      pl.BlockSpec(memory_space=pl.ANY),
                      pl.BlockSpec((bm, 1), lambda i, be, nb: (i, 0)),
                      wspec(D, F), wspec(1, F), wspec(D, F), wspec(1, F), wspec(F, D), wspec(1, D)],
            out_specs=pl.BlockSpec((bm * ROW_TILES, LANES), lambda i, be, nb: (i, 0)),
            scratch_shapes=[pltpu.VMEM((2, bm * ROW_TILES, LANES), F32),
                            pltpu.VMEM((D, F), BF16), pltpu.VMEM((D, F), BF16), pltpu.VMEM((F, D), BF16),
                            pltpu.SemaphoreType.DMA((2,))]),
        out_shape=jax.ShapeDtypeStruct((NB * bm * ROW_TILES, LANES), F32),
        compiler_params=pltpu.CompilerParams(dimension_semantics=("arbitrary",),
                                             vmem_limit_bytes=vmem_limit),
        name="ffn",
    )(block_e, nb_used, tok3, tok3, h2_rows, buf_gate, wg, bg, wu, bu, wd, bd)


def _combine_kernel(da_ref, db_ref, y_hbm, x1_ref, gt2_ref, fg_ref, o_ref, buf, sem):
    tm = x1_ref.shape[0]
    n = TOP_K * tm
    i = pl.program_id(0)
    slot = i % 2

    @pl.when(i == 0)
    def _():
        _issue_rows(da_ref, n, y_hbm, buf, 0, sem)

    @pl.when(i + 1 < pl.num_programs(0))
    def _():
        _issue_rows(db_ref, n, y_hbm, buf, 1 - slot, sem)

    _wait_rows(buf, slot, sem)
    y = _rows_2d(buf, slot, 0, tm)
    for k in range(1, TOP_K):
        y = y + _rows_2d(buf, slot, k * tm, tm)
    x2 = x1_ref[...] + gt2_ref[0] * y
    o_ref[...] = _rms(x2, fg_ref[...])


def _combine(dest3, y_rows, x1, gt2, fg, S):
    T, D = x1.shape
    tm = min(COMB_TM, S)
    NT = T // tm
    per_b = S // tm
    n = TOP_K * tm
    return pl.pallas_call(
        _combine_kernel,
        grid=(NT,),
        in_specs=[pl.BlockSpec((1, 1, n), lambda i: (i, 0, 0), memory_space=pltpu.SMEM),
                  pl.BlockSpec((1, 1, n), lambda i: (jnp.minimum(i + 1, NT - 1), 0, 0), memory_space=pltpu.SMEM),
                  pl.BlockSpec(memory_space=pl.ANY),
                  pl.BlockSpec((tm, D), lambda i: (i, 0)),
                  pl.BlockSpec((1, 1, D), lambda i: (i // per_b, 0, 0)),
                  pl.BlockSpec((1, D), lambda i: (0, 0))],
        out_specs=pl.BlockSpec((tm, D), lambda i: (i, 0)),
        out_shape=jax.ShapeDtypeStruct((T, D), F32),
        scratch_shapes=[pltpu.VMEM((2, n * ROW_TILES, LANES), F32), pltpu.SemaphoreType.DMA((2,))],
        compiler_params=pltpu.CompilerParams(dimension_semantics=("arbitrary",)),
        name="combine",
    )(dest3, dest3, y_rows, x1, gt2, fg)


def _rope_tables(S):
    inv = ROPE_THETA ** (-jnp.arange(0, ROT_DIM, 2, dtype=F32) / ROT_DIM)
    ang = jnp.arange(S, dtype=F32)[:, None] * inv[None, :]
    cos, sin = jnp.cos(ang), jnp.sin(ang)
    d = jnp.arange(KV_W) % HEAD_DIM
    first, second = d < ROT_HALF, (d >= ROT_HALF) & (d < ROT_DIM)
    cos_l = cos[:, d % ROT_HALF]
    sin_l = sin[:, d % ROT_HALF]
    rc = jnp.where((d < ROT_DIM)[None], cos_l, 1.0)
    rs1 = jnp.where(second[None], sin_l, 0.0)
    rs2 = jnp.where(first[None], -sin_l, 0.0)
    return rc, rs1, rs2, cos.T, sin.T


def _route_plan(route, counts, T):
    bm = FFN_BM
    A = T * TOP_K
    tok_major = lambda r: jnp.swapaxes(r, 1, 2).reshape(T, TOP_K)
    flat_e = tok_major(route[:, 0:TOP_K]).astype(I32).reshape(A)
    flat_g = tok_major(route[:, TOP_K:2 * TOP_K]).reshape(A)
    rank = tok_major(route[:, 2 * TOP_K:3 * TOP_K]).astype(I32)
    order = jnp.argsort(flat_e, stable=True).astype(I32)
    counts = counts.astype(I32)
    starts = jnp.cumsum(counts) - counts
    padded = (counts + bm - 1) // bm * bm
    pends = jnp.cumsum(padded)
    pstarts = pends - padded
    P = (A + N_EXPERTS * bm + bm - 1) // bm * bm
    NB = P // bm
    blk0 = jnp.arange(NB, dtype=I32) * bm
    block_e = jnp.minimum(jnp.sum((pends[None, :] <= blk0[:, None]).astype(I32), axis=1), N_EXPERTS - 1)
    r = (blk0 - pstarts[block_e])[:, None] + jnp.arange(bm, dtype=I32)[None, :]
    valid = r < counts[block_e][:, None]
    a_p = order[jnp.clip(starts[block_e][:, None] + r, 0, A - 1)]
    buf_tok = jnp.where(valid, a_p // TOP_K, 0)
    buf_gate = jnp.where(valid, flat_g[a_p], 0.0).reshape(P, 1)
    dest_assign = pstarts[flat_e].reshape(T, TOP_K) + rank
    nb_used = (pends[-1] // bm).astype(I32).reshape(1)
    return block_e, nb_used, buf_tok, buf_gate, dest_assign


def kernel(x, c, norm1_g, norm2_g, w_ada, b_ada, w_in, conv_w, conv_b, conv_ln_g, conv_ln_b, cmp_pe_k, cmp_pe_v,
           cmp_k_w1, cmp_k_w2, cmp_v_w1, cmp_v_w2, out_norm_conv, out_norm_nsa, w_out, w_router, b_router,
           w_gate, b_gate, w_up, b_up, w_down, b_down, final_norm_g):
    B, S, D = x.shape
    T = B * S
    G = KV_GROUPS
    assert D == D_MODEL and S % ATTN_TK == 0 and S % CMP_STRIDE == 0 and KV_W == LANES
    rc, rs1, rs2, cos_t, sin_t = _rope_tables(S)
    n_sel = S // SEL_BLK
    nc = S // CMP_STRIDE
    cstart = jnp.arange(nc) * CMP_STRIDE
    jstart = jnp.arange(n_sel) * SEL_BLK
    overlap_t = ((cstart[None, :] <= jstart[:, None] + SEL_BLK - 1)
                 & (cstart[None, :] + CMP_LEN - 1 >= jstart[:, None])
                 & (jnp.arange(nc)[None, :] < nc - 1)).astype(F32)

    assert w_ada.shape[0] == 1
    for l in range(1):
        mod = _adaln(c, w_ada[l], b_ada[l][None])
        sh1, sc1, gt1, sh2, sc2, gt2 = [m[:, None, :] for m in jnp.split(mod, 6, axis=-1)]

        wl = w_in[l]
        o = 2 * CONV_CH + NSA_WIDTH
        kvc = [wl[:, o + i * KV_W:o + (i + 1) * KV_W] for i in range(6)]
        gl = wl[:, o + 6 * KV_W:]
        per_g = 3 * Q_PER_G
        gpad = [jnp.pad(gl[:, per_g * g:per_g * (g + 1)], ((0, 0), (0, GATE_ROWS - per_g))) for g in range(G)]
        wn = jnp.concatenate([wl[:, :2 * CONV_CH], kvc[0], kvc[2], kvc[4], kvc[1]], axis=1).astype(BF16)
        wt = jnp.concatenate([wl[:, 2 * CONV_CH:o], kvc[3], kvc[5]] + gpad, axis=1).T.astype(BF16)
        u, kc, vc, ks, kw, qt, vst, vwt, gates = _inproj(x, sc1, sh1, norm1_g[l][None], wn, wt,
                                                         rc, rs1, rs2, cos_t, sin_t)

        conv_n = _conv(u, conv_w[l], conv_b[l][None], conv_ln_g[l][None], conv_ln_b[l][None],
                       out_norm_conv[l][None])

        chunk = CMP_STRIDE * HEAD_DIM
        kcmp, vcmp_t = _compress(kc.reshape(B, G, nc, chunk), vc.reshape(B, G, nc, chunk),
                                 cmp_pe_k[l].reshape(1, -1), cmp_pe_v[l].reshape(1, -1),
                                 cmp_k_w1[l], cmp_k_w2[l], cmp_v_w1[l], cmp_v_w2[l].T)
        nsa = _attention(qt, kcmp, vcmp_t, ks, vst, kw, vwt, gates, overlap_t)

        x1, h2_rows, route, counts = _outproj(x, conv_n, nsa, out_norm_nsa[l][None], w_out[l].astype(BF16), gt1,
                                              norm2_g[l][None], sc2, sh2, w_router[l], b_router[l][None])

        block_e, nb_used, buf_tok, buf_gate, dest_assign = _route_plan(route, counts[:, 0], T)
        y_rows = _ffn(block_e, nb_used, buf_tok, h2_rows.reshape(T, ROW_TILES, LANES), buf_gate,
                      w_gate[l], b_gate[l][:, None, :], w_up[l], b_up[l][:, None, :],
                      w_down[l], b_down[l][:, None, :])
        tm = min(COMB_TM, S)
        dest3 = dest_assign.reshape(T // tm, tm, TOP_K).transpose(0, 2, 1).reshape(T // tm, 1, TOP_K * tm)
        P = y_rows.shape[0] // ROW_TILES
        x = _combine(dest3, y_rows.reshape(P, ROW_TILES, LANES), x1.reshape(T, D), gt2, final_norm_g[None],
                     S).reshape(B, S, D)
    return x
```

```python
import functools

import jax
import jax.numpy as jnp
from jax import lax
from jax.experimental import pallas as pl
from jax.experimental.pallas import tpu as pltpu

F32 = jnp.float32
BF16 = jnp.bfloat16
I32 = jnp.int32
HI = lax.Precision.HIGHEST

D_MODEL = 1024
CONV_CH = 512
CONV_WIDTH = 31
NSA_HEADS = 8
KV_GROUPS = 2
Q_PER_G = NSA_HEADS // KV_GROUPS
HEAD_DIM = 64
NSA_WIDTH = NSA_HEADS * HEAD_DIM
KV_W = KV_GROUPS * HEAD_DIM
ROT_DIM = HEAD_DIM // 4
ROT_HALF = ROT_DIM // 2
ROPE_THETA = 500000.0
CMP_LEN = 32
CMP_STRIDE = 16
CMP_HIDDEN = 128
SEL_BLK = 64
SEL_TOPN = 16
WINDOW = 512
N_EXPERTS = 32
TOP_K = 4
D_FF = 1024
SWIGLU_ALPHA = 1.702
SWIGLU_LIMIT = 7.0
NORM_EPS = 1e-5
NEG_INF = -1e30
FORCE_SCORE = 1e9
LOG2_E = 1.4426950408889634

LANES = 128
SUBLANES = 8
ROW_TILES = D_MODEL // LANES

GATE_ROWS = 16

INPROJ_TM = 512
CONV_TR = 256
CONV_HALO = 32
ATTN_TQ = 256
ATTN_TK = 256
OUT_TM = 512
FFN_BM = 512
COMB_TM = 256
ROUTE_W = 16
ISSUE_UNROLL = 8


def _rms(x, g):
    return x * lax.rsqrt(jnp.mean(x * x, axis=-1, keepdims=True) + NORM_EPS) * g


def _ada_kernel(c_ref, w_ref, b_ref, o_ref):
    c = c_ref[...]
    ca = c * jax.nn.sigmoid(c)
    o_ref[...] = jnp.dot(ca, w_ref[...], preferred_element_type=F32, precision=HI) + b_ref[...]


def _adaln(c, w, b):
    B = c.shape[0]
    D = D_MODEL
    return pl.pallas_call(
        _ada_kernel,
        grid=(6,),
        in_specs=[pl.BlockSpec((B, D), lambda j: (0, 0)),
                  pl.BlockSpec((D, D), lambda j: (0, j)),
                  pl.BlockSpec((1, D), lambda j: (0, j))],
        out_specs=pl.BlockSpec((B, D), lambda j: (0, j)),
        out_shape=jax.ShapeDtypeStruct((B, 6 * D), F32),
        name="adaln",
    )(c, w, b)


def _inproj_kernel(x_ref, sc_ref, sh_ref, g_ref, wn_ref, wt_ref, rc_ref, rs1_ref, rs2_ref, ct_ref, st_ref,
                   u_ref, kc_ref, vc_ref, ks_ref, kw_ref, qt_ref, vst_ref, vwt_ref, gt_ref):
    tm = x_ref.shape[1]
    tq, tk = ATTN_TQ, ATTN_TK
    h = (_rms(x_ref[0], g_ref[...]) * (1.0 + sc_ref[0]) + sh_ref[0]).astype(BF16)

    p = jnp.dot(h, wn_ref[...], preferred_element_type=F32)
    u_ref[0] = p[:, 0:CONV_CH] * jax.nn.sigmoid(p[:, CONV_CH:2 * CONV_CH])
    c0 = 2 * CONV_CH
    rc, rs1, rs2 = rc_ref[...], rs1_ref[...], rs2_ref[...]
    for ref, roped in ((kc_ref, True), (ks_ref, True), (kw_ref, True), (vc_ref, False)):
        v = p[:, c0:c0 + KV_W]
        if roped:
            v = v * rc + pltpu.roll(v, ROT_HALF, 1) * rs1 + pltpu.roll(v, KV_W - ROT_HALF, 1) * rs2
        for gg in range(KV_GROUPS):
            ref[0, gg] = v[:, HEAD_DIM * gg:HEAD_DIM * (gg + 1)].astype(ref.dtype)
        c0 += KV_W

    pt = lax.dot_general(wt_ref[...], h, (((1,), (1,)), ((), ())), preferred_element_type=F32)
    cos_t, sin_t = ct_ref[...], st_ref[...]
    scale = HEAD_DIM ** -0.5 * LOG2_E
    for hh in range(NSA_HEADS):
        blk = pt[HEAD_DIM * hh:HEAD_DIM * (hh + 1), :]
        x1, x2 = blk[0:ROT_HALF], blk[ROT_HALF:ROT_DIM]
        qh = (jnp.concatenate([x1 * cos_t - x2 * sin_t, x2 * cos_t + x1 * sin_t, blk[ROT_DIM:]], axis=0)
              * scale).astype(BF16)
        gg, n = divmod(hh, Q_PER_G)
        for j in range(tm // tq):
            qt_ref[0, gg, j, :, n * tq:(n + 1) * tq] = qh[:, j * tq:(j + 1) * tq]
    r0 = NSA_WIDTH
    for ref in (vst_ref, vwt_ref):
        for gg in range(KV_GROUPS):
            blk = pt[r0 + HEAD_DIM * gg:r0 + HEAD_DIM * (gg + 1), :].astype(BF16)
            for j in range(tm // tk):
                ref[0, gg, j] = blk[:, j * tk:(j + 1) * tk]
        r0 += KV_W
    for gg in range(KV_GROUPS):
        gt_ref[0, gg] = jax.nn.sigmoid(pt[r0 + GATE_ROWS * gg:r0 + GATE_ROWS * (gg + 1), :])


def _inproj(x, sc, sh, g, wn, wt, rc, rs1, rs2, cos_t, sin_t):
    B, S, D = x.shape
    tm = min(INPROJ_TM, S)
    tq, tk = ATTN_TQ, ATTN_TK
    G = KV_GROUPS
    kv = lambda dt: jax.ShapeDtypeStruct((B, G, S, HEAD_DIM), dt)
    kv_spec = pl.BlockSpec((1, G, tm, HEAD_DIM), lambda b, i: (b, 0, i, 0))
    vt_shape = jax.ShapeDtypeStruct((B, G, S // tk, HEAD_DIM, tk), BF16)
    vt_spec = pl.BlockSpec((1, G, tm // tk, HEAD_DIM, tk), lambda b, i: (b, 0, i, 0, 0))
    row = pl.BlockSpec((1, 1, D), lambda b, i: (b, 0, 0))
    tab = pl.BlockSpec((tm, LANES), lambda b, i: (i, 0))
    tab_t = pl.BlockSpec((ROT_HALF, tm), lambda b, i: (0, i))
    return pl.pallas_call(
        _inproj_kernel,
        grid=(B, S // tm),
        in_specs=[pl.BlockSpec((1, tm, D), lambda b, i: (b, i, 0)), row, row,
                  pl.BlockSpec((1, D), lambda b, i: (0, 0)),
                  pl.BlockSpec(wn.shape, lambda b, i: (0, 0)),
                  pl.BlockSpec(wt.shape, lambda b, i: (0, 0)),
                  tab, tab, tab, tab_t, tab_t],
        out_specs=[pl.BlockSpec((1, tm, CONV_CH), lambda b, i: (b, i, 0)),
                   kv_spec, kv_spec, kv_spec, kv_spec,
                   pl.BlockSpec((1, G, tm // tq, HEAD_DIM, Q_PER_G * tq), lambda b, i: (b, 0, i, 0, 0)),
                   vt_spec, vt_spec,
                   pl.BlockSpec((1, G, GATE_ROWS, tm), lambda b, i: (b, 0, 0, i))],
        out_shape=[jax.ShapeDtypeStruct((B, S, CONV_CH), F32),
                   kv(F32), kv(F32), kv(BF16), kv(BF16),
                   jax.ShapeDtypeStruct((B, G, S // tq, HEAD_DIM, Q_PER_G * tq), BF16),
                   vt_shape, vt_shape,
                   jax.ShapeDtypeStruct((B, G, GATE_ROWS, S), F32)],
        compiler_params=pltpu.CompilerParams(dimension_semantics=("parallel", "parallel")),
        name="inproj",
    )(x, sc, sh, g, wn, wt, rc, rs1, rs2, cos_t, sin_t)


def _conv_kernel(prev_ref, cur_ref, w_ref, cb_ref, lg_ref, lb_ref, on_ref, o_ref, pad_ref, win_ref):
    tr = cur_ref.shape[1]
    first = pl.program_id(1) == 0
    halo = prev_ref[0, tr - CONV_HALO:tr, :]
    pad_ref[0:CONV_HALO, :] = jnp.where(first, 0.0, halo)
    pad_ref[CONV_HALO:CONV_HALO + tr, :] = cur_ref[0]
    off = CONV_HALO - (CONV_WIDTH - 1)
    acc = jnp.zeros((tr, CONV_CH), F32)
    for b in range(SUBLANES):
        taps = range(b, CONV_WIDTH, SUBLANES)
        rows = tr + SUBLANES * (len(taps) - 1)
        win_ref[b, 0:rows, :] = pad_ref[off + b:off + b + rows, :]
        for a, k in enumerate(taps):
            acc = acc + win_ref[b, SUBLANES * a:SUBLANES * a + tr, :] * w_ref[k:k + 1, :]
    y = acc + cb_ref[...]
    mu = jnp.mean(y, axis=-1, keepdims=True)
    yc = y - mu
    var = jnp.mean(yc * yc, axis=-1, keepdims=True)
    yn = yc * lax.rsqrt(var + NORM_EPS) * lg_ref[...] + lb_ref[...]
    s = yn * jax.nn.sigmoid(yn)
    o_ref[0] = _rms(s, on_ref[...]).astype(o_ref.dtype)


def _conv(u, w, cb, lg, lb, on):
    B, S, C = u.shape
    tr = min(CONV_TR, S)
    vec = pl.BlockSpec((1, C), lambda b, i: (0, 0))
    return pl.pallas_call(
        _conv_kernel,
        grid=(B, S // tr),
        in_specs=[pl.BlockSpec((1, tr, C), lambda b, i: (b, jnp.maximum(i - 1, 0), 0)),
                  pl.BlockSpec((1, tr, C), lambda b, i: (b, i, 0)),
                  pl.BlockSpec((CONV_WIDTH, C), lambda b, i: (0, 0)),
                  vec, vec, vec, vec],
        out_specs=pl.BlockSpec((1, tr, C), lambda b, i: (b, i, 0)),
        out_shape=jax.ShapeDtypeStruct((B, S, C), BF16),
        scratch_shapes=[pltpu.VMEM((CONV_HALO + tr, C), F32),
                        pltpu.VMEM((SUBLANES, tr + SUBLANES * ((CONV_WIDTH - 1) // SUBLANES), C), F32)],
        compiler_params=pltpu.CompilerParams(dimension_semantics=("parallel", "parallel")),
        name="conv",
    )(u, u, w, cb, lg, lb, on)


def _cmp_kernel(kx_ref, vx_ref, pek_ref, pev_ref, kw1_ref, kw2_ref, vw1_ref, vw2t_ref, ko_ref, vo_ref):
    nc = kx_ref.shape[2]
    half = kx_ref.shape[3]
    nt = (((1,), (1,)), ((), ()))
    for x_ref, pe_ref, w1_ref, w2_ref, o_ref, transposed in ((kx_ref, pek_ref, kw1_ref, kw2_ref, ko_ref, False),
                                                             (vx_ref, pev_ref, vw1_ref, vw2t_ref, vo_ref, True)):
        w1 = w1_ref[...]
        pe = jnp.broadcast_to(pe_ref[...], (SUBLANES, 2 * half))
        pe_proj = jnp.dot(pe, w1, preferred_element_type=F32, precision=HI)[0:1]
        for gg in range(KV_GROUPS):
            xg = x_ref[0, gg]
            first = jnp.dot(xg, w1[0:half], preferred_element_type=F32, precision=HI)
            second = jnp.dot(xg, w1[half:2 * half], preferred_element_type=F32, precision=HI)
            hid = first + pltpu.roll(second, nc - 1, 0) + pe_proj
            hid = hid * jax.nn.sigmoid(hid)
            if transposed:
                o_ref[0, gg] = lax.dot_general(w2_ref[...], hid, nt, preferred_element_type=F32, precision=HI)
            else:
                o_ref[0, gg] = jnp.dot(hid, w2_ref[...], preferred_element_type=F32, precision=HI)


def _compress(kx, vx, pek, pev, kw1, kw2, vw1, vw2t):
    B, G, NC, W = kx.shape
    xs = pl.BlockSpec((1, G, NC, W), lambda b: (b, 0, 0, 0))
    full = lambda a: pl.BlockSpec(a.shape, lambda b: (0,) * a.ndim)
    return pl.pallas_call(
        _cmp_kernel,
        grid=(B,),
        in_specs=[xs, xs, full(pek), full(pev), full(kw1), full(kw2), full(vw1), full(vw2t)],
        out_specs=[pl.BlockSpec((1, G, NC, HEAD_DIM), lambda b: (b, 0, 0, 0)),
                   pl.BlockSpec((1, G, HEAD_DIM, NC), lambda b: (b, 0, 0, 0))],
        out_shape=[jax.ShapeDtypeStruct((B, G, NC, HEAD_DIM), F32),
                   jax.ShapeDtypeStruct((B, G, HEAD_DIM, NC), F32)],
        compiler_params=pltpu.CompilerParams(dimension_semantics=("parallel",)),
        name="compress",
    )(kx, vx, pek, pev, kw1, kw2, vw1, vw2t)


def _attn_kernel(qt_ref, kc_ref, vct_ref, ks_ref, vst_ref, kw_ref, vwt_ref, gt_ref, ovt_ref, o_ref, sel_ref):
    tq, tk = ATTN_TQ, ATTN_TK
    R = Q_PER_G * tq
    per_tile = tk // SEL_BLK
    S = ks_ref.shape[2]
    ncp = kc_ref.shape[2]
    nsel = ovt_ref.shape[0]
    qi = pl.program_id(2)
    q0 = qi * tq
    qt = qt_ref[0, 0, 0]
    t_row = q0 + lax.broadcasted_iota(I32, (1, tq), 1)
    heads = lambda a: jnp.concatenate([a] * Q_PER_G, axis=1)

    sc = jnp.dot(kc_ref[0, 0].astype(BF16), qt, preferred_element_type=F32)
    c_io = lax.broadcasted_iota(I32, (ncp, tq), 0)
    m_c = (c_io * CMP_STRIDE + (CMP_LEN - 1) <= t_row) & (c_io < ncp - 1)
    scb = sc + heads(jnp.where(m_c, 0.0, NEG_INF))
    e = jnp.exp2(scb - jnp.max(scb, axis=0, keepdims=True)) * heads(jnp.where(m_c, 1.0, 0.0))
    den = jnp.sum(e, axis=0, keepdims=True)
    pc = e / jnp.where(den > 0.0, den, 1.0)
    o_cmp = jnp.dot(vct_ref[0, 0].astype(BF16), pc.astype(BF16), preferred_element_type=F32)

    psum = pc[:, 0:tq]
    for n in range(1, Q_PER_G):
        psum = psum + pc[:, n * tq:(n + 1) * tq]
    imp = jnp.dot(ovt_ref[...], psum, preferred_element_type=F32, precision=HI)
    j_io = lax.broadcasted_iota(I32, (nsel, tq), 0)
    cur = t_row // SEL_BLK
    valid = j_io * SEL_BLK <= t_row
    forced = (j_io == 0) | (j_io == cur) | (j_io == cur - 1)
    score = jnp.where(valid, jnp.where(forced, FORCE_SCORE, imp), NEG_INF)
    rank = jnp.zeros((nsel, tq), F32)
    for i in range(nsel):
        row = score[i:i + 1, :]
        tie = jnp.where(j_io > i, 1.0, 0.0)
        rank = rank + jnp.where(row > score, 1.0, jnp.where(row == score, tie, 0.0))
    sel_bias = jnp.where(rank < float(min(SEL_TOPN, nsel)), 0.0, NEG_INF)
    sel_ref[...] = jnp.zeros(sel_ref.shape, F32)
    for jj in range(S // tk):
        sel_ref[jj, 0:per_tile, :] = sel_bias[jj * per_tile:(jj + 1) * per_tile, :]

    k_io = lax.broadcasted_iota(I32, (tk, tq), 0)

    def flash_step(k_ref, vt_ref, kj, bias, carry):
        m, l, acc = carry
        k0 = pl.multiple_of(kj * tk, tk)
        s = jnp.dot(k_ref[0, 0, pl.ds(k0, tk), :], qt, preferred_element_type=F32) + heads(bias)
        m_new = jnp.maximum(m, jnp.max(s, axis=0, keepdims=True))
        alpha = jnp.exp2(m - m_new)
        p = jnp.exp2(s - m_new)
        l = alpha * l + jnp.sum(p, axis=0, keepdims=True)
        acc = alpha * acc + jnp.dot(vt_ref[0, 0, kj], p.astype(BF16), preferred_element_type=F32)
        return m_new, l, acc

    init = (jnp.full((1, R), NEG_INF, F32), jnp.zeros((1, R), F32), jnp.zeros((HEAD_DIM, R), F32))

    def slc_body(kj, carry):
        blocks = sel_ref[kj]
        bias = jnp.concatenate([jnp.broadcast_to(blocks[b:b + 1, :], (SEL_BLK, tq)) for b in range(per_tile)], axis=0)
        bias = jnp.where(kj * tk + k_io <= t_row, bias, NEG_INF)
        return flash_step(ks_ref, vst_ref, kj, bias, carry)

    def win_body(kj, carry):
        rel = t_row - (kj * tk + k_io)
        bias = jnp.where((rel >= 0) & (rel < WINDOW), 0.0, NEG_INF)
        return flash_step(kw_ref, vwt_ref, kj, bias, carry)

    n_slc = (q0 + tq + tk - 1) // tk
    lo_tile = jnp.maximum(q0 - (WINDOW - 1), 0) // tk
    n_pair = lo_tile // 2

    def pair_body(i, carry):
        return slc_body(2 * i, carry[0]), slc_body(2 * i + 1, carry[1])

    st_a, st_b = lax.fori_loop(0, n_pair, pair_body, (init, init))
    st_a = lax.fori_loop(2 * n_pair, lo_tile, slc_body, st_a)

    def both_body(kj, carry):
        return slc_body(kj, carry[0]), win_body(kj, carry[1])

    st_a, (_, l_w, acc_w) = lax.fori_loop(lo_tile, n_slc, both_body, (st_a, init))
    m_s = jnp.maximum(st_a[0], st_b[0])
    w_a, w_b = jnp.exp2(st_a[0] - m_s), jnp.exp2(st_b[0] - m_s)
    l_s = w_a * st_a[1] + w_b * st_b[1]
    acc_s = w_a * st_a[2] + w_b * st_b[2]


    gt = gt_ref[0, 0]
    o_slc = acc_s / l_s
    o_win = acc_w / l_w
    outs = []
    for n in range(Q_PER_G):
        cols = slice(n * tq, (n + 1) * tq)
        outs.append(gt[3 * n:3 * n + 1, :] * o_cmp[:, cols] + gt[3 * n + 1:3 * n + 2, :] * o_slc[:, cols]
                    + gt[3 * n + 2:3 * n + 3, :] * o_win[:, cols])
    o_ref[0] = jnp.concatenate(outs, axis=0).T


def _attention(qt, kc, vct, ks, vst, kw, vwt, gt, overlap_t):
    B, G, NQT, _, R = qt.shape
    S = ks.shape[2]
    tq, tk = ATTN_TQ, ATTN_TK
    ncp = kc.shape[2]
    kfull = pl.BlockSpec((1, 1, S, HEAD_DIM), lambda b, g, i: (b, g, 0, 0))
    vfull = pl.BlockSpec((1, 1, S // tk, HEAD_DIM, tk), lambda b, g, i: (b, g, 0, 0, 0))
    return pl.pallas_call(
        _attn_kernel,
        grid=(B, G, NQT),
        in_specs=[pl.BlockSpec((1, 1, 1, HEAD_DIM, R), lambda b, g, i: (b, g, i, 0, 0)),
                  pl.BlockSpec((1, 1, ncp, HEAD_DIM), lambda b, g, i: (b, g, 0, 0)),
                  pl.BlockSpec((1, 1, HEAD_DIM, ncp), lambda b, g, i: (b, g, 0, 0)),
                  kfull, vfull, kfull, vfull,
                  pl.BlockSpec((1, 1, GATE_ROWS, tq), lambda b, g, i: (b, g, 0, i)),
                  pl.BlockSpec(overlap_t.shape, lambda b, g, i: (0, 0))],
        out_specs=pl.BlockSpec((1, tq, Q_PER_G * HEAD_DIM), lambda b, g, i: (b, i, g)),
        out_shape=jax.ShapeDtypeStruct((B, S, NSA_WIDTH), F32),
        scratch_shapes=[pltpu.VMEM((S // tk, SUBLANES, tq), F32)],
        compiler_params=pltpu.CompilerParams(dimension_semantics=("parallel", "parallel", "arbitrary")),
        name="attn",
    )(qt, kc, vct, ks, vst, kw, vwt, gt, overlap_t)


def _outproj_kernel(x_ref, cv_ref, nsa_ref, on_ref, w_ref, gt1_ref, g2_ref, sc2_ref, sh2_ref, wrh_ref, wrl_ref, br_ref,
                    tri_ref, x1_ref, h2_ref, rt_ref, cnt_ref, run_ref):
    tm = x_ref.shape[1]

    @pl.when((pl.program_id(0) == 0) & (pl.program_id(1) == 0))
    def _():
        run_ref[...] = jnp.zeros(run_ref.shape, F32)

    nn = _rms(nsa_ref[0], on_ref[...]).astype(BF16)
    y = (jnp.dot(cv_ref[0], w_ref[0:CONV_CH, :], preferred_element_type=F32)
         + jnp.dot(nn, w_ref[CONV_CH:CONV_CH + NSA_WIDTH, :], preferred_element_type=F32))
    x1 = x_ref[0] + gt1_ref[0] * y
    x1_ref[0] = x1
    h2 = _rms(x1, g2_ref[...]) * (1.0 + sc2_ref[0]) + sh2_ref[0]
    for s in range(ROW_TILES):
        h2_ref[pl.ds(s, tm, stride=ROW_TILES), :] = h2[:, s * LANES:(s + 1) * LANES]
    nt = (((1,), (1,)), ((), ()))
    h_hi = h2.astype(BF16)
    h_lo = (h2 - h_hi.astype(F32)).astype(BF16)
    logits = (lax.dot_general(wrh_ref[...], h_hi, nt, preferred_element_type=F32)
              + lax.dot_general(wrh_ref[...], h_lo, nt, preferred_element_type=F32)
              + lax.dot_general(wrl_ref[...], h_hi, nt, preferred_element_type=F32)) + br_ref[...]
    eio = lax.broadcasted_iota(I32, (N_EXPERTS, tm), 0).astype(F32)
    vals, idxs = [], []
    for _ in range(TOP_K):
        m = jnp.max(logits, axis=0, keepdims=True)
        ix = jnp.min(jnp.where(logits == m, eio, float(N_EXPERTS)), axis=0, keepdims=True)
        vals.append(m)
        idxs.append(ix)
        logits = jnp.where(eio == ix, -jnp.inf, logits)
    es = [jnp.exp(v - vals[0]) for v in vals]
    den = es[0] + es[1] + es[2] + es[3]
    hot = jnp.zeros((N_EXPERTS, tm), F32)
    for r in range(TOP_K):
        hot = hot + jnp.where(eio == idxs[r], 1.0, 0.0)
    before = run_ref[...] + jnp.dot(hot.astype(BF16), tri_ref[...], preferred_element_type=F32)
    ranks = [jnp.sum(jnp.where(eio == idxs[r], before, 0.0), axis=0, keepdims=True) for r in range(TOP_K)]
    run_ref[...] = run_ref[...] + jnp.sum(hot, axis=1, keepdims=True)
    cnt_ref[...] = run_ref[...]
    rio = lax.broadcasted_iota(I32, (ROUTE_W, tm), 0)
    out = jnp.zeros((ROUTE_W, tm), F32)
    for r in range(TOP_K):
        out = jnp.where(rio == r, idxs[r], out)
        out = jnp.where(rio == TOP_K + r, es[r] / den, out)
        out = jnp.where(rio == 2 * TOP_K + r, ranks[r], out)
    rt_ref[0] = out


def _outproj(x, cv, nsa, on, w, gt1, g2, sc2, sh2, wr, br):
    B, S, D = x.shape
    tm = min(OUT_TM, S)
    nt = S // tm
    tri = (jnp.arange(tm)[:, None] < jnp.arange(tm)[None, :]).astype(BF16)
    wr_t = wr.T
    wr_hi = wr_t.astype(BF16)
    wr_lo = (wr_t - wr_hi.astype(F32)).astype(BF16)
    row = pl.BlockSpec((1, 1, D), lambda b, i: (b, 0, 0))
    vec = lambda n: pl.BlockSpec((1, n), lambda b, i: (0, 0))
    col = pl.BlockSpec((N_EXPERTS, 1), lambda b, i: (0, 0))
    wr_spec = pl.BlockSpec((N_EXPERTS, D), lambda b, i: (0, 0))
    return pl.pallas_call(
        _outproj_kernel,
        grid=(B, nt),
        in_specs=[pl.BlockSpec((1, tm, D), lambda b, i: (b, i, 0)),
                  pl.BlockSpec((1, tm, CONV_CH), lambda b, i: (b, i, 0)),
                  pl.BlockSpec((1, tm, NSA_WIDTH), lambda b, i: (b, i, 0)),
                  vec(NSA_WIDTH),
                  pl.BlockSpec((D, D), lambda b, i: (0, 0)),
                  row, vec(D), row, row,
                  wr_spec, wr_spec, col,
                  pl.BlockSpec((tm, tm), lambda b, i: (0, 0))],
        out_specs=[pl.BlockSpec((1, tm, D), lambda b, i: (b, i, 0)),
                   pl.BlockSpec((tm * ROW_TILES, LANES), lambda b, i: (b * nt + i, 0)),
                   pl.BlockSpec((1, ROUTE_W, tm), lambda b, i: (b, 0, i)),
                   col],
        out_shape=[jax.ShapeDtypeStruct((B, S, D), F32),
                   jax.ShapeDtypeStruct((B * S * ROW_TILES, LANES), F32),
                   jax.ShapeDtypeStruct((B, ROUTE_W, S), F32),
                   jax.ShapeDtypeStruct((N_EXPERTS, 1), F32)],
        scratch_shapes=[pltpu.VMEM((N_EXPERTS, 1), F32)],
        compiler_params=pltpu.CompilerParams(dimension_semantics=("arbitrary", "arbitrary")),
        name="outproj",
    )(x, cv, nsa, on, w, gt1, g2, sc2, sh2, wr_hi, wr_lo, br.reshape(N_EXPERTS, 1), tri)


def _issue_rows(idx_ref, n, src_hbm, dst, slot, sem):
    assert n % ISSUE_UNROLL == 0

    def body(c, carry):
        for u in range(ISSUE_UNROLL):
            r = c * ISSUE_UNROLL + u
            pltpu.make_async_copy(src_hbm.at[idx_ref[0, 0, r]],
                                  dst.at[slot, pl.ds(pl.multiple_of(r * ROW_TILES, ROW_TILES), ROW_TILES), :],
                                  sem.at[slot]).start(priority=u % 2)
        return carry
    lax.fori_loop(0, n // ISSUE_UNROLL, body, 0)


def _wait_rows(dst, slot, sem):
    pltpu.make_async_copy(dst.at[slot], dst.at[slot], sem.at[slot]).wait()


def _rows_2d(buf, slot, base, n):
    return jnp.concatenate(
        [buf[slot, pl.ds(base * ROW_TILES + s, n, stride=ROW_TILES), :] for s in range(ROW_TILES)], axis=1)


def _ffn_kernel(be_ref, nb_ref, ta_ref, tb_ref, h2_hbm, gate_ref, wg_ref, bg_ref, wu_ref, bu_ref, wd_ref, bd_ref,
                o_ref, xbuf, wgb, wub, wdb, sem):
    bm = ta_ref.shape[2]
    i = pl.program_id(0)
    nb = nb_ref[0]
    slot = i % 2

    @pl.when((i == 0) | (be_ref[i] != be_ref[jnp.maximum(i - 1, 0)]))
    def _():
        wgb[...] = wg_ref[0].astype(BF16)
        wub[...] = wu_ref[0].astype(BF16)
        wdb[...] = wd_ref[0].astype(BF16)

    @pl.when(i == 0)
    def _():
        _issue_rows(ta_ref, bm, h2_hbm, xbuf, 0, sem)

    @pl.when(i + 1 < nb)
    def _():
        _issue_rows(tb_ref, bm, h2_hbm, xbuf, 1 - slot, sem)

    @pl.when(i < nb)
    def _():
        _wait_rows(xbuf, slot, sem)
        x = _rows_2d(xbuf, slot, 0, bm).astype(BF16)
        g = jnp.dot(x, wgb[...], preferred_element_type=F32) + bg_ref[0]
        u = jnp.dot(x, wub[...], preferred_element_type=F32) + bu_ref[0]
        g = jnp.minimum(g, SWIGLU_LIMIT)
        u = jnp.clip(u, -SWIGLU_LIMIT, SWIGLU_LIMIT)
        act = g * jax.nn.sigmoid(SWIGLU_ALPHA * g) * (u + 1.0)
        y = (jnp.dot(act.astype(BF16), wdb[...], preferred_element_type=F32) + bd_ref[0]) * gate_ref[...]
        for s in range(ROW_TILES):
            o_ref[pl.ds(s, bm, stride=ROW_TILES), :] = y[:, s * LANES:(s + 1) * LANES]

    @pl.when(i >= nb)
    def _():
        o_ref[...] = jnp.zeros(o_ref.shape, o_ref.dtype)


def _ffn(block_e, nb_used, buf_tok, h2_rows, buf_gate, wg, bg, wu, bu, wd, bd):
    NB = block_e.shape[0]
    bm = FFN_BM
    D, F = D_MODEL, D_FF
    tok3 = buf_tok.reshape(NB, 1, bm)
    wspec = lambda r, c: pl.BlockSpec((1, r, c), lambda i, be, nb: (be[i], 0, 0))
    vmem_limit = 2 * 3 * D * F * 4 + 3 * D * F * 2 + 4 * bm * D * 4 + 6 * bm * F * 4
    return pl.pallas_call(
        _ffn_kernel,
        grid_spec=pltpu.PrefetchScalarGridSpec(
            num_scalar_prefetch=2,
            grid=(NB,),
            in_specs=[pl.BlockSpec((1, 1, bm), lambda i, be, nb: (i, 0, 0), memory_space=pltpu.SMEM),
                      pl.BlockSpec((1, 1, bm), lambda i, be, nb: (jnp.minimum(i + 1, NB - 1), 0, 0),
                                   memory_space=pltpu.SMEM),
                      pl.BlockSpec(memory_space=pl.ANY),
                      pl.BlockSpec((bm, 1), lambda i, be, nb: (i, 0)),
                      wspec(D, F), wspec(1, F), wspec(D, F), wspec(1, F), wspec(F, D), wspec(1, D)],
            out_specs=pl.BlockSpec((bm * ROW_TILES, LANES), lambda i, be, nb: (i, 0)),
            scratch_shapes=[pltpu.VMEM((2, bm * ROW_TILES, LANES), F32),
                            pltpu.VMEM((D, F), BF16), pltpu.VMEM((D, F), BF16), pltpu.VMEM((F, D), BF16),
                            pltpu.SemaphoreType.DMA((2,))]),
        out_shape=jax.ShapeDtypeStruct((NB * bm * ROW_TILES, LANES), F32),
        compiler_params=pltpu.CompilerParams(dimension_semantics=("arbitrary",),
                                             vmem_limit_bytes=vmem_limit),
        name="ffn",
    )(block_e, nb_used, tok3, tok3, h2_rows, buf_gate, wg, bg, wu, bu, wd, bd)


def _combine_kernel(da_ref, db_ref, y_hbm, x1_ref, gt2_ref, fg_ref, o_ref, buf, sem):
    tm = x1_ref.shape[0]
    n = TOP_K * tm
    i = pl.program_id(0)
    slot = i % 2

    @pl.when(i == 0)
    def _():
        _issue_rows(da_ref, n, y_hbm, buf, 0, sem)

    @pl.when(i + 1 < pl.num_programs(0))
    def _():
        _issue_rows(db_ref, n, y_hbm, buf, 1 - slot, sem)

    _wait_rows(buf, slot, sem)
    y = _rows_2d(buf, slot, 0, tm)
    for k in range(1, TOP_K):
        y = y + _rows_2d(buf, slot, k * tm, tm)
    x2 = x1_ref[...] + gt2_ref[0] * y
    o_ref[...] = _rms(x2, fg_ref[...])


def _combine(dest3, y_rows, x1, gt2, fg, S):
    T, D = x1.shape
    tm = min(COMB_TM, S)
    NT = T // tm
    per_b = S // tm
    n = TOP_K * tm
    return pl.pallas_call(
        _combine_kernel,
        grid=(NT,),
        in_specs=[pl.BlockSpec((1, 1, n), lambda i: (i, 0, 0), memory_space=pltpu.SMEM),
                  pl.BlockSpec((1, 1, n), lambda i: (jnp.minimum(i + 1, NT - 1), 0, 0), memory_space=pltpu.SMEM),
                  pl.BlockSpec(memory_space=pl.ANY),
                  pl.BlockSpec((tm, D), lambda i: (i, 0)),
                  pl.BlockSpec((1, 1, D), lambda i: (i // per_b, 0, 0)),
                  pl.BlockSpec((1, D), lambda i: (0, 0))],
        out_specs=pl.BlockSpec((tm, D), lambda i: (i, 0)),
        out_shape=jax.ShapeDtypeStruct((T, D), F32),
        scratch_shapes=[pltpu.VMEM((2, n * ROW_TILES, LANES), F32), pltpu.SemaphoreType.DMA((2,))],
        compiler_params=pltpu.CompilerParams(dimension_semantics=("arbitrary",)),
        name="combine",
    )(dest3, dest3, y_rows, x1, gt2, fg)


def _rope_tables(S):
    inv = ROPE_THETA ** (-jnp.arange(0, ROT_DIM, 2, dtype=F32) / ROT_DIM)
    ang = jnp.arange(S, dtype=F32)[:, None] * inv[None, :]
    cos, sin = jnp.cos(ang), jnp.sin(ang)
    d = jnp.arange(KV_W) % HEAD_DIM
    first, second = d < ROT_HALF, (d >= ROT_HALF) & (d < ROT_DIM)
    cos_l = cos[:, d % ROT_HALF]
    sin_l = sin[:, d % ROT_HALF]
    rc = jnp.where((d < ROT_DIM)[None], cos_l, 1.0)
    rs1 = jnp.where(second[None], sin_l, 0.0)
    rs2 = jnp.where(first[None], -sin_l, 0.0)
    return rc, rs1, rs2, cos.T, sin.T


def _route_plan(route, counts, T):
    bm = FFN_BM
    A = T * TOP_K
    k_major = lambda r: jnp.swapaxes(r, 0, 1).reshape(TOP_K, T)
    idx = k_major(route[:, 0:TOP_K]).astype(I32)
    gate = k_major(route[:, TOP_K:2 * TOP_K])
    rank = k_major(route[:, 2 * TOP_K:3 * TOP_K]).astype(I32)
    keys = idx * T + jnp.arange(T, dtype=I32)[None, :]
    skey, sgate = lax.sort((keys.reshape(A), gate.reshape(A)), num_keys=1)
    counts = counts.astype(I32)
    starts = jnp.cumsum(counts) - counts
    padded = (counts + bm - 1) // bm * bm
    pends = jnp.cumsum(padded)
    pstarts = pends - padded
    P = (A + N_EXPERTS * bm + bm - 1) // bm * bm
    NB = P // bm
    blk0 = jnp.arange(NB, dtype=I32) * bm
    block_e = jnp.minimum(jnp.sum((pends[None, :] <= blk0[:, None]).astype(I32), axis=1), N_EXPERTS - 1)
    r = (blk0 - pstarts[block_e])[:, None] + jnp.arange(bm, dtype=I32)[None, :]
    valid = r < counts[block_e][:, None]
    src = jnp.clip(starts[block_e][:, None] + r, 0, A - 1)
    buf_tok = jnp.where(valid, skey[src] - block_e[:, None] * T, 0)
    buf_gate = jnp.where(valid, sgate[src], 0.0).reshape(P, 1)
    dest = pstarts[idx] + rank
    nb_used = (pends[-1] // bm).astype(I32).reshape(1)
    return block_e, nb_used, buf_tok, buf_gate, dest


def kernel(x, c, norm1_g, norm2_g, w_ada, b_ada, w_in, conv_w, conv_b, conv_ln_g, conv_ln_b, cmp_pe_k, cmp_pe_v,
           cmp_k_w1, cmp_k_w2, cmp_v_w1, cmp_v_w2, out_norm_conv, out_norm_nsa, w_out, w_router, b_router,
           w_gate, b_gate, w_up, b_up, w_down, b_down, final_norm_g):
    B, S, D = x.shape
    T = B * S
    G = KV_GROUPS
    assert D == D_MODEL and S % ATTN_TK == 0 and S % CMP_STRIDE == 0 and KV_W == LANES
    rc, rs1, rs2, cos_t, sin_t = _rope_tables(S)
    n_sel = S // SEL_BLK
    nc = S // CMP_STRIDE
    cstart = jnp.arange(nc) * CMP_STRIDE
    jstart = jnp.arange(n_sel) * SEL_BLK
    overlap_t = ((cstart[None, :] <= jstart[:, None] + SEL_BLK - 1)
                 & (cstart[None, :] + CMP_LEN - 1 >= jstart[:, None])
                 & (jnp.arange(nc)[None, :] < nc - 1)).astype(F32)

    assert w_ada.shape[0] == 1
    for l in range(1):
        mod = _adaln(c, w_ada[l], b_ada[l][None])
        sh1, sc1, gt1, sh2, sc2, gt2 = [m[:, None, :] for m in jnp.split(mod, 6, axis=-1)]

        wl = w_in[l]
        o = 2 * CONV_CH + NSA_WIDTH
        kvc = [wl[:, o + i * KV_W:o + (i + 1) * KV_W] for i in range(6)]
        gl = wl[:, o + 6 * KV_W:]
        per_g = 3 * Q_PER_G
        gpad = [jnp.pad(gl[:, per_g * g:per_g * (g + 1)], ((0, 0), (0, GATE_ROWS - per_g))) for g in range(G)]
        wn = jnp.concatenate([wl[:, :2 * CONV_CH], kvc[0], kvc[2], kvc[4], kvc[1]], axis=1).astype(BF16)
        wt = jnp.concatenate([wl[:, 2 * CONV_CH:o], kvc[3], kvc[5]] + gpad, axis=1).T.astype(BF16)
        u, kc, vc, ks, kw, qt, vst, vwt, gates = _inproj(x, sc1, sh1, norm1_g[l][None], wn, wt,
                                                         rc, rs1, rs2, cos_t, sin_t)

        conv_n = _conv(u, conv_w[l], conv_b[l][None], conv_ln_g[l][None], conv_ln_b[l][None],
                       out_norm_conv[l][None])

        chunk = CMP_STRIDE * HEAD_DIM
        kcmp, vcmp_t = _compress(kc.reshape(B, G, nc, chunk), vc.reshape(B, G, nc, chunk),
                                 cmp_pe_k[l].reshape(1, -1), cmp_pe_v[l].reshape(1, -1),
                                 cmp_k_w1[l], cmp_k_w2[l], cmp_v_w1[l], cmp_v_w2[l].T)
        nsa = _attention(qt, kcmp, vcmp_t, ks, vst, kw, vwt, gates, overlap_t)

        x1, h2_rows, route, counts = _outproj(x, conv_n, nsa, out_norm_nsa[l][None], w_out[l].astype(BF16), gt1,
                                              norm2_g[l][None], sc2, sh2, w_router[l], b_router[l][None])

        block_e, nb_used, buf_tok, buf_gate, dest = _route_plan(route, counts[:, 0], T)
        y_rows = _ffn(block_e, nb_used, buf_tok, h2_rows.reshape(T, ROW_TILES, LANES), buf_gate,
                      w_gate[l], b_gate[l][:, None, :], w_up[l], b_up[l][:, None, :],
                      w_down[l], b_down[l][:, None, :])
        tm = min(COMB_TM, S)
        dest3 = dest.reshape(TOP_K, T // tm, tm).transpose(1, 0, 2).reshape(T // tm, 1, TOP_K * tm)
        P = y_rows.shape[0] // ROW_TILES
        x = _combine(dest3, y_rows.reshape(P, ROW_TILES, LANES), x1.reshape(T, D), gt2, final_norm_g[None],
                     S).reshape(B, S, D)
    return x
```

```python
import functools

import jax
import jax.numpy as jnp
from jax import lax
from jax.experimental import pallas as pl
from jax.experimental.pallas import tpu as pltpu

F32 = jnp.float32
BF16 = jnp.bfloat16
I32 = jnp.int32
HI = lax.Precision.HIGHEST

D_MODEL = 1024
CONV_CH = 512
CONV_WIDTH = 31
NSA_HEADS = 8
KV_GROUPS = 2
Q_PER_G = NSA_HEADS // KV_GROUPS
HEAD_DIM = 64
NSA_WIDTH = NSA_HEADS * HEAD_DIM
KV_W = KV_GROUPS * HEAD_DIM
ROT_DIM = HEAD_DIM // 4
ROT_HALF = ROT_DIM // 2
ROPE_THETA = 500000.0
CMP_LEN = 32
CMP_STRIDE = 16
CMP_HIDDEN = 128
SEL_BLK = 64
SEL_TOPN = 16
WINDOW = 512
N_EXPERTS = 32
TOP_K = 4
D_FF = 1024
SWIGLU_ALPHA = 1.702
SWIGLU_LIMIT = 7.0
NORM_EPS = 1e-5
NEG_INF = -1e30
FORCE_SCORE = 1e9
LOG2_E = 1.4426950408889634

LANES = 128
SUBLANES = 8
ROW_TILES = D_MODEL // LANES

GATE_ROWS = 16

INPROJ_TM = 512
CONV_TR = 256
CONV_HALO = 32
ATTN_TQ = 256
ATTN_TK = 256
OUT_TM = 512
FFN_BM = 512
COMB_TM = 256
ROUTE_W = 16
ISSUE_UNROLL = 8


def _rms(x, g):
    return x * lax.rsqrt(jnp.mean(x * x, axis=-1, keepdims=True) + NORM_EPS) * g


def _ada_kernel(c_ref, w_ref, b_ref, o_ref):
    c = c_ref[...]
    ca = c * jax.nn.sigmoid(c)
    o_ref[...] = jnp.dot(ca, w_ref[...], preferred_element_type=F32, precision=HI) + b_ref[...]


def _adaln(c, w, b):
    B = c.shape[0]
    D = D_MODEL
    return pl.pallas_call(
        _ada_kernel,
        grid=(6,),
        in_specs=[pl.BlockSpec((B, D), lambda j: (0, 0)),
                  pl.BlockSpec((D, D), lambda j: (0, j)),
                  pl.BlockSpec((1, D), lambda j: (0, j))],
        out_specs=pl.BlockSpec((B, D), lambda j: (0, j)),
        out_shape=jax.ShapeDtypeStruct((B, 6 * D), F32),
        name="adaln",
    )(c, w, b)


def _inproj_kernel(x_ref, sc_ref, sh_ref, g_ref, wn_ref, wt_ref, rc_ref, rs1_ref, rs2_ref, ct_ref, st_ref,
                   u_ref, kc_ref, vc_ref, ks_ref, kw_ref, qt_ref, vst_ref, vwt_ref, gt_ref):
    tm = x_ref.shape[1]
    tq, tk = ATTN_TQ, ATTN_TK
    h = (_rms(x_ref[0], g_ref[...]) * (1.0 + sc_ref[0]) + sh_ref[0]).astype(BF16)

    p = jnp.dot(h, wn_ref[...], preferred_element_type=F32)
    u_ref[0] = p[:, 0:CONV_CH] * jax.nn.sigmoid(p[:, CONV_CH:2 * CONV_CH])
    c0 = 2 * CONV_CH
    rc, rs1, rs2 = rc_ref[...], rs1_ref[...], rs2_ref[...]
    for ref, roped in ((kc_ref, True), (ks_ref, True), (kw_ref, True), (vc_ref, False)):
        v = p[:, c0:c0 + KV_W]
        if roped:
            v = v * rc + pltpu.roll(v, ROT_HALF, 1) * rs1 + pltpu.roll(v, KV_W - ROT_HALF, 1) * rs2
        for gg in range(KV_GROUPS):
            ref[0, gg] = v[:, HEAD_DIM * gg:HEAD_DIM * (gg + 1)].astype(ref.dtype)
        c0 += KV_W

    pt = lax.dot_general(wt_ref[...], h, (((1,), (1,)), ((), ())), preferred_element_type=F32)
    cos_t, sin_t = ct_ref[...], st_ref[...]
    scale = HEAD_DIM ** -0.5 * LOG2_E
    for hh in range(NSA_HEADS):
        blk = pt[HEAD_DIM * hh:HEAD_DIM * (hh + 1), :]
        x1, x2 = blk[0:ROT_HALF], blk[ROT_HALF:ROT_DIM]
        qh = (jnp.concatenate([x1 * cos_t - x2 * sin_t, x2 * cos_t + x1 * sin_t, blk[ROT_DIM:]], axis=0)
              * scale).astype(BF16)
        gg, n = divmod(hh, Q_PER_G)
        for j in range(tm // tq):
            qt_ref[0, gg, j, :, n * tq:(n + 1) * tq] = qh[:, j * tq:(j + 1) * tq]
    r0 = NSA_WIDTH
    for ref in (vst_ref, vwt_ref):
        for gg in range(KV_GROUPS):
            blk = pt[r0 + HEAD_DIM * gg:r0 + HEAD_DIM * (gg + 1), :].astype(BF16)
            for j in range(tm // tk):
                ref[0, gg, j] = blk[:, j * tk:(j + 1) * tk]
        r0 += KV_W
    for gg in range(KV_GROUPS):
        gt_ref[0, gg] = jax.nn.sigmoid(pt[r0 + GATE_ROWS * gg:r0 + GATE_ROWS * (gg + 1), :])


def _inproj(x, sc, sh, g, wn, wt, rc, rs1, rs2, cos_t, sin_t):
    B, S, D = x.shape
    tm = min(INPROJ_TM, S)
    tq, tk = ATTN_TQ, ATTN_TK
    G = KV_GROUPS
    kv = lambda dt: jax.ShapeDtypeStruct((B, G, S, HEAD_DIM), dt)
    kv_spec = pl.BlockSpec((1, G, tm, HEAD_DIM), lambda b, i: (b, 0, i, 0))
    vt_shape = jax.ShapeDtypeStruct((B, G, S // tk, HEAD_DIM, tk), BF16)
    vt_spec = pl.BlockSpec((1, G, tm // tk, HEAD_DIM, tk), lambda b, i: (b, 0, i, 0, 0))
    row = pl.BlockSpec((1, 1, D), lambda b, i: (b, 0, 0))
    tab = pl.BlockSpec((tm, LANES), lambda b, i: (i, 0))
    tab_t = pl.BlockSpec((ROT_HALF, tm), lambda b, i: (0, i))
    return pl.pallas_call(
        _inproj_kernel,
        grid=(B, S // tm),
        in_specs=[pl.BlockSpec((1, tm, D), lambda b, i: (b, i, 0)), row, row,
                  pl.BlockSpec((1, D), lambda b, i: (0, 0)),
                  pl.BlockSpec(wn.shape, lambda b, i: (0, 0)),
                  pl.BlockSpec(wt.shape, lambda b, i: (0, 0)),
                  tab, tab, tab, tab_t, tab_t],
        out_specs=[pl.BlockSpec((1, tm, CONV_CH), lambda b, i: (b, i, 0)),
                   kv_spec, kv_spec, kv_spec, kv_spec,
                   pl.BlockSpec((1, G, tm // tq, HEAD_DIM, Q_PER_G * tq), lambda b, i: (b, 0, i, 0, 0)),
                   vt_spec, vt_spec,
                   pl.BlockSpec((1, G, GATE_ROWS, tm), lambda b, i: (b, 0, 0, i))],
        out_shape=[jax.ShapeDtypeStruct((B, S, CONV_CH), F32),
                   kv(F32), kv(F32), kv(BF16), kv(BF16),
                   jax.ShapeDtypeStruct((B, G, S // tq, HEAD_DIM, Q_PER_G * tq), BF16),
                   vt_shape, vt_shape,
                   jax.ShapeDtypeStruct((B, G, GATE_ROWS, S), F32)],
        compiler_params=pltpu.CompilerParams(dimension_semantics=("parallel", "parallel")),
        name="inproj",
    )(x, sc, sh, g, wn, wt, rc, rs1, rs2, cos_t, sin_t)


def _conv_kernel(prev_ref, cur_ref, w_ref, cb_ref, lg_ref, lb_ref, on_ref, o_ref, pad_ref, win_ref):
    tr = cur_ref.shape[1]
    first = pl.program_id(1) == 0
    halo = prev_ref[0, tr - CONV_HALO:tr, :]
    pad_ref[0:CONV_HALO, :] = jnp.where(first, 0.0, halo)
    pad_ref[CONV_HALO:CONV_HALO + tr, :] = cur_ref[0]
    off = CONV_HALO - (CONV_WIDTH - 1)
    acc = jnp.zeros((tr, CONV_CH), F32)
    for b in range(SUBLANES):
        taps = range(b, CONV_WIDTH, SUBLANES)
        rows = tr + SUBLANES * (len(taps) - 1)
        win_ref[b, 0:rows, :] = pad_ref[off + b:off + b + rows, :]
        for a, k in enumerate(taps):
            acc = acc + win_ref[b, SUBLANES * a:SUBLANES * a + tr, :] * w_ref[k:k + 1, :]
    y = acc + cb_ref[...]
    mu = jnp.mean(y, axis=-1, keepdims=True)
    yc = y - mu
    var = jnp.mean(yc * yc, axis=-1, keepdims=True)
    yn = yc * lax.rsqrt(var + NORM_EPS) * lg_ref[...] + lb_ref[...]
    s = yn * jax.nn.sigmoid(yn)
    o_ref[0] = _rms(s, on_ref[...]).astype(o_ref.dtype)


def _conv(u, w, cb, lg, lb, on):
    B, S, C = u.shape
    tr = min(CONV_TR, S)
    vec = pl.BlockSpec((1, C), lambda b, i: (0, 0))
    return pl.pallas_call(
        _conv_kernel,
        grid=(B, S // tr),
        in_specs=[pl.BlockSpec((1, tr, C), lambda b, i: (b, jnp.maximum(i - 1, 0), 0)),
                  pl.BlockSpec((1, tr, C), lambda b, i: (b, i, 0)),
                  pl.BlockSpec((CONV_WIDTH, C), lambda b, i: (0, 0)),
                  vec, vec, vec, vec],
        out_specs=pl.BlockSpec((1, tr, C), lambda b, i: (b, i, 0)),
        out_shape=jax.ShapeDtypeStruct((B, S, C), BF16),
        scratch_shapes=[pltpu.VMEM((CONV_HALO + tr, C), F32),
                        pltpu.VMEM((SUBLANES, tr + SUBLANES * ((CONV_WIDTH - 1) // SUBLANES), C), F32)],
        compiler_params=pltpu.CompilerParams(dimension_semantics=("parallel", "parallel")),
        name="conv",
    )(u, u, w, cb, lg, lb, on)


def _cmp_kernel(kx_ref, vx_ref, pek_ref, pev_ref, kw1_ref, kw2_ref, vw1_ref, vw2t_ref, ko_ref, vo_ref):
    nc = kx_ref.shape[2]
    half = kx_ref.shape[3]
    nt = (((1,), (1,)), ((), ()))
    for x_ref, pe_ref, w1_ref, w2_ref, o_ref, transposed in ((kx_ref, pek_ref, kw1_ref, kw2_ref, ko_ref, False),
                                                             (vx_ref, pev_ref, vw1_ref, vw2t_ref, vo_ref, True)):
        w1 = w1_ref[...]
        pe = jnp.broadcast_to(pe_ref[...], (SUBLANES, 2 * half))
        pe_proj = jnp.dot(pe, w1, preferred_element_type=F32, precision=HI)[0:1]
        for gg in range(KV_GROUPS):
            xg = x_ref[0, gg]
            first = jnp.dot(xg, w1[0:half], preferred_element_type=F32, precision=HI)
            second = jnp.dot(xg, w1[half:2 * half], preferred_element_type=F32, precision=HI)
            hid = first + pltpu.roll(second, nc - 1, 0) + pe_proj
            hid = hid * jax.nn.sigmoid(hid)
            if transposed:
                o_ref[0, gg] = lax.dot_general(w2_ref[...], hid, nt, preferred_element_type=F32, precision=HI)
            else:
                o_ref[0, gg] = jnp.dot(hid, w2_ref[...], preferred_element_type=F32, precision=HI)


def _compress(kx, vx, pek, pev, kw1, kw2, vw1, vw2t):
    B, G, NC, W = kx.shape
    xs = pl.BlockSpec((1, G, NC, W), lambda b: (b, 0, 0, 0))
    full = lambda a: pl.BlockSpec(a.shape, lambda b: (0,) * a.ndim)
    return pl.pallas_call(
        _cmp_kernel,
        grid=(B,),
        in_specs=[xs, xs, full(pek), full(pev), full(kw1), full(kw2), full(vw1), full(vw2t)],
        out_specs=[pl.BlockSpec((1, G, NC, HEAD_DIM), lambda b: (b, 0, 0, 0)),
                   pl.BlockSpec((1, G, HEAD_DIM, NC), lambda b: (b, 0, 0, 0))],
        out_shape=[jax.ShapeDtypeStruct((B, G, NC, HEAD_DIM), F32),
                   jax.ShapeDtypeStruct((B, G, HEAD_DIM, NC), F32)],
        compiler_params=pltpu.CompilerParams(dimension_semantics=("parallel",)),
        name="compress",
    )(kx, vx, pek, pev, kw1, kw2, vw1, vw2t)


def _attn_kernel(qt_ref, kc_ref, vct_ref, ks_ref, vst_ref, kw_ref, vwt_ref, gt_ref, ovt_ref, o_ref, sel_ref):
    tq, tk = ATTN_TQ, ATTN_TK
    R = Q_PER_G * tq
    per_tile = tk // SEL_BLK
    S = ks_ref.shape[2]
    ncp = kc_ref.shape[2]
    nsel = ovt_ref.shape[0]
    qi = pl.program_id(2)
    q0 = qi * tq
    qt = qt_ref[0, 0, 0]
    t_row = q0 + lax.broadcasted_iota(I32, (1, tq), 1)
    heads = lambda a: jnp.concatenate([a] * Q_PER_G, axis=1)

    sc = jnp.dot(kc_ref[0, 0].astype(BF16), qt, preferred_element_type=F32)
    c_io = lax.broadcasted_iota(I32, (ncp, tq), 0)
    m_c = (c_io * CMP_STRIDE + (CMP_LEN - 1) <= t_row) & (c_io < ncp - 1)
    scb = sc + heads(jnp.where(m_c, 0.0, NEG_INF))
    e = jnp.exp2(scb - jnp.max(scb, axis=0, keepdims=True)) * heads(jnp.where(m_c, 1.0, 0.0))
    den = jnp.sum(e, axis=0, keepdims=True)
    pc = e / jnp.where(den > 0.0, den, 1.0)
    o_cmp = jnp.dot(vct_ref[0, 0].astype(BF16), pc.astype(BF16), preferred_element_type=F32)

    psum = pc[:, 0:tq]
    for n in range(1, Q_PER_G):
        psum = psum + pc[:, n * tq:(n + 1) * tq]
    imp = jnp.dot(ovt_ref[...], psum, preferred_element_type=F32, precision=HI)
    j_io = lax.broadcasted_iota(I32, (nsel, tq), 0)
    cur = t_row // SEL_BLK
    valid = j_io * SEL_BLK <= t_row
    forced = (j_io == 0) | (j_io == cur) | (j_io == cur - 1)
    score = jnp.where(valid, jnp.where(forced, FORCE_SCORE, imp), NEG_INF)
    rank = jnp.zeros((nsel, tq), F32)
    for i in range(nsel):
        row = score[i:i + 1, :]
        tie = jnp.where(j_io > i, 1.0, 0.0)
        rank = rank + jnp.where(row > score, 1.0, jnp.where(row == score, tie, 0.0))
    sel_bias = jnp.where(rank < float(min(SEL_TOPN, nsel)), 0.0, NEG_INF)
    sel_ref[...] = jnp.zeros(sel_ref.shape, F32)
    for jj in range(S // tk):
        sel_ref[jj, 0:per_tile, :] = sel_bias[jj * per_tile:(jj + 1) * per_tile, :]

    k_io = lax.broadcasted_iota(I32, (tk, tq), 0)

    def flash_step(k_ref, vt_ref, kj, bias, carry):
        m, l, acc = carry
        k0 = pl.multiple_of(kj * tk, tk)
        s = jnp.dot(k_ref[0, 0, pl.ds(k0, tk), :], qt, preferred_element_type=F32) + heads(bias)
        m_new = jnp.maximum(m, jnp.max(s, axis=0, keepdims=True))
        alpha = jnp.exp2(m - m_new)
        p = jnp.exp2(s - m_new)
        l = alpha * l + jnp.sum(p, axis=0, keepdims=True)
        acc = alpha * acc + jnp.dot(vt_ref[0, 0, kj], p.astype(BF16), preferred_element_type=F32)
        return m_new, l, acc

    init = (jnp.full((1, R), NEG_INF, F32), jnp.zeros((1, R), F32), jnp.zeros((HEAD_DIM, R), F32))

    def slc_body(kj, carry):
        blocks = sel_ref[kj]
        bias = jnp.concatenate([jnp.broadcast_to(blocks[b:b + 1, :], (SEL_BLK, tq)) for b in range(per_tile)], axis=0)
        bias = jnp.where(kj * tk + k_io <= t_row, bias, NEG_INF)
        return flash_step(ks_ref, vst_ref, kj, bias, carry)

    def win_body(kj, carry):
        rel = t_row - (kj * tk + k_io)
        bias = jnp.where((rel >= 0) & (rel < WINDOW), 0.0, NEG_INF)
        return flash_step(kw_ref, vwt_ref, kj, bias, carry)

    n_slc = (q0 + tq + tk - 1) // tk
    lo_tile = jnp.maximum(q0 - (WINDOW - 1), 0) // tk
    n_pair = lo_tile // 2

    def pair_body(i, carry):
        return slc_body(2 * i, carry[0]), slc_body(2 * i + 1, carry[1])

    st_a, st_b = lax.fori_loop(0, n_pair, pair_body, (init, init))
    st_a = lax.fori_loop(2 * n_pair, lo_tile, slc_body, st_a)

    def both_body(kj, carry):
        return slc_body(kj, carry[0]), win_body(kj, carry[1])

    st_a, (_, l_w, acc_w) = lax.fori_loop(lo_tile, n_slc, both_body, (st_a, init))
    m_s = jnp.maximum(st_a[0], st_b[0])
    w_a, w_b = jnp.exp2(st_a[0] - m_s), jnp.exp2(st_b[0] - m_s)
    l_s = w_a * st_a[1] + w_b * st_b[1]
    acc_s = w_a * st_a[2] + w_b * st_b[2]


    gt = gt_ref[0, 0]
    o_slc = acc_s / l_s
    o_win = acc_w / l_w
    outs = []
    for n in range(Q_PER_G):
        cols = slice(n * tq, (n + 1) * tq)
        outs.append(gt[3 * n:3 * n + 1, :] * o_cmp[:, cols] + gt[3 * n + 1:3 * n + 2, :] * o_slc[:, cols]
                    + gt[3 * n + 2:3 * n + 3, :] * o_win[:, cols])
    o_ref[0] = jnp.concatenate(outs, axis=0).T


def _attention(qt, kc, vct, ks, vst, kw, vwt, gt, overlap_t):
    B, G, NQT, _, R = qt.shape
    S = ks.shape[2]
    tq, tk = ATTN_TQ, ATTN_TK
    ncp = kc.shape[2]
    kfull = pl.BlockSpec((1, 1, S, HEAD_DIM), lambda b, g, i: (b, g, 0, 0))
    vfull = pl.BlockSpec((1, 1, S // tk, HEAD_DIM, tk), lambda b, g, i: (b, g, 0, 0, 0))
    return pl.pallas_call(
        _attn_kernel,
        grid=(B, G, NQT),
        in_specs=[pl.BlockSpec((1, 1, 1, HEAD_DIM, R), lambda b, g, i: (b, g, i, 0, 0)),
                  pl.BlockSpec((1, 1, ncp, HEAD_DIM), lambda b, g, i: (b, g, 0, 0)),
                  pl.BlockSpec((1, 1, HEAD_DIM, ncp), lambda b, g, i: (b, g, 0, 0)),
                  kfull, vfull, kfull, vfull,
                  pl.BlockSpec((1, 1, GATE_ROWS, tq), lambda b, g, i: (b, g, 0, i)),
                  pl.BlockSpec(overlap_t.shape, lambda b, g, i: (0, 0))],
        out_specs=pl.BlockSpec((1, tq, Q_PER_G * HEAD_DIM), lambda b, g, i: (b, i, g)),
        out_shape=jax.ShapeDtypeStruct((B, S, NSA_WIDTH), F32),
        scratch_shapes=[pltpu.VMEM((S // tk, SUBLANES, tq), F32)],
        compiler_params=pltpu.CompilerParams(dimension_semantics=("parallel", "parallel", "arbitrary")),
        name="attn",
    )(qt, kc, vct, ks, vst, kw, vwt, gt, overlap_t)


def _outproj_kernel(x_ref, cv_ref, nsa_ref, on_ref, w_ref, gt1_ref, g2_ref, sc2_ref, sh2_ref, wrh_ref, wrl_ref, br_ref,
                    tri_ref, x1_ref, h2_ref, rt_ref, cnt_ref, run_ref):
    tm = x_ref.shape[1]

    @pl.when((pl.program_id(0) == 0) & (pl.program_id(1) == 0))
    def _():
        run_ref[...] = jnp.zeros(run_ref.shape, F32)

    nn = _rms(nsa_ref[0], on_ref[...]).astype(BF16)
    y = (jnp.dot(cv_ref[0], w_ref[0:CONV_CH, :], preferred_element_type=F32)
         + jnp.dot(nn, w_ref[CONV_CH:CONV_CH + NSA_WIDTH, :], preferred_element_type=F32))
    x1 = x_ref[0] + gt1_ref[0] * y
    x1_ref[0] = x1
    h2 = _rms(x1, g2_ref[...]) * (1.0 + sc2_ref[0]) + sh2_ref[0]
    for s in range(ROW_TILES):
        h2_ref[pl.ds(s, tm, stride=ROW_TILES), :] = h2[:, s * LANES:(s + 1) * LANES]
    nt = (((1,), (1,)), ((), ()))
    h_hi = h2.astype(BF16)
    h_lo = (h2 - h_hi.astype(F32)).astype(BF16)
    logits = (lax.dot_general(wrh_ref[...], h_hi, nt, preferred_element_type=F32)
              + lax.dot_general(wrh_ref[...], h_lo, nt, preferred_element_type=F32)
              + lax.dot_general(wrl_ref[...], h_hi, nt, preferred_element_type=F32)) + br_ref[...]
    eio = lax.broadcasted_iota(I32, (N_EXPERTS, tm), 0).astype(F32)
    vals, idxs = [], []
    for _ in range(TOP_K):
        m = jnp.max(logits, axis=0, keepdims=True)
        ix = jnp.min(jnp.where(logits == m, eio, float(N_EXPERTS)), axis=0, keepdims=True)
        vals.append(m)
        idxs.append(ix)
        logits = jnp.where(eio == ix, -jnp.inf, logits)
    es = [jnp.exp(v - vals[0]) for v in vals]
    den = es[0] + es[1] + es[2] + es[3]
    hot = jnp.zeros((N_EXPERTS, tm), F32)
    for r in range(TOP_K):
        hot = hot + jnp.where(eio == idxs[r], 1.0, 0.0)
    before = run_ref[...] + jnp.dot(hot.astype(BF16), tri_ref[...], preferred_element_type=F32)
    ranks = [jnp.sum(jnp.where(eio == idxs[r], before, 0.0), axis=0, keepdims=True) for r in range(TOP_K)]
    run_ref[...] = run_ref[...] + jnp.sum(hot, axis=1, keepdims=True)
    cnt_ref[...] = run_ref[...]
    rio = lax.broadcasted_iota(I32, (ROUTE_W, tm), 0)
    out = jnp.zeros((ROUTE_W, tm), F32)
    for r in range(TOP_K):
        out = jnp.where(rio == r, idxs[r], out)
        out = jnp.where(rio == TOP_K + r, es[r] / den, out)
        out = jnp.where(rio == 2 * TOP_K + r, ranks[r], out)
    rt_ref[0] = out


def _outproj(x, cv, nsa, on, w, gt1, g2, sc2, sh2, wr, br):
    B, S, D = x.shape
    tm = min(OUT_TM, S)
    nt = S // tm
    tri = (jnp.arange(tm)[:, None] < jnp.arange(tm)[None, :]).astype(BF16)
    wr_t = wr.T
    wr_hi = wr_t.astype(BF16)
    wr_lo = (wr_t - wr_hi.astype(F32)).astype(BF16)
    row = pl.BlockSpec((1, 1, D), lambda b, i: (b, 0, 0))
    vec = lambda n: pl.BlockSpec((1, n), lambda b, i: (0, 0))
    col = pl.BlockSpec((N_EXPERTS, 1), lambda b, i: (0, 0))
    wr_spec = pl.BlockSpec((N_EXPERTS, D), lambda b, i: (0, 0))
    return pl.pallas_call(
        _outproj_kernel,
        grid=(B, nt),
        in_specs=[pl.BlockSpec((1, tm, D), lambda b, i: (b, i, 0)),
                  pl.BlockSpec((1, tm, CONV_CH), lambda b, i: (b, i, 0)),
                  pl.BlockSpec((1, tm, NSA_WIDTH), lambda b, i: (b, i, 0)),
                  vec(NSA_WIDTH),
                  pl.BlockSpec((D, D), lambda b, i: (0, 0)),
                  row, vec(D), row, row,
                  wr_spec, wr_spec, col,
                  pl.BlockSpec((tm, tm), lambda b, i: (0, 0))],
        out_specs=[pl.BlockSpec((1, tm, D), lambda b, i: (b, i, 0)),
                   pl.BlockSpec((tm * ROW_TILES, LANES), lambda b, i: (b * nt + i, 0)),
                   pl.BlockSpec((1, ROUTE_W, tm), lambda b, i: (b, 0, i)),
                   col],
        out_shape=[jax.ShapeDtypeStruct((B, S, D), F32),
                   jax.ShapeDtypeStruct((B * S * ROW_TILES, LANES), F32),
                   jax.ShapeDtypeStruct((B, ROUTE_W, S), F32),
                   jax.ShapeDtypeStruct((N_EXPERTS, 1), F32)],
        scratch_shapes=[pltpu.VMEM((N_EXPERTS, 1), F32)],
        compiler_params=pltpu.CompilerParams(dimension_semantics=("arbitrary", "arbitrary")),
        name="outproj",
    )(x, cv, nsa, on, w, gt1, g2, sc2, sh2, wr_hi, wr_lo, br.reshape(N_EXPERTS, 1), tri)


def _issue_rows(idx_ref, n, src_hbm, dst, slot, sem):
    assert n % ISSUE_UNROLL == 0

    def body(c, carry):
        for u in range(ISSUE_UNROLL):
            r = c * ISSUE_UNROLL + u
            pltpu.make_async_copy(src_hbm.at[idx_ref[0, 0, r]],
                                  dst.at[slot, pl.ds(pl.multiple_of(r * ROW_TILES, ROW_TILES), ROW_TILES), :],
                                  sem.at[slot]).start(priority=u % 2)
        return carry
    lax.fori_loop(0, n // ISSUE_UNROLL, body, 0)


def _wait_rows(dst, slot, sem):
    pltpu.make_async_copy(dst.at[slot], dst.at[slot], sem.at[slot]).wait()


def _rows_2d(buf, slot, base, n):
    return jnp.concatenate(
        [buf[slot, pl.ds(base * ROW_TILES + s, n, stride=ROW_TILES), :] for s in range(ROW_TILES)], axis=1)


def _ffn_kernel(be_ref, nb_ref, ta_ref, tb_ref, h2_hbm, gate_ref, wg_ref, bg_ref, wu_ref, bu_ref, wd_ref, bd_ref,
                o_ref, xbuf, wgb, wub, wdb, sem):
    bm = ta_ref.shape[2]
    i = pl.program_id(0)
    nb = nb_ref[0]
    slot = i % 2

    @pl.when((i == 0) | (be_ref[i] != be_ref[jnp.maximum(i - 1, 0)]))
    def _():
        wgb[...] = wg_ref[0].astype(BF16)
        wub[...] = wu_ref[0].astype(BF16)
        wdb[...] = wd_ref[0].astype(BF16)

    @pl.when(i == 0)
    def _():
        _issue_rows(ta_ref, bm, h2_hbm, xbuf, 0, sem)

    @pl.when(i + 1 < nb)
    def _():
        _issue_rows(tb_ref, bm, h2_hbm, xbuf, 1 - slot, sem)

    @pl.when(i < nb)
    def _():
        _wait_rows(xbuf, slot, sem)
        x = _rows_2d(xbuf, slot, 0, bm).astype(BF16)
        g = jnp.dot(x, wgb[...], preferred_element_type=F32) + bg_ref[0]
        u = jnp.dot(x, wub[...], preferred_element_type=F32) + bu_ref[0]
        g = jnp.minimum(g, SWIGLU_LIMIT)
        u = jnp.clip(u, -SWIGLU_LIMIT, SWIGLU_LIMIT)
        act = g * jax.nn.sigmoid(SWIGLU_ALPHA * g) * (u + 1.0)
        y = (jnp.dot(act.astype(BF16), wdb[...], preferred_element_type=F32) + bd_ref[0]) * gate_ref[...]
        for s in range(ROW_TILES):
            o_ref[pl.ds(s, bm, stride=ROW_TILES), :] = y[:, s * LANES:(s + 1) * LANES]

    @pl.when(i >= nb)
    def _():
        o_ref[...] = jnp.zeros(o_ref.shape, o_ref.dtype)


def _ffn(block_e, nb_used, buf_tok, h2_rows, buf_gate, wg, bg, wu, bu, wd, bd):
    NB = block_e.shape[0]
    bm = FFN_BM
    D, F = D_MODEL, D_FF
    tok3 = buf_tok.reshape(NB, 1, bm)
    wspec = lambda r, c: pl.BlockSpec((1, r, c), lambda i, be, nb: (be[i], 0, 0))
    vmem_limit = 2 * 3 * D * F * 4 + 3 * D * F * 2 + 4 * bm * D * 4 + 6 * bm * F * 4
    return pl.pallas_call(
        _ffn_kernel,
        grid_spec=pltpu.PrefetchScalarGridSpec(
            num_scalar_prefetch=2,
            grid=(NB,),
            in_specs=[pl.BlockSpec((1, 1, bm), lambda i, be, nb: (i, 0, 0), memory_space=pltpu.SMEM),
                      pl.BlockSpec((1, 1, bm), lambda i, be, nb: (jnp.minimum(i + 1, NB - 1), 0, 0),
                                   memory_space=pltpu.SMEM),
                      pl.BlockSpec(memory_space=pl.ANY),
                      pl.BlockSpec((bm, 1), lambda i, be, nb: (i, 0)),
                      wspec(D, F), wspec(1, F), wspec(D, F), wspec(1, F), wspec(F, D), wspec(1, D)],
            out_specs=pl.BlockSpec((bm * ROW_TILES, LANES), lambda i, be, nb: (i, 0)),
            scratch_shapes=[pltpu.VMEM((2, bm * ROW_TILES, LANES), F32),
                            pltpu.VMEM((D, F), BF16), pltpu.VMEM((D, F), BF16), pltpu.VMEM((F, D), BF16),
                            pltpu.SemaphoreType.DMA((2,))]),
        out_shape=jax.ShapeDtypeStruct((NB * bm * ROW_TILES, LANES), F32),
        compiler_params=pltpu.CompilerParams(dimension_semantics=("arbitrary",),
                                             vmem_limit_bytes=vmem_limit),
        name="ffn",
    )(block_e, nb_used, tok3, tok3, h2_rows, buf_gate, wg, bg, wu, bu, wd, bd)


def _combine_kernel(da_ref, db_ref, y_hbm, x1_ref, gt2_ref, fg_ref, o_ref, buf, sem):
    tm = x1_ref.shape[0]
    n = TOP_K * tm
    i = pl.program_id(0)
    slot = i % 2

    @pl.when(i == 0)
    def _():
        _issue_rows(da_ref, n, y_hbm, buf, 0, sem)

    @pl.when(i + 1 < pl.num_programs(0))
    def _():
        _issue_rows(db_ref, n, y_hbm, buf, 1 - slot, sem)

    _wait_rows(buf, slot, sem)
    y = _rows_2d(buf, slot, 0, tm)
    for k in range(1, TOP_K):
        y = y + _rows_2d(buf, slot, k * tm, tm)
    x2 = x1_ref[...] + gt2_ref[0] * y
    o_ref[...] = _rms(x2, fg_ref[...])


def _combine(dest3, y_rows, x1, gt2, fg, S):
    T, D = x1.shape
    tm = min(COMB_TM, S)
    NT = T // tm
    per_b = S // tm
    n = TOP_K * tm
    return pl.pallas_call(
        _combine_kernel,
        grid=(NT,),
        in_specs=[pl.BlockSpec((1, 1, n), lambda i: (i, 0, 0), memory_space=pltpu.SMEM),
                  pl.BlockSpec((1, 1, n), lambda i: (jnp.minimum(i + 1, NT - 1), 0, 0), memory_space=pltpu.SMEM),
                  pl.BlockSpec(memory_space=pl.ANY),
                  pl.BlockSpec((tm, D), lambda i: (i, 0)),
                  pl.BlockSpec((1, 1, D), lambda i: (i // per_b, 0, 0)),
                  pl.BlockSpec((1, D), lambda i: (0, 0))],
        out_specs=pl.BlockSpec((tm, D), lambda i: (i, 0)),
        out_shape=jax.ShapeDtypeStruct((T, D), F32),
        scratch_shapes=[pltpu.VMEM((2, n * ROW_TILES, LANES), F32), pltpu.SemaphoreType.DMA((2,))],
        compiler_params=pltpu.CompilerParams(dimension_semantics=("arbitrary",)),
        name="combine",
    )(dest3, dest3, y_rows, x1, gt2, fg)


def _rope_tables(S):
    inv = ROPE_THETA ** (-jnp.arange(0, ROT_DIM, 2, dtype=F32) / ROT_DIM)
    ang = jnp.arange(S, dtype=F32)[:, None] * inv[None, :]
    cos, sin = jnp.cos(ang), jnp.sin(ang)
    d = jnp.arange(KV_W) % HEAD_DIM
    first, second = d < ROT_HALF, (d >= ROT_HALF) & (d < ROT_DIM)
    cos_l = cos[:, d % ROT_HALF]
    sin_l = sin[:, d % ROT_HALF]
    rc = jnp.where((d < ROT_DIM)[None], cos_l, 1.0)
    rs1 = jnp.where(second[None], sin_l, 0.0)
    rs2 = jnp.where(first[None], -sin_l, 0.0)
    return rc, rs1, rs2, cos.T, sin.T


def _route_plan(route, counts, T):
    bm = FFN_BM
    A = T * TOP_K
    k_major = lambda r: jnp.swapaxes(r, 0, 1).reshape(TOP_K, T)
    idx = k_major(route[:, 0:TOP_K]).astype(I32)
    gate = k_major(route[:, TOP_K:2 * TOP_K])
    rank = k_major(route[:, 2 * TOP_K:3 * TOP_K]).astype(I32)
    keys = idx * T + jnp.arange(T, dtype=I32)[None, :]
    skey, sgate = lax.sort((keys.reshape(A), gate.reshape(A)), num_keys=1)
    counts = counts.astype(I32)
    starts = jnp.cumsum(counts) - counts
    padded = (counts + bm - 1) // bm * bm
    pends = jnp.cumsum(padded)
    pstarts = pends - padded
    P = (A + N_EXPERTS * bm + bm - 1) // bm * bm
    NB = P // bm
    blk0 = jnp.arange(NB, dtype=I32) * bm
    block_e = jnp.minimum(jnp.sum((pends[None, :] <= blk0[:, None]).astype(I32), axis=1), N_EXPERTS - 1)
    r = (blk0 - pstarts[block_e])[:, None] + jnp.arange(bm, dtype=I32)[None, :]
    valid = r < counts[block_e][:, None]
    src = jnp.clip(starts[block_e][:, None] + r, 0, A - 1)
    buf_tok = jnp.where(valid, skey[src] - block_e[:, None] * T, 0)
    buf_gate = jnp.where(valid, sgate[src], 0.0).reshape(P, 1)
    dest = rank
    for e in range(N_EXPERTS):
        dest = dest + jnp.where(idx == e, pstarts[e], 0)
    nb_used = (pends[-1] // bm).astype(I32).reshape(1)
    return block_e, nb_used, buf_tok, buf_gate, dest


def kernel(x, c, norm1_g, norm2_g, w_ada, b_ada, w_in, conv_w, conv_b, conv_ln_g, conv_ln_b, cmp_pe_k, cmp_pe_v,
           cmp_k_w1, cmp_k_w2, cmp_v_w1, cmp_v_w2, out_norm_conv, out_norm_nsa, w_out, w_router, b_router,
           w_gate, b_gate, w_up, b_up, w_down, b_down, final_norm_g):
    B, S, D = x.shape
    T = B * S
    G = KV_GROUPS
    assert D == D_MODEL and S % ATTN_TK == 0 and S % CMP_STRIDE == 0 and KV_W == LANES
    rc, rs1, rs2, cos_t, sin_t = _rope_tables(S)
    n_sel = S // SEL_BLK
    nc = S // CMP_STRIDE
    cstart = jnp.arange(nc) * CMP_STRIDE
    jstart = jnp.arange(n_sel) * SEL_BLK
    overlap_t = ((cstart[None, :] <= jstart[:, None] + SEL_BLK - 1)
                 & (cstart[None, :] + CMP_LEN - 1 >= jstart[:, None])
                 & (jnp.arange(nc)[None, :] < nc - 1)).astype(F32)

    assert w_ada.shape[0] == 1
    for l in range(1):
        mod = _adaln(c, w_ada[l], b_ada[l][None])
        sh1, sc1, gt1, sh2, sc2, gt2 = [m[:, None, :] for m in jnp.split(mod, 6, axis=-1)]

        wl = w_in[l]
        o = 2 * CONV_CH + NSA_WIDTH
        kvc = [wl[:, o + i * KV_W:o + (i + 1) * KV_W] for i in range(6)]
        gl = wl[:, o + 6 * KV_W:]
        per_g = 3 * Q_PER_G
        gpad = [jnp.pad(gl[:, per_g * g:per_g * (g + 1)], ((0, 0), (0, GATE_ROWS - per_g))) for g in range(G)]
        wn = jnp.concatenate([wl[:, :2 * CONV_CH], kvc[0], kvc[2], kvc[4], kvc[1]], axis=1).astype(BF16)
        wt = jnp.concatenate([wl[:, 2 * CONV_CH:o], kvc[3], kvc[5]] + gpad, axis=1).T.astype(BF16)
        u, kc, vc, ks, kw, qt, vst, vwt, gates = _inproj(x, sc1, sh1, norm1_g[l][None], wn, wt,
                                                         rc, rs1, rs2, cos_t, sin_t)

        conv_n = _conv(u, conv_w[l], conv_b[l][None], conv_ln_g[l][None], conv_ln_b[l][None],
                       out_norm_conv[l][None])

        chunk = CMP_STRIDE * HEAD_DIM
        kcmp, vcmp_t = _compress(kc.reshape(B, G, nc, chunk), vc.reshape(B, G, nc, chunk),
                                 cmp_pe_k[l].reshape(1, -1), cmp_pe_v[l].reshape(1, -1),
                                 cmp_k_w1[l], cmp_k_w2[l], cmp_v_w1[l], cmp_v_w2[l].T)
        nsa = _attention(qt, kcmp, vcmp_t, ks, vst, kw, vwt, gates, overlap_t)

        x1, h2_rows, route, counts = _outproj(x, conv_n, nsa, out_norm_nsa[l][None], w_out[l].astype(BF16), gt1,
                                              norm2_g[l][None], sc2, sh2, w_router[l], b_router[l][None])

        block_e, nb_used, buf_tok, buf_gate, dest = _route_plan(route, counts[:, 0], T)
        y_rows = _ffn(block_e, nb_used, buf_tok, h2_rows.reshape(T, ROW_TILES, LANES), buf_gate,
                      w_gate[l], b_gate[l][:, None, :], w_up[l], b_up[l][:, None, :],
                      w_down[l], b_down[l][:, None, :])
        tm = min(COMB_TM, S)
        dest3 = dest.reshape(TOP_K, T // tm, tm).transpose(1, 0, 2).reshape(T // tm, 1, TOP_K * tm)
        P = y_rows.shape[0] // ROW_TILES
        x = _combine(dest3, y_rows.reshape(P, ROW_TILES, LANES), x1.reshape(T, D), gt2, final_norm_g[None],
                     S).reshape(B, S, D)
    return x
```

```python
import functools

import jax
import jax.numpy as jnp
from jax import lax
from jax.experimental import pallas as pl
from jax.experimental.pallas import tpu as pltpu

F32 = jnp.float32
BF16 = jnp.bfloat16
I32 = jnp.int32
HI = lax.Precision.HIGHEST

D_MODEL = 1024
CONV_CH = 512
CONV_WIDTH = 31
NSA_HEADS = 8
KV_GROUPS = 2
Q_PER_G = NSA_HEADS // KV_GROUPS
HEAD_DIM = 64
NSA_WIDTH = NSA_HEADS * HEAD_DIM
KV_W = KV_GROUPS * HEAD_DIM
ROT_DIM = HEAD_DIM // 4
ROT_HALF = ROT_DIM // 2
ROPE_THETA = 500000.0
CMP_LEN = 32
CMP_STRIDE = 16
CMP_HIDDEN = 128
SEL_BLK = 64
SEL_TOPN = 16
WINDOW = 512
N_EXPERTS = 32
TOP_K = 4
D_FF = 1024
SWIGLU_ALPHA = 1.702
SWIGLU_LIMIT = 7.0
NORM_EPS = 1e-5
NEG_INF = -1e30
FORCE_SCORE = 1e9
LOG2_E = 1.4426950408889634

LANES = 128
SUBLANES = 8
ROW_TILES = D_MODEL // LANES

GATE_ROWS = 16

INPROJ_TM = 512
CONV_TR = 256
CONV_HALO = 32
ATTN_TQ = 256
ATTN_TK = 256
OUT_TM = 512
FFN_BM = 512
COMB_TM = 256
ROUTE_W = 16


def _rms(x, g):
    return x * lax.rsqrt(jnp.mean(x * x, axis=-1, keepdims=True) + NORM_EPS) * g


def _ada_kernel(c_ref, w_ref, b_ref, o_ref):
    c = c_ref[...]
    ca = c * jax.nn.sigmoid(c)
    o_ref[...] = jnp.dot(ca, w_ref[...], preferred_element_type=F32, precision=HI) + b_ref[...]


def _adaln(c, w, b):
    B = c.shape[0]
    D = D_MODEL
    return pl.pallas_call(
        _ada_kernel,
        grid=(6,),
        in_specs=[pl.BlockSpec((B, D), lambda j: (0, 0)),
                  pl.BlockSpec((D, D), lambda j: (0, j)),
                  pl.BlockSpec((1, D), lambda j: (0, j))],
        out_specs=pl.BlockSpec((B, D), lambda j: (0, j)),
        out_shape=jax.ShapeDtypeStruct((B, 6 * D), F32),
        name="adaln",
    )(c, w, b)


def _inproj_kernel(x_ref, sc_ref, sh_ref, g_ref, wn_ref, wt_ref, rc_ref, rs1_ref, rs2_ref, ct_ref, st_ref,
                   u_ref, kc_ref, vc_ref, ks_ref, kw_ref, qt_ref, vst_ref, vwt_ref, gt_ref):
    tm = x_ref.shape[1]
    tq, tk = ATTN_TQ, ATTN_TK
    h = (_rms(x_ref[0], g_ref[...]) * (1.0 + sc_ref[0]) + sh_ref[0]).astype(BF16)

    p = jnp.dot(h, wn_ref[...], preferred_element_type=F32)
    u_ref[0] = p[:, 0:CONV_CH] * jax.nn.sigmoid(p[:, CONV_CH:2 * CONV_CH])
    c0 = 2 * CONV_CH
    rc, rs1, rs2 = rc_ref[...], rs1_ref[...], rs2_ref[...]
    for ref, roped in ((kc_ref, True), (ks_ref, True), (kw_ref, True), (vc_ref, False)):
        v = p[:, c0:c0 + KV_W]
        if roped:
            v = v * rc + pltpu.roll(v, ROT_HALF, 1) * rs1 + pltpu.roll(v, KV_W - ROT_HALF, 1) * rs2
        for gg in range(KV_GROUPS):
            ref[0, gg] = v[:, HEAD_DIM * gg:HEAD_DIM * (gg + 1)].astype(ref.dtype)
        c0 += KV_W

    pt = lax.dot_general(wt_ref[...], h, (((1,), (1,)), ((), ())), preferred_element_type=F32)
    cos_t, sin_t = ct_ref[...], st_ref[...]
    scale = HEAD_DIM ** -0.5 * LOG2_E
    for hh in range(NSA_HEADS):
        blk = pt[HEAD_DIM * hh:HEAD_DIM * (hh + 1), :]
        x1, x2 = blk[0:ROT_HALF], blk[ROT_HALF:ROT_DIM]
        qh = (jnp.concatenate([x1 * cos_t - x2 * sin_t, x2 * cos_t + x1 * sin_t, blk[ROT_DIM:]], axis=0)
              * scale).astype(BF16)
        gg, n = divmod(hh, Q_PER_G)
        for j in range(tm // tq):
            qt_ref[0, gg, j, :, n * tq:(n + 1) * tq] = qh[:, j * tq:(j + 1) * tq]
    r0 = NSA_WIDTH
    for ref in (vst_ref, vwt_ref):
        for gg in range(KV_GROUPS):
            blk = pt[r0 + HEAD_DIM * gg:r0 + HEAD_DIM * (gg + 1), :].astype(BF16)
            for j in range(tm // tk):
                ref[0, gg, j] = blk[:, j * tk:(j + 1) * tk]
        r0 += KV_W
    for gg in range(KV_GROUPS):
        gt_ref[0, gg] = jax.nn.sigmoid(pt[r0 + GATE_ROWS * gg:r0 + GATE_ROWS * (gg + 1), :])


def _inproj(x, sc, sh, g, wn, wt, rc, rs1, rs2, cos_t, sin_t):
    B, S, D = x.shape
    tm = min(INPROJ_TM, S)
    tq, tk = ATTN_TQ, ATTN_TK
    G = KV_GROUPS
    kv = lambda dt: jax.ShapeDtypeStruct((B, G, S, HEAD_DIM), dt)
    kv_spec = pl.BlockSpec((1, G, tm, HEAD_DIM), lambda b, i: (b, 0, i, 0))
    vt_shape = jax.ShapeDtypeStruct((B, G, S // tk, HEAD_DIM, tk), BF16)
    vt_spec = pl.BlockSpec((1, G, tm // tk, HEAD_DIM, tk), lambda b, i: (b, 0, i, 0, 0))
    row = pl.BlockSpec((1, 1, D), lambda b, i: (b, 0, 0))
    tab = pl.BlockSpec((tm, LANES), lambda b, i: (i, 0))
    tab_t = pl.BlockSpec((ROT_HALF, tm), lambda b, i: (0, i))
    return pl.pallas_call(
        _inproj_kernel,
        grid=(B, S // tm),
        in_specs=[pl.BlockSpec((1, tm, D), lambda b, i: (b, i, 0)), row, row,
                  pl.BlockSpec((1, D), lambda b, i: (0, 0)),
                  pl.BlockSpec(wn.shape, lambda b, i: (0, 0)),
                  pl.BlockSpec(wt.shape, lambda b, i: (0, 0)),
                  tab, tab, tab, tab_t, tab_t],
        out_specs=[pl.BlockSpec((1, tm, CONV_CH), lambda b, i: (b, i, 0)),
                   kv_spec, kv_spec, kv_spec, kv_spec,
                   pl.BlockSpec((1, G, tm // tq, HEAD_DIM, Q_PER_G * tq), lambda b, i: (b, 0, i, 0, 0)),
                   vt_spec, vt_spec,
                   pl.BlockSpec((1, G, GATE_ROWS, tm), lambda b, i: (b, 0, 0, i))],
        out_shape=[jax.ShapeDtypeStruct((B, S, CONV_CH), F32),
                   kv(F32), kv(F32), kv(BF16), kv(BF16),
                   jax.ShapeDtypeStruct((B, G, S // tq, HEAD_DIM, Q_PER_G * tq), BF16),
                   vt_shape, vt_shape,
                   jax.ShapeDtypeStruct((B, G, GATE_ROWS, S), F32)],
        compiler_params=pltpu.CompilerParams(dimension_semantics=("parallel", "parallel")),
        name="inproj",
    )(x, sc, sh, g, wn, wt, rc, rs1, rs2, cos_t, sin_t)


def _conv_kernel(prev_ref, cur_ref, w_ref, cb_ref, lg_ref, lb_ref, on_ref, o_ref, pad_ref, win_ref):
    tr = cur_ref.shape[1]
    first = pl.program_id(1) == 0
    halo = prev_ref[0, tr - CONV_HALO:tr, :]
    pad_ref[0:CONV_HALO, :] = jnp.where(first, 0.0, halo)
    pad_ref[CONV_HALO:CONV_HALO + tr, :] = cur_ref[0]
    off = CONV_HALO - (CONV_WIDTH - 1)
    acc = jnp.zeros((tr, CONV_CH), F32)
    for b in range(SUBLANES):
        taps = range(b, CONV_WIDTH, SUBLANES)
        rows = tr + SUBLANES * (len(taps) - 1)
        win_ref[b, 0:rows, :] = pad_ref[off + b:off + b + rows, :]
        for a, k in enumerate(taps):
            acc = acc + win_ref[b, SUBLANES * a:SUBLANES * a + tr, :] * w_ref[k:k + 1, :]
    y = acc + cb_ref[...]
    mu = jnp.mean(y, axis=-1, keepdims=True)
    yc = y - mu
    var = jnp.mean(yc * yc, axis=-1, keepdims=True)
    yn = yc * lax.rsqrt(var + NORM_EPS) * lg_ref[...] + lb_ref[...]
    s = yn * jax.nn.sigmoid(yn)
    o_ref[0] = _rms(s, on_ref[...]).astype(o_ref.dtype)


def _conv(u, w, cb, lg, lb, on):
    B, S, C = u.shape
    tr = min(CONV_TR, S)
    vec = pl.BlockSpec((1, C), lambda b, i: (0, 0))
    return pl.pallas_call(
        _conv_kernel,
        grid=(B, S // tr),
        in_specs=[pl.BlockSpec((1, tr, C), lambda b, i: (b, jnp.maximum(i - 1, 0), 0)),
                  pl.BlockSpec((1, tr, C), lambda b, i: (b, i, 0)),
                  pl.BlockSpec((CONV_WIDTH, C), lambda b, i: (0, 0)),
                  vec, vec, vec, vec],
        out_specs=pl.BlockSpec((1, tr, C), lambda b, i: (b, i, 0)),
        out_shape=jax.ShapeDtypeStruct((B, S, C), BF16),
        scratch_shapes=[pltpu.VMEM((CONV_HALO + tr, C), F32),
                        pltpu.VMEM((SUBLANES, tr + SUBLANES * ((CONV_WIDTH - 1) // SUBLANES), C), F32)],
        compiler_params=pltpu.CompilerParams(dimension_semantics=("parallel", "parallel")),
        name="conv",
    )(u, u, w, cb, lg, lb, on)


def _cmp_kernel(kx_ref, vx_ref, pek_ref, pev_ref, kw1_ref, kw2_ref, vw1_ref, vw2t_ref, ko_ref, vo_ref):
    nc = kx_ref.shape[2]
    half = kx_ref.shape[3]
    nt = (((1,), (1,)), ((), ()))
    for x_ref, pe_ref, w1_ref, w2_ref, o_ref, transposed in ((kx_ref, pek_ref, kw1_ref, kw2_ref, ko_ref, False),
                                                             (vx_ref, pev_ref, vw1_ref, vw2t_ref, vo_ref, True)):
        w1 = w1_ref[...]
        pe = jnp.broadcast_to(pe_ref[...], (SUBLANES, 2 * half))
        pe_proj = jnp.dot(pe, w1, preferred_element_type=F32, precision=HI)[0:1]
        for gg in range(KV_GROUPS):
            xg = x_ref[0, gg]
            first = jnp.dot(xg, w1[0:half], preferred_element_type=F32, precision=HI)
            second = jnp.dot(xg, w1[half:2 * half], preferred_element_type=F32, precision=HI)
            hid = first + pltpu.roll(second, nc - 1, 0) + pe_proj
            hid = hid * jax.nn.sigmoid(hid)
            if transposed:
                o_ref[0, gg] = lax.dot_general(w2_ref[...], hid, nt, preferred_element_type=F32, precision=HI)
            else:
                o_ref[0, gg] = jnp.dot(hid, w2_ref[...], preferred_element_type=F32, precision=HI)


def _compress(kx, vx, pek, pev, kw1, kw2, vw1, vw2t):
    B, G, NC, W = kx.shape
    xs = pl.BlockSpec((1, G, NC, W), lambda b: (b, 0, 0, 0))
    full = lambda a: pl.BlockSpec(a.shape, lambda b: (0,) * a.ndim)
    return pl.pallas_call(
        _cmp_kernel,
        grid=(B,),
        in_specs=[xs, xs, full(pek), full(pev), full(kw1), full(kw2), full(vw1), full(vw2t)],
        out_specs=[pl.BlockSpec((1, G, NC, HEAD_DIM), lambda b: (b, 0, 0, 0)),
                   pl.BlockSpec((1, G, HEAD_DIM, NC), lambda b: (b, 0, 0, 0))],
        out_shape=[jax.ShapeDtypeStruct((B, G, NC, HEAD_DIM), F32),
                   jax.ShapeDtypeStruct((B, G, HEAD_DIM, NC), F32)],
        compiler_params=pltpu.CompilerParams(dimension_semantics=("parallel",)),
        name="compress",
    )(kx, vx, pek, pev, kw1, kw2, vw1, vw2t)


def _attn_kernel(qt_ref, kc_ref, vct_ref, ks_ref, vst_ref, kw_ref, vwt_ref, gt_ref, ovt_ref, o_ref, sel_ref):
    tq, tk = ATTN_TQ, ATTN_TK
    R = Q_PER_G * tq
    per_tile = tk // SEL_BLK
    S = ks_ref.shape[2]
    ncp = kc_ref.shape[2]
    nsel = ovt_ref.shape[0]
    qi = pl.program_id(2)
    q0 = qi * tq
    qt = qt_ref[0, 0, 0]
    t_row = q0 + lax.broadcasted_iota(I32, (1, tq), 1)
    heads = lambda a: jnp.concatenate([a] * Q_PER_G, axis=1)

    sc = jnp.dot(kc_ref[0, 0].astype(BF16), qt, preferred_element_type=F32)
    c_io = lax.broadcasted_iota(I32, (ncp, tq), 0)
    m_c = (c_io * CMP_STRIDE + (CMP_LEN - 1) <= t_row) & (c_io < ncp - 1)
    scb = sc + heads(jnp.where(m_c, 0.0, NEG_INF))
    e = jnp.exp2(scb - jnp.max(scb, axis=0, keepdims=True)) * heads(jnp.where(m_c, 1.0, 0.0))
    den = jnp.sum(e, axis=0, keepdims=True)
    pc = e / jnp.where(den > 0.0, den, 1.0)
    o_cmp = jnp.dot(vct_ref[0, 0].astype(BF16), pc.astype(BF16), preferred_element_type=F32)

    psum = pc[:, 0:tq]
    for n in range(1, Q_PER_G):
        psum = psum + pc[:, n * tq:(n + 1) * tq]
    imp = jnp.dot(ovt_ref[...], psum, preferred_element_type=F32, precision=HI)
    j_io = lax.broadcasted_iota(I32, (nsel, tq), 0)
    cur = t_row // SEL_BLK
    valid = j_io * SEL_BLK <= t_row
    forced = (j_io == 0) | (j_io == cur) | (j_io == cur - 1)
    score = jnp.where(valid, jnp.where(forced, FORCE_SCORE, imp), NEG_INF)
    rank = jnp.zeros((nsel, tq), F32)
    for i in range(nsel):
        row = score[i:i + 1, :]
        tie = jnp.where(j_io > i, 1.0, 0.0)
        rank = rank + jnp.where(row > score, 1.0, jnp.where(row == score, tie, 0.0))
    sel_bias = jnp.where(rank < float(min(SEL_TOPN, nsel)), 0.0, NEG_INF)
    sel_ref[...] = jnp.zeros(sel_ref.shape, F32)
    for jj in range(S // tk):
        sel_ref[jj, 0:per_tile, :] = sel_bias[jj * per_tile:(jj + 1) * per_tile, :]

    k_io = lax.broadcasted_iota(I32, (tk, tq), 0)

    def flash_step(k_ref, vt_ref, kj, bias, carry):
        m, l, acc = carry
        k0 = pl.multiple_of(kj * tk, tk)
        s = jnp.dot(k_ref[0, 0, pl.ds(k0, tk), :], qt, preferred_element_type=F32) + heads(bias)
        m_new = jnp.maximum(m, jnp.max(s, axis=0, keepdims=True))
        alpha = jnp.exp2(m - m_new)
        p = jnp.exp2(s - m_new)
        l = alpha * l + jnp.sum(p, axis=0, keepdims=True)
        acc = alpha * acc + jnp.dot(vt_ref[0, 0, kj], p.astype(BF16), preferred_element_type=F32)
        return m_new, l, acc

    init = (jnp.full((1, R), NEG_INF, F32), jnp.zeros((1, R), F32), jnp.zeros((HEAD_DIM, R), F32))

    def slc_body(kj, carry):
        blocks = sel_ref[kj]
        bias = jnp.concatenate([jnp.broadcast_to(blocks[b:b + 1, :], (SEL_BLK, tq)) for b in range(per_tile)], axis=0)
        bias = jnp.where(kj * tk + k_io <= t_row, bias, NEG_INF)
        return flash_step(ks_ref, vst_ref, kj, bias, carry)

    def win_body(kj, carry):
        rel = t_row - (kj * tk + k_io)
        bias = jnp.where((rel >= 0) & (rel < WINDOW), 0.0, NEG_INF)
        return flash_step(kw_ref, vwt_ref, kj, bias, carry)

    n_slc = (q0 + tq + tk - 1) // tk
    lo_tile = jnp.maximum(q0 - (WINDOW - 1), 0) // tk
    n_pair = lo_tile // 2

    def pair_body(i, carry):
        return slc_body(2 * i, carry[0]), slc_body(2 * i + 1, carry[1])

    st_a, st_b = lax.fori_loop(0, n_pair, pair_body, (init, init))
    st_a = lax.fori_loop(2 * n_pair, lo_tile, slc_body, st_a)

    def both_body(kj, carry):
        return slc_body(kj, carry[0]), win_body(kj, carry[1])

    st_a, (_, l_w, acc_w) = lax.fori_loop(lo_tile, n_slc, both_body, (st_a, init))
    m_s = jnp.maximum(st_a[0], st_b[0])
    w_a, w_b = jnp.exp2(st_a[0] - m_s), jnp.exp2(st_b[0] - m_s)
    l_s = w_a * st_a[1] + w_b * st_b[1]
    acc_s = w_a * st_a[2] + w_b * st_b[2]


    gt = gt_ref[0, 0]
    o_slc = acc_s / l_s
    o_win = acc_w / l_w
    outs = []
    for n in range(Q_PER_G):
        cols = slice(n * tq, (n + 1) * tq)
        outs.append(gt[3 * n:3 * n + 1, :] * o_cmp[:, cols] + gt[3 * n + 1:3 * n + 2, :] * o_slc[:, cols]
                    + gt[3 * n + 2:3 * n + 3, :] * o_win[:, cols])
    o_ref[0] = jnp.concatenate(outs, axis=0).T


def _attention(qt, kc, vct, ks, vst, kw, vwt, gt, overlap_t):
    B, G, NQT, _, R = qt.shape
    S = ks.shape[2]
    tq, tk = ATTN_TQ, ATTN_TK
    ncp = kc.shape[2]
    kfull = pl.BlockSpec((1, 1, S, HEAD_DIM), lambda b, g, i: (b, g, 0, 0))
    vfull = pl.BlockSpec((1, 1, S // tk, HEAD_DIM, tk), lambda b, g, i: (b, g, 0, 0, 0))
    return pl.pallas_call(
        _attn_kernel,
        grid=(B, G, NQT),
        in_specs=[pl.BlockSpec((1, 1, 1, HEAD_DIM, R), lambda b, g, i: (b, g, i, 0, 0)),
                  pl.BlockSpec((1, 1, ncp, HEAD_DIM), lambda b, g, i: (b, g, 0, 0)),
                  pl.BlockSpec((1, 1, HEAD_DIM, ncp), lambda b, g, i: (b, g, 0, 0)),
                  kfull, vfull, kfull, vfull,
                  pl.BlockSpec((1, 1, GATE_ROWS, tq), lambda b, g, i: (b, g, 0, i)),
                  pl.BlockSpec(overlap_t.shape, lambda b, g, i: (0, 0))],
        out_specs=pl.BlockSpec((1, tq, Q_PER_G * HEAD_DIM), lambda b, g, i: (b, i, g)),
        out_shape=jax.ShapeDtypeStruct((B, S, NSA_WIDTH), F32),
        scratch_shapes=[pltpu.VMEM((S // tk, SUBLANES, tq), F32)],
        compiler_params=pltpu.CompilerParams(dimension_semantics=("parallel", "parallel", "arbitrary")),
        name="attn",
    )(qt, kc, vct, ks, vst, kw, vwt, gt, overlap_t)


def _outproj_kernel(x_ref, cv_ref, nsa_ref, on_ref, w_ref, gt1_ref, g2_ref, sc2_ref, sh2_ref, wrh_ref, wrl_ref, br_ref,
                    tri_ref, x1_ref, h2_ref, rt_ref, cnt_ref, run_ref):
    tm = x_ref.shape[1]

    @pl.when((pl.program_id(0) == 0) & (pl.program_id(1) == 0))
    def _():
        run_ref[...] = jnp.zeros(run_ref.shape, F32)

    nn = _rms(nsa_ref[0], on_ref[...]).astype(BF16)
    y = jnp.dot(jnp.concatenate([cv_ref[0], nn], axis=1), w_ref[...], preferred_element_type=F32)
    x1 = x_ref[0] + gt1_ref[0] * y
    x1_ref[0] = x1
    h2 = _rms(x1, g2_ref[...]) * (1.0 + sc2_ref[0]) + sh2_ref[0]
    for s in range(ROW_TILES):
        h2_ref[pl.ds(s, tm, stride=ROW_TILES), :] = h2[:, s * LANES:(s + 1) * LANES]
    nt = (((1,), (1,)), ((), ()))
    h_hi = h2.astype(BF16)
    h_lo = (h2 - h_hi.astype(F32)).astype(BF16)
    logits = (lax.dot_general(wrh_ref[...], h_hi, nt, preferred_element_type=F32)
              + lax.dot_general(wrh_ref[...], h_lo, nt, preferred_element_type=F32)
              + lax.dot_general(wrl_ref[...], h_hi, nt, preferred_element_type=F32)) + br_ref[...]
    eio = lax.broadcasted_iota(I32, (N_EXPERTS, tm), 0).astype(F32)
    vals, idxs = [], []
    for _ in range(TOP_K):
        m = jnp.max(logits, axis=0, keepdims=True)
        ix = jnp.min(jnp.where(logits == m, eio, float(N_EXPERTS)), axis=0, keepdims=True)
        vals.append(m)
        idxs.append(ix)
        logits = jnp.where(eio == ix, -jnp.inf, logits)
    es = [jnp.exp(v - vals[0]) for v in vals]
    den = es[0] + es[1] + es[2] + es[3]
    hot = jnp.zeros((N_EXPERTS, tm), F32)
    for r in range(TOP_K):
        hot = hot + jnp.where(eio == idxs[r], 1.0, 0.0)
    before = run_ref[...] + jnp.dot(hot.astype(BF16), tri_ref[...], preferred_element_type=F32)
    ranks = [jnp.sum(jnp.where(eio == idxs[r], before, 0.0), axis=0, keepdims=True) for r in range(TOP_K)]
    run_ref[...] = run_ref[...] + jnp.sum(hot, axis=1, keepdims=True)
    cnt_ref[...] = run_ref[...]
    rio = lax.broadcasted_iota(I32, (ROUTE_W, tm), 0)
    out = jnp.zeros((ROUTE_W, tm), F32)
    for r in range(TOP_K):
        out = jnp.where(rio == r, idxs[r], out)
        out = jnp.where(rio == TOP_K + r, es[r] / den, out)
        out = jnp.where(rio == 2 * TOP_K + r, ranks[r], out)
    rt_ref[...] = out


def _outproj(x, cv, nsa, on, w, gt1, g2, sc2, sh2, wr, br):
    B, S, D = x.shape
    tm = min(OUT_TM, S)
    nt = S // tm
    tri = (jnp.arange(tm)[:, None] < jnp.arange(tm)[None, :]).astype(BF16)
    wr_t = wr.T
    wr_hi = wr_t.astype(BF16)
    wr_lo = (wr_t - wr_hi.astype(F32)).astype(BF16)
    row = pl.BlockSpec((1, 1, D), lambda b, i: (b, 0, 0))
    vec = lambda n: pl.BlockSpec((1, n), lambda b, i: (0, 0))
    col = pl.BlockSpec((N_EXPERTS, 1), lambda b, i: (0, 0))
    wr_spec = pl.BlockSpec((N_EXPERTS, D), lambda b, i: (0, 0))
    return pl.pallas_call(
        _outproj_kernel,
        grid=(B, nt),
        in_specs=[pl.BlockSpec((1, tm, D), lambda b, i: (b, i, 0)),
                  pl.BlockSpec((1, tm, CONV_CH), lambda b, i: (b, i, 0)),
                  pl.BlockSpec((1, tm, NSA_WIDTH), lambda b, i: (b, i, 0)),
                  vec(NSA_WIDTH),
                  pl.BlockSpec((D, D), lambda b, i: (0, 0)),
                  row, vec(D), row, row,
                  wr_spec, wr_spec, col,
                  pl.BlockSpec((tm, tm), lambda b, i: (0, 0))],
        out_specs=[pl.BlockSpec((1, tm, D), lambda b, i: (b, i, 0)),
                   pl.BlockSpec((tm * ROW_TILES, LANES), lambda b, i: (b * nt + i, 0)),
                   pl.BlockSpec((ROUTE_W, tm), lambda b, i: (0, b * nt + i)),
                   col],
        out_shape=[jax.ShapeDtypeStruct((B, S, D), F32),
                   jax.ShapeDtypeStruct((B * S * ROW_TILES, LANES), F32),
                   jax.ShapeDtypeStruct((ROUTE_W, B * S), F32),
                   jax.ShapeDtypeStruct((N_EXPERTS, 1), F32)],
        scratch_shapes=[pltpu.VMEM((N_EXPERTS, 1), F32)],
        compiler_params=pltpu.CompilerParams(dimension_semantics=("arbitrary", "arbitrary")),
        name="outproj",
    )(x, cv, nsa, on, w, gt1, g2, sc2, sh2, wr_hi, wr_lo, br.reshape(N_EXPERTS, 1), tri)


def _issue_rows(idx_ref, n, src_hbm, dst, slot, sem):
    for r in range(n):
        pltpu.make_async_copy(src_hbm.at[idx_ref[0, 0, r]],
                              dst.at[slot, pl.ds(r * ROW_TILES, ROW_TILES), :],
                              sem.at[slot]).start(priority=r % 2)


def _wait_rows(dst, slot, sem):
    pltpu.make_async_copy(dst.at[slot], dst.at[slot], sem.at[slot]).wait()


def _rows_2d(buf, slot, base, n):
    return jnp.concatenate(
        [buf[slot, pl.ds(base * ROW_TILES + s, n, stride=ROW_TILES), :] for s in range(ROW_TILES)], axis=1)


def _ffn_kernel(be_ref, nb_ref, ta_ref, tb_ref, h2_hbm, gate_ref, wg_ref, bg_ref, wu_ref, bu_ref, wd_ref, bd_ref,
                o_ref, xbuf, wgb, wub, wdb, sem):
    bm = ta_ref.shape[2]
    i = pl.program_id(0)
    nb = nb_ref[0]
    slot = i % 2

    @pl.when((i == 0) | (be_ref[i] != be_ref[jnp.maximum(i - 1, 0)]))
    def _():
        wgb[...] = wg_ref[0].astype(BF16)
        wub[...] = wu_ref[0].astype(BF16)
        wdb[...] = wd_ref[0].astype(BF16)

    @pl.when(i == 0)
    def _():
        _issue_rows(ta_ref, bm, h2_hbm, xbuf, 0, sem)

    @pl.when(i + 1 < nb)
    def _():
        _issue_rows(tb_ref, bm, h2_hbm, xbuf, 1 - slot, sem)

    @pl.when(i < nb)
    def _():
        _wait_rows(xbuf, slot, sem)
        x = _rows_2d(xbuf, slot, 0, bm).astype(BF16)
        g = jnp.dot(x, wgb[...], preferred_element_type=F32) + bg_ref[0]
        u = jnp.dot(x, wub[...], preferred_element_type=F32) + bu_ref[0]
        g = jnp.minimum(g, SWIGLU_LIMIT)
        u = jnp.clip(u, -SWIGLU_LIMIT, SWIGLU_LIMIT)
        act = g * jax.nn.sigmoid(SWIGLU_ALPHA * g) * (u + 1.0)
        y = (jnp.dot(act.astype(BF16), wdb[...], preferred_element_type=F32) + bd_ref[0]) * gate_ref[...]
        for s in range(ROW_TILES):
            o_ref[pl.ds(s, bm, stride=ROW_TILES), :] = y[:, s * LANES:(s + 1) * LANES]

    @pl.when(i >= nb)
    def _():
        o_ref[...] = jnp.zeros(o_ref.shape, o_ref.dtype)


def _ffn(block_e, nb_used, buf_tok, h2_rows, buf_gate, wg, bg, wu, bu, wd, bd):
    NB = block_e.shape[0]
    bm = FFN_BM
    D, F = D_MODEL, D_FF
    tok3 = buf_tok.reshape(NB, 1, bm)
    wspec = lambda r, c: pl.BlockSpec((1, r, c), lambda i, be, nb: (be[i], 0, 0))
    vmem_limit = 2 * 3 * D * F * 4 + 3 * D * F * 2 + 4 * bm * D * 4 + 6 * bm * F * 4
    return pl.pallas_call(
        _ffn_kernel,
        grid_spec=pltpu.PrefetchScalarGridSpec(
            num_scalar_prefetch=2,
            grid=(NB,),
            in_specs=[pl.BlockSpec((1, 1, bm), lambda i, be, nb: (i, 0, 0), memory_space=pltpu.SMEM),
                      pl.BlockSpec((1, 1, bm), lambda i, be, nb: (jnp.minimum(i + 1, NB - 1), 0, 0),
                                   memory_space=pltpu.SMEM),
                      pl.BlockSpec(memory_space=pl.ANY),
                      pl.BlockSpec((bm, 1), lambda i, be, nb: (i, 0)),
                      wspec(D, F), wspec(1, F), wspec(D, F), wspec(1, F), wspec(F, D), wspec(1, D)],
            out_specs=pl.BlockSpec((bm * ROW_TILES, LANES), lambda i, be, nb: (i, 0)),
            scratch_shapes=[pltpu.VMEM((2, bm * ROW_TILES, LANES), F32),
                            pltpu.VMEM((D, F), BF16), pltpu.VMEM((D, F), BF16), pltpu.VMEM((F, D), BF16),
                            pltpu.SemaphoreType.DMA((2,))]),
        out_shape=jax.ShapeDtypeStruct((NB * bm * ROW_TILES, LANES), F32),
        compiler_params=pltpu.CompilerParams(dimension_semantics=("arbitrary",),
                                             vmem_limit_bytes=vmem_limit),
        name="ffn",
    )(block_e, nb_used, tok3, tok3, h2_rows, buf_gate, wg, bg, wu, bu, wd, bd)


def _combine_kernel(da_ref, db_ref, y_hbm, x1_ref, gt2_ref, fg_ref, o_ref, buf, sem):
    tm = x1_ref.shape[0]
    n = TOP_K * tm
    i = pl.program_id(0)
    slot = i % 2

    @pl.when(i == 0)
    def _():
        _issue_rows(da_ref, n, y_hbm, buf, 0, sem)

    @pl.when(i + 1 < pl.num_programs(0))
    def _():
        _issue_rows(db_ref, n, y_hbm, buf, 1 - slot, sem)

    _wait_rows(buf, slot, sem)
    y = _rows_2d(buf, slot, 0, tm)
    for k in range(1, TOP_K):
        y = y + _rows_2d(buf, slot, k * tm, tm)
    x2 = x1_ref[...] + gt2_ref[0] * y
    o_ref[...] = _rms(x2, fg_ref[...])


def _combine(dest3, y_rows, x1, gt2, fg, S):
    T, D = x1.shape
    tm = min(COMB_TM, S)
    NT = T // tm
    per_b = S // tm
    n = TOP_K * tm
    return pl.pallas_call(
        _combine_kernel,
        grid=(NT,),
        in_specs=[pl.BlockSpec((1, 1, n), lambda i: (i, 0, 0), memory_space=pltpu.SMEM),
                  pl.BlockSpec((1, 1, n), lambda i: (jnp.minimum(i + 1, NT - 1), 0, 0), memory_space=pltpu.SMEM),
                  pl.BlockSpec(memory_space=pl.ANY),
                  pl.BlockSpec((tm, D), lambda i: (i, 0)),
                  pl.BlockSpec((1, 1, D), lambda i: (i // per_b, 0, 0)),
                  pl.BlockSpec((1, D), lambda i: (0, 0))],
        out_specs=pl.BlockSpec((tm, D), lambda i: (i, 0)),
        out_shape=jax.ShapeDtypeStruct((T, D), F32),
        scratch_shapes=[pltpu.VMEM((2, n * ROW_TILES, LANES), F32), pltpu.SemaphoreType.DMA((2,))],
        compiler_params=pltpu.CompilerParams(dimension_semantics=("arbitrary",)),
        name="combine",
    )(dest3, dest3, y_rows, x1, gt2, fg)


def _rope_tables(S):
    inv = ROPE_THETA ** (-jnp.arange(0, ROT_DIM, 2, dtype=F32) / ROT_DIM)
    ang = jnp.arange(S, dtype=F32)[:, None] * inv[None, :]
    cos, sin = jnp.cos(ang), jnp.sin(ang)
    d = jnp.arange(KV_W) % HEAD_DIM
    first, second = d < ROT_HALF, (d >= ROT_HALF) & (d < ROT_DIM)
    cos_l = cos[:, d % ROT_HALF]
    sin_l = sin[:, d % ROT_HALF]
    rc = jnp.where((d < ROT_DIM)[None], cos_l, 1.0)
    rs1 = jnp.where(second[None], sin_l, 0.0)
    rs2 = jnp.where(first[None], -sin_l, 0.0)
    return rc, rs1, rs2, cos.T, sin.T


def _route_plan(route, counts, T):
    bm = FFN_BM
    A = T * TOP_K
    idx = route[0:TOP_K].astype(I32)
    gate = route[TOP_K:2 * TOP_K]
    rank = route[2 * TOP_K:3 * TOP_K].astype(I32)
    keys = idx * T + jnp.arange(T, dtype=I32)[None, :]
    skey, sgate = lax.sort((keys.reshape(A), gate.reshape(A)), num_keys=1)
    counts = counts.astype(I32)
    starts = jnp.cumsum(counts) - counts
    padded = (counts + bm - 1) // bm * bm
    pends = jnp.cumsum(padded)
    pstarts = pends - padded
    P = (A + N_EXPERTS * bm + bm - 1) // bm * bm
    NB = P // bm
    blk0 = jnp.arange(NB, dtype=I32) * bm
    block_e = jnp.minimum(jnp.sum((pends[None, :] <= blk0[:, None]).astype(I32), axis=1), N_EXPERTS - 1)
    r = (blk0 - pstarts[block_e])[:, None] + jnp.arange(bm, dtype=I32)[None, :]
    valid = r < counts[block_e][:, None]
    src = jnp.clip(starts[block_e][:, None] + r, 0, A - 1)
    buf_tok = jnp.where(valid, skey[src] - block_e[:, None] * T, 0)
    buf_gate = jnp.where(valid, sgate[src], 0.0).reshape(P, 1)
    dest = rank
    for e in range(N_EXPERTS):
        dest = dest + jnp.where(idx == e, pstarts[e], 0)
    nb_used = (pends[-1] // bm).astype(I32).reshape(1)
    return block_e, nb_used, buf_tok, buf_gate, dest


def kernel(x, c, norm1_g, norm2_g, w_ada, b_ada, w_in, conv_w, conv_b, conv_ln_g, conv_ln_b, cmp_pe_k, cmp_pe_v,
           cmp_k_w1, cmp_k_w2, cmp_v_w1, cmp_v_w2, out_norm_conv, out_norm_nsa, w_out, w_router, b_router,
           w_gate, b_gate, w_up, b_up, w_down, b_down, final_norm_g):
    B, S, D = x.shape
    T = B * S
    G = KV_GROUPS
    assert D == D_MODEL and S % ATTN_TK == 0 and S % CMP_STRIDE == 0 and KV_W == LANES
    rc, rs1, rs2, cos_t, sin_t = _rope_tables(S)
    n_sel = S // SEL_BLK
    nc = S // CMP_STRIDE
    cstart = jnp.arange(nc) * CMP_STRIDE
    jstart = jnp.arange(n_sel) * SEL_BLK
    overlap_t = ((cstart[None, :] <= jstart[:, None] + SEL_BLK - 1)
                 & (cstart[None, :] + CMP_LEN - 1 >= jstart[:, None])
                 & (jnp.arange(nc)[None, :] < nc - 1)).astype(F32)

    assert w_ada.shape[0] == 1
    for l in range(1):
        mod = _adaln(c, w_ada[l], b_ada[l][None])
        sh1, sc1, gt1, sh2, sc2, gt2 = [m[:, None, :] for m in jnp.split(mod, 6, axis=-1)]

        wl = w_in[l]
        o = 2 * CONV_CH + NSA_WIDTH
        kvc = [wl[:, o + i * KV_W:o + (i + 1) * KV_W] for i in range(6)]
        gl = wl[:, o + 6 * KV_W:]
        per_g = 3 * Q_PER_G
        gpad = [jnp.pad(gl[:, per_g * g:per_g * (g + 1)], ((0, 0), (0, GATE_ROWS - per_g))) for g in range(G)]
        wn = jnp.concatenate([wl[:, :2 * CONV_CH], kvc[0], kvc[2], kvc[4], kvc[1]], axis=1).astype(BF16)
        wt = jnp.concatenate([wl[:, 2 * CONV_CH:o], kvc[3], kvc[5]] + gpad, axis=1).T.astype(BF16)
        u, kc, vc, ks, kw, qt, vst, vwt, gates = _inproj(x, sc1, sh1, norm1_g[l][None], wn, wt,
                                                         rc, rs1, rs2, cos_t, sin_t)

        conv_n = _conv(u, conv_w[l], conv_b[l][None], conv_ln_g[l][None], conv_ln_b[l][None],
                       out_norm_conv[l][None])

        chunk = CMP_STRIDE * HEAD_DIM
        kcmp, vcmp_t = _compress(kc.reshape(B, G, nc, chunk), vc.reshape(B, G, nc, chunk),
                                 cmp_pe_k[l].reshape(1, -1), cmp_pe_v[l].reshape(1, -1),
                                 cmp_k_w1[l], cmp_k_w2[l], cmp_v_w1[l], cmp_v_w2[l].T)
        nsa = _attention(qt, kcmp, vcmp_t, ks, vst, kw, vwt, gates, overlap_t)

        x1, h2_rows, route, counts = _outproj(x, conv_n, nsa, out_norm_nsa[l][None], w_out[l].astype(BF16), gt1,
                                              norm2_g[l][None], sc2, sh2, w_router[l], b_router[l][None])

        block_e, nb_used, buf_tok, buf_gate, dest = _route_plan(route, counts[:, 0], T)
        y_rows = _ffn(block_e, nb_used, buf_tok, h2_rows.reshape(T, ROW_TILES, LANES), buf_gate,
                      w_gate[l], b_gate[l][:, None, :], w_up[l], b_up[l][:, None, :],
                      w_down[l], b_down[l][:, None, :])
        tm = min(COMB_TM, S)
        dest3 = dest.reshape(TOP_K, T // tm, tm).transpose(1, 0, 2).reshape(T // tm, 1, TOP_K * tm)
        P = y_rows.shape[0] // ROW_TILES
        x = _combine(dest3, y_rows.reshape(P, ROW_TILES, LANES), x1.reshape(T, D), gt2, final_norm_g[None],
                     S).reshape(B, S, D)
    return x
```

```python
import functools

import jax
import jax.numpy as jnp
from jax import lax
from jax.experimental import pallas as pl
from jax.experimental.pallas import tpu as pltpu

F32 = jnp.float32
BF16 = jnp.bfloat16
I32 = jnp.int32
HI = lax.Precision.HIGHEST

D_MODEL = 1024
CONV_CH = 512
CONV_WIDTH = 31
NSA_HEADS = 8
KV_GROUPS = 2
Q_PER_G = NSA_HEADS // KV_GROUPS
HEAD_DIM = 64
NSA_WIDTH = NSA_HEADS * HEAD_DIM
KV_W = KV_GROUPS * HEAD_DIM
ROT_DIM = HEAD_DIM // 4
ROT_HALF = ROT_DIM // 2
ROPE_THETA = 500000.0
CMP_LEN = 32
CMP_STRIDE = 16
CMP_HIDDEN = 128
SEL_BLK = 64
SEL_TOPN = 16
WINDOW = 512
N_EXPERTS = 32
TOP_K = 4
D_FF = 1024
SWIGLU_ALPHA = 1.702
SWIGLU_LIMIT = 7.0
NORM_EPS = 1e-5
NEG_INF = -1e30
FORCE_SCORE = 1e9
LOG2_E = 1.4426950408889634

LANES = 128
SUBLANES = 8
ROW_TILES = D_MODEL // LANES

GATE_ROWS = 16

INPROJ_TM = 512
CONV_TR = 256
CONV_HALO = 32
ATTN_TQ = 256
ATTN_TK = 256
OUT_TM = 512
FFN_BM = 512
FFN_CHUNKS = 4
COMB_TM = 256
ROUTE_W = 16


def _rms(x, g):
    return x * lax.rsqrt(jnp.mean(x * x, axis=-1, keepdims=True) + NORM_EPS) * g


def _ada_kernel(c_ref, w_ref, b_ref, o_ref):
    c = c_ref[...]
    ca = c * jax.nn.sigmoid(c)
    o_ref[...] = jnp.dot(ca, w_ref[...], preferred_element_type=F32, precision=HI) + b_ref[...]


def _adaln(c, w, b):
    B = c.shape[0]
    D = D_MODEL
    return pl.pallas_call(
        _ada_kernel,
        grid=(6,),
        in_specs=[pl.BlockSpec((B, D), lambda j: (0, 0)),
                  pl.BlockSpec((D, D), lambda j: (0, j)),
                  pl.BlockSpec((1, D), lambda j: (0, j))],
        out_specs=pl.BlockSpec((B, D), lambda j: (0, j)),
        out_shape=jax.ShapeDtypeStruct((B, 6 * D), F32),
        name="adaln",
    )(c, w, b)


def _inproj_kernel(x_ref, sc_ref, sh_ref, g_ref, wn_ref, wt_ref, rc_ref, rs1_ref, rs2_ref, ct_ref, st_ref,
                   u_ref, kc_ref, vc_ref, ks_ref, kw_ref, qt_ref, vst_ref, vwt_ref, gt_ref):
    tm = x_ref.shape[1]
    tq, tk = ATTN_TQ, ATTN_TK
    h = (_rms(x_ref[0], g_ref[...]) * (1.0 + sc_ref[0]) + sh_ref[0]).astype(BF16)

    p = jnp.dot(h, wn_ref[...], preferred_element_type=F32)
    u_ref[0] = p[:, 0:CONV_CH] * jax.nn.sigmoid(p[:, CONV_CH:2 * CONV_CH])
    c0 = 2 * CONV_CH
    rc, rs1, rs2 = rc_ref[...], rs1_ref[...], rs2_ref[...]
    for ref, roped in ((kc_ref, True), (ks_ref, True), (kw_ref, True), (vc_ref, False)):
        v = p[:, c0:c0 + KV_W]
        if roped:
            v = v * rc + pltpu.roll(v, ROT_HALF, 1) * rs1 + pltpu.roll(v, KV_W - ROT_HALF, 1) * rs2
        for gg in range(KV_GROUPS):
            ref[0, gg] = v[:, HEAD_DIM * gg:HEAD_DIM * (gg + 1)].astype(ref.dtype)
        c0 += KV_W

    pt = lax.dot_general(wt_ref[...], h, (((1,), (1,)), ((), ())), preferred_element_type=F32)
    cos_t, sin_t = ct_ref[...], st_ref[...]
    scale = HEAD_DIM ** -0.5 * LOG2_E
    for hh in range(NSA_HEADS):
        blk = pt[HEAD_DIM * hh:HEAD_DIM * (hh + 1), :]
        x1, x2 = blk[0:ROT_HALF], blk[ROT_HALF:ROT_DIM]
        qh = (jnp.concatenate([x1 * cos_t - x2 * sin_t, x2 * cos_t + x1 * sin_t, blk[ROT_DIM:]], axis=0)
              * scale).astype(BF16)
        gg, n = divmod(hh, Q_PER_G)
        for j in range(tm // tq):
            qt_ref[0, gg, j, :, n * tq:(n + 1) * tq] = qh[:, j * tq:(j + 1) * tq]
    r0 = NSA_WIDTH
    for ref in (vst_ref, vwt_ref):
        for gg in range(KV_GROUPS):
            blk = pt[r0 + HEAD_DIM * gg:r0 + HEAD_DIM * (gg + 1), :].astype(BF16)
            for j in range(tm // tk):
                ref[0, gg, j] = blk[:, j * tk:(j + 1) * tk]
        r0 += KV_W
    for gg in range(KV_GROUPS):
        gt_ref[0, gg] = jax.nn.sigmoid(pt[r0 + GATE_ROWS * gg:r0 + GATE_ROWS * (gg + 1), :])


def _inproj(x, sc, sh, g, wn, wt, rc, rs1, rs2, cos_t, sin_t):
    B, S, D = x.shape
    tm = min(INPROJ_TM, S)
    tq, tk = ATTN_TQ, ATTN_TK
    G = KV_GROUPS
    kv = lambda dt: jax.ShapeDtypeStruct((B, G, S, HEAD_DIM), dt)
    kv_spec = pl.BlockSpec((1, G, tm, HEAD_DIM), lambda b, i: (b, 0, i, 0))
    vt_shape = jax.ShapeDtypeStruct((B, G, S // tk, HEAD_DIM, tk), BF16)
    vt_spec = pl.BlockSpec((1, G, tm // tk, HEAD_DIM, tk), lambda b, i: (b, 0, i, 0, 0))
    row = pl.BlockSpec((1, 1, D), lambda b, i: (b, 0, 0))
    tab = pl.BlockSpec((tm, LANES), lambda b, i: (i, 0))
    tab_t = pl.BlockSpec((ROT_HALF, tm), lambda b, i: (0, i))
    return pl.pallas_call(
        _inproj_kernel,
        grid=(B, S // tm),
        in_specs=[pl.BlockSpec((1, tm, D), lambda b, i: (b, i, 0)), row, row,
                  pl.BlockSpec((1, D), lambda b, i: (0, 0)),
                  pl.BlockSpec(wn.shape, lambda b, i: (0, 0)),
                  pl.BlockSpec(wt.shape, lambda b, i: (0, 0)),
                  tab, tab, tab, tab_t, tab_t],
        out_specs=[pl.BlockSpec((1, tm, CONV_CH), lambda b, i: (b, i, 0)),
                   kv_spec, kv_spec, kv_spec, kv_spec,
                   pl.BlockSpec((1, G, tm // tq, HEAD_DIM, Q_PER_G * tq), lambda b, i: (b, 0, i, 0, 0)),
                   vt_spec, vt_spec,
                   pl.BlockSpec((1, G, GATE_ROWS, tm), lambda b, i: (b, 0, 0, i))],
        out_shape=[jax.ShapeDtypeStruct((B, S, CONV_CH), F32),
                   kv(F32), kv(F32), kv(BF16), kv(BF16),
                   jax.ShapeDtypeStruct((B, G, S // tq, HEAD_DIM, Q_PER_G * tq), BF16),
                   vt_shape, vt_shape,
                   jax.ShapeDtypeStruct((B, G, GATE_ROWS, S), F32)],
        compiler_params=pltpu.CompilerParams(dimension_semantics=("parallel", "parallel")),
        name="inproj",
    )(x, sc, sh, g, wn, wt, rc, rs1, rs2, cos_t, sin_t)


def _conv_kernel(prev_ref, cur_ref, w_ref, cb_ref, lg_ref, lb_ref, on_ref, o_ref, pad_ref, win_ref):
    tr = cur_ref.shape[1]
    first = pl.program_id(1) == 0
    halo = prev_ref[0, tr - CONV_HALO:tr, :]
    pad_ref[0:CONV_HALO, :] = jnp.where(first, 0.0, halo)
    pad_ref[CONV_HALO:CONV_HALO + tr, :] = cur_ref[0]
    off = CONV_HALO - (CONV_WIDTH - 1)
    acc = jnp.zeros((tr, CONV_CH), F32)
    for b in range(SUBLANES):
        taps = range(b, CONV_WIDTH, SUBLANES)
        rows = tr + SUBLANES * (len(taps) - 1)
        win_ref[b, 0:rows, :] = pad_ref[off + b:off + b + rows, :]
        for a, k in enumerate(taps):
            acc = acc + win_ref[b, SUBLANES * a:SUBLANES * a + tr, :] * w_ref[k:k + 1, :]
    y = acc + cb_ref[...]
    mu = jnp.mean(y, axis=-1, keepdims=True)
    yc = y - mu
    var = jnp.mean(yc * yc, axis=-1, keepdims=True)
    yn = yc * lax.rsqrt(var + NORM_EPS) * lg_ref[...] + lb_ref[...]
    s = yn * jax.nn.sigmoid(yn)
    o_ref[0] = _rms(s, on_ref[...]).astype(o_ref.dtype)


def _conv(u, w, cb, lg, lb, on):
    B, S, C = u.shape
    tr = min(CONV_TR, S)
    vec = pl.BlockSpec((1, C), lambda b, i: (0, 0))
    return pl.pallas_call(
        _conv_kernel,
        grid=(B, S // tr),
        in_specs=[pl.BlockSpec((1, tr, C), lambda b, i: (b, jnp.maximum(i - 1, 0), 0)),
                  pl.BlockSpec((1, tr, C), lambda b, i: (b, i, 0)),
                  pl.BlockSpec((CONV_WIDTH, C), lambda b, i: (0, 0)),
                  vec, vec, vec, vec],
        out_specs=pl.BlockSpec((1, tr, C), lambda b, i: (b, i, 0)),
        out_shape=jax.ShapeDtypeStruct((B, S, C), BF16),
        scratch_shapes=[pltpu.VMEM((CONV_HALO + tr, C), F32),
                        pltpu.VMEM((SUBLANES, tr + SUBLANES * ((CONV_WIDTH - 1) // SUBLANES), C), F32)],
        compiler_params=pltpu.CompilerParams(dimension_semantics=("parallel", "parallel")),
        name="conv",
    )(u, u, w, cb, lg, lb, on)


def _cmp_kernel(kx_ref, vx_ref, pek_ref, pev_ref, kw1_ref, kw2_ref, vw1_ref, vw2t_ref, ko_ref, vo_ref):
    nc = kx_ref.shape[2]
    half = kx_ref.shape[3]
    nt = (((1,), (1,)), ((), ()))
    for x_ref, pe_ref, w1_ref, w2_ref, o_ref, transposed in ((kx_ref, pek_ref, kw1_ref, kw2_ref, ko_ref, False),
                                                             (vx_ref, pev_ref, vw1_ref, vw2t_ref, vo_ref, True)):
        w1 = w1_ref[...]
        pe = jnp.broadcast_to(pe_ref[...], (SUBLANES, 2 * half))
        pe_proj = jnp.dot(pe, w1, preferred_element_type=F32, precision=HI)[0:1]
        for gg in range(KV_GROUPS):
            xg = x_ref[0, gg]
            first = jnp.dot(xg, w1[0:half], preferred_element_type=F32, precision=HI)
            second = jnp.dot(xg, w1[half:2 * half], preferred_element_type=F32, precision=HI)
            hid = first + pltpu.roll(second, nc - 1, 0) + pe_proj
            hid = hid * jax.nn.sigmoid(hid)
            if transposed:
                o_ref[0, gg] = lax.dot_general(w2_ref[...], hid, nt, preferred_element_type=F32, precision=HI)
            else:
                o_ref[0, gg] = jnp.dot(hid, w2_ref[...], preferred_element_type=F32, precision=HI)


def _compress(kx, vx, pek, pev, kw1, kw2, vw1, vw2t):
    B, G, NC, W = kx.shape
    xs = pl.BlockSpec((1, G, NC, W), lambda b: (b, 0, 0, 0))
    full = lambda a: pl.BlockSpec(a.shape, lambda b: (0,) * a.ndim)
    return pl.pallas_call(
        _cmp_kernel,
        grid=(B,),
        in_specs=[xs, xs, full(pek), full(pev), full(kw1), full(kw2), full(vw1), full(vw2t)],
        out_specs=[pl.BlockSpec((1, G, NC, HEAD_DIM), lambda b: (b, 0, 0, 0)),
                   pl.BlockSpec((1, G, HEAD_DIM, NC), lambda b: (b, 0, 0, 0))],
        out_shape=[jax.ShapeDtypeStruct((B, G, NC, HEAD_DIM), F32),
                   jax.ShapeDtypeStruct((B, G, HEAD_DIM, NC), F32)],
        compiler_params=pltpu.CompilerParams(dimension_semantics=("parallel",)),
        name="compress",
    )(kx, vx, pek, pev, kw1, kw2, vw1, vw2t)


def _attn_kernel(qt_ref, kc_ref, vct_ref, ks_ref, vst_ref, kw_ref, vwt_ref, gt_ref, ovt_ref, o_ref, sel_ref):
    tq, tk = ATTN_TQ, ATTN_TK
    R = Q_PER_G * tq
    per_tile = tk // SEL_BLK
    S = ks_ref.shape[2]
    ncp = kc_ref.shape[2]
    nsel = ovt_ref.shape[0]
    qi = pl.program_id(2)
    q0 = qi * tq
    qt = qt_ref[0, 0, 0]
    t_row = q0 + lax.broadcasted_iota(I32, (1, tq), 1)
    heads = lambda a: jnp.concatenate([a] * Q_PER_G, axis=1)

    sc = jnp.dot(kc_ref[0, 0].astype(BF16), qt, preferred_element_type=F32)
    c_io = lax.broadcasted_iota(I32, (ncp, tq), 0)
    m_c = (c_io * CMP_STRIDE + (CMP_LEN - 1) <= t_row) & (c_io < ncp - 1)
    scb = sc + heads(jnp.where(m_c, 0.0, NEG_INF))
    e = jnp.exp2(scb - jnp.max(scb, axis=0, keepdims=True)) * heads(jnp.where(m_c, 1.0, 0.0))
    den = jnp.sum(e, axis=0, keepdims=True)
    pc = e / jnp.where(den > 0.0, den, 1.0)
    o_cmp = jnp.dot(vct_ref[0, 0].astype(BF16), pc.astype(BF16), preferred_element_type=F32)

    psum = pc[:, 0:tq]
    for n in range(1, Q_PER_G):
        psum = psum + pc[:, n * tq:(n + 1) * tq]
    imp = jnp.dot(ovt_ref[...], psum, preferred_element_type=F32, precision=HI)
    j_io = lax.broadcasted_iota(I32, (nsel, tq), 0)
    cur = t_row // SEL_BLK
    valid = j_io * SEL_BLK <= t_row
    forced = (j_io == 0) | (j_io == cur) | (j_io == cur - 1)
    score = jnp.where(valid, jnp.where(forced, FORCE_SCORE, imp), NEG_INF)
    rank = jnp.zeros((nsel, tq), F32)
    for i in range(nsel):
        row = score[i:i + 1, :]
        tie = jnp.where(j_io > i, 1.0, 0.0)
        rank = rank + jnp.where(row > score, 1.0, jnp.where(row == score, tie, 0.0))
    sel_bias = jnp.where(rank < float(min(SEL_TOPN, nsel)), 0.0, NEG_INF)
    sel_ref[...] = jnp.zeros(sel_ref.shape, F32)
    for jj in range(S // tk):
        sel_ref[jj, 0:per_tile, :] = sel_bias[jj * per_tile:(jj + 1) * per_tile, :]

    k_io = lax.broadcasted_iota(I32, (tk, tq), 0)

    def flash_step(k_ref, vt_ref, kj, bias, carry):
        m, l, acc = carry
        k0 = pl.multiple_of(kj * tk, tk)
        s = jnp.dot(k_ref[0, 0, pl.ds(k0, tk), :], qt, preferred_element_type=F32) + heads(bias)
        m_new = jnp.maximum(m, jnp.max(s, axis=0, keepdims=True))
        alpha = jnp.exp2(m - m_new)
        p = jnp.exp2(s - m_new)
        l = alpha * l + jnp.sum(p, axis=0, keepdims=True)
        acc = alpha * acc + jnp.dot(vt_ref[0, 0, kj], p.astype(BF16), preferred_element_type=F32)
        return m_new, l, acc

    init = (jnp.full((1, R), NEG_INF, F32), jnp.zeros((1, R), F32), jnp.zeros((HEAD_DIM, R), F32))

    def slc_body(kj, carry):
        blocks = sel_ref[kj]
        bias = jnp.concatenate([jnp.broadcast_to(blocks[b:b + 1, :], (SEL_BLK, tq)) for b in range(per_tile)], axis=0)
        bias = jnp.where(kj * tk + k_io <= t_row, bias, NEG_INF)
        return flash_step(ks_ref, vst_ref, kj, bias, carry)

    def win_body(kj, carry):
        rel = t_row - (kj * tk + k_io)
        bias = jnp.where((rel >= 0) & (rel < WINDOW), 0.0, NEG_INF)
        return flash_step(kw_ref, vwt_ref, kj, bias, carry)

    n_slc = (q0 + tq + tk - 1) // tk
    lo_tile = jnp.maximum(q0 - (WINDOW - 1), 0) // tk
    n_pair = lo_tile // 2

    def pair_body(i, carry):
        return slc_body(2 * i, carry[0]), slc_body(2 * i + 1, carry[1])

    st_a, st_b = lax.fori_loop(0, n_pair, pair_body, (init, init))
    st_a = lax.fori_loop(2 * n_pair, lo_tile, slc_body, st_a)

    def both_body(kj, carry):
        return slc_body(kj, carry[0]), win_body(kj, carry[1])

    st_a, (_, l_w, acc_w) = lax.fori_loop(lo_tile, n_slc, both_body, (st_a, init))
    m_s = jnp.maximum(st_a[0], st_b[0])
    w_a, w_b = jnp.exp2(st_a[0] - m_s), jnp.exp2(st_b[0] - m_s)
    l_s = w_a * st_a[1] + w_b * st_b[1]
    acc_s = w_a * st_a[2] + w_b * st_b[2]


    gt = gt_ref[0, 0]
    o_slc = acc_s / l_s
    o_win = acc_w / l_w
    outs = []
    for n in range(Q_PER_G):
        cols = slice(n * tq, (n + 1) * tq)
        outs.append(gt[3 * n:3 * n + 1, :] * o_cmp[:, cols] + gt[3 * n + 1:3 * n + 2, :] * o_slc[:, cols]
                    + gt[3 * n + 2:3 * n + 3, :] * o_win[:, cols])
    o_ref[0] = jnp.concatenate(outs, axis=0).T


def _attention(qt, kc, vct, ks, vst, kw, vwt, gt, overlap_t):
    B, G, NQT, _, R = qt.shape
    S = ks.shape[2]
    tq, tk = ATTN_TQ, ATTN_TK
    ncp = kc.shape[2]
    kfull = pl.BlockSpec((1, 1, S, HEAD_DIM), lambda b, g, i: (b, g, 0, 0))
    vfull = pl.BlockSpec((1, 1, S // tk, HEAD_DIM, tk), lambda b, g, i: (b, g, 0, 0, 0))
    return pl.pallas_call(
        _attn_kernel,
        grid=(B, G, NQT),
        in_specs=[pl.BlockSpec((1, 1, 1, HEAD_DIM, R), lambda b, g, i: (b, g, i, 0, 0)),
                  pl.BlockSpec((1, 1, ncp, HEAD_DIM), lambda b, g, i: (b, g, 0, 0)),
                  pl.BlockSpec((1, 1, HEAD_DIM, ncp), lambda b, g, i: (b, g, 0, 0)),
                  kfull, vfull, kfull, vfull,
                  pl.BlockSpec((1, 1, GATE_ROWS, tq), lambda b, g, i: (b, g, 0, i)),
                  pl.BlockSpec(overlap_t.shape, lambda b, g, i: (0, 0))],
        out_specs=pl.BlockSpec((1, tq, Q_PER_G * HEAD_DIM), lambda b, g, i: (b, i, g)),
        out_shape=jax.ShapeDtypeStruct((B, S, NSA_WIDTH), F32),
        scratch_shapes=[pltpu.VMEM((S // tk, SUBLANES, tq), F32)],
        compiler_params=pltpu.CompilerParams(dimension_semantics=("parallel", "parallel", "arbitrary")),
        name="attn",
    )(qt, kc, vct, ks, vst, kw, vwt, gt, overlap_t)


def _outproj_kernel(x_ref, cv_ref, nsa_ref, on_ref, w_ref, gt1_ref, g2_ref, sc2_ref, sh2_ref, wrh_ref, wrl_ref, br_ref,
                    tri_ref, x1_ref, h2_ref, rt_ref, cnt_ref, run_ref):
    tm = x_ref.shape[1]

    @pl.when((pl.program_id(0) == 0) & (pl.program_id(1) == 0))
    def _():
        run_ref[...] = jnp.zeros(run_ref.shape, F32)

    nn = _rms(nsa_ref[0], on_ref[...]).astype(BF16)
    y = jnp.dot(jnp.concatenate([cv_ref[0], nn], axis=1), w_ref[...], preferred_element_type=F32)
    x1 = x_ref[0] + gt1_ref[0] * y
    x1_ref[0] = x1
    h2 = _rms(x1, g2_ref[...]) * (1.0 + sc2_ref[0]) + sh2_ref[0]
    for s in range(ROW_TILES):
        h2_ref[pl.ds(s, tm, stride=ROW_TILES), :] = h2[:, s * LANES:(s + 1) * LANES]
    nt = (((1,), (1,)), ((), ()))
    h_hi = h2.astype(BF16)
    h_lo = (h2 - h_hi.astype(F32)).astype(BF16)
    logits = (lax.dot_general(wrh_ref[...], h_hi, nt, preferred_element_type=F32)
              + lax.dot_general(wrh_ref[...], h_lo, nt, preferred_element_type=F32)
              + lax.dot_general(wrl_ref[...], h_hi, nt, preferred_element_type=F32)) + br_ref[...]
    eio = lax.broadcasted_iota(I32, (N_EXPERTS, tm), 0).astype(F32)
    vals, idxs = [], []
    for _ in range(TOP_K):
        m = jnp.max(logits, axis=0, keepdims=True)
        ix = jnp.min(jnp.where(logits == m, eio, float(N_EXPERTS)), axis=0, keepdims=True)
        vals.append(m)
        idxs.append(ix)
        logits = jnp.where(eio == ix, -jnp.inf, logits)
    es = [jnp.exp(v - vals[0]) for v in vals]
    den = es[0] + es[1] + es[2] + es[3]
    hot = jnp.zeros((N_EXPERTS, tm), F32)
    for r in range(TOP_K):
        hot = hot + jnp.where(eio == idxs[r], 1.0, 0.0)
    before = run_ref[...] + jnp.dot(hot.astype(BF16), tri_ref[...], preferred_element_type=F32)
    ranks = [jnp.sum(jnp.where(eio == idxs[r], before, 0.0), axis=0, keepdims=True) for r in range(TOP_K)]
    run_ref[...] = run_ref[...] + jnp.sum(hot, axis=1, keepdims=True)
    cnt_ref[...] = run_ref[...]
    rio = lax.broadcasted_iota(I32, (ROUTE_W, tm), 0)
    out = jnp.zeros((ROUTE_W, tm), F32)
    for r in range(TOP_K):
        out = jnp.where(rio == r, idxs[r], out)
        out = jnp.where(rio == TOP_K + r, es[r] / den, out)
        out = jnp.where(rio == 2 * TOP_K + r, ranks[r], out)
    rt_ref[...] = out


def _outproj(x, cv, nsa, on, w, gt1, g2, sc2, sh2, wr, br):
    B, S, D = x.shape
    tm = min(OUT_TM, S)
    nt = S // tm
    tri = (jnp.arange(tm)[:, None] < jnp.arange(tm)[None, :]).astype(BF16)
    wr_t = wr.T
    wr_hi = wr_t.astype(BF16)
    wr_lo = (wr_t - wr_hi.astype(F32)).astype(BF16)
    row = pl.BlockSpec((1, 1, D), lambda b, i: (b, 0, 0))
    vec = lambda n: pl.BlockSpec((1, n), lambda b, i: (0, 0))
    col = pl.BlockSpec((N_EXPERTS, 1), lambda b, i: (0, 0))
    wr_spec = pl.BlockSpec((N_EXPERTS, D), lambda b, i: (0, 0))
    return pl.pallas_call(
        _outproj_kernel,
        grid=(B, nt),
        in_specs=[pl.BlockSpec((1, tm, D), lambda b, i: (b, i, 0)),
                  pl.BlockSpec((1, tm, CONV_CH), lambda b, i: (b, i, 0)),
                  pl.BlockSpec((1, tm, NSA_WIDTH), lambda b, i: (b, i, 0)),
                  vec(NSA_WIDTH),
                  pl.BlockSpec((D, D), lambda b, i: (0, 0)),
                  row, vec(D), row, row,
                  wr_spec, wr_spec, col,
                  pl.BlockSpec((tm, tm), lambda b, i: (0, 0))],
        out_specs=[pl.BlockSpec((1, tm, D), lambda b, i: (b, i, 0)),
                   pl.BlockSpec((tm * ROW_TILES, LANES), lambda b, i: (b * nt + i, 0)),
                   pl.BlockSpec((ROUTE_W, tm), lambda b, i: (0, b * nt + i)),
                   col],
        out_shape=[jax.ShapeDtypeStruct((B, S, D), F32),
                   jax.ShapeDtypeStruct((B * S * ROW_TILES, LANES), F32),
                   jax.ShapeDtypeStruct((ROUTE_W, B * S), F32),
                   jax.ShapeDtypeStruct((N_EXPERTS, 1), F32)],
        scratch_shapes=[pltpu.VMEM((N_EXPERTS, 1), F32)],
        compiler_params=pltpu.CompilerParams(dimension_semantics=("arbitrary", "arbitrary")),
        name="outproj",
    )(x, cv, nsa, on, w, gt1, g2, sc2, sh2, wr_hi, wr_lo, br.reshape(N_EXPERTS, 1), tri)


def _issue_rows(idx_ref, rows, src_hbm, dst, slot, sem):
    for r in rows:
        pltpu.make_async_copy(src_hbm.at[idx_ref[0, 0, r]],
                              dst.at[slot, pl.ds(r * ROW_TILES, ROW_TILES), :],
                              sem.at[slot]).start(priority=r % 2)


def _wait_rows(dst, slot, sem):
    pltpu.make_async_copy(dst.at[slot], dst.at[slot], sem.at[slot]).wait()


def _rows_2d(buf, slot, base, n):
    return jnp.concatenate(
        [buf[slot, pl.ds(base * ROW_TILES + s, n, stride=ROW_TILES), :] for s in range(ROW_TILES)], axis=1)


def _ffn_kernel(be_ref, nb_ref, ta_ref, tb_ref, h2_hbm, gate_ref, wg_ref, bg_ref, wu_ref, bu_ref, wd_ref, bd_ref,
                o_ref, xbuf, wgb, wub, wdb, sem):
    bm = ta_ref.shape[2]
    i = pl.program_id(0)
    nb = nb_ref[0]
    slot = i % 2

    @pl.when((i == 0) | (be_ref[i] != be_ref[jnp.maximum(i - 1, 0)]))
    def _():
        wgb[...] = wg_ref[0].astype(BF16)
        wub[...] = wu_ref[0].astype(BF16)
        wdb[...] = wd_ref[0].astype(BF16)

    @pl.when(i == 0)
    def _():
        _issue_rows(ta_ref, range(bm), h2_hbm, xbuf, 0, sem)

    @pl.when(i < nb)
    def _():
        _wait_rows(xbuf, slot, sem)
        x = _rows_2d(xbuf, slot, 0, bm).astype(BF16)
        fc = D_FF // FFN_CHUNKS
        burst = bm // FFN_CHUNKS
        y = None
        for c in range(FFN_CHUNKS):
            cols = slice(c * fc, (c + 1) * fc)
            g = jnp.dot(x, wgb[:, cols], preferred_element_type=F32) + bg_ref[0][:, cols]
            u = jnp.dot(x, wub[:, cols], preferred_element_type=F32) + bu_ref[0][:, cols]
            g = jnp.minimum(g, SWIGLU_LIMIT)
            u = jnp.clip(u, -SWIGLU_LIMIT, SWIGLU_LIMIT)
            act = g * jax.nn.sigmoid(SWIGLU_ALPHA * g) * (u + 1.0)
            part = jnp.dot(act.astype(BF16), wdb[cols, :], preferred_element_type=F32)
            y = part if y is None else y + part
            _issue_rows(tb_ref, range(c * burst, (c + 1) * burst), h2_hbm, xbuf, 1 - slot, sem)
        y = (y + bd_ref[0]) * gate_ref[...]
        for s in range(ROW_TILES):
            o_ref[pl.ds(s, bm, stride=ROW_TILES), :] = y[:, s * LANES:(s + 1) * LANES]

        @pl.when(i + 1 >= nb)
        def _():
            _wait_rows(xbuf, 1 - slot, sem)

    @pl.when(i >= nb)
    def _():
        o_ref[...] = jnp.zeros(o_ref.shape, o_ref.dtype)


def _ffn(block_e, nb_used, buf_tok, h2_rows, buf_gate, wg, bg, wu, bu, wd, bd):
    NB = block_e.shape[0]
    bm = FFN_BM
    D, F = D_MODEL, D_FF
    tok3 = buf_tok.reshape(NB, 1, bm)
    wspec = lambda r, c: pl.BlockSpec((1, r, c), lambda i, be, nb: (be[i], 0, 0))
    vmem_limit = 2 * 3 * D * F * 4 + 3 * D * F * 2 + 4 * bm * D * 4 + 6 * bm * F * 4
    return pl.pallas_call(
        _ffn_kernel,
        grid_spec=pltpu.PrefetchScalarGridSpec(
            num_scalar_prefetch=2,
            grid=(NB,),
            in_specs=[pl.BlockSpec((1, 1, bm), lambda i, be, nb: (i, 0, 0), memory_space=pltpu.SMEM),
                      pl.BlockSpec((1, 1, bm), lambda i, be, nb: (jnp.minimum(i + 1, NB - 1), 0, 0),
                                   memory_space=pltpu.SMEM),
                      pl.BlockSpec(memory_space=pl.ANY),
                      pl.BlockSpec((bm, 1), lambda i, be, nb: (i, 0)),
                      wspec(D, F), wspec(1, F), wspec(D, F), wspec(1, F), wspec(F, D), wspec(1, D)],
            out_specs=pl.BlockSpec((bm * ROW_TILES, LANES), lambda i, be, nb: (i, 0)),
            scratch_shapes=[pltpu.VMEM((2, bm * ROW_TILES, LANES), F32),
                            pltpu.VMEM((D, F), BF16), pltpu.VMEM((D, F), BF16), pltpu.VMEM((F, D), BF16),
                            pltpu.SemaphoreType.DMA((2,))]),
        out_shape=jax.ShapeDtypeStruct((NB * bm * ROW_TILES, LANES), F32),
        compiler_params=pltpu.CompilerParams(dimension_semantics=("arbitrary",),
                                             vmem_limit_bytes=vmem_limit),
        name="ffn",
    )(block_e, nb_used, tok3, tok3, h2_rows, buf_gate, wg, bg, wu, bu, wd, bd)


def _combine_kernel(da_ref, db_ref, y_hbm, x1_ref, gt2_ref, fg_ref, o_ref, buf, sem):
    tm = x1_ref.shape[0]
    n = TOP_K * tm
    i = pl.program_id(0)
    slot = i % 2

    @pl.when(i == 0)
    def _():
        _issue_rows(da_ref, range(n), y_hbm, buf, 0, sem)

    @pl.when(i + 1 < pl.num_programs(0))
    def _():
        _issue_rows(db_ref, range(n), y_hbm, buf, 1 - slot, sem)

    _wait_rows(buf, slot, sem)
    y = _rows_2d(buf, slot, 0, tm)
    for k in range(1, TOP_K):
        y = y + _rows_2d(buf, slot, k * tm, tm)
    x2 = x1_ref[...] + gt2_ref[0] * y
    o_ref[...] = _rms(x2, fg_ref[...])


def _combine(dest3, y_rows, x1, gt2, fg, S):
    T, D = x1.shape
    tm = min(COMB_TM, S)
    NT = T // tm
    per_b = S // tm
    n = TOP_K * tm
    return pl.pallas_call(
        _combine_kernel,
        grid=(NT,),
        in_specs=[pl.BlockSpec((1, 1, n), lambda i: (i, 0, 0), memory_space=pltpu.SMEM),
                  pl.BlockSpec((1, 1, n), lambda i: (jnp.minimum(i + 1, NT - 1), 0, 0), memory_space=pltpu.SMEM),
                  pl.BlockSpec(memory_space=pl.ANY),
                  pl.BlockSpec((tm, D), lambda i: (i, 0)),
                  pl.BlockSpec((1, 1, D), lambda i: (i // per_b, 0, 0)),
                  pl.BlockSpec((1, D), lambda i: (0, 0))],
        out_specs=pl.BlockSpec((tm, D), lambda i: (i, 0)),
        out_shape=jax.ShapeDtypeStruct((T, D), F32),
        scratch_shapes=[pltpu.VMEM((2, n * ROW_TILES, LANES), F32), pltpu.SemaphoreType.DMA((2,))],
        compiler_params=pltpu.CompilerParams(dimension_semantics=("arbitrary",)),
        name="combine",
    )(dest3, dest3, y_rows, x1, gt2, fg)


def _rope_tables(S):
    inv = ROPE_THETA ** (-jnp.arange(0, ROT_DIM, 2, dtype=F32) / ROT_DIM)
    ang = jnp.arange(S, dtype=F32)[:, None] * inv[None, :]
    cos, sin = jnp.cos(ang), jnp.sin(ang)
    d = jnp.arange(KV_W) % HEAD_DIM
    first, second = d < ROT_HALF, (d >= ROT_HALF) & (d < ROT_DIM)
    cos_l = cos[:, d % ROT_HALF]
    sin_l = sin[:, d % ROT_HALF]
    rc = jnp.where((d < ROT_DIM)[None], cos_l, 1.0)
    rs1 = jnp.where(second[None], sin_l, 0.0)
    rs2 = jnp.where(first[None], -sin_l, 0.0)
    return rc, rs1, rs2, cos.T, sin.T


def _route_plan(route, counts, T):
    bm = FFN_BM
    A = T * TOP_K
    idx = route[0:TOP_K].astype(I32)
    gate = route[TOP_K:2 * TOP_K]
    rank = route[2 * TOP_K:3 * TOP_K].astype(I32)
    keys = idx * T + jnp.arange(T, dtype=I32)[None, :]
    skey, sgate = lax.sort((keys.reshape(A), gate.reshape(A)), num_keys=1)
    counts = counts.astype(I32)
    starts = jnp.cumsum(counts) - counts
    padded = (counts + bm - 1) // bm * bm
    pends = jnp.cumsum(padded)
    pstarts = pends - padded
    P = (A + N_EXPERTS * bm + bm - 1) // bm * bm
    NB = P // bm
    blk0 = jnp.arange(NB, dtype=I32) * bm
    block_e = jnp.minimum(jnp.sum((pends[None, :] <= blk0[:, None]).astype(I32), axis=1), N_EXPERTS - 1)
    r = (blk0 - pstarts[block_e])[:, None] + jnp.arange(bm, dtype=I32)[None, :]
    valid = r < counts[block_e][:, None]
    src = jnp.clip(starts[block_e][:, None] + r, 0, A - 1)
    buf_tok = jnp.where(valid, skey[src] - block_e[:, None] * T, 0)
    buf_gate = jnp.where(valid, sgate[src], 0.0).reshape(P, 1)
    dest = rank
    for e in range(N_EXPERTS):
        dest = dest + jnp.where(idx == e, pstarts[e], 0)
    nb_used = (pends[-1] // bm).astype(I32).reshape(1)
    return block_e, nb_used, buf_tok, buf_gate, dest


def kernel(x, c, norm1_g, norm2_g, w_ada, b_ada, w_in, conv_w, conv_b, conv_ln_g, conv_ln_b, cmp_pe_k, cmp_pe_v,
           cmp_k_w1, cmp_k_w2, cmp_v_w1, cmp_v_w2, out_norm_conv, out_norm_nsa, w_out, w_router, b_router,
           w_gate, b_gate, w_up, b_up, w_down, b_down, final_norm_g):
    B, S, D = x.shape
    T = B * S
    G = KV_GROUPS
    assert D == D_MODEL and S % ATTN_TK == 0 and S % CMP_STRIDE == 0 and KV_W == LANES
    rc, rs1, rs2, cos_t, sin_t = _rope_tables(S)
    n_sel = S // SEL_BLK
    nc = S // CMP_STRIDE
    cstart = jnp.arange(nc) * CMP_STRIDE
    jstart = jnp.arange(n_sel) * SEL_BLK
    overlap_t = ((cstart[None, :] <= jstart[:, None] + SEL_BLK - 1)
                 & (cstart[None, :] + CMP_LEN - 1 >= jstart[:, None])
                 & (jnp.arange(nc)[None, :] < nc - 1)).astype(F32)

    assert w_ada.shape[0] == 1
    for l in range(1):
        mod = _adaln(c, w_ada[l], b_ada[l][None])
        sh1, sc1, gt1, sh2, sc2, gt2 = [m[:, None, :] for m in jnp.split(mod, 6, axis=-1)]

        wl = w_in[l]
        o = 2 * CONV_CH + NSA_WIDTH
        kvc = [wl[:, o + i * KV_W:o + (i + 1) * KV_W] for i in range(6)]
        gl = wl[:, o + 6 * KV_W:]
        per_g = 3 * Q_PER_G
        gpad = [jnp.pad(gl[:, per_g * g:per_g * (g + 1)], ((0, 0), (0, GATE_ROWS - per_g))) for g in range(G)]
        wn = jnp.concatenate([wl[:, :2 * CONV_CH], kvc[0], kvc[2], kvc[4], kvc[1]], axis=1).astype(BF16)
        wt = jnp.concatenate([wl[:, 2 * CONV_CH:o], kvc[3], kvc[5]] + gpad, axis=1).T.astype(BF16)
        u, kc, vc, ks, kw, qt, vst, vwt, gates = _inproj(x, sc1, sh1, norm1_g[l][None], wn, wt,
                                                         rc, rs1, rs2, cos_t, sin_t)

        conv_n = _conv(u, conv_w[l], conv_b[l][None], conv_ln_g[l][None], conv_ln_b[l][None],
                       out_norm_conv[l][None])

        chunk = CMP_STRIDE * HEAD_DIM
        kcmp, vcmp_t = _compress(kc.reshape(B, G, nc, chunk), vc.reshape(B, G, nc, chunk),
                                 cmp_pe_k[l].reshape(1, -1), cmp_pe_v[l].reshape(1, -1),
                                 cmp_k_w1[l], cmp_k_w2[l], cmp_v_w1[l], cmp_v_w2[l].T)
        nsa = _attention(qt, kcmp, vcmp_t, ks, vst, kw, vwt, gates, overlap_t)

        x1, h2_rows, route, counts = _outproj(x, conv_n, nsa, out_norm_nsa[l][None], w_out[l].astype(BF16), gt1,
                                              norm2_g[l][None], sc2, sh2, w_router[l], b_router[l][None])

        block_e, nb_used, buf_tok, buf_gate, dest = _route_plan(route, counts[:, 0], T)
        y_rows = _ffn(block_e, nb_used, buf_tok, h2_rows.reshape(T, ROW_TILES, LANES), buf_gate,
                      w_gate[l], b_gate[l][:, None, :], w_up[l], b_up[l][:, None, :],
                      w_down[l], b_down[l][:, None, :])
        tm = min(COMB_TM, S)
        dest3 = dest.reshape(TOP_K, T // tm, tm).transpose(1, 0, 2).reshape(T // tm, 1, TOP_K * tm)
        P = y_rows.shape[0] // ROW_TILES
        x = _combine(dest3, y_rows.reshape(P, ROW_TILES, LANES), x1.reshape(T, D), gt2, final_norm_g[None],
                     S).reshape(B, S, D)
    return x
```

```python
import functools

import jax
import jax.numpy as jnp
from jax import lax
from jax.experimental import pallas as pl
from jax.experimental.pallas import tpu as pltpu

F32 = jnp.float32
BF16 = jnp.bfloat16
I32 = jnp.int32
HI = lax.Precision.HIGHEST

D_MODEL = 1024
CONV_CH = 512
CONV_WIDTH = 31
NSA_HEADS = 8
KV_GROUPS = 2
Q_PER_G = NSA_HEADS // KV_GROUPS
HEAD_DIM = 64
NSA_WIDTH = NSA_HEADS * HEAD_DIM
KV_W = KV_GROUPS * HEAD_DIM
ROT_DIM = HEAD_DIM // 4
ROT_HALF = ROT_DIM // 2
ROPE_THETA = 500000.0
CMP_LEN = 32
CMP_STRIDE = 16
CMP_HIDDEN = 128
SEL_BLK = 64
SEL_TOPN = 16
WINDOW = 512
N_EXPERTS = 32
TOP_K = 4
D_FF = 1024
SWIGLU_ALPHA = 1.702
SWIGLU_LIMIT = 7.0
NORM_EPS = 1e-5
NEG_INF = -1e30
FORCE_SCORE = 1e9
LOG2_E = 1.4426950408889634

LANES = 128
SUBLANES = 8
ROW_TILES = D_MODEL // LANES

GATE_ROWS = 16

INPROJ_TM = 512
CONV_TR = 256
CONV_HALO = 32
ATTN_TQ = 256
ATTN_TK = 256
OUT_TM = 512
FFN_BM = 512
COMB_TM = 256
ROUTE_W = 16


def _rms(x, g):
    return x * lax.rsqrt(jnp.mean(x * x, axis=-1, keepdims=True) + NORM_EPS) * g


def _split_bf16(a):
    hi = a.astype(BF16)
    return hi, (a - hi.astype(F32)).astype(BF16)


def _dot3(a, b, dims=(((1,), (0,)), ((), ()))):
    ah, al = _split_bf16(a)
    bh, bl = _split_bf16(b)
    d = lambda x, y: lax.dot_general(x, y, dims, preferred_element_type=F32)
    return d(ah, bh) + d(al, bh) + d(ah, bl)


def _ada_kernel(c_ref, w_ref, b_ref, o_ref):
    c = c_ref[...]
    ca = c * jax.nn.sigmoid(c)
    o_ref[...] = jnp.dot(ca, w_ref[...], preferred_element_type=F32, precision=HI) + b_ref[...]


def _adaln(c, w, b):
    B = c.shape[0]
    D = D_MODEL
    return pl.pallas_call(
        _ada_kernel,
        grid=(6,),
        in_specs=[pl.BlockSpec((B, D), lambda j: (0, 0)),
                  pl.BlockSpec((D, D), lambda j: (0, j)),
                  pl.BlockSpec((1, D), lambda j: (0, j))],
        out_specs=pl.BlockSpec((B, D), lambda j: (0, j)),
        out_shape=jax.ShapeDtypeStruct((B, 6 * D), F32),
        name="adaln",
    )(c, w, b)


def _inproj_kernel(x_ref, sc_ref, sh_ref, g_ref, wn_ref, wt_ref, rc_ref, rs1_ref, rs2_ref, ct_ref, st_ref,
                   u_ref, kc_ref, vc_ref, ks_ref, kw_ref, qt_ref, vst_ref, vwt_ref, gt_ref, stage_ref):
    tm = x_ref.shape[1]
    tq, tk = ATTN_TQ, ATTN_TK
    h = (_rms(x_ref[0], g_ref[...]) * (1.0 + sc_ref[0]) + sh_ref[0]).astype(BF16)

    p = jnp.dot(h, wn_ref[...], preferred_element_type=F32)
    u_ref[0] = p[:, 0:CONV_CH] * jax.nn.sigmoid(p[:, CONV_CH:2 * CONV_CH])
    c0 = 2 * CONV_CH
    rc, rs1, rs2 = rc_ref[...], rs1_ref[...], rs2_ref[...]
    for ref, roped in ((kc_ref, True), (ks_ref, True), (kw_ref, True), (vc_ref, False)):
        v = p[:, c0:c0 + KV_W]
        if roped:
            v = v * rc + pltpu.roll(v, ROT_HALF, 1) * rs1 + pltpu.roll(v, KV_W - ROT_HALF, 1) * rs2
        c0 += KV_W
        if ref is kc_ref or ref is vc_ref:
            stage_ref[...] = v
            left = lax.broadcasted_iota(I32, (tm // CMP_STRIDE, KV_W), 1) < HEAD_DIM
            pieces = [stage_ref[pl.ds(tl, tm // CMP_STRIDE, stride=CMP_STRIDE), :] for tl in range(CMP_STRIDE)]
            for gg in range(KV_GROUPS):
                cols = []
                for tl in range(0, CMP_STRIDE, 2):
                    a, b = pieces[tl], pieces[tl + 1]
                    if gg == 0:
                        cols.append(jnp.where(left, a, pltpu.roll(b, HEAD_DIM, 1)))
                    else:
                        cols.append(jnp.where(left, pltpu.roll(a, HEAD_DIM, 1), b))
                ref[0, gg] = jnp.concatenate(cols, axis=1)
            continue
        for gg in range(KV_GROUPS):
            ref[0, gg] = v[:, HEAD_DIM * gg:HEAD_DIM * (gg + 1)].astype(ref.dtype)

    pt = lax.dot_general(wt_ref[...], h, (((1,), (1,)), ((), ())), preferred_element_type=F32)
    cos_t, sin_t = ct_ref[...], st_ref[...]
    scale = HEAD_DIM ** -0.5 * LOG2_E
    for hh in range(NSA_HEADS):
        blk = pt[HEAD_DIM * hh:HEAD_DIM * (hh + 1), :]
        x1, x2 = blk[0:ROT_HALF], blk[ROT_HALF:ROT_DIM]
        qh = (jnp.concatenate([x1 * cos_t - x2 * sin_t, x2 * cos_t + x1 * sin_t, blk[ROT_DIM:]], axis=0)
              * scale).astype(BF16)
        gg, n = divmod(hh, Q_PER_G)
        for j in range(tm // tq):
            qt_ref[0, gg, j, :, n * tq:(n + 1) * tq] = qh[:, j * tq:(j + 1) * tq]
    r0 = NSA_WIDTH
    for ref in (vst_ref, vwt_ref):
        for gg in range(KV_GROUPS):
            blk = pt[r0 + HEAD_DIM * gg:r0 + HEAD_DIM * (gg + 1), :].astype(BF16)
            for j in range(tm // tk):
                ref[0, gg, j] = blk[:, j * tk:(j + 1) * tk]
        r0 += KV_W
    for gg in range(KV_GROUPS):
        gt_ref[0, gg] = jax.nn.sigmoid(pt[r0 + GATE_ROWS * gg:r0 + GATE_ROWS * (gg + 1), :])


def _inproj(x, sc, sh, g, wn, wt, rc, rs1, rs2, cos_t, sin_t):
    B, S, D = x.shape
    tm = min(INPROJ_TM, S)
    tq, tk = ATTN_TQ, ATTN_TK
    G = KV_GROUPS
    kv = lambda dt: jax.ShapeDtypeStruct((B, G, S, HEAD_DIM), dt)
    kv_spec = pl.BlockSpec((1, G, tm, HEAD_DIM), lambda b, i: (b, 0, i, 0))
    chunk = CMP_STRIDE * HEAD_DIM
    ck_shape = jax.ShapeDtypeStruct((B, G, S // CMP_STRIDE, chunk), F32)
    ck_spec = pl.BlockSpec((1, G, tm // CMP_STRIDE, chunk), lambda b, i: (b, 0, i, 0))
    vt_shape = jax.ShapeDtypeStruct((B, G, S // tk, HEAD_DIM, tk), BF16)
    vt_spec = pl.BlockSpec((1, G, tm // tk, HEAD_DIM, tk), lambda b, i: (b, 0, i, 0, 0))
    row = pl.BlockSpec((1, 1, D), lambda b, i: (b, 0, 0))
    tab = pl.BlockSpec((tm, LANES), lambda b, i: (i, 0))
    tab_t = pl.BlockSpec((ROT_HALF, tm), lambda b, i: (0, i))
    return pl.pallas_call(
        _inproj_kernel,
        grid=(B, S // tm),
        in_specs=[pl.BlockSpec((1, tm, D), lambda b, i: (b, i, 0)), row, row,
                  pl.BlockSpec((1, D), lambda b, i: (0, 0)),
                  pl.BlockSpec(wn.shape, lambda b, i: (0, 0)),
                  pl.BlockSpec(wt.shape, lambda b, i: (0, 0)),
                  tab, tab, tab, tab_t, tab_t],
        out_specs=[pl.BlockSpec((1, tm, CONV_CH), lambda b, i: (b, i, 0)),
                   ck_spec, ck_spec, kv_spec, kv_spec,
                   pl.BlockSpec((1, G, tm // tq, HEAD_DIM, Q_PER_G * tq), lambda b, i: (b, 0, i, 0, 0)),
                   vt_spec, vt_spec,
                   pl.BlockSpec((1, G, GATE_ROWS, tm), lambda b, i: (b, 0, 0, i))],
        out_shape=[jax.ShapeDtypeStruct((B, S, CONV_CH), F32),
                   ck_shape, ck_shape, kv(BF16), kv(BF16),
                   jax.ShapeDtypeStruct((B, G, S // tq, HEAD_DIM, Q_PER_G * tq), BF16),
                   vt_shape, vt_shape,
                   jax.ShapeDtypeStruct((B, G, GATE_ROWS, S), F32)],
        scratch_shapes=[pltpu.VMEM((tm, KV_W), F32)],
        compiler_params=pltpu.CompilerParams(dimension_semantics=("parallel", "parallel")),
        name="inproj",
    )(x, sc, sh, g, wn, wt, rc, rs1, rs2, cos_t, sin_t)


def _conv_kernel(prev_ref, cur_ref, w_ref, cb_ref, lg_ref, lb_ref, on_ref, o_ref, pad_ref, win_ref):
    tr = cur_ref.shape[1]
    first = pl.program_id(1) == 0
    halo = prev_ref[0, tr - CONV_HALO:tr, :]
    pad_ref[0:CONV_HALO, :] = jnp.where(first, 0.0, halo)
    pad_ref[CONV_HALO:CONV_HALO + tr, :] = cur_ref[0]
    off = CONV_HALO - (CONV_WIDTH - 1)
    acc = jnp.zeros((tr, CONV_CH), F32)
    for b in range(SUBLANES):
        taps = range(b, CONV_WIDTH, SUBLANES)
        rows = tr + SUBLANES * (len(taps) - 1)
        win_ref[b, 0:rows, :] = pad_ref[off + b:off + b + rows, :]
        for a, k in enumerate(taps):
            acc = acc + win_ref[b, SUBLANES * a:SUBLANES * a + tr, :] * w_ref[k:k + 1, :]
    y = acc + cb_ref[...]
    mu = jnp.mean(y, axis=-1, keepdims=True)
    yc = y - mu
    var = jnp.mean(yc * yc, axis=-1, keepdims=True)
    yn = yc * lax.rsqrt(var + NORM_EPS) * lg_ref[...] + lb_ref[...]
    s = yn * jax.nn.sigmoid(yn)
    o_ref[0] = _rms(s, on_ref[...]).astype(o_ref.dtype)


def _conv(u, w, cb, lg, lb, on):
    B, S, C = u.shape
    tr = min(CONV_TR, S)
    vec = pl.BlockSpec((1, C), lambda b, i: (0, 0))
    return pl.pallas_call(
        _conv_kernel,
        grid=(B, S // tr),
        in_specs=[pl.BlockSpec((1, tr, C), lambda b, i: (b, jnp.maximum(i - 1, 0), 0)),
                  pl.BlockSpec((1, tr, C), lambda b, i: (b, i, 0)),
                  pl.BlockSpec((CONV_WIDTH, C), lambda b, i: (0, 0)),
                  vec, vec, vec, vec],
        out_specs=pl.BlockSpec((1, tr, C), lambda b, i: (b, i, 0)),
        out_shape=jax.ShapeDtypeStruct((B, S, C), BF16),
        scratch_shapes=[pltpu.VMEM((CONV_HALO + tr, C), F32),
                        pltpu.VMEM((SUBLANES, tr + SUBLANES * ((CONV_WIDTH - 1) // SUBLANES), C), F32)],
        compiler_params=pltpu.CompilerParams(dimension_semantics=("parallel", "parallel")),
        name="conv",
    )(u, u, w, cb, lg, lb, on)


def _cmp_kernel(kx_ref, vx_ref, pek_ref, pev_ref, kw1_ref, kw2_ref, vw1_ref, vw2t_ref, ko_ref, vo_ref):
    nc = kx_ref.shape[2]
    half = kx_ref.shape[3]
    nt = (((1,), (1,)), ((), ()))
    for x_ref, pe_ref, w1_ref, w2_ref, o_ref, transposed in ((kx_ref, pek_ref, kw1_ref, kw2_ref, ko_ref, False),
                                                             (vx_ref, pev_ref, vw1_ref, vw2t_ref, vo_ref, True)):
        w1 = w1_ref[...]
        pe = jnp.broadcast_to(pe_ref[...], (SUBLANES, 2 * half))
        pe_proj = _dot3(pe, w1)[0:1]
        for gg in range(KV_GROUPS):
            xg = x_ref[0, gg]
            first = _dot3(xg, w1[0:half])
            second = _dot3(xg, w1[half:2 * half])
            hid = first + pltpu.roll(second, nc - 1, 0) + pe_proj
            hid = hid * jax.nn.sigmoid(hid)
            if transposed:
                o_ref[0, gg] = _dot3(w2_ref[...], hid, nt)
            else:
                o_ref[0, gg] = _dot3(hid, w2_ref[...])


def _compress(kx, vx, pek, pev, kw1, kw2, vw1, vw2t):
    B, G, NC, W = kx.shape
    xs = pl.BlockSpec((1, G, NC, W), lambda b: (b, 0, 0, 0))
    full = lambda a: pl.BlockSpec(a.shape, lambda b: (0,) * a.ndim)
    return pl.pallas_call(
        _cmp_kernel,
        grid=(B,),
        in_specs=[xs, xs, full(pek), full(pev), full(kw1), full(kw2), full(vw1), full(vw2t)],
        out_specs=[pl.BlockSpec((1, G, NC, HEAD_DIM), lambda b: (b, 0, 0, 0)),
                   pl.BlockSpec((1, G, HEAD_DIM, NC), lambda b: (b, 0, 0, 0))],
        out_shape=[jax.ShapeDtypeStruct((B, G, NC, HEAD_DIM), F32),
                   jax.ShapeDtypeStruct((B, G, HEAD_DIM, NC), F32)],
        compiler_params=pltpu.CompilerParams(dimension_semantics=("parallel",)),
        name="compress",
    )(kx, vx, pek, pev, kw1, kw2, vw1, vw2t)


def _attn_kernel(qt_ref, kc_ref, vct_ref, ks_ref, vst_ref, kw_ref, vwt_ref, gt_ref, ovt_ref, o_ref, sel_ref):
    tq, tk = ATTN_TQ, ATTN_TK
    R = Q_PER_G * tq
    per_tile = tk // SEL_BLK
    S = ks_ref.shape[2]
    ncp = kc_ref.shape[2]
    nsel = ovt_ref.shape[0]
    qi = pl.program_id(2)
    q0 = qi * tq
    qt = qt_ref[0, 0, 0]
    t_row = q0 + lax.broadcasted_iota(I32, (1, tq), 1)
    heads = lambda a: jnp.concatenate([a] * Q_PER_G, axis=1)

    sc = jnp.dot(kc_ref[0, 0].astype(BF16), qt, preferred_element_type=F32)
    c_io = lax.broadcasted_iota(I32, (ncp, tq), 0)
    m_c = (c_io * CMP_STRIDE + (CMP_LEN - 1) <= t_row) & (c_io < ncp - 1)
    scb = sc + heads(jnp.where(m_c, 0.0, NEG_INF))
    e = jnp.exp2(scb - jnp.max(scb, axis=0, keepdims=True)) * heads(jnp.where(m_c, 1.0, 0.0))
    den = jnp.sum(e, axis=0, keepdims=True)
    pc = e / jnp.where(den > 0.0, den, 1.0)
    o_cmp = jnp.dot(vct_ref[0, 0].astype(BF16), pc.astype(BF16), preferred_element_type=F32)

    psum = pc[:, 0:tq]
    for n in range(1, Q_PER_G):
        psum = psum + pc[:, n * tq:(n + 1) * tq]
    imp = _dot3(ovt_ref[...], psum)
    j_io = lax.broadcasted_iota(I32, (nsel, tq), 0)
    cur = t_row // SEL_BLK
    valid = j_io * SEL_BLK <= t_row
    forced = (j_io == 0) | (j_io == cur) | (j_io == cur - 1)
    score = jnp.where(valid, jnp.where(forced, FORCE_SCORE, imp), NEG_INF)
    rank = jnp.zeros((nsel, tq), F32)
    for i in range(nsel):
        row = score[i:i + 1, :]
        tie = jnp.where(j_io > i, 1.0, 0.0)
        rank = rank + jnp.where(row > score, 1.0, jnp.where(row == score, tie, 0.0))
    sel_bias = jnp.where(rank < float(min(SEL_TOPN, nsel)), 0.0, NEG_INF)
    sel_ref[...] = jnp.zeros(sel_ref.shape, F32)
    for jj in range(S // tk):
        sel_ref[jj, 0:per_tile, :] = sel_bias[jj * per_tile:(jj + 1) * per_tile, :]

    k_io = lax.broadcasted_iota(I32, (tk, tq), 0)

    def flash_step(k_ref, vt_ref, kj, bias, carry):
        m, l, acc = carry
        k0 = pl.multiple_of(kj * tk, tk)
        s = jnp.dot(k_ref[0, 0, pl.ds(k0, tk), :], qt, preferred_element_type=F32) + heads(bias)
        m_new = jnp.maximum(m, jnp.max(s, axis=0, keepdims=True))
        alpha = jnp.exp2(m - m_new)
        p = jnp.exp2(s - m_new)
        l = alpha * l + jnp.sum(p, axis=0, keepdims=True)
        acc = alpha * acc + jnp.dot(vt_ref[0, 0, kj], p.astype(BF16), preferred_element_type=F32)
        return m_new, l, acc

    init = (jnp.full((1, R), NEG_INF, F32), jnp.zeros((1, R), F32), jnp.zeros((HEAD_DIM, R), F32))

    def slc_body(kj, carry):
        blocks = sel_ref[kj]
        bias = jnp.concatenate([jnp.broadcast_to(blocks[b:b + 1, :], (SEL_BLK, tq)) for b in range(per_tile)], axis=0)
        bias = jnp.where(kj * tk + k_io <= t_row, bias, NEG_INF)
        return flash_step(ks_ref, vst_ref, kj, bias, carry)

    def win_body(kj, carry):
        rel = t_row - (kj * tk + k_io)
        bias = jnp.where((rel >= 0) & (rel < WINDOW), 0.0, NEG_INF)
        return flash_step(kw_ref, vwt_ref, kj, bias, carry)

    n_slc = (q0 + tq + tk - 1) // tk
    lo_tile = jnp.maximum(q0 - (WINDOW - 1), 0) // tk
    n_pair = lo_tile // 2

    def pair_body(i, carry):
        return slc_body(2 * i, carry[0]), slc_body(2 * i + 1, carry[1])

    st_a, st_b = lax.fori_loop(0, n_pair, pair_body, (init, init))
    st_a = lax.fori_loop(2 * n_pair, lo_tile, slc_body, st_a)

    def both_body(kj, carry):
        return slc_body(kj, carry[0]), win_body(kj, carry[1])

    st_a, (_, l_w, acc_w) = lax.fori_loop(lo_tile, n_slc, both_body, (st_a, init))
    m_s = jnp.maximum(st_a[0], st_b[0])
    w_a, w_b = jnp.exp2(st_a[0] - m_s), jnp.exp2(st_b[0] - m_s)
    l_s = w_a * st_a[1] + w_b * st_b[1]
    acc_s = w_a * st_a[2] + w_b * st_b[2]


    gt = gt_ref[0, 0]
    o_slc = acc_s / l_s
    o_win = acc_w / l_w
    outs = []
    for n in range(Q_PER_G):
        cols = slice(n * tq, (n + 1) * tq)
        outs.append(gt[3 * n:3 * n + 1, :] * o_cmp[:, cols] + gt[3 * n + 1:3 * n + 2, :] * o_slc[:, cols]
                    + gt[3 * n + 2:3 * n + 3, :] * o_win[:, cols])
    o_ref[0] = jnp.concatenate(outs, axis=0).T


def _attention(qt, kc, vct, ks, vst, kw, vwt, gt, overlap_t):
    B, G, NQT, _, R = qt.shape
    S = ks.shape[2]
    tq, tk = ATTN_TQ, ATTN_TK
    ncp = kc.shape[2]
    kfull = pl.BlockSpec((1, 1, S, HEAD_DIM), lambda b, g, i: (b, g, 0, 0))
    vfull = pl.BlockSpec((1, 1, S // tk, HEAD_DIM, tk), lambda b, g, i: (b, g, 0, 0, 0))
    return pl.pallas_call(
        _attn_kernel,
        grid=(B, G, NQT),
        in_specs=[pl.BlockSpec((1, 1, 1, HEAD_DIM, R), lambda b, g, i: (b, g, i, 0, 0)),
                  pl.BlockSpec((1, 1, ncp, HEAD_DIM), lambda b, g, i: (b, g, 0, 0)),
                  pl.BlockSpec((1, 1, HEAD_DIM, ncp), lambda b, g, i: (b, g, 0, 0)),
                  kfull, vfull, kfull, vfull,
                  pl.BlockSpec((1, 1, GATE_ROWS, tq), lambda b, g, i: (b, g, 0, i)),
                  pl.BlockSpec(overlap_t.shape, lambda b, g, i: (0, 0))],
        out_specs=pl.BlockSpec((1, tq, Q_PER_G * HEAD_DIM), lambda b, g, i: (b, i, g)),
        out_shape=jax.ShapeDtypeStruct((B, S, NSA_WIDTH), F32),
        scratch_shapes=[pltpu.VMEM((S // tk, SUBLANES, tq), F32)],
        compiler_params=pltpu.CompilerParams(dimension_semantics=("parallel", "parallel", "arbitrary")),
        name="attn",
    )(qt, kc, vct, ks, vst, kw, vwt, gt, overlap_t)


def _outproj_kernel(x_ref, cv_ref, nsa_ref, on_ref, w_ref, gt1_ref, g2_ref, sc2_ref, sh2_ref, wrh_ref, wrl_ref, br_ref,
                    tri_ref, x1_ref, h2_ref, rt_ref, cnt_ref, run_ref):
    tm = x_ref.shape[1]

    @pl.when((pl.program_id(0) == 0) & (pl.program_id(1) == 0))
    def _():
        run_ref[...] = jnp.zeros(run_ref.shape, F32)

    nn = _rms(nsa_ref[0], on_ref[...]).astype(BF16)
    y = jnp.dot(jnp.concatenate([cv_ref[0], nn], axis=1), w_ref[...], preferred_element_type=F32)
    x1 = x_ref[0] + gt1_ref[0] * y
    x1_ref[0] = x1
    h2 = _rms(x1, g2_ref[...]) * (1.0 + sc2_ref[0]) + sh2_ref[0]
    for s in range(ROW_TILES):
        h2_ref[pl.ds(s, tm, stride=ROW_TILES), :] = h2[:, s * LANES:(s + 1) * LANES]
    nt = (((1,), (1,)), ((), ()))
    h_hi = h2.astype(BF16)
    h_lo = (h2 - h_hi.astype(F32)).astype(BF16)
    logits = (lax.dot_general(wrh_ref[...], h_hi, nt, preferred_element_type=F32)
              + lax.dot_general(wrh_ref[...], h_lo, nt, preferred_element_type=F32)
              + lax.dot_general(wrl_ref[...], h_hi, nt, preferred_element_type=F32)) + br_ref[...]
    eio = lax.broadcasted_iota(I32, (N_EXPERTS, tm), 0).astype(F32)
    vals, idxs = [], []
    for _ in range(TOP_K):
        m = jnp.max(logits, axis=0, keepdims=True)
        ix = jnp.min(jnp.where(logits == m, eio, float(N_EXPERTS)), axis=0, keepdims=True)
        vals.append(m)
        idxs.append(ix)
        logits = jnp.where(eio == ix, -jnp.inf, logits)
    es = [jnp.exp(v - vals[0]) for v in vals]
    den = es[0] + es[1] + es[2] + es[3]
    hot = jnp.zeros((N_EXPERTS, tm), F32)
    for r in range(TOP_K):
        hot = hot + jnp.where(eio == idxs[r], 1.0, 0.0)
    before = run_ref[...] + jnp.dot(hot.astype(BF16), tri_ref[...], preferred_element_type=F32)
    ranks = [jnp.sum(jnp.where(eio == idxs[r], before, 0.0), axis=0, keepdims=True) for r in range(TOP_K)]
    run_ref[...] = run_ref[...] + jnp.sum(hot, axis=1, keepdims=True)
    cnt_ref[...] = run_ref[...]
    rio = lax.broadcasted_iota(I32, (ROUTE_W, tm), 0)
    out = jnp.zeros((ROUTE_W, tm), F32)
    for r in range(TOP_K):
        out = jnp.where(rio == r, idxs[r], out)
        out = jnp.where(rio == TOP_K + r, es[r] / den, out)
        out = jnp.where(rio == 2 * TOP_K + r, ranks[r], out)
    rt_ref[...] = out


def _outproj(x, cv, nsa, on, w, gt1, g2, sc2, sh2, wr, br):
    B, S, D = x.shape
    tm = min(OUT_TM, S)
    nt = S // tm
    tri = (jnp.arange(tm)[:, None] < jnp.arange(tm)[None, :]).astype(BF16)
    wr_t = wr.T
    wr_hi = wr_t.astype(BF16)
    wr_lo = (wr_t - wr_hi.astype(F32)).astype(BF16)
    row = pl.BlockSpec((1, 1, D), lambda b, i: (b, 0, 0))
    vec = lambda n: pl.BlockSpec((1, n), lambda b, i: (0, 0))
    col = pl.BlockSpec((N_EXPERTS, 1), lambda b, i: (0, 0))
    wr_spec = pl.BlockSpec((N_EXPERTS, D), lambda b, i: (0, 0))
    return pl.pallas_call(
        _outproj_kernel,
        grid=(B, nt),
        in_specs=[pl.BlockSpec((1, tm, D), lambda b, i: (b, i, 0)),
                  pl.BlockSpec((1, tm, CONV_CH), lambda b, i: (b, i, 0)),
                  pl.BlockSpec((1, tm, NSA_WIDTH), lambda b, i: (b, i, 0)),
                  vec(NSA_WIDTH),
                  pl.BlockSpec((D, D), lambda b, i: (0, 0)),
                  row, vec(D), row, row,
                  wr_spec, wr_spec, col,
                  pl.BlockSpec((tm, tm), lambda b, i: (0, 0))],
        out_specs=[pl.BlockSpec((1, tm, D), lambda b, i: (b, i, 0)),
                   pl.BlockSpec((tm * ROW_TILES, LANES), lambda b, i: (b * nt + i, 0)),
                   pl.BlockSpec((ROUTE_W, tm), lambda b, i: (0, b * nt + i)),
                   col],
        out_shape=[jax.ShapeDtypeStruct((B, S, D), F32),
                   jax.ShapeDtypeStruct((B * S * ROW_TILES, LANES), F32),
                   jax.ShapeDtypeStruct((ROUTE_W, B * S), F32),
                   jax.ShapeDtypeStruct((N_EXPERTS, 1), F32)],
        scratch_shapes=[pltpu.VMEM((N_EXPERTS, 1), F32)],
        compiler_params=pltpu.CompilerParams(dimension_semantics=("arbitrary", "arbitrary")),
        name="outproj",
    )(x, cv, nsa, on, w, gt1, g2, sc2, sh2, wr_hi, wr_lo, br.reshape(N_EXPERTS, 1), tri)


def _issue_rows(idx_ref, rows, src_hbm, dst, slot, sem):
    for r in rows:
        pltpu.make_async_copy(src_hbm.at[idx_ref[0, 0, r]],
                              dst.at[slot, pl.ds(r * ROW_TILES, ROW_TILES), :],
                              sem.at[slot]).start(priority=r % 2)


def _wait_rows(dst, slot, sem):
    pltpu.make_async_copy(dst.at[slot], dst.at[slot], sem.at[slot]).wait()


def _rows_2d(buf, slot, base, n):
    return jnp.concatenate(
        [buf[slot, pl.ds(base * ROW_TILES + s, n, stride=ROW_TILES), :] for s in range(ROW_TILES)], axis=1)


def _ffn_kernel(be_ref, nb_ref, ta_ref, tb_ref, h2_hbm, gate_ref, wg_ref, bg_ref, wu_ref, bu_ref, wd_ref, bd_ref,
                o_ref, xbuf, wgb, wub, wdb, sem):
    bm = ta_ref.shape[2]
    i = pl.program_id(0)
    nb = nb_ref[0]
    slot = i % 2

    @pl.when((i == 0) | (be_ref[i] != be_ref[jnp.maximum(i - 1, 0)]))
    def _():
        wgb[...] = wg_ref[0].astype(BF16)
        wub[...] = wu_ref[0].astype(BF16)
        wdb[...] = wd_ref[0].astype(BF16)

    @pl.when(i == 0)
    def _():
        _issue_rows(ta_ref, range(bm), h2_hbm, xbuf, 0, sem)

    @pl.when(i + 1 < nb)
    def _():
        _issue_rows(tb_ref, range(bm), h2_hbm, xbuf, 1 - slot, sem)

    @pl.when(i < nb)
    def _():
        _wait_rows(xbuf, slot, sem)
        x = _rows_2d(xbuf, slot, 0, bm).astype(BF16)
        g = jnp.dot(x, wgb[...], preferred_element_type=F32) + bg_ref[0]
        u = jnp.dot(x, wub[...], preferred_element_type=F32) + bu_ref[0]
        g = jnp.minimum(g, SWIGLU_LIMIT)
        u = jnp.clip(u, -SWIGLU_LIMIT, SWIGLU_LIMIT)
        act = g * jax.nn.sigmoid(SWIGLU_ALPHA * g) * (u + 1.0)
        y = (jnp.dot(act.astype(BF16), wdb[...], preferred_element_type=F32) + bd_ref[0]) * gate_ref[...]
        for s in range(ROW_TILES):
            o_ref[pl.ds(s, bm, stride=ROW_TILES), :] = y[:, s * LANES:(s + 1) * LANES]

    @pl.when(i >= nb)
    def _():
        o_ref[...] = jnp.zeros(o_ref.shape, o_ref.dtype)


def _ffn(block_e, nb_used, buf_tok, h2_rows, buf_gate, wg, bg, wu, bu, wd, bd):
    NB = block_e.shape[0]
    bm = FFN_BM
    D, F = D_MODEL, D_FF
    tok3 = buf_tok.reshape(NB, 1, bm)
    wspec = lambda r, c: pl.BlockSpec((1, r, c), lambda i, be, nb: (be[i], 0, 0))
    vmem_limit = 2 * 3 * D * F * 4 + 3 * D * F * 2 + 4 * bm * D * 4 + 6 * bm * F * 4
    return pl.pallas_call(
        _ffn_kernel,
        grid_spec=pltpu.PrefetchScalarGridSpec(
            num_scalar_prefetch=2,
            grid=(NB,),
            in_specs=[pl.BlockSpec((1, 1, bm), lambda i, be, nb: (i, 0, 0), memory_space=pltpu.SMEM),
                      pl.BlockSpec((1, 1, bm), lambda i, be, nb: (jnp.minimum(i + 1, NB - 1), 0, 0),
                                   memory_space=pltpu.SMEM),
                      pl.BlockSpec(memory_space=pl.ANY),
                      pl.BlockSpec((bm, 1), lambda i, be, nb: (i, 0)),
                      wspec(D, F), wspec(1, F), wspec(D, F), wspec(1, F), wspec(F, D), wspec(1, D)],
            out_specs=pl.BlockSpec((bm * ROW_TILES, LANES), lambda i, be, nb: (i, 0)),
            scratch_shapes=[pltpu.VMEM((2, bm * ROW_TILES, LANES), F32),
                            pltpu.VMEM((D, F), BF16), pltpu.VMEM((D, F), BF16), pltpu.VMEM((F, D), BF16),
                            pltpu.SemaphoreType.DMA((2,))]),
        out_shape=jax.ShapeDtypeStruct((NB * bm * ROW_TILES, LANES), F32),
        compiler_params=pltpu.CompilerParams(dimension_semantics=("arbitrary",),
                                             vmem_limit_bytes=vmem_limit),
        name="ffn",
    )(block_e, nb_used, tok3, tok3, h2_rows, buf_gate, wg, bg, wu, bu, wd, bd)


def _combine_kernel(da_ref, db_ref, y_hbm, x1_ref, gt2_ref, fg_ref, o_ref, buf, sem):
    tm = x1_ref.shape[0]
    n = TOP_K * tm
    i = pl.program_id(0)
    slot = i % 2

    @pl.when(i == 0)
    def _():
        _issue_rows(da_ref, range(n), y_hbm, buf, 0, sem)

    @pl.when(i + 1 < pl.num_programs(0))
    def _():
        _issue_rows(db_ref, range(n), y_hbm, buf, 1 - slot, sem)

    _wait_rows(buf, slot, sem)
    y = _rows_2d(buf, slot, 0, tm)
    for k in range(1, TOP_K):
        y = y + _rows_2d(buf, slot, k * tm, tm)
    x2 = x1_ref[...] + gt2_ref[0] * y
    o_ref[...] = _rms(x2, fg_ref[...])


def _combine(dest3, y_rows, x1, gt2, fg, S):
    T, D = x1.shape
    tm = min(COMB_TM, S)
    NT = T // tm
    per_b = S // tm
    n = TOP_K * tm
    return pl.pallas_call(
        _combine_kernel,
        grid=(NT,),
        in_specs=[pl.BlockSpec((1, 1, n), lambda i: (i, 0, 0), memory_space=pltpu.SMEM),
                  pl.BlockSpec((1, 1, n), lambda i: (jnp.minimum(i + 1, NT - 1), 0, 0), memory_space=pltpu.SMEM),
                  pl.BlockSpec(memory_space=pl.ANY),
                  pl.BlockSpec((tm, D), lambda i: (i, 0)),
                  pl.BlockSpec((1, 1, D), lambda i: (i // per_b, 0, 0)),
                  pl.BlockSpec((1, D), lambda i: (0, 0))],
        out_specs=pl.BlockSpec((tm, D), lambda i: (i, 0)),
        out_shape=jax.ShapeDtypeStruct((T, D), F32),
        scratch_shapes=[pltpu.VMEM((2, n * ROW_TILES, LANES), F32), pltpu.SemaphoreType.DMA((2,))],
        compiler_params=pltpu.CompilerParams(dimension_semantics=("arbitrary",)),
        name="combine",
    )(dest3, dest3, y_rows, x1, gt2, fg)


def _rope_tables(S):
    inv = ROPE_THETA ** (-jnp.arange(0, ROT_DIM, 2, dtype=F32) / ROT_DIM)
    ang = jnp.arange(S, dtype=F32)[:, None] * inv[None, :]
    cos, sin = jnp.cos(ang), jnp.sin(ang)
    d = jnp.arange(KV_W) % HEAD_DIM
    first, second = d < ROT_HALF, (d >= ROT_HALF) & (d < ROT_DIM)
    cos_l = cos[:, d % ROT_HALF]
    sin_l = sin[:, d % ROT_HALF]
    rc = jnp.where((d < ROT_DIM)[None], cos_l, 1.0)
    rs1 = jnp.where(second[None], sin_l, 0.0)
    rs2 = jnp.where(first[None], -sin_l, 0.0)
    return rc, rs1, rs2, cos.T, sin.T


def _route_plan(route, counts, T):
    bm = FFN_BM
    A = T * TOP_K
    idx = route[0:TOP_K].astype(I32)
    gate = route[TOP_K:2 * TOP_K]
    rank = route[2 * TOP_K:3 * TOP_K].astype(I32)
    keys = idx * T + jnp.arange(T, dtype=I32)[None, :]
    skey, sgate = lax.sort((keys.reshape(A), gate.reshape(A)), num_keys=1)
    counts = counts.astype(I32)
    starts = jnp.cumsum(counts) - counts
    padded = (counts + bm - 1) // bm * bm
    pends = jnp.cumsum(padded)
    pstarts = pends - padded
    P = (A + N_EXPERTS * bm + bm - 1) // bm * bm
    NB = P // bm
    blk0 = jnp.arange(NB, dtype=I32) * bm
    block_e = jnp.minimum(jnp.sum((pends[None, :] <= blk0[:, None]).astype(I32), axis=1), N_EXPERTS - 1)
    r = (blk0 - pstarts[block_e])[:, None] + jnp.arange(bm, dtype=I32)[None, :]
    valid = r < counts[block_e][:, None]
    src = jnp.clip(starts[block_e][:, None] + r, 0, A - 1)
    buf_tok = jnp.where(valid, skey[src] - block_e[:, None] * T, 0)
    buf_gate = jnp.where(valid, sgate[src], 0.0).reshape(P, 1)
    dest = rank
    for e in range(N_EXPERTS):
        dest = dest + jnp.where(idx == e, pstarts[e], 0)
    nb_used = (pends[-1] // bm).astype(I32).reshape(1)
    return block_e, nb_used, buf_tok, buf_gate, dest


def kernel(x, c, norm1_g, norm2_g, w_ada, b_ada, w_in, conv_w, conv_b, conv_ln_g, conv_ln_b, cmp_pe_k, cmp_pe_v,
           cmp_k_w1, cmp_k_w2, cmp_v_w1, cmp_v_w2, out_norm_conv, out_norm_nsa, w_out, w_router, b_router,
           w_gate, b_gate, w_up, b_up, w_down, b_down, final_norm_g):
    B, S, D = x.shape
    T = B * S
    G = KV_GROUPS
    assert D == D_MODEL and S % ATTN_TK == 0 and S % CMP_STRIDE == 0 and KV_W == LANES
    rc, rs1, rs2, cos_t, sin_t = _rope_tables(S)
    n_sel = S // SEL_BLK
    nc = S // CMP_STRIDE
    cstart = jnp.arange(nc) * CMP_STRIDE
    jstart = jnp.arange(n_sel) * SEL_BLK
    overlap_t = ((cstart[None, :] <= jstart[:, None] + SEL_BLK - 1)
                 & (cstart[None, :] + CMP_LEN - 1 >= jstart[:, None])
                 & (jnp.arange(nc)[None, :] < nc - 1)).astype(F32)

    assert w_ada.shape[0] == 1
    for l in range(1):
        mod = _adaln(c, w_ada[l], b_ada[l][None])
        sh1, sc1, gt1, sh2, sc2, gt2 = [m[:, None, :] for m in jnp.split(mod, 6, axis=-1)]

        wl = w_in[l]
        o = 2 * CONV_CH + NSA_WIDTH
        kvc = [wl[:, o + i * KV_W:o + (i + 1) * KV_W] for i in range(6)]
        gl = wl[:, o + 6 * KV_W:]
        per_g = 3 * Q_PER_G
        gpad = [jnp.pad(gl[:, per_g * g:per_g * (g + 1)], ((0, 0), (0, GATE_ROWS - per_g))) for g in range(G)]
        wn = jnp.concatenate([wl[:, :2 * CONV_CH], kvc[0], kvc[2], kvc[4], kvc[1]], axis=1).astype(BF16)
        wt = jnp.concatenate([wl[:, 2 * CONV_CH:o], kvc[3], kvc[5]] + gpad, axis=1).T.astype(BF16)
        u, kc, vc, ks, kw, qt, vst, vwt, gates = _inproj(x, sc1, sh1, norm1_g[l][None], wn, wt,
                                                         rc, rs1, rs2, cos_t, sin_t)

        conv_n = _conv(u, conv_w[l], conv_b[l][None], conv_ln_g[l][None], conv_ln_b[l][None],
                       out_norm_conv[l][None])

        kcmp, vcmp_t = _compress(kc, vc, cmp_pe_k[l].reshape(1, -1), cmp_pe_v[l].reshape(1, -1),
                                 cmp_k_w1[l], cmp_k_w2[l], cmp_v_w1[l], cmp_v_w2[l].T)
        nsa = _attention(qt, kcmp, vcmp_t, ks, vst, kw, vwt, gates, overlap_t)

        x1, h2_rows, route, counts = _outproj(x, conv_n, nsa, out_norm_nsa[l][None], w_out[l].astype(BF16), gt1,
                                              norm2_g[l][None], sc2, sh2, w_router[l], b_router[l][None])

        block_e, nb_used, buf_tok, buf_gate, dest = _route_plan(route, counts[:, 0], T)
        y_rows = _ffn(block_e, nb_used, buf_tok, h2_rows.reshape(T, ROW_TILES, LANES), buf_gate,
                      w_gate[l], b_gate[l][:, None, :], w_up[l], b_up[l][:, None, :],
                      w_down[l], b_down[l][:, None, :])
        tm = min(COMB_TM, S)
        dest3 = dest.reshape(TOP_K, T // tm, tm).transpose(1, 0, 2).reshape(T // tm, 1, TOP_K * tm)
        P = y_rows.shape[0] // ROW_TILES
        x = _combine(dest3, y_rows.reshape(P, ROW_TILES, LANES), x1.reshape(T, D), gt2, final_norm_g[None],
                     S).reshape(B, S, D)
    return x
```

```python
import functools

import jax
import jax.numpy as jnp
from jax import lax
from jax.experimental import pallas as pl
from jax.experimental.pallas import tpu as pltpu

F32 = jnp.float32
BF16 = jnp.bfloat16
I32 = jnp.int32
HI = lax.Precision.HIGHEST

D_MODEL = 1024
CONV_CH = 512
CONV_WIDTH = 31
NSA_HEADS = 8
KV_GROUPS = 2
Q_PER_G = NSA_HEADS // KV_GROUPS
HEAD_DIM = 64
NSA_WIDTH = NSA_HEADS * HEAD_DIM
KV_W = KV_GROUPS * HEAD_DIM
ROT_DIM = HEAD_DIM // 4
ROT_HALF = ROT_DIM // 2
ROPE_THETA = 500000.0
CMP_LEN = 32
CMP_STRIDE = 16
CMP_HIDDEN = 128
SEL_BLK = 64
SEL_TOPN = 16
WINDOW = 512
N_EXPERTS = 32
TOP_K = 4
D_FF = 1024
SWIGLU_ALPHA = 1.702
SWIGLU_LIMIT = 7.0
NORM_EPS = 1e-5
NEG_INF = -1e30
FORCE_SCORE = 1e9
LOG2_E = 1.4426950408889634

LANES = 128
SUBLANES = 8
ROW_TILES = D_MODEL // LANES

GATE_ROWS = 16

INPROJ_TM = 512
CONV_TR = 256
CONV_HALO = 32
ATTN_TQ = 256
ATTN_TK = 256
OUT_TM = 512
FFN_BM = 512
COMB_TM = 256
ROUTE_W = 16


def _rms(x, g):
    return x * lax.rsqrt(jnp.mean(x * x, axis=-1, keepdims=True) + NORM_EPS) * g


def _split_bf16(a):
    hi = a.astype(BF16)
    return hi, (a - hi.astype(F32)).astype(BF16)


def _dot3(a, b, dims=(((1,), (0,)), ((), ()))):
    ah, al = _split_bf16(a)
    bh, bl = _split_bf16(b)
    d = lambda x, y: lax.dot_general(x, y, dims, preferred_element_type=F32)
    return d(ah, bh) + d(al, bh) + d(ah, bl)


def _ada_kernel(c_ref, w_ref, b_ref, o_ref):
    c = c_ref[...]
    ca = c * jax.nn.sigmoid(c)
    o_ref[...] = jnp.dot(ca, w_ref[...], preferred_element_type=F32, precision=HI) + b_ref[...]


def _adaln(c, w, b):
    B = c.shape[0]
    D = D_MODEL
    return pl.pallas_call(
        _ada_kernel,
        grid=(6,),
        in_specs=[pl.BlockSpec((B, D), lambda j: (0, 0)),
                  pl.BlockSpec((D, D), lambda j: (0, j)),
                  pl.BlockSpec((1, D), lambda j: (0, j))],
        out_specs=pl.BlockSpec((B, D), lambda j: (0, j)),
        out_shape=jax.ShapeDtypeStruct((B, 6 * D), F32),
        name="adaln",
    )(c, w, b)


def _inproj_kernel(x_ref, sc_ref, sh_ref, g_ref, wn_ref, wt_ref, rc_ref, rs1_ref, rs2_ref, ct_ref, st_ref,
                   u_ref, kc_ref, vc_ref, ks_ref, kw_ref, qt_ref, vst_ref, vwt_ref, gt_ref, stage_ref):
    tm = x_ref.shape[1]
    tq, tk = ATTN_TQ, ATTN_TK
    h = (_rms(x_ref[0], g_ref[...]) * (1.0 + sc_ref[0]) + sh_ref[0]).astype(BF16)

    p = jnp.dot(h, wn_ref[...], preferred_element_type=F32)
    u_ref[0] = p[:, 0:CONV_CH] * jax.nn.sigmoid(p[:, CONV_CH:2 * CONV_CH])
    c0 = 2 * CONV_CH
    rc, rs1, rs2 = rc_ref[...], rs1_ref[...], rs2_ref[...]
    for ref, roped in ((kc_ref, True), (ks_ref, True), (kw_ref, True), (vc_ref, False)):
        v = p[:, c0:c0 + KV_W]
        if roped:
            v = v * rc + pltpu.roll(v, ROT_HALF, 1) * rs1 + pltpu.roll(v, KV_W - ROT_HALF, 1) * rs2
        c0 += KV_W
        if ref is kc_ref or ref is vc_ref:
            stage_ref[...] = v
            left = lax.broadcasted_iota(I32, (tm // CMP_STRIDE, KV_W), 1) < HEAD_DIM
            pieces = [stage_ref[pl.ds(tl, tm // CMP_STRIDE, stride=CMP_STRIDE), :] for tl in range(CMP_STRIDE)]
            for gg in range(KV_GROUPS):
                cols = []
                for tl in range(0, CMP_STRIDE, 2):
                    a, b = pieces[tl], pieces[tl + 1]
                    if gg == 0:
                        cols.append(jnp.where(left, a, pltpu.roll(b, HEAD_DIM, 1)))
                    else:
                        cols.append(jnp.where(left, pltpu.roll(a, HEAD_DIM, 1), b))
                ref[0, gg] = jnp.concatenate(cols, axis=1)
            continue
        for gg in range(KV_GROUPS):
            ref[0, gg] = v[:, HEAD_DIM * gg:HEAD_DIM * (gg + 1)].astype(ref.dtype)

    pt = lax.dot_general(wt_ref[...], h, (((1,), (1,)), ((), ())), preferred_element_type=F32)
    cos_t, sin_t = ct_ref[...], st_ref[...]
    scale = HEAD_DIM ** -0.5 * LOG2_E
    for hh in range(NSA_HEADS):
        blk = pt[HEAD_DIM * hh:HEAD_DIM * (hh + 1), :]
        x1, x2 = blk[0:ROT_HALF], blk[ROT_HALF:ROT_DIM]
        qh = (jnp.concatenate([x1 * cos_t - x2 * sin_t, x2 * cos_t + x1 * sin_t, blk[ROT_DIM:]], axis=0)
              * scale).astype(BF16)
        gg, n = divmod(hh, Q_PER_G)
        for j in range(tm // tq):
            qt_ref[0, gg, j, :, n * tq:(n + 1) * tq] = qh[:, j * tq:(j + 1) * tq]
    r0 = NSA_WIDTH
    for ref in (vst_ref, vwt_ref):
        for gg in range(KV_GROUPS):
            blk = pt[r0 + HEAD_DIM * gg:r0 + HEAD_DIM * (gg + 1), :].astype(BF16)
            for j in range(tm // tk):
                ref[0, gg, j] = blk[:, j * tk:(j + 1) * tk]
        r0 += KV_W
    for gg in range(KV_GROUPS):
        gt_ref[0, gg] = jax.nn.sigmoid(pt[r0 + GATE_ROWS * gg:r0 + GATE_ROWS * (gg + 1), :])


def _inproj(x, sc, sh, g, wn, wt, rc, rs1, rs2, cos_t, sin_t):
    B, S, D = x.shape
    tm = min(INPROJ_TM, S)
    tq, tk = ATTN_TQ, ATTN_TK
    G = KV_GROUPS
    kv = lambda dt: jax.ShapeDtypeStruct((B, G, S, HEAD_DIM), dt)
    kv_spec = pl.BlockSpec((1, G, tm, HEAD_DIM), lambda b, i: (b, 0, i, 0))
    chunk = CMP_STRIDE * HEAD_DIM
    ck_shape = jax.ShapeDtypeStruct((B, G, S // CMP_STRIDE, chunk), F32)
    ck_spec = pl.BlockSpec((1, G, tm // CMP_STRIDE, chunk), lambda b, i: (b, 0, i, 0))
    vt_shape = jax.ShapeDtypeStruct((B, G, S // tk, HEAD_DIM, tk), BF16)
    vt_spec = pl.BlockSpec((1, G, tm // tk, HEAD_DIM, tk), lambda b, i: (b, 0, i, 0, 0))
    row = pl.BlockSpec((1, 1, D), lambda b, i: (b, 0, 0))
    tab = pl.BlockSpec((tm, LANES), lambda b, i: (i, 0))
    tab_t = pl.BlockSpec((ROT_HALF, tm), lambda b, i: (0, i))
    return pl.pallas_call(
        _inproj_kernel,
        grid=(B, S // tm),
        in_specs=[pl.BlockSpec((1, tm, D), lambda b, i: (b, i, 0)), row, row,
                  pl.BlockSpec((1, D), lambda b, i: (0, 0)),
                  pl.BlockSpec(wn.shape, lambda b, i: (0, 0)),
                  pl.BlockSpec(wt.shape, lambda b, i: (0, 0)),
                  tab, tab, tab, tab_t, tab_t],
        out_specs=[pl.BlockSpec((1, tm, CONV_CH), lambda b, i: (b, i, 0)),
                   ck_spec, ck_spec, kv_spec, kv_spec,
                   pl.BlockSpec((1, G, tm // tq, HEAD_DIM, Q_PER_G * tq), lambda b, i: (b, 0, i, 0, 0)),
                   vt_spec, vt_spec,
                   pl.BlockSpec((1, G, GATE_ROWS, tm), lambda b, i: (b, 0, 0, i))],
        out_shape=[jax.ShapeDtypeStruct((B, S, CONV_CH), F32),
                   ck_shape, ck_shape, kv(BF16), kv(BF16),
                   jax.ShapeDtypeStruct((B, G, S // tq, HEAD_DIM, Q_PER_G * tq), BF16),
                   vt_shape, vt_shape,
                   jax.ShapeDtypeStruct((B, G, GATE_ROWS, S), F32)],
        scratch_shapes=[pltpu.VMEM((tm, KV_W), F32)],
        compiler_params=pltpu.CompilerParams(dimension_semantics=("parallel", "parallel")),
        name="inproj",
    )(x, sc, sh, g, wn, wt, rc, rs1, rs2, cos_t, sin_t)


def _conv_kernel(prev_ref, cur_ref, w_ref, cb_ref, lg_ref, lb_ref, on_ref, o_ref, pad_ref, win_ref):
    tr = cur_ref.shape[1]
    first = pl.program_id(1) == 0
    halo = prev_ref[0, tr - CONV_HALO:tr, :]
    pad_ref[0:CONV_HALO, :] = jnp.where(first, 0.0, halo)
    pad_ref[CONV_HALO:CONV_HALO + tr, :] = cur_ref[0]
    off = CONV_HALO - (CONV_WIDTH - 1)
    acc = jnp.zeros((tr, CONV_CH), F32)
    for b in range(SUBLANES):
        taps = range(b, CONV_WIDTH, SUBLANES)
        rows = tr + SUBLANES * (len(taps) - 1)
        win_ref[b, 0:rows, :] = pad_ref[off + b:off + b + rows, :]
        for a, k in enumerate(taps):
            acc = acc + win_ref[b, SUBLANES * a:SUBLANES * a + tr, :] * w_ref[k:k + 1, :]
    y = acc + cb_ref[...]
    mu = jnp.mean(y, axis=-1, keepdims=True)
    yc = y - mu
    var = jnp.mean(yc * yc, axis=-1, keepdims=True)
    yn = yc * lax.rsqrt(var + NORM_EPS) * lg_ref[...] + lb_ref[...]
    s = yn * jax.nn.sigmoid(yn)
    o_ref[0] = _rms(s, on_ref[...]).astype(o_ref.dtype)


def _conv(u, w, cb, lg, lb, on):
    B, S, C = u.shape
    tr = min(CONV_TR, S)
    vec = pl.BlockSpec((1, C), lambda b, i: (0, 0))
    return pl.pallas_call(
        _conv_kernel,
        grid=(B, S // tr),
        in_specs=[pl.BlockSpec((1, tr, C), lambda b, i: (b, jnp.maximum(i - 1, 0), 0)),
                  pl.BlockSpec((1, tr, C), lambda b, i: (b, i, 0)),
                  pl.BlockSpec((CONV_WIDTH, C), lambda b, i: (0, 0)),
                  vec, vec, vec, vec],
        out_specs=pl.BlockSpec((1, tr, C), lambda b, i: (b, i, 0)),
        out_shape=jax.ShapeDtypeStruct((B, S, C), BF16),
        scratch_shapes=[pltpu.VMEM((CONV_HALO + tr, C), F32),
                        pltpu.VMEM((SUBLANES, tr + SUBLANES * ((CONV_WIDTH - 1) // SUBLANES), C), F32)],
        compiler_params=pltpu.CompilerParams(dimension_semantics=("parallel", "parallel")),
        name="conv",
    )(u, u, w, cb, lg, lb, on)


def _cmp_kernel(kx_ref, vx_ref, pek_ref, pev_ref, kw1_ref, kw2_ref, vw1_ref, vw2t_ref, ko_ref, vo_ref):
    nc = kx_ref.shape[2]
    half = kx_ref.shape[3]
    nt = (((1,), (1,)), ((), ()))
    for x_ref, pe_ref, w1_ref, w2_ref, o_ref, transposed in ((kx_ref, pek_ref, kw1_ref, kw2_ref, ko_ref, False),
                                                             (vx_ref, pev_ref, vw1_ref, vw2t_ref, vo_ref, True)):
        w1 = w1_ref[...]
        pe = jnp.broadcast_to(pe_ref[...], (SUBLANES, 2 * half))
        pe_proj = _dot3(pe, w1)[0:1]
        for gg in range(KV_GROUPS):
            xg = x_ref[0, gg]
            first = _dot3(xg, w1[0:half])
            second = _dot3(xg, w1[half:2 * half])
            hid = first + pltpu.roll(second, nc - 1, 0) + pe_proj
            hid = hid * jax.nn.sigmoid(hid)
            if transposed:
                o_ref[0, gg] = _dot3(w2_ref[...], hid, nt)
            else:
                o_ref[0, gg] = _dot3(hid, w2_ref[...])


def _compress(kx, vx, pek, pev, kw1, kw2, vw1, vw2t):
    B, G, NC, W = kx.shape
    xs = pl.BlockSpec((1, G, NC, W), lambda b: (b, 0, 0, 0))
    full = lambda a: pl.BlockSpec(a.shape, lambda b: (0,) * a.ndim)
    return pl.pallas_call(
        _cmp_kernel,
        grid=(B,),
        in_specs=[xs, xs, full(pek), full(pev), full(kw1), full(kw2), full(vw1), full(vw2t)],
        out_specs=[pl.BlockSpec((1, G, NC, HEAD_DIM), lambda b: (b, 0, 0, 0)),
                   pl.BlockSpec((1, G, HEAD_DIM, NC), lambda b: (b, 0, 0, 0))],
        out_shape=[jax.ShapeDtypeStruct((B, G, NC, HEAD_DIM), F32),
                   jax.ShapeDtypeStruct((B, G, HEAD_DIM, NC), F32)],
        compiler_params=pltpu.CompilerParams(dimension_semantics=("parallel",)),
        name="compress",
    )(kx, vx, pek, pev, kw1, kw2, vw1, vw2t)


def _attn_kernel(qt_ref, kc_ref, vct_ref, ks_ref, vst_ref, kw_ref, vwt_ref, gt_ref, ovt_ref, o_ref, sel_ref):
    tq, tk = ATTN_TQ, ATTN_TK
    R = Q_PER_G * tq
    per_tile = tk // SEL_BLK
    S = ks_ref.shape[2]
    ncp = kc_ref.shape[2]
    nsel = ovt_ref.shape[0]
    qi = pl.program_id(2)
    q0 = qi * tq
    qt = qt_ref[0, 0, 0]
    t_row = q0 + lax.broadcasted_iota(I32, (1, tq), 1)
    heads = lambda a: jnp.concatenate([a] * Q_PER_G, axis=1)

    sc = jnp.dot(kc_ref[0, 0].astype(BF16), qt, preferred_element_type=F32)
    c_io = lax.broadcasted_iota(I32, (ncp, tq), 0)
    m_c = (c_io * CMP_STRIDE + (CMP_LEN - 1) <= t_row) & (c_io < ncp - 1)
    scb = sc + heads(jnp.where(m_c, 0.0, NEG_INF))
    e = jnp.exp2(scb - jnp.max(scb, axis=0, keepdims=True)) * heads(jnp.where(m_c, 1.0, 0.0))
    den = jnp.sum(e, axis=0, keepdims=True)
    pc = e / jnp.where(den > 0.0, den, 1.0)
    o_cmp = jnp.dot(vct_ref[0, 0].astype(BF16), pc.astype(BF16), preferred_element_type=F32)

    psum = pc[:, 0:tq]
    for n in range(1, Q_PER_G):
        psum = psum + pc[:, n * tq:(n + 1) * tq]
    imp = _dot3(ovt_ref[...], psum)
    j_io = lax.broadcasted_iota(I32, (nsel, tq), 0)
    cur = t_row // SEL_BLK
    valid = j_io * SEL_BLK <= t_row
    forced = (j_io == 0) | (j_io == cur) | (j_io == cur - 1)
    score = jnp.where(valid, jnp.where(forced, FORCE_SCORE, imp), NEG_INF)
    rank = jnp.zeros((nsel, tq), F32)
    for i in range(nsel):
        row = score[i:i + 1, :]
        tie = jnp.where(j_io > i, 1.0, 0.0)
        rank = rank + jnp.where(row > score, 1.0, jnp.where(row == score, tie, 0.0))
    sel_bias = jnp.where(rank < float(min(SEL_TOPN, nsel)), 0.0, NEG_INF)
    sel_ref[...] = jnp.zeros(sel_ref.shape, F32)
    for jj in range(S // tk):
        sel_ref[jj, 0:per_tile, :] = sel_bias[jj * per_tile:(jj + 1) * per_tile, :]

    k_io = lax.broadcasted_iota(I32, (tk, tq), 0)

    def flash_step(k_ref, vt_ref, kj, bias, carry):
        m, l, acc = carry
        k0 = pl.multiple_of(kj * tk, tk)
        s = jnp.dot(k_ref[0, 0, pl.ds(k0, tk), :], qt, preferred_element_type=F32) + heads(bias)
        m_new = jnp.maximum(m, jnp.max(s, axis=0, keepdims=True))
        alpha = jnp.exp2(m - m_new)
        p = jnp.exp2(s - m_new)
        l = alpha * l + jnp.sum(p, axis=0, keepdims=True)
        acc = alpha * acc + jnp.dot(vt_ref[0, 0, kj], p.astype(BF16), preferred_element_type=F32)
        return m_new, l, acc

    init = (jnp.full((1, R), NEG_INF, F32), jnp.zeros((1, R), F32), jnp.zeros((HEAD_DIM, R), F32))

    def slc_body(kj, carry):
        blocks = sel_ref[kj]
        bias = jnp.concatenate([jnp.broadcast_to(blocks[b:b + 1, :], (SEL_BLK, tq)) for b in range(per_tile)], axis=0)
        bias = jnp.where(kj * tk + k_io <= t_row, bias, NEG_INF)
        return flash_step(ks_ref, vst_ref, kj, bias, carry)

    def win_body(kj, carry):
        rel = t_row - (kj * tk + k_io)
        bias = jnp.where((rel >= 0) & (rel < WINDOW), 0.0, NEG_INF)
        return flash_step(kw_ref, vwt_ref, kj, bias, carry)

    n_slc = (q0 + tq + tk - 1) // tk
    lo_tile = jnp.maximum(q0 - (WINDOW - 1), 0) // tk
    n_pair = lo_tile // 2

    def pair_body(i, carry):
        return slc_body(2 * i, carry[0]), slc_body(2 * i + 1, carry[1])

    st_a, st_b = lax.fori_loop(0, n_pair, pair_body, (init, init))
    st_a = lax.fori_loop(2 * n_pair, lo_tile, slc_body, st_a)

    def both_body(kj, carry):
        return slc_body(kj, carry[0]), win_body(kj, carry[1])

    st_a, (_, l_w, acc_w) = lax.fori_loop(lo_tile, n_slc, both_body, (st_a, init))
    m_s = jnp.maximum(st_a[0], st_b[0])
    w_a, w_b = jnp.exp2(st_a[0] - m_s), jnp.exp2(st_b[0] - m_s)
    l_s = w_a * st_a[1] + w_b * st_b[1]
    acc_s = w_a * st_a[2] + w_b * st_b[2]


    gt = gt_ref[0, 0]
    o_slc = acc_s / l_s
    o_win = acc_w / l_w
    outs = []
    for n in range(Q_PER_G):
        cols = slice(n * tq, (n + 1) * tq)
        outs.append(gt[3 * n:3 * n + 1, :] * o_cmp[:, cols] + gt[3 * n + 1:3 * n + 2, :] * o_slc[:, cols]
                    + gt[3 * n + 2:3 * n + 3, :] * o_win[:, cols])
    o_ref[0] = jnp.concatenate(outs, axis=0).T


def _attention(qt, kc, vct, ks, vst, kw, vwt, gt, overlap_t):
    B, G, NQT, _, R = qt.shape
    S = ks.shape[2]
    tq, tk = ATTN_TQ, ATTN_TK
    ncp = kc.shape[2]
    kfull = pl.BlockSpec((1, 1, S, HEAD_DIM), lambda b, g, i: (b, g, 0, 0))
    vfull = pl.BlockSpec((1, 1, S // tk, HEAD_DIM, tk), lambda b, g, i: (b, g, 0, 0, 0))
    return pl.pallas_call(
        _attn_kernel,
        grid=(B, G, NQT),
        in_specs=[pl.BlockSpec((1, 1, 1, HEAD_DIM, R), lambda b, g, i: (b, g, i, 0, 0)),
                  pl.BlockSpec((1, 1, ncp, HEAD_DIM), lambda b, g, i: (b, g, 0, 0)),
                  pl.BlockSpec((1, 1, HEAD_DIM, ncp), lambda b, g, i: (b, g, 0, 0)),
                  kfull, vfull, kfull, vfull,
                  pl.BlockSpec((1, 1, GATE_ROWS, tq), lambda b, g, i: (b, g, 0, i)),
                  pl.BlockSpec(overlap_t.shape, lambda b, g, i: (0, 0))],
        out_specs=pl.BlockSpec((1, tq, Q_PER_G * HEAD_DIM), lambda b, g, i: (b, i, g)),
        out_shape=jax.ShapeDtypeStruct((B, S, NSA_WIDTH), F32),
        scratch_shapes=[pltpu.VMEM((S // tk, SUBLANES, tq), F32)],
        compiler_params=pltpu.CompilerParams(dimension_semantics=("parallel", "parallel", "arbitrary")),
        name="attn",
    )(qt, kc, vct, ks, vst, kw, vwt, gt, overlap_t)


def _outproj_kernel(x_ref, cv_ref, nsa_ref, on_ref, w_ref, gt1_ref, g2_ref, sc2_ref, sh2_ref, wrh_ref, wrl_ref, br_ref,
                    tri_ref, x1_ref, h2_ref, rt_ref, cnt_ref, run_ref):
    tm = x_ref.shape[1]

    @pl.when((pl.program_id(0) == 0) & (pl.program_id(1) == 0))
    def _():
        run_ref[...] = jnp.zeros(run_ref.shape, F32)

    nn = _rms(nsa_ref[0], on_ref[...]).astype(BF16)
    y = jnp.dot(jnp.concatenate([cv_ref[0], nn], axis=1), w_ref[...], preferred_element_type=F32)
    x1 = x_ref[0] + gt1_ref[0] * y
    x1_ref[0] = x1
    h2 = _rms(x1, g2_ref[...]) * (1.0 + sc2_ref[0]) + sh2_ref[0]
    for s in range(ROW_TILES):
        h2_ref[pl.ds(s, tm, stride=ROW_TILES), :] = h2[:, s * LANES:(s + 1) * LANES]
    nt = (((1,), (1,)), ((), ()))
    h_hi = h2.astype(BF16)
    h_lo = (h2 - h_hi.astype(F32)).astype(BF16)
    logits = (lax.dot_general(wrh_ref[...], h_hi, nt, preferred_element_type=F32)
              + lax.dot_general(wrh_ref[...], h_lo, nt, preferred_element_type=F32)
              + lax.dot_general(wrl_ref[...], h_hi, nt, preferred_element_type=F32)) + br_ref[...]
    eio = lax.broadcasted_iota(I32, (N_EXPERTS, tm), 0).astype(F32)
    vals, idxs = [], []
    for _ in range(TOP_K):
        m = jnp.max(logits, axis=0, keepdims=True)
        ix = jnp.min(jnp.where(logits == m, eio, float(N_EXPERTS)), axis=0, keepdims=True)
        vals.append(m)
        idxs.append(ix)
        logits = jnp.where(eio == ix, -jnp.inf, logits)
    es = [jnp.exp(v - vals[0]) for v in vals]
    den = es[0] + es[1] + es[2] + es[3]
    hot = jnp.zeros((N_EXPERTS, tm), F32)
    for r in range(TOP_K):
        hot = hot + jnp.where(eio == idxs[r], 1.0, 0.0)
    before = run_ref[...] + jnp.dot(hot.astype(BF16), tri_ref[...], preferred_element_type=F32)
    ranks = [jnp.sum(jnp.where(eio == idxs[r], before, 0.0), axis=0, keepdims=True) for r in range(TOP_K)]
    run_ref[...] = run_ref[...] + jnp.sum(hot, axis=1, keepdims=True)
    cnt_ref[...] = run_ref[...]
    rio = lax.broadcasted_iota(I32, (ROUTE_W, tm), 0)
    out = jnp.zeros((ROUTE_W, tm), F32)
    for r in range(TOP_K):
        out = jnp.where(rio == r, idxs[r], out)
        out = jnp.where(rio == TOP_K + r, es[r] / den, out)
        out = jnp.where(rio == 2 * TOP_K + r, ranks[r], out)
    rt_ref[...] = out


def _outproj(x, cv, nsa, on, w, gt1, g2, sc2, sh2, wr, br):
    B, S, D = x.shape
    tm = min(OUT_TM, S)
    nt = S // tm
    tri = (jnp.arange(tm)[:, None] < jnp.arange(tm)[None, :]).astype(BF16)
    wr_t = wr.T
    wr_hi = wr_t.astype(BF16)
    wr_lo = (wr_t - wr_hi.astype(F32)).astype(BF16)
    row = pl.BlockSpec((1, 1, D), lambda b, i: (b, 0, 0))
    vec = lambda n: pl.BlockSpec((1, n), lambda b, i: (0, 0))
    col = pl.BlockSpec((N_EXPERTS, 1), lambda b, i: (0, 0))
    wr_spec = pl.BlockSpec((N_EXPERTS, D), lambda b, i: (0, 0))
    return pl.pallas_call(
        _outproj_kernel,
        grid=(B, nt),
        in_specs=[pl.BlockSpec((1, tm, D), lambda b, i: (b, i, 0)),
                  pl.BlockSpec((1, tm, CONV_CH), lambda b, i: (b, i, 0)),
                  pl.BlockSpec((1, tm, NSA_WIDTH), lambda b, i: (b, i, 0)),
                  vec(NSA_WIDTH),
                  pl.BlockSpec((D, D), lambda b, i: (0, 0)),
                  row, vec(D), row, row,
                  wr_spec, wr_spec, col,
                  pl.BlockSpec((tm, tm), lambda b, i: (0, 0))],
        out_specs=[pl.BlockSpec((1, tm, D), lambda b, i: (b, i, 0)),
                   pl.BlockSpec((tm * ROW_TILES, LANES), lambda b, i: (b * nt + i, 0)),
                   pl.BlockSpec((ROUTE_W, tm), lambda b, i: (0, b * nt + i)),
                   col],
        out_shape=[jax.ShapeDtypeStruct((B, S, D), F32),
                   jax.ShapeDtypeStruct((B * S * ROW_TILES, LANES), F32),
                   jax.ShapeDtypeStruct((ROUTE_W, B * S), F32),
                   jax.ShapeDtypeStruct((N_EXPERTS, 1), F32)],
        scratch_shapes=[pltpu.VMEM((N_EXPERTS, 1), F32)],
        compiler_params=pltpu.CompilerParams(dimension_semantics=("arbitrary", "arbitrary")),
        name="outproj",
    )(x, cv, nsa, on, w, gt1, g2, sc2, sh2, wr_hi, wr_lo, br.reshape(N_EXPERTS, 1), tri)


def _issue_rows(idx_ref, rows, src_hbm, dst, slot, sem):
    for r in rows:
        pltpu.make_async_copy(src_hbm.at[idx_ref[0, 0, r]],
                              dst.at[slot, pl.ds(r * ROW_TILES, ROW_TILES), :],
                              sem.at[slot]).start(priority=r % 2)


def _wait_rows(dst, slot, sem):
    pltpu.make_async_copy(dst.at[slot], dst.at[slot], sem.at[slot]).wait()


def _row_to_col(row):
    eye = lax.broadcasted_iota(I32, (LANES, LANES), 0) == lax.broadcasted_iota(I32, (LANES, LANES), 1)
    cols = [jnp.sum(jnp.where(eye, jnp.broadcast_to(row[:, j * LANES:(j + 1) * LANES], (LANES, LANES)), 0.0),
                    axis=1, keepdims=True) for j in range(row.shape[1] // LANES)]
    return jnp.concatenate(cols, axis=0)


def _rows_2d(buf, slot, base, n):
    return jnp.concatenate(
        [buf[slot, pl.ds(base * ROW_TILES + s, n, stride=ROW_TILES), :] for s in range(ROW_TILES)], axis=1)


def _ffn_kernel(be_ref, nb_ref, ta_ref, tb_ref, h2_hbm, gate_ref, wg_ref, bg_ref, wu_ref, bu_ref, wd_ref, bd_ref,
                o_ref, xbuf, wgb, wub, wdb, sem):
    bm = ta_ref.shape[2]
    i = pl.program_id(0)
    nb = nb_ref[0]
    slot = i % 2

    @pl.when((i == 0) | (be_ref[i] != be_ref[jnp.maximum(i - 1, 0)]))
    def _():
        wgb[...] = wg_ref[0].astype(BF16)
        wub[...] = wu_ref[0].astype(BF16)
        wdb[...] = wd_ref[0].astype(BF16)

    @pl.when(i == 0)
    def _():
        _issue_rows(ta_ref, range(bm), h2_hbm, xbuf, 0, sem)

    @pl.when(i + 1 < nb)
    def _():
        _issue_rows(tb_ref, range(bm), h2_hbm, xbuf, 1 - slot, sem)

    @pl.when(i < nb)
    def _():
        _wait_rows(xbuf, slot, sem)
        x = _rows_2d(xbuf, slot, 0, bm).astype(BF16)
        g = jnp.dot(x, wgb[...], preferred_element_type=F32) + bg_ref[0]
        u = jnp.dot(x, wub[...], preferred_element_type=F32) + bu_ref[0]
        g = jnp.minimum(g, SWIGLU_LIMIT)
        u = jnp.clip(u, -SWIGLU_LIMIT, SWIGLU_LIMIT)
        act = g * jax.nn.sigmoid(SWIGLU_ALPHA * g) * (u + 1.0)
        y = (jnp.dot(act.astype(BF16), wdb[...], preferred_element_type=F32) + bd_ref[0]) * _row_to_col(gate_ref[0])
        for s in range(ROW_TILES):
            o_ref[pl.ds(s, bm, stride=ROW_TILES), :] = y[:, s * LANES:(s + 1) * LANES]

    @pl.when(i >= nb)
    def _():
        o_ref[...] = jnp.zeros(o_ref.shape, o_ref.dtype)


def _ffn(block_e, nb_used, buf_tok, h2_rows, buf_gate, wg, bg, wu, bu, wd, bd):
    NB = block_e.shape[0]
    bm = FFN_BM
    D, F = D_MODEL, D_FF
    tok3 = buf_tok.reshape(NB, 1, bm)
    wspec = lambda r, c: pl.BlockSpec((1, r, c), lambda i, be, nb: (be[i], 0, 0))
    vmem_limit = 2 * 3 * D * F * 4 + 3 * D * F * 2 + 4 * bm * D * 4 + 6 * bm * F * 4
    return pl.pallas_call(
        _ffn_kernel,
        grid_spec=pltpu.PrefetchScalarGridSpec(
            num_scalar_prefetch=2,
            grid=(NB,),
            in_specs=[pl.BlockSpec((1, 1, bm), lambda i, be, nb: (i, 0, 0), memory_space=pltpu.SMEM),
                      pl.BlockSpec((1, 1, bm), lambda i, be, nb: (jnp.minimum(i + 1, NB - 1), 0, 0),
                                   memory_space=pltpu.SMEM),
                      pl.BlockSpec(memory_space=pl.ANY),
                      pl.BlockSpec((1, 1, bm), lambda i, be, nb: (i, 0, 0)),
                      wspec(D, F), wspec(1, F), wspec(D, F), wspec(1, F), wspec(F, D), wspec(1, D)],
            out_specs=pl.BlockSpec((bm * ROW_TILES, LANES), lambda i, be, nb: (i, 0)),
            scratch_shapes=[pltpu.VMEM((2, bm * ROW_TILES, LANES), F32),
                            pltpu.VMEM((D, F), BF16), pltpu.VMEM((D, F), BF16), pltpu.VMEM((F, D), BF16),
                            pltpu.SemaphoreType.DMA((2,))]),
        out_shape=jax.ShapeDtypeStruct((NB * bm * ROW_TILES, LANES), F32),
        compiler_params=pltpu.CompilerParams(dimension_semantics=("arbitrary",),
                                             vmem_limit_bytes=vmem_limit),
        name="ffn",
    )(block_e, nb_used, tok3, tok3, h2_rows, buf_gate, wg, bg, wu, bu, wd, bd)


def _combine_kernel(da_ref, db_ref, y_hbm, x1_ref, gt2_ref, fg_ref, o_ref, buf, sem):
    tm = x1_ref.shape[0]
    n = TOP_K * tm
    i = pl.program_id(0)
    slot = i % 2

    @pl.when(i == 0)
    def _():
        _issue_rows(da_ref, range(n), y_hbm, buf, 0, sem)

    @pl.when(i + 1 < pl.num_programs(0))
    def _():
        _issue_rows(db_ref, range(n), y_hbm, buf, 1 - slot, sem)

    _wait_rows(buf, slot, sem)
    y = _rows_2d(buf, slot, 0, tm)
    for k in range(1, TOP_K):
        y = y + _rows_2d(buf, slot, k * tm, tm)
    x2 = x1_ref[...] + gt2_ref[0] * y
    o_ref[...] = _rms(x2, fg_ref[...])


def _combine(dest3, y_rows, x1, gt2, fg, S):
    T, D = x1.shape
    tm = min(COMB_TM, S)
    NT = T // tm
    per_b = S // tm
    n = TOP_K * tm
    return pl.pallas_call(
        _combine_kernel,
        grid=(NT,),
        in_specs=[pl.BlockSpec((1, 1, n), lambda i: (i, 0, 0), memory_space=pltpu.SMEM),
                  pl.BlockSpec((1, 1, n), lambda i: (jnp.minimum(i + 1, NT - 1), 0, 0), memory_space=pltpu.SMEM),
                  pl.BlockSpec(memory_space=pl.ANY),
                  pl.BlockSpec((tm, D), lambda i: (i, 0)),
                  pl.BlockSpec((1, 1, D), lambda i: (i // per_b, 0, 0)),
                  pl.BlockSpec((1, D), lambda i: (0, 0))],
        out_specs=pl.BlockSpec((tm, D), lambda i: (i, 0)),
        out_shape=jax.ShapeDtypeStruct((T, D), F32),
        scratch_shapes=[pltpu.VMEM((2, n * ROW_TILES, LANES), F32), pltpu.SemaphoreType.DMA((2,))],
        compiler_params=pltpu.CompilerParams(dimension_semantics=("arbitrary",)),
        name="combine",
    )(dest3, dest3, y_rows, x1, gt2, fg)


def _rope_tables(S):
    inv = ROPE_THETA ** (-jnp.arange(0, ROT_DIM, 2, dtype=F32) / ROT_DIM)
    ang = jnp.arange(S, dtype=F32)[:, None] * inv[None, :]
    cos, sin = jnp.cos(ang), jnp.sin(ang)
    d = jnp.arange(KV_W) % HEAD_DIM
    first, second = d < ROT_HALF, (d >= ROT_HALF) & (d < ROT_DIM)
    cos_l = cos[:, d % ROT_HALF]
    sin_l = sin[:, d % ROT_HALF]
    rc = jnp.where((d < ROT_DIM)[None], cos_l, 1.0)
    rs1 = jnp.where(second[None], sin_l, 0.0)
    rs2 = jnp.where(first[None], -sin_l, 0.0)
    return rc, rs1, rs2, cos.T, sin.T


def _route_plan(route, counts, T):
    bm = FFN_BM
    A = T * TOP_K
    idx = route[0:TOP_K].astype(I32)
    gate = route[TOP_K:2 * TOP_K]
    rank = route[2 * TOP_K:3 * TOP_K].astype(I32)
    keys = idx * T + jnp.arange(T, dtype=I32)[None, :]
    skey, sgate = lax.sort((keys.reshape(A), gate.reshape(A)), num_keys=1)
    counts = counts.astype(I32)
    starts = jnp.cumsum(counts) - counts
    padded = (counts + bm - 1) // bm * bm
    pends = jnp.cumsum(padded)
    pstarts = pends - padded
    P = (A + N_EXPERTS * bm + bm - 1) // bm * bm
    NB = P // bm
    blk0 = jnp.arange(NB, dtype=I32) * bm
    block_e = jnp.minimum(jnp.sum((pends[None, :] <= blk0[:, None]).astype(I32), axis=1), N_EXPERTS - 1)
    r = (blk0 - pstarts[block_e])[:, None] + jnp.arange(bm, dtype=I32)[None, :]
    valid = r < counts[block_e][:, None]
    src = jnp.clip(starts[block_e][:, None] + r, 0, A - 1)
    buf_tok = jnp.where(valid, skey[src] - block_e[:, None] * T, 0)
    buf_gate = jnp.where(valid, sgate[src], 0.0).reshape(NB, 1, bm)
    dest = rank
    for e in range(N_EXPERTS):
        dest = dest + jnp.where(idx == e, pstarts[e], 0)
    nb_used = (pends[-1] // bm).astype(I32).reshape(1)
    return block_e, nb_used, buf_tok, buf_gate, dest


def kernel(x, c, norm1_g, norm2_g, w_ada, b_ada, w_in, conv_w, conv_b, conv_ln_g, conv_ln_b, cmp_pe_k, cmp_pe_v,
           cmp_k_w1, cmp_k_w2, cmp_v_w1, cmp_v_w2, out_norm_conv, out_norm_nsa, w_out, w_router, b_router,
           w_gate, b_gate, w_up, b_up, w_down, b_down, final_norm_g):
    B, S, D = x.shape
    T = B * S
    G = KV_GROUPS
    assert D == D_MODEL and S % ATTN_TK == 0 and S % CMP_STRIDE == 0 and KV_W == LANES
    rc, rs1, rs2, cos_t, sin_t = _rope_tables(S)
    n_sel = S // SEL_BLK
    nc = S // CMP_STRIDE
    cstart = jnp.arange(nc) * CMP_STRIDE
    jstart = jnp.arange(n_sel) * SEL_BLK
    overlap_t = ((cstart[None, :] <= jstart[:, None] + SEL_BLK - 1)
                 & (cstart[None, :] + CMP_LEN - 1 >= jstart[:, None])
                 & (jnp.arange(nc)[None, :] < nc - 1)).astype(F32)

    assert w_ada.shape[0] == 1
    for l in range(1):
        mod = _adaln(c, w_ada[l], b_ada[l][None])
        sh1, sc1, gt1, sh2, sc2, gt2 = [m[:, None, :] for m in jnp.split(mod, 6, axis=-1)]

        wl = w_in[l]
        o = 2 * CONV_CH + NSA_WIDTH
        kvc = [wl[:, o + i * KV_W:o + (i + 1) * KV_W] for i in range(6)]
        gl = wl[:, o + 6 * KV_W:]
        per_g = 3 * Q_PER_G
        gpad = [jnp.pad(gl[:, per_g * g:per_g * (g + 1)], ((0, 0), (0, GATE_ROWS - per_g))) for g in range(G)]
        wn = jnp.concatenate([wl[:, :2 * CONV_CH], kvc[0], kvc[2], kvc[4], kvc[1]], axis=1).astype(BF16)
        wt = jnp.concatenate([wl[:, 2 * CONV_CH:o], kvc[3], kvc[5]] + gpad, axis=1).T.astype(BF16)
        u, kc, vc, ks, kw, qt, vst, vwt, gates = _inproj(x, sc1, sh1, norm1_g[l][None], wn, wt,
                                                         rc, rs1, rs2, cos_t, sin_t)

        conv_n = _conv(u, conv_w[l], conv_b[l][None], conv_ln_g[l][None], conv_ln_b[l][None],
                       out_norm_conv[l][None])

        kcmp, vcmp_t = _compress(kc, vc, cmp_pe_k[l].reshape(1, -1), cmp_pe_v[l].reshape(1, -1),
                                 cmp_k_w1[l], cmp_k_w2[l], cmp_v_w1[l], cmp_v_w2[l].T)
        nsa = _attention(qt, kcmp, vcmp_t, ks, vst, kw, vwt, gates, overlap_t)

        x1, h2_rows, route, counts = _outproj(x, conv_n, nsa, out_norm_nsa[l][None], w_out[l].astype(BF16), gt1,
                                              norm2_g[l][None], sc2, sh2, w_router[l], b_router[l][None])

        block_e, nb_used, buf_tok, buf_gate, dest = _route_plan(route, counts[:, 0], T)
        y_rows = _ffn(block_e, nb_used, buf_tok, h2_rows.reshape(T, ROW_TILES, LANES), buf_gate,
                      w_gate[l], b_gate[l][:, None, :], w_up[l], b_up[l][:, None, :],
                      w_down[l], b_down[l][:, None, :])
        tm = min(COMB_TM, S)
        dest3 = dest.reshape(TOP_K, T // tm, tm).transpose(1, 0, 2).reshape(T // tm, 1, TOP_K * tm)
        P = y_rows.shape[0] // ROW_TILES
        x = _combine(dest3, y_rows.reshape(P, ROW_TILES, LANES), x1.reshape(T, D), gt2, final_norm_g[None],
                     S).reshape(B, S, D)
    return x
```

```python
import functools

import jax
import jax.numpy as jnp
from jax import lax
from jax.experimental import pallas as pl
from jax.experimental.pallas import tpu as pltpu

F32 = jnp.float32
BF16 = jnp.bfloat16
I32 = jnp.int32
HI = lax.Precision.HIGHEST

D_MODEL = 1024
CONV_CH = 512
CONV_WIDTH = 31
NSA_HEADS = 8
KV_GROUPS = 2
Q_PER_G = NSA_HEADS // KV_GROUPS
HEAD_DIM = 64
NSA_WIDTH = NSA_HEADS * HEAD_DIM
KV_W = KV_GROUPS * HEAD_DIM
ROT_DIM = HEAD_DIM // 4
ROT_HALF = ROT_DIM // 2
ROPE_THETA = 500000.0
CMP_LEN = 32
CMP_STRIDE = 16
CMP_HIDDEN = 128
SEL_BLK = 64
SEL_TOPN = 16
WINDOW = 512
N_EXPERTS = 32
TOP_K = 4
D_FF = 1024
SWIGLU_ALPHA = 1.702
SWIGLU_LIMIT = 7.0
NORM_EPS = 1e-5
NEG_INF = -1e30
FORCE_SCORE = 1e9
LOG2_E = 1.4426950408889634

LANES = 128
SUBLANES = 8
ROW_TILES = D_MODEL // LANES

GATE_ROWS = 16

INPROJ_TM = 512
CONV_TR = 512
CONV_HALO = 32
ATTN_TQ = 256
ATTN_TK = 256
OUT_TM = 512
FFN_BM = 512
COMB_TM = 256
ROUTE_W = 16


def _rms(x, g):
    return x * lax.rsqrt(jnp.mean(x * x, axis=-1, keepdims=True) + NORM_EPS) * g


def _split_bf16(a):
    hi = a.astype(BF16)
    return hi, (a - hi.astype(F32)).astype(BF16)


def _dot3(a, b, dims=(((1,), (0,)), ((), ()))):
    ah, al = _split_bf16(a)
    bh, bl = _split_bf16(b)
    d = lambda x, y: lax.dot_general(x, y, dims, preferred_element_type=F32)
    return d(ah, bh) + d(al, bh) + d(ah, bl)


def _ada_kernel(c_ref, w_ref, b_ref, o_ref):
    c = c_ref[...]
    ca = c * jax.nn.sigmoid(c)
    o_ref[...] = jnp.dot(ca, w_ref[...], preferred_element_type=F32, precision=HI) + b_ref[...]


def _adaln(c, w, b):
    B = c.shape[0]
    D = D_MODEL
    return pl.pallas_call(
        _ada_kernel,
        grid=(6,),
        in_specs=[pl.BlockSpec((B, D), lambda j: (0, 0)),
                  pl.BlockSpec((D, D), lambda j: (0, j)),
                  pl.BlockSpec((1, D), lambda j: (0, j))],
        out_specs=pl.BlockSpec((B, D), lambda j: (0, j)),
        out_shape=jax.ShapeDtypeStruct((B, 6 * D), F32),
        name="adaln",
    )(c, w, b)


def _inproj_kernel(x_ref, sc_ref, sh_ref, g_ref, wn_ref, wt_ref, rc_ref, rs1_ref, rs2_ref, ct_ref, st_ref,
                   u_ref, kc_ref, vc_ref, ks_ref, kw_ref, qt_ref, vst_ref, vwt_ref, gt_ref, stage_ref):
    tm = x_ref.shape[1]
    tq, tk = ATTN_TQ, ATTN_TK
    h = (_rms(x_ref[0], g_ref[...]) * (1.0 + sc_ref[0]) + sh_ref[0]).astype(BF16)

    p = jnp.dot(h, wn_ref[...], preferred_element_type=F32)
    u_ref[0] = p[:, 0:CONV_CH] * jax.nn.sigmoid(p[:, CONV_CH:2 * CONV_CH])
    c0 = 2 * CONV_CH
    rc, rs1, rs2 = rc_ref[...], rs1_ref[...], rs2_ref[...]
    for ref, roped in ((kc_ref, True), (ks_ref, True), (kw_ref, True), (vc_ref, False)):
        v = p[:, c0:c0 + KV_W]
        if roped:
            v = v * rc + pltpu.roll(v, ROT_HALF, 1) * rs1 + pltpu.roll(v, KV_W - ROT_HALF, 1) * rs2
        c0 += KV_W
        if ref is kc_ref or ref is vc_ref:
            stage_ref[...] = v
            left = lax.broadcasted_iota(I32, (tm // CMP_STRIDE, KV_W), 1) < HEAD_DIM
            pieces = [stage_ref[pl.ds(tl, tm // CMP_STRIDE, stride=CMP_STRIDE), :] for tl in range(CMP_STRIDE)]
            for gg in range(KV_GROUPS):
                cols = []
                for tl in range(0, CMP_STRIDE, 2):
                    a, b = pieces[tl], pieces[tl + 1]
                    if gg == 0:
                        cols.append(jnp.where(left, a, pltpu.roll(b, HEAD_DIM, 1)))
                    else:
                        cols.append(jnp.where(left, pltpu.roll(a, HEAD_DIM, 1), b))
                ref[0, gg] = jnp.concatenate(cols, axis=1)
            continue
        for gg in range(KV_GROUPS):
            ref[0, gg] = v[:, HEAD_DIM * gg:HEAD_DIM * (gg + 1)].astype(ref.dtype)

    pt = lax.dot_general(wt_ref[...], h, (((1,), (1,)), ((), ())), preferred_element_type=F32)
    cos_t, sin_t = ct_ref[...], st_ref[...]
    scale = HEAD_DIM ** -0.5 * LOG2_E
    for hh in range(NSA_HEADS):
        blk = pt[HEAD_DIM * hh:HEAD_DIM * (hh + 1), :]
        x1, x2 = blk[0:ROT_HALF], blk[ROT_HALF:ROT_DIM]
        qh = (jnp.concatenate([x1 * cos_t - x2 * sin_t, x2 * cos_t + x1 * sin_t, blk[ROT_DIM:]], axis=0)
              * scale).astype(BF16)
        gg, n = divmod(hh, Q_PER_G)
        for j in range(tm // tq):
            qt_ref[0, gg, j, :, n * tq:(n + 1) * tq] = qh[:, j * tq:(j + 1) * tq]
    r0 = NSA_WIDTH
    for ref in (vst_ref, vwt_ref):
        for gg in range(KV_GROUPS):
            blk = pt[r0 + HEAD_DIM * gg:r0 + HEAD_DIM * (gg + 1), :].astype(BF16)
            for j in range(tm // tk):
                ref[0, gg, j] = blk[:, j * tk:(j + 1) * tk]
        r0 += KV_W
    for gg in range(KV_GROUPS):
        gt_ref[0, gg] = jax.nn.sigmoid(pt[r0 + GATE_ROWS * gg:r0 + GATE_ROWS * (gg + 1), :])


def _inproj(x, sc, sh, g, wn, wt, rc, rs1, rs2, cos_t, sin_t):
    B, S, D = x.shape
    tm = min(INPROJ_TM, S)
    tq, tk = ATTN_TQ, ATTN_TK
    G = KV_GROUPS
    kv = lambda dt: jax.ShapeDtypeStruct((B, G, S, HEAD_DIM), dt)
    kv_spec = pl.BlockSpec((1, G, tm, HEAD_DIM), lambda b, i: (b, 0, i, 0))
    chunk = CMP_STRIDE * HEAD_DIM
    ck_shape = jax.ShapeDtypeStruct((B, G, S // CMP_STRIDE, chunk), F32)
    ck_spec = pl.BlockSpec((1, G, tm // CMP_STRIDE, chunk), lambda b, i: (b, 0, i, 0))
    vt_shape = jax.ShapeDtypeStruct((B, G, S // tk, HEAD_DIM, tk), BF16)
    vt_spec = pl.BlockSpec((1, G, tm // tk, HEAD_DIM, tk), lambda b, i: (b, 0, i, 0, 0))
    row = pl.BlockSpec((1, 1, D), lambda b, i: (b, 0, 0))
    tab = pl.BlockSpec((tm, LANES), lambda b, i: (i, 0))
    tab_t = pl.BlockSpec((ROT_HALF, tm), lambda b, i: (0, i))
    return pl.pallas_call(
        _inproj_kernel,
        grid=(B, S // tm),
        in_specs=[pl.BlockSpec((1, tm, D), lambda b, i: (b, i, 0)), row, row,
                  pl.BlockSpec((1, D), lambda b, i: (0, 0)),
                  pl.BlockSpec(wn.shape, lambda b, i: (0, 0)),
                  pl.BlockSpec(wt.shape, lambda b, i: (0, 0)),
                  tab, tab, tab, tab_t, tab_t],
        out_specs=[pl.BlockSpec((1, tm, CONV_CH), lambda b, i: (b, i, 0)),
                   ck_spec, ck_spec, kv_spec, kv_spec,
                   pl.BlockSpec((1, G, tm // tq, HEAD_DIM, Q_PER_G * tq), lambda b, i: (b, 0, i, 0, 0)),
                   vt_spec, vt_spec,
                   pl.BlockSpec((1, G, GATE_ROWS, tm), lambda b, i: (b, 0, 0, i))],
        out_shape=[jax.ShapeDtypeStruct((B, S, CONV_CH), F32),
                   ck_shape, ck_shape, kv(BF16), kv(BF16),
                   jax.ShapeDtypeStruct((B, G, S // tq, HEAD_DIM, Q_PER_G * tq), BF16),
                   vt_shape, vt_shape,
                   jax.ShapeDtypeStruct((B, G, GATE_ROWS, S), F32)],
        scratch_shapes=[pltpu.VMEM((tm, KV_W), F32)],
        compiler_params=pltpu.CompilerParams(dimension_semantics=("parallel", "parallel")),
        name="inproj",
    )(x, sc, sh, g, wn, wt, rc, rs1, rs2, cos_t, sin_t)


def _conv_kernel(prev_ref, cur_ref, w_ref, cb_ref, lg_ref, lb_ref, on_ref, o_ref, pad_ref, win_ref):
    tr = cur_ref.shape[1]
    first = pl.program_id(1) == 0
    halo = prev_ref[0, tr - CONV_HALO:tr, :]
    pad_ref[0:CONV_HALO, :] = jnp.where(first, 0.0, halo)
    pad_ref[CONV_HALO:CONV_HALO + tr, :] = cur_ref[0]
    off = CONV_HALO - (CONV_WIDTH - 1)
    acc = jnp.zeros((tr, CONV_CH), F32)
    for b in range(SUBLANES):
        taps = range(b, CONV_WIDTH, SUBLANES)
        rows = tr + SUBLANES * (len(taps) - 1)
        win_ref[b, 0:rows, :] = pad_ref[off + b:off + b + rows, :]
        for a, k in enumerate(taps):
            acc = acc + win_ref[b, SUBLANES * a:SUBLANES * a + tr, :] * w_ref[k:k + 1, :]
    y = acc + cb_ref[...]
    mu = jnp.mean(y, axis=-1, keepdims=True)
    yc = y - mu
    var = jnp.mean(yc * yc, axis=-1, keepdims=True)
    yn = yc * lax.rsqrt(var + NORM_EPS) * lg_ref[...] + lb_ref[...]
    s = yn * jax.nn.sigmoid(yn)
    o_ref[0] = _rms(s, on_ref[...]).astype(o_ref.dtype)


def _conv(u, w, cb, lg, lb, on):
    B, S, C = u.shape
    tr = min(CONV_TR, S)
    vec = pl.BlockSpec((1, C), lambda b, i: (0, 0))
    return pl.pallas_call(
        _conv_kernel,
        grid=(B, S // tr),
        in_specs=[pl.BlockSpec((1, tr, C), lambda b, i: (b, jnp.maximum(i - 1, 0), 0)),
                  pl.BlockSpec((1, tr, C), lambda b, i: (b, i, 0)),
                  pl.BlockSpec((CONV_WIDTH, C), lambda b, i: (0, 0)),
                  vec, vec, vec, vec],
        out_specs=pl.BlockSpec((1, tr, C), lambda b, i: (b, i, 0)),
        out_shape=jax.ShapeDtypeStruct((B, S, C), BF16),
        scratch_shapes=[pltpu.VMEM((CONV_HALO + tr, C), F32),
                        pltpu.VMEM((SUBLANES, tr + SUBLANES * ((CONV_WIDTH - 1) // SUBLANES), C), F32)],
        compiler_params=pltpu.CompilerParams(dimension_semantics=("parallel", "parallel")),
        name="conv",
    )(u, u, w, cb, lg, lb, on)


def _cmp_kernel(kx_ref, vx_ref, pek_ref, pev_ref, kw1_ref, kw2_ref, vw1_ref, vw2t_ref, ko_ref, vo_ref):
    nc = kx_ref.shape[2]
    half = kx_ref.shape[3]
    nt = (((1,), (1,)), ((), ()))
    for x_ref, pe_ref, w1_ref, w2_ref, o_ref, transposed in ((kx_ref, pek_ref, kw1_ref, kw2_ref, ko_ref, False),
                                                             (vx_ref, pev_ref, vw1_ref, vw2t_ref, vo_ref, True)):
        w1 = w1_ref[...]
        pe = jnp.broadcast_to(pe_ref[...], (SUBLANES, 2 * half))
        pe_proj = _dot3(pe, w1)[0:1]
        for gg in range(KV_GROUPS):
            xg = x_ref[0, gg]
            first = _dot3(xg, w1[0:half])
            second = _dot3(xg, w1[half:2 * half])
            hid = first + pltpu.roll(second, nc - 1, 0) + pe_proj
            hid = hid * jax.nn.sigmoid(hid)
            if transposed:
                o_ref[0, gg] = _dot3(w2_ref[...], hid, nt)
            else:
                o_ref[0, gg] = _dot3(hid, w2_ref[...])


def _compress(kx, vx, pek, pev, kw1, kw2, vw1, vw2t):
    B, G, NC, W = kx.shape
    xs = pl.BlockSpec((1, G, NC, W), lambda b: (b, 0, 0, 0))
    full = lambda a: pl.BlockSpec(a.shape, lambda b: (0,) * a.ndim)
    return pl.pallas_call(
        _cmp_kernel,
        grid=(B,),
        in_specs=[xs, xs, full(pek), full(pev), full(kw1), full(kw2), full(vw1), full(vw2t)],
        out_specs=[pl.BlockSpec((1, G, NC, HEAD_DIM), lambda b: (b, 0, 0, 0)),
                   pl.BlockSpec((1, G, HEAD_DIM, NC), lambda b: (b, 0, 0, 0))],
        out_shape=[jax.ShapeDtypeStruct((B, G, NC, HEAD_DIM), F32),
                   jax.ShapeDtypeStruct((B, G, HEAD_DIM, NC), F32)],
        compiler_params=pltpu.CompilerParams(dimension_semantics=("parallel",)),
        name="compress",
    )(kx, vx, pek, pev, kw1, kw2, vw1, vw2t)


def _attn_kernel(qt_ref, kc_ref, vct_ref, ks_ref, vst_ref, kw_ref, vwt_ref, gt_ref, ovt_ref, o_ref, sel_ref):
    tq, tk = ATTN_TQ, ATTN_TK
    R = Q_PER_G * tq
    per_tile = tk // SEL_BLK
    S = ks_ref.shape[2]
    ncp = kc_ref.shape[2]
    nsel = ovt_ref.shape[0]
    qi = pl.program_id(2)
    q0 = qi * tq
    qt = qt_ref[0, 0, 0]
    t_row = q0 + lax.broadcasted_iota(I32, (1, tq), 1)
    heads = lambda a: jnp.concatenate([a] * Q_PER_G, axis=1)

    sc = jnp.dot(kc_ref[0, 0].astype(BF16), qt, preferred_element_type=F32)
    c_io = lax.broadcasted_iota(I32, (ncp, tq), 0)
    m_c = (c_io * CMP_STRIDE + (CMP_LEN - 1) <= t_row) & (c_io < ncp - 1)
    scb = sc + heads(jnp.where(m_c, 0.0, NEG_INF))
    e = jnp.exp2(scb - jnp.max(scb, axis=0, keepdims=True)) * heads(jnp.where(m_c, 1.0, 0.0))
    den = jnp.sum(e, axis=0, keepdims=True)
    pc = e / jnp.where(den > 0.0, den, 1.0)
    o_cmp = jnp.dot(vct_ref[0, 0].astype(BF16), pc.astype(BF16), preferred_element_type=F32)

    psum = pc[:, 0:tq]
    for n in range(1, Q_PER_G):
        psum = psum + pc[:, n * tq:(n + 1) * tq]
    imp = _dot3(ovt_ref[...], psum)
    j_io = lax.broadcasted_iota(I32, (nsel, tq), 0)
    cur = t_row // SEL_BLK
    valid = j_io * SEL_BLK <= t_row
    forced = (j_io == 0) | (j_io == cur) | (j_io == cur - 1)
    score = jnp.where(valid, jnp.where(forced, FORCE_SCORE, imp), NEG_INF)
    rank = jnp.zeros((nsel, tq), F32)
    for i in range(nsel):
        row = score[i:i + 1, :]
        tie = jnp.where(j_io > i, 1.0, 0.0)
        rank = rank + jnp.where(row > score, 1.0, jnp.where(row == score, tie, 0.0))
    sel_bias = jnp.where(rank < float(min(SEL_TOPN, nsel)), 0.0, NEG_INF)
    sel_ref[...] = jnp.zeros(sel_ref.shape, F32)
    for jj in range(S // tk):
        sel_ref[jj, 0:per_tile, :] = sel_bias[jj * per_tile:(jj + 1) * per_tile, :]

    k_io = lax.broadcasted_iota(I32, (tk, tq), 0)

    def flash_step(k_ref, vt_ref, kj, bias, carry):
        m, l, acc = carry
        k0 = pl.multiple_of(kj * tk, tk)
        s = jnp.dot(k_ref[0, 0, pl.ds(k0, tk), :], qt, preferred_element_type=F32) + heads(bias)
        m_new = jnp.maximum(m, jnp.max(s, axis=0, keepdims=True))
        alpha = jnp.exp2(m - m_new)
        p = jnp.exp2(s - m_new)
        l = alpha * l + jnp.sum(p, axis=0, keepdims=True)
        acc = alpha * acc + jnp.dot(vt_ref[0, 0, kj], p.astype(BF16), preferred_element_type=F32)
        return m_new, l, acc

    init = (jnp.full((1, R), NEG_INF, F32), jnp.zeros((1, R), F32), jnp.zeros((HEAD_DIM, R), F32))

    def slc_body(kj, carry):
        blocks = sel_ref[kj]
        bias = jnp.concatenate([jnp.broadcast_to(blocks[b:b + 1, :], (SEL_BLK, tq)) for b in range(per_tile)], axis=0)
        bias = jnp.where(kj * tk + k_io <= t_row, bias, NEG_INF)
        return flash_step(ks_ref, vst_ref, kj, bias, carry)

    def win_body(kj, carry):
        rel = t_row - (kj * tk + k_io)
        bias = jnp.where((rel >= 0) & (rel < WINDOW), 0.0, NEG_INF)
        return flash_step(kw_ref, vwt_ref, kj, bias, carry)

    n_slc = (q0 + tq + tk - 1) // tk
    lo_tile = jnp.maximum(q0 - (WINDOW - 1), 0) // tk
    n_pair = lo_tile // 2

    def pair_body(i, carry):
        return slc_body(2 * i, carry[0]), slc_body(2 * i + 1, carry[1])

    st_a, st_b = lax.fori_loop(0, n_pair, pair_body, (init, init))
    st_a = lax.fori_loop(2 * n_pair, lo_tile, slc_body, st_a)

    def both_body(kj, carry):
        return slc_body(kj, carry[0]), win_body(kj, carry[1])

    st_a, (_, l_w, acc_w) = lax.fori_loop(lo_tile, n_slc, both_body, (st_a, init))
    m_s = jnp.maximum(st_a[0], st_b[0])
    w_a, w_b = jnp.exp2(st_a[0] - m_s), jnp.exp2(st_b[0] - m_s)
    l_s = w_a * st_a[1] + w_b * st_b[1]
    acc_s = w_a * st_a[2] + w_b * st_b[2]


    gt = gt_ref[0, 0]
    o_slc = acc_s / l_s
    o_win = acc_w / l_w
    outs = []
    for n in range(Q_PER_G):
        cols = slice(n * tq, (n + 1) * tq)
        outs.append(gt[3 * n:3 * n + 1, :] * o_cmp[:, cols] + gt[3 * n + 1:3 * n + 2, :] * o_slc[:, cols]
                    + gt[3 * n + 2:3 * n + 3, :] * o_win[:, cols])
    o_ref[0] = jnp.concatenate(outs, axis=0).T


def _attention(qt, kc, vct, ks, vst, kw, vwt, gt, overlap_t):
    B, G, NQT, _, R = qt.shape
    S = ks.shape[2]
    tq, tk = ATTN_TQ, ATTN_TK
    ncp = kc.shape[2]
    kfull = pl.BlockSpec((1, 1, S, HEAD_DIM), lambda b, g, i: (b, g, 0, 0))
    vfull = pl.BlockSpec((1, 1, S // tk, HEAD_DIM, tk), lambda b, g, i: (b, g, 0, 0, 0))
    return pl.pallas_call(
        _attn_kernel,
        grid=(B, G, NQT),
        in_specs=[pl.BlockSpec((1, 1, 1, HEAD_DIM, R), lambda b, g, i: (b, g, i, 0, 0)),
                  pl.BlockSpec((1, 1, ncp, HEAD_DIM), lambda b, g, i: (b, g, 0, 0)),
                  pl.BlockSpec((1, 1, HEAD_DIM, ncp), lambda b, g, i: (b, g, 0, 0)),
                  kfull, vfull, kfull, vfull,
                  pl.BlockSpec((1, 1, GATE_ROWS, tq), lambda b, g, i: (b, g, 0, i)),
                  pl.BlockSpec(overlap_t.shape, lambda b, g, i: (0, 0))],
        out_specs=pl.BlockSpec((1, tq, Q_PER_G * HEAD_DIM), lambda b, g, i: (b, i, g)),
        out_shape=jax.ShapeDtypeStruct((B, S, NSA_WIDTH), F32),
        scratch_shapes=[pltpu.VMEM((S // tk, SUBLANES, tq), F32)],
        compiler_params=pltpu.CompilerParams(dimension_semantics=("parallel", "parallel", "arbitrary")),
        name="attn",
    )(qt, kc, vct, ks, vst, kw, vwt, gt, overlap_t)


def _outproj_kernel(x_ref, cv_ref, nsa_ref, on_ref, w_ref, gt1_ref, g2_ref, sc2_ref, sh2_ref, wrh_ref, wrl_ref, br_ref,
                    tri_ref, x1_ref, h2_ref, rt_ref, cnt_ref, run_ref):
    tm = x_ref.shape[1]

    @pl.when((pl.program_id(0) == 0) & (pl.program_id(1) == 0))
    def _():
        run_ref[...] = jnp.zeros(run_ref.shape, F32)

    nn = _rms(nsa_ref[0], on_ref[...]).astype(BF16)
    y = jnp.dot(jnp.concatenate([cv_ref[0], nn], axis=1), w_ref[...], preferred_element_type=F32)
    x1 = x_ref[0] + gt1_ref[0] * y
    x1_ref[0] = x1
    h2 = _rms(x1, g2_ref[...]) * (1.0 + sc2_ref[0]) + sh2_ref[0]
    for s in range(ROW_TILES):
        h2_ref[pl.ds(s, tm, stride=ROW_TILES), :] = h2[:, s * LANES:(s + 1) * LANES]
    nt = (((1,), (1,)), ((), ()))
    h_hi = h2.astype(BF16)
    h_lo = (h2 - h_hi.astype(F32)).astype(BF16)
    logits = (lax.dot_general(wrh_ref[...], h_hi, nt, preferred_element_type=F32)
              + lax.dot_general(wrh_ref[...], h_lo, nt, preferred_element_type=F32)
              + lax.dot_general(wrl_ref[...], h_hi, nt, preferred_element_type=F32)) + br_ref[...]
    eio = lax.broadcasted_iota(I32, (N_EXPERTS, tm), 0).astype(F32)
    vals, idxs = [], []
    for _ in range(TOP_K):
        m = jnp.max(logits, axis=0, keepdims=True)
        ix = jnp.min(jnp.where(logits == m, eio, float(N_EXPERTS)), axis=0, keepdims=True)
        vals.append(m)
        idxs.append(ix)
        logits = jnp.where(eio == ix, -jnp.inf, logits)
    es = [jnp.exp(v - vals[0]) for v in vals]
    den = es[0] + es[1] + es[2] + es[3]
    hot = jnp.zeros((N_EXPERTS, tm), F32)
    for r in range(TOP_K):
        hot = hot + jnp.where(eio == idxs[r], 1.0, 0.0)
    before = run_ref[...] + jnp.dot(hot.astype(BF16), tri_ref[...], preferred_element_type=F32)
    ranks = [jnp.sum(jnp.where(eio == idxs[r], before, 0.0), axis=0, keepdims=True) for r in range(TOP_K)]
    run_ref[...] = run_ref[...] + jnp.sum(hot, axis=1, keepdims=True)
    cnt_ref[...] = run_ref[...]
    rio = lax.broadcasted_iota(I32, (ROUTE_W, tm), 0)
    out = jnp.zeros((ROUTE_W, tm), F32)
    for r in range(TOP_K):
        out = jnp.where(rio == r, idxs[r], out)
        out = jnp.where(rio == TOP_K + r, es[r] / den, out)
        out = jnp.where(rio == 2 * TOP_K + r, ranks[r], out)
    rt_ref[...] = out


def _outproj(x, cv, nsa, on, w, gt1, g2, sc2, sh2, wr, br):
    B, S, D = x.shape
    tm = min(OUT_TM, S)
    nt = S // tm
    tri = (jnp.arange(tm)[:, None] < jnp.arange(tm)[None, :]).astype(BF16)
    wr_t = wr.T
    wr_hi = wr_t.astype(BF16)
    wr_lo = (wr_t - wr_hi.astype(F32)).astype(BF16)
    row = pl.BlockSpec((1, 1, D), lambda b, i: (b, 0, 0))
    vec = lambda n: pl.BlockSpec((1, n), lambda b, i: (0, 0))
    col = pl.BlockSpec((N_EXPERTS, 1), lambda b, i: (0, 0))
    wr_spec = pl.BlockSpec((N_EXPERTS, D), lambda b, i: (0, 0))
    return pl.pallas_call(
        _outproj_kernel,
        grid=(B, nt),
        in_specs=[pl.BlockSpec((1, tm, D), lambda b, i: (b, i, 0)),
                  pl.BlockSpec((1, tm, CONV_CH), lambda b, i: (b, i, 0)),
                  pl.BlockSpec((1, tm, NSA_WIDTH), lambda b, i: (b, i, 0)),
                  vec(NSA_WIDTH),
                  pl.BlockSpec((D, D), lambda b, i: (0, 0)),
                  row, vec(D), row, row,
                  wr_spec, wr_spec, col,
                  pl.BlockSpec((tm, tm), lambda b, i: (0, 0))],
        out_specs=[pl.BlockSpec((1, tm, D), lambda b, i: (b, i, 0)),
                   pl.BlockSpec((tm * ROW_TILES, LANES), lambda b, i: (b * nt + i, 0)),
                   pl.BlockSpec((ROUTE_W, tm), lambda b, i: (0, b * nt + i)),
                   col],
        out_shape=[jax.ShapeDtypeStruct((B, S, D), F32),
                   jax.ShapeDtypeStruct((B * S * ROW_TILES, LANES), F32),
                   jax.ShapeDtypeStruct((ROUTE_W, B * S), F32),
                   jax.ShapeDtypeStruct((N_EXPERTS, 1), F32)],
        scratch_shapes=[pltpu.VMEM((N_EXPERTS, 1), F32)],
        compiler_params=pltpu.CompilerParams(dimension_semantics=("arbitrary", "arbitrary")),
        name="outproj",
    )(x, cv, nsa, on, w, gt1, g2, sc2, sh2, wr_hi, wr_lo, br.reshape(N_EXPERTS, 1), tri)


def _issue_rows(idx_ref, rows, src_hbm, dst, slot, sem):
    for r in rows:
        pltpu.make_async_copy(src_hbm.at[idx_ref[0, 0, r]],
                              dst.at[slot, pl.ds(r * ROW_TILES, ROW_TILES), :],
                              sem.at[slot]).start(priority=r % 2)


def _wait_rows(dst, slot, sem):
    pltpu.make_async_copy(dst.at[slot], dst.at[slot], sem.at[slot]).wait()


def _rows_2d(buf, slot, base, n):
    return jnp.concatenate(
        [buf[slot, pl.ds(base * ROW_TILES + s, n, stride=ROW_TILES), :] for s in range(ROW_TILES)], axis=1)


def _ffn_kernel(be_ref, nb_ref, ta_ref, tb_ref, h2_hbm, gate_ref, wg_ref, bg_ref, wu_ref, bu_ref, wd_ref, bd_ref,
                o_ref, xbuf, wgb, wub, wdb, sem):
    bm = ta_ref.shape[2]
    i = pl.program_id(0)
    nb = nb_ref[0]
    slot = i % 2

    @pl.when((i == 0) | (be_ref[i] != be_ref[jnp.maximum(i - 1, 0)]))
    def _():
        wgb[...] = wg_ref[0].astype(BF16)
        wub[...] = wu_ref[0].astype(BF16)
        wdb[...] = wd_ref[0].astype(BF16)

    @pl.when(i == 0)
    def _():
        _issue_rows(ta_ref, range(bm), h2_hbm, xbuf, 0, sem)

    @pl.when(i + 1 < nb)
    def _():
        _issue_rows(tb_ref, range(bm), h2_hbm, xbuf, 1 - slot, sem)

    @pl.when(i < nb)
    def _():
        _wait_rows(xbuf, slot, sem)
        x = _rows_2d(xbuf, slot, 0, bm).astype(BF16)
        g = jnp.dot(x, wgb[...], preferred_element_type=F32) + bg_ref[0]
        u = jnp.dot(x, wub[...], preferred_element_type=F32) + bu_ref[0]
        g = jnp.minimum(g, SWIGLU_LIMIT)
        u = jnp.clip(u, -SWIGLU_LIMIT, SWIGLU_LIMIT)
        act = g * jax.nn.sigmoid(SWIGLU_ALPHA * g) * (u + 1.0)
        y = (jnp.dot(act.astype(BF16), wdb[...], preferred_element_type=F32) + bd_ref[0]) * gate_ref[...]
        for s in range(ROW_TILES):
            o_ref[pl.ds(s, bm, stride=ROW_TILES), :] = y[:, s * LANES:(s + 1) * LANES]

    @pl.when(i >= nb)
    def _():
        o_ref[...] = jnp.zeros(o_ref.shape, o_ref.dtype)


def _ffn(block_e, nb_used, buf_tok, h2_rows, buf_gate, wg, bg, wu, bu, wd, bd):
    NB = block_e.shape[0]
    bm = FFN_BM
    D, F = D_MODEL, D_FF
    tok3 = buf_tok.reshape(NB, 1, bm)
    wspec = lambda r, c: pl.BlockSpec((1, r, c), lambda i, be, nb: (be[i], 0, 0))
    vmem_limit = 2 * 3 * D * F * 4 + 3 * D * F * 2 + 4 * bm * D * 4 + 6 * bm * F * 4
    return pl.pallas_call(
        _ffn_kernel,
        grid_spec=pltpu.PrefetchScalarGridSpec(
            num_scalar_prefetch=2,
            grid=(NB,),
            in_specs=[pl.BlockSpec((1, 1, bm), lambda i, be, nb: (i, 0, 0), memory_space=pltpu.SMEM),
                      pl.BlockSpec((1, 1, bm), lambda i, be, nb: (jnp.minimum(i + 1, NB - 1), 0, 0),
                                   memory_space=pltpu.SMEM),
                      pl.BlockSpec(memory_space=pl.ANY),
                      pl.BlockSpec((bm, 1), lambda i, be, nb: (i, 0)),
                      wspec(D, F), wspec(1, F), wspec(D, F), wspec(1, F), wspec(F, D), wspec(1, D)],
            out_specs=pl.BlockSpec((bm * ROW_TILES, LANES), lambda i, be, nb: (i, 0)),
            scratch_shapes=[pltpu.VMEM((2, bm * ROW_TILES, LANES), F32),
                            pltpu.VMEM((D, F), BF16), pltpu.VMEM((D, F), BF16), pltpu.VMEM((F, D), BF16),
                            pltpu.SemaphoreType.DMA((2,))]),
        out_shape=jax.ShapeDtypeStruct((NB * bm * ROW_TILES, LANES), F32),
        compiler_params=pltpu.CompilerParams(dimension_semantics=("arbitrary",),
                                             vmem_limit_bytes=vmem_limit),
        name="ffn",
    )(block_e, nb_used, tok3, tok3, h2_rows, buf_gate, wg, bg, wu, bu, wd, bd)


def _combine_kernel(da_ref, db_ref, y_hbm, x1_ref, gt2_ref, fg_ref, o_ref, buf, sem):
    tm = x1_ref.shape[0]
    n = TOP_K * tm
    i = pl.program_id(0)
    slot = i % 2

    @pl.when(i == 0)
    def _():
        _issue_rows(da_ref, range(n), y_hbm, buf, 0, sem)

    @pl.when(i + 1 < pl.num_programs(0))
    def _():
        _issue_rows(db_ref, range(n), y_hbm, buf, 1 - slot, sem)

    _wait_rows(buf, slot, sem)
    y = _rows_2d(buf, slot, 0, tm)
    for k in range(1, TOP_K):
        y = y + _rows_2d(buf, slot, k * tm, tm)
    x2 = x1_ref[...] + gt2_ref[0] * y
    o_ref[...] = _rms(x2, fg_ref[...])


def _combine(dest3, y_rows, x1, gt2, fg, S):
    T, D = x1.shape
    tm = min(COMB_TM, S)
    NT = T // tm
    per_b = S // tm
    n = TOP_K * tm
    return pl.pallas_call(
        _combine_kernel,
        grid=(NT,),
        in_specs=[pl.BlockSpec((1, 1, n), lambda i: (i, 0, 0), memory_space=pltpu.SMEM),
                  pl.BlockSpec((1, 1, n), lambda i: (jnp.minimum(i + 1, NT - 1), 0, 0), memory_space=pltpu.SMEM),
                  pl.BlockSpec(memory_space=pl.ANY),
                  pl.BlockSpec((tm, D), lambda i: (i, 0)),
                  pl.BlockSpec((1, 1, D), lambda i: (i // per_b, 0, 0)),
                  pl.BlockSpec((1, D), lambda i: (0, 0))],
        out_specs=pl.BlockSpec((tm, D), lambda i: (i, 0)),
        out_shape=jax.ShapeDtypeStruct((T, D), F32),
        scratch_shapes=[pltpu.VMEM((2, n * ROW_TILES, LANES), F32), pltpu.SemaphoreType.DMA((2,))],
        compiler_params=pltpu.CompilerParams(dimension_semantics=("arbitrary",)),
        name="combine",
    )(dest3, dest3, y_rows, x1, gt2, fg)


def _rope_tables(S):
    inv = ROPE_THETA ** (-jnp.arange(0, ROT_DIM, 2, dtype=F32) / ROT_DIM)
    ang = jnp.arange(S, dtype=F32)[:, None] * inv[None, :]
    cos, sin = jnp.cos(ang), jnp.sin(ang)
    d = jnp.arange(KV_W) % HEAD_DIM
    first, second = d < ROT_HALF, (d >= ROT_HALF) & (d < ROT_DIM)
    cos_l = cos[:, d % ROT_HALF]
    sin_l = sin[:, d % ROT_HALF]
    rc = jnp.where((d < ROT_DIM)[None], cos_l, 1.0)
    rs1 = jnp.where(second[None], sin_l, 0.0)
    rs2 = jnp.where(first[None], -sin_l, 0.0)
    return rc, rs1, rs2, cos.T, sin.T


def _route_plan(route, counts, T):
    bm = FFN_BM
    A = T * TOP_K
    idx = route[0:TOP_K].astype(I32)
    gate = route[TOP_K:2 * TOP_K]
    rank = route[2 * TOP_K:3 * TOP_K].astype(I32)
    keys = idx * T + jnp.arange(T, dtype=I32)[None, :]
    skey, sgate = lax.sort((keys.reshape(A), gate.reshape(A)), num_keys=1)
    counts = counts.astype(I32)
    starts = jnp.cumsum(counts) - counts
    padded = (counts + bm - 1) // bm * bm
    pends = jnp.cumsum(padded)
    pstarts = pends - padded
    P = (A + N_EXPERTS * bm + bm - 1) // bm * bm
    NB = P // bm
    blk0 = jnp.arange(NB, dtype=I32) * bm
    block_e = jnp.minimum(jnp.sum((pends[None, :] <= blk0[:, None]).astype(I32), axis=1), N_EXPERTS - 1)
    r = (blk0 - pstarts[block_e])[:, None] + jnp.arange(bm, dtype=I32)[None, :]
    valid = r < counts[block_e][:, None]
    src = jnp.clip(starts[block_e][:, None] + r, 0, A - 1)
    buf_tok = jnp.where(valid, skey[src] - block_e[:, None] * T, 0)
    buf_gate = jnp.where(valid, sgate[src], 0.0).reshape(P, 1)
    dest = rank
    for e in range(N_EXPERTS):
        dest = dest + jnp.where(idx == e, pstarts[e], 0)
    nb_used = (pends[-1] // bm).astype(I32).reshape(1)
    return block_e, nb_used, buf_tok, buf_gate, dest


def kernel(x, c, norm1_g, norm2_g, w_ada, b_ada, w_in, conv_w, conv_b, conv_ln_g, conv_ln_b, cmp_pe_k, cmp_pe_v,
           cmp_k_w1, cmp_k_w2, cmp_v_w1, cmp_v_w2, out_norm_conv, out_norm_nsa, w_out, w_router, b_router,
           w_gate, b_gate, w_up, b_up, w_down, b_down, final_norm_g):
    B, S, D = x.shape
    T = B * S
    G = KV_GROUPS
    assert D == D_MODEL and S % ATTN_TK == 0 and S % CMP_STRIDE == 0 and KV_W == LANES
    rc, rs1, rs2, cos_t, sin_t = _rope_tables(S)
    n_sel = S // SEL_BLK
    nc = S // CMP_STRIDE
    cstart = jnp.arange(nc) * CMP_STRIDE
    jstart = jnp.arange(n_sel) * SEL_BLK
    overlap_t = ((cstart[None, :] <= jstart[:, None] + SEL_BLK - 1)
                 & (cstart[None, :] + CMP_LEN - 1 >= jstart[:, None])
                 & (jnp.arange(nc)[None, :] < nc - 1)).astype(F32)

    assert w_ada.shape[0] == 1
    for l in range(1):
        mod = _adaln(c, w_ada[l], b_ada[l][None])
        sh1, sc1, gt1, sh2, sc2, gt2 = [m[:, None, :] for m in jnp.split(mod, 6, axis=-1)]

        wl = w_in[l]
        o = 2 * CONV_CH + NSA_WIDTH
        kvc = [wl[:, o + i * KV_W:o + (i + 1) * KV_W] for i in range(6)]
        gl = wl[:, o + 6 * KV_W:]
        per_g = 3 * Q_PER_G
        gpad = [jnp.pad(gl[:, per_g * g:per_g * (g + 1)], ((0, 0), (0, GATE_ROWS - per_g))) for g in range(G)]
        wn = jnp.concatenate([wl[:, :2 * CONV_CH], kvc[0], kvc[2], kvc[4], kvc[1]], axis=1).astype(BF16)
        wt = jnp.concatenate([wl[:, 2 * CONV_CH:o], kvc[3], kvc[5]] + gpad, axis=1).T.astype(BF16)
        u, kc, vc, ks, kw, qt, vst, vwt, gates = _inproj(x, sc1, sh1, norm1_g[l][None], wn, wt,
                                                         rc, rs1, rs2, cos_t, sin_t)

        conv_n = _conv(u, conv_w[l], conv_b[l][None], conv_ln_g[l][None], conv_ln_b[l][None],
                       out_norm_conv[l][None])

        kcmp, vcmp_t = _compress(kc, vc, cmp_pe_k[l].reshape(1, -1), cmp_pe_v[l].reshape(1, -1),
                                 cmp_k_w1[l], cmp_k_w2[l], cmp_v_w1[l], cmp_v_w2[l].T)
        nsa = _attention(qt, kcmp, vcmp_t, ks, vst, kw, vwt, gates, overlap_t)

        x1, h2_rows, route, counts = _outproj(x, conv_n, nsa, out_norm_nsa[l][None], w_out[l].astype(BF16), gt1,
                                              norm2_g[l][None], sc2, sh2, w_router[l], b_router[l][None])

        block_e, nb_used, buf_tok, buf_gate, dest = _route_plan(route, counts[:, 0], T)
        y_rows = _ffn(block_e, nb_used, buf_tok, h2_rows.reshape(T, ROW_TILES, LANES), buf_gate,
                      w_gate[l], b_gate[l][:, None, :], w_up[l], b_up[l][:, None, :],
                      w_down[l], b_down[l][:, None, :])
        tm = min(COMB_TM, S)
        dest3 = dest.reshape(TOP_K, T // tm, tm).transpose(1, 0, 2).reshape(T // tm, 1, TOP_K * tm)
        P = y_rows.shape[0] // ROW_TILES
        x = _combine(dest3, y_rows.reshape(P, ROW_TILES, LANES), x1.reshape(T, D), gt2, final_norm_g[None],
                     S).reshape(B, S, D)
    return x
```

```python
import functools

import jax
import jax.numpy as jnp
from jax import lax
from jax.experimental import pallas as pl
from jax.experimental.pallas import tpu as pltpu

F32 = jnp.float32
BF16 = jnp.bfloat16
I32 = jnp.int32
HI = lax.Precision.HIGHEST

D_MODEL = 1024
CONV_CH = 512
CONV_WIDTH = 31
NSA_HEADS = 8
KV_GROUPS = 2
Q_PER_G = NSA_HEADS // KV_GROUPS
HEAD_DIM = 64
NSA_WIDTH = NSA_HEADS * HEAD_DIM
KV_W = KV_GROUPS * HEAD_DIM
ROT_DIM = HEAD_DIM // 4
ROT_HALF = ROT_DIM // 2
ROPE_THETA = 500000.0
CMP_LEN = 32
CMP_STRIDE = 16
CMP_HIDDEN = 128
SEL_BLK = 64
SEL_TOPN = 16
WINDOW = 512
N_EXPERTS = 32
TOP_K = 4
D_FF = 1024
SWIGLU_ALPHA = 1.702
SWIGLU_LIMIT = 7.0
NORM_EPS = 1e-5
NEG_INF = -1e30
FORCE_SCORE = 1e9
LOG2_E = 1.4426950408889634

LANES = 128
SUBLANES = 8
ROW_TILES = D_MODEL // LANES

GATE_ROWS = 16

INPROJ_TM = 512
CONV_TR = 512
CONV_HALO = 32
ATTN_TQ = 256
ATTN_TK = 512
OUT_TM = 512
FFN_BM = 512
COMB_TM = 256
ROUTE_W = 16


def _rms(x, g):
    return x * lax.rsqrt(jnp.mean(x * x, axis=-1, keepdims=True) + NORM_EPS) * g


def _split_bf16(a):
    hi = a.astype(BF16)
    return hi, (a - hi.astype(F32)).astype(BF16)


def _dot3(a, b, dims=(((1,), (0,)), ((), ()))):
    ah, al = _split_bf16(a)
    bh, bl = _split_bf16(b)
    d = lambda x, y: lax.dot_general(x, y, dims, preferred_element_type=F32)
    return d(ah, bh) + d(al, bh) + d(ah, bl)


def _ada_kernel(c_ref, w_ref, b_ref, o_ref):
    c = c_ref[...]
    ca = c * jax.nn.sigmoid(c)
    o_ref[...] = jnp.dot(ca, w_ref[...], preferred_element_type=F32, precision=HI) + b_ref[...]


def _adaln(c, w, b):
    B = c.shape[0]
    D = D_MODEL
    return pl.pallas_call(
        _ada_kernel,
        grid=(6,),
        in_specs=[pl.BlockSpec((B, D), lambda j: (0, 0)),
                  pl.BlockSpec((D, D), lambda j: (0, j)),
                  pl.BlockSpec((1, D), lambda j: (0, j))],
        out_specs=pl.BlockSpec((B, D), lambda j: (0, j)),
        out_shape=jax.ShapeDtypeStruct((B, 6 * D), F32),
        name="adaln",
    )(c, w, b)


def _inproj_kernel(x_ref, sc_ref, sh_ref, g_ref, wn_ref, wt_ref, rc_ref, rs1_ref, rs2_ref, ct_ref, st_ref,
                   u_ref, kc_ref, vc_ref, ks_ref, kw_ref, qt_ref, vst_ref, vwt_ref, gt_ref, stage_ref):
    tm = x_ref.shape[1]
    tq, tk = ATTN_TQ, ATTN_TK
    h = (_rms(x_ref[0], g_ref[...]) * (1.0 + sc_ref[0]) + sh_ref[0]).astype(BF16)

    p = jnp.dot(h, wn_ref[...], preferred_element_type=F32)
    u_ref[0] = p[:, 0:CONV_CH] * jax.nn.sigmoid(p[:, CONV_CH:2 * CONV_CH])
    c0 = 2 * CONV_CH
    rc, rs1, rs2 = rc_ref[...], rs1_ref[...], rs2_ref[...]
    for ref, roped in ((kc_ref, True), (ks_ref, True), (kw_ref, True), (vc_ref, False)):
        v = p[:, c0:c0 + KV_W]
        if roped:
            v = v * rc + pltpu.roll(v, ROT_HALF, 1) * rs1 + pltpu.roll(v, KV_W - ROT_HALF, 1) * rs2
        c0 += KV_W
        if ref is kc_ref or ref is vc_ref:
            stage_ref[...] = v
            left = lax.broadcasted_iota(I32, (tm // CMP_STRIDE, KV_W), 1) < HEAD_DIM
            pieces = [stage_ref[pl.ds(tl, tm // CMP_STRIDE, stride=CMP_STRIDE), :] for tl in range(CMP_STRIDE)]
            for gg in range(KV_GROUPS):
                cols = []
                for tl in range(0, CMP_STRIDE, 2):
                    a, b = pieces[tl], pieces[tl + 1]
                    if gg == 0:
                        cols.append(jnp.where(left, a, pltpu.roll(b, HEAD_DIM, 1)))
                    else:
                        cols.append(jnp.where(left, pltpu.roll(a, HEAD_DIM, 1), b))
                ref[0, gg] = jnp.concatenate(cols, axis=1)
            continue
        for gg in range(KV_GROUPS):
            ref[0, gg] = v[:, HEAD_DIM * gg:HEAD_DIM * (gg + 1)].astype(ref.dtype)

    pt = lax.dot_general(wt_ref[...], h, (((1,), (1,)), ((), ())), preferred_element_type=F32)
    cos_t, sin_t = ct_ref[...], st_ref[...]
    scale = HEAD_DIM ** -0.5 * LOG2_E
    for hh in range(NSA_HEADS):
        blk = pt[HEAD_DIM * hh:HEAD_DIM * (hh + 1), :]
        x1, x2 = blk[0:ROT_HALF], blk[ROT_HALF:ROT_DIM]
        qh = (jnp.concatenate([x1 * cos_t - x2 * sin_t, x2 * cos_t + x1 * sin_t, blk[ROT_DIM:]], axis=0)
              * scale).astype(BF16)
        gg, n = divmod(hh, Q_PER_G)
        for j in range(tm // tq):
            qt_ref[0, gg, j, :, n * tq:(n + 1) * tq] = qh[:, j * tq:(j + 1) * tq]
    r0 = NSA_WIDTH
    for ref in (vst_ref, vwt_ref):
        for gg in range(KV_GROUPS):
            blk = pt[r0 + HEAD_DIM * gg:r0 + HEAD_DIM * (gg + 1), :].astype(BF16)
            for j in range(tm // tk):
                ref[0, gg, j] = blk[:, j * tk:(j + 1) * tk]
        r0 += KV_W
    for gg in range(KV_GROUPS):
        gt_ref[0, gg] = jax.nn.sigmoid(pt[r0 + GATE_ROWS * gg:r0 + GATE_ROWS * (gg + 1), :])


def _inproj(x, sc, sh, g, wn, wt, rc, rs1, rs2, cos_t, sin_t):
    B, S, D = x.shape
    tm = min(INPROJ_TM, S)
    tq, tk = ATTN_TQ, ATTN_TK
    G = KV_GROUPS
    kv = lambda dt: jax.ShapeDtypeStruct((B, G, S, HEAD_DIM), dt)
    kv_spec = pl.BlockSpec((1, G, tm, HEAD_DIM), lambda b, i: (b, 0, i, 0))
    chunk = CMP_STRIDE * HEAD_DIM
    ck_shape = jax.ShapeDtypeStruct((B, G, S // CMP_STRIDE, chunk), F32)
    ck_spec = pl.BlockSpec((1, G, tm // CMP_STRIDE, chunk), lambda b, i: (b, 0, i, 0))
    vt_shape = jax.ShapeDtypeStruct((B, G, S // tk, HEAD_DIM, tk), BF16)
    vt_spec = pl.BlockSpec((1, G, tm // tk, HEAD_DIM, tk), lambda b, i: (b, 0, i, 0, 0))
    row = pl.BlockSpec((1, 1, D), lambda b, i: (b, 0, 0))
    tab = pl.BlockSpec((tm, LANES), lambda b, i: (i, 0))
    tab_t = pl.BlockSpec((ROT_HALF, tm), lambda b, i: (0, i))
    return pl.pallas_call(
        _inproj_kernel,
        grid=(B, S // tm),
        in_specs=[pl.BlockSpec((1, tm, D), lambda b, i: (b, i, 0)), row, row,
                  pl.BlockSpec((1, D), lambda b, i: (0, 0)),
                  pl.BlockSpec(wn.shape, lambda b, i: (0, 0)),
                  pl.BlockSpec(wt.shape, lambda b, i: (0, 0)),
                  tab, tab, tab, tab_t, tab_t],
        out_specs=[pl.BlockSpec((1, tm, CONV_CH), lambda b, i: (b, i, 0)),
                   ck_spec, ck_spec, kv_spec, kv_spec,
                   pl.BlockSpec((1, G, tm // tq, HEAD_DIM, Q_PER_G * tq), lambda b, i: (b, 0, i, 0, 0)),
                   vt_spec, vt_spec,
                   pl.BlockSpec((1, G, GATE_ROWS, tm), lambda b, i: (b, 0, 0, i))],
        out_shape=[jax.ShapeDtypeStruct((B, S, CONV_CH), F32),
                   ck_shape, ck_shape, kv(BF16), kv(BF16),
                   jax.ShapeDtypeStruct((B, G, S // tq, HEAD_DIM, Q_PER_G * tq), BF16),
                   vt_shape, vt_shape,
                   jax.ShapeDtypeStruct((B, G, GATE_ROWS, S), F32)],
        scratch_shapes=[pltpu.VMEM((tm, KV_W), F32)],
        compiler_params=pltpu.CompilerParams(dimension_semantics=("parallel", "parallel")),
        name="inproj",
    )(x, sc, sh, g, wn, wt, rc, rs1, rs2, cos_t, sin_t)


def _conv_kernel(prev_ref, cur_ref, w_ref, cb_ref, lg_ref, lb_ref, on_ref, o_ref, pad_ref, win_ref):
    tr = cur_ref.shape[1]
    first = pl.program_id(1) == 0
    halo = prev_ref[0, tr - CONV_HALO:tr, :]
    pad_ref[0:CONV_HALO, :] = jnp.where(first, 0.0, halo)
    pad_ref[CONV_HALO:CONV_HALO + tr, :] = cur_ref[0]
    off = CONV_HALO - (CONV_WIDTH - 1)
    acc = jnp.zeros((tr, CONV_CH), F32)
    for b in range(SUBLANES):
        taps = range(b, CONV_WIDTH, SUBLANES)
        rows = tr + SUBLANES * (len(taps) - 1)
        win_ref[b, 0:rows, :] = pad_ref[off + b:off + b + rows, :]
        for a, k in enumerate(taps):
            acc = acc + win_ref[b, SUBLANES * a:SUBLANES * a + tr, :] * w_ref[k:k + 1, :]
    y = acc + cb_ref[...]
    mu = jnp.mean(y, axis=-1, keepdims=True)
    yc = y - mu
    var = jnp.mean(yc * yc, axis=-1, keepdims=True)
    yn = yc * lax.rsqrt(var + NORM_EPS) * lg_ref[...] + lb_ref[...]
    s = yn * jax.nn.sigmoid(yn)
    o_ref[0] = _rms(s, on_ref[...]).astype(o_ref.dtype)


def _conv(u, w, cb, lg, lb, on):
    B, S, C = u.shape
    tr = min(CONV_TR, S)
    vec = pl.BlockSpec((1, C), lambda b, i: (0, 0))
    return pl.pallas_call(
        _conv_kernel,
        grid=(B, S // tr),
        in_specs=[pl.BlockSpec((1, tr, C), lambda b, i: (b, jnp.maximum(i - 1, 0), 0)),
                  pl.BlockSpec((1, tr, C), lambda b, i: (b, i, 0)),
                  pl.BlockSpec((CONV_WIDTH, C), lambda b, i: (0, 0)),
                  vec, vec, vec, vec],
        out_specs=pl.BlockSpec((1, tr, C), lambda b, i: (b, i, 0)),
        out_shape=jax.ShapeDtypeStruct((B, S, C), BF16),
        scratch_shapes=[pltpu.VMEM((CONV_HALO + tr, C), F32),
                        pltpu.VMEM((SUBLANES, tr + SUBLANES * ((CONV_WIDTH - 1) // SUBLANES), C), F32)],
        compiler_params=pltpu.CompilerParams(dimension_semantics=("parallel", "parallel")),
        name="conv",
    )(u, u, w, cb, lg, lb, on)


def _cmp_kernel(kx_ref, vx_ref, pek_ref, pev_ref, kw1_ref, kw2_ref, vw1_ref, vw2t_ref, ko_ref, vo_ref):
    nc = kx_ref.shape[2]
    half = kx_ref.shape[3]
    nt = (((1,), (1,)), ((), ()))
    for x_ref, pe_ref, w1_ref, w2_ref, o_ref, transposed in ((kx_ref, pek_ref, kw1_ref, kw2_ref, ko_ref, False),
                                                             (vx_ref, pev_ref, vw1_ref, vw2t_ref, vo_ref, True)):
        w1 = w1_ref[...]
        pe = jnp.broadcast_to(pe_ref[...], (SUBLANES, 2 * half))
        pe_proj = _dot3(pe, w1)[0:1]
        for gg in range(KV_GROUPS):
            xg = x_ref[0, gg]
            first = _dot3(xg, w1[0:half])
            second = _dot3(xg, w1[half:2 * half])
            hid = first + pltpu.roll(second, nc - 1, 0) + pe_proj
            hid = hid * jax.nn.sigmoid(hid)
            if transposed:
                o_ref[0, gg] = _dot3(w2_ref[...], hid, nt)
            else:
                o_ref[0, gg] = _dot3(hid, w2_ref[...])


def _compress(kx, vx, pek, pev, kw1, kw2, vw1, vw2t):
    B, G, NC, W = kx.shape
    xs = pl.BlockSpec((1, G, NC, W), lambda b: (b, 0, 0, 0))
    full = lambda a: pl.BlockSpec(a.shape, lambda b: (0,) * a.ndim)
    return pl.pallas_call(
        _cmp_kernel,
        grid=(B,),
        in_specs=[xs, xs, full(pek), full(pev), full(kw1), full(kw2), full(vw1), full(vw2t)],
        out_specs=[pl.BlockSpec((1, G, NC, HEAD_DIM), lambda b: (b, 0, 0, 0)),
                   pl.BlockSpec((1, G, HEAD_DIM, NC), lambda b: (b, 0, 0, 0))],
        out_shape=[jax.ShapeDtypeStruct((B, G, NC, HEAD_DIM), F32),
                   jax.ShapeDtypeStruct((B, G, HEAD_DIM, NC), F32)],
        compiler_params=pltpu.CompilerParams(dimension_semantics=("parallel",)),
        name="compress",
    )(kx, vx, pek, pev, kw1, kw2, vw1, vw2t)


def _attn_kernel(qt_ref, kc_ref, vct_ref, ks_ref, vst_ref, kw_ref, vwt_ref, gt_ref, ovt_ref, o_ref, sel_ref):
    tq, tk = ATTN_TQ, ATTN_TK
    R = Q_PER_G * tq
    per_tile = tk // SEL_BLK
    S = ks_ref.shape[2]
    ncp = kc_ref.shape[2]
    nsel = ovt_ref.shape[0]
    qi = pl.program_id(2)
    q0 = qi * tq
    qt = qt_ref[0, 0, 0]
    t_row = q0 + lax.broadcasted_iota(I32, (1, tq), 1)
    heads = lambda a: jnp.concatenate([a] * Q_PER_G, axis=1)

    sc = jnp.dot(kc_ref[0, 0].astype(BF16), qt, preferred_element_type=F32)
    c_io = lax.broadcasted_iota(I32, (ncp, tq), 0)
    m_c = (c_io * CMP_STRIDE + (CMP_LEN - 1) <= t_row) & (c_io < ncp - 1)
    scb = sc + heads(jnp.where(m_c, 0.0, NEG_INF))
    e = jnp.exp2(scb - jnp.max(scb, axis=0, keepdims=True)) * heads(jnp.where(m_c, 1.0, 0.0))
    den = jnp.sum(e, axis=0, keepdims=True)
    pc = e / jnp.where(den > 0.0, den, 1.0)
    o_cmp = jnp.dot(vct_ref[0, 0].astype(BF16), pc.astype(BF16), preferred_element_type=F32)

    psum = pc[:, 0:tq]
    for n in range(1, Q_PER_G):
        psum = psum + pc[:, n * tq:(n + 1) * tq]
    imp = _dot3(ovt_ref[...], psum)
    j_io = lax.broadcasted_iota(I32, (nsel, tq), 0)
    cur = t_row // SEL_BLK
    valid = j_io * SEL_BLK <= t_row
    forced = (j_io == 0) | (j_io == cur) | (j_io == cur - 1)
    score = jnp.where(valid, jnp.where(forced, FORCE_SCORE, imp), NEG_INF)
    rank = jnp.zeros((nsel, tq), F32)
    for i in range(nsel):
        row = score[i:i + 1, :]
        tie = jnp.where(j_io > i, 1.0, 0.0)
        rank = rank + jnp.where(row > score, 1.0, jnp.where(row == score, tie, 0.0))
    sel_bias = jnp.where(rank < float(min(SEL_TOPN, nsel)), 0.0, NEG_INF)
    sel_ref[...] = jnp.zeros(sel_ref.shape, F32)
    for jj in range(S // tk):
        sel_ref[jj, 0:per_tile, :] = sel_bias[jj * per_tile:(jj + 1) * per_tile, :]

    k_io = lax.broadcasted_iota(I32, (tk, tq), 0)

    def flash_step(k_ref, vt_ref, kj, bias, carry):
        m, l, acc = carry
        k0 = pl.multiple_of(kj * tk, tk)
        s = jnp.dot(k_ref[0, 0, pl.ds(k0, tk), :], qt, preferred_element_type=F32) + heads(bias)
        m_new = jnp.maximum(m, jnp.max(s, axis=0, keepdims=True))
        alpha = jnp.exp2(m - m_new)
        p = jnp.exp2(s - m_new)
        l = alpha * l + jnp.sum(p, axis=0, keepdims=True)
        acc = alpha * acc + jnp.dot(vt_ref[0, 0, kj], p.astype(BF16), preferred_element_type=F32)
        return m_new, l, acc

    init = (jnp.full((1, R), NEG_INF, F32), jnp.zeros((1, R), F32), jnp.zeros((HEAD_DIM, R), F32))

    def slc_body(kj, carry):
        blocks = sel_ref[kj]
        bias = jnp.concatenate([jnp.broadcast_to(blocks[b:b + 1, :], (SEL_BLK, tq)) for b in range(per_tile)], axis=0)
        bias = jnp.where(kj * tk + k_io <= t_row, bias, NEG_INF)
        return flash_step(ks_ref, vst_ref, kj, bias, carry)

    def win_body(kj, carry):
        rel = t_row - (kj * tk + k_io)
        bias = jnp.where((rel >= 0) & (rel < WINDOW), 0.0, NEG_INF)
        return flash_step(kw_ref, vwt_ref, kj, bias, carry)

    n_slc = (q0 + tq + tk - 1) // tk
    lo_tile = jnp.maximum(q0 - (WINDOW - 1), 0) // tk
    n_pair = lo_tile // 2

    def pair_body(i, carry):
        return slc_body(2 * i, carry[0]), slc_body(2 * i + 1, carry[1])

    st_a, st_b = lax.fori_loop(0, n_pair, pair_body, (init, init))
    st_a = lax.fori_loop(2 * n_pair, lo_tile, slc_body, st_a)

    def both_body(kj, carry):
        return slc_body(kj, carry[0]), win_body(kj, carry[1])

    st_a, (_, l_w, acc_w) = lax.fori_loop(lo_tile, n_slc, both_body, (st_a, init))
    m_s = jnp.maximum(st_a[0], st_b[0])
    w_a, w_b = jnp.exp2(st_a[0] - m_s), jnp.exp2(st_b[0] - m_s)
    l_s = w_a * st_a[1] + w_b * st_b[1]
    acc_s = w_a * st_a[2] + w_b * st_b[2]


    gt = gt_ref[0, 0]
    o_slc = acc_s / l_s
    o_win = acc_w / l_w
    outs = []
    for n in range(Q_PER_G):
        cols = slice(n * tq, (n + 1) * tq)
        outs.append(gt[3 * n:3 * n + 1, :] * o_cmp[:, cols] + gt[3 * n + 1:3 * n + 2, :] * o_slc[:, cols]
                    + gt[3 * n + 2:3 * n + 3, :] * o_win[:, cols])
    o_ref[0] = jnp.concatenate(outs, axis=0).T


def _attention(qt, kc, vct, ks, vst, kw, vwt, gt, overlap_t):
    B, G, NQT, _, R = qt.shape
    S = ks.shape[2]
    tq, tk = ATTN_TQ, ATTN_TK
    ncp = kc.shape[2]
    kfull = pl.BlockSpec((1, 1, S, HEAD_DIM), lambda b, g, i: (b, g, 0, 0))
    vfull = pl.BlockSpec((1, 1, S // tk, HEAD_DIM, tk), lambda b, g, i: (b, g, 0, 0, 0))
    return pl.pallas_call(
        _attn_kernel,
        grid=(B, G, NQT),
        in_specs=[pl.BlockSpec((1, 1, 1, HEAD_DIM, R), lambda b, g, i: (b, g, i, 0, 0)),
                  pl.BlockSpec((1, 1, ncp, HEAD_DIM), lambda b, g, i: (b, g, 0, 0)),
                  pl.BlockSpec((1, 1, HEAD_DIM, ncp), lambda b, g, i: (b, g, 0, 0)),
                  kfull, vfull, kfull, vfull,
                  pl.BlockSpec((1, 1, GATE_ROWS, tq), lambda b, g, i: (b, g, 0, i)),
                  pl.BlockSpec(overlap_t.shape, lambda b, g, i: (0, 0))],
        out_specs=pl.BlockSpec((1, tq, Q_PER_G * HEAD_DIM), lambda b, g, i: (b, i, g)),
        out_shape=jax.ShapeDtypeStruct((B, S, NSA_WIDTH), F32),
        scratch_shapes=[pltpu.VMEM((S // tk, SUBLANES, tq), F32)],
        compiler_params=pltpu.CompilerParams(dimension_semantics=("parallel", "parallel", "arbitrary")),
        name="attn",
    )(qt, kc, vct, ks, vst, kw, vwt, gt, overlap_t)


def _outproj_kernel(x_ref, cv_ref, nsa_ref, on_ref, w_ref, gt1_ref, g2_ref, sc2_ref, sh2_ref, wrh_ref, wrl_ref, br_ref,
                    tri_ref, x1_ref, h2_ref, rt_ref, cnt_ref, run_ref):
    tm = x_ref.shape[1]

    @pl.when((pl.program_id(0) == 0) & (pl.program_id(1) == 0))
    def _():
        run_ref[...] = jnp.zeros(run_ref.shape, F32)

    nn = _rms(nsa_ref[0], on_ref[...]).astype(BF16)
    y = jnp.dot(jnp.concatenate([cv_ref[0], nn], axis=1), w_ref[...], preferred_element_type=F32)
    x1 = x_ref[0] + gt1_ref[0] * y
    x1_ref[0] = x1
    h2 = _rms(x1, g2_ref[...]) * (1.0 + sc2_ref[0]) + sh2_ref[0]
    for s in range(ROW_TILES):
        h2_ref[pl.ds(s, tm, stride=ROW_TILES), :] = h2[:, s * LANES:(s + 1) * LANES]
    nt = (((1,), (1,)), ((), ()))
    h_hi = h2.astype(BF16)
    h_lo = (h2 - h_hi.astype(F32)).astype(BF16)
    logits = (lax.dot_general(wrh_ref[...], h_hi, nt, preferred_element_type=F32)
              + lax.dot_general(wrh_ref[...], h_lo, nt, preferred_element_type=F32)
              + lax.dot_general(wrl_ref[...], h_hi, nt, preferred_element_type=F32)) + br_ref[...]
    eio = lax.broadcasted_iota(I32, (N_EXPERTS, tm), 0).astype(F32)
    vals, idxs = [], []
    for _ in range(TOP_K):
        m = jnp.max(logits, axis=0, keepdims=True)
        ix = jnp.min(jnp.where(logits == m, eio, float(N_EXPERTS)), axis=0, keepdims=True)
        vals.append(m)
        idxs.append(ix)
        logits = jnp.where(eio == ix, -jnp.inf, logits)
    es = [jnp.exp(v - vals[0]) for v in vals]
    den = es[0] + es[1] + es[2] + es[3]
    hot = jnp.zeros((N_EXPERTS, tm), F32)
    for r in range(TOP_K):
        hot = hot + jnp.where(eio == idxs[r], 1.0, 0.0)
    before = run_ref[...] + jnp.dot(hot.astype(BF16), tri_ref[...], preferred_element_type=F32)
    ranks = [jnp.sum(jnp.where(eio == idxs[r], before, 0.0), axis=0, keepdims=True) for r in range(TOP_K)]
    run_ref[...] = run_ref[...] + jnp.sum(hot, axis=1, keepdims=True)
    cnt_ref[...] = run_ref[...]
    rio = lax.broadcasted_iota(I32, (ROUTE_W, tm), 0)
    out = jnp.zeros((ROUTE_W, tm), F32)
    for r in range(TOP_K):
        out = jnp.where(rio == r, idxs[r], out)
        out = jnp.where(rio == TOP_K + r, es[r] / den, out)
        out = jnp.where(rio == 2 * TOP_K + r, ranks[r], out)
    rt_ref[...] = out


def _outproj(x, cv, nsa, on, w, gt1, g2, sc2, sh2, wr, br):
    B, S, D = x.shape
    tm = min(OUT_TM, S)
    nt = S // tm
    tri = (jnp.arange(tm)[:, None] < jnp.arange(tm)[None, :]).astype(BF16)
    wr_t = wr.T
    wr_hi = wr_t.astype(BF16)
    wr_lo = (wr_t - wr_hi.astype(F32)).astype(BF16)
    row = pl.BlockSpec((1, 1, D), lambda b, i: (b, 0, 0))
    vec = lambda n: pl.BlockSpec((1, n), lambda b, i: (0, 0))
    col = pl.BlockSpec((N_EXPERTS, 1), lambda b, i: (0, 0))
    wr_spec = pl.BlockSpec((N_EXPERTS, D), lambda b, i: (0, 0))
    return pl.pallas_call(
        _outproj_kernel,
        grid=(B, nt),
        in_specs=[pl.BlockSpec((1, tm, D), lambda b, i: (b, i, 0)),
                  pl.BlockSpec((1, tm, CONV_CH), lambda b, i: (b, i, 0)),
                  pl.BlockSpec((1, tm, NSA_WIDTH), lambda b, i: (b, i, 0)),
                  vec(NSA_WIDTH),
                  pl.BlockSpec((D, D), lambda b, i: (0, 0)),
                  row, vec(D), row, row,
                  wr_spec, wr_spec, col,
                  pl.BlockSpec((tm, tm), lambda b, i: (0, 0))],
        out_specs=[pl.BlockSpec((1, tm, D), lambda b, i: (b, i, 0)),
                   pl.BlockSpec((tm * ROW_TILES, LANES), lambda b, i: (b * nt + i, 0)),
                   pl.BlockSpec((ROUTE_W, tm), lambda b, i: (0, b * nt + i)),
                   col],
        out_shape=[jax.ShapeDtypeStruct((B, S, D), F32),
                   jax.ShapeDtypeStruct((B * S * ROW_TILES, LANES), F32),
                   jax.ShapeDtypeStruct((ROUTE_W, B * S), F32),
                   jax.ShapeDtypeStruct((N_EXPERTS, 1), F32)],
        scratch_shapes=[pltpu.VMEM((N_EXPERTS, 1), F32)],
        compiler_params=pltpu.CompilerParams(dimension_semantics=("arbitrary", "arbitrary")),
        name="outproj",
    )(x, cv, nsa, on, w, gt1, g2, sc2, sh2, wr_hi, wr_lo, br.reshape(N_EXPERTS, 1), tri)


def _issue_rows(idx_ref, rows, src_hbm, dst, slot, sem):
    for r in rows:
        pltpu.make_async_copy(src_hbm.at[idx_ref[0, 0, r]],
                              dst.at[slot, pl.ds(r * ROW_TILES, ROW_TILES), :],
                              sem.at[slot]).start(priority=r % 2)


def _wait_rows(dst, slot, sem):
    pltpu.make_async_copy(dst.at[slot], dst.at[slot], sem.at[slot]).wait()


def _rows_2d(buf, slot, base, n):
    return jnp.concatenate(
        [buf[slot, pl.ds(base * ROW_TILES + s, n, stride=ROW_TILES), :] for s in range(ROW_TILES)], axis=1)


def _ffn_kernel(be_ref, nb_ref, ta_ref, tb_ref, h2_hbm, gate_ref, wg_ref, bg_ref, wu_ref, bu_ref, wd_ref, bd_ref,
                o_ref, xbuf, wgb, wub, wdb, sem):
    bm = ta_ref.shape[2]
    i = pl.program_id(0)
    nb = nb_ref[0]
    slot = i % 2

    @pl.when((i == 0) | (be_ref[i] != be_ref[jnp.maximum(i - 1, 0)]))
    def _():
        wgb[...] = wg_ref[0].astype(BF16)
        wub[...] = wu_ref[0].astype(BF16)
        wdb[...] = wd_ref[0].astype(BF16)

    @pl.when(i == 0)
    def _():
        _issue_rows(ta_ref, range(bm), h2_hbm, xbuf, 0, sem)

    @pl.when(i + 1 < nb)
    def _():
        _issue_rows(tb_ref, range(bm), h2_hbm, xbuf, 1 - slot, sem)

    @pl.when(i < nb)
    def _():
        _wait_rows(xbuf, slot, sem)
        x = _rows_2d(xbuf, slot, 0, bm).astype(BF16)
        g = jnp.dot(x, wgb[...], preferred_element_type=F32) + bg_ref[0]
        u = jnp.dot(x, wub[...], preferred_element_type=F32) + bu_ref[0]
        g = jnp.minimum(g, SWIGLU_LIMIT)
        u = jnp.clip(u, -SWIGLU_LIMIT, SWIGLU_LIMIT)
        act = g * jax.nn.sigmoid(SWIGLU_ALPHA * g) * (u + 1.0)
        y = (jnp.dot(act.astype(BF16), wdb[...], preferred_element_type=F32) + bd_ref[0]) * gate_ref[...]
        for s in range(ROW_TILES):
            o_ref[pl.ds(s, bm, stride=ROW_TILES), :] = y[:, s * LANES:(s + 1) * LANES]

    @pl.when(i >= nb)
    def _():
        o_ref[...] = jnp.zeros(o_ref.shape, o_ref.dtype)


def _ffn(block_e, nb_used, buf_tok, h2_rows, buf_gate, wg, bg, wu, bu, wd, bd):
    NB = block_e.shape[0]
    bm = FFN_BM
    D, F = D_MODEL, D_FF
    tok3 = buf_tok.reshape(NB, 1, bm)
    wspec = lambda r, c: pl.BlockSpec((1, r, c), lambda i, be, nb: (be[i], 0, 0))
    vmem_limit = 2 * 3 * D * F * 4 + 3 * D * F * 2 + 4 * bm * D * 4 + 6 * bm * F * 4
    return pl.pallas_call(
        _ffn_kernel,
        grid_spec=pltpu.PrefetchScalarGridSpec(
            num_scalar_prefetch=2,
            grid=(NB,),
            in_specs=[pl.BlockSpec((1, 1, bm), lambda i, be, nb: (i, 0, 0), memory_space=pltpu.SMEM),
                      pl.BlockSpec((1, 1, bm), lambda i, be, nb: (jnp.minimum(i + 1, NB - 1), 0, 0),
                                   memory_space=pltpu.SMEM),
                      pl.BlockSpec(memory_space=pl.ANY),
                      pl.BlockSpec((bm, 1), lambda i, be, nb: (i, 0)),
                      wspec(D, F), wspec(1, F), wspec(D, F), wspec(1, F), wspec(F, D), wspec(1, D)],
            out_specs=pl.BlockSpec((bm * ROW_TILES, LANES), lambda i, be, nb: (i, 0)),
            scratch_shapes=[pltpu.VMEM((2, bm * ROW_TILES, LANES), F32),
                            pltpu.VMEM((D, F), BF16), pltpu.VMEM((D, F), BF16), pltpu.VMEM((F, D), BF16),
                            pltpu.SemaphoreType.DMA((2,))]),
        out_shape=jax.ShapeDtypeStruct((NB * bm * ROW_TILES, LANES), F32),
        compiler_params=pltpu.CompilerParams(dimension_semantics=("arbitrary",),
                                             vmem_limit_bytes=vmem_limit),
        name="ffn",
    )(block_e, nb_used, tok3, tok3, h2_rows, buf_gate, wg, bg, wu, bu, wd, bd)


def _combine_kernel(da_ref, db_ref, y_hbm, x1_ref, gt2_ref, fg_ref, o_ref, buf, sem):
    tm = x1_ref.shape[0]
    n = TOP_K * tm
    i = pl.program_id(0)
    slot = i % 2

    @pl.when(i == 0)
    def _():
        _issue_rows(da_ref, range(n), y_hbm, buf, 0, sem)

    @pl.when(i + 1 < pl.num_programs(0))
    def _():
        _issue_rows(db_ref, range(n), y_hbm, buf, 1 - slot, sem)

    _wait_rows(buf, slot, sem)
    y = _rows_2d(buf, slot, 0, tm)
    for k in range(1, TOP_K):
        y = y + _rows_2d(buf, slot, k * tm, tm)
    x2 = x1_ref[...] + gt2_ref[0] * y
    o_ref[...] = _rms(x2, fg_ref[...])


def _combine(dest3, y_rows, x1, gt2, fg, S):
    T, D = x1.shape
    tm = min(COMB_TM, S)
    NT = T // tm
    per_b = S // tm
    n = TOP_K * tm
    return pl.pallas_call(
        _combine_kernel,
        grid=(NT,),
        in_specs=[pl.BlockSpec((1, 1, n), lambda i: (i, 0, 0), memory_space=pltpu.SMEM),
                  pl.BlockSpec((1, 1, n), lambda i: (jnp.minimum(i + 1, NT - 1), 0, 0), memory_space=pltpu.SMEM),
                  pl.BlockSpec(memory_space=pl.ANY),
                  pl.BlockSpec((tm, D), lambda i: (i, 0)),
                  pl.BlockSpec((1, 1, D), lambda i: (i // per_b, 0, 0)),
                  pl.BlockSpec((1, D), lambda i: (0, 0))],
        out_specs=pl.BlockSpec((tm, D), lambda i: (i, 0)),
        out_shape=jax.ShapeDtypeStruct((T, D), F32),
        scratch_shapes=[pltpu.VMEM((2, n * ROW_TILES, LANES), F32), pltpu.SemaphoreType.DMA((2,))],
        compiler_params=pltpu.CompilerParams(dimension_semantics=("arbitrary",)),
        name="combine",
    )(dest3, dest3, y_rows, x1, gt2, fg)


def _rope_tables(S):
    inv = ROPE_THETA ** (-jnp.arange(0, ROT_DIM, 2, dtype=F32) / ROT_DIM)
    ang = jnp.arange(S, dtype=F32)[:, None] * inv[None, :]
    cos, sin = jnp.cos(ang), jnp.sin(ang)
    d = jnp.arange(KV_W) % HEAD_DIM
    first, second = d < ROT_HALF, (d >= ROT_HALF) & (d < ROT_DIM)
    cos_l = cos[:, d % ROT_HALF]
    sin_l = sin[:, d % ROT_HALF]
    rc = jnp.where((d < ROT_DIM)[None], cos_l, 1.0)
    rs1 = jnp.where(second[None], sin_l, 0.0)
    rs2 = jnp.where(first[None], -sin_l, 0.0)
    return rc, rs1, rs2, cos.T, sin.T


def _route_plan(route, counts, T):
    bm = FFN_BM
    A = T * TOP_K
    idx = route[0:TOP_K].astype(I32)
    gate = route[TOP_K:2 * TOP_K]
    rank = route[2 * TOP_K:3 * TOP_K].astype(I32)
    keys = idx * T + jnp.arange(T, dtype=I32)[None, :]
    skey, sgate = lax.sort((keys.reshape(A), gate.reshape(A)), num_keys=1)
    counts = counts.astype(I32)
    starts = jnp.cumsum(counts) - counts
    padded = (counts + bm - 1) // bm * bm
    pends = jnp.cumsum(padded)
    pstarts = pends - padded
    P = (A + N_EXPERTS * bm + bm - 1) // bm * bm
    NB = P // bm
    blk0 = jnp.arange(NB, dtype=I32) * bm
    block_e = jnp.minimum(jnp.sum((pends[None, :] <= blk0[:, None]).astype(I32), axis=1), N_EXPERTS - 1)
    r = (blk0 - pstarts[block_e])[:, None] + jnp.arange(bm, dtype=I32)[None, :]
    valid = r < counts[block_e][:, None]
    src = jnp.clip(starts[block_e][:, None] + r, 0, A - 1)
    buf_tok = jnp.where(valid, skey[src] - block_e[:, None] * T, 0)
    buf_gate = jnp.where(valid, sgate[src], 0.0).reshape(P, 1)
    dest = rank
    for e in range(N_EXPERTS):
        dest = dest + jnp.where(idx == e, pstarts[e], 0)
    nb_used = (pends[-1] // bm).astype(I32).reshape(1)
    return block_e, nb_used, buf_tok, buf_gate, dest


def kernel(x, c, norm1_g, norm2_g, w_ada, b_ada, w_in, conv_w, conv_b, conv_ln_g, conv_ln_b, cmp_pe_k, cmp_pe_v,
           cmp_k_w1, cmp_k_w2, cmp_v_w1, cmp_v_w2, out_norm_conv, out_norm_nsa, w_out, w_router, b_router,
           w_gate, b_gate, w_up, b_up, w_down, b_down, final_norm_g):
    B, S, D = x.shape
    T = B * S
    G = KV_GROUPS
    assert D == D_MODEL and S % ATTN_TK == 0 and S % CMP_STRIDE == 0 and KV_W == LANES
    rc, rs1, rs2, cos_t, sin_t = _rope_tables(S)
    n_sel = S // SEL_BLK
    nc = S // CMP_STRIDE
    cstart = jnp.arange(nc) * CMP_STRIDE
    jstart = jnp.arange(n_sel) * SEL_BLK
    overlap_t = ((cstart[None, :] <= jstart[:, None] + SEL_BLK - 1)
                 & (cstart[None, :] + CMP_LEN - 1 >= jstart[:, None])
                 & (jnp.arange(nc)[None, :] < nc - 1)).astype(F32)

    assert w_ada.shape[0] == 1
    for l in range(1):
        mod = _adaln(c, w_ada[l], b_ada[l][None])
        sh1, sc1, gt1, sh2, sc2, gt2 = [m[:, None, :] for m in jnp.split(mod, 6, axis=-1)]

        wl = w_in[l]
        o = 2 * CONV_CH + NSA_WIDTH
        kvc = [wl[:, o + i * KV_W:o + (i + 1) * KV_W] for i in range(6)]
        gl = wl[:, o + 6 * KV_W:]
        per_g = 3 * Q_PER_G
        gpad = [jnp.pad(gl[:, per_g * g:per_g * (g + 1)], ((0, 0), (0, GATE_ROWS - per_g))) for g in range(G)]
        wn = jnp.concatenate([wl[:, :2 * CONV_CH], kvc[0], kvc[2], kvc[4], kvc[1]], axis=1).astype(BF16)
        wt = jnp.concatenate([wl[:, 2 * CONV_CH:o], kvc[3], kvc[5]] + gpad, axis=1).T.astype(BF16)
        u, kc, vc, ks, kw, qt, vst, vwt, gates = _inproj(x, sc1, sh1, norm1_g[l][None], wn, wt,
                                                         rc, rs1, rs2, cos_t, sin_t)

        conv_n = _conv(u, conv_w[l], conv_b[l][None], conv_ln_g[l][None], conv_ln_b[l][None],
                       out_norm_conv[l][None])

        kcmp, vcmp_t = _compress(kc, vc, cmp_pe_k[l].reshape(1, -1), cmp_pe_v[l].reshape(1, -1),
                                 cmp_k_w1[l], cmp_k_w2[l], cmp_v_w1[l], cmp_v_w2[l].T)
        nsa = _attention(qt, kcmp, vcmp_t, ks, vst, kw, vwt, gates, overlap_t)

        x1, h2_rows, route, counts = _outproj(x, conv_n, nsa, out_norm_nsa[l][None], w_out[l].astype(BF16), gt1,
                                              norm2_g[l][None], sc2, sh2, w_router[l], b_router[l][None])

        block_e, nb_used, buf_tok, buf_gate, dest = _route_plan(route, counts[:, 0], T)
        y_rows = _ffn(block_e, nb_used, buf_tok, h2_rows.reshape(T, ROW_TILES, LANES), buf_gate,
                      w_gate[l], b_gate[l][:, None, :], w_up[l], b_up[l][:, None, :],
                      w_down[l], b_down[l][:, None, :])
        tm = min(COMB_TM, S)
        dest3 = dest.reshape(TOP_K, T // tm, tm).transpose(1, 0, 2).reshape(T // tm, 1, TOP_K * tm)
        P = y_rows.shape[0] // ROW_TILES
        x = _combine(dest3, y_rows.reshape(P, ROW_TILES, LANES), x1.reshape(T, D), gt2, final_norm_g[None],
                     S).reshape(B, S, D)
    return x
```

```python
import functools

import jax
import jax.numpy as jnp
from jax import lax
from jax.experimental import pallas as pl
from jax.experimental.pallas import tpu as pltpu

F32 = jnp.float32
BF16 = jnp.bfloat16
I32 = jnp.int32
HI = lax.Precision.HIGHEST

D_MODEL = 1024
CONV_CH = 512
CONV_WIDTH = 31
NSA_HEADS = 8
KV_GROUPS = 2
Q_PER_G = NSA_HEADS // KV_GROUPS
HEAD_DIM = 64
NSA_WIDTH = NSA_HEADS * HEAD_DIM
KV_W = KV_GROUPS * HEAD_DIM
ROT_DIM = HEAD_DIM // 4
ROT_HALF = ROT_DIM // 2
ROPE_THETA = 500000.0
CMP_LEN = 32
CMP_STRIDE = 16
CMP_HIDDEN = 128
SEL_BLK = 64
SEL_TOPN = 16
WINDOW = 512
N_EXPERTS = 32
TOP_K = 4
D_FF = 1024
SWIGLU_ALPHA = 1.702
SWIGLU_LIMIT = 7.0
NORM_EPS = 1e-5
NEG_INF = -1e30
FORCE_SCORE = 1e9
LOG2_E = 1.4426950408889634

LANES = 128
SUBLANES = 8
ROW_TILES = D_MODEL // LANES

GATE_ROWS = 16

INPROJ_TM = 512
CONV_TR = 512
CONV_HALO = 32
ATTN_TQ = 512
ATTN_TK = 512
OUT_TM = 1024
FFN_BM = 512
COMB_TM = 256
ROUTE_W = 16


def _rms(x, g):
    return x * lax.rsqrt(jnp.mean(x * x, axis=-1, keepdims=True) + NORM_EPS) * g


def _split_bf16(a):
    hi = a.astype(BF16)
    return hi, (a - hi.astype(F32)).astype(BF16)


def _dot3(a, b, dims=(((1,), (0,)), ((), ()))):
    ah, al = _split_bf16(a)
    bh, bl = _split_bf16(b)
    d = lambda x, y: lax.dot_general(x, y, dims, preferred_element_type=F32)
    return d(ah, bh) + d(al, bh) + d(ah, bl)


def _ada_kernel(c_ref, w_ref, b_ref, o_ref):
    c = c_ref[...]
    ca = c * jax.nn.sigmoid(c)
    o_ref[...] = jnp.dot(ca, w_ref[...], preferred_element_type=F32, precision=HI) + b_ref[...]


def _adaln(c, w, b):
    B = c.shape[0]
    D = D_MODEL
    return pl.pallas_call(
        _ada_kernel,
        grid=(6,),
        in_specs=[pl.BlockSpec((B, D), lambda j: (0, 0)),
                  pl.BlockSpec((D, D), lambda j: (0, j)),
                  pl.BlockSpec((1, D), lambda j: (0, j))],
        out_specs=pl.BlockSpec((B, D), lambda j: (0, j)),
        out_shape=jax.ShapeDtypeStruct((B, 6 * D), F32),
        name="adaln",
    )(c, w, b)


def _inproj_kernel(x_ref, sc_ref, sh_ref, g_ref, wn_ref, wt_ref, rc_ref, rs1_ref, rs2_ref, ct_ref, st_ref,
                   u_ref, kc_ref, vc_ref, ks_ref, kw_ref, qt_ref, vst_ref, vwt_ref, gt_ref, stage_ref):
    tm = x_ref.shape[1]
    tq, tk = ATTN_TQ, ATTN_TK
    h = (_rms(x_ref[0], g_ref[...]) * (1.0 + sc_ref[0]) + sh_ref[0]).astype(BF16)

    p = jnp.dot(h, wn_ref[...], preferred_element_type=F32)
    u_ref[0] = p[:, 0:CONV_CH] * jax.nn.sigmoid(p[:, CONV_CH:2 * CONV_CH])
    c0 = 2 * CONV_CH
    rc, rs1, rs2 = rc_ref[...], rs1_ref[...], rs2_ref[...]
    for ref, roped in ((kc_ref, True), (ks_ref, True), (kw_ref, True), (vc_ref, False)):
        v = p[:, c0:c0 + KV_W]
        if roped:
            v = v * rc + pltpu.roll(v, ROT_HALF, 1) * rs1 + pltpu.roll(v, KV_W - ROT_HALF, 1) * rs2
        c0 += KV_W
        if ref is kc_ref or ref is vc_ref:
            stage_ref[...] = v
            left = lax.broadcasted_iota(I32, (tm // CMP_STRIDE, KV_W), 1) < HEAD_DIM
            pieces = [stage_ref[pl.ds(tl, tm // CMP_STRIDE, stride=CMP_STRIDE), :] for tl in range(CMP_STRIDE)]
            for gg in range(KV_GROUPS):
                cols = []
                for tl in range(0, CMP_STRIDE, 2):
                    a, b = pieces[tl], pieces[tl + 1]
                    if gg == 0:
                        cols.append(jnp.where(left, a, pltpu.roll(b, HEAD_DIM, 1)))
                    else:
                        cols.append(jnp.where(left, pltpu.roll(a, HEAD_DIM, 1), b))
                ref[0, gg] = jnp.concatenate(cols, axis=1)
            continue
        for gg in range(KV_GROUPS):
            ref[0, gg] = v[:, HEAD_DIM * gg:HEAD_DIM * (gg + 1)].astype(ref.dtype)

    pt = lax.dot_general(wt_ref[...], h, (((1,), (1,)), ((), ())), preferred_element_type=F32)
    cos_t, sin_t = ct_ref[...], st_ref[...]
    scale = HEAD_DIM ** -0.5 * LOG2_E
    for hh in range(NSA_HEADS):
        blk = pt[HEAD_DIM * hh:HEAD_DIM * (hh + 1), :]
        x1, x2 = blk[0:ROT_HALF], blk[ROT_HALF:ROT_DIM]
        qh = (jnp.concatenate([x1 * cos_t - x2 * sin_t, x2 * cos_t + x1 * sin_t, blk[ROT_DIM:]], axis=0)
              * scale).astype(BF16)
        gg, n = divmod(hh, Q_PER_G)
        for j in range(tm // tq):
            qt_ref[0, gg, j, :, n * tq:(n + 1) * tq] = qh[:, j * tq:(j + 1) * tq]
    r0 = NSA_WIDTH
    for ref in (vst_ref, vwt_ref):
        for gg in range(KV_GROUPS):
            blk = pt[r0 + HEAD_DIM * gg:r0 + HEAD_DIM * (gg + 1), :].astype(BF16)
            for j in range(tm // tk):
                ref[0, gg, j] = blk[:, j * tk:(j + 1) * tk]
        r0 += KV_W
    for gg in range(KV_GROUPS):
        gt_ref[0, gg] = jax.nn.sigmoid(pt[r0 + GATE_ROWS * gg:r0 + GATE_ROWS * (gg + 1), :])


def _inproj(x, sc, sh, g, wn, wt, rc, rs1, rs2, cos_t, sin_t):
    B, S, D = x.shape
    tm = min(INPROJ_TM, S)
    tq, tk = ATTN_TQ, ATTN_TK
    G = KV_GROUPS
    kv = lambda dt: jax.ShapeDtypeStruct((B, G, S, HEAD_DIM), dt)
    kv_spec = pl.BlockSpec((1, G, tm, HEAD_DIM), lambda b, i: (b, 0, i, 0))
    chunk = CMP_STRIDE * HEAD_DIM
    ck_shape = jax.ShapeDtypeStruct((B, G, S // CMP_STRIDE, chunk), F32)
    ck_spec = pl.BlockSpec((1, G, tm // CMP_STRIDE, chunk), lambda b, i: (b, 0, i, 0))
    vt_shape = jax.ShapeDtypeStruct((B, G, S // tk, HEAD_DIM, tk), BF16)
    vt_spec = pl.BlockSpec((1, G, tm // tk, HEAD_DIM, tk), lambda b, i: (b, 0, i, 0, 0))
    row = pl.BlockSpec((1, 1, D), lambda b, i: (b, 0, 0))
    tab = pl.BlockSpec((tm, LANES), lambda b, i: (i, 0))
    tab_t = pl.BlockSpec((ROT_HALF, tm), lambda b, i: (0, i))
    return pl.pallas_call(
        _inproj_kernel,
        grid=(B, S // tm),
        in_specs=[pl.BlockSpec((1, tm, D), lambda b, i: (b, i, 0)), row, row,
                  pl.BlockSpec((1, D), lambda b, i: (0, 0)),
                  pl.BlockSpec(wn.shape, lambda b, i: (0, 0)),
                  pl.BlockSpec(wt.shape, lambda b, i: (0, 0)),
                  tab, tab, tab, tab_t, tab_t],
        out_specs=[pl.BlockSpec((1, tm, CONV_CH), lambda b, i: (b, i, 0)),
                   ck_spec, ck_spec, kv_spec, kv_spec,
                   pl.BlockSpec((1, G, tm // tq, HEAD_DIM, Q_PER_G * tq), lambda b, i: (b, 0, i, 0, 0)),
                   vt_spec, vt_spec,
                   pl.BlockSpec((1, G, GATE_ROWS, tm), lambda b, i: (b, 0, 0, i))],
        out_shape=[jax.ShapeDtypeStruct((B, S, CONV_CH), F32),
                   ck_shape, ck_shape, kv(BF16), kv(BF16),
                   jax.ShapeDtypeStruct((B, G, S // tq, HEAD_DIM, Q_PER_G * tq), BF16),
                   vt_shape, vt_shape,
                   jax.ShapeDtypeStruct((B, G, GATE_ROWS, S), F32)],
        scratch_shapes=[pltpu.VMEM((tm, KV_W), F32)],
        compiler_params=pltpu.CompilerParams(dimension_semantics=("parallel", "parallel")),
        name="inproj",
    )(x, sc, sh, g, wn, wt, rc, rs1, rs2, cos_t, sin_t)


def _conv_kernel(prev_ref, cur_ref, w_ref, cb_ref, lg_ref, lb_ref, on_ref, o_ref, pad_ref, win_ref):
    tr = cur_ref.shape[1]
    first = pl.program_id(1) == 0
    halo = prev_ref[0, tr - CONV_HALO:tr, :]
    pad_ref[0:CONV_HALO, :] = jnp.where(first, 0.0, halo)
    pad_ref[CONV_HALO:CONV_HALO + tr, :] = cur_ref[0]
    off = CONV_HALO - (CONV_WIDTH - 1)
    acc = jnp.zeros((tr, CONV_CH), F32)
    for b in range(SUBLANES):
        taps = range(b, CONV_WIDTH, SUBLANES)
        rows = tr + SUBLANES * (len(taps) - 1)
        win_ref[b, 0:rows, :] = pad_ref[off + b:off + b + rows, :]
        for a, k in enumerate(taps):
            acc = acc + win_ref[b, SUBLANES * a:SUBLANES * a + tr, :] * w_ref[k:k + 1, :]
    y = acc + cb_ref[...]
    mu = jnp.mean(y, axis=-1, keepdims=True)
    yc = y - mu
    var = jnp.mean(yc * yc, axis=-1, keepdims=True)
    yn = yc * lax.rsqrt(var + NORM_EPS) * lg_ref[...] + lb_ref[...]
    s = yn * jax.nn.sigmoid(yn)
    o_ref[0] = _rms(s, on_ref[...]).astype(o_ref.dtype)


def _conv(u, w, cb, lg, lb, on):
    B, S, C = u.shape
    tr = min(CONV_TR, S)
    vec = pl.BlockSpec((1, C), lambda b, i: (0, 0))
    return pl.pallas_call(
        _conv_kernel,
        grid=(B, S // tr),
        in_specs=[pl.BlockSpec((1, tr, C), lambda b, i: (b, jnp.maximum(i - 1, 0), 0)),
                  pl.BlockSpec((1, tr, C), lambda b, i: (b, i, 0)),
                  pl.BlockSpec((CONV_WIDTH, C), lambda b, i: (0, 0)),
                  vec, vec, vec, vec],
        out_specs=pl.BlockSpec((1, tr, C), lambda b, i: (b, i, 0)),
        out_shape=jax.ShapeDtypeStruct((B, S, C), BF16),
        scratch_shapes=[pltpu.VMEM((CONV_HALO + tr, C), F32),
                        pltpu.VMEM((SUBLANES, tr + SUBLANES * ((CONV_WIDTH - 1) // SUBLANES), C), F32)],
        compiler_params=pltpu.CompilerParams(dimension_semantics=("parallel", "parallel")),
        name="conv",
    )(u, u, w, cb, lg, lb, on)


def _cmp_kernel(kx_ref, vx_ref, pek_ref, pev_ref, kw1_ref, kw2_ref, vw1_ref, vw2t_ref, ko_ref, vo_ref):
    nc = kx_ref.shape[2]
    half = kx_ref.shape[3]
    nt = (((1,), (1,)), ((), ()))
    for x_ref, pe_ref, w1_ref, w2_ref, o_ref, transposed in ((kx_ref, pek_ref, kw1_ref, kw2_ref, ko_ref, False),
                                                             (vx_ref, pev_ref, vw1_ref, vw2t_ref, vo_ref, True)):
        w1 = w1_ref[...]
        pe = jnp.broadcast_to(pe_ref[...], (SUBLANES, 2 * half))
        pe_proj = _dot3(pe, w1)[0:1]
        for gg in range(KV_GROUPS):
            xg = x_ref[0, gg]
            first = _dot3(xg, w1[0:half])
            second = _dot3(xg, w1[half:2 * half])
            hid = first + pltpu.roll(second, nc - 1, 0) + pe_proj
            hid = hid * jax.nn.sigmoid(hid)
            if transposed:
                o_ref[0, gg] = _dot3(w2_ref[...], hid, nt)
            else:
                o_ref[0, gg] = _dot3(hid, w2_ref[...])


def _compress(kx, vx, pek, pev, kw1, kw2, vw1, vw2t):
    B, G, NC, W = kx.shape
    xs = pl.BlockSpec((1, G, NC, W), lambda b: (b, 0, 0, 0))
    full = lambda a: pl.BlockSpec(a.shape, lambda b: (0,) * a.ndim)
    return pl.pallas_call(
        _cmp_kernel,
        grid=(B,),
        in_specs=[xs, xs, full(pek), full(pev), full(kw1), full(kw2), full(vw1), full(vw2t)],
        out_specs=[pl.BlockSpec((1, G, NC, HEAD_DIM), lambda b: (b, 0, 0, 0)),
                   pl.BlockSpec((1, G, HEAD_DIM, NC), lambda b: (b, 0, 0, 0))],
        out_shape=[jax.ShapeDtypeStruct((B, G, NC, HEAD_DIM), F32),
                   jax.ShapeDtypeStruct((B, G, HEAD_DIM, NC), F32)],
        compiler_params=pltpu.CompilerParams(dimension_semantics=("parallel",)),
        name="compress",
    )(kx, vx, pek, pev, kw1, kw2, vw1, vw2t)


def _attn_kernel(qt_ref, kc_ref, vct_ref, ks_ref, vst_ref, kw_ref, vwt_ref, gt_ref, ovt_ref, o_ref, sel_ref):
    tq, tk = ATTN_TQ, ATTN_TK
    R = Q_PER_G * tq
    per_tile = tk // SEL_BLK
    S = ks_ref.shape[2]
    ncp = kc_ref.shape[2]
    nsel = ovt_ref.shape[0]
    qi = pl.program_id(2)
    q0 = qi * tq
    qt = qt_ref[0, 0, 0]
    t_row = q0 + lax.broadcasted_iota(I32, (1, tq), 1)
    heads = lambda a: jnp.concatenate([a] * Q_PER_G, axis=1)

    sc = jnp.dot(kc_ref[0, 0].astype(BF16), qt, preferred_element_type=F32)
    c_io = lax.broadcasted_iota(I32, (ncp, tq), 0)
    m_c = (c_io * CMP_STRIDE + (CMP_LEN - 1) <= t_row) & (c_io < ncp - 1)
    scb = sc + heads(jnp.where(m_c, 0.0, NEG_INF))
    e = jnp.exp2(scb - jnp.max(scb, axis=0, keepdims=True)) * heads(jnp.where(m_c, 1.0, 0.0))
    den = jnp.sum(e, axis=0, keepdims=True)
    pc = e / jnp.where(den > 0.0, den, 1.0)
    o_cmp = jnp.dot(vct_ref[0, 0].astype(BF16), pc.astype(BF16), preferred_element_type=F32)

    psum = pc[:, 0:tq]
    for n in range(1, Q_PER_G):
        psum = psum + pc[:, n * tq:(n + 1) * tq]
    imp = _dot3(ovt_ref[...], psum)
    j_io = lax.broadcasted_iota(I32, (nsel, tq), 0)
    cur = t_row // SEL_BLK
    valid = j_io * SEL_BLK <= t_row
    forced = (j_io == 0) | (j_io == cur) | (j_io == cur - 1)
    score = jnp.where(valid, jnp.where(forced, FORCE_SCORE, imp), NEG_INF)
    rank = jnp.zeros((nsel, tq), F32)
    for i in range(nsel):
        row = score[i:i + 1, :]
        tie = jnp.where(j_io > i, 1.0, 0.0)
        rank = rank + jnp.where(row > score, 1.0, jnp.where(row == score, tie, 0.0))
    sel_bias = jnp.where(rank < float(min(SEL_TOPN, nsel)), 0.0, NEG_INF)
    sel_ref[...] = jnp.zeros(sel_ref.shape, F32)
    for jj in range(S // tk):
        sel_ref[jj, 0:per_tile, :] = sel_bias[jj * per_tile:(jj + 1) * per_tile, :]

    k_io = lax.broadcasted_iota(I32, (tk, tq), 0)

    def flash_step(k_ref, vt_ref, kj, bias, carry):
        m, l, acc = carry
        k0 = pl.multiple_of(kj * tk, tk)
        s = jnp.dot(k_ref[0, 0, pl.ds(k0, tk), :], qt, preferred_element_type=F32) + heads(bias)
        m_new = jnp.maximum(m, jnp.max(s, axis=0, keepdims=True))
        alpha = jnp.exp2(m - m_new)
        p = jnp.exp2(s - m_new)
        l = alpha * l + jnp.sum(p, axis=0, keepdims=True)
        acc = alpha * acc + jnp.dot(vt_ref[0, 0, kj], p.astype(BF16), preferred_element_type=F32)
        return m_new, l, acc

    init = (jnp.full((1, R), NEG_INF, F32), jnp.zeros((1, R), F32), jnp.zeros((HEAD_DIM, R), F32))

    def slc_body(kj, carry):
        blocks = sel_ref[kj]
        bias = jnp.concatenate([jnp.broadcast_to(blocks[b:b + 1, :], (SEL_BLK, tq)) for b in range(per_tile)], axis=0)
        bias = jnp.where(kj * tk + k_io <= t_row, bias, NEG_INF)
        return flash_step(ks_ref, vst_ref, kj, bias, carry)

    def win_body(kj, carry):
        rel = t_row - (kj * tk + k_io)
        bias = jnp.where((rel >= 0) & (rel < WINDOW), 0.0, NEG_INF)
        return flash_step(kw_ref, vwt_ref, kj, bias, carry)

    n_slc = (q0 + tq + tk - 1) // tk
    lo_tile = jnp.maximum(q0 - (WINDOW - 1), 0) // tk
    n_pair = lo_tile // 2

    def pair_body(i, carry):
        return slc_body(2 * i, carry[0]), slc_body(2 * i + 1, carry[1])

    st_a, st_b = lax.fori_loop(0, n_pair, pair_body, (init, init))
    st_a = lax.fori_loop(2 * n_pair, lo_tile, slc_body, st_a)

    def both_body(kj, carry):
        return slc_body(kj, carry[0]), win_body(kj, carry[1])

    st_a, (_, l_w, acc_w) = lax.fori_loop(lo_tile, n_slc, both_body, (st_a, init))
    m_s = jnp.maximum(st_a[0], st_b[0])
    w_a, w_b = jnp.exp2(st_a[0] - m_s), jnp.exp2(st_b[0] - m_s)
    l_s = w_a * st_a[1] + w_b * st_b[1]
    acc_s = w_a * st_a[2] + w_b * st_b[2]


    gt = gt_ref[0, 0]
    o_slc = acc_s / l_s
    o_win = acc_w / l_w
    outs = []
    for n in range(Q_PER_G):
        cols = slice(n * tq, (n + 1) * tq)
        outs.append(gt[3 * n:3 * n + 1, :] * o_cmp[:, cols] + gt[3 * n + 1:3 * n + 2, :] * o_slc[:, cols]
                    + gt[3 * n + 2:3 * n + 3, :] * o_win[:, cols])
    o_ref[0] = jnp.concatenate(outs, axis=0).T


def _attention(qt, kc, vct, ks, vst, kw, vwt, gt, overlap_t):
    B, G, NQT, _, R = qt.shape
    S = ks.shape[2]
    tq, tk = ATTN_TQ, ATTN_TK
    ncp = kc.shape[2]
    kfull = pl.BlockSpec((1, 1, S, HEAD_DIM), lambda b, g, i: (b, g, 0, 0))
    vfull = pl.BlockSpec((1, 1, S // tk, HEAD_DIM, tk), lambda b, g, i: (b, g, 0, 0, 0))
    return pl.pallas_call(
        _attn_kernel,
        grid=(B, G, NQT),
        in_specs=[pl.BlockSpec((1, 1, 1, HEAD_DIM, R), lambda b, g, i: (b, g, i, 0, 0)),
                  pl.BlockSpec((1, 1, ncp, HEAD_DIM), lambda b, g, i: (b, g, 0, 0)),
                  pl.BlockSpec((1, 1, HEAD_DIM, ncp), lambda b, g, i: (b, g, 0, 0)),
                  kfull, vfull, kfull, vfull,
                  pl.BlockSpec((1, 1, GATE_ROWS, tq), lambda b, g, i: (b, g, 0, i)),
                  pl.BlockSpec(overlap_t.shape, lambda b, g, i: (0, 0))],
        out_specs=pl.BlockSpec((1, tq, Q_PER_G * HEAD_DIM), lambda b, g, i: (b, i, g)),
        out_shape=jax.ShapeDtypeStruct((B, S, NSA_WIDTH), F32),
        scratch_shapes=[pltpu.VMEM((S // tk, SUBLANES, tq), F32)],
        compiler_params=pltpu.CompilerParams(dimension_semantics=("parallel", "parallel", "arbitrary")),
        name="attn",
    )(qt, kc, vct, ks, vst, kw, vwt, gt, overlap_t)


def _outproj_kernel(x_ref, cv_ref, nsa_ref, on_ref, w_ref, gt1_ref, g2_ref, sc2_ref, sh2_ref, wrh_ref, wrl_ref, br_ref,
                    tri_ref, x1_ref, h2_ref, rt_ref, cnt_ref, run_ref):
    tm = x_ref.shape[1]

    @pl.when((pl.program_id(0) == 0) & (pl.program_id(1) == 0))
    def _():
        run_ref[...] = jnp.zeros(run_ref.shape, F32)

    nn = _rms(nsa_ref[0], on_ref[...]).astype(BF16)
    y = jnp.dot(jnp.concatenate([cv_ref[0], nn], axis=1), w_ref[...], preferred_element_type=F32)
    x1 = x_ref[0] + gt1_ref[0] * y
    x1_ref[0] = x1
    h2 = _rms(x1, g2_ref[...]) * (1.0 + sc2_ref[0]) + sh2_ref[0]
    for s in range(ROW_TILES):
        h2_ref[pl.ds(s, tm, stride=ROW_TILES), :] = h2[:, s * LANES:(s + 1) * LANES]
    nt = (((1,), (1,)), ((), ()))
    h_hi = h2.astype(BF16)
    h_lo = (h2 - h_hi.astype(F32)).astype(BF16)
    logits = (lax.dot_general(wrh_ref[...], h_hi, nt, preferred_element_type=F32)
              + lax.dot_general(wrh_ref[...], h_lo, nt, preferred_element_type=F32)
              + lax.dot_general(wrl_ref[...], h_hi, nt, preferred_element_type=F32)) + br_ref[...]
    eio = lax.broadcasted_iota(I32, (N_EXPERTS, tm), 0).astype(F32)
    vals, idxs = [], []
    for _ in range(TOP_K):
        m = jnp.max(logits, axis=0, keepdims=True)
        ix = jnp.min(jnp.where(logits == m, eio, float(N_EXPERTS)), axis=0, keepdims=True)
        vals.append(m)
        idxs.append(ix)
        logits = jnp.where(eio == ix, -jnp.inf, logits)
    es = [jnp.exp(v - vals[0]) for v in vals]
    den = es[0] + es[1] + es[2] + es[3]
    hot = jnp.zeros((N_EXPERTS, tm), F32)
    for r in range(TOP_K):
        hot = hot + jnp.where(eio == idxs[r], 1.0, 0.0)
    before = run_ref[...] + jnp.dot(hot.astype(BF16), tri_ref[...], preferred_element_type=F32)
    ranks = [jnp.sum(jnp.where(eio == idxs[r], before, 0.0), axis=0, keepdims=True) for r in range(TOP_K)]
    run_ref[...] = run_ref[...] + jnp.sum(hot, axis=1, keepdims=True)
    cnt_ref[...] = run_ref[...]
    rio = lax.broadcasted_iota(I32, (ROUTE_W, tm), 0)
    out = jnp.zeros((ROUTE_W, tm), F32)
    for r in range(TOP_K):
        out = jnp.where(rio == r, idxs[r], out)
        out = jnp.where(rio == TOP_K + r, es[r] / den, out)
        out = jnp.where(rio == 2 * TOP_K + r, ranks[r], out)
    rt_ref[...] = out


def _outproj(x, cv, nsa, on, w, gt1, g2, sc2, sh2, wr, br):
    B, S, D = x.shape
    tm = min(OUT_TM, S)
    nt = S // tm
    tri = (jnp.arange(tm)[:, None] < jnp.arange(tm)[None, :]).astype(BF16)
    wr_t = wr.T
    wr_hi = wr_t.astype(BF16)
    wr_lo = (wr_t - wr_hi.astype(F32)).astype(BF16)
    row = pl.BlockSpec((1, 1, D), lambda b, i: (b, 0, 0))
    vec = lambda n: pl.BlockSpec((1, n), lambda b, i: (0, 0))
    col = pl.BlockSpec((N_EXPERTS, 1), lambda b, i: (0, 0))
    wr_spec = pl.BlockSpec((N_EXPERTS, D), lambda b, i: (0, 0))
    return pl.pallas_call(
        _outproj_kernel,
        grid=(B, nt),
        in_specs=[pl.BlockSpec((1, tm, D), lambda b, i: (b, i, 0)),
                  pl.BlockSpec((1, tm, CONV_CH), lambda b, i: (b, i, 0)),
                  pl.BlockSpec((1, tm, NSA_WIDTH), lambda b, i: (b, i, 0)),
                  vec(NSA_WIDTH),
                  pl.BlockSpec((D, D), lambda b, i: (0, 0)),
                  row, vec(D), row, row,
                  wr_spec, wr_spec, col,
                  pl.BlockSpec((tm, tm), lambda b, i: (0, 0))],
        out_specs=[pl.BlockSpec((1, tm, D), lambda b, i: (b, i, 0)),
                   pl.BlockSpec((tm * ROW_TILES, LANES), lambda b, i: (b * nt + i, 0)),
                   pl.BlockSpec((ROUTE_W, tm), lambda b, i: (0, b * nt + i)),
                   col],
        out_shape=[jax.ShapeDtypeStruct((B, S, D), F32),
                   jax.ShapeDtypeStruct((B * S * ROW_TILES, LANES), F32),
                   jax.ShapeDtypeStruct((ROUTE_W, B * S), F32),
                   jax.ShapeDtypeStruct((N_EXPERTS, 1), F32)],
        scratch_shapes=[pltpu.VMEM((N_EXPERTS, 1), F32)],
        compiler_params=pltpu.CompilerParams(dimension_semantics=("arbitrary", "arbitrary")),
        name="outproj",
    )(x, cv, nsa, on, w, gt1, g2, sc2, sh2, wr_hi, wr_lo, br.reshape(N_EXPERTS, 1), tri)


def _issue_rows(idx_ref, rows, src_hbm, dst, slot, sem):
    for r in rows:
        pltpu.make_async_copy(src_hbm.at[idx_ref[0, 0, r]],
                              dst.at[slot, pl.ds(r * ROW_TILES, ROW_TILES), :],
                              sem.at[slot]).start(priority=r % 2)


def _wait_rows(dst, slot, sem):
    pltpu.make_async_copy(dst.at[slot], dst.at[slot], sem.at[slot]).wait()


def _rows_2d(buf, slot, base, n):
    return jnp.concatenate(
        [buf[slot, pl.ds(base * ROW_TILES + s, n, stride=ROW_TILES), :] for s in range(ROW_TILES)], axis=1)


def _ffn_kernel(be_ref, nb_ref, ta_ref, tb_ref, h2_hbm, gate_ref, wg_ref, bg_ref, wu_ref, bu_ref, wd_ref, bd_ref,
                o_ref, xbuf, wgb, wub, wdb, sem):
    bm = ta_ref.shape[2]
    i = pl.program_id(0)
    nb = nb_ref[0]
    slot = i % 2

    @pl.when((i == 0) | (be_ref[i] != be_ref[jnp.maximum(i - 1, 0)]))
    def _():
        wgb[...] = wg_ref[0].astype(BF16)
        wub[...] = wu_ref[0].astype(BF16)
        wdb[...] = wd_ref[0].astype(BF16)

    @pl.when(i == 0)
    def _():
        _issue_rows(ta_ref, range(bm), h2_hbm, xbuf, 0, sem)

    @pl.when(i + 1 < nb)
    def _():
        _issue_rows(tb_ref, range(bm), h2_hbm, xbuf, 1 - slot, sem)

    @pl.when(i < nb)
    def _():
        _wait_rows(xbuf, slot, sem)
        x = _rows_2d(xbuf, slot, 0, bm).astype(BF16)
        g = jnp.dot(x, wgb[...], preferred_element_type=F32) + bg_ref[0]
        u = jnp.dot(x, wub[...], preferred_element_type=F32) + bu_ref[0]
        g = jnp.minimum(g, SWIGLU_LIMIT)
        u = jnp.clip(u, -SWIGLU_LIMIT, SWIGLU_LIMIT)
        act = g * jax.nn.sigmoid(SWIGLU_ALPHA * g) * (u + 1.0)
        y = (jnp.dot(act.astype(BF16), wdb[...], preferred_element_type=F32) + bd_ref[0]) * gate_ref[...]
        for s in range(ROW_TILES):
            o_ref[pl.ds(s, bm, stride=ROW_TILES), :] = y[:, s * LANES:(s + 1) * LANES]

    @pl.when(i >= nb)
    def _():
        o_ref[...] = jnp.zeros(o_ref.shape, o_ref.dtype)


def _ffn(block_e, nb_used, buf_tok, h2_rows, buf_gate, wg, bg, wu, bu, wd, bd):
    NB = block_e.shape[0]
    bm = FFN_BM
    D, F = D_MODEL, D_FF
    tok3 = buf_tok.reshape(NB, 1, bm)
    wspec = lambda r, c: pl.BlockSpec((1, r, c), lambda i, be, nb: (be[i], 0, 0))
    vmem_limit = 2 * 3 * D * F * 4 + 3 * D * F * 2 + 4 * bm * D * 4 + 6 * bm * F * 4
    return pl.pallas_call(
        _ffn_kernel,
        grid_spec=pltpu.PrefetchScalarGridSpec(
            num_scalar_prefetch=2,
            grid=(NB,),
            in_specs=[pl.BlockSpec((1, 1, bm), lambda i, be, nb: (i, 0, 0), memory_space=pltpu.SMEM),
                      pl.BlockSpec((1, 1, bm), lambda i, be, nb: (jnp.minimum(i + 1, NB - 1), 0, 0),
                                   memory_space=pltpu.SMEM),
                      pl.BlockSpec(memory_space=pl.ANY),
                      pl.BlockSpec((bm, 1), lambda i, be, nb: (i, 0)),
                      wspec(D, F), wspec(1, F), wspec(D, F), wspec(1, F), wspec(F, D), wspec(1, D)],
            out_specs=pl.BlockSpec((bm * ROW_TILES, LANES), lambda i, be, nb: (i, 0)),
            scratch_shapes=[pltpu.VMEM((2, bm * ROW_TILES, LANES), F32),
                            pltpu.VMEM((D, F), BF16), pltpu.VMEM((D, F), BF16), pltpu.VMEM((F, D), BF16),
                            pltpu.SemaphoreType.DMA((2,))]),
        out_shape=jax.ShapeDtypeStruct((NB * bm * ROW_TILES, LANES), F32),
        compiler_params=pltpu.CompilerParams(dimension_semantics=("arbitrary",),
                                             vmem_limit_bytes=vmem_limit),
        name="ffn",
    )(block_e, nb_used, tok3, tok3, h2_rows, buf_gate, wg, bg, wu, bu, wd, bd)


def _combine_kernel(da_ref, db_ref, y_hbm, x1_ref, gt2_ref, fg_ref, o_ref, buf, sem):
    tm = x1_ref.shape[0]
    n = TOP_K * tm
    i = pl.program_id(0)
    slot = i % 2

    @pl.when(i == 0)
    def _():
        _issue_rows(da_ref, range(n), y_hbm, buf, 0, sem)

    @pl.when(i + 1 < pl.num_programs(0))
    def _():
        _issue_rows(db_ref, range(n), y_hbm, buf, 1 - slot, sem)

    _wait_rows(buf, slot, sem)
    y = _rows_2d(buf, slot, 0, tm)
    for k in range(1, TOP_K):
        y = y + _rows_2d(buf, slot, k * tm, tm)
    x2 = x1_ref[...] + gt2_ref[0] * y
    o_ref[...] = _rms(x2, fg_ref[...])


def _combine(dest3, y_rows, x1, gt2, fg, S):
    T, D = x1.shape
    tm = min(COMB_TM, S)
    NT = T // tm
    per_b = S // tm
    n = TOP_K * tm
    return pl.pallas_call(
        _combine_kernel,
        grid=(NT,),
        in_specs=[pl.BlockSpec((1, 1, n), lambda i: (i, 0, 0), memory_space=pltpu.SMEM),
                  pl.BlockSpec((1, 1, n), lambda i: (jnp.minimum(i + 1, NT - 1), 0, 0), memory_space=pltpu.SMEM),
                  pl.BlockSpec(memory_space=pl.ANY),
                  pl.BlockSpec((tm, D), lambda i: (i, 0)),
                  pl.BlockSpec((1, 1, D), lambda i: (i // per_b, 0, 0)),
                  pl.BlockSpec((1, D), lambda i: (0, 0))],
        out_specs=pl.BlockSpec((tm, D), lambda i: (i, 0)),
        out_shape=jax.ShapeDtypeStruct((T, D), F32),
        scratch_shapes=[pltpu.VMEM((2, n * ROW_TILES, LANES), F32), pltpu.SemaphoreType.DMA((2,))],
        compiler_params=pltpu.CompilerParams(dimension_semantics=("arbitrary",)),
        name="combine",
    )(dest3, dest3, y_rows, x1, gt2, fg)


def _rope_tables(S):
    inv = ROPE_THETA ** (-jnp.arange(0, ROT_DIM, 2, dtype=F32) / ROT_DIM)
    ang = jnp.arange(S, dtype=F32)[:, None] * inv[None, :]
    cos, sin = jnp.cos(ang), jnp.sin(ang)
    d = jnp.arange(KV_W) % HEAD_DIM
    first, second = d < ROT_HALF, (d >= ROT_HALF) & (d < ROT_DIM)
    cos_l = cos[:, d % ROT_HALF]
    sin_l = sin[:, d % ROT_HALF]
    rc = jnp.where((d < ROT_DIM)[None], cos_l, 1.0)
    rs1 = jnp.where(second[None], sin_l, 0.0)
    rs2 = jnp.where(first[None], -sin_l, 0.0)
    return rc, rs1, rs2, cos.T, sin.T


def _route_plan(route, counts, T):
    bm = FFN_BM
    A = T * TOP_K
    idx = route[0:TOP_K].astype(I32)
    gate = route[TOP_K:2 * TOP_K]
    rank = route[2 * TOP_K:3 * TOP_K].astype(I32)
    keys = idx * T + jnp.arange(T, dtype=I32)[None, :]
    skey, sgate = lax.sort((keys.reshape(A), gate.reshape(A)), num_keys=1)
    counts = counts.astype(I32)
    starts = jnp.cumsum(counts) - counts
    padded = (counts + bm - 1) // bm * bm
    pends = jnp.cumsum(padded)
    pstarts = pends - padded
    P = (A + N_EXPERTS * bm + bm - 1) // bm * bm
    NB = P // bm
    blk0 = jnp.arange(NB, dtype=I32) * bm
    block_e = jnp.minimum(jnp.sum((pends[None, :] <= blk0[:, None]).astype(I32), axis=1), N_EXPERTS - 1)
    r = (blk0 - pstarts[block_e])[:, None] + jnp.arange(bm, dtype=I32)[None, :]
    valid = r < counts[block_e][:, None]
    src = jnp.clip(starts[block_e][:, None] + r, 0, A - 1)
    buf_tok = jnp.where(valid, skey[src] - block_e[:, None] * T, 0)
    buf_gate = jnp.where(valid, sgate[src], 0.0).reshape(P, 1)
    dest = rank
    for e in range(N_EXPERTS):
        dest = dest + jnp.where(idx == e, pstarts[e], 0)
    nb_used = (pends[-1] // bm).astype(I32).reshape(1)
    return block_e, nb_used, buf_tok, buf_gate, dest


def kernel(x, c, norm1_g, norm2_g, w_ada, b_ada, w_in, conv_w, conv_b, conv_ln_g, conv_ln_b, cmp_pe_k, cmp_pe_v,
           cmp_k_w1, cmp_k_w2, cmp_v_w1, cmp_v_w2, out_norm_conv, out_norm_nsa, w_out, w_router, b_router,
           w_gate, b_gate, w_up, b_up, w_down, b_down, final_norm_g):
    B, S, D = x.shape
    T = B * S
    G = KV_GROUPS
    assert D == D_MODEL and S % ATTN_TK == 0 and S % CMP_STRIDE == 0 and KV_W == LANES
    rc, rs1, rs2, cos_t, sin_t = _rope_tables(S)
    n_sel = S // SEL_BLK
    nc = S // CMP_STRIDE
    cstart = jnp.arange(nc) * CMP_STRIDE
    jstart = jnp.arange(n_sel) * SEL_BLK
    overlap_t = ((cstart[None, :] <= jstart[:, None] + SEL_BLK - 1)
                 & (cstart[None, :] + CMP_LEN - 1 >= jstart[:, None])
                 & (jnp.arange(nc)[None, :] < nc - 1)).astype(F32)

    assert w_ada.shape[0] == 1
    for l in range(1):
        mod = _adaln(c, w_ada[l], b_ada[l][None])
        sh1, sc1, gt1, sh2, sc2, gt2 = [m[:, None, :] for m in jnp.split(mod, 6, axis=-1)]

        wl = w_in[l]
        o = 2 * CONV_CH + NSA_WIDTH
        kvc = [wl[:, o + i * KV_W:o + (i + 1) * KV_W] for i in range(6)]
        gl = wl[:, o + 6 * KV_W:]
        per_g = 3 * Q_PER_G
        gpad = [jnp.pad(gl[:, per_g * g:per_g * (g + 1)], ((0, 0), (0, GATE_ROWS - per_g))) for g in range(G)]
        wn = jnp.concatenate([wl[:, :2 * CONV_CH], kvc[0], kvc[2], kvc[4], kvc[1]], axis=1).astype(BF16)
        wt = jnp.concatenate([wl[:, 2 * CONV_CH:o], kvc[3], kvc[5]] + gpad, axis=1).T.astype(BF16)
        u, kc, vc, ks, kw, qt, vst, vwt, gates = _inproj(x, sc1, sh1, norm1_g[l][None], wn, wt,
                                                         rc, rs1, rs2, cos_t, sin_t)

        conv_n = _conv(u, conv_w[l], conv_b[l][None], conv_ln_g[l][None], conv_ln_b[l][None],
                       out_norm_conv[l][None])

        kcmp, vcmp_t = _compress(kc, vc, cmp_pe_k[l].reshape(1, -1), cmp_pe_v[l].reshape(1, -1),
                                 cmp_k_w1[l], cmp_k_w2[l], cmp_v_w1[l], cmp_v_w2[l].T)
        nsa = _attention(qt, kcmp, vcmp_t, ks, vst, kw, vwt, gates, overlap_t)

        x1, h2_rows, route, counts = _outproj(x, conv_n, nsa, out_norm_nsa[l][None], w_out[l].astype(BF16), gt1,
                                              norm2_g[l][None], sc2, sh2, w_router[l], b_router[l][None])

        block_e, nb_used, buf_tok, buf_gate, dest = _route_plan(route, counts[:, 0], T)
        y_rows = _ffn(block_e, nb_used, buf_tok, h2_rows.reshape(T, ROW_TILES, LANES), buf_gate,
                      w_gate[l], b_gate[l][:, None, :], w_up[l], b_up[l][:, None, :],
                      w_down[l], b_down[l][:, None, :])
        tm = min(COMB_TM, S)
        dest3 = dest.reshape(TOP_K, T // tm, tm).transpose(1, 0, 2).reshape(T // tm, 1, TOP_K * tm)
        P = y_rows.shape[0] // ROW_TILES
        x = _combine(dest3, y_rows.reshape(P, ROW_TILES, LANES), x1.reshape(T, D), gt2, final_norm_g[None],
                     S).reshape(B, S, D)
    return x
```

```python
import functools

import jax
import jax.numpy as jnp
from jax import lax
from jax.experimental import pallas as pl
from jax.experimental.pallas import tpu as pltpu

F32 = jnp.float32
BF16 = jnp.bfloat16
I32 = jnp.int32
HI = lax.Precision.HIGHEST

D_MODEL = 1024
CONV_CH = 512
CONV_WIDTH = 31
NSA_HEADS = 8
KV_GROUPS = 2
Q_PER_G = NSA_HEADS // KV_GROUPS
HEAD_DIM = 64
NSA_WIDTH = NSA_HEADS * HEAD_DIM
KV_W = KV_GROUPS * HEAD_DIM
ROT_DIM = HEAD_DIM // 4
ROT_HALF = ROT_DIM // 2
ROPE_THETA = 500000.0
CMP_LEN = 32
CMP_STRIDE = 16
CMP_HIDDEN = 128
SEL_BLK = 64
SEL_TOPN = 16
WINDOW = 512
N_EXPERTS = 32
TOP_K = 4
D_FF = 1024
SWIGLU_ALPHA = 1.702
SWIGLU_LIMIT = 7.0
NORM_EPS = 1e-5
NEG_INF = -1e30
FORCE_SCORE = 1e9
LOG2_E = 1.4426950408889634

LANES = 128
SUBLANES = 8
ROW_TILES = D_MODEL // LANES

GATE_ROWS = 16

INPROJ_TM = 1024
CONV_TR = 512
CONV_HALO = 32
ATTN_TQ = 512
ATTN_TK = 512
OUT_TM = 1024
FFN_BM = 512
COMB_TM = 256
ROUTE_W = 16


def _rms(x, g):
    return x * lax.rsqrt(jnp.mean(x * x, axis=-1, keepdims=True) + NORM_EPS) * g


def _split_bf16(a):
    hi = a.astype(BF16)
    return hi, (a - hi.astype(F32)).astype(BF16)


def _dot3(a, b, dims=(((1,), (0,)), ((), ()))):
    ah, al = _split_bf16(a)
    bh, bl = _split_bf16(b)
    d = lambda x, y: lax.dot_general(x, y, dims, preferred_element_type=F32)
    return d(ah, bh) + d(al, bh) + d(ah, bl)


def _ada_kernel(c_ref, w_ref, b_ref, o_ref):
    c = c_ref[...]
    ca = c * jax.nn.sigmoid(c)
    o_ref[...] = jnp.dot(ca, w_ref[...], preferred_element_type=F32, precision=HI) + b_ref[...]


def _adaln(c, w, b):
    B = c.shape[0]
    D = D_MODEL
    return pl.pallas_call(
        _ada_kernel,
        grid=(6,),
        in_specs=[pl.BlockSpec((B, D), lambda j: (0, 0)),
                  pl.BlockSpec((D, D), lambda j: (0, j)),
                  pl.BlockSpec((1, D), lambda j: (0, j))],
        out_specs=pl.BlockSpec((B, D), lambda j: (0, j)),
        out_shape=jax.ShapeDtypeStruct((B, 6 * D), F32),
        name="adaln",
    )(c, w, b)


def _inproj_kernel(x_ref, sc_ref, sh_ref, g_ref, wn_ref, wt_ref, rc_ref, rs1_ref, rs2_ref, ct_ref, st_ref,
                   u_ref, kc_ref, vc_ref, ks_ref, kw_ref, qt_ref, vst_ref, vwt_ref, gt_ref, stage_ref):
    tm = x_ref.shape[1]
    tq, tk = ATTN_TQ, ATTN_TK
    h = (_rms(x_ref[0], g_ref[...]) * (1.0 + sc_ref[0]) + sh_ref[0]).astype(BF16)

    p = jnp.dot(h, wn_ref[...], preferred_element_type=F32)
    u_ref[0] = p[:, 0:CONV_CH] * jax.nn.sigmoid(p[:, CONV_CH:2 * CONV_CH])
    c0 = 2 * CONV_CH
    rc, rs1, rs2 = rc_ref[...], rs1_ref[...], rs2_ref[...]
    for ref, roped in ((kc_ref, True), (ks_ref, True), (kw_ref, True), (vc_ref, False)):
        v = p[:, c0:c0 + KV_W]
        if roped:
            v = v * rc + pltpu.roll(v, ROT_HALF, 1) * rs1 + pltpu.roll(v, KV_W - ROT_HALF, 1) * rs2
        c0 += KV_W
        if ref is kc_ref or ref is vc_ref:
            stage_ref[...] = v
            left = lax.broadcasted_iota(I32, (tm // CMP_STRIDE, KV_W), 1) < HEAD_DIM
            pieces = [stage_ref[pl.ds(tl, tm // CMP_STRIDE, stride=CMP_STRIDE), :] for tl in range(CMP_STRIDE)]
            for gg in range(KV_GROUPS):
                cols = []
                for tl in range(0, CMP_STRIDE, 2):
                    a, b = pieces[tl], pieces[tl + 1]
                    if gg == 0:
                        cols.append(jnp.where(left, a, pltpu.roll(b, HEAD_DIM, 1)))
                    else:
                        cols.append(jnp.where(left, pltpu.roll(a, HEAD_DIM, 1), b))
                ref[0, gg] = jnp.concatenate(cols, axis=1)
            continue
        for gg in range(KV_GROUPS):
            ref[0, gg] = v[:, HEAD_DIM * gg:HEAD_DIM * (gg + 1)].astype(ref.dtype)

    pt = lax.dot_general(wt_ref[...], h, (((1,), (1,)), ((), ())), preferred_element_type=F32)
    cos_t, sin_t = ct_ref[...], st_ref[...]
    scale = HEAD_DIM ** -0.5 * LOG2_E
    for hh in range(NSA_HEADS):
        blk = pt[HEAD_DIM * hh:HEAD_DIM * (hh + 1), :]
        x1, x2 = blk[0:ROT_HALF], blk[ROT_HALF:ROT_DIM]
        qh = (jnp.concatenate([x1 * cos_t - x2 * sin_t, x2 * cos_t + x1 * sin_t, blk[ROT_DIM:]], axis=0)
              * scale).astype(BF16)
        gg, n = divmod(hh, Q_PER_G)
        for j in range(tm // tq):
            qt_ref[0, gg, j, :, n * tq:(n + 1) * tq] = qh[:, j * tq:(j + 1) * tq]
    r0 = NSA_WIDTH
    for ref in (vst_ref, vwt_ref):
        for gg in range(KV_GROUPS):
            blk = pt[r0 + HEAD_DIM * gg:r0 + HEAD_DIM * (gg + 1), :].astype(BF16)
            for j in range(tm // tk):
                ref[0, gg, j] = blk[:, j * tk:(j + 1) * tk]
        r0 += KV_W
    for gg in range(KV_GROUPS):
        gt_ref[0, gg] = jax.nn.sigmoid(pt[r0 + GATE_ROWS * gg:r0 + GATE_ROWS * (gg + 1), :])


def _inproj(x, sc, sh, g, wn, wt, rc, rs1, rs2, cos_t, sin_t):
    B, S, D = x.shape
    tm = min(INPROJ_TM, S)
    tq, tk = ATTN_TQ, ATTN_TK
    G = KV_GROUPS
    kv = lambda dt: jax.ShapeDtypeStruct((B, G, S, HEAD_DIM), dt)
    kv_spec = pl.BlockSpec((1, G, tm, HEAD_DIM), lambda b, i: (b, 0, i, 0))
    chunk = CMP_STRIDE * HEAD_DIM
    ck_shape = jax.ShapeDtypeStruct((B, G, S // CMP_STRIDE, chunk), F32)
    ck_spec = pl.BlockSpec((1, G, tm // CMP_STRIDE, chunk), lambda b, i: (b, 0, i, 0))
    vt_shape = jax.ShapeDtypeStruct((B, G, S // tk, HEAD_DIM, tk), BF16)
    vt_spec = pl.BlockSpec((1, G, tm // tk, HEAD_DIM, tk), lambda b, i: (b, 0, i, 0, 0))
    row = pl.BlockSpec((1, 1, D), lambda b, i: (b, 0, 0))
    tab = pl.BlockSpec((tm, LANES), lambda b, i: (i, 0))
    tab_t = pl.BlockSpec((ROT_HALF, tm), lambda b, i: (0, i))
    return pl.pallas_call(
        _inproj_kernel,
        grid=(B, S // tm),
        in_specs=[pl.BlockSpec((1, tm, D), lambda b, i: (b, i, 0)), row, row,
                  pl.BlockSpec((1, D), lambda b, i: (0, 0)),
                  pl.BlockSpec(wn.shape, lambda b, i: (0, 0)),
                  pl.BlockSpec(wt.shape, lambda b, i: (0, 0)),
                  tab, tab, tab, tab_t, tab_t],
        out_specs=[pl.BlockSpec((1, tm, CONV_CH), lambda b, i: (b, i, 0)),
                   ck_spec, ck_spec, kv_spec, kv_spec,
                   pl.BlockSpec((1, G, tm // tq, HEAD_DIM, Q_PER_G * tq), lambda b, i: (b, 0, i, 0, 0)),
                   vt_spec, vt_spec,
                   pl.BlockSpec((1, G, GATE_ROWS, tm), lambda b, i: (b, 0, 0, i))],
        out_shape=[jax.ShapeDtypeStruct((B, S, CONV_CH), F32),
                   ck_shape, ck_shape, kv(BF16), kv(BF16),
                   jax.ShapeDtypeStruct((B, G, S // tq, HEAD_DIM, Q_PER_G * tq), BF16),
                   vt_shape, vt_shape,
                   jax.ShapeDtypeStruct((B, G, GATE_ROWS, S), F32)],
        scratch_shapes=[pltpu.VMEM((tm, KV_W), F32)],
        compiler_params=pltpu.CompilerParams(dimension_semantics=("parallel", "parallel")),
        name="inproj",
    )(x, sc, sh, g, wn, wt, rc, rs1, rs2, cos_t, sin_t)


def _conv_kernel(prev_ref, cur_ref, w_ref, cb_ref, lg_ref, lb_ref, on_ref, o_ref, pad_ref, win_ref):
    tr = cur_ref.shape[1]
    first = pl.program_id(1) == 0
    halo = prev_ref[0, tr - CONV_HALO:tr, :]
    pad_ref[0:CONV_HALO, :] = jnp.where(first, 0.0, halo)
    pad_ref[CONV_HALO:CONV_HALO + tr, :] = cur_ref[0]
    off = CONV_HALO - (CONV_WIDTH - 1)
    acc = jnp.zeros((tr, CONV_CH), F32)
    for b in range(SUBLANES):
        taps = range(b, CONV_WIDTH, SUBLANES)
        rows = tr + SUBLANES * (len(taps) - 1)
        win_ref[b, 0:rows, :] = pad_ref[off + b:off + b + rows, :]
        for a, k in enumerate(taps):
            acc = acc + win_ref[b, SUBLANES * a:SUBLANES * a + tr, :] * w_ref[k:k + 1, :]
    y = acc + cb_ref[...]
    mu = jnp.mean(y, axis=-1, keepdims=True)
    yc = y - mu
    var = jnp.mean(yc * yc, axis=-1, keepdims=True)
    yn = yc * lax.rsqrt(var + NORM_EPS) * lg_ref[...] + lb_ref[...]
    s = yn * jax.nn.sigmoid(yn)
    o_ref[0] = _rms(s, on_ref[...]).astype(o_ref.dtype)


def _conv(u, w, cb, lg, lb, on):
    B, S, C = u.shape
    tr = min(CONV_TR, S)
    vec = pl.BlockSpec((1, C), lambda b, i: (0, 0))
    return pl.pallas_call(
        _conv_kernel,
        grid=(B, S // tr),
        in_specs=[pl.BlockSpec((1, tr, C), lambda b, i: (b, jnp.maximum(i - 1, 0), 0)),
                  pl.BlockSpec((1, tr, C), lambda b, i: (b, i, 0)),
                  pl.BlockSpec((CONV_WIDTH, C), lambda b, i: (0, 0)),
                  vec, vec, vec, vec],
        out_specs=pl.BlockSpec((1, tr, C), lambda b, i: (b, i, 0)),
        out_shape=jax.ShapeDtypeStruct((B, S, C), BF16),
        scratch_shapes=[pltpu.VMEM((CONV_HALO + tr, C), F32),
                        pltpu.VMEM((SUBLANES, tr + SUBLANES * ((CONV_WIDTH - 1) // SUBLANES), C), F32)],
        compiler_params=pltpu.CompilerParams(dimension_semantics=("parallel", "parallel")),
        name="conv",
    )(u, u, w, cb, lg, lb, on)


def _cmp_kernel(kx_ref, vx_ref, pek_ref, pev_ref, kw1_ref, kw2_ref, vw1_ref, vw2t_ref, ko_ref, vo_ref):
    nc = kx_ref.shape[2]
    half = kx_ref.shape[3]
    nt = (((1,), (1,)), ((), ()))
    for x_ref, pe_ref, w1_ref, w2_ref, o_ref, transposed in ((kx_ref, pek_ref, kw1_ref, kw2_ref, ko_ref, False),
                                                             (vx_ref, pev_ref, vw1_ref, vw2t_ref, vo_ref, True)):
        w1 = w1_ref[...]
        pe = jnp.broadcast_to(pe_ref[...], (SUBLANES, 2 * half))
        pe_proj = _dot3(pe, w1)[0:1]
        for gg in range(KV_GROUPS):
            xg = x_ref[0, gg]
            first = _dot3(xg, w1[0:half])
            second = _dot3(xg, w1[half:2 * half])
            hid = first + pltpu.roll(second, nc - 1, 0) + pe_proj
            hid = hid * jax.nn.sigmoid(hid)
            if transposed:
                o_ref[0, gg] = _dot3(w2_ref[...], hid, nt)
            else:
                o_ref[0, gg] = _dot3(hid, w2_ref[...])


def _compress(kx, vx, pek, pev, kw1, kw2, vw1, vw2t):
    B, G, NC, W = kx.shape
    xs = pl.BlockSpec((1, G, NC, W), lambda b: (b, 0, 0, 0))
    full = lambda a: pl.BlockSpec(a.shape, lambda b: (0,) * a.ndim)
    return pl.pallas_call(
        _cmp_kernel,
        grid=(B,),
        in_specs=[xs, xs, full(pek), full(pev), full(kw1), full(kw2), full(vw1), full(vw2t)],
        out_specs=[pl.BlockSpec((1, G, NC, HEAD_DIM), lambda b: (b, 0, 0, 0)),
                   pl.BlockSpec((1, G, HEAD_DIM, NC), lambda b: (b, 0, 0, 0))],
        out_shape=[jax.ShapeDtypeStruct((B, G, NC, HEAD_DIM), F32),
                   jax.ShapeDtypeStruct((B, G, HEAD_DIM, NC), F32)],
        compiler_params=pltpu.CompilerParams(dimension_semantics=("parallel",)),
        name="compress",
    )(kx, vx, pek, pev, kw1, kw2, vw1, vw2t)


def _attn_kernel(qt_ref, kc_ref, vct_ref, ks_ref, vst_ref, kw_ref, vwt_ref, gt_ref, ovt_ref, o_ref, sel_ref):
    tq, tk = ATTN_TQ, ATTN_TK
    R = Q_PER_G * tq
    per_tile = tk // SEL_BLK
    S = ks_ref.shape[2]
    ncp = kc_ref.shape[2]
    nsel = ovt_ref.shape[0]
    qi = pl.program_id(2)
    q0 = qi * tq
    qt = qt_ref[0, 0, 0]
    t_row = q0 + lax.broadcasted_iota(I32, (1, tq), 1)
    heads = lambda a: jnp.concatenate([a] * Q_PER_G, axis=1)

    sc = jnp.dot(kc_ref[0, 0].astype(BF16), qt, preferred_element_type=F32)
    c_io = lax.broadcasted_iota(I32, (ncp, tq), 0)
    m_c = (c_io * CMP_STRIDE + (CMP_LEN - 1) <= t_row) & (c_io < ncp - 1)
    scb = sc + heads(jnp.where(m_c, 0.0, NEG_INF))
    e = jnp.exp2(scb - jnp.max(scb, axis=0, keepdims=True)) * heads(jnp.where(m_c, 1.0, 0.0))
    den = jnp.sum(e, axis=0, keepdims=True)
    pc = e / jnp.where(den > 0.0, den, 1.0)
    o_cmp = jnp.dot(vct_ref[0, 0].astype(BF16), pc.astype(BF16), preferred_element_type=F32)

    psum = pc[:, 0:tq]
    for n in range(1, Q_PER_G):
        psum = psum + pc[:, n * tq:(n + 1) * tq]
    imp = _dot3(ovt_ref[...], psum)
    j_io = lax.broadcasted_iota(I32, (nsel, tq), 0)
    cur = t_row // SEL_BLK
    valid = j_io * SEL_BLK <= t_row
    forced = (j_io == 0) | (j_io == cur) | (j_io == cur - 1)
    score = jnp.where(valid, jnp.where(forced, FORCE_SCORE, imp), NEG_INF)
    rank = jnp.zeros((nsel, tq), F32)
    for i in range(nsel):
        row = score[i:i + 1, :]
        tie = jnp.where(j_io > i, 1.0, 0.0)
        rank = rank + jnp.where(row > score, 1.0, jnp.where(row == score, tie, 0.0))
    sel_bias = jnp.where(rank < float(min(SEL_TOPN, nsel)), 0.0, NEG_INF)
    sel_ref[...] = jnp.zeros(sel_ref.shape, F32)
    for jj in range(S // tk):
        sel_ref[jj, 0:per_tile, :] = sel_bias[jj * per_tile:(jj + 1) * per_tile, :]

    k_io = lax.broadcasted_iota(I32, (tk, tq), 0)

    def flash_step(k_ref, vt_ref, kj, bias, carry):
        m, l, acc = carry
        k0 = pl.multiple_of(kj * tk, tk)
        s = jnp.dot(k_ref[0, 0, pl.ds(k0, tk), :], qt, preferred_element_type=F32) + heads(bias)
        m_new = jnp.maximum(m, jnp.max(s, axis=0, keepdims=True))
        alpha = jnp.exp2(m - m_new)
        p = jnp.exp2(s - m_new)
        l = alpha * l + jnp.sum(p, axis=0, keepdims=True)
        acc = alpha * acc + jnp.dot(vt_ref[0, 0, kj], p.astype(BF16), preferred_element_type=F32)
        return m_new, l, acc

    init = (jnp.full((1, R), NEG_INF, F32), jnp.zeros((1, R), F32), jnp.zeros((HEAD_DIM, R), F32))

    def slc_body(kj, carry):
        blocks = sel_ref[kj]
        bias = jnp.concatenate([jnp.broadcast_to(blocks[b:b + 1, :], (SEL_BLK, tq)) for b in range(per_tile)], axis=0)
        bias = jnp.where(kj * tk + k_io <= t_row, bias, NEG_INF)
        return flash_step(ks_ref, vst_ref, kj, bias, carry)

    def win_body(kj, carry):
        rel = t_row - (kj * tk + k_io)
        bias = jnp.where((rel >= 0) & (rel < WINDOW), 0.0, NEG_INF)
        return flash_step(kw_ref, vwt_ref, kj, bias, carry)

    n_slc = (q0 + tq + tk - 1) // tk
    lo_tile = jnp.maximum(q0 - (WINDOW - 1), 0) // tk
    n_pair = lo_tile // 2

    def pair_body(i, carry):
        return slc_body(2 * i, carry[0]), slc_body(2 * i + 1, carry[1])

    st_a, st_b = lax.fori_loop(0, n_pair, pair_body, (init, init))
    st_a = lax.fori_loop(2 * n_pair, lo_tile, slc_body, st_a)

    def both_body(kj, carry):
        return slc_body(kj, carry[0]), win_body(kj, carry[1])

    st_a, (_, l_w, acc_w) = lax.fori_loop(lo_tile, n_slc, both_body, (st_a, init))
    m_s = jnp.maximum(st_a[0], st_b[0])
    w_a, w_b = jnp.exp2(st_a[0] - m_s), jnp.exp2(st_b[0] - m_s)
    l_s = w_a * st_a[1] + w_b * st_b[1]
    acc_s = w_a * st_a[2] + w_b * st_b[2]


    gt = gt_ref[0, 0]
    o_slc = acc_s / l_s
    o_win = acc_w / l_w
    outs = []
    for n in range(Q_PER_G):
        cols = slice(n * tq, (n + 1) * tq)
        outs.append(gt[3 * n:3 * n + 1, :] * o_cmp[:, cols] + gt[3 * n + 1:3 * n + 2, :] * o_slc[:, cols]
                    + gt[3 * n + 2:3 * n + 3, :] * o_win[:, cols])
    o_ref[0] = jnp.concatenate(outs, axis=0).T


def _attention(qt, kc, vct, ks, vst, kw, vwt, gt, overlap_t):
    B, G, NQT, _, R = qt.shape
    S = ks.shape[2]
    tq, tk = ATTN_TQ, ATTN_TK
    ncp = kc.shape[2]
    kfull = pl.BlockSpec((1, 1, S, HEAD_DIM), lambda b, g, i: (b, g, 0, 0))
    vfull = pl.BlockSpec((1, 1, S // tk, HEAD_DIM, tk), lambda b, g, i: (b, g, 0, 0, 0))
    return pl.pallas_call(
        _attn_kernel,
        grid=(B, G, NQT),
        in_specs=[pl.BlockSpec((1, 1, 1, HEAD_DIM, R), lambda b, g, i: (b, g, i, 0, 0)),
                  pl.BlockSpec((1, 1, ncp, HEAD_DIM), lambda b, g, i: (b, g, 0, 0)),
                  pl.BlockSpec((1, 1, HEAD_DIM, ncp), lambda b, g, i: (b, g, 0, 0)),
                  kfull, vfull, kfull, vfull,
                  pl.BlockSpec((1, 1, GATE_ROWS, tq), lambda b, g, i: (b, g, 0, i)),
                  pl.BlockSpec(overlap_t.shape, lambda b, g, i: (0, 0))],
        out_specs=pl.BlockSpec((1, tq, Q_PER_G * HEAD_DIM), lambda b, g, i: (b, i, g)),
        out_shape=jax.ShapeDtypeStruct((B, S, NSA_WIDTH), F32),
        scratch_shapes=[pltpu.VMEM((S // tk, SUBLANES, tq), F32)],
        compiler_params=pltpu.CompilerParams(dimension_semantics=("parallel", "parallel", "arbitrary")),
        name="attn",
    )(qt, kc, vct, ks, vst, kw, vwt, gt, overlap_t)


def _outproj_kernel(x_ref, cv_ref, nsa_ref, on_ref, w_ref, gt1_ref, g2_ref, sc2_ref, sh2_ref, wrh_ref, wrl_ref, br_ref,
                    tri_ref, x1_ref, h2_ref, rt_ref, cnt_ref, run_ref):
    tm = x_ref.shape[1]

    @pl.when((pl.program_id(0) == 0) & (pl.program_id(1) == 0))
    def _():
        run_ref[...] = jnp.zeros(run_ref.shape, F32)

    nn = _rms(nsa_ref[0], on_ref[...]).astype(BF16)
    y = jnp.dot(jnp.concatenate([cv_ref[0], nn], axis=1), w_ref[...], preferred_element_type=F32)
    x1 = x_ref[0] + gt1_ref[0] * y
    x1_ref[0] = x1
    h2 = _rms(x1, g2_ref[...]) * (1.0 + sc2_ref[0]) + sh2_ref[0]
    for s in range(ROW_TILES):
        h2_ref[pl.ds(s, tm, stride=ROW_TILES), :] = h2[:, s * LANES:(s + 1) * LANES]
    nt = (((1,), (1,)), ((), ()))
    h_hi = h2.astype(BF16)
    h_lo = (h2 - h_hi.astype(F32)).astype(BF16)
    logits = (lax.dot_general(wrh_ref[...], h_hi, nt, preferred_element_type=F32)
              + lax.dot_general(wrh_ref[...], h_lo, nt, preferred_element_type=F32)
              + lax.dot_general(wrl_ref[...], h_hi, nt, preferred_element_type=F32)) + br_ref[...]
    eio = lax.broadcasted_iota(I32, (N_EXPERTS, tm), 0).astype(F32)
    vals, idxs = [], []
    for _ in range(TOP_K):
        m = jnp.max(logits, axis=0, keepdims=True)
        ix = jnp.min(jnp.where(logits == m, eio, float(N_EXPERTS)), axis=0, keepdims=True)
        vals.append(m)
        idxs.append(ix)
        logits = jnp.where(eio == ix, -jnp.inf, logits)
    es = [jnp.exp(v - vals[0]) for v in vals]
    den = es[0] + es[1] + es[2] + es[3]
    hot = jnp.zeros((N_EXPERTS, tm), F32)
    for r in range(TOP_K):
        hot = hot + jnp.where(eio == idxs[r], 1.0, 0.0)
    before = run_ref[...] + jnp.dot(hot.astype(BF16), tri_ref[...], preferred_element_type=F32)
    ranks = [jnp.sum(jnp.where(eio == idxs[r], before, 0.0), axis=0, keepdims=True) for r in range(TOP_K)]
    run_ref[...] = run_ref[...] + jnp.sum(hot, axis=1, keepdims=True)
    cnt_ref[...] = run_ref[...]
    rio = lax.broadcasted_iota(I32, (ROUTE_W, tm), 0)
    out = jnp.zeros((ROUTE_W, tm), F32)
    for r in range(TOP_K):
        out = jnp.where(rio == r, idxs[r], out)
        out = jnp.where(rio == TOP_K + r, es[r] / den, out)
        out = jnp.where(rio == 2 * TOP_K + r, ranks[r], out)
    rt_ref[...] = out


def _outproj(x, cv, nsa, on, w, gt1, g2, sc2, sh2, wr, br):
    B, S, D = x.shape
    tm = min(OUT_TM, S)
    nt = S // tm
    tri = (jnp.arange(tm)[:, None] < jnp.arange(tm)[None, :]).astype(BF16)
    wr_t = wr.T
    wr_hi = wr_t.astype(BF16)
    wr_lo = (wr_t - wr_hi.astype(F32)).astype(BF16)
    row = pl.BlockSpec((1, 1, D), lambda b, i: (b, 0, 0))
    vec = lambda n: pl.BlockSpec((1, n), lambda b, i: (0, 0))
    col = pl.BlockSpec((N_EXPERTS, 1), lambda b, i: (0, 0))
    wr_spec = pl.BlockSpec((N_EXPERTS, D), lambda b, i: (0, 0))
    return pl.pallas_call(
        _outproj_kernel,
        grid=(B, nt),
        in_specs=[pl.BlockSpec((1, tm, D), lambda b, i: (b, i, 0)),
                  pl.BlockSpec((1, tm, CONV_CH), lambda b, i: (b, i, 0)),
                  pl.BlockSpec((1, tm, NSA_WIDTH), lambda b, i: (b, i, 0)),
                  vec(NSA_WIDTH),
                  pl.BlockSpec((D, D), lambda b, i: (0, 0)),
                  row, vec(D), row, row,
                  wr_spec, wr_spec, col,
                  pl.BlockSpec((tm, tm), lambda b, i: (0, 0))],
        out_specs=[pl.BlockSpec((1, tm, D), lambda b, i: (b, i, 0)),
                   pl.BlockSpec((tm * ROW_TILES, LANES), lambda b, i: (b * nt + i, 0)),
                   pl.BlockSpec((ROUTE_W, tm), lambda b, i: (0, b * nt + i)),
                   col],
        out_shape=[jax.ShapeDtypeStruct((B, S, D), F32),
                   jax.ShapeDtypeStruct((B * S * ROW_TILES, LANES), F32),
                   jax.ShapeDtypeStruct((ROUTE_W, B * S), F32),
                   jax.ShapeDtypeStruct((N_EXPERTS, 1), F32)],
        scratch_shapes=[pltpu.VMEM((N_EXPERTS, 1), F32)],
        compiler_params=pltpu.CompilerParams(dimension_semantics=("arbitrary", "arbitrary")),
        name="outproj",
    )(x, cv, nsa, on, w, gt1, g2, sc2, sh2, wr_hi, wr_lo, br.reshape(N_EXPERTS, 1), tri)


def _issue_rows(idx_ref, rows, src_hbm, dst, slot, sem, queues=2):
    for r in rows:
        pltpu.make_async_copy(src_hbm.at[idx_ref[0, 0, r]],
                              dst.at[slot, pl.ds(r * ROW_TILES, ROW_TILES), :],
                              sem.at[slot]).start(priority=r % 2 if queues == 2 else 1)


def _wait_rows(dst, slot, sem):
    pltpu.make_async_copy(dst.at[slot], dst.at[slot], sem.at[slot]).wait()


def _rows_2d(buf, slot, base, n):
    return jnp.concatenate(
        [buf[slot, pl.ds(base * ROW_TILES + s, n, stride=ROW_TILES), :] for s in range(ROW_TILES)], axis=1)


def _ffn_kernel(be_ref, nb_ref, ta_ref, tb_ref, h2_hbm, gate_ref, wg_ref, bg_ref, wu_ref, bu_ref, wd_ref, bd_ref,
                o_ref, xbuf, wgb, wub, wdb, sem):
    bm = ta_ref.shape[2]
    i = pl.program_id(0)
    nb = nb_ref[0]
    slot = i % 2

    @pl.when((i == 0) | (be_ref[i] != be_ref[jnp.maximum(i - 1, 0)]))
    def _():
        wgb[...] = wg_ref[0].astype(BF16)
        wub[...] = wu_ref[0].astype(BF16)
        wdb[...] = wd_ref[0].astype(BF16)

    @pl.when(i == 0)
    def _():
        _issue_rows(ta_ref, range(bm), h2_hbm, xbuf, 0, sem, queues=1)

    @pl.when(i + 1 < nb)
    def _():
        _issue_rows(tb_ref, range(bm), h2_hbm, xbuf, 1 - slot, sem, queues=1)

    @pl.when(i < nb)
    def _():
        _wait_rows(xbuf, slot, sem)
        x = _rows_2d(xbuf, slot, 0, bm).astype(BF16)
        g = jnp.dot(x, wgb[...], preferred_element_type=F32) + bg_ref[0]
        u = jnp.dot(x, wub[...], preferred_element_type=F32) + bu_ref[0]
        g = jnp.minimum(g, SWIGLU_LIMIT)
        u = jnp.clip(u, -SWIGLU_LIMIT, SWIGLU_LIMIT)
        act = g * jax.nn.sigmoid(SWIGLU_ALPHA * g) * (u + 1.0)
        y = (jnp.dot(act.astype(BF16), wdb[...], preferred_element_type=F32) + bd_ref[0]) * gate_ref[...]
        for s in range(ROW_TILES):
            o_ref[pl.ds(s, bm, stride=ROW_TILES), :] = y[:, s * LANES:(s + 1) * LANES]

    @pl.when(i >= nb)
    def _():
        o_ref[...] = jnp.zeros(o_ref.shape, o_ref.dtype)


def _ffn(block_e, nb_used, buf_tok, h2_rows, buf_gate, wg, bg, wu, bu, wd, bd):
    NB = block_e.shape[0]
    bm = FFN_BM
    D, F = D_MODEL, D_FF
    tok3 = buf_tok.reshape(NB, 1, bm)
    wspec = lambda r, c: pl.BlockSpec((1, r, c), lambda i, be, nb: (be[i], 0, 0))
    vmem_limit = 2 * 3 * D * F * 4 + 3 * D * F * 2 + 4 * bm * D * 4 + 6 * bm * F * 4
    return pl.pallas_call(
        _ffn_kernel,
        grid_spec=pltpu.PrefetchScalarGridSpec(
            num_scalar_prefetch=2,
            grid=(NB,),
            in_specs=[pl.BlockSpec((1, 1, bm), lambda i, be, nb: (i, 0, 0), memory_space=pltpu.SMEM),
                      pl.BlockSpec((1, 1, bm), lambda i, be, nb: (jnp.minimum(i + 1, NB - 1), 0, 0),
                                   memory_space=pltpu.SMEM),
                      pl.BlockSpec(memory_space=pl.ANY),
                      pl.BlockSpec((bm, 1), lambda i, be, nb: (i, 0)),
                      wspec(D, F), wspec(1, F), wspec(D, F), wspec(1, F), wspec(F, D), wspec(1, D)],
            out_specs=pl.BlockSpec((bm * ROW_TILES, LANES), lambda i, be, nb: (i, 0)),
            scratch_shapes=[pltpu.VMEM((2, bm * ROW_TILES, LANES), F32),
                            pltpu.VMEM((D, F), BF16), pltpu.VMEM((D, F), BF16), pltpu.VMEM((F, D), BF16),
                            pltpu.SemaphoreType.DMA((2,))]),
        out_shape=jax.ShapeDtypeStruct((NB * bm * ROW_TILES, LANES), F32),
        compiler_params=pltpu.CompilerParams(dimension_semantics=("arbitrary",),
                                             vmem_limit_bytes=vmem_limit),
        name="ffn",
    )(block_e, nb_used, tok3, tok3, h2_rows, buf_gate, wg, bg, wu, bu, wd, bd)


def _combine_kernel(da_ref, db_ref, y_hbm, x1_ref, gt2_ref, fg_ref, o_ref, buf, sem):
    tm = x1_ref.shape[0]
    n = TOP_K * tm
    i = pl.program_id(0)
    slot = i % 2

    @pl.when(i == 0)
    def _():
        _issue_rows(da_ref, range(n), y_hbm, buf, 0, sem)

    @pl.when(i + 1 < pl.num_programs(0))
    def _():
        _issue_rows(db_ref, range(n), y_hbm, buf, 1 - slot, sem)

    _wait_rows(buf, slot, sem)
    y = _rows_2d(buf, slot, 0, tm)
    for k in range(1, TOP_K):
        y = y + _rows_2d(buf, slot, k * tm, tm)
    x2 = x1_ref[...] + gt2_ref[0] * y
    o_ref[...] = _rms(x2, fg_ref[...])


def _combine(dest3, y_rows, x1, gt2, fg, S):
    T, D = x1.shape
    tm = min(COMB_TM, S)
    NT = T // tm
    per_b = S // tm
    n = TOP_K * tm
    return pl.pallas_call(
        _combine_kernel,
        grid=(NT,),
        in_specs=[pl.BlockSpec((1, 1, n), lambda i: (i, 0, 0), memory_space=pltpu.SMEM),
                  pl.BlockSpec((1, 1, n), lambda i: (jnp.minimum(i + 1, NT - 1), 0, 0), memory_space=pltpu.SMEM),
                  pl.BlockSpec(memory_space=pl.ANY),
                  pl.BlockSpec((tm, D), lambda i: (i, 0)),
                  pl.BlockSpec((1, 1, D), lambda i: (i // per_b, 0, 0)),
                  pl.BlockSpec((1, D), lambda i: (0, 0))],
        out_specs=pl.BlockSpec((tm, D), lambda i: (i, 0)),
        out_shape=jax.ShapeDtypeStruct((T, D), F32),
        scratch_shapes=[pltpu.VMEM((2, n * ROW_TILES, LANES), F32), pltpu.SemaphoreType.DMA((2,))],
        compiler_params=pltpu.CompilerParams(dimension_semantics=("arbitrary",)),
        name="combine",
    )(dest3, dest3, y_rows, x1, gt2, fg)


def _rope_tables(S):
    inv = ROPE_THETA ** (-jnp.arange(0, ROT_DIM, 2, dtype=F32) / ROT_DIM)
    ang = jnp.arange(S, dtype=F32)[:, None] * inv[None, :]
    cos, sin = jnp.cos(ang), jnp.sin(ang)
    d = jnp.arange(KV_W) % HEAD_DIM
    first, second = d < ROT_HALF, (d >= ROT_HALF) & (d < ROT_DIM)
    cos_l = cos[:, d % ROT_HALF]
    sin_l = sin[:, d % ROT_HALF]
    rc = jnp.where((d < ROT_DIM)[None], cos_l, 1.0)
    rs1 = jnp.where(second[None], sin_l, 0.0)
    rs2 = jnp.where(first[None], -sin_l, 0.0)
    return rc, rs1, rs2, cos.T, sin.T


def _route_plan(route, counts, T):
    bm = FFN_BM
    A = T * TOP_K
    idx = route[0:TOP_K].astype(I32)
    gate = route[TOP_K:2 * TOP_K]
    rank = route[2 * TOP_K:3 * TOP_K].astype(I32)
    keys = idx * T + jnp.arange(T, dtype=I32)[None, :]
    skey, sgate = lax.sort((keys.reshape(A), gate.reshape(A)), num_keys=1)
    counts = counts.astype(I32)
    starts = jnp.cumsum(counts) - counts
    padded = (counts + bm - 1) // bm * bm
    pends = jnp.cumsum(padded)
    pstarts = pends - padded
    P = (A + N_EXPERTS * bm + bm - 1) // bm * bm
    NB = P // bm
    blk0 = jnp.arange(NB, dtype=I32) * bm
    block_e = jnp.minimum(jnp.sum((pends[None, :] <= blk0[:, None]).astype(I32), axis=1), N_EXPERTS - 1)
    r = (blk0 - pstarts[block_e])[:, None] + jnp.arange(bm, dtype=I32)[None, :]
    valid = r < counts[block_e][:, None]
    src = jnp.clip(starts[block_e][:, None] + r, 0, A - 1)
    buf_tok = jnp.where(valid, skey[src] - block_e[:, None] * T, 0)
    buf_gate = jnp.where(valid, sgate[src], 0.0).reshape(P, 1)
    dest = rank
    for e in range(N_EXPERTS):
        dest = dest + jnp.where(idx == e, pstarts[e], 0)
    nb_used = (pends[-1] // bm).astype(I32).reshape(1)
    return block_e, nb_used, buf_tok, buf_gate, dest


def kernel(x, c, norm1_g, norm2_g, w_ada, b_ada, w_in, conv_w, conv_b, conv_ln_g, conv_ln_b, cmp_pe_k, cmp_pe_v,
           cmp_k_w1, cmp_k_w2, cmp_v_w1, cmp_v_w2, out_norm_conv, out_norm_nsa, w_out, w_router, b_router,
           w_gate, b_gate, w_up, b_up, w_down, b_down, final_norm_g):
    B, S, D = x.shape
    T = B * S
    G = KV_GROUPS
    assert D == D_MODEL and S % ATTN_TK == 0 and S % CMP_STRIDE == 0 and KV_W == LANES
    rc, rs1, rs2, cos_t, sin_t = _rope_tables(S)
    n_sel = S // SEL_BLK
    nc = S // CMP_STRIDE
    cstart = jnp.arange(nc) * CMP_STRIDE
    jstart = jnp.arange(n_sel) * SEL_BLK
    overlap_t = ((cstart[None, :] <= jstart[:, None] + SEL_BLK - 1)
                 & (cstart[None, :] + CMP_LEN - 1 >= jstart[:, None])
                 & (jnp.arange(nc)[None, :] < nc - 1)).astype(F32)

    assert w_ada.shape[0] == 1
    for l in range(1):
        mod = _adaln(c, w_ada[l], b_ada[l][None])
        sh1, sc1, gt1, sh2, sc2, gt2 = [m[:, None, :] for m in jnp.split(mod, 6, axis=-1)]

        wl = w_in[l]
        o = 2 * CONV_CH + NSA_WIDTH
        kvc = [wl[:, o + i * KV_W:o + (i + 1) * KV_W] for i in range(6)]
        gl = wl[:, o + 6 * KV_W:]
        per_g = 3 * Q_PER_G
        gpad = [jnp.pad(gl[:, per_g * g:per_g * (g + 1)], ((0, 0), (0, GATE_ROWS - per_g))) for g in range(G)]
        wn = jnp.concatenate([wl[:, :2 * CONV_CH], kvc[0], kvc[2], kvc[4], kvc[1]], axis=1).astype(BF16)
        wt = jnp.concatenate([wl[:, 2 * CONV_CH:o], kvc[3], kvc[5]] + gpad, axis=1).T.astype(BF16)
        u, kc, vc, ks, kw, qt, vst, vwt, gates = _inproj(x, sc1, sh1, norm1_g[l][None], wn, wt,
                                                         rc, rs1, rs2, cos_t, sin_t)

        conv_n = _conv(u, conv_w[l], conv_b[l][None], conv_ln_g[l][None], conv_ln_b[l][None],
                       out_norm_conv[l][None])

        kcmp, vcmp_t = _compress(kc, vc, cmp_pe_k[l].reshape(1, -1), cmp_pe_v[l].reshape(1, -1),
                                 cmp_k_w1[l], cmp_k_w2[l], cmp_v_w1[l], cmp_v_w2[l].T)
        nsa = _attention(qt, kcmp, vcmp_t, ks, vst, kw, vwt, gates, overlap_t)

        x1, h2_rows, route, counts = _outproj(x, conv_n, nsa, out_norm_nsa[l][None], w_out[l].astype(BF16), gt1,
                                              norm2_g[l][None], sc2, sh2, w_router[l], b_router[l][None])

        block_e, nb_used, buf_tok, buf_gate, dest = _route_plan(route, counts[:, 0], T)
        y_rows = _ffn(block_e, nb_used, buf_tok, h2_rows.reshape(T, ROW_TILES, LANES), buf_gate,
                      w_gate[l], b_gate[l][:, None, :], w_up[l], b_up[l][:, None, :],
                      w_down[l], b_down[l][:, None, :])
        tm = min(COMB_TM, S)
        dest3 = dest.reshape(TOP_K, T // tm, tm).transpose(1, 0, 2).reshape(T // tm, 1, TOP_K * tm)
        P = y_rows.shape[0] // ROW_TILES
        x = _combine(dest3, y_rows.reshape(P, ROW_TILES, LANES), x1.reshape(T, D), gt2, final_norm_g[None],
                     S).reshape(B, S, D)
    return x
```

```python
import functools

import jax
import jax.numpy as jnp
from jax import lax
from jax.experimental import pallas as pl
from jax.experimental.pallas import tpu as pltpu

F32 = jnp.float32
BF16 = jnp.bfloat16
I32 = jnp.int32
HI = lax.Precision.HIGHEST

D_MODEL = 1024
CONV_CH = 512
CONV_WIDTH = 31
NSA_HEADS = 8
KV_GROUPS = 2
Q_PER_G = NSA_HEADS // KV_GROUPS
HEAD_DIM = 64
NSA_WIDTH = NSA_HEADS * HEAD_DIM
KV_W = KV_GROUPS * HEAD_DIM
ROT_DIM = HEAD_DIM // 4
ROT_HALF = ROT_DIM // 2
ROPE_THETA = 500000.0
CMP_LEN = 32
CMP_STRIDE = 16
CMP_HIDDEN = 128
SEL_BLK = 64
SEL_TOPN = 16
WINDOW = 512
N_EXPERTS = 32
TOP_K = 4
D_FF = 1024
SWIGLU_ALPHA = 1.702
SWIGLU_LIMIT = 7.0
NORM_EPS = 1e-5
NEG_INF = -1e30
FORCE_SCORE = 1e9
LOG2_E = 1.4426950408889634

LANES = 128
SUBLANES = 8
ROW_TILES = D_MODEL // LANES

GATE_ROWS = 16

INPROJ_TM = 1024
CONV_TR = 512
CONV_HALO = 32
ATTN_TQ = 512
ATTN_TK = 512
OUT_TM = 1024
FFN_BM = 512
FFN_SLOTS = 3
COMB_TM = 256
ROUTE_W = 16


def _rms(x, g):
    return x * lax.rsqrt(jnp.mean(x * x, axis=-1, keepdims=True) + NORM_EPS) * g


def _split_bf16(a):
    hi = a.astype(BF16)
    return hi, (a - hi.astype(F32)).astype(BF16)


def _dot3(a, b, dims=(((1,), (0,)), ((), ()))):
    ah, al = _split_bf16(a)
    bh, bl = _split_bf16(b)
    d = lambda x, y: lax.dot_general(x, y, dims, preferred_element_type=F32)
    return d(ah, bh) + d(al, bh) + d(ah, bl)


def _ada_kernel(c_ref, w_ref, b_ref, o_ref):
    c = c_ref[...]
    ca = c * jax.nn.sigmoid(c)
    o_ref[...] = jnp.dot(ca, w_ref[...], preferred_element_type=F32, precision=HI) + b_ref[...]


def _adaln(c, w, b):
    B = c.shape[0]
    D = D_MODEL
    return pl.pallas_call(
        _ada_kernel,
        grid=(6,),
        in_specs=[pl.BlockSpec((B, D), lambda j: (0, 0)),
                  pl.BlockSpec((D, D), lambda j: (0, j)),
                  pl.BlockSpec((1, D), lambda j: (0, j))],
        out_specs=pl.BlockSpec((B, D), lambda j: (0, j)),
        out_shape=jax.ShapeDtypeStruct((B, 6 * D), F32),
        name="adaln",
    )(c, w, b)


def _inproj_kernel(x_ref, sc_ref, sh_ref, g_ref, wn_ref, wt_ref, rc_ref, rs1_ref, rs2_ref, ct_ref, st_ref,
                   u_ref, kc_ref, vc_ref, ks_ref, kw_ref, qt_ref, vst_ref, vwt_ref, gt_ref, stage_ref):
    tm = x_ref.shape[1]
    tq, tk = ATTN_TQ, ATTN_TK
    h = (_rms(x_ref[0], g_ref[...]) * (1.0 + sc_ref[0]) + sh_ref[0]).astype(BF16)

    p = jnp.dot(h, wn_ref[...], preferred_element_type=F32)
    u_ref[0] = p[:, 0:CONV_CH] * jax.nn.sigmoid(p[:, CONV_CH:2 * CONV_CH])
    c0 = 2 * CONV_CH
    rc, rs1, rs2 = rc_ref[...], rs1_ref[...], rs2_ref[...]
    for ref, roped in ((kc_ref, True), (ks_ref, True), (kw_ref, True), (vc_ref, False)):
        v = p[:, c0:c0 + KV_W]
        if roped:
            v = v * rc + pltpu.roll(v, ROT_HALF, 1) * rs1 + pltpu.roll(v, KV_W - ROT_HALF, 1) * rs2
        c0 += KV_W
        if ref is kc_ref or ref is vc_ref:
            stage_ref[...] = v
            left = lax.broadcasted_iota(I32, (tm // CMP_STRIDE, KV_W), 1) < HEAD_DIM
            pieces = [stage_ref[pl.ds(tl, tm // CMP_STRIDE, stride=CMP_STRIDE), :] for tl in range(CMP_STRIDE)]
            for gg in range(KV_GROUPS):
                cols = []
                for tl in range(0, CMP_STRIDE, 2):
                    a, b = pieces[tl], pieces[tl + 1]
                    if gg == 0:
                        cols.append(jnp.where(left, a, pltpu.roll(b, HEAD_DIM, 1)))
                    else:
                        cols.append(jnp.where(left, pltpu.roll(a, HEAD_DIM, 1), b))
                ref[0, gg] = jnp.concatenate(cols, axis=1)
            continue
        for gg in range(KV_GROUPS):
            ref[0, gg] = v[:, HEAD_DIM * gg:HEAD_DIM * (gg + 1)].astype(ref.dtype)

    pt = lax.dot_general(wt_ref[...], h, (((1,), (1,)), ((), ())), preferred_element_type=F32)
    cos_t, sin_t = ct_ref[...], st_ref[...]
    scale = HEAD_DIM ** -0.5 * LOG2_E
    for hh in range(NSA_HEADS):
        blk = pt[HEAD_DIM * hh:HEAD_DIM * (hh + 1), :]
        x1, x2 = blk[0:ROT_HALF], blk[ROT_HALF:ROT_DIM]
        qh = (jnp.concatenate([x1 * cos_t - x2 * sin_t, x2 * cos_t + x1 * sin_t, blk[ROT_DIM:]], axis=0)
              * scale).astype(BF16)
        gg, n = divmod(hh, Q_PER_G)
        for j in range(tm // tq):
            qt_ref[0, gg, j, :, n * tq:(n + 1) * tq] = qh[:, j * tq:(j + 1) * tq]
    r0 = NSA_WIDTH
    for ref in (vst_ref, vwt_ref):
        for gg in range(KV_GROUPS):
            blk = pt[r0 + HEAD_DIM * gg:r0 + HEAD_DIM * (gg + 1), :].astype(BF16)
            for j in range(tm // tk):
                ref[0, gg, j] = blk[:, j * tk:(j + 1) * tk]
        r0 += KV_W
    for gg in range(KV_GROUPS):
        gt_ref[0, gg] = jax.nn.sigmoid(pt[r0 + GATE_ROWS * gg:r0 + GATE_ROWS * (gg + 1), :])


def _inproj(x, sc, sh, g, wn, wt, rc, rs1, rs2, cos_t, sin_t):
    B, S, D = x.shape
    tm = min(INPROJ_TM, S)
    tq, tk = ATTN_TQ, ATTN_TK
    G = KV_GROUPS
    kv = lambda dt: jax.ShapeDtypeStruct((B, G, S, HEAD_DIM), dt)
    kv_spec = pl.BlockSpec((1, G, tm, HEAD_DIM), lambda b, i: (b, 0, i, 0))
    chunk = CMP_STRIDE * HEAD_DIM
    ck_shape = jax.ShapeDtypeStruct((B, G, S // CMP_STRIDE, chunk), F32)
    ck_spec = pl.BlockSpec((1, G, tm // CMP_STRIDE, chunk), lambda b, i: (b, 0, i, 0))
    vt_shape = jax.ShapeDtypeStruct((B, G, S // tk, HEAD_DIM, tk), BF16)
    vt_spec = pl.BlockSpec((1, G, tm // tk, HEAD_DIM, tk), lambda b, i: (b, 0, i, 0, 0))
    row = pl.BlockSpec((1, 1, D), lambda b, i: (b, 0, 0))
    tab = pl.BlockSpec((tm, LANES), lambda b, i: (i, 0))
    tab_t = pl.BlockSpec((ROT_HALF, tm), lambda b, i: (0, i))
    return pl.pallas_call(
        _inproj_kernel,
        grid=(B, S // tm),
        in_specs=[pl.BlockSpec((1, tm, D), lambda b, i: (b, i, 0)), row, row,
                  pl.BlockSpec((1, D), lambda b, i: (0, 0)),
                  pl.BlockSpec(wn.shape, lambda b, i: (0, 0)),
                  pl.BlockSpec(wt.shape, lambda b, i: (0, 0)),
                  tab, tab, tab, tab_t, tab_t],
        out_specs=[pl.BlockSpec((1, tm, CONV_CH), lambda b, i: (b, i, 0)),
                   ck_spec, ck_spec, kv_spec, kv_spec,
                   pl.BlockSpec((1, G, tm // tq, HEAD_DIM, Q_PER_G * tq), lambda b, i: (b, 0, i, 0, 0)),
                   vt_spec, vt_spec,
                   pl.BlockSpec((1, G, GATE_ROWS, tm), lambda b, i: (b, 0, 0, i))],
        out_shape=[jax.ShapeDtypeStruct((B, S, CONV_CH), F32),
                   ck_shape, ck_shape, kv(BF16), kv(BF16),
                   jax.ShapeDtypeStruct((B, G, S // tq, HEAD_DIM, Q_PER_G * tq), BF16),
                   vt_shape, vt_shape,
                   jax.ShapeDtypeStruct((B, G, GATE_ROWS, S), F32)],
        scratch_shapes=[pltpu.VMEM((tm, KV_W), F32)],
        compiler_params=pltpu.CompilerParams(dimension_semantics=("parallel", "parallel")),
        name="inproj",
    )(x, sc, sh, g, wn, wt, rc, rs1, rs2, cos_t, sin_t)


def _conv_kernel(prev_ref, cur_ref, w_ref, cb_ref, lg_ref, lb_ref, on_ref, o_ref, pad_ref, win_ref):
    tr = cur_ref.shape[1]
    first = pl.program_id(1) == 0
    halo = prev_ref[0, tr - CONV_HALO:tr, :]
    pad_ref[0:CONV_HALO, :] = jnp.where(first, 0.0, halo)
    pad_ref[CONV_HALO:CONV_HALO + tr, :] = cur_ref[0]
    off = CONV_HALO - (CONV_WIDTH - 1)
    acc = jnp.zeros((tr, CONV_CH), F32)
    for b in range(SUBLANES):
        taps = range(b, CONV_WIDTH, SUBLANES)
        rows = tr + SUBLANES * (len(taps) - 1)
        win_ref[b, 0:rows, :] = pad_ref[off + b:off + b + rows, :]
        for a, k in enumerate(taps):
            acc = acc + win_ref[b, SUBLANES * a:SUBLANES * a + tr, :] * w_ref[k:k + 1, :]
    y = acc + cb_ref[...]
    mu = jnp.mean(y, axis=-1, keepdims=True)
    yc = y - mu
    var = jnp.mean(yc * yc, axis=-1, keepdims=True)
    yn = yc * lax.rsqrt(var + NORM_EPS) * lg_ref[...] + lb_ref[...]
    s = yn * jax.nn.sigmoid(yn)
    o_ref[0] = _rms(s, on_ref[...]).astype(o_ref.dtype)


def _conv(u, w, cb, lg, lb, on):
    B, S, C = u.shape
    tr = min(CONV_TR, S)
    vec = pl.BlockSpec((1, C), lambda b, i: (0, 0))
    return pl.pallas_call(
        _conv_kernel,
        grid=(B, S // tr),
        in_specs=[pl.BlockSpec((1, tr, C), lambda b, i: (b, jnp.maximum(i - 1, 0), 0)),
                  pl.BlockSpec((1, tr, C), lambda b, i: (b, i, 0)),
                  pl.BlockSpec((CONV_WIDTH, C), lambda b, i: (0, 0)),
                  vec, vec, vec, vec],
        out_specs=pl.BlockSpec((1, tr, C), lambda b, i: (b, i, 0)),
        out_shape=jax.ShapeDtypeStruct((B, S, C), BF16),
        scratch_shapes=[pltpu.VMEM((CONV_HALO + tr, C), F32),
                        pltpu.VMEM((SUBLANES, tr + SUBLANES * ((CONV_WIDTH - 1) // SUBLANES), C), F32)],
        compiler_params=pltpu.CompilerParams(dimension_semantics=("parallel", "parallel")),
        name="conv",
    )(u, u, w, cb, lg, lb, on)


def _cmp_kernel(kx_ref, vx_ref, pek_ref, pev_ref, kw1_ref, kw2_ref, vw1_ref, vw2t_ref, ko_ref, vo_ref):
    nc = kx_ref.shape[2]
    half = kx_ref.shape[3]
    nt = (((1,), (1,)), ((), ()))
    for x_ref, pe_ref, w1_ref, w2_ref, o_ref, transposed in ((kx_ref, pek_ref, kw1_ref, kw2_ref, ko_ref, False),
                                                             (vx_ref, pev_ref, vw1_ref, vw2t_ref, vo_ref, True)):
        w1 = w1_ref[...]
        pe = jnp.broadcast_to(pe_ref[...], (SUBLANES, 2 * half))
        pe_proj = _dot3(pe, w1)[0:1]
        for gg in range(KV_GROUPS):
            xg = x_ref[0, gg]
            first = _dot3(xg, w1[0:half])
            second = _dot3(xg, w1[half:2 * half])
            hid = first + pltpu.roll(second, nc - 1, 0) + pe_proj
            hid = hid * jax.nn.sigmoid(hid)
            if transposed:
                o_ref[0, gg] = _dot3(w2_ref[...], hid, nt)
            else:
                o_ref[0, gg] = _dot3(hid, w2_ref[...])


def _compress(kx, vx, pek, pev, kw1, kw2, vw1, vw2t):
    B, G, NC, W = kx.shape
    xs = pl.BlockSpec((1, G, NC, W), lambda b: (b, 0, 0, 0))
    full = lambda a: pl.BlockSpec(a.shape, lambda b: (0,) * a.ndim)
    return pl.pallas_call(
        _cmp_kernel,
        grid=(B,),
        in_specs=[xs, xs, full(pek), full(pev), full(kw1), full(kw2), full(vw1), full(vw2t)],
        out_specs=[pl.BlockSpec((1, G, NC, HEAD_DIM), lambda b: (b, 0, 0, 0)),
                   pl.BlockSpec((1, G, HEAD_DIM, NC), lambda b: (b, 0, 0, 0))],
        out_shape=[jax.ShapeDtypeStruct((B, G, NC, HEAD_DIM), F32),
                   jax.ShapeDtypeStruct((B, G, HEAD_DIM, NC), F32)],
        compiler_params=pltpu.CompilerParams(dimension_semantics=("parallel",)),
        name="compress",
    )(kx, vx, pek, pev, kw1, kw2, vw1, vw2t)


def _attn_kernel(qt_ref, kc_ref, vct_ref, ks_ref, vst_ref, kw_ref, vwt_ref, gt_ref, ovt_ref, o_ref, sel_ref):
    tq, tk = ATTN_TQ, ATTN_TK
    R = Q_PER_G * tq
    per_tile = tk // SEL_BLK
    S = ks_ref.shape[2]
    ncp = kc_ref.shape[2]
    nsel = ovt_ref.shape[0]
    qi = pl.program_id(2)
    q0 = qi * tq
    qt = qt_ref[0, 0, 0]
    t_row = q0 + lax.broadcasted_iota(I32, (1, tq), 1)
    heads = lambda a: jnp.concatenate([a] * Q_PER_G, axis=1)

    sc = jnp.dot(kc_ref[0, 0].astype(BF16), qt, preferred_element_type=F32)
    c_io = lax.broadcasted_iota(I32, (ncp, tq), 0)
    m_c = (c_io * CMP_STRIDE + (CMP_LEN - 1) <= t_row) & (c_io < ncp - 1)
    scb = sc + heads(jnp.where(m_c, 0.0, NEG_INF))
    e = jnp.exp2(scb - jnp.max(scb, axis=0, keepdims=True)) * heads(jnp.where(m_c, 1.0, 0.0))
    den = jnp.sum(e, axis=0, keepdims=True)
    pc = e / jnp.where(den > 0.0, den, 1.0)
    o_cmp = jnp.dot(vct_ref[0, 0].astype(BF16), pc.astype(BF16), preferred_element_type=F32)

    psum = pc[:, 0:tq]
    for n in range(1, Q_PER_G):
        psum = psum + pc[:, n * tq:(n + 1) * tq]
    imp = _dot3(ovt_ref[...], psum)
    j_io = lax.broadcasted_iota(I32, (nsel, tq), 0)
    cur = t_row // SEL_BLK
    valid = j_io * SEL_BLK <= t_row
    forced = (j_io == 0) | (j_io == cur) | (j_io == cur - 1)
    score = jnp.where(valid, jnp.where(forced, FORCE_SCORE, imp), NEG_INF)
    rank = jnp.zeros((nsel, tq), F32)
    for i in range(nsel):
        row = score[i:i + 1, :]
        tie = jnp.where(j_io > i, 1.0, 0.0)
        rank = rank + jnp.where(row > score, 1.0, jnp.where(row == score, tie, 0.0))
    sel_bias = jnp.where(rank < float(min(SEL_TOPN, nsel)), 0.0, NEG_INF)
    sel_ref[...] = jnp.zeros(sel_ref.shape, F32)
    for jj in range(S // tk):
        sel_ref[jj, 0:per_tile, :] = sel_bias[jj * per_tile:(jj + 1) * per_tile, :]

    k_io = lax.broadcasted_iota(I32, (tk, tq), 0)

    def flash_step(k_ref, vt_ref, kj, bias, carry):
        m, l, acc = carry
        k0 = pl.multiple_of(kj * tk, tk)
        s = jnp.dot(k_ref[0, 0, pl.ds(k0, tk), :], qt, preferred_element_type=F32) + heads(bias)
        m_new = jnp.maximum(m, jnp.max(s, axis=0, keepdims=True))
        alpha = jnp.exp2(m - m_new)
        p = jnp.exp2(s - m_new)
        l = alpha * l + jnp.sum(p, axis=0, keepdims=True)
        acc = alpha * acc + jnp.dot(vt_ref[0, 0, kj], p.astype(BF16), preferred_element_type=F32)
        return m_new, l, acc

    init = (jnp.full((1, R), NEG_INF, F32), jnp.zeros((1, R), F32), jnp.zeros((HEAD_DIM, R), F32))

    def slc_body(kj, carry):
        blocks = sel_ref[kj]
        bias = jnp.concatenate([jnp.broadcast_to(blocks[b:b + 1, :], (SEL_BLK, tq)) for b in range(per_tile)], axis=0)
        bias = jnp.where(kj * tk + k_io <= t_row, bias, NEG_INF)
        return flash_step(ks_ref, vst_ref, kj, bias, carry)

    def win_body(kj, carry):
        rel = t_row - (kj * tk + k_io)
        bias = jnp.where((rel >= 0) & (rel < WINDOW), 0.0, NEG_INF)
        return flash_step(kw_ref, vwt_ref, kj, bias, carry)

    n_slc = (q0 + tq + tk - 1) // tk
    lo_tile = jnp.maximum(q0 - (WINDOW - 1), 0) // tk
    n_pair = lo_tile // 2

    def pair_body(i, carry):
        return slc_body(2 * i, carry[0]), slc_body(2 * i + 1, carry[1])

    st_a, st_b = lax.fori_loop(0, n_pair, pair_body, (init, init))
    st_a = lax.fori_loop(2 * n_pair, lo_tile, slc_body, st_a)

    def both_body(kj, carry):
        return slc_body(kj, carry[0]), win_body(kj, carry[1])

    st_a, (_, l_w, acc_w) = lax.fori_loop(lo_tile, n_slc, both_body, (st_a, init))
    m_s = jnp.maximum(st_a[0], st_b[0])
    w_a, w_b = jnp.exp2(st_a[0] - m_s), jnp.exp2(st_b[0] - m_s)
    l_s = w_a * st_a[1] + w_b * st_b[1]
    acc_s = w_a * st_a[2] + w_b * st_b[2]


    gt = gt_ref[0, 0]
    o_slc = acc_s / l_s
    o_win = acc_w / l_w
    outs = []
    for n in range(Q_PER_G):
        cols = slice(n * tq, (n + 1) * tq)
        outs.append(gt[3 * n:3 * n + 1, :] * o_cmp[:, cols] + gt[3 * n + 1:3 * n + 2, :] * o_slc[:, cols]
                    + gt[3 * n + 2:3 * n + 3, :] * o_win[:, cols])
    o_ref[0] = jnp.concatenate(outs, axis=0).T


def _attention(qt, kc, vct, ks, vst, kw, vwt, gt, overlap_t):
    B, G, NQT, _, R = qt.shape
    S = ks.shape[2]
    tq, tk = ATTN_TQ, ATTN_TK
    ncp = kc.shape[2]
    kfull = pl.BlockSpec((1, 1, S, HEAD_DIM), lambda b, g, i: (b, g, 0, 0))
    vfull = pl.BlockSpec((1, 1, S // tk, HEAD_DIM, tk), lambda b, g, i: (b, g, 0, 0, 0))
    return pl.pallas_call(
        _attn_kernel,
        grid=(B, G, NQT),
        in_specs=[pl.BlockSpec((1, 1, 1, HEAD_DIM, R), lambda b, g, i: (b, g, i, 0, 0)),
                  pl.BlockSpec((1, 1, ncp, HEAD_DIM), lambda b, g, i: (b, g, 0, 0)),
                  pl.BlockSpec((1, 1, HEAD_DIM, ncp), lambda b, g, i: (b, g, 0, 0)),
                  kfull, vfull, kfull, vfull,
                  pl.BlockSpec((1, 1, GATE_ROWS, tq), lambda b, g, i: (b, g, 0, i)),
                  pl.BlockSpec(overlap_t.shape, lambda b, g, i: (0, 0))],
        out_specs=pl.BlockSpec((1, tq, Q_PER_G * HEAD_DIM), lambda b, g, i: (b, i, g)),
        out_shape=jax.ShapeDtypeStruct((B, S, NSA_WIDTH), F32),
        scratch_shapes=[pltpu.VMEM((S // tk, SUBLANES, tq), F32)],
        compiler_params=pltpu.CompilerParams(dimension_semantics=("parallel", "parallel", "arbitrary")),
        name="attn",
    )(qt, kc, vct, ks, vst, kw, vwt, gt, overlap_t)


def _outproj_kernel(x_ref, cv_ref, nsa_ref, on_ref, w_ref, gt1_ref, g2_ref, sc2_ref, sh2_ref, wrh_ref, wrl_ref, br_ref,
                    tri_ref, x1_ref, h2_ref, rt_ref, cnt_ref, run_ref):
    tm = x_ref.shape[1]

    @pl.when((pl.program_id(0) == 0) & (pl.program_id(1) == 0))
    def _():
        run_ref[...] = jnp.zeros(run_ref.shape, F32)

    nn = _rms(nsa_ref[0], on_ref[...]).astype(BF16)
    y = jnp.dot(jnp.concatenate([cv_ref[0], nn], axis=1), w_ref[...], preferred_element_type=F32)
    x1 = x_ref[0] + gt1_ref[0] * y
    x1_ref[0] = x1
    h2 = _rms(x1, g2_ref[...]) * (1.0 + sc2_ref[0]) + sh2_ref[0]
    for s in range(ROW_TILES):
        h2_ref[pl.ds(s, tm, stride=ROW_TILES), :] = h2[:, s * LANES:(s + 1) * LANES]
    nt = (((1,), (1,)), ((), ()))
    h_hi = h2.astype(BF16)
    h_lo = (h2 - h_hi.astype(F32)).astype(BF16)
    logits = (lax.dot_general(wrh_ref[...], h_hi, nt, preferred_element_type=F32)
              + lax.dot_general(wrh_ref[...], h_lo, nt, preferred_element_type=F32)
              + lax.dot_general(wrl_ref[...], h_hi, nt, preferred_element_type=F32)) + br_ref[...]
    eio = lax.broadcasted_iota(I32, (N_EXPERTS, tm), 0).astype(F32)
    vals, idxs = [], []
    for _ in range(TOP_K):
        m = jnp.max(logits, axis=0, keepdims=True)
        ix = jnp.min(jnp.where(logits == m, eio, float(N_EXPERTS)), axis=0, keepdims=True)
        vals.append(m)
        idxs.append(ix)
        logits = jnp.where(eio == ix, -jnp.inf, logits)
    es = [jnp.exp(v - vals[0]) for v in vals]
    den = es[0] + es[1] + es[2] + es[3]
    hot = jnp.zeros((N_EXPERTS, tm), F32)
    for r in range(TOP_K):
        hot = hot + jnp.where(eio == idxs[r], 1.0, 0.0)
    before = run_ref[...] + jnp.dot(hot.astype(BF16), tri_ref[...], preferred_element_type=F32)
    ranks = [jnp.sum(jnp.where(eio == idxs[r], before, 0.0), axis=0, keepdims=True) for r in range(TOP_K)]
    run_ref[...] = run_ref[...] + jnp.sum(hot, axis=1, keepdims=True)
    cnt_ref[...] = run_ref[...]
    rio = lax.broadcasted_iota(I32, (ROUTE_W, tm), 0)
    out = jnp.zeros((ROUTE_W, tm), F32)
    for r in range(TOP_K):
        out = jnp.where(rio == r, idxs[r], out)
        out = jnp.where(rio == TOP_K + r, es[r] / den, out)
        out = jnp.where(rio == 2 * TOP_K + r, ranks[r], out)
    rt_ref[...] = out


def _outproj(x, cv, nsa, on, w, gt1, g2, sc2, sh2, wr, br):
    B, S, D = x.shape
    tm = min(OUT_TM, S)
    nt = S // tm
    tri = (jnp.arange(tm)[:, None] < jnp.arange(tm)[None, :]).astype(BF16)
    wr_t = wr.T
    wr_hi = wr_t.astype(BF16)
    wr_lo = (wr_t - wr_hi.astype(F32)).astype(BF16)
    row = pl.BlockSpec((1, 1, D), lambda b, i: (b, 0, 0))
    vec = lambda n: pl.BlockSpec((1, n), lambda b, i: (0, 0))
    col = pl.BlockSpec((N_EXPERTS, 1), lambda b, i: (0, 0))
    wr_spec = pl.BlockSpec((N_EXPERTS, D), lambda b, i: (0, 0))
    return pl.pallas_call(
        _outproj_kernel,
        grid=(B, nt),
        in_specs=[pl.BlockSpec((1, tm, D), lambda b, i: (b, i, 0)),
                  pl.BlockSpec((1, tm, CONV_CH), lambda b, i: (b, i, 0)),
                  pl.BlockSpec((1, tm, NSA_WIDTH), lambda b, i: (b, i, 0)),
                  vec(NSA_WIDTH),
                  pl.BlockSpec((D, D), lambda b, i: (0, 0)),
                  row, vec(D), row, row,
                  wr_spec, wr_spec, col,
                  pl.BlockSpec((tm, tm), lambda b, i: (0, 0))],
        out_specs=[pl.BlockSpec((1, tm, D), lambda b, i: (b, i, 0)),
                   pl.BlockSpec((tm * ROW_TILES, LANES), lambda b, i: (b * nt + i, 0)),
                   pl.BlockSpec((ROUTE_W, tm), lambda b, i: (0, b * nt + i)),
                   col],
        out_shape=[jax.ShapeDtypeStruct((B, S, D), F32),
                   jax.ShapeDtypeStruct((B * S * ROW_TILES, LANES), F32),
                   jax.ShapeDtypeStruct((ROUTE_W, B * S), F32),
                   jax.ShapeDtypeStruct((N_EXPERTS, 1), F32)],
        scratch_shapes=[pltpu.VMEM((N_EXPERTS, 1), F32)],
        compiler_params=pltpu.CompilerParams(dimension_semantics=("arbitrary", "arbitrary")),
        name="outproj",
    )(x, cv, nsa, on, w, gt1, g2, sc2, sh2, wr_hi, wr_lo, br.reshape(N_EXPERTS, 1), tri)


def _issue_rows(idx_ref, rows, src_hbm, dst, slot, sem, queues=2):
    for r in rows:
        pltpu.make_async_copy(src_hbm.at[idx_ref[0, 0, r]],
                              dst.at[slot, pl.ds(r * ROW_TILES, ROW_TILES), :],
                              sem.at[slot]).start(priority=r % 2 if queues == 2 else 1)


def _wait_rows(dst, slot, sem):
    pltpu.make_async_copy(dst.at[slot], dst.at[slot], sem.at[slot]).wait()


def _rows_2d(buf, slot, base, n):
    return jnp.concatenate(
        [buf[slot, pl.ds(base * ROW_TILES + s, n, stride=ROW_TILES), :] for s in range(ROW_TILES)], axis=1)


def _ffn_kernel(be_ref, nb_ref, ta_ref, tb_ref, tc_ref, h2_hbm, gate_ref, wg_ref, bg_ref, wu_ref, bu_ref, wd_ref,
                bd_ref, o_ref, xbuf, wgb, wub, wdb, sem):
    bm = ta_ref.shape[2]
    i = pl.program_id(0)
    nb = nb_ref[0]
    slot = i % FFN_SLOTS

    @pl.when((i == 0) | (be_ref[i] != be_ref[jnp.maximum(i - 1, 0)]))
    def _():
        wgb[...] = wg_ref[0].astype(BF16)
        wub[...] = wu_ref[0].astype(BF16)
        wdb[...] = wd_ref[0].astype(BF16)

    @pl.when(i == 0)
    def _():
        _issue_rows(ta_ref, range(bm), h2_hbm, xbuf, 0, sem, queues=1)

    @pl.when((i == 0) & (1 < nb))
    def _():
        _issue_rows(tb_ref, range(bm), h2_hbm, xbuf, 1, sem, queues=1)

    @pl.when(i + 2 < nb)
    def _():
        _issue_rows(tc_ref, range(bm), h2_hbm, xbuf, (i + 2) % FFN_SLOTS, sem, queues=1)

    @pl.when(i < nb)
    def _():
        _wait_rows(xbuf, slot, sem)
        x = _rows_2d(xbuf, slot, 0, bm).astype(BF16)
        g = jnp.dot(x, wgb[...], preferred_element_type=F32) + bg_ref[0]
        u = jnp.dot(x, wub[...], preferred_element_type=F32) + bu_ref[0]
        g = jnp.minimum(g, SWIGLU_LIMIT)
        u = jnp.clip(u, -SWIGLU_LIMIT, SWIGLU_LIMIT)
        act = g * jax.nn.sigmoid(SWIGLU_ALPHA * g) * (u + 1.0)
        y = (jnp.dot(act.astype(BF16), wdb[...], preferred_element_type=F32) + bd_ref[0]) * gate_ref[...]
        for s in range(ROW_TILES):
            o_ref[pl.ds(s, bm, stride=ROW_TILES), :] = y[:, s * LANES:(s + 1) * LANES]

    @pl.when(i >= nb)
    def _():
        o_ref[...] = jnp.zeros(o_ref.shape, o_ref.dtype)


def _ffn(block_e, nb_used, buf_tok, h2_rows, buf_gate, wg, bg, wu, bu, wd, bd):
    NB = block_e.shape[0]
    bm = FFN_BM
    D, F = D_MODEL, D_FF
    tok3 = buf_tok.reshape(NB, 1, bm)
    wspec = lambda r, c: pl.BlockSpec((1, r, c), lambda i, be, nb: (be[i], 0, 0))
    vmem_limit = 2 * 3 * D * F * 4 + 3 * D * F * 2 + (2 + FFN_SLOTS) * bm * D * 4 + 6 * bm * F * 4
    return pl.pallas_call(
        _ffn_kernel,
        grid_spec=pltpu.PrefetchScalarGridSpec(
            num_scalar_prefetch=2,
            grid=(NB,),
            in_specs=[pl.BlockSpec((1, 1, bm), lambda i, be, nb: (i, 0, 0), memory_space=pltpu.SMEM),
                      pl.BlockSpec((1, 1, bm), lambda i, be, nb: (jnp.minimum(i + 1, NB - 1), 0, 0),
                                   memory_space=pltpu.SMEM),
                      pl.BlockSpec((1, 1, bm), lambda i, be, nb: (jnp.minimum(i + 2, NB - 1), 0, 0),
                                   memory_space=pltpu.SMEM),
                      pl.BlockSpec(memory_space=pl.ANY),
                      pl.BlockSpec((bm, 1), lambda i, be, nb: (i, 0)),
                      wspec(D, F), wspec(1, F), wspec(D, F), wspec(1, F), wspec(F, D), wspec(1, D)],
            out_specs=pl.BlockSpec((bm * ROW_TILES, LANES), lambda i, be, nb: (i, 0)),
            scratch_shapes=[pltpu.VMEM((FFN_SLOTS, bm * ROW_TILES, LANES), F32),
                            pltpu.VMEM((D, F), BF16), pltpu.VMEM((D, F), BF16), pltpu.VMEM((F, D), BF16),
                            pltpu.SemaphoreType.DMA((FFN_SLOTS,))]),
        out_shape=jax.ShapeDtypeStruct((NB * bm * ROW_TILES, LANES), F32),
        compiler_params=pltpu.CompilerParams(dimension_semantics=("arbitrary",),
                                             vmem_limit_bytes=vmem_limit),
        name="ffn",
    )(block_e, nb_used, tok3, tok3, tok3, h2_rows, buf_gate, wg, bg, wu, bu, wd, bd)


def _combine_kernel(da_ref, db_ref, y_hbm, x1_ref, gt2_ref, fg_ref, o_ref, buf, sem):
    tm = x1_ref.shape[0]
    n = TOP_K * tm
    i = pl.program_id(0)
    slot = i % 2

    @pl.when(i == 0)
    def _():
        _issue_rows(da_ref, range(n), y_hbm, buf, 0, sem)

    @pl.when(i + 1 < pl.num_programs(0))
    def _():
        _issue_rows(db_ref, range(n), y_hbm, buf, 1 - slot, sem)

    _wait_rows(buf, slot, sem)
    y = _rows_2d(buf, slot, 0, tm)
    for k in range(1, TOP_K):
        y = y + _rows_2d(buf, slot, k * tm, tm)
    x2 = x1_ref[...] + gt2_ref[0] * y
    o_ref[...] = _rms(x2, fg_ref[...])


def _combine(dest3, y_rows, x1, gt2, fg, S):
    T, D = x1.shape
    tm = min(COMB_TM, S)
    NT = T // tm
    per_b = S // tm
    n = TOP_K * tm
    return pl.pallas_call(
        _combine_kernel,
        grid=(NT,),
        in_specs=[pl.BlockSpec((1, 1, n), lambda i: (i, 0, 0), memory_space=pltpu.SMEM),
                  pl.BlockSpec((1, 1, n), lambda i: (jnp.minimum(i + 1, NT - 1), 0, 0), memory_space=pltpu.SMEM),
                  pl.BlockSpec(memory_space=pl.ANY),
                  pl.BlockSpec((tm, D), lambda i: (i, 0)),
                  pl.BlockSpec((1, 1, D), lambda i: (i // per_b, 0, 0)),
                  pl.BlockSpec((1, D), lambda i: (0, 0))],
        out_specs=pl.BlockSpec((tm, D), lambda i: (i, 0)),
        out_shape=jax.ShapeDtypeStruct((T, D), F32),
        scratch_shapes=[pltpu.VMEM((2, n * ROW_TILES, LANES), F32), pltpu.SemaphoreType.DMA((2,))],
        compiler_params=pltpu.CompilerParams(dimension_semantics=("arbitrary",)),
        name="combine",
    )(dest3, dest3, y_rows, x1, gt2, fg)


def _rope_tables(S):
    inv = ROPE_THETA ** (-jnp.arange(0, ROT_DIM, 2, dtype=F32) / ROT_DIM)
    ang = jnp.arange(S, dtype=F32)[:, None] * inv[None, :]
    cos, sin = jnp.cos(ang), jnp.sin(ang)
    d = jnp.arange(KV_W) % HEAD_DIM
    first, second = d < ROT_HALF, (d >= ROT_HALF) & (d < ROT_DIM)
    cos_l = cos[:, d % ROT_HALF]
    sin_l = sin[:, d % ROT_HALF]
    rc = jnp.where((d < ROT_DIM)[None], cos_l, 1.0)
    rs1 = jnp.where(second[None], sin_l, 0.0)
    rs2 = jnp.where(first[None], -sin_l, 0.0)
    return rc, rs1, rs2, cos.T, sin.T


def _route_plan(route, counts, T):
    bm = FFN_BM
    A = T * TOP_K
    idx = route[0:TOP_K].astype(I32)
    gate = route[TOP_K:2 * TOP_K]
    rank = route[2 * TOP_K:3 * TOP_K].astype(I32)
    keys = idx * T + jnp.arange(T, dtype=I32)[None, :]
    skey, sgate = lax.sort((keys.reshape(A), gate.reshape(A)), num_keys=1)
    counts = counts.astype(I32)
    starts = jnp.cumsum(counts) - counts
    padded = (counts + bm - 1) // bm * bm
    pends = jnp.cumsum(padded)
    pstarts = pends - padded
    P = (A + N_EXPERTS * bm + bm - 1) // bm * bm
    NB = P // bm
    blk0 = jnp.arange(NB, dtype=I32) * bm
    block_e = jnp.minimum(jnp.sum((pends[None, :] <= blk0[:, None]).astype(I32), axis=1), N_EXPERTS - 1)
    r = (blk0 - pstarts[block_e])[:, None] + jnp.arange(bm, dtype=I32)[None, :]
    valid = r < counts[block_e][:, None]
    src = jnp.clip(starts[block_e][:, None] + r, 0, A - 1)
    buf_tok = jnp.where(valid, skey[src] - block_e[:, None] * T, 0)
    buf_gate = jnp.where(valid, sgate[src], 0.0).reshape(P, 1)
    dest = rank
    for e in range(N_EXPERTS):
        dest = dest + jnp.where(idx == e, pstarts[e], 0)
    nb_used = (pends[-1] // bm).astype(I32).reshape(1)
    return block_e, nb_used, buf_tok, buf_gate, dest


def kernel(x, c, norm1_g, norm2_g, w_ada, b_ada, w_in, conv_w, conv_b, conv_ln_g, conv_ln_b, cmp_pe_k, cmp_pe_v,
           cmp_k_w1, cmp_k_w2, cmp_v_w1, cmp_v_w2, out_norm_conv, out_norm_nsa, w_out, w_router, b_router,
           w_gate, b_gate, w_up, b_up, w_down, b_down, final_norm_g):
    B, S, D = x.shape
    T = B * S
    G = KV_GROUPS
    assert D == D_MODEL and S % ATTN_TK == 0 and S % CMP_STRIDE == 0 and KV_W == LANES
    rc, rs1, rs2, cos_t, sin_t = _rope_tables(S)
    n_sel = S // SEL_BLK
    nc = S // CMP_STRIDE
    cstart = jnp.arange(nc) * CMP_STRIDE
    jstart = jnp.arange(n_sel) * SEL_BLK
    overlap_t = ((cstart[None, :] <= jstart[:, None] + SEL_BLK - 1)
                 & (cstart[None, :] + CMP_LEN - 1 >= jstart[:, None])
                 & (jnp.arange(nc)[None, :] < nc - 1)).astype(F32)

    assert w_ada.shape[0] == 1
    for l in range(1):
        mod = _adaln(c, w_ada[l], b_ada[l][None])
        sh1, sc1, gt1, sh2, sc2, gt2 = [m[:, None, :] for m in jnp.split(mod, 6, axis=-1)]

        wl = w_in[l]
        o = 2 * CONV_CH + NSA_WIDTH
        kvc = [wl[:, o + i * KV_W:o + (i + 1) * KV_W] for i in range(6)]
        gl = wl[:, o + 6 * KV_W:]
        per_g = 3 * Q_PER_G
        gpad = [jnp.pad(gl[:, per_g * g:per_g * (g + 1)], ((0, 0), (0, GATE_ROWS - per_g))) for g in range(G)]
        wn = jnp.concatenate([wl[:, :2 * CONV_CH], kvc[0], kvc[2], kvc[4], kvc[1]], axis=1).astype(BF16)
        wt = jnp.concatenate([wl[:, 2 * CONV_CH:o], kvc[3], kvc[5]] + gpad, axis=1).T.astype(BF16)
        u, kc, vc, ks, kw, qt, vst, vwt, gates = _inproj(x, sc1, sh1, norm1_g[l][None], wn, wt,
                                                         rc, rs1, rs2, cos_t, sin_t)

        conv_n = _conv(u, conv_w[l], conv_b[l][None], conv_ln_g[l][None], conv_ln_b[l][None],
                       out_norm_conv[l][None])

        kcmp, vcmp_t = _compress(kc, vc, cmp_pe_k[l].reshape(1, -1), cmp_pe_v[l].reshape(1, -1),
                                 cmp_k_w1[l], cmp_k_w2[l], cmp_v_w1[l], cmp_v_w2[l].T)
        nsa = _attention(qt, kcmp, vcmp_t, ks, vst, kw, vwt, gates, overlap_t)

        x1, h2_rows, route, counts = _outproj(x, conv_n, nsa, out_norm_nsa[l][None], w_out[l].astype(BF16), gt1,
                                              norm2_g[l][None], sc2, sh2, w_router[l], b_router[l][None])

        block_e, nb_used, buf_tok, buf_gate, dest = _route_plan(route, counts[:, 0], T)
        y_rows = _ffn(block_e, nb_used, buf_tok, h2_rows.reshape(T, ROW_TILES, LANES), buf_gate,
                      w_gate[l], b_gate[l][:, None, :], w_up[l], b_up[l][:, None, :],
                      w_down[l], b_down[l][:, None, :])
        tm = min(COMB_TM, S)
        dest3 = dest.reshape(TOP_K, T // tm, tm).transpose(1, 0, 2).reshape(T // tm, 1, TOP_K * tm)
        P = y_rows.shape[0] // ROW_TILES
        x = _combine(dest3, y_rows.reshape(P, ROW_TILES, LANES), x1.reshape(T, D), gt2, final_norm_g[None],
                     S).reshape(B, S, D)
    return x
```

```python
import functools

import jax
import jax.numpy as jnp
from jax import lax
from jax.experimental import pallas as pl
from jax.experimental.pallas import tpu as pltpu

F32 = jnp.float32
BF16 = jnp.bfloat16
I32 = jnp.int32
HI = lax.Precision.HIGHEST

D_MODEL = 1024
CONV_CH = 512
CONV_WIDTH = 31
NSA_HEADS = 8
KV_GROUPS = 2
Q_PER_G = NSA_HEADS // KV_GROUPS
HEAD_DIM = 64
NSA_WIDTH = NSA_HEADS * HEAD_DIM
KV_W = KV_GROUPS * HEAD_DIM
ROT_DIM = HEAD_DIM // 4
ROT_HALF = ROT_DIM // 2
ROPE_THETA = 500000.0
CMP_LEN = 32
CMP_STRIDE = 16
CMP_HIDDEN = 128
SEL_BLK = 64
SEL_TOPN = 16
WINDOW = 512
N_EXPERTS = 32
TOP_K = 4
D_FF = 1024
SWIGLU_ALPHA = 1.702
SWIGLU_LIMIT = 7.0
NORM_EPS = 1e-5
NEG_INF = -1e30
FORCE_SCORE = 1e9
LOG2_E = 1.4426950408889634

LANES = 128
SUBLANES = 8
ROW_TILES = D_MODEL // LANES

GATE_ROWS = 16

INPROJ_TM = 1024
CONV_TR = 512
CONV_HALO = 32
ATTN_TQ = 512
ATTN_TK = 512
OUT_TM = 1024
FFN_BM = 512
FFN_SLOTS = 4
COMB_TM = 256
ROUTE_W = 16


def _rms(x, g):
    return x * lax.rsqrt(jnp.mean(x * x, axis=-1, keepdims=True) + NORM_EPS) * g


def _split_bf16(a):
    hi = a.astype(BF16)
    return hi, (a - hi.astype(F32)).astype(BF16)


def _dot3(a, b, dims=(((1,), (0,)), ((), ()))):
    ah, al = _split_bf16(a)
    bh, bl = _split_bf16(b)
    d = lambda x, y: lax.dot_general(x, y, dims, preferred_element_type=F32)
    return d(ah, bh) + d(al, bh) + d(ah, bl)


def _ada_kernel(c_ref, w_ref, b_ref, o_ref):
    c = c_ref[...]
    ca = c * jax.nn.sigmoid(c)
    o_ref[...] = jnp.dot(ca, w_ref[...], preferred_element_type=F32, precision=HI) + b_ref[...]


def _adaln(c, w, b):
    B = c.shape[0]
    D = D_MODEL
    return pl.pallas_call(
        _ada_kernel,
        grid=(6,),
        in_specs=[pl.BlockSpec((B, D), lambda j: (0, 0)),
                  pl.BlockSpec((D, D), lambda j: (0, j)),
                  pl.BlockSpec((1, D), lambda j: (0, j))],
        out_specs=pl.BlockSpec((B, D), lambda j: (0, j)),
        out_shape=jax.ShapeDtypeStruct((B, 6 * D), F32),
        name="adaln",
    )(c, w, b)


def _inproj_kernel(x_ref, sc_ref, sh_ref, g_ref, wn_ref, wt_ref, rc_ref, rs1_ref, rs2_ref, ct_ref, st_ref,
                   u_ref, kc_ref, vc_ref, ks_ref, kw_ref, qt_ref, vst_ref, vwt_ref, gt_ref, stage_ref):
    tm = x_ref.shape[1]
    tq, tk = ATTN_TQ, ATTN_TK
    h = (_rms(x_ref[0], g_ref[...]) * (1.0 + sc_ref[0]) + sh_ref[0]).astype(BF16)

    p = jnp.dot(h, wn_ref[...], preferred_element_type=F32)
    u_ref[0] = p[:, 0:CONV_CH] * jax.nn.sigmoid(p[:, CONV_CH:2 * CONV_CH])
    c0 = 2 * CONV_CH
    rc, rs1, rs2 = rc_ref[...], rs1_ref[...], rs2_ref[...]
    for ref, roped in ((kc_ref, True), (ks_ref, True), (kw_ref, True), (vc_ref, False)):
        v = p[:, c0:c0 + KV_W]
        if roped:
            v = v * rc + pltpu.roll(v, ROT_HALF, 1) * rs1 + pltpu.roll(v, KV_W - ROT_HALF, 1) * rs2
        c0 += KV_W
        if ref is kc_ref or ref is vc_ref:
            stage_ref[...] = v
            left = lax.broadcasted_iota(I32, (tm // CMP_STRIDE, KV_W), 1) < HEAD_DIM
            pieces = [stage_ref[pl.ds(tl, tm // CMP_STRIDE, stride=CMP_STRIDE), :] for tl in range(CMP_STRIDE)]
            for gg in range(KV_GROUPS):
                cols = []
                for tl in range(0, CMP_STRIDE, 2):
                    a, b = pieces[tl], pieces[tl + 1]
                    if gg == 0:
                        cols.append(jnp.where(left, a, pltpu.roll(b, HEAD_DIM, 1)))
                    else:
                        cols.append(jnp.where(left, pltpu.roll(a, HEAD_DIM, 1), b))
                ref[0, gg] = jnp.concatenate(cols, axis=1)
            continue
        for gg in range(KV_GROUPS):
            ref[0, gg] = v[:, HEAD_DIM * gg:HEAD_DIM * (gg + 1)].astype(ref.dtype)

    pt = lax.dot_general(wt_ref[...], h, (((1,), (1,)), ((), ())), preferred_element_type=F32)
    cos_t, sin_t = ct_ref[...], st_ref[...]
    scale = HEAD_DIM ** -0.5 * LOG2_E
    for hh in range(NSA_HEADS):
        blk = pt[HEAD_DIM * hh:HEAD_DIM * (hh + 1), :]
        x1, x2 = blk[0:ROT_HALF], blk[ROT_HALF:ROT_DIM]
        qh = (jnp.concatenate([x1 * cos_t - x2 * sin_t, x2 * cos_t + x1 * sin_t, blk[ROT_DIM:]], axis=0)
              * scale).astype(BF16)
        gg, n = divmod(hh, Q_PER_G)
        for j in range(tm // tq):
            qt_ref[0, gg, j, :, n * tq:(n + 1) * tq] = qh[:, j * tq:(j + 1) * tq]
    r0 = NSA_WIDTH
    for ref in (vst_ref, vwt_ref):
        for gg in range(KV_GROUPS):
            blk = pt[r0 + HEAD_DIM * gg:r0 + HEAD_DIM * (gg + 1), :].astype(BF16)
            for j in range(tm // tk):
                ref[0, gg, j] = blk[:, j * tk:(j + 1) * tk]
        r0 += KV_W
    for gg in range(KV_GROUPS):
        gt_ref[0, gg] = jax.nn.sigmoid(pt[r0 + GATE_ROWS * gg:r0 + GATE_ROWS * (gg + 1), :])


def _inproj(x, sc, sh, g, wn, wt, rc, rs1, rs2, cos_t, sin_t):
    B, S, D = x.shape
    tm = min(INPROJ_TM, S)
    tq, tk = ATTN_TQ, ATTN_TK
    G = KV_GROUPS
    kv = lambda dt: jax.ShapeDtypeStruct((B, G, S, HEAD_DIM), dt)
    kv_spec = pl.BlockSpec((1, G, tm, HEAD_DIM), lambda b, i: (b, 0, i, 0))
    chunk = CMP_STRIDE * HEAD_DIM
    ck_shape = jax.ShapeDtypeStruct((B, G, S // CMP_STRIDE, chunk), F32)
    ck_spec = pl.BlockSpec((1, G, tm // CMP_STRIDE, chunk), lambda b, i: (b, 0, i, 0))
    vt_shape = jax.ShapeDtypeStruct((B, G, S // tk, HEAD_DIM, tk), BF16)
    vt_spec = pl.BlockSpec((1, G, tm // tk, HEAD_DIM, tk), lambda b, i: (b, 0, i, 0, 0))
    row = pl.BlockSpec((1, 1, D), lambda b, i: (b, 0, 0))
    tab = pl.BlockSpec((tm, LANES), lambda b, i: (i, 0))
    tab_t = pl.BlockSpec((ROT_HALF, tm), lambda b, i: (0, i))
    return pl.pallas_call(
        _inproj_kernel,
        grid=(B, S // tm),
        in_specs=[pl.BlockSpec((1, tm, D), lambda b, i: (b, i, 0)), row, row,
                  pl.BlockSpec((1, D), lambda b, i: (0, 0)),
                  pl.BlockSpec(wn.shape, lambda b, i: (0, 0)),
                  pl.BlockSpec(wt.shape, lambda b, i: (0, 0)),
                  tab, tab, tab, tab_t, tab_t],
        out_specs=[pl.BlockSpec((1, tm, CONV_CH), lambda b, i: (b, i, 0)),
                   ck_spec, ck_spec, kv_spec, kv_spec,
                   pl.BlockSpec((1, G, tm // tq, HEAD_DIM, Q_PER_G * tq), lambda b, i: (b, 0, i, 0, 0)),
                   vt_spec, vt_spec,
                   pl.BlockSpec((1, G, GATE_ROWS, tm), lambda b, i: (b, 0, 0, i))],
        out_shape=[jax.ShapeDtypeStruct((B, S, CONV_CH), F32),
                   ck_shape, ck_shape, kv(BF16), kv(BF16),
                   jax.ShapeDtypeStruct((B, G, S // tq, HEAD_DIM, Q_PER_G * tq), BF16),
                   vt_shape, vt_shape,
                   jax.ShapeDtypeStruct((B, G, GATE_ROWS, S), F32)],
        scratch_shapes=[pltpu.VMEM((tm, KV_W), F32)],
        compiler_params=pltpu.CompilerParams(dimension_semantics=("parallel", "parallel")),
        name="inproj",
    )(x, sc, sh, g, wn, wt, rc, rs1, rs2, cos_t, sin_t)


def _conv_kernel(prev_ref, cur_ref, w_ref, cb_ref, lg_ref, lb_ref, on_ref, o_ref, pad_ref, win_ref):
    tr = cur_ref.shape[1]
    first = pl.program_id(1) == 0
    halo = prev_ref[0, tr - CONV_HALO:tr, :]
    pad_ref[0:CONV_HALO, :] = jnp.where(first, 0.0, halo)
    pad_ref[CONV_HALO:CONV_HALO + tr, :] = cur_ref[0]
    off = CONV_HALO - (CONV_WIDTH - 1)
    acc = jnp.zeros((tr, CONV_CH), F32)
    for b in range(SUBLANES):
        taps = range(b, CONV_WIDTH, SUBLANES)
        rows = tr + SUBLANES * (len(taps) - 1)
        win_ref[b, 0:rows, :] = pad_ref[off + b:off + b + rows, :]
        for a, k in enumerate(taps):
            acc = acc + win_ref[b, SUBLANES * a:SUBLANES * a + tr, :] * w_ref[k:k + 1, :]
    y = acc + cb_ref[...]
    mu = jnp.mean(y, axis=-1, keepdims=True)
    yc = y - mu
    var = jnp.mean(yc * yc, axis=-1, keepdims=True)
    yn = yc * lax.rsqrt(var + NORM_EPS) * lg_ref[...] + lb_ref[...]
    s = yn * jax.nn.sigmoid(yn)
    o_ref[0] = _rms(s, on_ref[...]).astype(o_ref.dtype)


def _conv(u, w, cb, lg, lb, on):
    B, S, C = u.shape
    tr = min(CONV_TR, S)
    vec = pl.BlockSpec((1, C), lambda b, i: (0, 0))
    return pl.pallas_call(
        _conv_kernel,
        grid=(B, S // tr),
        in_specs=[pl.BlockSpec((1, tr, C), lambda b, i: (b, jnp.maximum(i - 1, 0), 0)),
                  pl.BlockSpec((1, tr, C), lambda b, i: (b, i, 0)),
                  pl.BlockSpec((CONV_WIDTH, C), lambda b, i: (0, 0)),
                  vec, vec, vec, vec],
        out_specs=pl.BlockSpec((1, tr, C), lambda b, i: (b, i, 0)),
        out_shape=jax.ShapeDtypeStruct((B, S, C), BF16),
        scratch_shapes=[pltpu.VMEM((CONV_HALO + tr, C), F32),
                        pltpu.VMEM((SUBLANES, tr + SUBLANES * ((CONV_WIDTH - 1) // SUBLANES), C), F32)],
        compiler_params=pltpu.CompilerParams(dimension_semantics=("parallel", "parallel")),
        name="conv",
    )(u, u, w, cb, lg, lb, on)


def _cmp_kernel(kx_ref, vx_ref, pek_ref, pev_ref, kw1_ref, kw2_ref, vw1_ref, vw2t_ref, ko_ref, vo_ref):
    nc = kx_ref.shape[2]
    half = kx_ref.shape[3]
    nt = (((1,), (1,)), ((), ()))
    for x_ref, pe_ref, w1_ref, w2_ref, o_ref, transposed in ((kx_ref, pek_ref, kw1_ref, kw2_ref, ko_ref, False),
                                                             (vx_ref, pev_ref, vw1_ref, vw2t_ref, vo_ref, True)):
        w1 = w1_ref[...]
        pe = jnp.broadcast_to(pe_ref[...], (SUBLANES, 2 * half))
        pe_proj = _dot3(pe, w1)[0:1]
        for gg in range(KV_GROUPS):
            xg = x_ref[0, gg]
            first = _dot3(xg, w1[0:half])
            second = _dot3(xg, w1[half:2 * half])
            hid = first + pltpu.roll(second, nc - 1, 0) + pe_proj
            hid = hid * jax.nn.sigmoid(hid)
            if transposed:
                o_ref[0, gg] = _dot3(w2_ref[...], hid, nt)
            else:
                o_ref[0, gg] = _dot3(hid, w2_ref[...])


def _compress(kx, vx, pek, pev, kw1, kw2, vw1, vw2t):
    B, G, NC, W = kx.shape
    xs = pl.BlockSpec((1, G, NC, W), lambda b: (b, 0, 0, 0))
    full = lambda a: pl.BlockSpec(a.shape, lambda b: (0,) * a.ndim)
    return pl.pallas_call(
        _cmp_kernel,
        grid=(B,),
        in_specs=[xs, xs, full(pek), full(pev), full(kw1), full(kw2), full(vw1), full(vw2t)],
        out_specs=[pl.BlockSpec((1, G, NC, HEAD_DIM), lambda b: (b, 0, 0, 0)),
                   pl.BlockSpec((1, G, HEAD_DIM, NC), lambda b: (b, 0, 0, 0))],
        out_shape=[jax.ShapeDtypeStruct((B, G, NC, HEAD_DIM), F32),
                   jax.ShapeDtypeStruct((B, G, HEAD_DIM, NC), F32)],
        compiler_params=pltpu.CompilerParams(dimension_semantics=("parallel",)),
        name="compress",
    )(kx, vx, pek, pev, kw1, kw2, vw1, vw2t)


def _attn_kernel(qt_ref, kc_ref, vct_ref, ks_ref, vst_ref, kw_ref, vwt_ref, gt_ref, ovt_ref, o_ref, sel_ref):
    tq, tk = ATTN_TQ, ATTN_TK
    R = Q_PER_G * tq
    per_tile = tk // SEL_BLK
    S = ks_ref.shape[2]
    ncp = kc_ref.shape[2]
    nsel = ovt_ref.shape[0]
    qi = pl.program_id(2)
    q0 = qi * tq
    qt = qt_ref[0, 0, 0]
    t_row = q0 + lax.broadcasted_iota(I32, (1, tq), 1)
    heads = lambda a: jnp.concatenate([a] * Q_PER_G, axis=1)

    sc = jnp.dot(kc_ref[0, 0].astype(BF16), qt, preferred_element_type=F32)
    c_io = lax.broadcasted_iota(I32, (ncp, tq), 0)
    m_c = (c_io * CMP_STRIDE + (CMP_LEN - 1) <= t_row) & (c_io < ncp - 1)
    scb = sc + heads(jnp.where(m_c, 0.0, NEG_INF))
    e = jnp.exp2(scb - jnp.max(scb, axis=0, keepdims=True)) * heads(jnp.where(m_c, 1.0, 0.0))
    den = jnp.sum(e, axis=0, keepdims=True)
    pc = e / jnp.where(den > 0.0, den, 1.0)
    o_cmp = jnp.dot(vct_ref[0, 0].astype(BF16), pc.astype(BF16), preferred_element_type=F32)

    psum = pc[:, 0:tq]
    for n in range(1, Q_PER_G):
        psum = psum + pc[:, n * tq:(n + 1) * tq]
    imp = _dot3(ovt_ref[...], psum)
    j_io = lax.broadcasted_iota(I32, (nsel, tq), 0)
    cur = t_row // SEL_BLK
    valid = j_io * SEL_BLK <= t_row
    forced = (j_io == 0) | (j_io == cur) | (j_io == cur - 1)
    score = jnp.where(valid, jnp.where(forced, FORCE_SCORE, imp), NEG_INF)
    rank = jnp.zeros((nsel, tq), F32)
    for i in range(nsel):
        row = score[i:i + 1, :]
        tie = jnp.where(j_io > i, 1.0, 0.0)
        rank = rank + jnp.where(row > score, 1.0, jnp.where(row == score, tie, 0.0))
    sel_bias = jnp.where(rank < float(min(SEL_TOPN, nsel)), 0.0, NEG_INF)
    sel_ref[...] = jnp.zeros(sel_ref.shape, F32)
    for jj in range(S // tk):
        sel_ref[jj, 0:per_tile, :] = sel_bias[jj * per_tile:(jj + 1) * per_tile, :]

    k_io = lax.broadcasted_iota(I32, (tk, tq), 0)

    def flash_step(k_ref, vt_ref, kj, bias, carry):
        m, l, acc = carry
        k0 = pl.multiple_of(kj * tk, tk)
        s = jnp.dot(k_ref[0, 0, pl.ds(k0, tk), :], qt, preferred_element_type=F32) + heads(bias)
        m_new = jnp.maximum(m, jnp.max(s, axis=0, keepdims=True))
        alpha = jnp.exp2(m - m_new)
        p = jnp.exp2(s - m_new)
        l = alpha * l + jnp.sum(p, axis=0, keepdims=True)
        acc = alpha * acc + jnp.dot(vt_ref[0, 0, kj], p.astype(BF16), preferred_element_type=F32)
        return m_new, l, acc

    init = (jnp.full((1, R), NEG_INF, F32), jnp.zeros((1, R), F32), jnp.zeros((HEAD_DIM, R), F32))

    def slc_body(kj, carry):
        blocks = sel_ref[kj]
        bias = jnp.concatenate([jnp.broadcast_to(blocks[b:b + 1, :], (SEL_BLK, tq)) for b in range(per_tile)], axis=0)
        bias = jnp.where(kj * tk + k_io <= t_row, bias, NEG_INF)
        return flash_step(ks_ref, vst_ref, kj, bias, carry)

    def win_body(kj, carry):
        rel = t_row - (kj * tk + k_io)
        bias = jnp.where((rel >= 0) & (rel < WINDOW), 0.0, NEG_INF)
        return flash_step(kw_ref, vwt_ref, kj, bias, carry)

    n_slc = (q0 + tq + tk - 1) // tk
    lo_tile = jnp.maximum(q0 - (WINDOW - 1), 0) // tk
    n_pair = lo_tile // 2

    def pair_body(i, carry):
        return slc_body(2 * i, carry[0]), slc_body(2 * i + 1, carry[1])

    st_a, st_b = lax.fori_loop(0, n_pair, pair_body, (init, init))
    st_a = lax.fori_loop(2 * n_pair, lo_tile, slc_body, st_a)

    def both_body(kj, carry):
        return slc_body(kj, carry[0]), win_body(kj, carry[1])

    st_a, (_, l_w, acc_w) = lax.fori_loop(lo_tile, n_slc, both_body, (st_a, init))
    m_s = jnp.maximum(st_a[0], st_b[0])
    w_a, w_b = jnp.exp2(st_a[0] - m_s), jnp.exp2(st_b[0] - m_s)
    l_s = w_a * st_a[1] + w_b * st_b[1]
    acc_s = w_a * st_a[2] + w_b * st_b[2]


    gt = gt_ref[0, 0]
    o_slc = acc_s / l_s
    o_win = acc_w / l_w
    outs = []
    for n in range(Q_PER_G):
        cols = slice(n * tq, (n + 1) * tq)
        outs.append(gt[3 * n:3 * n + 1, :] * o_cmp[:, cols] + gt[3 * n + 1:3 * n + 2, :] * o_slc[:, cols]
                    + gt[3 * n + 2:3 * n + 3, :] * o_win[:, cols])
    o_ref[0] = jnp.concatenate(outs, axis=0).T


def _attention(qt, kc, vct, ks, vst, kw, vwt, gt, overlap_t):
    B, G, NQT, _, R = qt.shape
    S = ks.shape[2]
    tq, tk = ATTN_TQ, ATTN_TK
    ncp = kc.shape[2]
    kfull = pl.BlockSpec((1, 1, S, HEAD_DIM), lambda b, g, i: (b, g, 0, 0))
    vfull = pl.BlockSpec((1, 1, S // tk, HEAD_DIM, tk), lambda b, g, i: (b, g, 0, 0, 0))
    return pl.pallas_call(
        _attn_kernel,
        grid=(B, G, NQT),
        in_specs=[pl.BlockSpec((1, 1, 1, HEAD_DIM, R), lambda b, g, i: (b, g, i, 0, 0)),
                  pl.BlockSpec((1, 1, ncp, HEAD_DIM), lambda b, g, i: (b, g, 0, 0)),
                  pl.BlockSpec((1, 1, HEAD_DIM, ncp), lambda b, g, i: (b, g, 0, 0)),
                  kfull, vfull, kfull, vfull,
                  pl.BlockSpec((1, 1, GATE_ROWS, tq), lambda b, g, i: (b, g, 0, i)),
                  pl.BlockSpec(overlap_t.shape, lambda b, g, i: (0, 0))],
        out_specs=pl.BlockSpec((1, tq, Q_PER_G * HEAD_DIM), lambda b, g, i: (b, i, g)),
        out_shape=jax.ShapeDtypeStruct((B, S, NSA_WIDTH), F32),
        scratch_shapes=[pltpu.VMEM((S // tk, SUBLANES, tq), F32)],
        compiler_params=pltpu.CompilerParams(dimension_semantics=("parallel", "parallel", "arbitrary")),
        name="attn",
    )(qt, kc, vct, ks, vst, kw, vwt, gt, overlap_t)


def _outproj_kernel(x_ref, cv_ref, nsa_ref, on_ref, w_ref, gt1_ref, g2_ref, sc2_ref, sh2_ref, wrh_ref, wrl_ref, br_ref,
                    tri_ref, x1_ref, h2_ref, rt_ref, cnt_ref, run_ref):
    tm = x_ref.shape[1]

    @pl.when((pl.program_id(0) == 0) & (pl.program_id(1) == 0))
    def _():
        run_ref[...] = jnp.zeros(run_ref.shape, F32)

    nn = _rms(nsa_ref[0], on_ref[...]).astype(BF16)
    y = jnp.dot(jnp.concatenate([cv_ref[0], nn], axis=1), w_ref[...], preferred_element_type=F32)
    x1 = x_ref[0] + gt1_ref[0] * y
    x1_ref[0] = x1
    h2 = _rms(x1, g2_ref[...]) * (1.0 + sc2_ref[0]) + sh2_ref[0]
    for s in range(ROW_TILES):
        h2_ref[pl.ds(s, tm, stride=ROW_TILES), :] = h2[:, s * LANES:(s + 1) * LANES]
    nt = (((1,), (1,)), ((), ()))
    h_hi = h2.astype(BF16)
    h_lo = (h2 - h_hi.astype(F32)).astype(BF16)
    logits = (lax.dot_general(wrh_ref[...], h_hi, nt, preferred_element_type=F32)
              + lax.dot_general(wrh_ref[...], h_lo, nt, preferred_element_type=F32)
              + lax.dot_general(wrl_ref[...], h_hi, nt, preferred_element_type=F32)) + br_ref[...]
    eio = lax.broadcasted_iota(I32, (N_EXPERTS, tm), 0).astype(F32)
    vals, idxs = [], []
    for _ in range(TOP_K):
        m = jnp.max(logits, axis=0, keepdims=True)
        ix = jnp.min(jnp.where(logits == m, eio, float(N_EXPERTS)), axis=0, keepdims=True)
        vals.append(m)
        idxs.append(ix)
        logits = jnp.where(eio == ix, -jnp.inf, logits)
    es = [jnp.exp(v - vals[0]) for v in vals]
    den = es[0] + es[1] + es[2] + es[3]
    hot = jnp.zeros((N_EXPERTS, tm), F32)
    for r in range(TOP_K):
        hot = hot + jnp.where(eio == idxs[r], 1.0, 0.0)
    before = run_ref[...] + jnp.dot(hot.astype(BF16), tri_ref[...], preferred_element_type=F32)
    ranks = [jnp.sum(jnp.where(eio == idxs[r], before, 0.0), axis=0, keepdims=True) for r in range(TOP_K)]
    run_ref[...] = run_ref[...] + jnp.sum(hot, axis=1, keepdims=True)
    cnt_ref[...] = run_ref[...]
    rio = lax.broadcasted_iota(I32, (ROUTE_W, tm), 0)
    out = jnp.zeros((ROUTE_W, tm), F32)
    for r in range(TOP_K):
        out = jnp.where(rio == r, idxs[r], out)
        out = jnp.where(rio == TOP_K + r, es[r] / den, out)
        out = jnp.where(rio == 2 * TOP_K + r, ranks[r], out)
    rt_ref[...] = out


def _outproj(x, cv, nsa, on, w, gt1, g2, sc2, sh2, wr, br):
    B, S, D = x.shape
    tm = min(OUT_TM, S)
    nt = S // tm
    tri = (jnp.arange(tm)[:, None] < jnp.arange(tm)[None, :]).astype(BF16)
    wr_t = wr.T
    wr_hi = wr_t.astype(BF16)
    wr_lo = (wr_t - wr_hi.astype(F32)).astype(BF16)
    row = pl.BlockSpec((1, 1, D), lambda b, i: (b, 0, 0))
    vec = lambda n: pl.BlockSpec((1, n), lambda b, i: (0, 0))
    col = pl.BlockSpec((N_EXPERTS, 1), lambda b, i: (0, 0))
    wr_spec = pl.BlockSpec((N_EXPERTS, D), lambda b, i: (0, 0))
    return pl.pallas_call(
        _outproj_kernel,
        grid=(B, nt),
        in_specs=[pl.BlockSpec((1, tm, D), lambda b, i: (b, i, 0)),
                  pl.BlockSpec((1, tm, CONV_CH), lambda b, i: (b, i, 0)),
                  pl.BlockSpec((1, tm, NSA_WIDTH), lambda b, i: (b, i, 0)),
                  vec(NSA_WIDTH),
                  pl.BlockSpec((D, D), lambda b, i: (0, 0)),
                  row, vec(D), row, row,
                  wr_spec, wr_spec, col,
                  pl.BlockSpec((tm, tm), lambda b, i: (0, 0))],
        out_specs=[pl.BlockSpec((1, tm, D), lambda b, i: (b, i, 0)),
                   pl.BlockSpec((tm * ROW_TILES, LANES), lambda b, i: (b * nt + i, 0)),
                   pl.BlockSpec((ROUTE_W, tm), lambda b, i: (0, b * nt + i)),
                   col],
        out_shape=[jax.ShapeDtypeStruct((B, S, D), F32),
                   jax.ShapeDtypeStruct((B * S * ROW_TILES, LANES), F32),
                   jax.ShapeDtypeStruct((ROUTE_W, B * S), F32),
                   jax.ShapeDtypeStruct((N_EXPERTS, 1), F32)],
        scratch_shapes=[pltpu.VMEM((N_EXPERTS, 1), F32)],
        compiler_params=pltpu.CompilerParams(dimension_semantics=("arbitrary", "arbitrary")),
        name="outproj",
    )(x, cv, nsa, on, w, gt1, g2, sc2, sh2, wr_hi, wr_lo, br.reshape(N_EXPERTS, 1), tri)


def _issue_rows(idx_ref, rows, src_hbm, dst, slot, sem, queues=2):
    for r in rows:
        pltpu.make_async_copy(src_hbm.at[idx_ref[0, 0, r]],
                              dst.at[slot, pl.ds(r * ROW_TILES, ROW_TILES), :],
                              sem.at[slot]).start(priority=r % 2 if queues == 2 else 1)


def _wait_rows(dst, slot, sem):
    pltpu.make_async_copy(dst.at[slot], dst.at[slot], sem.at[slot]).wait()


def _rows_2d(buf, slot, base, n):
    return jnp.concatenate(
        [buf[slot, pl.ds(base * ROW_TILES + s, n, stride=ROW_TILES), :] for s in range(ROW_TILES)], axis=1)


def _ffn_kernel(be_ref, nb_ref, *refs):
    tok_refs = refs[:FFN_SLOTS]
    h2_hbm, gate_ref, wg_ref, bg_ref, wu_ref, bu_ref, wd_ref, bd_ref, o_ref, xbuf, wgb, wub, wdb, sem = refs[FFN_SLOTS:]
    bm = tok_refs[0].shape[2]
    i = pl.program_id(0)
    nb = nb_ref[0]
    slot = i % FFN_SLOTS

    @pl.when((i == 0) | (be_ref[i] != be_ref[jnp.maximum(i - 1, 0)]))
    def _():
        wgb[...] = wg_ref[0].astype(BF16)
        wub[...] = wu_ref[0].astype(BF16)
        wdb[...] = wd_ref[0].astype(BF16)

    ahead = FFN_SLOTS - 1
    for b in range(ahead):
        @pl.when((i == 0) & (b < nb))
        def _(b=b):
            _issue_rows(tok_refs[b], range(bm), h2_hbm, xbuf, b, sem, queues=1)

    @pl.when(i + ahead < nb)
    def _():
        _issue_rows(tok_refs[ahead], range(bm), h2_hbm, xbuf, (i + ahead) % FFN_SLOTS, sem, queues=1)

    @pl.when(i < nb)
    def _():
        _wait_rows(xbuf, slot, sem)
        x = _rows_2d(xbuf, slot, 0, bm).astype(BF16)
        g = jnp.dot(x, wgb[...], preferred_element_type=F32) + bg_ref[0]
        u = jnp.dot(x, wub[...], preferred_element_type=F32) + bu_ref[0]
        g = jnp.minimum(g, SWIGLU_LIMIT)
        u = jnp.clip(u, -SWIGLU_LIMIT, SWIGLU_LIMIT)
        act = g * jax.nn.sigmoid(SWIGLU_ALPHA * g) * (u + 1.0)
        y = (jnp.dot(act.astype(BF16), wdb[...], preferred_element_type=F32) + bd_ref[0]) * gate_ref[...]
        for s in range(ROW_TILES):
            o_ref[pl.ds(s, bm, stride=ROW_TILES), :] = y[:, s * LANES:(s + 1) * LANES]

    @pl.when(i >= nb)
    def _():
        o_ref[...] = jnp.zeros(o_ref.shape, o_ref.dtype)


def _ffn(block_e, nb_used, buf_tok, h2_rows, buf_gate, wg, bg, wu, bu, wd, bd):
    NB = block_e.shape[0]
    bm = FFN_BM
    D, F = D_MODEL, D_FF
    tok3 = buf_tok.reshape(NB, 1, bm)
    tok_index = lambda j, i, be, nb: (jnp.minimum(i + j, NB - 1), 0, 0)
    wspec = lambda r, c: pl.BlockSpec((1, r, c), lambda i, be, nb: (be[i], 0, 0))
    vmem_limit = 2 * 3 * D * F * 4 + 3 * D * F * 2 + (2 + FFN_SLOTS) * bm * D * 4 + 6 * bm * F * 4
    return pl.pallas_call(
        _ffn_kernel,
        grid_spec=pltpu.PrefetchScalarGridSpec(
            num_scalar_prefetch=2,
            grid=(NB,),
            in_specs=[pl.BlockSpec((1, 1, bm), functools.partial(tok_index, j), memory_space=pltpu.SMEM)
                      for j in range(FFN_SLOTS)] + [
                      pl.BlockSpec(memory_space=pl.ANY),
                      pl.BlockSpec((bm, 1), lambda i, be, nb: (i, 0)),
                      wspec(D, F), wspec(1, F), wspec(D, F), wspec(1, F), wspec(F, D), wspec(1, D)],
            out_specs=pl.BlockSpec((bm * ROW_TILES, LANES), lambda i, be, nb: (i, 0)),
            scratch_shapes=[pltpu.VMEM((FFN_SLOTS, bm * ROW_TILES, LANES), F32),
                            pltpu.VMEM((D, F), BF16), pltpu.VMEM((D, F), BF16), pltpu.VMEM((F, D), BF16),
                            pltpu.SemaphoreType.DMA((FFN_SLOTS,))]),
        out_shape=jax.ShapeDtypeStruct((NB * bm * ROW_TILES, LANES), F32),
        compiler_params=pltpu.CompilerParams(dimension_semantics=("arbitrary",),
                                             vmem_limit_bytes=vmem_limit),
        name="ffn",
    )(block_e, nb_used, *([tok3] * FFN_SLOTS), h2_rows, buf_gate, wg, bg, wu, bu, wd, bd)


def _combine_kernel(da_ref, db_ref, y_hbm, x1_ref, gt2_ref, fg_ref, o_ref, buf, sem):
    tm = x1_ref.shape[0]
    n = TOP_K * tm
    i = pl.program_id(0)
    slot = i % 2

    @pl.when(i == 0)
    def _():
        _issue_rows(da_ref, range(n), y_hbm, buf, 0, sem)

    @pl.when(i + 1 < pl.num_programs(0))
    def _():
        _issue_rows(db_ref, range(n), y_hbm, buf, 1 - slot, sem)

    _wait_rows(buf, slot, sem)
    y = _rows_2d(buf, slot, 0, tm)
    for k in range(1, TOP_K):
        y = y + _rows_2d(buf, slot, k * tm, tm)
    x2 = x1_ref[...] + gt2_ref[0] * y
    o_ref[...] = _rms(x2, fg_ref[...])


def _combine(dest3, y_rows, x1, gt2, fg, S):
    T, D = x1.shape
    tm = min(COMB_TM, S)
    NT = T // tm
    per_b = S // tm
    n = TOP_K * tm
    return pl.pallas_call(
        _combine_kernel,
        grid=(NT,),
        in_specs=[pl.BlockSpec((1, 1, n), lambda i: (i, 0, 0), memory_space=pltpu.SMEM),
                  pl.BlockSpec((1, 1, n), lambda i: (jnp.minimum(i + 1, NT - 1), 0, 0), memory_space=pltpu.SMEM),
                  pl.BlockSpec(memory_space=pl.ANY),
                  pl.BlockSpec((tm, D), lambda i: (i, 0)),
                  pl.BlockSpec((1, 1, D), lambda i: (i // per_b, 0, 0)),
                  pl.BlockSpec((1, D), lambda i: (0, 0))],
        out_specs=pl.BlockSpec((tm, D), lambda i: (i, 0)),
        out_shape=jax.ShapeDtypeStruct((T, D), F32),
        scratch_shapes=[pltpu.VMEM((2, n * ROW_TILES, LANES), F32), pltpu.SemaphoreType.DMA((2,))],
        compiler_params=pltpu.CompilerParams(dimension_semantics=("arbitrary",)),
        name="combine",
    )(dest3, dest3, y_rows, x1, gt2, fg)


def _rope_tables(S):
    inv = ROPE_THETA ** (-jnp.arange(0, ROT_DIM, 2, dtype=F32) / ROT_DIM)
    ang = jnp.arange(S, dtype=F32)[:, None] * inv[None, :]
    cos, sin = jnp.cos(ang), jnp.sin(ang)
    d = jnp.arange(KV_W) % HEAD_DIM
    first, second = d < ROT_HALF, (d >= ROT_HALF) & (d < ROT_DIM)
    cos_l = cos[:, d % ROT_HALF]
    sin_l = sin[:, d % ROT_HALF]
    rc = jnp.where((d < ROT_DIM)[None], cos_l, 1.0)
    rs1 = jnp.where(second[None], sin_l, 0.0)
    rs2 = jnp.where(first[None], -sin_l, 0.0)
    return rc, rs1, rs2, cos.T, sin.T


def _route_plan(route, counts, T):
    bm = FFN_BM
    A = T * TOP_K
    idx = route[0:TOP_K].astype(I32)
    gate = route[TOP_K:2 * TOP_K]
    rank = route[2 * TOP_K:3 * TOP_K].astype(I32)
    keys = idx * T + jnp.arange(T, dtype=I32)[None, :]
    skey, sgate = lax.sort((keys.reshape(A), gate.reshape(A)), num_keys=1)
    counts = counts.astype(I32)
    starts = jnp.cumsum(counts) - counts
    padded = (counts + bm - 1) // bm * bm
    pends = jnp.cumsum(padded)
    pstarts = pends - padded
    P = (A + N_EXPERTS * bm + bm - 1) // bm * bm
    NB = P // bm
    blk0 = jnp.arange(NB, dtype=I32) * bm
    block_e = jnp.minimum(jnp.sum((pends[None, :] <= blk0[:, None]).astype(I32), axis=1), N_EXPERTS - 1)
    r = (blk0 - pstarts[block_e])[:, None] + jnp.arange(bm, dtype=I32)[None, :]
    valid = r < counts[block_e][:, None]
    src = jnp.clip(starts[block_e][:, None] + r, 0, A - 1)
    buf_tok = jnp.where(valid, skey[src] - block_e[:, None] * T, 0)
    buf_gate = jnp.where(valid, sgate[src], 0.0).reshape(P, 1)
    dest = rank
    for e in range(N_EXPERTS):
        dest = dest + jnp.where(idx == e, pstarts[e], 0)
    nb_used = (pends[-1] // bm).astype(I32).reshape(1)
    return block_e, nb_used, buf_tok, buf_gate, dest


def kernel(x, c, norm1_g, norm2_g, w_ada, b_ada, w_in, conv_w, conv_b, conv_ln_g, conv_ln_b, cmp_pe_k, cmp_pe_v,
           cmp_k_w1, cmp_k_w2, cmp_v_w1, cmp_v_w2, out_norm_conv, out_norm_nsa, w_out, w_router, b_router,
           w_gate, b_gate, w_up, b_up, w_down, b_down, final_norm_g):
    B, S, D = x.shape
    T = B * S
    G = KV_GROUPS
    assert D == D_MODEL and S % ATTN_TK == 0 and S % CMP_STRIDE == 0 and KV_W == LANES
    rc, rs1, rs2, cos_t, sin_t = _rope_tables(S)
    n_sel = S // SEL_BLK
    nc = S // CMP_STRIDE
    cstart = jnp.arange(nc) * CMP_STRIDE
    jstart = jnp.arange(n_sel) * SEL_BLK
    overlap_t = ((cstart[None, :] <= jstart[:, None] + SEL_BLK - 1)
                 & (cstart[None, :] + CMP_LEN - 1 >= jstart[:, None])
                 & (jnp.arange(nc)[None, :] < nc - 1)).astype(F32)

    assert w_ada.shape[0] == 1
    for l in range(1):
        mod = _adaln(c, w_ada[l], b_ada[l][None])
        sh1, sc1, gt1, sh2, sc2, gt2 = [m[:, None, :] for m in jnp.split(mod, 6, axis=-1)]

        wl = w_in[l]
        o = 2 * CONV_CH + NSA_WIDTH
        kvc = [wl[:, o + i * KV_W:o + (i + 1) * KV_W] for i in range(6)]
        gl = wl[:, o + 6 * KV_W:]
        per_g = 3 * Q_PER_G
        gpad = [jnp.pad(gl[:, per_g * g:per_g * (g + 1)], ((0, 0), (0, GATE_ROWS - per_g))) for g in range(G)]
        wn = jnp.concatenate([wl[:, :2 * CONV_CH], kvc[0], kvc[2], kvc[4], kvc[1]], axis=1).astype(BF16)
        wt = jnp.concatenate([wl[:, 2 * CONV_CH:o], kvc[3], kvc[5]] + gpad, axis=1).T.astype(BF16)
        u, kc, vc, ks, kw, qt, vst, vwt, gates = _inproj(x, sc1, sh1, norm1_g[l][None], wn, wt,
                                                         rc, rs1, rs2, cos_t, sin_t)

        conv_n = _conv(u, conv_w[l], conv_b[l][None], conv_ln_g[l][None], conv_ln_b[l][None],
                       out_norm_conv[l][None])

        kcmp, vcmp_t = _compress(kc, vc, cmp_pe_k[l].reshape(1, -1), cmp_pe_v[l].reshape(1, -1),
                                 cmp_k_w1[l], cmp_k_w2[l], cmp_v_w1[l], cmp_v_w2[l].T)
        nsa = _attention(qt, kcmp, vcmp_t, ks, vst, kw, vwt, gates, overlap_t)

        x1, h2_rows, route, counts = _outproj(x, conv_n, nsa, out_norm_nsa[l][None], w_out[l].astype(BF16), gt1,
                                              norm2_g[l][None], sc2, sh2, w_router[l], b_router[l][None])

        block_e, nb_used, buf_tok, buf_gate, dest = _route_plan(route, counts[:, 0], T)
        y_rows = _ffn(block_e, nb_used, buf_tok, h2_rows.reshape(T, ROW_TILES, LANES), buf_gate,
                      w_gate[l], b_gate[l][:, None, :], w_up[l], b_up[l][:, None, :],
                      w_down[l], b_down[l][:, None, :])
        tm = min(COMB_TM, S)
        dest3 = dest.reshape(TOP_K, T // tm, tm).transpose(1, 0, 2).reshape(T // tm, 1, TOP_K * tm)
        P = y_rows.shape[0] // ROW_TILES
        x = _combine(dest3, y_rows.reshape(P, ROW_TILES, LANES), x1.reshape(T, D), gt2, final_norm_g[None],
                     S).reshape(B, S, D)
    return x
```

```python
import functools

import jax
import jax.numpy as jnp
from jax import lax
from jax.experimental import pallas as pl
from jax.experimental.pallas import tpu as pltpu

F32 = jnp.float32
BF16 = jnp.bfloat16
I32 = jnp.int32
HI = lax.Precision.HIGHEST

D_MODEL = 1024
CONV_CH = 512
CONV_WIDTH = 31
NSA_HEADS = 8
KV_GROUPS = 2
Q_PER_G = NSA_HEADS // KV_GROUPS
HEAD_DIM = 64
NSA_WIDTH = NSA_HEADS * HEAD_DIM
KV_W = KV_GROUPS * HEAD_DIM
ROT_DIM = HEAD_DIM // 4
ROT_HALF = ROT_DIM // 2
ROPE_THETA = 500000.0
CMP_LEN = 32
CMP_STRIDE = 16
CMP_HIDDEN = 128
SEL_BLK = 64
SEL_TOPN = 16
WINDOW = 512
N_EXPERTS = 32
TOP_K = 4
D_FF = 1024
SWIGLU_ALPHA = 1.702
SWIGLU_LIMIT = 7.0
NORM_EPS = 1e-5
NEG_INF = -1e30
FORCE_SCORE = 1e9
LOG2_E = 1.4426950408889634

LANES = 128
SUBLANES = 8
ROW_TILES = D_MODEL // LANES

GATE_ROWS = 16

INPROJ_TM = 1024
CONV_TR = 512
CONV_HALO = 32
ATTN_TQ = 512
ATTN_TK = 512
OUT_TM = 1024
FFN_BM = 512
FFN_SLOTS = 4
COMB_TM = 256
COMB_SLOTS = 3
ROUTE_W = 16


def _rms(x, g):
    return x * lax.rsqrt(jnp.mean(x * x, axis=-1, keepdims=True) + NORM_EPS) * g


def _split_bf16(a):
    hi = a.astype(BF16)
    return hi, (a - hi.astype(F32)).astype(BF16)


def _dot3(a, b, dims=(((1,), (0,)), ((), ()))):
    ah, al = _split_bf16(a)
    bh, bl = _split_bf16(b)
    d = lambda x, y: lax.dot_general(x, y, dims, preferred_element_type=F32)
    return d(ah, bh) + d(al, bh) + d(ah, bl)


def _ada_kernel(c_ref, w_ref, b_ref, o_ref):
    c = c_ref[...]
    ca = c * jax.nn.sigmoid(c)
    o_ref[...] = jnp.dot(ca, w_ref[...], preferred_element_type=F32, precision=HI) + b_ref[...]


def _adaln(c, w, b):
    B = c.shape[0]
    D = D_MODEL
    return pl.pallas_call(
        _ada_kernel,
        grid=(6,),
        in_specs=[pl.BlockSpec((B, D), lambda j: (0, 0)),
                  pl.BlockSpec((D, D), lambda j: (0, j)),
                  pl.BlockSpec((1, D), lambda j: (0, j))],
        out_specs=pl.BlockSpec((B, D), lambda j: (0, j)),
        out_shape=jax.ShapeDtypeStruct((B, 6 * D), F32),
        name="adaln",
    )(c, w, b)


def _inproj_kernel(x_ref, sc_ref, sh_ref, g_ref, wn_ref, wt_ref, rc_ref, rs1_ref, rs2_ref, ct_ref, st_ref,
                   u_ref, kc_ref, vc_ref, ks_ref, kw_ref, qt_ref, vst_ref, vwt_ref, gt_ref, stage_ref):
    tm = x_ref.shape[1]
    tq, tk = ATTN_TQ, ATTN_TK
    h = (_rms(x_ref[0], g_ref[...]) * (1.0 + sc_ref[0]) + sh_ref[0]).astype(BF16)

    p = jnp.dot(h, wn_ref[...], preferred_element_type=F32)
    u_ref[0] = p[:, 0:CONV_CH] * jax.nn.sigmoid(p[:, CONV_CH:2 * CONV_CH])
    c0 = 2 * CONV_CH
    rc, rs1, rs2 = rc_ref[...], rs1_ref[...], rs2_ref[...]
    for ref, roped in ((kc_ref, True), (ks_ref, True), (kw_ref, True), (vc_ref, False)):
        v = p[:, c0:c0 + KV_W]
        if roped:
            v = v * rc + pltpu.roll(v, ROT_HALF, 1) * rs1 + pltpu.roll(v, KV_W - ROT_HALF, 1) * rs2
        c0 += KV_W
        if ref is kc_ref or ref is vc_ref:
            stage_ref[...] = v
            left = lax.broadcasted_iota(I32, (tm // CMP_STRIDE, KV_W), 1) < HEAD_DIM
            pieces = [stage_ref[pl.ds(tl, tm // CMP_STRIDE, stride=CMP_STRIDE), :] for tl in range(CMP_STRIDE)]
            for gg in range(KV_GROUPS):
                cols = []
                for tl in range(0, CMP_STRIDE, 2):
                    a, b = pieces[tl], pieces[tl + 1]
                    if gg == 0:
                        cols.append(jnp.where(left, a, pltpu.roll(b, HEAD_DIM, 1)))
                    else:
                        cols.append(jnp.where(left, pltpu.roll(a, HEAD_DIM, 1), b))
                ref[0, gg] = jnp.concatenate(cols, axis=1)
            continue
        for gg in range(KV_GROUPS):
            ref[0, gg] = v[:, HEAD_DIM * gg:HEAD_DIM * (gg + 1)].astype(ref.dtype)

    pt = lax.dot_general(wt_ref[...], h, (((1,), (1,)), ((), ())), preferred_element_type=F32)
    cos_t, sin_t = ct_ref[...], st_ref[...]
    scale = HEAD_DIM ** -0.5 * LOG2_E
    for hh in range(NSA_HEADS):
        blk = pt[HEAD_DIM * hh:HEAD_DIM * (hh + 1), :]
        x1, x2 = blk[0:ROT_HALF], blk[ROT_HALF:ROT_DIM]
        qh = (jnp.concatenate([x1 * cos_t - x2 * sin_t, x2 * cos_t + x1 * sin_t, blk[ROT_DIM:]], axis=0)
              * scale).astype(BF16)
        gg, n = divmod(hh, Q_PER_G)
        for j in range(tm // tq):
            qt_ref[0, gg, j, :, n * tq:(n + 1) * tq] = qh[:, j * tq:(j + 1) * tq]
    r0 = NSA_WIDTH
    for ref in (vst_ref, vwt_ref):
        for gg in range(KV_GROUPS):
            blk = pt[r0 + HEAD_DIM * gg:r0 + HEAD_DIM * (gg + 1), :].astype(BF16)
            for j in range(tm // tk):
                ref[0, gg, j] = blk[:, j * tk:(j + 1) * tk]
        r0 += KV_W
    for gg in range(KV_GROUPS):
        gt_ref[0, gg] = jax.nn.sigmoid(pt[r0 + GATE_ROWS * gg:r0 + GATE_ROWS * (gg + 1), :])


def _inproj(x, sc, sh, g, wn, wt, rc, rs1, rs2, cos_t, sin_t):
    B, S, D = x.shape
    tm = min(INPROJ_TM, S)
    tq, tk = ATTN_TQ, ATTN_TK
    G = KV_GROUPS
    kv = lambda dt: jax.ShapeDtypeStruct((B, G, S, HEAD_DIM), dt)
    kv_spec = pl.BlockSpec((1, G, tm, HEAD_DIM), lambda b, i: (b, 0, i, 0))
    chunk = CMP_STRIDE * HEAD_DIM
    ck_shape = jax.ShapeDtypeStruct((B, G, S // CMP_STRIDE, chunk), F32)
    ck_spec = pl.BlockSpec((1, G, tm // CMP_STRIDE, chunk), lambda b, i: (b, 0, i, 0))
    vt_shape = jax.ShapeDtypeStruct((B, G, S // tk, HEAD_DIM, tk), BF16)
    vt_spec = pl.BlockSpec((1, G, tm // tk, HEAD_DIM, tk), lambda b, i: (b, 0, i, 0, 0))
    row = pl.BlockSpec((1, 1, D), lambda b, i: (b, 0, 0))
    tab = pl.BlockSpec((tm, LANES), lambda b, i: (i, 0))
    tab_t = pl.BlockSpec((ROT_HALF, tm), lambda b, i: (0, i))
    return pl.pallas_call(
        _inproj_kernel,
        grid=(B, S // tm),
        in_specs=[pl.BlockSpec((1, tm, D), lambda b, i: (b, i, 0)), row, row,
                  pl.BlockSpec((1, D), lambda b, i: (0, 0)),
                  pl.BlockSpec(wn.shape, lambda b, i: (0, 0)),
                  pl.BlockSpec(wt.shape, lambda b, i: (0, 0)),
                  tab, tab, tab, tab_t, tab_t],
        out_specs=[pl.BlockSpec((1, tm, CONV_CH), lambda b, i: (b, i, 0)),
                   ck_spec, ck_spec, kv_spec, kv_spec,
                   pl.BlockSpec((1, G, tm // tq, HEAD_DIM, Q_PER_G * tq), lambda b, i: (b, 0, i, 0, 0)),
                   vt_spec, vt_spec,
                   pl.BlockSpec((1, G, GATE_ROWS, tm), lambda b, i: (b, 0, 0, i))],
        out_shape=[jax.ShapeDtypeStruct((B, S, CONV_CH), F32),
                   ck_shape, ck_shape, kv(BF16), kv(BF16),
                   jax.ShapeDtypeStruct((B, G, S // tq, HEAD_DIM, Q_PER_G * tq), BF16),
                   vt_shape, vt_shape,
                   jax.ShapeDtypeStruct((B, G, GATE_ROWS, S), F32)],
        scratch_shapes=[pltpu.VMEM((tm, KV_W), F32)],
        compiler_params=pltpu.CompilerParams(dimension_semantics=("parallel", "parallel")),
        name="inproj",
    )(x, sc, sh, g, wn, wt, rc, rs1, rs2, cos_t, sin_t)


def _conv_kernel(prev_ref, cur_ref, w_ref, cb_ref, lg_ref, lb_ref, on_ref, o_ref, pad_ref, win_ref):
    tr = cur_ref.shape[1]
    first = pl.program_id(1) == 0
    halo = prev_ref[0, tr - CONV_HALO:tr, :]
    pad_ref[0:CONV_HALO, :] = jnp.where(first, 0.0, halo)
    pad_ref[CONV_HALO:CONV_HALO + tr, :] = cur_ref[0]
    off = CONV_HALO - (CONV_WIDTH - 1)
    acc = jnp.zeros((tr, CONV_CH), F32)
    for b in range(SUBLANES):
        taps = range(b, CONV_WIDTH, SUBLANES)
        rows = tr + SUBLANES * (len(taps) - 1)
        win_ref[b, 0:rows, :] = pad_ref[off + b:off + b + rows, :]
        for a, k in enumerate(taps):
            acc = acc + win_ref[b, SUBLANES * a:SUBLANES * a + tr, :] * w_ref[k:k + 1, :]
    y = acc + cb_ref[...]
    mu = jnp.mean(y, axis=-1, keepdims=True)
    yc = y - mu
    var = jnp.mean(yc * yc, axis=-1, keepdims=True)
    yn = yc * lax.rsqrt(var + NORM_EPS) * lg_ref[...] + lb_ref[...]
    s = yn * jax.nn.sigmoid(yn)
    o_ref[0] = _rms(s, on_ref[...]).astype(o_ref.dtype)


def _conv(u, w, cb, lg, lb, on):
    B, S, C = u.shape
    tr = min(CONV_TR, S)
    vec = pl.BlockSpec((1, C), lambda b, i: (0, 0))
    return pl.pallas_call(
        _conv_kernel,
        grid=(B, S // tr),
        in_specs=[pl.BlockSpec((1, tr, C), lambda b, i: (b, jnp.maximum(i - 1, 0), 0)),
                  pl.BlockSpec((1, tr, C), lambda b, i: (b, i, 0)),
                  pl.BlockSpec((CONV_WIDTH, C), lambda b, i: (0, 0)),
                  vec, vec, vec, vec],
        out_specs=pl.BlockSpec((1, tr, C), lambda b, i: (b, i, 0)),
        out_shape=jax.ShapeDtypeStruct((B, S, C), BF16),
        scratch_shapes=[pltpu.VMEM((CONV_HALO + tr, C), F32),
                        pltpu.VMEM((SUBLANES, tr + SUBLANES * ((CONV_WIDTH - 1) // SUBLANES), C), F32)],
        compiler_params=pltpu.CompilerParams(dimension_semantics=("parallel", "parallel")),
        name="conv",
    )(u, u, w, cb, lg, lb, on)


def _cmp_kernel(kx_ref, vx_ref, pek_ref, pev_ref, kw1_ref, kw2_ref, vw1_ref, vw2t_ref, ko_ref, vo_ref):
    nc = kx_ref.shape[2]
    half = kx_ref.shape[3]
    nt = (((1,), (1,)), ((), ()))
    for x_ref, pe_ref, w1_ref, w2_ref, o_ref, transposed in ((kx_ref, pek_ref, kw1_ref, kw2_ref, ko_ref, False),
                                                             (vx_ref, pev_ref, vw1_ref, vw2t_ref, vo_ref, True)):
        w1 = w1_ref[...]
        pe = jnp.broadcast_to(pe_ref[...], (SUBLANES, 2 * half))
        pe_proj = _dot3(pe, w1)[0:1]
        for gg in range(KV_GROUPS):
            xg = x_ref[0, gg]
            first = _dot3(xg, w1[0:half])
            second = _dot3(xg, w1[half:2 * half])
            hid = first + pltpu.roll(second, nc - 1, 0) + pe_proj
            hid = hid * jax.nn.sigmoid(hid)
            if transposed:
                o_ref[0, gg] = _dot3(w2_ref[...], hid, nt)
            else:
                o_ref[0, gg] = _dot3(hid, w2_ref[...])


def _compress(kx, vx, pek, pev, kw1, kw2, vw1, vw2t):
    B, G, NC, W = kx.shape
    xs = pl.BlockSpec((1, G, NC, W), lambda b: (b, 0, 0, 0))
    full = lambda a: pl.BlockSpec(a.shape, lambda b: (0,) * a.ndim)
    return pl.pallas_call(
        _cmp_kernel,
        grid=(B,),
        in_specs=[xs, xs, full(pek), full(pev), full(kw1), full(kw2), full(vw1), full(vw2t)],
        out_specs=[pl.BlockSpec((1, G, NC, HEAD_DIM), lambda b: (b, 0, 0, 0)),
                   pl.BlockSpec((1, G, HEAD_DIM, NC), lambda b: (b, 0, 0, 0))],
        out_shape=[jax.ShapeDtypeStruct((B, G, NC, HEAD_DIM), F32),
                   jax.ShapeDtypeStruct((B, G, HEAD_DIM, NC), F32)],
        compiler_params=pltpu.CompilerParams(dimension_semantics=("parallel",)),
        name="compress",
    )(kx, vx, pek, pev, kw1, kw2, vw1, vw2t)


def _attn_kernel(qt_ref, kc_ref, vct_ref, ks_ref, vst_ref, kw_ref, vwt_ref, gt_ref, ovt_ref, o_ref, sel_ref):
    tq, tk = ATTN_TQ, ATTN_TK
    R = Q_PER_G * tq
    per_tile = tk // SEL_BLK
    S = ks_ref.shape[2]
    ncp = kc_ref.shape[2]
    nsel = ovt_ref.shape[0]
    qi = pl.program_id(2)
    q0 = qi * tq
    qt = qt_ref[0, 0, 0]
    t_row = q0 + lax.broadcasted_iota(I32, (1, tq), 1)
    heads = lambda a: jnp.concatenate([a] * Q_PER_G, axis=1)

    sc = jnp.dot(kc_ref[0, 0].astype(BF16), qt, preferred_element_type=F32)
    c_io = lax.broadcasted_iota(I32, (ncp, tq), 0)
    m_c = (c_io * CMP_STRIDE + (CMP_LEN - 1) <= t_row) & (c_io < ncp - 1)
    scb = sc + heads(jnp.where(m_c, 0.0, NEG_INF))
    e = jnp.exp2(scb - jnp.max(scb, axis=0, keepdims=True)) * heads(jnp.where(m_c, 1.0, 0.0))
    den = jnp.sum(e, axis=0, keepdims=True)
    pc = e / jnp.where(den > 0.0, den, 1.0)
    o_cmp = jnp.dot(vct_ref[0, 0].astype(BF16), pc.astype(BF16), preferred_element_type=F32)

    psum = pc[:, 0:tq]
    for n in range(1, Q_PER_G):
        psum = psum + pc[:, n * tq:(n + 1) * tq]
    imp = _dot3(ovt_ref[...], psum)
    j_io = lax.broadcasted_iota(I32, (nsel, tq), 0)
    cur = t_row // SEL_BLK
    valid = j_io * SEL_BLK <= t_row
    forced = (j_io == 0) | (j_io == cur) | (j_io == cur - 1)
    score = jnp.where(valid, jnp.where(forced, FORCE_SCORE, imp), NEG_INF)
    rank = jnp.zeros((nsel, tq), F32)
    for i in range(nsel):
        row = score[i:i + 1, :]
        tie = jnp.where(j_io > i, 1.0, 0.0)
        rank = rank + jnp.where(row > score, 1.0, jnp.where(row == score, tie, 0.0))
    sel_bias = jnp.where(rank < float(min(SEL_TOPN, nsel)), 0.0, NEG_INF)
    sel_ref[...] = jnp.zeros(sel_ref.shape, F32)
    for jj in range(S // tk):
        sel_ref[jj, 0:per_tile, :] = sel_bias[jj * per_tile:(jj + 1) * per_tile, :]

    k_io = lax.broadcasted_iota(I32, (tk, tq), 0)

    def flash_step(k_ref, vt_ref, kj, bias, carry):
        m, l, acc = carry
        k0 = pl.multiple_of(kj * tk, tk)
        s = jnp.dot(k_ref[0, 0, pl.ds(k0, tk), :], qt, preferred_element_type=F32) + heads(bias)
        m_new = jnp.maximum(m, jnp.max(s, axis=0, keepdims=True))
        alpha = jnp.exp2(m - m_new)
        p = jnp.exp2(s - m_new)
        l = alpha * l + jnp.sum(p, axis=0, keepdims=True)
        acc = alpha * acc + jnp.dot(vt_ref[0, 0, kj], p.astype(BF16), preferred_element_type=F32)
        return m_new, l, acc

    init = (jnp.full((1, R), NEG_INF, F32), jnp.zeros((1, R), F32), jnp.zeros((HEAD_DIM, R), F32))

    def slc_body(kj, carry):
        blocks = sel_ref[kj]
        bias = jnp.concatenate([jnp.broadcast_to(blocks[b:b + 1, :], (SEL_BLK, tq)) for b in range(per_tile)], axis=0)
        bias = jnp.where(kj * tk + k_io <= t_row, bias, NEG_INF)
        return flash_step(ks_ref, vst_ref, kj, bias, carry)

    def win_body(kj, carry):
        rel = t_row - (kj * tk + k_io)
        bias = jnp.where((rel >= 0) & (rel < WINDOW), 0.0, NEG_INF)
        return flash_step(kw_ref, vwt_ref, kj, bias, carry)

    n_slc = (q0 + tq + tk - 1) // tk
    lo_tile = jnp.maximum(q0 - (WINDOW - 1), 0) // tk
    n_pair = lo_tile // 2

    def pair_body(i, carry):
        return slc_body(2 * i, carry[0]), slc_body(2 * i + 1, carry[1])

    st_a, st_b = lax.fori_loop(0, n_pair, pair_body, (init, init))
    st_a = lax.fori_loop(2 * n_pair, lo_tile, slc_body, st_a)

    def both_body(kj, carry):
        return slc_body(kj, carry[0]), win_body(kj, carry[1])

    st_a, (_, l_w, acc_w) = lax.fori_loop(lo_tile, n_slc, both_body, (st_a, init))
    m_s = jnp.maximum(st_a[0], st_b[0])
    w_a, w_b = jnp.exp2(st_a[0] - m_s), jnp.exp2(st_b[0] - m_s)
    l_s = w_a * st_a[1] + w_b * st_b[1]
    acc_s = w_a * st_a[2] + w_b * st_b[2]


    gt = gt_ref[0, 0]
    o_slc = acc_s / l_s
    o_win = acc_w / l_w
    outs = []
    for n in range(Q_PER_G):
        cols = slice(n * tq, (n + 1) * tq)
        outs.append(gt[3 * n:3 * n + 1, :] * o_cmp[:, cols] + gt[3 * n + 1:3 * n + 2, :] * o_slc[:, cols]
                    + gt[3 * n + 2:3 * n + 3, :] * o_win[:, cols])
    o_ref[0] = jnp.concatenate(outs, axis=0).T


def _attention(qt, kc, vct, ks, vst, kw, vwt, gt, overlap_t):
    B, G, NQT, _, R = qt.shape
    S = ks.shape[2]
    tq, tk = ATTN_TQ, ATTN_TK
    ncp = kc.shape[2]
    kfull = pl.BlockSpec((1, 1, S, HEAD_DIM), lambda b, g, i: (b, g, 0, 0))
    vfull = pl.BlockSpec((1, 1, S // tk, HEAD_DIM, tk), lambda b, g, i: (b, g, 0, 0, 0))
    return pl.pallas_call(
        _attn_kernel,
        grid=(B, G, NQT),
        in_specs=[pl.BlockSpec((1, 1, 1, HEAD_DIM, R), lambda b, g, i: (b, g, i, 0, 0)),
                  pl.BlockSpec((1, 1, ncp, HEAD_DIM), lambda b, g, i: (b, g, 0, 0)),
                  pl.BlockSpec((1, 1, HEAD_DIM, ncp), lambda b, g, i: (b, g, 0, 0)),
                  kfull, vfull, kfull, vfull,
                  pl.BlockSpec((1, 1, GATE_ROWS, tq), lambda b, g, i: (b, g, 0, i)),
                  pl.BlockSpec(overlap_t.shape, lambda b, g, i: (0, 0))],
        out_specs=pl.BlockSpec((1, tq, Q_PER_G * HEAD_DIM), lambda b, g, i: (b, i, g)),
        out_shape=jax.ShapeDtypeStruct((B, S, NSA_WIDTH), F32),
        scratch_shapes=[pltpu.VMEM((S // tk, SUBLANES, tq), F32)],
        compiler_params=pltpu.CompilerParams(dimension_semantics=("parallel", "parallel", "arbitrary")),
        name="attn",
    )(qt, kc, vct, ks, vst, kw, vwt, gt, overlap_t)


def _outproj_kernel(x_ref, cv_ref, nsa_ref, on_ref, w_ref, gt1_ref, g2_ref, sc2_ref, sh2_ref, wrh_ref, wrl_ref, br_ref,
                    tri_ref, x1_ref, h2_ref, rt_ref, cnt_ref, run_ref):
    tm = x_ref.shape[1]

    @pl.when((pl.program_id(0) == 0) & (pl.program_id(1) == 0))
    def _():
        run_ref[...] = jnp.zeros(run_ref.shape, F32)

    nn = _rms(nsa_ref[0], on_ref[...]).astype(BF16)
    y = jnp.dot(jnp.concatenate([cv_ref[0], nn], axis=1), w_ref[...], preferred_element_type=F32)
    x1 = x_ref[0] + gt1_ref[0] * y
    x1_ref[0] = x1
    h2 = _rms(x1, g2_ref[...]) * (1.0 + sc2_ref[0]) + sh2_ref[0]
    for s in range(ROW_TILES):
        h2_ref[pl.ds(s, tm, stride=ROW_TILES), :] = h2[:, s * LANES:(s + 1) * LANES]
    nt = (((1,), (1,)), ((), ()))
    h_hi = h2.astype(BF16)
    h_lo = (h2 - h_hi.astype(F32)).astype(BF16)
    logits = (lax.dot_general(wrh_ref[...], h_hi, nt, preferred_element_type=F32)
              + lax.dot_general(wrh_ref[...], h_lo, nt, preferred_element_type=F32)
              + lax.dot_general(wrl_ref[...], h_hi, nt, preferred_element_type=F32)) + br_ref[...]
    eio = lax.broadcasted_iota(I32, (N_EXPERTS, tm), 0).astype(F32)
    vals, idxs = [], []
    for _ in range(TOP_K):
        m = jnp.max(logits, axis=0, keepdims=True)
        ix = jnp.min(jnp.where(logits == m, eio, float(N_EXPERTS)), axis=0, keepdims=True)
        vals.append(m)
        idxs.append(ix)
        logits = jnp.where(eio == ix, -jnp.inf, logits)
    es = [jnp.exp(v - vals[0]) for v in vals]
    den = es[0] + es[1] + es[2] + es[3]
    hot = jnp.zeros((N_EXPERTS, tm), F32)
    for r in range(TOP_K):
        hot = hot + jnp.where(eio == idxs[r], 1.0, 0.0)
    before = run_ref[...] + jnp.dot(hot.astype(BF16), tri_ref[...], preferred_element_type=F32)
    ranks = [jnp.sum(jnp.where(eio == idxs[r], before, 0.0), axis=0, keepdims=True) for r in range(TOP_K)]
    run_ref[...] = run_ref[...] + jnp.sum(hot, axis=1, keepdims=True)
    cnt_ref[...] = run_ref[...]
    rio = lax.broadcasted_iota(I32, (ROUTE_W, tm), 0)
    out = jnp.zeros((ROUTE_W, tm), F32)
    for r in range(TOP_K):
        out = jnp.where(rio == r, idxs[r], out)
        out = jnp.where(rio == TOP_K + r, es[r] / den, out)
        out = jnp.where(rio == 2 * TOP_K + r, ranks[r], out)
    rt_ref[...] = out


def _outproj(x, cv, nsa, on, w, gt1, g2, sc2, sh2, wr, br):
    B, S, D = x.shape
    tm = min(OUT_TM, S)
    nt = S // tm
    tri = (jnp.arange(tm)[:, None] < jnp.arange(tm)[None, :]).astype(BF16)
    wr_t = wr.T
    wr_hi = wr_t.astype(BF16)
    wr_lo = (wr_t - wr_hi.astype(F32)).astype(BF16)
    row = pl.BlockSpec((1, 1, D), lambda b, i: (b, 0, 0))
    vec = lambda n: pl.BlockSpec((1, n), lambda b, i: (0, 0))
    col = pl.BlockSpec((N_EXPERTS, 1), lambda b, i: (0, 0))
    wr_spec = pl.BlockSpec((N_EXPERTS, D), lambda b, i: (0, 0))
    return pl.pallas_call(
        _outproj_kernel,
        grid=(B, nt),
        in_specs=[pl.BlockSpec((1, tm, D), lambda b, i: (b, i, 0)),
                  pl.BlockSpec((1, tm, CONV_CH), lambda b, i: (b, i, 0)),
                  pl.BlockSpec((1, tm, NSA_WIDTH), lambda b, i: (b, i, 0)),
                  vec(NSA_WIDTH),
                  pl.BlockSpec((D, D), lambda b, i: (0, 0)),
                  row, vec(D), row, row,
                  wr_spec, wr_spec, col,
                  pl.BlockSpec((tm, tm), lambda b, i: (0, 0))],
        out_specs=[pl.BlockSpec((1, tm, D), lambda b, i: (b, i, 0)),
                   pl.BlockSpec((tm * ROW_TILES, LANES), lambda b, i: (b * nt + i, 0)),
                   pl.BlockSpec((ROUTE_W, tm), lambda b, i: (0, b * nt + i)),
                   col],
        out_shape=[jax.ShapeDtypeStruct((B, S, D), F32),
                   jax.ShapeDtypeStruct((B * S * ROW_TILES, LANES), F32),
                   jax.ShapeDtypeStruct((ROUTE_W, B * S), F32),
                   jax.ShapeDtypeStruct((N_EXPERTS, 1), F32)],
        scratch_shapes=[pltpu.VMEM((N_EXPERTS, 1), F32)],
        compiler_params=pltpu.CompilerParams(dimension_semantics=("arbitrary", "arbitrary")),
        name="outproj",
    )(x, cv, nsa, on, w, gt1, g2, sc2, sh2, wr_hi, wr_lo, br.reshape(N_EXPERTS, 1), tri)


def _issue_rows(idx_ref, rows, src_hbm, dst, slot, sem, queues=2):
    for r in rows:
        pltpu.make_async_copy(src_hbm.at[idx_ref[0, 0, r]],
                              dst.at[slot, pl.ds(r * ROW_TILES, ROW_TILES), :],
                              sem.at[slot]).start(priority=r % 2 if queues == 2 else 1)


def _wait_rows(dst, slot, sem):
    pltpu.make_async_copy(dst.at[slot], dst.at[slot], sem.at[slot]).wait()


def _rows_2d(buf, slot, base, n):
    return jnp.concatenate(
        [buf[slot, pl.ds(base * ROW_TILES + s, n, stride=ROW_TILES), :] for s in range(ROW_TILES)], axis=1)


def _ffn_kernel(be_ref, nb_ref, *refs):
    tok_refs = refs[:FFN_SLOTS]
    h2_hbm, gate_ref, wg_ref, bg_ref, wu_ref, bu_ref, wd_ref, bd_ref, o_ref, xbuf, wgb, wub, wdb, sem = refs[FFN_SLOTS:]
    bm = tok_refs[0].shape[2]
    i = pl.program_id(0)
    nb = nb_ref[0]
    slot = i % FFN_SLOTS

    @pl.when((i == 0) | (be_ref[i] != be_ref[jnp.maximum(i - 1, 0)]))
    def _():
        wgb[...] = wg_ref[0].astype(BF16)
        wub[...] = wu_ref[0].astype(BF16)
        wdb[...] = wd_ref[0].astype(BF16)

    ahead = FFN_SLOTS - 1
    for b in range(ahead):
        @pl.when((i == 0) & (b < nb))
        def _(b=b):
            _issue_rows(tok_refs[b], range(bm), h2_hbm, xbuf, b, sem, queues=1)

    @pl.when(i + ahead < nb)
    def _():
        _issue_rows(tok_refs[ahead], range(bm), h2_hbm, xbuf, (i + ahead) % FFN_SLOTS, sem)

    @pl.when(i < nb)
    def _():
        _wait_rows(xbuf, slot, sem)
        x = _rows_2d(xbuf, slot, 0, bm).astype(BF16)
        g = jnp.dot(x, wgb[...], preferred_element_type=F32) + bg_ref[0]
        u = jnp.dot(x, wub[...], preferred_element_type=F32) + bu_ref[0]
        g = jnp.minimum(g, SWIGLU_LIMIT)
        u = jnp.clip(u, -SWIGLU_LIMIT, SWIGLU_LIMIT)
        act = g * jax.nn.sigmoid(SWIGLU_ALPHA * g) * (u + 1.0)
        y = (jnp.dot(act.astype(BF16), wdb[...], preferred_element_type=F32) + bd_ref[0]) * gate_ref[...]
        for s in range(ROW_TILES):
            o_ref[pl.ds(s, bm, stride=ROW_TILES), :] = y[:, s * LANES:(s + 1) * LANES]

    @pl.when(i >= nb)
    def _():
        o_ref[...] = jnp.zeros(o_ref.shape, o_ref.dtype)


def _ffn(block_e, nb_used, buf_tok, h2_rows, buf_gate, wg, bg, wu, bu, wd, bd):
    NB = block_e.shape[0]
    bm = FFN_BM
    D, F = D_MODEL, D_FF
    tok3 = buf_tok.reshape(NB, 1, bm)
    tok_index = lambda j, i, be, nb: (jnp.minimum(i + j, NB - 1), 0, 0)
    wspec = lambda r, c: pl.BlockSpec((1, r, c), lambda i, be, nb: (be[i], 0, 0))
    vmem_limit = 2 * 3 * D * F * 4 + 3 * D * F * 2 + (2 + FFN_SLOTS) * bm * D * 4 + 6 * bm * F * 4
    return pl.pallas_call(
        _ffn_kernel,
        grid_spec=pltpu.PrefetchScalarGridSpec(
            num_scalar_prefetch=2,
            grid=(NB,),
            in_specs=[pl.BlockSpec((1, 1, bm), functools.partial(tok_index, j), memory_space=pltpu.SMEM)
                      for j in range(FFN_SLOTS)] + [
                      pl.BlockSpec(memory_space=pl.ANY),
                      pl.BlockSpec((bm, 1), lambda i, be, nb: (i, 0)),
                      wspec(D, F), wspec(1, F), wspec(D, F), wspec(1, F), wspec(F, D), wspec(1, D)],
            out_specs=pl.BlockSpec((bm * ROW_TILES, LANES), lambda i, be, nb: (i, 0)),
            scratch_shapes=[pltpu.VMEM((FFN_SLOTS, bm * ROW_TILES, LANES), F32),
                            pltpu.VMEM((D, F), BF16), pltpu.VMEM((D, F), BF16), pltpu.VMEM((F, D), BF16),
                            pltpu.SemaphoreType.DMA((FFN_SLOTS,))]),
        out_shape=jax.ShapeDtypeStruct((NB * bm * ROW_TILES, LANES), F32),
        compiler_params=pltpu.CompilerParams(dimension_semantics=("arbitrary",),
                                             vmem_limit_bytes=vmem_limit),
        name="ffn",
    )(block_e, nb_used, *([tok3] * FFN_SLOTS), h2_rows, buf_gate, wg, bg, wu, bu, wd, bd)


def _combine_kernel(*refs):
    idx_refs = refs[:COMB_SLOTS]
    y_hbm, x1_ref, gt2_ref, fg_ref, o_ref, buf, sem = refs[COMB_SLOTS:]
    tm = x1_ref.shape[0]
    n = TOP_K * tm
    i = pl.program_id(0)
    nt = pl.num_programs(0)
    slot = i % COMB_SLOTS

    ahead = COMB_SLOTS - 1
    for b in range(ahead):
        @pl.when((i == 0) & (b < nt))
        def _(b=b):
            _issue_rows(idx_refs[b], range(n), y_hbm, buf, b, sem)

    @pl.when(i + ahead < nt)
    def _():
        _issue_rows(idx_refs[ahead], range(n), y_hbm, buf, (i + ahead) % COMB_SLOTS, sem)

    _wait_rows(buf, slot, sem)
    y = _rows_2d(buf, slot, 0, tm)
    for k in range(1, TOP_K):
        y = y + _rows_2d(buf, slot, k * tm, tm)
    x2 = x1_ref[...] + gt2_ref[0] * y
    o_ref[...] = _rms(x2, fg_ref[...])


def _combine(dest3, y_rows, x1, gt2, fg, S):
    T, D = x1.shape
    tm = min(COMB_TM, S)
    NT = T // tm
    per_b = S // tm
    n = TOP_K * tm
    idx_index = lambda j, i: (jnp.minimum(i + j, NT - 1), 0, 0)
    return pl.pallas_call(
        _combine_kernel,
        grid=(NT,),
        in_specs=[pl.BlockSpec((1, 1, n), functools.partial(idx_index, j), memory_space=pltpu.SMEM)
                  for j in range(COMB_SLOTS)] + [
                  pl.BlockSpec(memory_space=pl.ANY),
                  pl.BlockSpec((tm, D), lambda i: (i, 0)),
                  pl.BlockSpec((1, 1, D), lambda i: (i // per_b, 0, 0)),
                  pl.BlockSpec((1, D), lambda i: (0, 0))],
        out_specs=pl.BlockSpec((tm, D), lambda i: (i, 0)),
        out_shape=jax.ShapeDtypeStruct((T, D), F32),
        scratch_shapes=[pltpu.VMEM((COMB_SLOTS, n * ROW_TILES, LANES), F32), pltpu.SemaphoreType.DMA((COMB_SLOTS,))],
        compiler_params=pltpu.CompilerParams(dimension_semantics=("arbitrary",)),
        name="combine",
    )(*([dest3] * COMB_SLOTS), y_rows, x1, gt2, fg)


def _rope_tables(S):
    inv = ROPE_THETA ** (-jnp.arange(0, ROT_DIM, 2, dtype=F32) / ROT_DIM)
    ang = jnp.arange(S, dtype=F32)[:, None] * inv[None, :]
    cos, sin = jnp.cos(ang), jnp.sin(ang)
    d = jnp.arange(KV_W) % HEAD_DIM
    first, second = d < ROT_HALF, (d >= ROT_HALF) & (d < ROT_DIM)
    cos_l = cos[:, d % ROT_HALF]
    sin_l = sin[:, d % ROT_HALF]
    rc = jnp.where((d < ROT_DIM)[None], cos_l, 1.0)
    rs1 = jnp.where(second[None], sin_l, 0.0)
    rs2 = jnp.where(first[None], -sin_l, 0.0)
    return rc, rs1, rs2, cos.T, sin.T


def _route_plan(route, counts, T):
    bm = FFN_BM
    A = T * TOP_K
    idx = route[0:TOP_K].astype(I32)
    gate = route[TOP_K:2 * TOP_K]
    rank = route[2 * TOP_K:3 * TOP_K].astype(I32)
    keys = idx * T + jnp.arange(T, dtype=I32)[None, :]
    skey, sgate = lax.sort((keys.reshape(A), gate.reshape(A)), num_keys=1)
    counts = counts.astype(I32)
    starts = jnp.cumsum(counts) - counts
    padded = (counts + bm - 1) // bm * bm
    pends = jnp.cumsum(padded)
    pstarts = pends - padded
    P = (A + N_EXPERTS * bm + bm - 1) // bm * bm
    NB = P // bm
    blk0 = jnp.arange(NB, dtype=I32) * bm
    block_e = jnp.minimum(jnp.sum((pends[None, :] <= blk0[:, None]).astype(I32), axis=1), N_EXPERTS - 1)
    r = (blk0 - pstarts[block_e])[:, None] + jnp.arange(bm, dtype=I32)[None, :]
    valid = r < counts[block_e][:, None]
    src = jnp.clip(starts[block_e][:, None] + r, 0, A - 1)
    buf_tok = jnp.where(valid, skey[src] - block_e[:, None] * T, 0)
    buf_gate = jnp.where(valid, sgate[src], 0.0).reshape(P, 1)
    dest = rank
    for e in range(N_EXPERTS):
        dest = dest + jnp.where(idx == e, pstarts[e], 0)
    nb_used = (pends[-1] // bm).astype(I32).reshape(1)
    return block_e, nb_used, buf_tok, buf_gate, dest


def kernel(x, c, norm1_g, norm2_g, w_ada, b_ada, w_in, conv_w, conv_b, conv_ln_g, conv_ln_b, cmp_pe_k, cmp_pe_v,
           cmp_k_w1, cmp_k_w2, cmp_v_w1, cmp_v_w2, out_norm_conv, out_norm_nsa, w_out, w_router, b_router,
           w_gate, b_gate, w_up, b_up, w_down, b_down, final_norm_g):
    B, S, D = x.shape
    T = B * S
    G = KV_GROUPS
    assert D == D_MODEL and S % ATTN_TK == 0 and S % CMP_STRIDE == 0 and KV_W == LANES
    rc, rs1, rs2, cos_t, sin_t = _rope_tables(S)
    n_sel = S // SEL_BLK
    nc = S // CMP_STRIDE
    cstart = jnp.arange(nc) * CMP_STRIDE
    jstart = jnp.arange(n_sel) * SEL_BLK
    overlap_t = ((cstart[None, :] <= jstart[:, None] + SEL_BLK - 1)
                 & (cstart[None, :] + CMP_LEN - 1 >= jstart[:, None])
                 & (jnp.arange(nc)[None, :] < nc - 1)).astype(F32)

    assert w_ada.shape[0] == 1
    for l in range(1):
        mod = _adaln(c, w_ada[l], b_ada[l][None])
        sh1, sc1, gt1, sh2, sc2, gt2 = [m[:, None, :] for m in jnp.split(mod, 6, axis=-1)]

        wl = w_in[l]
        o = 2 * CONV_CH + NSA_WIDTH
        kvc = [wl[:, o + i * KV_W:o + (i + 1) * KV_W] for i in range(6)]
        gl = wl[:, o + 6 * KV_W:]
        per_g = 3 * Q_PER_G
        gpad = [jnp.pad(gl[:, per_g * g:per_g * (g + 1)], ((0, 0), (0, GATE_ROWS - per_g))) for g in range(G)]
        wn = jnp.concatenate([wl[:, :2 * CONV_CH], kvc[0], kvc[2], kvc[4], kvc[1]], axis=1).astype(BF16)
        wt = jnp.concatenate([wl[:, 2 * CONV_CH:o], kvc[3], kvc[5]] + gpad, axis=1).T.astype(BF16)
        u, kc, vc, ks, kw, qt, vst, vwt, gates = _inproj(x, sc1, sh1, norm1_g[l][None], wn, wt,
                                                         rc, rs1, rs2, cos_t, sin_t)

        conv_n = _conv(u, conv_w[l], conv_b[l][None], conv_ln_g[l][None], conv_ln_b[l][None],
                       out_norm_conv[l][None])

        kcmp, vcmp_t = _compress(kc, vc, cmp_pe_k[l].reshape(1, -1), cmp_pe_v[l].reshape(1, -1),
                                 cmp_k_w1[l], cmp_k_w2[l], cmp_v_w1[l], cmp_v_w2[l].T)
        nsa = _attention(qt, kcmp, vcmp_t, ks, vst, kw, vwt, gates, overlap_t)

        x1, h2_rows, route, counts = _outproj(x, conv_n, nsa, out_norm_nsa[l][None], w_out[l].astype(BF16), gt1,
                                              norm2_g[l][None], sc2, sh2, w_router[l], b_router[l][None])

        block_e, nb_used, buf_tok, buf_gate, dest = _route_plan(route, counts[:, 0], T)
        y_rows = _ffn(block_e, nb_used, buf_tok, h2_rows.reshape(T, ROW_TILES, LANES), buf_gate,
                      w_gate[l], b_gate[l][:, None, :], w_up[l], b_up[l][:, None, :],
                      w_down[l], b_down[l][:, None, :])
        tm = min(COMB_TM, S)
        dest3 = dest.reshape(TOP_K, T // tm, tm).transpose(1, 0, 2).reshape(T // tm, 1, TOP_K * tm)
        P = y_rows.shape[0] // ROW_TILES
        x = _combine(dest3, y_rows.reshape(P, ROW_TILES, LANES), x1.reshape(T, D), gt2, final_norm_g[None],
                     S).reshape(B, S, D)
    return x
```
